```python
import math
import jax, jax.numpy as jnp
from jax import lax
import numpy as np

D_MODEL = 1024
BATCH = 4
SEQ = 8192
DEPTH = 2

GRID_W = 64
CTX_LEN = 256
N_MIXERS = 2
N_MOD = 6
NORM_EPS = 1e-6
POS_BASE = 10000.0
RNN_WIDTH = 1280
RNN_HEADS = 5
RNN_BLOCK = RNN_WIDTH // RNN_HEADS
CONV_WIDTH = 4
CONV_PAD_LEFT = 2
CONV_PAD_RIGHT = CONV_WIDTH - 1 - CONV_PAD_LEFT
LRU_C = 8.0
LAMBDA_MIN = 0.9
LAMBDA_MAX = 0.999
SGU_WIDTH = 2048
SGU_HEADS = 8
SGU_GROUP = SGU_WIDTH // SGU_HEADS
CHUNK = 128
N_EXPERTS = 64
TOP_K = 8
N_GROUPS = 8
TOPK_GROUPS = 4
EXPERTS_PER_GROUP = N_EXPERTS // N_GROUPS
EXPERT_FF = 256
SHARED_FF = 256
ROUTED_SCALE = 2.5
MOE_BLOCK = 128

kernel_name = 'hybrid_rglru_sgu_moe_dit'


def rmsnorm(x, g):
    x32 = x.astype(jnp.float32)
    y = x32 * lax.rsqrt(jnp.mean(x32 * x32, axis=-1, keepdims=True) + NORM_EPS)
    return (y * g.astype(jnp.float32)).astype(x.dtype)


def layernorm(x, g, b):
    x32 = x.astype(jnp.float32)
    mu = jnp.mean(x32, axis=-1, keepdims=True)
    xc = x32 - mu
    y = xc * lax.rsqrt(jnp.mean(xc * xc, axis=-1, keepdims=True) + NORM_EPS)
    return (y * g.astype(jnp.float32) + b.astype(jnp.float32)).astype(x.dtype)


def sincos_2d(rows, d):
    quarter = d // 4
    omega = 1.0 / (POS_BASE ** (jnp.arange(quarter, dtype=jnp.float32) / quarter))
    def emb(n):
        p = jnp.arange(n, dtype=jnp.float32)[:, None] * omega[None, :]
        return jnp.concatenate([jnp.sin(p), jnp.cos(p)], axis=-1)
    er, ec = emb(rows), emb(GRID_W)
    pe = jnp.concatenate([jnp.broadcast_to(er[:, None, :], (rows, GRID_W, d // 2)),
                          jnp.broadcast_to(ec[None, :, :], (rows, GRID_W, d // 2))], axis=-1)
    return pe.reshape(rows * GRID_W, d)


def dwconv_centred(u, w, b):
    t = u.shape[1]
    up = jnp.pad(u, ((0, 0), (CONV_PAD_LEFT, CONV_PAD_RIGHT), (0, 0)))
    return sum(up[:, k:k + t] * w[k] for k in range(CONV_WIDTH)) + b


def lru_coeffs(u, r_w, r_b, i_w, i_b, lam, reset_first):
    u = u.astype(jnp.float32)
    ub = u.reshape(u.shape[:2] + (RNN_HEADS, RNN_BLOCK))
    r = jax.nn.sigmoid(jnp.einsum('bthi,hij->bthj', ub, r_w.astype(jnp.float32)).reshape(u.shape) + r_b.astype(jnp.float32))
    ig = jax.nn.sigmoid(jnp.einsum('bthi,hij->bthj', ub, i_w.astype(jnp.float32)).reshape(u.shape) + i_b.astype(jnp.float32))
    log_a = LRU_C * r * jax.nn.log_sigmoid(lam.astype(jnp.float32))
    a = jnp.exp(log_a)
    mult = jnp.sqrt(-jnp.expm1(2.0 * log_a))
    if reset_first:
        mult = mult.at[:, 0].set(1.0)
    return a, mult * ig * u


def linear_scan(a, b):
    def combine(l, r):
        return l[0] * r[0], r[0] * l[1] + r[1]
    _, h = lax.associative_scan(combine, (a, b), axis=1)
    return h


def rglru_mixer(hx, hc, w_in, conv_w, conv_b, r_w, r_b, i_w, i_b, lam, w_out, ctx_out):
    gate_x, ux = jnp.split(hx @ w_in, 2, axis=-1)
    ux = dwconv_centred(ux, conv_w, conv_b)
    uc = dwconv_centred(hc @ w_in[:, RNN_WIDTH:], conv_w, conv_b)
    yx = 0.0
    yc = 0.0
    for d in range(2):
        flip = (lambda t: jnp.flip(t, axis=1)) if d == 1 else (lambda t: t)
        a_c, b_c = lru_coeffs(flip(uc), r_w[d], r_b[d], i_w[d], i_b[d], lam[d], True)
        h_c = linear_scan(a_c, b_c)
        a_x, b_x = lru_coeffs(flip(ux), r_w[d], r_b[d], i_w[d], i_b[d], lam[d], False)
        b_x = b_x.at[:, 0].add(a_x[:, 0] * h_c[:, -1])
        yx = yx + flip(linear_scan(a_x, b_x))
        if ctx_out:
            yc = yc + flip(h_c)
    out_x = (jax.nn.gelu(gate_x) * yx.astype(hx.dtype)) @ w_out
    if not ctx_out:
        return out_x, None
    gate_c = hc @ w_in[:, :RNN_WIDTH]
    out_c = (jax.nn.gelu(gate_c) * yc.astype(hc.dtype)) @ w_out
    return out_x, out_c


def sgu_mixer(h, w_in, ln_g, ln_b, w_s, b_s, w_out):
    bsz, t, _ = h.shape
    u, v = jnp.split(jax.nn.gelu(h @ w_in), 2, axis=-1)
    v = layernorm(v, ln_g, ln_b)
    vc = v.reshape(bsz, t // CHUNK, CHUNK, SGU_HEADS, SGU_GROUP)
    sv = jnp.einsum('gpq,bnqgd->bnpgd', w_s, vc) + b_s.T[:, :, None]
    return (u * sv.reshape(bsz, t, SGU_WIDTH)) @ w_out


def swiglu(x, w_gu, w_down):
    g, u = jnp.split(x @ w_gu, 2, axis=-1)
    return (jax.nn.silu(g) * u) @ w_down


def moe_ffn(xn, router_w, router_b, w_gu, w_down, ws_gu, ws_down):
    n_tok, d = xn.shape
    scores = jax.nn.sigmoid(jnp.dot(xn.astype(jnp.float32), router_w.astype(jnp.float32)))
    sel = scores + router_b.astype(jnp.float32)
    grp_score = lax.top_k(sel.reshape(n_tok, N_GROUPS, EXPERTS_PER_GROUP), 2)[0].sum(-1)
    _, top_grp = lax.top_k(grp_score, TOPK_GROUPS)
    grp_keep = jnp.any(top_grp[:, :, None] == jnp.arange(N_GROUPS)[None, None, :], axis=1)
    keep = jnp.repeat(grp_keep, EXPERTS_PER_GROUP, axis=1)
    _, eidx = lax.top_k(jnp.where(keep, sel, -jnp.inf), TOP_K)
    gate = jnp.take_along_axis(scores, eidx, axis=1)
    gate = (gate / jnp.sum(gate, axis=-1, keepdims=True) * ROUTED_SCALE).astype(xn.dtype)
    n_assign = n_tok * TOP_K
    n_blocks = (n_assign + N_EXPERTS * (MOE_BLOCK - 1) + MOE_BLOCK - 1) // MOE_BLOCK
    n_pad = n_blocks * MOE_BLOCK
    flat_e = eidx.reshape(-1)
    flat_tok = jnp.repeat(jnp.arange(n_tok, dtype=jnp.int32), TOP_K)
    order = jnp.argsort(flat_e)
    se, stok, sw = flat_e[order], flat_tok[order], gate.reshape(-1)[order]
    counts = jax.ops.segment_sum(jnp.ones_like(flat_e), flat_e, num_segments=N_EXPERTS)
    padded = (counts + MOE_BLOCK - 1) // MOE_BLOCK * MOE_BLOCK
    pad_end = jnp.cumsum(padded)
    pad_start = pad_end - padded
    start = jnp.cumsum(counts) - counts
    dest = pad_start[se] + jnp.arange(n_assign, dtype=se.dtype) - start[se]
    buf_tok = jnp.full((n_pad,), n_tok, jnp.int32).at[dest].set(stok)
    buf_w = jnp.zeros((n_pad,), xn.dtype).at[dest].set(sw)
    block_e = jnp.minimum(jnp.searchsorted(pad_end, jnp.arange(n_blocks, dtype=pad_end.dtype) * MOE_BLOCK, side='right'), N_EXPERTS - 1)
    x_pad = jnp.concatenate([xn, jnp.zeros((1, d), xn.dtype)], axis=0)
    def expert_block(args):
        tok, w, e = args
        return swiglu(x_pad[tok], w_gu[e], w_down[e]) * w[:, None]
    y = lax.map(expert_block, (buf_tok.reshape(n_blocks, MOE_BLOCK), buf_w.reshape(n_blocks, MOE_BLOCK), block_e))
    routed = jax.ops.segment_sum(y.reshape(n_pad, d), buf_tok, num_segments=n_tok + 1)[:n_tok]
    return routed + swiglu(xn, ws_gu, ws_down)


def setup_inputs(seed: int = 0) -> dict:
    key = jax.random.key(seed)
    ks = iter(jax.random.split(key, 40))
    def nrm(shape, scale):
        return jax.random.normal(next(ks), shape, jnp.float32) * scale
    d = D_MODEL
    n_a = (DEPTH + N_MIXERS - 1) // N_MIXERS
    n_b = DEPTH // N_MIXERS
    lam_u = jax.random.uniform(next(ks), (n_a, 2, RNN_WIDTH), jnp.float32, LAMBDA_MIN, LAMBDA_MAX)
    return {
        'x': nrm((BATCH, SEQ, d), 1.0),
        'c': nrm((BATCH, d), 1.0),
        'ctx': nrm((BATCH, CTX_LEN, d), 1.0),
        'c_ctx': nrm((d,), 1.0),
        'ada_w': nrm((DEPTH, d, N_MOD * d), 0.5 * d ** -0.5),
        'ada_b': nrm((DEPTH, N_MOD * d), 0.02),
        'mix_norm_g': 1.0 + nrm((DEPTH, d), 0.02),
        'ffn_norm_g': 1.0 + nrm((DEPTH, d), 0.02),
        'a_w_in': nrm((n_a, d, 2 * RNN_WIDTH), d ** -0.5),
        'a_conv_w': nrm((n_a, CONV_WIDTH, RNN_WIDTH), CONV_WIDTH ** -0.5),
        'a_conv_b': nrm((n_a, RNN_WIDTH), 0.02),
        'a_gate_r_w': nrm((n_a, 2, RNN_HEADS, RNN_BLOCK, RNN_BLOCK), RNN_BLOCK ** -0.5),
        'a_gate_r_b': nrm((n_a, 2, RNN_WIDTH), 0.02),
        'a_gate_i_w': nrm((n_a, 2, RNN_HEADS, RNN_BLOCK, RNN_BLOCK), RNN_BLOCK ** -0.5),
        'a_gate_i_b': nrm((n_a, 2, RNN_WIDTH), 0.02),
        'a_lambda': jnp.log(lam_u) - jnp.log1p(-lam_u),
        'a_w_out': nrm((n_a, RNN_WIDTH, d), RNN_WIDTH ** -0.5),
        'b_w_in': nrm((n_b, d, 2 * SGU_WIDTH), d ** -0.5),
        'b_ln_g': 1.0 + nrm((n_b, SGU_WIDTH), 0.02),
        'b_ln_b': nrm((n_b, SGU_WIDTH), 0.02),
        'b_w_s': nrm((n_b, SGU_HEADS, CHUNK, CHUNK), CHUNK ** -0.5),
        'b_b_s': 1.0 + nrm((n_b, SGU_HEADS, CHUNK), 0.02),
        'b_w_out': nrm((n_b, SGU_WIDTH, d), SGU_WIDTH ** -0.5),
        'router_w': nrm((DEPTH, d, N_EXPERTS), d ** -0.5),
        'router_b': nrm((DEPTH, N_EXPERTS), 0.01),
        'moe_w_gu': nrm((DEPTH, N_EXPERTS, d, 2 * EXPERT_FF), d ** -0.5),
        'moe_w_down': nrm((DEPTH, N_EXPERTS, EXPERT_FF, d), EXPERT_FF ** -0.5),
        'shared_w_gu': nrm((DEPTH, d, 2 * SHARED_FF), d ** -0.5),
        'shared_w_down': nrm((DEPTH, SHARED_FF, d), SHARED_FF ** -0.5),
        'final_norm_g': 1.0 + nrm((d,), 0.02),
    }


def reference(x, c, ctx, c_ctx, ada_w, ada_b, mix_norm_g, ffn_norm_g,
              a_w_in, a_conv_w, a_conv_b, a_gate_r_w, a_gate_r_b, a_gate_i_w, a_gate_i_b, a_lambda, a_w_out,
              b_w_in, b_ln_g, b_ln_b, b_w_s, b_b_s, b_w_out,
              router_w, router_b, moe_w_gu, moe_w_down, shared_w_gu, shared_w_down, final_norm_g):
    bsz, s, d = x.shape
    rows = s // GRID_W
    x = x + sincos_2d(rows, d).astype(x.dtype)[None]
    cx = ctx
    silu_c = jax.nn.silu(c)
    silu_cc = jax.nn.silu(c_ctx)
    for i in range(DEPTH):
        kind, j = i % N_MIXERS, i // N_MIXERS
        ctx_live = any(l % N_MIXERS == 0 for l in range(i + 1, DEPTH))
        mod = silu_c @ ada_w[i] + ada_b[i]
        mod_c = silu_cc @ ada_w[i] + ada_b[i]
        sh1, sc1, g1, sh2, sc2, g2 = jnp.split(mod[:, None, :], N_MOD, axis=-1)
        csh1, csc1, cg1, csh2, csc2, cg2 = jnp.split(mod_c, N_MOD)
        hx = rmsnorm(x, mix_norm_g[i]) * (1.0 + sc1) + sh1
        if kind == 0:
            hc = rmsnorm(cx, mix_norm_g[i]) * (1.0 + csc1) + csh1
            yx, yc = rglru_mixer(hx, hc, a_w_in[j], a_conv_w[j], a_conv_b[j], a_gate_r_w[j], a_gate_r_b[j],
                                 a_gate_i_w[j], a_gate_i_b[j], a_lambda[j], a_w_out[j], ctx_live)
        else:
            yx = sgu_mixer(hx, b_w_in[j], b_ln_g[j], b_ln_b[j], b_w_s[j], b_b_s[j], b_w_out[j])
            if ctx_live:
                hc = rmsnorm(cx, mix_norm_g[i]) * (1.0 + csc1) + csh1
                yc = sgu_mixer(hc, b_w_in[j], b_ln_g[j], b_ln_b[j], b_w_s[j], b_b_s[j], b_w_out[j])
        x = x + g1 * yx
        if ctx_live:
            cx = cx + cg1 * yc
        hx = rmsnorm(x, ffn_norm_g[i]) * (1.0 + sc2) + sh2
        if ctx_live:
            hc = rmsnorm(cx, ffn_norm_g[i]) * (1.0 + csc2) + csh2
            n_c = hc.shape[0] * hc.shape[1]
            out = moe_ffn(jnp.concatenate([hc.reshape(-1, d), hx.reshape(-1, d)], axis=0), router_w[i], router_b[i],
                          moe_w_gu[i], moe_w_down[i], shared_w_gu[i], shared_w_down[i])
            cx = cx + cg2 * out[:n_c].reshape(cx.shape)
            x = x + g2 * out[n_c:].reshape(x.shape)
        else:
            out = moe_ffn(hx.reshape(-1, d), router_w[i], router_b[i], moe_w_gu[i], moe_w_down[i],
                          shared_w_gu[i], shared_w_down[i])
            x = x + g2 * out.reshape(x.shape)
    return rmsnorm(x, final_norm_g)
```

```python
import functools

import jax
import jax.numpy as jnp
from jax import lax
from jax.experimental import pallas as pl
from jax.experimental.pallas import tpu as pltpu

F32 = jnp.float32
BF16 = jnp.bfloat16
HIGHEST = lax.Precision.HIGHEST

GRID_W = 64
N_MOD = 6
NORM_EPS = 1e-6
POS_BASE = 10000.0
RNN_HEADS = 5
CONV_WIDTH = 4
CONV_PAD_LEFT = 2
LRU_C = 8.0
SGU_HEADS = 8
CHUNK = 128
N_EXPERTS = 64
TOP_K = 8
N_GROUPS = 8
TOPK_GROUPS = 4
EXPERTS_PER_GROUP = N_EXPERTS // N_GROUPS
ROUTED_SCALE = 2.5

SUBLANES = 8
MOD_ROWS = 8
TOKEN_TILE = 256
EXPERT_BLOCK = 256
VMEM_LIMIT = 56 * 1024 * 1024


def _params(semantics, vmem=VMEM_LIMIT):
    return pltpu.CompilerParams(dimension_semantics=semantics, vmem_limit_bytes=vmem)


def _silu(x):
    return x * jax.nn.sigmoid(x)


def _rms_mod(x, g, sc, sh):
    y = x * lax.rsqrt(jnp.mean(x * x, axis=-1, keepdims=True) + NORM_EPS)
    return (y * g) * (1.0 + sc) + sh


def _mod_chunk(mod_ref, row, k, d):
    return mod_ref[pl.ds(row, 1), k * d:(k + 1) * d]


def _mod_kernel(cc_ref, w_ref, b_ref, o_ref):
    s = _silu(cc_ref[...])
    o_ref[0] = jnp.dot(s, w_ref[0], preferred_element_type=F32, precision=HIGHEST) + b_ref[0]


def _modulation(cc, ada_w, ada_b):
    depth, d, nd = ada_w.shape
    return pl.pallas_call(
        _mod_kernel,
        grid=(depth, nd // d),
        in_specs=[pl.BlockSpec((MOD_ROWS, d), lambda l, j: (0, 0)),
                  pl.BlockSpec((1, d, d), lambda l, j: (l, 0, j)),
                  pl.BlockSpec((1, 1, d), lambda l, j: (l, 0, j))],
        out_specs=pl.BlockSpec((1, MOD_ROWS, d), lambda l, j: (l, 0, j)),
        out_shape=jax.ShapeDtypeStruct((depth, MOD_ROWS, nd), F32),
        compiler_params=_params(("arbitrary", "arbitrary")),
        name="modulation",
    )(cc, ada_w, ada_b.reshape(depth, 1, nd))


def _rglru_in_kernel(x_ref, pe_ref, mod_ref, g_ref, w_ref, x0_ref, gate_ref, u_ref, *, tiles_per_seq):
    d = x_ref.shape[1]
    c = u_ref.shape[1]
    row = pl.program_id(0) // tiles_per_seq
    x = x_ref[...] + pe_ref[...]
    hx = _rms_mod(x, g_ref[...], _mod_chunk(mod_ref, row, 1, d), _mod_chunk(mod_ref, row, 0, d))
    z = jnp.dot(hx.astype(BF16), w_ref[...], preferred_element_type=F32)
    x0_ref[...] = x
    gate_ref[...] = jax.nn.gelu(z[:, :c]).astype(BF16)
    u_ref[...] = z[:, c:]


def _rglru_in(x2, pe, mod, g, w_in):
    n, d = x2.shape
    c = w_in.shape[1] // 2
    s = pe.shape[0]
    tps = s // TOKEN_TILE
    row = lambda i: (i, 0)
    return pl.pallas_call(
        functools.partial(_rglru_in_kernel, tiles_per_seq=tps),
        grid=(n // TOKEN_TILE,),
        in_specs=[pl.BlockSpec((TOKEN_TILE, d), row),
                  pl.BlockSpec((TOKEN_TILE, d), lambda i: (i % tps, 0)),
                  pl.BlockSpec(mod.shape, lambda i: (0, 0)),
                  pl.BlockSpec((1, d), lambda i: (0, 0)),
                  pl.BlockSpec(w_in.shape, lambda i: (0, 0))],
        out_specs=[pl.BlockSpec((TOKEN_TILE, d), row),
                   pl.BlockSpec((TOKEN_TILE, c), row),
                   pl.BlockSpec((TOKEN_TILE, c), row)],
        out_shape=[jax.ShapeDtypeStruct((n, d), F32),
                   jax.ShapeDtypeStruct((n, c), BF16),
                   jax.ShapeDtypeStruct((n, c), F32)],
        compiler_params=_params(("arbitrary",)),
        name="rglru_in",
    )(x2, pe, mod, g, w_in)


def _ctx_in_kernel(x_ref, mod_ref, g_ref, w_ref, u_ref, *, ctx_row):
    d = x_ref.shape[1]
    hx = _rms_mod(x_ref[...], g_ref[...], _mod_chunk(mod_ref, ctx_row, 1, d),
                  _mod_chunk(mod_ref, ctx_row, 0, d))
    u_ref[...] = jnp.dot(hx.astype(BF16), w_ref[...], preferred_element_type=F32)


def _ctx_in(c2, mod, g, w_u, ctx_row):
    n, d = c2.shape
    c = w_u.shape[1]
    return pl.pallas_call(
        functools.partial(_ctx_in_kernel, ctx_row=ctx_row),
        grid=(n // TOKEN_TILE,),
        in_specs=[pl.BlockSpec((TOKEN_TILE, d), lambda i: (i, 0)),
                  pl.BlockSpec(mod.shape, lambda i: (0, 0)),
                  pl.BlockSpec((1, d), lambda i: (0, 0)),
                  pl.BlockSpec(w_u.shape, lambda i: (0, 0))],
        out_specs=pl.BlockSpec((TOKEN_TILE, c), lambda i: (i, 0)),
        out_shape=jax.ShapeDtypeStruct((n, c), F32),
        compiler_params=_params(("arbitrary",)),
        name="ctx_in",
    )(c2, mod, g, w_u)


def _log_sigmoid(x):
    return jnp.minimum(x, 0.0) - jnp.log1p(jnp.exp(-jnp.abs(x)))


def _lru_scan_kernel(u_ref, up_ref, un_ref, cw_ref, cb_ref, wri_ref, rb_ref, ib_ref, lam_ref, h0_ref,
                     out_ref, ubuf, a_scr, b_scr, h_scr, *, n_batch, n_tiles, reset_first, emit_y):
    t_rows, c = u_ref.shape
    hb = c // RNN_HEADS
    d = pl.program_id(0)
    b = pl.program_id(1)
    j = pl.program_id(2)
    jj = j + d * (n_tiles - 1 - 2 * j)

    ubuf[SUBLANES:SUBLANES + t_rows, :] = u_ref[...]
    ubuf[0:SUBLANES, :] = jnp.where(jj == 0, 0.0, up_ref[...])
    ubuf[SUBLANES + t_rows:, :] = jnp.where(jj == n_tiles - 1, 0.0, un_ref[...])
    u = cb_ref[...]
    for k in range(CONV_WIDTH):
        u = u + cw_ref[k:k + 1, :] * ubuf[pl.ds(SUBLANES - CONV_PAD_LEFT + k, t_rows), :]

    log_lam = LRU_C * _log_sigmoid(lam_ref[0])
    rows = lax.broadcasted_iota(jnp.int32, (t_rows, 1), 0)
    first_row = jnp.where(j == 0, jnp.where(d == 0, 0, t_rows - 1), -1)
    for h in range(RNN_HEADS):
        sl = slice(h * hb, (h + 1) * hb)
        uh = u[:, sl]
        z = jnp.dot(uh.astype(BF16), wri_ref[0, h], preferred_element_type=F32)
        r = jax.nn.sigmoid(z[:, :hb] + rb_ref[0][:, sl])
        ig = jax.nn.sigmoid(z[:, hb:] + ib_ref[0][:, sl])
        log_a = r * log_lam[:, sl]
        a = jnp.exp(log_a)
        mult = jnp.sqrt((1.0 - a) * (1.0 + a))
        if reset_first:
            mult = jnp.where(rows == first_row, 1.0, mult)
        a_scr[:, sl] = a
        b_scr[:, sl] = mult * ig * uh

    @pl.when(j == 0)
    def _():
        h_scr[...] = h0_ref[pl.ds(d * n_batch + b, 1), :]

    def step(t, h):
        tt = t + d * (t_rows - 1 - 2 * t)
        h = a_scr[pl.ds(tt, 1), :] * h + b_scr[pl.ds(tt, 1), :]
        if emit_y:
            out_ref[0, pl.ds(tt, 1), :] = h
        return h

    h = lax.fori_loop(0, t_rows, step, h_scr[...], unroll=8)
    h_scr[...] = h
    if not emit_y:
        @pl.when(j == n_tiles - 1)
        def _():
            out_ref[pl.ds(d * n_batch + b, 1), :] = h


def _lru_scan(u, conv_w, conv_b, w_ri, r_b, i_b, lam, h0, *, n_batch, reset_first, emit_y):
    n, c = u.shape
    n_tiles = n // n_batch // TOKEN_TILE
    sub = TOKEN_TILE // SUBLANES
    n_sub = n // SUBLANES

    def tile(d, b, j):
        return b * n_tiles + j + d * (n_tiles - 1 - 2 * j)

    in_specs = [pl.BlockSpec((TOKEN_TILE, c), lambda d, b, j: (tile(d, b, j), 0)),
                pl.BlockSpec((SUBLANES, c), lambda d, b, j: (jnp.maximum(tile(d, b, j) * sub - 1, 0), 0)),
                pl.BlockSpec((SUBLANES, c), lambda d, b, j: (jnp.minimum((tile(d, b, j) + 1) * sub, n_sub - 1), 0)),
                pl.BlockSpec(conv_w.shape, lambda d, b, j: (0, 0)),
                pl.BlockSpec((1, c), lambda d, b, j: (0, 0)),
                pl.BlockSpec((1,) + w_ri.shape[1:], lambda d, b, j: (d, 0, 0, 0)),
                pl.BlockSpec((1, 1, c), lambda d, b, j: (d, 0, 0)),
                pl.BlockSpec((1, 1, c), lambda d, b, j: (d, 0, 0)),
                pl.BlockSpec((1, 1, c), lambda d, b, j: (d, 0, 0)),
                pl.BlockSpec(h0.shape, lambda d, b, j: (0, 0))]
    if emit_y:
        out_specs = pl.BlockSpec((1, TOKEN_TILE, c), lambda d, b, j: (d, tile(d, b, j), 0))
        out_shape = jax.ShapeDtypeStruct((2, n, c), F32)
    else:
        out_specs = pl.BlockSpec(h0.shape, lambda d, b, j: (0, 0))
        out_shape = jax.ShapeDtypeStruct(h0.shape, F32)
    return pl.pallas_call(
        functools.partial(_lru_scan_kernel, n_batch=n_batch, n_tiles=n_tiles,
                          reset_first=reset_first, emit_y=emit_y),
        grid=(2, n_batch, n_tiles),
        in_specs=in_specs,
        out_specs=out_specs,
        out_shape=out_shape,
        scratch_shapes=[pltpu.VMEM((TOKEN_TILE + 2 * SUBLANES, c), F32),
                        pltpu.VMEM((TOKEN_TILE, c), F32),
                        pltpu.VMEM((TOKEN_TILE, c), F32),
                        pltpu.VMEM((1, c), F32)],
        compiler_params=_params(("arbitrary", "arbitrary", "arbitrary")),
        name="lru_scan" if emit_y else "lru_scan_ctx",
    )(u, u, u, conv_w, conv_b.reshape(1, c), w_ri, r_b.reshape(2, 1, c), i_b.reshape(2, 1, c),
      lam.reshape(2, 1, c), h0)


def _route(logits, rb):
    e, t = logits.shape
    neg = -jnp.inf
    scores = jax.nn.sigmoid(logits)
    sel = scores + rb
    iota_g = lax.broadcasted_iota(jnp.int32, (N_GROUPS, t), 0).astype(F32)
    iota_e = lax.broadcasted_iota(jnp.int32, (e, t), 0).astype(F32)

    gs = jnp.full((N_GROUPS, t), neg, F32)
    for g in range(N_GROUPS):
        sg = sel[g * EXPERTS_PER_GROUP:(g + 1) * EXPERTS_PER_GROUP, :]
        m1 = jnp.max(sg, axis=0, keepdims=True)
        i1 = jnp.min(jnp.where(sg == m1, iota_g, float(EXPERTS_PER_GROUP)), axis=0, keepdims=True)
        m2 = jnp.max(jnp.where(iota_g == i1, neg, sg), axis=0, keepdims=True)
        gs = jnp.where(iota_g == float(g), m1 + m2, gs)

    keep = jnp.zeros((N_GROUPS, t), F32)
    for _ in range(TOPK_GROUPS):
        m = jnp.max(gs, axis=0, keepdims=True)
        idx = jnp.min(jnp.where(gs == m, iota_g, float(N_GROUPS)), axis=0, keepdims=True)
        hit = iota_g == idx
        keep = jnp.where(hit, 1.0, keep)
        gs = jnp.where(hit, neg, gs)

    masked = jnp.concatenate(
        [jnp.where(keep[g:g + 1, :] > 0.0, sel[g * EXPERTS_PER_GROUP:(g + 1) * EXPERTS_PER_GROUP, :], neg)
         for g in range(N_GROUPS)], axis=0)

    iota_k = lax.broadcasted_iota(jnp.int32, (TOP_K, t), 0)
    selmask = jnp.zeros((e, t), F32)
    eidx = jnp.zeros((TOP_K, t), F32)
    gw = jnp.zeros((TOP_K, t), F32)
    for k in range(TOP_K):
        m = jnp.max(masked, axis=0, keepdims=True)
        idx = jnp.min(jnp.where(masked == m, iota_e, float(e)), axis=0, keepdims=True)
        hit = iota_e == idx
        gk = jnp.sum(jnp.where(hit, scores, 0.0), axis=0, keepdims=True)
        masked = jnp.where(hit, neg, masked)
        selmask = jnp.where(hit, 1.0, selmask)
        eidx = jnp.where(iota_k == k, idx, eidx)
        gw = jnp.where(iota_k == k, gk, gw)
    gw = gw / jnp.sum(gw, axis=0, keepdims=True) * ROUTED_SCALE
    return eidx, gw, selmask


def _ffn_pre(x1, mod_ref, row, g2_ref, rwt_ref, rb_ref, x1_ref, hx2_ref, eidx_ref, gw_ref, mask_ref, cnt_ref):
    d = x1.shape[1]
    hx2 = _rms_mod(x1, g2_ref[...], _mod_chunk(mod_ref, row, 4, d), _mod_chunk(mod_ref, row, 3, d))
    x1_ref[...] = x1
    hx2_ref[...] = hx2
    logits = lax.dot_general(rwt_ref[...], hx2, (((1,), (1,)), ((), ())),
                             preferred_element_type=F32, precision=HIGHEST)
    eidx, gw, selmask = _route(logits, rb_ref[...])
    eidx_ref[...] = eidx.astype(jnp.int32)
    gw_ref[...] = gw
    mask_ref[...] = selmask

    @pl.when(pl.program_id(0) == 0)
    def _():
        cnt_ref[...] = jnp.zeros_like(cnt_ref)

    cnt_ref[...] += jnp.broadcast_to(jnp.sum(selmask, axis=1, keepdims=True), cnt_ref.shape)


def _ffn_pre_specs(n, d):
    row = lambda i: (i, 0)
    col = lambda i: (0, i)
    out_specs = [pl.BlockSpec((TOKEN_TILE, d), row),
                 pl.BlockSpec((TOKEN_TILE, d), row),
                 pl.BlockSpec((TOP_K, TOKEN_TILE), col),
                 pl.BlockSpec((TOP_K, TOKEN_TILE), col),
                 pl.BlockSpec((N_EXPERTS, TOKEN_TILE), col),
                 pl.BlockSpec((N_EXPERTS, 128), lambda i: (0, 0))]
    out_shape = [jax.ShapeDtypeStruct((n, d), F32),
                 jax.ShapeDtypeStruct((n, d), F32),
                 jax.ShapeDtypeStruct((TOP_K, n), jnp.int32),
                 jax.ShapeDtypeStruct((TOP_K, n), F32),
                 jax.ShapeDtypeStruct((N_EXPERTS, n), F32),
                 jax.ShapeDtypeStruct((N_EXPERTS, 128), F32)]
    return out_specs, out_shape


def _rglru_out_kernel(y_ref, gate_ref, x0_ref, mod_ref, wout_ref, g2_ref, rwt_ref, rb_ref,
                      x1_ref, hx2_ref, eidx_ref, gw_ref, mask_ref, cnt_ref, *, tiles_per_seq):
    d = x0_ref.shape[1]
    row = pl.program_id(0) // tiles_per_seq
    yx = y_ref[0] + y_ref[1]
    v = gate_ref[...].astype(F32) * yx
    out = jnp.dot(v.astype(BF16), wout_ref[...], preferred_element_type=F32)
    x1 = x0_ref[...] + _mod_chunk(mod_ref, row, 2, d) * out
    _ffn_pre(x1, mod_ref, row, g2_ref, rwt_ref, rb_ref, x1_ref, hx2_ref, eidx_ref, gw_ref, mask_ref, cnt_ref)


def _rglru_out(y, gate, x0, mod, w_out, g2, rwt, rb, tiles_per_seq):
    n, d = x0.shape
    c = gate.shape[1]
    row = lambda i: (i, 0)
    const = lambda i: (0, 0)
    out_specs, out_shape = _ffn_pre_specs(n, d)
    return pl.pallas_call(
        functools.partial(_rglru_out_kernel, tiles_per_seq=tiles_per_seq),
        grid=(n // TOKEN_TILE,),
        in_specs=[pl.BlockSpec((2, TOKEN_TILE, c), lambda i: (0, i, 0)),
                  pl.BlockSpec((TOKEN_TILE, c), row),
                  pl.BlockSpec((TOKEN_TILE, d), row),
                  pl.BlockSpec(mod.shape, const),
                  pl.BlockSpec(w_out.shape, const),
                  pl.BlockSpec((1, d), const),
                  pl.BlockSpec(rwt.shape, const),
                  pl.BlockSpec(rb.shape, const)],
        out_specs=out_specs,
        out_shape=out_shape,
        compiler_params=_params(("arbitrary",)),
        name="rglru_out",
    )(y, gate, x0, mod, w_out, g2, rwt, rb)


def _sgu_kernel(x_ref, mod_ref, g_ref, win_ref, lng_ref, lnb_ref, ws_ref, bst_ref, wout_ref,
                g2_ref, rwt_ref, rb_ref,
                x1_ref, hx2_ref, eidx_ref, gw_ref, mask_ref, cnt_ref, m_scr, *, tiles_per_seq):
    t_rows, d = x_ref.shape
    w = wout_ref.shape[0]
    gd = w // SGU_HEADS
    row = pl.program_id(0) // tiles_per_seq
    x = x_ref[...]
    hx = _rms_mod(x, g_ref[...], _mod_chunk(mod_ref, row, 1, d), _mod_chunk(mod_ref, row, 0, d))
    z = jax.nn.gelu(jnp.dot(hx.astype(BF16), win_ref[...], preferred_element_type=F32))
    u = z[:, :w]
    v = z[:, w:]
    mu = jnp.mean(v, axis=-1, keepdims=True)
    vc = v - mu
    v = vc * lax.rsqrt(jnp.mean(vc * vc, axis=-1, keepdims=True) + NORM_EPS) * lng_ref[...] + lnb_ref[...]
    vb = v.astype(BF16)
    for ch in range(t_rows // CHUNK):
        rs = slice(ch * CHUNK, (ch + 1) * CHUNK)
        for g in range(SGU_HEADS):
            cs = slice(g * gd, (g + 1) * gd)
            sv = jnp.dot(ws_ref[g], vb[rs, cs], preferred_element_type=F32) + bst_ref[:, g:g + 1]
            m_scr[rs, cs] = (u[rs, cs] * sv).astype(BF16)
    out = jnp.dot(m_scr[...], wout_ref[...], preferred_element_type=F32)
    x1 = x + _mod_chunk(mod_ref, row, 2, d) * out
    _ffn_pre(x1, mod_ref, row, g2_ref, rwt_ref, rb_ref, x1_ref, hx2_ref, eidx_ref, gw_ref, mask_ref, cnt_ref)


def _sgu(x, mod, g, w_in, ln_g, ln_b, w_s, b_st, w_out, g2, rwt, rb, tiles_per_seq):
    n, d = x.shape
    w = w_out.shape[0]
    const = lambda i: (0, 0)
    out_specs, out_shape = _ffn_pre_specs(n, d)
    return pl.pallas_call(
        functools.partial(_sgu_kernel, tiles_per_seq=tiles_per_seq),
        grid=(n // TOKEN_TILE,),
        in_specs=[pl.BlockSpec((TOKEN_TILE, d), lambda i: (i, 0)),
                  pl.BlockSpec(mod.shape, const),
                  pl.BlockSpec((1, d), const),
                  pl.BlockSpec(w_in.shape, const),
                  pl.BlockSpec((1, w), const),
                  pl.BlockSpec((1, w), const),
                  pl.BlockSpec(w_s.shape, lambda i: (0, 0, 0)),
                  pl.BlockSpec(b_st.shape, const),
                  pl.BlockSpec(w_out.shape, const),
                  pl.BlockSpec((1, d), const),
                  pl.BlockSpec(rwt.shape, const),
                  pl.BlockSpec(rb.shape, const)],
        out_specs=out_specs,
        out_shape=out_shape,
        scratch_shapes=[pltpu.VMEM((TOKEN_TILE, w), BF16)],
        compiler_params=_params(("arbitrary",)),
        name="sgu",
    )(x, mod, g, w_in, ln_g, ln_b, w_s, b_st, w_out, g2, rwt, rb)


def _rank_kernel(mask_ref, eidx_ref, pstart_ref, dest_ref, carry):
    e, t = mask_ref.shape

    @pl.when(pl.program_id(0) == 0)
    def _():
        carry[...] = jnp.zeros_like(carry)

    m = mask_ref[...]
    r = lax.broadcasted_iota(jnp.int32, (t, t), 0)
    c = lax.broadcasted_iota(jnp.int32, (t, t), 1)
    upper = jnp.where(r <= c, 1.0, 0.0).astype(BF16)
    incl = jnp.dot(m.astype(BF16), upper, preferred_element_type=F32)
    pos = carry[...] + incl - m + pstart_ref[...]
    iota_e = lax.broadcasted_iota(jnp.int32, (e, t), 0)
    iota_k = lax.broadcasted_iota(jnp.int32, (TOP_K, t), 0)
    eidx = eidx_ref[...]
    dest = jnp.zeros((TOP_K, t), F32)
    for k in range(TOP_K):
        dk = jnp.sum(jnp.where(iota_e == eidx[k:k + 1, :], pos, 0.0), axis=0, keepdims=True)
        dest = jnp.where(iota_k == k, dk, dest)
    dest_ref[...] = dest.astype(jnp.int32)
    carry[...] = carry[...] + incl[:, t - 1:t]


def _rank(mask_t, eidx_t, pstart):
    e, n = mask_t.shape
    col = lambda i: (0, i)
    return pl.pallas_call(
        _rank_kernel,
        grid=(n // TOKEN_TILE,),
        in_specs=[pl.BlockSpec((e, TOKEN_TILE), col),
                  pl.BlockSpec((TOP_K, TOKEN_TILE), col),
                  pl.BlockSpec((e, 1), lambda i: (0, 0))],
        out_specs=pl.BlockSpec((TOP_K, TOKEN_TILE), col),
        out_shape=jax.ShapeDtypeStruct((TOP_K, n), jnp.int32),
        scratch_shapes=[pltpu.VMEM((e, 1), F32)],
        compiler_params=_params(("arbitrary",)),
        name="moe_rank",
    )(mask_t, eidx_t, pstart)


def _dispatch_kernel(dest_ref, x_hbm, xs_in, xs_hbm, sem):
    del xs_in
    base = pl.program_id(0) * TOKEN_TILE

    def body(t, carry):
        for k in range(TOP_K):
            pltpu.make_async_copy(x_hbm.at[pl.ds(base + t, 1)],
                                  xs_hbm.at[pl.ds(dest_ref[t * TOP_K + k], 1)], sem).start()
        return carry

    lax.fori_loop(0, TOKEN_TILE, body, 0)
    rows = TOKEN_TILE * TOP_K
    pltpu.make_async_copy(x_hbm.at[pl.ds(0, rows)], xs_hbm.at[pl.ds(0, rows)], sem).wait()


def _dispatch(dest_flat, hx2, xs_zero):
    n, d = hx2.shape
    rows = TOKEN_TILE * TOP_K
    return pl.pallas_call(
        _dispatch_kernel,
        grid=(n // TOKEN_TILE,),
        in_specs=[pl.BlockSpec((rows,), lambda i: (i,), memory_space=pltpu.SMEM),
                  pl.BlockSpec(memory_space=pl.ANY),
                  pl.BlockSpec(memory_space=pl.ANY)],
        out_specs=pl.BlockSpec(memory_space=pl.ANY),
        out_shape=jax.ShapeDtypeStruct(xs_zero.shape, xs_zero.dtype),
        scratch_shapes=[pltpu.SemaphoreType.DMA],
        input_output_aliases={2: 0},
        compiler_params=_params(("arbitrary",)),
        name="moe_dispatch",
    )(dest_flat, hx2, xs_zero)


def _experts_kernel(be_ref, nb_ref, xs_ref, wgu_ref, wd_ref, ys_ref):
    del be_ref
    ff = wd_ref.shape[1]

    @pl.when(pl.program_id(0) < nb_ref[0])
    def _():
        h = jnp.dot(xs_ref[...].astype(BF16), wgu_ref[0].astype(BF16), preferred_element_type=F32)
        a = _silu(h[:, :ff]) * h[:, ff:]
        ys_ref[...] = jnp.dot(a.astype(BF16), wd_ref[0].astype(BF16), preferred_element_type=F32)

    @pl.when(pl.program_id(0) >= nb_ref[0])
    def _():
        ys_ref[...] = jnp.zeros_like(ys_ref)


def _experts(block_e, n_used, xs, w_gu, w_down):
    rows, d = xs.shape
    nb = rows // EXPERT_BLOCK
    blk = lambda i, be, nu: (jnp.minimum(i, nu[0] - 1), 0)
    wmap = lambda i, be, nu: (be[i], 0, 0)
    return pl.pallas_call(
        _experts_kernel,
        grid_spec=pltpu.PrefetchScalarGridSpec(
            num_scalar_prefetch=2,
            grid=(nb,),
            in_specs=[pl.BlockSpec((EXPERT_BLOCK, d), blk),
                      pl.BlockSpec((1,) + w_gu.shape[1:], wmap),
                      pl.BlockSpec((1,) + w_down.shape[1:], wmap)],
            out_specs=pl.BlockSpec((EXPERT_BLOCK, d), lambda i, be, nu: (i, 0))),
        out_shape=jax.ShapeDtypeStruct((rows, d), F32),
        compiler_params=_params(("arbitrary",)),
        name="moe_experts",
    )(block_e, n_used, xs, w_gu, w_down)


def _combine_kernel(dest_ref, gw_ref, ys_hbm, hx2_ref, x1_ref, mod_ref, wsgu_ref, wsd_ref, fg_ref,
                    out_ref, buf, sem, *, tiles_per_seq, final_norm):
    t_rows, d = x1_ref.shape
    ff = wsd_ref.shape[0]
    row = pl.program_id(0) // tiles_per_seq

    def body(t, carry):
        for k in range(TOP_K):
            pltpu.make_async_copy(ys_hbm.at[pl.ds(dest_ref[t * TOP_K + k], 1)],
                                  buf.at[k, pl.ds(t, 1)], sem).start()
        return carry

    lax.fori_loop(0, t_rows, body, 0)

    hs = jnp.dot(hx2_ref[...].astype(BF16), wsgu_ref[...], preferred_element_type=F32)
    shared = jnp.dot((_silu(hs[:, :ff]) * hs[:, ff:]).astype(BF16), wsd_ref[...], preferred_element_type=F32)

    for k in range(TOP_K):
        pltpu.make_async_copy(ys_hbm.at[pl.ds(0, t_rows)], buf.at[k], sem).wait()
    gw = gw_ref[...]
    routed = gw[:, 0:1] * buf[0]
    for k in range(1, TOP_K):
        routed = routed + gw[:, k:k + 1] * buf[k]
    x2 = x1_ref[...] + _mod_chunk(mod_ref, row, 5, d) * (routed + shared)
    if final_norm:
        x2 = x2 * lax.rsqrt(jnp.mean(x2 * x2, axis=-1, keepdims=True) + NORM_EPS) * fg_ref[...]
    out_ref[...] = x2


def _combine(dest_flat, gw, ys, hx2, x1, mod, ws_gu, ws_down, fg, tiles_per_seq, final_norm):
    n, d = x1.shape
    rows = TOKEN_TILE * TOP_K
    row = lambda i: (i, 0)
    const = lambda i: (0, 0)
    return pl.pallas_call(
        functools.partial(_combine_kernel, tiles_per_seq=tiles_per_seq, final_norm=final_norm),
        grid=(n // TOKEN_TILE,),
        in_specs=[pl.BlockSpec((rows,), lambda i: (i,), memory_space=pltpu.SMEM),
                  pl.BlockSpec((TOKEN_TILE, TOP_K), row),
                  pl.BlockSpec(memory_space=pl.ANY),
                  pl.BlockSpec((TOKEN_TILE, d), row),
                  pl.BlockSpec((TOKEN_TILE, d), row),
                  pl.BlockSpec(mod.shape, const),
                  pl.BlockSpec(ws_gu.shape, const),
                  pl.BlockSpec(ws_down.shape, const),
                  pl.BlockSpec((1, d), const)],
        out_specs=pl.BlockSpec((TOKEN_TILE, d), row),
        out_shape=jax.ShapeDtypeStruct((n, d), F32),
        scratch_shapes=[pltpu.VMEM((TOP_K, TOKEN_TILE, d), F32), pltpu.SemaphoreType.DMA],
        compiler_params=_params(("arbitrary",)),
        name="moe_combine",
    )(dest_flat, gw, ys, hx2, x1, mod, ws_gu, ws_down, fg)


def _moe(x1, hx2, eidx_t, gw_t, mask_t, cnt, mod, w_gu, w_down, ws_gu, ws_down, fg, tiles_per_seq, final_norm):
    n, d = x1.shape
    counts = cnt[:, 0].astype(jnp.int32)
    padded = (counts + EXPERT_BLOCK - 1) // EXPERT_BLOCK * EXPERT_BLOCK
    pad_end = jnp.cumsum(padded)
    pad_start = pad_end - padded
    n_blocks = (n * TOP_K + N_EXPERTS * (EXPERT_BLOCK - 1) + EXPERT_BLOCK - 1) // EXPERT_BLOCK
    n_used = (pad_end[-1:] // EXPERT_BLOCK).astype(jnp.int32)
    block_start = jnp.arange(n_blocks, dtype=jnp.int32) * EXPERT_BLOCK
    block_e = jnp.minimum(jnp.sum((pad_end[None, :] <= block_start[:, None]).astype(jnp.int32), axis=1),
                          N_EXPERTS - 1)

    dest_t = _rank(mask_t, eidx_t, pad_start.astype(F32).reshape(N_EXPERTS, 1))
    dest_flat = dest_t.T.reshape(-1)
    xs = _dispatch(dest_flat, hx2, jnp.zeros((n_blocks * EXPERT_BLOCK, d), F32))
    ys = _experts(block_e, n_used, xs, w_gu, w_down)
    return _combine(dest_flat, gw_t.T, ys, hx2, x1, mod, ws_gu, ws_down, fg, tiles_per_seq, final_norm)


def _sincos_2d(rows, d):
    quarter = d // 4
    omega = 1.0 / (POS_BASE ** (jnp.arange(quarter, dtype=F32) / quarter))

    def emb(n):
        p = jnp.arange(n, dtype=F32)[:, None] * omega[None, :]
        return jnp.concatenate([jnp.sin(p), jnp.cos(p)], axis=-1)

    er, ec = emb(rows), emb(GRID_W)
    pe = jnp.concatenate([jnp.broadcast_to(er[:, None, :], (rows, GRID_W, d // 2)),
                          jnp.broadcast_to(ec[None, :, :], (rows, GRID_W, d // 2))], axis=-1)
    return pe.reshape(rows * GRID_W, d)


def kernel(x, c, ctx, c_ctx, ada_w, ada_b, mix_norm_g, ffn_norm_g, a_w_in, a_conv_w, a_conv_b, a_gate_r_w, a_gate_r_b, a_gate_i_w, a_gate_i_b, a_lambda, a_w_out, b_w_in, b_ln_g, b_ln_b, b_w_s, b_b_s, b_w_out, router_w, router_b, moe_w_gu, moe_w_down, shared_w_gu, shared_w_down, final_norm_g):
    bsz, s, d = x.shape
    ctx_len = ctx.shape[1]
    depth = ada_w.shape[0]
    assert depth == 2 and bsz < MOD_ROWS and s % TOKEN_TILE == 0 and ctx_len % TOKEN_TILE == 0
    n = bsz * s
    tps = s // TOKEN_TILE
    ctx_row = bsz

    cc = jnp.zeros((MOD_ROWS, d), F32).at[:bsz].set(c).at[ctx_row].set(c_ctx)
    mod = _modulation(cc, ada_w, ada_b)
    pe = _sincos_2d(s // GRID_W, d)
    rc = a_w_in.shape[2] // 2

    w_in0 = a_w_in[0].astype(BF16)
    g_mix0 = mix_norm_g[0].reshape(1, d)
    x0, gate, ux = _rglru_in(x.reshape(n, d), pe, mod[0], g_mix0, w_in0)
    uc = _ctx_in(ctx.reshape(bsz * ctx_len, d), mod[0], g_mix0, w_in0[:, rc:], ctx_row)
    w_ri = jnp.concatenate([a_gate_r_w[0], a_gate_i_w[0]], axis=-1).astype(BF16)
    scan_args = (a_conv_w[0], a_conv_b[0], w_ri, a_gate_r_b[0], a_gate_i_b[0], a_lambda[0])
    h_ctx = _lru_scan(uc, *scan_args, jnp.zeros((2 * bsz, rc), F32), n_batch=bsz, reset_first=True, emit_y=False)
    y = _lru_scan(ux, *scan_args, h_ctx, n_batch=bsz, reset_first=False, emit_y=True)
    pre = _rglru_out(y, gate, x0, mod[0], a_w_out[0].astype(BF16), ffn_norm_g[0].reshape(1, d),
                     router_w[0].T, router_b[0].reshape(N_EXPERTS, 1), tps)
    x1 = _moe(*pre, mod[0], moe_w_gu[0], moe_w_down[0], shared_w_gu[0].astype(BF16),
              shared_w_down[0].astype(BF16), final_norm_g.reshape(1, d), tps, False)

    pre = _sgu(x1, mod[1], mix_norm_g[1].reshape(1, d), b_w_in[0].astype(BF16),
               b_ln_g[0].reshape(1, -1), b_ln_b[0].reshape(1, -1), b_w_s[0].astype(BF16), b_b_s[0].T,
               b_w_out[0].astype(BF16), ffn_norm_g[1].reshape(1, d),
               router_w[1].T, router_b[1].reshape(N_EXPERTS, 1), tps)
    out = _moe(*pre, mod[1], moe_w_gu[1], moe_w_down[1], shared_w_gu[1].astype(BF16),
               shared_w_down[1].astype(BF16), final_norm_g.reshape(1, d), tps, True)
    return out.reshape(bsz, s, d)
```

```python
import functools

import jax
import jax.numpy as jnp
from jax import lax
from jax.experimental import pallas as pl
from jax.experimental.pallas import tpu as pltpu

F32 = jnp.float32
BF16 = jnp.bfloat16
HIGHEST = lax.Precision.HIGHEST

GRID_W = 64
N_MOD = 6
NORM_EPS = 1e-6
POS_BASE = 10000.0
RNN_HEADS = 5
CONV_WIDTH = 4
CONV_PAD_LEFT = 2
LRU_C = 8.0
SGU_HEADS = 8
CHUNK = 128
N_EXPERTS = 64
TOP_K = 8
N_GROUPS = 8
TOPK_GROUPS = 4
EXPERTS_PER_GROUP = N_EXPERTS // N_GROUPS
ROUTED_SCALE = 2.5

SUBLANES = 8
MOD_ROWS = 8
TOKEN_TILE = 256
EXPERT_BLOCK = 256
VMEM_LIMIT = 56 * 1024 * 1024


def _params(semantics, vmem=VMEM_LIMIT):
    return pltpu.CompilerParams(dimension_semantics=semantics, vmem_limit_bytes=vmem)


def _silu(x):
    return x * jax.nn.sigmoid(x)


def _rms_mod(x, g, sc, sh):
    y = x * lax.rsqrt(jnp.mean(x * x, axis=-1, keepdims=True) + NORM_EPS)
    return (y * g) * (1.0 + sc) + sh


def _mod_chunk(mod_ref, row, k, d):
    return mod_ref[pl.ds(row, 1), k * d:(k + 1) * d]


def _mod_kernel(cc_ref, w_ref, b_ref, o_ref):
    s = _silu(cc_ref[...])
    o_ref[0] = jnp.dot(s, w_ref[0], preferred_element_type=F32, precision=HIGHEST) + b_ref[0]


def _modulation(cc, ada_w, ada_b):
    depth, d, nd = ada_w.shape
    return pl.pallas_call(
        _mod_kernel,
        grid=(depth, nd // d),
        in_specs=[pl.BlockSpec((MOD_ROWS, d), lambda l, j: (0, 0)),
                  pl.BlockSpec((1, d, d), lambda l, j: (l, 0, j)),
                  pl.BlockSpec((1, 1, d), lambda l, j: (l, 0, j))],
        out_specs=pl.BlockSpec((1, MOD_ROWS, d), lambda l, j: (l, 0, j)),
        out_shape=jax.ShapeDtypeStruct((depth, MOD_ROWS, nd), F32),
        compiler_params=_params(("arbitrary", "arbitrary")),
        name="modulation",
    )(cc, ada_w, ada_b.reshape(depth, 1, nd))


def _rglru_in_kernel(x_ref, pe_ref, mod_ref, g_ref, w_ref, x0_ref, gate_ref, u_ref, *, tiles_per_seq):
    d = x_ref.shape[1]
    c = u_ref.shape[1]
    row = pl.program_id(0) // tiles_per_seq
    x = x_ref[...] + pe_ref[...]
    hx = _rms_mod(x, g_ref[...], _mod_chunk(mod_ref, row, 1, d), _mod_chunk(mod_ref, row, 0, d))
    z = jnp.dot(hx.astype(BF16), w_ref[...], preferred_element_type=F32)
    x0_ref[...] = x
    gate_ref[...] = jax.nn.gelu(z[:, :c]).astype(BF16)
    u_ref[...] = z[:, c:]


def _rglru_in(x2, pe, mod, g, w_in):
    n, d = x2.shape
    c = w_in.shape[1] // 2
    s = pe.shape[0]
    tps = s // TOKEN_TILE
    row = lambda i: (i, 0)
    return pl.pallas_call(
        functools.partial(_rglru_in_kernel, tiles_per_seq=tps),
        grid=(n // TOKEN_TILE,),
        in_specs=[pl.BlockSpec((TOKEN_TILE, d), row),
                  pl.BlockSpec((TOKEN_TILE, d), lambda i: (i % tps, 0)),
                  pl.BlockSpec(mod.shape, lambda i: (0, 0)),
                  pl.BlockSpec((1, d), lambda i: (0, 0)),
                  pl.BlockSpec(w_in.shape, lambda i: (0, 0))],
        out_specs=[pl.BlockSpec((TOKEN_TILE, d), row),
                   pl.BlockSpec((TOKEN_TILE, c), row),
                   pl.BlockSpec((TOKEN_TILE, c), row)],
        out_shape=[jax.ShapeDtypeStruct((n, d), F32),
                   jax.ShapeDtypeStruct((n, c), BF16),
                   jax.ShapeDtypeStruct((n, c), F32)],
        compiler_params=_params(("arbitrary",)),
        name="rglru_in",
    )(x2, pe, mod, g, w_in)


def _ctx_in_kernel(x_ref, mod_ref, g_ref, w_ref, u_ref, *, ctx_row):
    d = x_ref.shape[1]
    hx = _rms_mod(x_ref[...], g_ref[...], _mod_chunk(mod_ref, ctx_row, 1, d),
                  _mod_chunk(mod_ref, ctx_row, 0, d))
    u_ref[...] = jnp.dot(hx.astype(BF16), w_ref[...], preferred_element_type=F32)


def _ctx_in(c2, mod, g, w_u, ctx_row):
    n, d = c2.shape
    c = w_u.shape[1]
    return pl.pallas_call(
        functools.partial(_ctx_in_kernel, ctx_row=ctx_row),
        grid=(n // TOKEN_TILE,),
        in_specs=[pl.BlockSpec((TOKEN_TILE, d), lambda i: (i, 0)),
                  pl.BlockSpec(mod.shape, lambda i: (0, 0)),
                  pl.BlockSpec((1, d), lambda i: (0, 0)),
                  pl.BlockSpec(w_u.shape, lambda i: (0, 0))],
        out_specs=pl.BlockSpec((TOKEN_TILE, c), lambda i: (i, 0)),
        out_shape=jax.ShapeDtypeStruct((n, c), F32),
        compiler_params=_params(("arbitrary",)),
        name="ctx_in",
    )(c2, mod, g, w_u)


def _log_sigmoid(x):
    return jnp.minimum(x, 0.0) - jnp.log1p(jnp.exp(-jnp.abs(x)))


def _lru_scan_kernel(u_ref, up_ref, un_ref, cw_ref, cb_ref, wri_ref, rb_ref, ib_ref, lam_ref, h0_ref,
                     out_ref, ubuf, a_scr, b_scr, h_scr, *, n_batch, n_tiles, reset_first, emit_y):
    t_rows, c = u_ref.shape
    hb = c // RNN_HEADS
    d = pl.program_id(0)
    b = pl.program_id(1)
    j = pl.program_id(2)
    jj = j + d * (n_tiles - 1 - 2 * j)

    ubuf[SUBLANES:SUBLANES + t_rows, :] = u_ref[...]
    ubuf[0:SUBLANES, :] = jnp.where(jj == 0, 0.0, up_ref[...])
    ubuf[SUBLANES + t_rows:, :] = jnp.where(jj == n_tiles - 1, 0.0, un_ref[...])
    u = cb_ref[...]
    for k in range(CONV_WIDTH):
        u = u + cw_ref[k:k + 1, :] * ubuf[pl.ds(SUBLANES - CONV_PAD_LEFT + k, t_rows), :]

    log_lam = LRU_C * _log_sigmoid(lam_ref[0])
    rows = lax.broadcasted_iota(jnp.int32, (t_rows, 1), 0)
    first_row = jnp.where(j == 0, jnp.where(d == 0, 0, t_rows - 1), -1)
    for h in range(RNN_HEADS):
        sl = slice(h * hb, (h + 1) * hb)
        uh = u[:, sl]
        z = jnp.dot(uh.astype(BF16), wri_ref[0, h], preferred_element_type=F32)
        r = jax.nn.sigmoid(z[:, :hb] + rb_ref[0][:, sl])
        ig = jax.nn.sigmoid(z[:, hb:] + ib_ref[0][:, sl])
        log_a = r * log_lam[:, sl]
        a = jnp.exp(log_a)
        mult = jnp.sqrt((1.0 - a) * (1.0 + a))
        if reset_first:
            mult = jnp.where(rows == first_row, 1.0, mult)
        a_scr[:, sl] = a
        b_scr[:, sl] = mult * ig * uh

    @pl.when(j == 0)
    def _():
        h_scr[...] = h0_ref[pl.ds(d * n_batch + b, 1), :]

    def step(t, h):
        tt = t + d * (t_rows - 1 - 2 * t)
        h = a_scr[pl.ds(tt, 1), :] * h + b_scr[pl.ds(tt, 1), :]
        if emit_y:
            out_ref[0, pl.ds(tt, 1), :] = h
        return h

    h = lax.fori_loop(0, t_rows, step, h_scr[...], unroll=8)
    h_scr[...] = h
    if not emit_y:
        @pl.when(j == n_tiles - 1)
        def _():
            out_ref[pl.ds(d * n_batch + b, 1), :] = h


def _lru_scan(u, conv_w, conv_b, w_ri, r_b, i_b, lam, h0, *, n_batch, reset_first, emit_y):
    n, c = u.shape
    n_tiles = n // n_batch // TOKEN_TILE
    sub = TOKEN_TILE // SUBLANES
    n_sub = n // SUBLANES

    def tile(d, b, j):
        return b * n_tiles + j + d * (n_tiles - 1 - 2 * j)

    in_specs = [pl.BlockSpec((TOKEN_TILE, c), lambda d, b, j: (tile(d, b, j), 0)),
                pl.BlockSpec((SUBLANES, c), lambda d, b, j: (jnp.maximum(tile(d, b, j) * sub - 1, 0), 0)),
                pl.BlockSpec((SUBLANES, c), lambda d, b, j: (jnp.minimum((tile(d, b, j) + 1) * sub, n_sub - 1), 0)),
                pl.BlockSpec(conv_w.shape, lambda d, b, j: (0, 0)),
                pl.BlockSpec((1, c), lambda d, b, j: (0, 0)),
                pl.BlockSpec((1,) + w_ri.shape[1:], lambda d, b, j: (d, 0, 0, 0)),
                pl.BlockSpec((1, 1, c), lambda d, b, j: (d, 0, 0)),
                pl.BlockSpec((1, 1, c), lambda d, b, j: (d, 0, 0)),
                pl.BlockSpec((1, 1, c), lambda d, b, j: (d, 0, 0)),
                pl.BlockSpec(h0.shape, lambda d, b, j: (0, 0))]
    if emit_y:
        out_specs = pl.BlockSpec((1, TOKEN_TILE, c), lambda d, b, j: (d, tile(d, b, j), 0))
        out_shape = jax.ShapeDtypeStruct((2, n, c), F32)
    else:
        out_specs = pl.BlockSpec(h0.shape, lambda d, b, j: (0, 0))
        out_shape = jax.ShapeDtypeStruct(h0.shape, F32)
    return pl.pallas_call(
        functools.partial(_lru_scan_kernel, n_batch=n_batch, n_tiles=n_tiles,
                          reset_first=reset_first, emit_y=emit_y),
        grid=(2, n_batch, n_tiles),
        in_specs=in_specs,
        out_specs=out_specs,
        out_shape=out_shape,
        scratch_shapes=[pltpu.VMEM((TOKEN_TILE + 2 * SUBLANES, c), F32),
                        pltpu.VMEM((TOKEN_TILE, c), F32),
                        pltpu.VMEM((TOKEN_TILE, c), F32),
                        pltpu.VMEM((1, c), F32)],
        compiler_params=_params(("arbitrary", "arbitrary", "arbitrary")),
        name="lru_scan" if emit_y else "lru_scan_ctx",
    )(u, u, u, conv_w, conv_b.reshape(1, c), w_ri, r_b.reshape(2, 1, c), i_b.reshape(2, 1, c),
      lam.reshape(2, 1, c), h0)


def _route(logits, rb):
    e, t = logits.shape
    neg = -jnp.inf
    scores = jax.nn.sigmoid(logits)
    sel = scores + rb
    iota_g = lax.broadcasted_iota(jnp.int32, (N_GROUPS, t), 0).astype(F32)
    iota_e = lax.broadcasted_iota(jnp.int32, (e, t), 0).astype(F32)

    gs = jnp.full((N_GROUPS, t), neg, F32)
    for g in range(N_GROUPS):
        sg = sel[g * EXPERTS_PER_GROUP:(g + 1) * EXPERTS_PER_GROUP, :]
        m1 = jnp.max(sg, axis=0, keepdims=True)
        i1 = jnp.min(jnp.where(sg == m1, iota_g, float(EXPERTS_PER_GROUP)), axis=0, keepdims=True)
        m2 = jnp.max(jnp.where(iota_g == i1, neg, sg), axis=0, keepdims=True)
        gs = jnp.where(iota_g == float(g), m1 + m2, gs)

    keep = jnp.zeros((N_GROUPS, t), F32)
    for _ in range(TOPK_GROUPS):
        m = jnp.max(gs, axis=0, keepdims=True)
        idx = jnp.min(jnp.where(gs == m, iota_g, float(N_GROUPS)), axis=0, keepdims=True)
        hit = iota_g == idx
        keep = jnp.where(hit, 1.0, keep)
        gs = jnp.where(hit, neg, gs)

    masked = jnp.concatenate(
        [jnp.where(keep[g:g + 1, :] > 0.0, sel[g * EXPERTS_PER_GROUP:(g + 1) * EXPERTS_PER_GROUP, :], neg)
         for g in range(N_GROUPS)], axis=0)

    iota_k = lax.broadcasted_iota(jnp.int32, (TOP_K, t), 0)
    selmask = jnp.zeros((e, t), F32)
    eidx = jnp.zeros((TOP_K, t), F32)
    gw = jnp.zeros((TOP_K, t), F32)
    for k in range(TOP_K):
        m = jnp.max(masked, axis=0, keepdims=True)
        idx = jnp.min(jnp.where(masked == m, iota_e, float(e)), axis=0, keepdims=True)
        hit = iota_e == idx
        gk = jnp.sum(jnp.where(hit, scores, 0.0), axis=0, keepdims=True)
        masked = jnp.where(hit, neg, masked)
        selmask = jnp.where(hit, 1.0, selmask)
        eidx = jnp.where(iota_k == k, idx, eidx)
        gw = jnp.where(iota_k == k, gk, gw)
    gw = gw / jnp.sum(gw, axis=0, keepdims=True) * ROUTED_SCALE
    return eidx, gw, selmask


def _ffn_pre(x1, mod_ref, row, g2_ref, rwt_ref, rb_ref, x1_ref, hx2_ref, eidx_ref, gw_ref, mask_ref, cnt_ref):
    d = x1.shape[1]
    hx2 = _rms_mod(x1, g2_ref[...], _mod_chunk(mod_ref, row, 4, d), _mod_chunk(mod_ref, row, 3, d))
    x1_ref[...] = x1
    hx2_ref[...] = hx2
    logits = lax.dot_general(rwt_ref[...], hx2, (((1,), (1,)), ((), ())),
                             preferred_element_type=F32, precision=HIGHEST)
    eidx, gw, selmask = _route(logits, rb_ref[...])
    eidx_ref[...] = eidx.astype(jnp.int32)
    gw_ref[...] = gw
    mask_ref[...] = selmask

    @pl.when(pl.program_id(0) == 0)
    def _():
        cnt_ref[...] = jnp.zeros_like(cnt_ref)

    cnt_ref[...] += jnp.broadcast_to(jnp.sum(selmask, axis=1, keepdims=True), cnt_ref.shape)


def _ffn_pre_specs(n, d):
    row = lambda i: (i, 0)
    col = lambda i: (0, i)
    out_specs = [pl.BlockSpec((TOKEN_TILE, d), row),
                 pl.BlockSpec((TOKEN_TILE, d), row),
                 pl.BlockSpec((TOP_K, TOKEN_TILE), col),
                 pl.BlockSpec((TOP_K, TOKEN_TILE), col),
                 pl.BlockSpec((N_EXPERTS, TOKEN_TILE), col),
                 pl.BlockSpec((N_EXPERTS, 128), lambda i: (0, 0))]
    out_shape = [jax.ShapeDtypeStruct((n, d), F32),
                 jax.ShapeDtypeStruct((n, d), F32),
                 jax.ShapeDtypeStruct((TOP_K, n), jnp.int32),
                 jax.ShapeDtypeStruct((TOP_K, n), F32),
                 jax.ShapeDtypeStruct((N_EXPERTS, n), F32),
                 jax.ShapeDtypeStruct((N_EXPERTS, 128), F32)]
    return out_specs, out_shape


def _rglru_out_kernel(y_ref, gate_ref, x0_ref, mod_ref, wout_ref, g2_ref, rwt_ref, rb_ref,
                      x1_ref, hx2_ref, eidx_ref, gw_ref, mask_ref, cnt_ref, *, tiles_per_seq):
    d = x0_ref.shape[1]
    row = pl.program_id(0) // tiles_per_seq
    yx = y_ref[0] + y_ref[1]
    v = gate_ref[...].astype(F32) * yx
    out = jnp.dot(v.astype(BF16), wout_ref[...], preferred_element_type=F32)
    x1 = x0_ref[...] + _mod_chunk(mod_ref, row, 2, d) * out
    _ffn_pre(x1, mod_ref, row, g2_ref, rwt_ref, rb_ref, x1_ref, hx2_ref, eidx_ref, gw_ref, mask_ref, cnt_ref)


def _rglru_out(y, gate, x0, mod, w_out, g2, rwt, rb, tiles_per_seq):
    n, d = x0.shape
    c = gate.shape[1]
    row = lambda i: (i, 0)
    const = lambda i: (0, 0)
    out_specs, out_shape = _ffn_pre_specs(n, d)
    return pl.pallas_call(
        functools.partial(_rglru_out_kernel, tiles_per_seq=tiles_per_seq),
        grid=(n // TOKEN_TILE,),
        in_specs=[pl.BlockSpec((2, TOKEN_TILE, c), lambda i: (0, i, 0)),
                  pl.BlockSpec((TOKEN_TILE, c), row),
                  pl.BlockSpec((TOKEN_TILE, d), row),
                  pl.BlockSpec(mod.shape, const),
                  pl.BlockSpec(w_out.shape, const),
                  pl.BlockSpec((1, d), const),
                  pl.BlockSpec(rwt.shape, const),
                  pl.BlockSpec(rb.shape, const)],
        out_specs=out_specs,
        out_shape=out_shape,
        compiler_params=_params(("arbitrary",)),
        name="rglru_out",
    )(y, gate, x0, mod, w_out, g2, rwt, rb)


def _sgu_kernel(x_ref, mod_ref, g_ref, win_ref, lng_ref, lnb_ref, ws_ref, bst_ref, wout_ref,
                g2_ref, rwt_ref, rb_ref,
                x1_ref, hx2_ref, eidx_ref, gw_ref, mask_ref, cnt_ref, m_scr, *, tiles_per_seq):
    t_rows, d = x_ref.shape
    w = wout_ref.shape[0]
    gd = w // SGU_HEADS
    row = pl.program_id(0) // tiles_per_seq
    x = x_ref[...]
    hx = _rms_mod(x, g_ref[...], _mod_chunk(mod_ref, row, 1, d), _mod_chunk(mod_ref, row, 0, d))
    z = jax.nn.gelu(jnp.dot(hx.astype(BF16), win_ref[...], preferred_element_type=F32))
    u = z[:, :w]
    v = z[:, w:]
    mu = jnp.mean(v, axis=-1, keepdims=True)
    vc = v - mu
    v = vc * lax.rsqrt(jnp.mean(vc * vc, axis=-1, keepdims=True) + NORM_EPS) * lng_ref[...] + lnb_ref[...]
    vb = v.astype(BF16)
    for ch in range(t_rows // CHUNK):
        rs = slice(ch * CHUNK, (ch + 1) * CHUNK)
        for g in range(SGU_HEADS):
            cs = slice(g * gd, (g + 1) * gd)
            sv = jnp.dot(ws_ref[g], vb[rs, cs], preferred_element_type=F32) + bst_ref[:, g:g + 1]
            m_scr[rs, cs] = (u[rs, cs] * sv).astype(BF16)
    out = jnp.dot(m_scr[...], wout_ref[...], preferred_element_type=F32)
    x1 = x + _mod_chunk(mod_ref, row, 2, d) * out
    _ffn_pre(x1, mod_ref, row, g2_ref, rwt_ref, rb_ref, x1_ref, hx2_ref, eidx_ref, gw_ref, mask_ref, cnt_ref)


def _sgu(x, mod, g, w_in, ln_g, ln_b, w_s, b_st, w_out, g2, rwt, rb, tiles_per_seq):
    n, d = x.shape
    w = w_out.shape[0]
    const = lambda i: (0, 0)
    out_specs, out_shape = _ffn_pre_specs(n, d)
    return pl.pallas_call(
        functools.partial(_sgu_kernel, tiles_per_seq=tiles_per_seq),
        grid=(n // TOKEN_TILE,),
        in_specs=[pl.BlockSpec((TOKEN_TILE, d), lambda i: (i, 0)),
                  pl.BlockSpec(mod.shape, const),
                  pl.BlockSpec((1, d), const),
                  pl.BlockSpec(w_in.shape, const),
                  pl.BlockSpec((1, w), const),
                  pl.BlockSpec((1, w), const),
                  pl.BlockSpec(w_s.shape, lambda i: (0, 0, 0)),
                  pl.BlockSpec(b_st.shape, const),
                  pl.BlockSpec(w_out.shape, const),
                  pl.BlockSpec((1, d), const),
                  pl.BlockSpec(rwt.shape, const),
                  pl.BlockSpec(rb.shape, const)],
        out_specs=out_specs,
        out_shape=out_shape,
        scratch_shapes=[pltpu.VMEM((TOKEN_TILE, w), BF16)],
        compiler_params=_params(("arbitrary",)),
        name="sgu",
    )(x, mod, g, w_in, ln_g, ln_b, w_s, b_st, w_out, g2, rwt, rb)


def _rank_kernel(mask_ref, eidx_ref, pstart_ref, dest_ref, carry):
    e, t = mask_ref.shape

    @pl.when(pl.program_id(0) == 0)
    def _():
        carry[...] = jnp.zeros_like(carry)

    m = mask_ref[...]
    r = lax.broadcasted_iota(jnp.int32, (t, t), 0)
    c = lax.broadcasted_iota(jnp.int32, (t, t), 1)
    upper = jnp.where(r <= c, 1.0, 0.0).astype(BF16)
    incl = jnp.dot(m.astype(BF16), upper, preferred_element_type=F32)
    pos = carry[...] + incl - m + pstart_ref[...]
    iota_e = lax.broadcasted_iota(jnp.int32, (e, t), 0)
    iota_k = lax.broadcasted_iota(jnp.int32, (TOP_K, t), 0)
    eidx = eidx_ref[...]
    dest = jnp.zeros((TOP_K, t), F32)
    for k in range(TOP_K):
        dk = jnp.sum(jnp.where(iota_e == eidx[k:k + 1, :], pos, 0.0), axis=0, keepdims=True)
        dest = jnp.where(iota_k == k, dk, dest)
    dest_ref[...] = dest.astype(jnp.int32)
    carry[...] = carry[...] + incl[:, t - 1:t]


def _rank(mask_t, eidx_t, pstart):
    e, n = mask_t.shape
    col = lambda i: (0, i)
    return pl.pallas_call(
        _rank_kernel,
        grid=(n // TOKEN_TILE,),
        in_specs=[pl.BlockSpec((e, TOKEN_TILE), col),
                  pl.BlockSpec((TOP_K, TOKEN_TILE), col),
                  pl.BlockSpec((e, 1), lambda i: (0, 0))],
        out_specs=pl.BlockSpec((TOP_K, TOKEN_TILE), col),
        out_shape=jax.ShapeDtypeStruct((TOP_K, n), jnp.int32),
        scratch_shapes=[pltpu.VMEM((e, 1), F32)],
        compiler_params=_params(("arbitrary",)),
        name="moe_rank",
    )(mask_t, eidx_t, pstart)


def _dispatch_kernel(dest_ref, x_ref, xs_in, xs_hbm, sem):
    del xs_in
    t_rows = x_ref.shape[0]

    def body(t, carry):
        for k in range(TOP_K):
            pltpu.make_async_copy(x_ref.at[pl.ds(t, 1)],
                                  xs_hbm.at[pl.ds(dest_ref[t * TOP_K + k], 1)], sem).start()
        return carry

    lax.fori_loop(0, t_rows, body, 0)
    for _ in range(TOP_K):
        pltpu.make_async_copy(x_ref, xs_hbm.at[pl.ds(0, t_rows)], sem).wait()


def _dispatch(dest_flat, hx2, xs_zero):
    n, d = hx2.shape
    rows = TOKEN_TILE * TOP_K
    return pl.pallas_call(
        _dispatch_kernel,
        grid=(n // TOKEN_TILE,),
        in_specs=[pl.BlockSpec((rows,), lambda i: (i,), memory_space=pltpu.SMEM),
                  pl.BlockSpec((TOKEN_TILE, d), lambda i: (i, 0)),
                  pl.BlockSpec(memory_space=pl.ANY)],
        out_specs=pl.BlockSpec(memory_space=pl.ANY),
        out_shape=jax.ShapeDtypeStruct(xs_zero.shape, xs_zero.dtype),
        scratch_shapes=[pltpu.SemaphoreType.DMA],
        input_output_aliases={2: 0},
        compiler_params=_params(("arbitrary",)),
        name="moe_dispatch",
    )(dest_flat, hx2, xs_zero)


def _experts_kernel(be_ref, nb_ref, xs_ref, wgu_ref, wd_ref, ys_ref):
    del be_ref
    ff = wd_ref.shape[1]

    @pl.when(pl.program_id(0) < nb_ref[0])
    def _():
        h = jnp.dot(xs_ref[...].astype(BF16), wgu_ref[0].astype(BF16), preferred_element_type=F32)
        a = _silu(h[:, :ff]) * h[:, ff:]
        ys_ref[...] = jnp.dot(a.astype(BF16), wd_ref[0].astype(BF16), preferred_element_type=F32)

    @pl.when(pl.program_id(0) >= nb_ref[0])
    def _():
        ys_ref[...] = jnp.zeros_like(ys_ref)


def _experts(block_e, n_used, xs, w_gu, w_down):
    rows, d = xs.shape
    nb = rows // EXPERT_BLOCK
    blk = lambda i, be, nu: (jnp.maximum(jnp.minimum(i, nu[0] - 1), 0), 0)
    wmap = lambda i, be, nu: (be[i], 0, 0)
    return pl.pallas_call(
        _experts_kernel,
        grid_spec=pltpu.PrefetchScalarGridSpec(
            num_scalar_prefetch=2,
            grid=(nb,),
            in_specs=[pl.BlockSpec((EXPERT_BLOCK, d), blk),
                      pl.BlockSpec((1,) + w_gu.shape[1:], wmap),
                      pl.BlockSpec((1,) + w_down.shape[1:], wmap)],
            out_specs=pl.BlockSpec((EXPERT_BLOCK, d), lambda i, be, nu: (i, 0))),
        out_shape=jax.ShapeDtypeStruct((rows, d), F32),
        compiler_params=_params(("arbitrary",)),
        name="moe_experts",
    )(block_e, n_used, xs, w_gu, w_down)


def _combine_kernel(dest_ref, gw_ref, ys_hbm, hx2_ref, x1_ref, mod_ref, wsgu_ref, wsd_ref, fg_ref,
                    out_ref, buf, sem, *, tiles_per_seq, final_norm):
    t_rows, d = x1_ref.shape
    ff = wsd_ref.shape[0]
    row = pl.program_id(0) // tiles_per_seq

    def body(t, carry):
        for k in range(TOP_K):
            pltpu.make_async_copy(ys_hbm.at[pl.ds(dest_ref[t * TOP_K + k], 1)],
                                  buf.at[k, pl.ds(t, 1)], sem).start()
        return carry

    lax.fori_loop(0, t_rows, body, 0)

    hs = jnp.dot(hx2_ref[...].astype(BF16), wsgu_ref[...], preferred_element_type=F32)
    shared = jnp.dot((_silu(hs[:, :ff]) * hs[:, ff:]).astype(BF16), wsd_ref[...], preferred_element_type=F32)

    for k in range(TOP_K):
        pltpu.make_async_copy(ys_hbm.at[pl.ds(0, t_rows)], buf.at[k], sem).wait()
    gw = gw_ref[...]
    routed = gw[:, 0:1] * buf[0]
    for k in range(1, TOP_K):
        routed = routed + gw[:, k:k + 1] * buf[k]
    x2 = x1_ref[...] + _mod_chunk(mod_ref, row, 5, d) * (routed + shared)
    if final_norm:
        x2 = x2 * lax.rsqrt(jnp.mean(x2 * x2, axis=-1, keepdims=True) + NORM_EPS) * fg_ref[...]
    out_ref[...] = x2


def _combine(dest_flat, gw, ys, hx2, x1, mod, ws_gu, ws_down, fg, tiles_per_seq, final_norm):
    n, d = x1.shape
    rows = TOKEN_TILE * TOP_K
    row = lambda i: (i, 0)
    const = lambda i: (0, 0)
    return pl.pallas_call(
        functools.partial(_combine_kernel, tiles_per_seq=tiles_per_seq, final_norm=final_norm),
        grid=(n // TOKEN_TILE,),
        in_specs=[pl.BlockSpec((rows,), lambda i: (i,), memory_space=pltpu.SMEM),
                  pl.BlockSpec((TOKEN_TILE, TOP_K), row),
                  pl.BlockSpec(memory_space=pl.ANY),
                  pl.BlockSpec((TOKEN_TILE, d), row),
                  pl.BlockSpec((TOKEN_TILE, d), row),
                  pl.BlockSpec(mod.shape, const),
                  pl.BlockSpec(ws_gu.shape, const),
                  pl.BlockSpec(ws_down.shape, const),
                  pl.BlockSpec((1, d), const)],
        out_specs=pl.BlockSpec((TOKEN_TILE, d), row),
        out_shape=jax.ShapeDtypeStruct((n, d), F32),
        scratch_shapes=[pltpu.VMEM((TOP_K, TOKEN_TILE, d), F32), pltpu.SemaphoreType.DMA],
        compiler_params=_params(("arbitrary",)),
        name="moe_combine",
    )(dest_flat, gw, ys, hx2, x1, mod, ws_gu, ws_down, fg)


def _moe(x1, hx2, eidx_t, gw_t, mask_t, cnt, mod, w_gu, w_down, ws_gu, ws_down, fg, tiles_per_seq, final_norm):
    n, d = x1.shape
    counts = cnt[:, 0].astype(jnp.int32)
    padded = (counts + EXPERT_BLOCK - 1) // EXPERT_BLOCK * EXPERT_BLOCK
    pad_end = jnp.cumsum(padded)
    pad_start = pad_end - padded
    n_blocks = (n * TOP_K + N_EXPERTS * (EXPERT_BLOCK - 1) + EXPERT_BLOCK - 1) // EXPERT_BLOCK
    n_used = (pad_end[-1:] // EXPERT_BLOCK).astype(jnp.int32)
    block_start = jnp.arange(n_blocks, dtype=jnp.int32) * EXPERT_BLOCK
    block_e = jnp.minimum(jnp.sum((pad_end[None, :] <= block_start[:, None]).astype(jnp.int32), axis=1),
                          N_EXPERTS - 1)

    dest_t = _rank(mask_t, eidx_t, pad_start.astype(F32).reshape(N_EXPERTS, 1))
    dest_flat = dest_t.T.reshape(-1)
    xs = _dispatch(dest_flat, hx2, jnp.zeros((n_blocks * EXPERT_BLOCK, d), F32))
    ys = _experts(block_e, n_used, xs, w_gu, w_down)
    return _combine(dest_flat, gw_t.T, ys, hx2, x1, mod, ws_gu, ws_down, fg, tiles_per_seq, final_norm)


def _sincos_2d(rows, d):
    quarter = d // 4
    omega = 1.0 / (POS_BASE ** (jnp.arange(quarter, dtype=F32) / quarter))

    def emb(n):
        p = jnp.arange(n, dtype=F32)[:, None] * omega[None, :]
        return jnp.concatenate([jnp.sin(p), jnp.cos(p)], axis=-1)

    er, ec = emb(rows), emb(GRID_W)
    pe = jnp.concatenate([jnp.broadcast_to(er[:, None, :], (rows, GRID_W, d // 2)),
                          jnp.broadcast_to(ec[None, :, :], (rows, GRID_W, d // 2))], axis=-1)
    return pe.reshape(rows * GRID_W, d)


def kernel(x, c, ctx, c_ctx, ada_w, ada_b, mix_norm_g, ffn_norm_g, a_w_in, a_conv_w, a_conv_b, a_gate_r_w, a_gate_r_b, a_gate_i_w, a_gate_i_b, a_lambda, a_w_out, b_w_in, b_ln_g, b_ln_b, b_w_s, b_b_s, b_w_out, router_w, router_b, moe_w_gu, moe_w_down, shared_w_gu, shared_w_down, final_norm_g):
    bsz, s, d = x.shape
    ctx_len = ctx.shape[1]
    depth = ada_w.shape[0]
    assert depth == 2 and bsz < MOD_ROWS and s % TOKEN_TILE == 0 and ctx_len % TOKEN_TILE == 0
    n = bsz * s
    tps = s // TOKEN_TILE
    ctx_row = bsz

    cc = jnp.zeros((MOD_ROWS, d), F32).at[:bsz].set(c).at[ctx_row].set(c_ctx)
    mod = _modulation(cc, ada_w, ada_b)
    pe = _sincos_2d(s // GRID_W, d)
    rc = a_w_in.shape[2] // 2

    w_in0 = a_w_in[0].astype(BF16)
    g_mix0 = mix_norm_g[0].reshape(1, d)
    x0, gate, ux = _rglru_in(x.reshape(n, d), pe, mod[0], g_mix0, w_in0)
    uc = _ctx_in(ctx.reshape(bsz * ctx_len, d), mod[0], g_mix0, w_in0[:, rc:], ctx_row)
    w_ri = jnp.concatenate([a_gate_r_w[0], a_gate_i_w[0]], axis=-1).astype(BF16)
    scan_args = (a_conv_w[0], a_conv_b[0], w_ri, a_gate_r_b[0], a_gate_i_b[0], a_lambda[0])
    h_ctx = _lru_scan(uc, *scan_args, jnp.zeros((2 * bsz, rc), F32), n_batch=bsz, reset_first=True, emit_y=False)
    y = _lru_scan(ux, *scan_args, h_ctx, n_batch=bsz, reset_first=False, emit_y=True)
    pre = _rglru_out(y, gate, x0, mod[0], a_w_out[0].astype(BF16), ffn_norm_g[0].reshape(1, d),
                     router_w[0].T, router_b[0].reshape(N_EXPERTS, 1), tps)
    x1 = _moe(*pre, mod[0], moe_w_gu[0], moe_w_down[0], shared_w_gu[0].astype(BF16),
              shared_w_down[0].astype(BF16), final_norm_g.reshape(1, d), tps, False)

    pre = _sgu(x1, mod[1], mix_norm_g[1].reshape(1, d), b_w_in[0].astype(BF16),
               b_ln_g[0].reshape(1, -1), b_ln_b[0].reshape(1, -1), b_w_s[0].astype(BF16), b_b_s[0].T,
               b_w_out[0].astype(BF16), ffn_norm_g[1].reshape(1, d),
               router_w[1].T, router_b[1].reshape(N_EXPERTS, 1), tps)
    out = _moe(*pre, mod[1], moe_w_gu[1], moe_w_down[1], shared_w_gu[1].astype(BF16),
               shared_w_down[1].astype(BF16), final_norm_g.reshape(1, d), tps, True)
    return out.reshape(bsz, s, d)
```

```python
import functools

import jax
import jax.numpy as jnp
from jax import lax
from jax.experimental import pallas as pl
from jax.experimental.pallas import tpu as pltpu

F32 = jnp.float32
BF16 = jnp.bfloat16
HIGHEST = lax.Precision.HIGHEST

GRID_W = 64
N_MOD = 6
NORM_EPS = 1e-6
POS_BASE = 10000.0
RNN_HEADS = 5
CONV_WIDTH = 4
CONV_PAD_LEFT = 2
LRU_C = 8.0
SGU_HEADS = 8
CHUNK = 128
N_EXPERTS = 64
TOP_K = 8
N_GROUPS = 8
TOPK_GROUPS = 4
EXPERTS_PER_GROUP = N_EXPERTS // N_GROUPS
ROUTED_SCALE = 2.5

SUBLANES = 8
LANES = 128
MOD_ROWS = 8
TOKEN_TILE = 256
EXPERT_BLOCK = 512
SORTED_ROWS = TOKEN_TILE * TOP_K + N_EXPERTS * SUBLANES
PERM_CHUNK = 512
TAB_WIDTH = 256
VMEM_LIMIT = 56 * 1024 * 1024


def _params(semantics, vmem=VMEM_LIMIT):
    return pltpu.CompilerParams(dimension_semantics=semantics, vmem_limit_bytes=vmem)


def _silu(x):
    return x * jax.nn.sigmoid(x)


def _rms_mod(x, g, sc, sh):
    y = x * lax.rsqrt(jnp.mean(x * x, axis=-1, keepdims=True) + NORM_EPS)
    return (y * g) * (1.0 + sc) + sh


def _mod_chunk(mod_ref, row, k, d):
    return mod_ref[pl.ds(row, 1), k * d:(k + 1) * d]


def _round_up_rows(count):
    return jnp.ceil(count * (1.0 / SUBLANES)) * float(SUBLANES)


def _mod_kernel(cc_ref, w_ref, b_ref, o_ref):
    s = _silu(cc_ref[...])
    o_ref[0] = jnp.dot(s, w_ref[0], preferred_element_type=F32, precision=HIGHEST) + b_ref[0]


def _modulation(cc, ada_w, ada_b):
    depth, d, nd = ada_w.shape
    return pl.pallas_call(
        _mod_kernel,
        grid=(depth, nd // d),
        in_specs=[pl.BlockSpec((MOD_ROWS, d), lambda l, j: (0, 0)),
                  pl.BlockSpec((1, d, d), lambda l, j: (l, 0, j)),
                  pl.BlockSpec((1, 1, d), lambda l, j: (l, 0, j))],
        out_specs=pl.BlockSpec((1, MOD_ROWS, d), lambda l, j: (l, 0, j)),
        out_shape=jax.ShapeDtypeStruct((depth, MOD_ROWS, nd), F32),
        compiler_params=_params(("arbitrary", "arbitrary")),
        name="modulation",
    )(cc, ada_w, ada_b.reshape(depth, 1, nd))


def _rglru_in_kernel(x_ref, pe_ref, mod_ref, g_ref, w_ref, x0_ref, gate_ref, u_ref, *, tiles_per_seq):
    d = x_ref.shape[1]
    c = u_ref.shape[1]
    row = pl.program_id(0) // tiles_per_seq
    x = x_ref[...] + pe_ref[...]
    hx = _rms_mod(x, g_ref[...], _mod_chunk(mod_ref, row, 1, d), _mod_chunk(mod_ref, row, 0, d))
    z = jnp.dot(hx.astype(BF16), w_ref[...], preferred_element_type=F32)
    x0_ref[...] = x
    gate_ref[...] = jax.nn.gelu(z[:, :c]).astype(BF16)
    u_ref[...] = z[:, c:]


def _rglru_in(x2, pe, mod, g, w_in):
    n, d = x2.shape
    c = w_in.shape[1] // 2
    s = pe.shape[0]
    tps = s // TOKEN_TILE
    row = lambda i: (i, 0)
    return pl.pallas_call(
        functools.partial(_rglru_in_kernel, tiles_per_seq=tps),
        grid=(n // TOKEN_TILE,),
        in_specs=[pl.BlockSpec((TOKEN_TILE, d), row),
                  pl.BlockSpec((TOKEN_TILE, d), lambda i: (i % tps, 0)),
                  pl.BlockSpec(mod.shape, lambda i: (0, 0)),
                  pl.BlockSpec((1, d), lambda i: (0, 0)),
                  pl.BlockSpec(w_in.shape, lambda i: (0, 0))],
        out_specs=[pl.BlockSpec((TOKEN_TILE, d), row),
                   pl.BlockSpec((TOKEN_TILE, c), row),
                   pl.BlockSpec((TOKEN_TILE, c), row)],
        out_shape=[jax.ShapeDtypeStruct((n, d), F32),
                   jax.ShapeDtypeStruct((n, c), BF16),
                   jax.ShapeDtypeStruct((n, c), F32)],
        compiler_params=_params(("arbitrary",)),
        name="rglru_in",
    )(x2, pe, mod, g, w_in)


def _ctx_in_kernel(x_ref, mod_ref, g_ref, w_ref, u_ref, *, ctx_row):
    d = x_ref.shape[1]
    hx = _rms_mod(x_ref[...], g_ref[...], _mod_chunk(mod_ref, ctx_row, 1, d),
                  _mod_chunk(mod_ref, ctx_row, 0, d))
    u_ref[...] = jnp.dot(hx.astype(BF16), w_ref[...], preferred_element_type=F32)


def _ctx_in(c2, mod, g, w_u, ctx_row):
    n, d = c2.shape
    c = w_u.shape[1]
    return pl.pallas_call(
        functools.partial(_ctx_in_kernel, ctx_row=ctx_row),
        grid=(n // TOKEN_TILE,),
        in_specs=[pl.BlockSpec((TOKEN_TILE, d), lambda i: (i, 0)),
                  pl.BlockSpec(mod.shape, lambda i: (0, 0)),
                  pl.BlockSpec((1, d), lambda i: (0, 0)),
                  pl.BlockSpec(w_u.shape, lambda i: (0, 0))],
        out_specs=pl.BlockSpec((TOKEN_TILE, c), lambda i: (i, 0)),
        out_shape=jax.ShapeDtypeStruct((n, c), F32),
        compiler_params=_params(("arbitrary",)),
        name="ctx_in",
    )(c2, mod, g, w_u)


def _log_sigmoid(x):
    return jnp.minimum(x, 0.0) - jnp.log1p(jnp.exp(-jnp.abs(x)))


def _lru_scan_kernel(u_ref, up_ref, un_ref, cw_ref, cb_ref, wri_ref, rb_ref, ib_ref, lam_ref, h0_ref,
                     out_ref, ubuf, a_scr, b_scr, h_scr, *, n_batch, n_tiles, reset_first, emit_y):
    t_rows, c = u_ref.shape
    hb = c // RNN_HEADS
    d = pl.program_id(0)
    b = pl.program_id(1)
    j = pl.program_id(2)
    jj = j + d * (n_tiles - 1 - 2 * j)

    ubuf[SUBLANES:SUBLANES + t_rows, :] = u_ref[...]
    ubuf[0:SUBLANES, :] = jnp.where(jj == 0, 0.0, up_ref[...])
    ubuf[SUBLANES + t_rows:, :] = jnp.where(jj == n_tiles - 1, 0.0, un_ref[...])
    u = cb_ref[...]
    for k in range(CONV_WIDTH):
        u = u + cw_ref[k:k + 1, :] * ubuf[pl.ds(SUBLANES - CONV_PAD_LEFT + k, t_rows), :]

    log_lam = LRU_C * _log_sigmoid(lam_ref[0])
    rows = lax.broadcasted_iota(jnp.int32, (t_rows, 1), 0)
    first_row = jnp.where(j == 0, jnp.where(d == 0, 0, t_rows - 1), -1)
    for h in range(RNN_HEADS):
        sl = slice(h * hb, (h + 1) * hb)
        uh = u[:, sl]
        z = jnp.dot(uh.astype(BF16), wri_ref[0, h], preferred_element_type=F32)
        r = jax.nn.sigmoid(z[:, :hb] + rb_ref[0][:, sl])
        ig = jax.nn.sigmoid(z[:, hb:] + ib_ref[0][:, sl])
        log_a = r * log_lam[:, sl]
        a = jnp.exp(log_a)
        mult = jnp.sqrt((1.0 - a) * (1.0 + a))
        if reset_first:
            mult = jnp.where(rows == first_row, 1.0, mult)
        a_scr[:, sl] = a
        b_scr[:, sl] = mult * ig * uh

    @pl.when(j == 0)
    def _():
        h_scr[...] = h0_ref[pl.ds(d * n_batch + b, 1), :]

    def step(t, h):
        tt = t + d * (t_rows - 1 - 2 * t)
        h = a_scr[pl.ds(tt, 1), :] * h + b_scr[pl.ds(tt, 1), :]
        if emit_y:
            out_ref[0, pl.ds(tt, 1), :] = h
        return h

    h = lax.fori_loop(0, t_rows, step, h_scr[...], unroll=8)
    h_scr[...] = h
    if not emit_y:
        @pl.when(j == n_tiles - 1)
        def _():
            out_ref[pl.ds(d * n_batch + b, 1), :] = h


def _lru_scan(u, conv_w, conv_b, w_ri, r_b, i_b, lam, h0, *, n_batch, reset_first, emit_y):
    n, c = u.shape
    n_tiles = n // n_batch // TOKEN_TILE
    sub = TOKEN_TILE // SUBLANES
    n_sub = n // SUBLANES

    def tile(d, b, j):
        return b * n_tiles + j + d * (n_tiles - 1 - 2 * j)

    in_specs = [pl.BlockSpec((TOKEN_TILE, c), lambda d, b, j: (tile(d, b, j), 0)),
                pl.BlockSpec((SUBLANES, c), lambda d, b, j: (jnp.maximum(tile(d, b, j) * sub - 1, 0), 0)),
                pl.BlockSpec((SUBLANES, c), lambda d, b, j: (jnp.minimum((tile(d, b, j) + 1) * sub, n_sub - 1), 0)),
                pl.BlockSpec(conv_w.shape, lambda d, b, j: (0, 0)),
                pl.BlockSpec((1, c), lambda d, b, j: (0, 0)),
                pl.BlockSpec((1,) + w_ri.shape[1:], lambda d, b, j: (d, 0, 0, 0)),
                pl.BlockSpec((1, 1, c), lambda d, b, j: (d, 0, 0)),
                pl.BlockSpec((1, 1, c), lambda d, b, j: (d, 0, 0)),
                pl.BlockSpec((1, 1, c), lambda d, b, j: (d, 0, 0)),
                pl.BlockSpec(h0.shape, lambda d, b, j: (0, 0))]
    if emit_y:
        out_specs = pl.BlockSpec((1, TOKEN_TILE, c), lambda d, b, j: (d, tile(d, b, j), 0))
        out_shape = jax.ShapeDtypeStruct((2, n, c), F32)
    else:
        out_specs = pl.BlockSpec(h0.shape, lambda d, b, j: (0, 0))
        out_shape = jax.ShapeDtypeStruct(h0.shape, F32)
    return pl.pallas_call(
        functools.partial(_lru_scan_kernel, n_batch=n_batch, n_tiles=n_tiles,
                          reset_first=reset_first, emit_y=emit_y),
        grid=(2, n_batch, n_tiles),
        in_specs=in_specs,
        out_specs=out_specs,
        out_shape=out_shape,
        scratch_shapes=[pltpu.VMEM((TOKEN_TILE + 2 * SUBLANES, c), F32),
                        pltpu.VMEM((TOKEN_TILE, c), F32),
                        pltpu.VMEM((TOKEN_TILE, c), F32),
                        pltpu.VMEM((1, c), F32)],
        compiler_params=_params(("arbitrary", "arbitrary", "arbitrary")),
        name="lru_scan" if emit_y else "lru_scan_ctx",
    )(u, u, u, conv_w, conv_b.reshape(1, c), w_ri, r_b.reshape(2, 1, c), i_b.reshape(2, 1, c),
      lam.reshape(2, 1, c), h0)


def _route(logits, rb):
    e, t = logits.shape
    neg = -jnp.inf
    scores = jax.nn.sigmoid(logits)
    sel = scores + rb
    iota_g = lax.broadcasted_iota(jnp.int32, (N_GROUPS, t), 0).astype(F32)
    iota_e = lax.broadcasted_iota(jnp.int32, (e, t), 0).astype(F32)

    gs = jnp.full((N_GROUPS, t), neg, F32)
    for g in range(N_GROUPS):
        sg = sel[g * EXPERTS_PER_GROUP:(g + 1) * EXPERTS_PER_GROUP, :]
        m1 = jnp.max(sg, axis=0, keepdims=True)
        i1 = jnp.min(jnp.where(sg == m1, iota_g, float(EXPERTS_PER_GROUP)), axis=0, keepdims=True)
        m2 = jnp.max(jnp.where(iota_g == i1, neg, sg), axis=0, keepdims=True)
        gs = jnp.where(iota_g == float(g), m1 + m2, gs)

    keep = jnp.zeros((N_GROUPS, t), F32)
    for _ in range(TOPK_GROUPS):
        m = jnp.max(gs, axis=0, keepdims=True)
        idx = jnp.min(jnp.where(gs == m, iota_g, float(N_GROUPS)), axis=0, keepdims=True)
        hit = iota_g == idx
        keep = jnp.where(hit, 1.0, keep)
        gs = jnp.where(hit, neg, gs)

    masked = jnp.concatenate(
        [jnp.where(keep[g:g + 1, :] > 0.0, sel[g * EXPERTS_PER_GROUP:(g + 1) * EXPERTS_PER_GROUP, :], neg)
         for g in range(N_GROUPS)], axis=0)

    iota_k = lax.broadcasted_iota(jnp.int32, (TOP_K, t), 0)
    selmask = jnp.zeros((e, t), F32)
    eidx = jnp.zeros((TOP_K, t), F32)
    gw = jnp.zeros((TOP_K, t), F32)
    for k in range(TOP_K):
        m = jnp.max(masked, axis=0, keepdims=True)
        idx = jnp.min(jnp.where(masked == m, iota_e, float(e)), axis=0, keepdims=True)
        hit = iota_e == idx
        gk = jnp.sum(jnp.where(hit, scores, 0.0), axis=0, keepdims=True)
        masked = jnp.where(hit, neg, masked)
        selmask = jnp.where(hit, 1.0, selmask)
        eidx = jnp.where(iota_k == k, idx, eidx)
        gw = jnp.where(iota_k == k, gk, gw)
    gw = gw / jnp.sum(gw, axis=0, keepdims=True) * ROUTED_SCALE
    return eidx, gw, selmask


def _ffn_pre(x1, mod_ref, row, g2_ref, rwt_ref, rb_ref, x1_ref, hx2_ref, eidx_ref, gw_ref, mask_ref, cnt_ref):
    d = x1.shape[1]
    hx2 = _rms_mod(x1, g2_ref[...], _mod_chunk(mod_ref, row, 4, d), _mod_chunk(mod_ref, row, 3, d))
    x1_ref[...] = x1
    hx2_ref[...] = hx2
    logits = lax.dot_general(rwt_ref[...], hx2, (((1,), (1,)), ((), ())),
                             preferred_element_type=F32, precision=HIGHEST)
    eidx, gw, selmask = _route(logits, rb_ref[...])
    eidx_ref[...] = eidx.astype(jnp.int32)
    gw_ref[...] = gw
    mask_ref[...] = selmask

    @pl.when(pl.program_id(0) == 0)
    def _():
        cnt_ref[...] = jnp.zeros_like(cnt_ref)

    cnt_ref[...] += jnp.broadcast_to(_round_up_rows(jnp.sum(selmask, axis=1, keepdims=True)), cnt_ref.shape)


def _ffn_pre_specs(n, d):
    row = lambda i: (i, 0)
    col = lambda i: (0, i)
    out_specs = [pl.BlockSpec((TOKEN_TILE, d), row),
                 pl.BlockSpec((TOKEN_TILE, d), row),
                 pl.BlockSpec((TOP_K, TOKEN_TILE), col),
                 pl.BlockSpec((TOP_K, TOKEN_TILE), col),
                 pl.BlockSpec((N_EXPERTS, TOKEN_TILE), col),
                 pl.BlockSpec((N_EXPERTS, 128), lambda i: (0, 0))]
    out_shape = [jax.ShapeDtypeStruct((n, d), F32),
                 jax.ShapeDtypeStruct((n, d), F32),
                 jax.ShapeDtypeStruct((TOP_K, n), jnp.int32),
                 jax.ShapeDtypeStruct((TOP_K, n), F32),
                 jax.ShapeDtypeStruct((N_EXPERTS, n), F32),
                 jax.ShapeDtypeStruct((N_EXPERTS, 128), F32)]
    return out_specs, out_shape


def _rglru_out_kernel(y_ref, gate_ref, x0_ref, mod_ref, wout_ref, g2_ref, rwt_ref, rb_ref,
                      x1_ref, hx2_ref, eidx_ref, gw_ref, mask_ref, cnt_ref, *, tiles_per_seq):
    d = x0_ref.shape[1]
    row = pl.program_id(0) // tiles_per_seq
    yx = y_ref[0] + y_ref[1]
    v = gate_ref[...].astype(F32) * yx
    out = jnp.dot(v.astype(BF16), wout_ref[...], preferred_element_type=F32)
    x1 = x0_ref[...] + _mod_chunk(mod_ref, row, 2, d) * out
    _ffn_pre(x1, mod_ref, row, g2_ref, rwt_ref, rb_ref, x1_ref, hx2_ref, eidx_ref, gw_ref, mask_ref, cnt_ref)


def _rglru_out(y, gate, x0, mod, w_out, g2, rwt, rb, tiles_per_seq):
    n, d = x0.shape
    c = gate.shape[1]
    row = lambda i: (i, 0)
    const = lambda i: (0, 0)
    out_specs, out_shape = _ffn_pre_specs(n, d)
    return pl.pallas_call(
        functools.partial(_rglru_out_kernel, tiles_per_seq=tiles_per_seq),
        grid=(n // TOKEN_TILE,),
        in_specs=[pl.BlockSpec((2, TOKEN_TILE, c), lambda i: (0, i, 0)),
                  pl.BlockSpec((TOKEN_TILE, c), row),
                  pl.BlockSpec((TOKEN_TILE, d), row),
                  pl.BlockSpec(mod.shape, const),
                  pl.BlockSpec(w_out.shape, const),
                  pl.BlockSpec((1, d), const),
                  pl.BlockSpec(rwt.shape, const),
                  pl.BlockSpec(rb.shape, const)],
        out_specs=out_specs,
        out_shape=out_shape,
        compiler_params=_params(("arbitrary",)),
        name="rglru_out",
    )(y, gate, x0, mod, w_out, g2, rwt, rb)


def _sgu_kernel(x_ref, mod_ref, g_ref, win_ref, lng_ref, lnb_ref, ws_ref, bst_ref, wout_ref,
                g2_ref, rwt_ref, rb_ref,
                x1_ref, hx2_ref, eidx_ref, gw_ref, mask_ref, cnt_ref, m_scr, *, tiles_per_seq):
    t_rows, d = x_ref.shape
    w = wout_ref.shape[0]
    gd = w // SGU_HEADS
    row = pl.program_id(0) // tiles_per_seq
    x = x_ref[...]
    hx = _rms_mod(x, g_ref[...], _mod_chunk(mod_ref, row, 1, d), _mod_chunk(mod_ref, row, 0, d))
    z = jax.nn.gelu(jnp.dot(hx.astype(BF16), win_ref[...], preferred_element_type=F32))
    u = z[:, :w]
    v = z[:, w:]
    mu = jnp.mean(v, axis=-1, keepdims=True)
    vc = v - mu
    v = vc * lax.rsqrt(jnp.mean(vc * vc, axis=-1, keepdims=True) + NORM_EPS) * lng_ref[...] + lnb_ref[...]
    vb = v.astype(BF16)
    for ch in range(t_rows // CHUNK):
        rs = slice(ch * CHUNK, (ch + 1) * CHUNK)
        for g in range(SGU_HEADS):
            cs = slice(g * gd, (g + 1) * gd)
            sv = jnp.dot(ws_ref[g], vb[rs, cs], preferred_element_type=F32) + bst_ref[:, g:g + 1]
            m_scr[rs, cs] = (u[rs, cs] * sv).astype(BF16)
    out = jnp.dot(m_scr[...], wout_ref[...], preferred_element_type=F32)
    x1 = x + _mod_chunk(mod_ref, row, 2, d) * out
    _ffn_pre(x1, mod_ref, row, g2_ref, rwt_ref, rb_ref, x1_ref, hx2_ref, eidx_ref, gw_ref, mask_ref, cnt_ref)


def _sgu(x, mod, g, w_in, ln_g, ln_b, w_s, b_st, w_out, g2, rwt, rb, tiles_per_seq):
    n, d = x.shape
    w = w_out.shape[0]
    const = lambda i: (0, 0)
    out_specs, out_shape = _ffn_pre_specs(n, d)
    return pl.pallas_call(
        functools.partial(_sgu_kernel, tiles_per_seq=tiles_per_seq),
        grid=(n // TOKEN_TILE,),
        in_specs=[pl.BlockSpec((TOKEN_TILE, d), lambda i: (i, 0)),
                  pl.BlockSpec(mod.shape, const),
                  pl.BlockSpec((1, d), const),
                  pl.BlockSpec(w_in.shape, const),
                  pl.BlockSpec((1, w), const),
                  pl.BlockSpec((1, w), const),
                  pl.BlockSpec(w_s.shape, lambda i: (0, 0, 0)),
                  pl.BlockSpec(b_st.shape, const),
                  pl.BlockSpec(w_out.shape, const),
                  pl.BlockSpec((1, d), const),
                  pl.BlockSpec(rwt.shape, const),
                  pl.BlockSpec(rb.shape, const)],
        out_specs=out_specs,
        out_shape=out_shape,
        scratch_shapes=[pltpu.VMEM((TOKEN_TILE, w), BF16)],
        compiler_params=_params(("arbitrary",)),
        name="sgu",
    )(x, mod, g, w_in, ln_g, ln_b, w_s, b_st, w_out, g2, rwt, rb)


def _rank_kernel(mask_ref, eidx_ref, pstart_ref, jpos_ref, tab_ref, carry):
    e, t = mask_ref.shape

    @pl.when(pl.program_id(0) == 0)
    def _():
        carry[...] = jnp.zeros_like(carry)

    m = mask_ref[...]
    r = lax.broadcasted_iota(jnp.int32, (t, t), 0)
    c = lax.broadcasted_iota(jnp.int32, (t, t), 1)
    upper = jnp.where(r <= c, 1.0, 0.0).astype(BF16)
    incl = jnp.dot(m.astype(BF16), upper, preferred_element_type=F32)
    run = _round_up_rows(incl[:, t - 1:t])
    lanes = tab_ref.shape[2]
    re = lax.broadcasted_iota(jnp.int32, (e, e), 0)
    ce = lax.broadcasted_iota(jnp.int32, (e, e), 1)
    lower = jnp.where(ce < re, 1.0, 0.0).astype(BF16)
    tiles = jnp.broadcast_to(run * (1.0 / SUBLANES), (e, lanes)).astype(BF16)
    lstart = jnp.dot(lower, tiles, preferred_element_type=F32)[:, 0:1] * float(SUBLANES)
    pos = lstart + incl - m
    iota_e = lax.broadcasted_iota(jnp.int32, (e, t), 0)
    iota_k = lax.broadcasted_iota(jnp.int32, (TOP_K, t), 0)
    eidx = eidx_ref[...]
    jpos = jnp.zeros((TOP_K, t), F32)
    for k in range(TOP_K):
        jk = jnp.sum(jnp.where(iota_e == eidx[k:k + 1, :], pos, 0.0), axis=0, keepdims=True)
        jpos = jnp.where(iota_k == k, jk, jpos)
    jpos_ref[...] = jpos.astype(jnp.int32)
    lane = lax.broadcasted_iota(jnp.int32, (e, lanes), 1)
    tab_ref[0] = jnp.where(lane == 0, run, jnp.where(lane == 1, pstart_ref[...] + carry[...], 0.0))
    carry[...] = carry[...] + run


def _rank(mask_t, eidx_t, pstart):
    e, n = mask_t.shape
    n_tiles = n // TOKEN_TILE
    col = lambda i: (0, i)
    return pl.pallas_call(
        _rank_kernel,
        grid=(n_tiles,),
        in_specs=[pl.BlockSpec((e, TOKEN_TILE), col),
                  pl.BlockSpec((TOP_K, TOKEN_TILE), col),
                  pl.BlockSpec((e, 1), lambda i: (0, 0))],
        out_specs=[pl.BlockSpec((TOP_K, TOKEN_TILE), col),
                   pl.BlockSpec((1, e, LANES), lambda i: (i, 0, 0))],
        out_shape=[jax.ShapeDtypeStruct((TOP_K, n), jnp.int32),
                   jax.ShapeDtypeStruct((n_tiles, e, LANES), F32)],
        scratch_shapes=[pltpu.VMEM((e, 1), F32)],
        compiler_params=_params(("arbitrary",)),
        name="moe_rank",
    )(mask_t, eidx_t, pstart)


def _aligned(v):
    return pl.multiple_of(v, SUBLANES)


def _run_copies(tab_ref, make_copy):
    def body(e, off):
        rows = _aligned(tab_ref[e])

        @pl.when(rows > 0)
        def _():
            make_copy(_aligned(off), _aligned(tab_ref[N_EXPERTS + e]), rows).start()

        return off + rows

    lax.fori_loop(0, N_EXPERTS, body, 0)


def _dispatch_kernel(tab_ref, ztab_ref, jpos_ref, x_ref, xs_hbm, sbuf, zbuf, sems, zsem):
    i = pl.program_id(0)
    slot = i % 2
    t = x_ref.shape[0]
    rows = sbuf.shape[1]

    def zero_copy(e):
        n = _aligned(ztab_ref[e])
        return pltpu.make_async_copy(zbuf.at[pl.ds(0, n)], xs_hbm.at[pl.ds(_aligned(ztab_ref[N_EXPERTS + e]), n)], zsem)

    def for_zero_runs(fn):
        def body(e, c):
            @pl.when(ztab_ref[e] > 0)
            def _():
                fn(zero_copy(e))
            return c
        lax.fori_loop(0, N_EXPERTS, body, 0)

        def tail(b, c):
            fn(pltpu.make_async_copy(zbuf, xs_hbm.at[pl.ds(pl.multiple_of(b * EXPERT_BLOCK, EXPERT_BLOCK),
                                                           EXPERT_BLOCK)], zsem))
            return c
        lax.fori_loop(ztab_ref[2 * N_EXPERTS], xs_hbm.shape[0] // EXPERT_BLOCK, tail, 0)

    @pl.when(i == 0)
    def _():
        zbuf[...] = jnp.zeros_like(zbuf)
        for_zero_runs(lambda cp: cp.start())

    jp = jpos_ref[...]
    xb = x_ref[...].astype(BF16)
    for r0 in range(0, rows, PERM_CHUNK):
        iota_j = lax.broadcasted_iota(jnp.int32, (PERM_CHUNK, t), 0) + r0
        p = jnp.zeros((PERM_CHUNK, t), F32)
        for k in range(TOP_K):
            p = jnp.where(iota_j == jp[k:k + 1, :], 1.0, p)
        sbuf[slot, r0:r0 + PERM_CHUNK, :] = jnp.dot(p.astype(BF16), xb, preferred_element_type=F32)

    _run_copies(tab_ref, lambda loc, glob, n: pltpu.make_async_copy(
        sbuf.at[slot, pl.ds(loc, n)], xs_hbm.at[pl.ds(glob, n)], sems.at[slot]))

    def wait_rows(s, n):
        pltpu.make_async_copy(sbuf.at[s, pl.ds(0, n)], xs_hbm.at[pl.ds(0, n)], sems.at[s]).wait()

    @pl.when(i > 0)
    def _():
        wait_rows(1 - slot, _aligned(tab_ref[2 * N_EXPERTS + 1]))

    @pl.when(i == pl.num_programs(0) - 1)
    def _():
        wait_rows(slot, _aligned(tab_ref[2 * N_EXPERTS]))

    @pl.when(i == 0)
    def _():
        for_zero_runs(lambda cp: cp.wait())


def _dispatch(tab, ztab, jpos_t, hx2, xs_rows):
    n, d = hx2.shape
    col = lambda i: (0, i)
    return pl.pallas_call(
        _dispatch_kernel,
        grid=(n // TOKEN_TILE,),
        in_specs=[pl.BlockSpec((TAB_WIDTH,), lambda i: (i,), memory_space=pltpu.SMEM),
                  pl.BlockSpec(memory_space=pltpu.SMEM),
                  pl.BlockSpec((TOP_K, TOKEN_TILE), col),
                  pl.BlockSpec((TOKEN_TILE, d), lambda i: (i, 0))],
        out_specs=pl.BlockSpec(memory_space=pl.ANY),
        out_shape=jax.ShapeDtypeStruct((xs_rows, d), F32),
        scratch_shapes=[pltpu.VMEM((2, SORTED_ROWS, d), F32),
                        pltpu.VMEM((EXPERT_BLOCK, d), F32),
                        pltpu.SemaphoreType.DMA((2,)),
                        pltpu.SemaphoreType.DMA],
        compiler_params=_params(("arbitrary",)),
        name="moe_dispatch",
    )(tab, ztab, jpos_t, hx2)


def _experts_kernel(be_ref, nb_ref, xs_ref, wgu_ref, wd_ref, ys_ref, wgu_b, wd_b):
    i = pl.program_id(0)
    ff = wd_ref.shape[1]
    used = i < nb_ref[0]
    new_expert = (i == 0) | (be_ref[i] != be_ref[jnp.maximum(i - 1, 0)])

    @pl.when(used & new_expert)
    def _():
        wgu_b[...] = wgu_ref[0].astype(BF16)
        wd_b[...] = wd_ref[0].astype(BF16)

    @pl.when(used)
    def _():
        h = jnp.dot(xs_ref[...].astype(BF16), wgu_b[...], preferred_element_type=F32)
        a = _silu(h[:, :ff]) * h[:, ff:]
        ys_ref[...] = jnp.dot(a.astype(BF16), wd_b[...], preferred_element_type=F32)

    @pl.when(jnp.logical_not(used))
    def _():
        ys_ref[...] = jnp.zeros_like(ys_ref)


def _experts(block_e, n_used, xs, w_gu, w_down):
    rows, d = xs.shape
    nb = rows // EXPERT_BLOCK
    blk = lambda i, be, nu: (jnp.maximum(jnp.minimum(i, nu[0] - 1), 0), 0)
    wmap = lambda i, be, nu: (be[i], 0, 0)
    return pl.pallas_call(
        _experts_kernel,
        grid_spec=pltpu.PrefetchScalarGridSpec(
            num_scalar_prefetch=2,
            grid=(nb,),
            in_specs=[pl.BlockSpec((EXPERT_BLOCK, d), blk),
                      pl.BlockSpec((1,) + w_gu.shape[1:], wmap),
                      pl.BlockSpec((1,) + w_down.shape[1:], wmap)],
            out_specs=pl.BlockSpec((EXPERT_BLOCK, d), lambda i, be, nu: (i, 0)),
            scratch_shapes=[pltpu.VMEM(w_gu.shape[1:], BF16), pltpu.VMEM(w_down.shape[1:], BF16)]),
        out_shape=jax.ShapeDtypeStruct((rows, d), F32),
        compiler_params=_params(("arbitrary",)),
        name="moe_experts",
    )(block_e, n_used, xs, w_gu, w_down)


def _combine_kernel(tab_ref, jpos_ref, gw_ref, ys_hbm, hx2_ref, x1_ref, mod_ref, wsgu_ref, wsd_ref, fg_ref,
                    out_ref, ybuf, sem, *, tiles_per_seq, final_norm):
    t_rows, d = x1_ref.shape
    ff = wsd_ref.shape[0]
    rows = ybuf.shape[0]
    row = pl.program_id(0) // tiles_per_seq

    @pl.when(pl.program_id(0) == 0)
    def _():
        ybuf[...] = jnp.zeros_like(ybuf)

    _run_copies(tab_ref, lambda loc, glob, n: pltpu.make_async_copy(
        ys_hbm.at[pl.ds(glob, n)], ybuf.at[pl.ds(loc, n)], sem))

    hs = jnp.dot(hx2_ref[...].astype(BF16), wsgu_ref[...], preferred_element_type=F32)
    shared = jnp.dot((_silu(hs[:, :ff]) * hs[:, ff:]).astype(BF16), wsd_ref[...], preferred_element_type=F32)

    jp = jpos_ref[...]
    gw = gw_ref[...]
    iota_j = lax.broadcasted_iota(jnp.int32, (t_rows, rows), 1)
    g = jnp.zeros((t_rows, rows), F32)
    for k in range(TOP_K):
        g = jnp.where(iota_j == jp[:, k:k + 1], gw[:, k:k + 1], g)

    total = _aligned(tab_ref[2 * N_EXPERTS])
    pltpu.make_async_copy(ys_hbm.at[pl.ds(0, total)], ybuf.at[pl.ds(0, total)], sem).wait()
    routed = jnp.dot(g.astype(BF16), ybuf[...].astype(BF16), preferred_element_type=F32)
    x2 = x1_ref[...] + _mod_chunk(mod_ref, row, 5, d) * (routed + shared)
    if final_norm:
        x2 = x2 * lax.rsqrt(jnp.mean(x2 * x2, axis=-1, keepdims=True) + NORM_EPS) * fg_ref[...]
    out_ref[...] = x2


def _combine(tab, jpos, gw, ys, hx2, x1, mod, ws_gu, ws_down, fg, tiles_per_seq, final_norm):
    n, d = x1.shape
    row = lambda i: (i, 0)
    const = lambda i: (0, 0)
    return pl.pallas_call(
        functools.partial(_combine_kernel, tiles_per_seq=tiles_per_seq, final_norm=final_norm),
        grid=(n // TOKEN_TILE,),
        in_specs=[pl.BlockSpec((TAB_WIDTH,), lambda i: (i,), memory_space=pltpu.SMEM),
                  pl.BlockSpec((TOKEN_TILE, TOP_K), row),
                  pl.BlockSpec((TOKEN_TILE, TOP_K), row),
                  pl.BlockSpec(memory_space=pl.ANY),
                  pl.BlockSpec((TOKEN_TILE, d), row),
                  pl.BlockSpec((TOKEN_TILE, d), row),
                  pl.BlockSpec(mod.shape, const),
                  pl.BlockSpec(ws_gu.shape, const),
                  pl.BlockSpec(ws_down.shape, const),
                  pl.BlockSpec((1, d), const)],
        out_specs=pl.BlockSpec((TOKEN_TILE, d), row),
        out_shape=jax.ShapeDtypeStruct((n, d), F32),
        scratch_shapes=[pltpu.VMEM((SORTED_ROWS, d), F32), pltpu.SemaphoreType.DMA],
        compiler_params=_params(("arbitrary",)),
        name="moe_combine",
    )(tab, jpos, gw, ys, hx2, x1, mod, ws_gu, ws_down, fg)


def _moe(x1, hx2, eidx_t, gw_t, mask_t, cnt, mod, w_gu, w_down, ws_gu, ws_down, fg, tiles_per_seq, final_norm):
    n, d = x1.shape
    n_tiles = n // TOKEN_TILE
    counts = cnt[:, 0].astype(jnp.int32)
    padded = (counts + EXPERT_BLOCK - 1) // EXPERT_BLOCK * EXPERT_BLOCK
    pad_end = jnp.cumsum(padded)
    pad_start = pad_end - padded
    max_rows = n * TOP_K + n_tiles * N_EXPERTS * (SUBLANES - 1) + N_EXPERTS * (EXPERT_BLOCK - SUBLANES)
    n_blocks = (max_rows + EXPERT_BLOCK - 1) // EXPERT_BLOCK
    n_used = (pad_end[-1:] // EXPERT_BLOCK).astype(jnp.int32)
    block_start = jnp.arange(n_blocks, dtype=jnp.int32) * EXPERT_BLOCK
    block_e = jnp.minimum(jnp.sum((pad_end[None, :] <= block_start[:, None]).astype(jnp.int32), axis=1),
                          N_EXPERTS - 1)

    jpos_t, tab_f = _rank(mask_t, eidx_t, pad_start.astype(F32).reshape(N_EXPERTS, 1))
    run_len = tab_f[:, :, 0].astype(jnp.int32)
    run_start = tab_f[:, :, 1].astype(jnp.int32)
    total = jnp.sum(run_len, axis=1, keepdims=True)
    prev_total = jnp.concatenate([jnp.zeros((1, 1), jnp.int32), total[:-1]], axis=0)
    fill = jnp.zeros((n_tiles, TAB_WIDTH - 2 * N_EXPERTS - 2), jnp.int32)
    tab = jnp.concatenate([run_len, run_start, total, prev_total, fill], axis=1).reshape(-1)
    ztab = jnp.concatenate([padded - counts, pad_start + counts, n_used])

    xs = _dispatch(tab, ztab, jpos_t, hx2, n_blocks * EXPERT_BLOCK)
    ys = _experts(block_e, n_used, xs, w_gu, w_down)
    return _combine(tab, jpos_t.T, gw_t.T, ys, hx2, x1, mod, ws_gu, ws_down, fg, tiles_per_seq, final_norm)


def _sincos_2d(rows, d):
    quarter = d // 4
    omega = 1.0 / (POS_BASE ** (jnp.arange(quarter, dtype=F32) / quarter))

    def emb(n):
        p = jnp.arange(n, dtype=F32)[:, None] * omega[None, :]
        return jnp.concatenate([jnp.sin(p), jnp.cos(p)], axis=-1)

    er, ec = emb(rows), emb(GRID_W)
    pe = jnp.concatenate([jnp.broadcast_to(er[:, None, :], (rows, GRID_W, d // 2)),
                          jnp.broadcast_to(ec[None, :, :], (rows, GRID_W, d // 2))], axis=-1)
    return pe.reshape(rows * GRID_W, d)


def kernel(x, c, ctx, c_ctx, ada_w, ada_b, mix_norm_g, ffn_norm_g, a_w_in, a_conv_w, a_conv_b, a_gate_r_w, a_gate_r_b, a_gate_i_w, a_gate_i_b, a_lambda, a_w_out, b_w_in, b_ln_g, b_ln_b, b_w_s, b_b_s, b_w_out, router_w, router_b, moe_w_gu, moe_w_down, shared_w_gu, shared_w_down, final_norm_g):
    bsz, s, d = x.shape
    ctx_len = ctx.shape[1]
    depth = ada_w.shape[0]
    assert depth == 2 and bsz < MOD_ROWS and s % TOKEN_TILE == 0 and ctx_len % TOKEN_TILE == 0
    n = bsz * s
    tps = s // TOKEN_TILE
    ctx_row = bsz

    cc = jnp.zeros((MOD_ROWS, d), F32).at[:bsz].set(c).at[ctx_row].set(c_ctx)
    mod = _modulation(cc, ada_w, ada_b)
    pe = _sincos_2d(s // GRID_W, d)
    rc = a_w_in.shape[2] // 2

    w_in0 = a_w_in[0].astype(BF16)
    g_mix0 = mix_norm_g[0].reshape(1, d)
    x0, gate, ux = _rglru_in(x.reshape(n, d), pe, mod[0], g_mix0, w_in0)
    uc = _ctx_in(ctx.reshape(bsz * ctx_len, d), mod[0], g_mix0, w_in0[:, rc:], ctx_row)
    w_ri = jnp.concatenate([a_gate_r_w[0], a_gate_i_w[0]], axis=-1).astype(BF16)
    scan_args = (a_conv_w[0], a_conv_b[0], w_ri, a_gate_r_b[0], a_gate_i_b[0], a_lambda[0])
    h_ctx = _lru_scan(uc, *scan_args, jnp.zeros((2 * bsz, rc), F32), n_batch=bsz, reset_first=True, emit_y=False)
    y = _lru_scan(ux, *scan_args, h_ctx, n_batch=bsz, reset_first=False, emit_y=True)
    pre = _rglru_out(y, gate, x0, mod[0], a_w_out[0].astype(BF16), ffn_norm_g[0].reshape(1, d),
                     router_w[0].T, router_b[0].reshape(N_EXPERTS, 1), tps)
    x1 = _moe(*pre, mod[0], moe_w_gu[0], moe_w_down[0], shared_w_gu[0].astype(BF16),
              shared_w_down[0].astype(BF16), final_norm_g.reshape(1, d), tps, False)

    pre = _sgu(x1, mod[1], mix_norm_g[1].reshape(1, d), b_w_in[0].astype(BF16),
               b_ln_g[0].reshape(1, -1), b_ln_b[0].reshape(1, -1), b_w_s[0].astype(BF16), b_b_s[0].T,
               b_w_out[0].astype(BF16), ffn_norm_g[1].reshape(1, d),
               router_w[1].T, router_b[1].reshape(N_EXPERTS, 1), tps)
    out = _moe(*pre, mod[1], moe_w_gu[1], moe_w_down[1], shared_w_gu[1].astype(BF16),
               shared_w_down[1].astype(BF16), final_norm_g.reshape(1, d), tps, True)
    return out.reshape(bsz, s, d)
```

```python
import functools

import jax
import jax.numpy as jnp
from jax import lax
from jax.experimental import pallas as pl
from jax.experimental.pallas import tpu as pltpu

F32 = jnp.float32
BF16 = jnp.bfloat16
HIGHEST = lax.Precision.HIGHEST

GRID_W = 64
N_MOD = 6
NORM_EPS = 1e-6
POS_BASE = 10000.0
RNN_HEADS = 5
CONV_WIDTH = 4
CONV_PAD_LEFT = 2
LRU_C = 8.0
SGU_HEADS = 8
CHUNK = 128
N_EXPERTS = 64
TOP_K = 8
N_GROUPS = 8
TOPK_GROUPS = 4
EXPERTS_PER_GROUP = N_EXPERTS // N_GROUPS
ROUTED_SCALE = 2.5

SUBLANES = 8
ROW_ALIGN = 16
LANES = 128
MOD_ROWS = 8
TOKEN_TILE = 256
EXPERT_BLOCK = 512
SORTED_ROWS = TOKEN_TILE * TOP_K + N_EXPERTS * ROW_ALIGN
PERM_CHUNK = 512
TAB_WIDTH = 256
VMEM_LIMIT = 56 * 1024 * 1024


def _params(semantics, vmem=VMEM_LIMIT):
    return pltpu.CompilerParams(dimension_semantics=semantics, vmem_limit_bytes=vmem)


def _silu(x):
    return x * jax.nn.sigmoid(x)


def _rms_mod(x, g, sc, sh):
    y = x * lax.rsqrt(jnp.mean(x * x, axis=-1, keepdims=True) + NORM_EPS)
    return (y * g) * (1.0 + sc) + sh


def _mod_chunk(mod_ref, row, k, d):
    return mod_ref[pl.ds(row, 1), k * d:(k + 1) * d]


def _round_up_rows(count):
    return jnp.ceil(count * (1.0 / ROW_ALIGN)) * float(ROW_ALIGN)


def _mod_kernel(cc_ref, w_ref, b_ref, o_ref):
    s = _silu(cc_ref[...])
    o_ref[0] = jnp.dot(s, w_ref[0], preferred_element_type=F32, precision=HIGHEST) + b_ref[0]


def _modulation(cc, ada_w, ada_b):
    depth, d, nd = ada_w.shape
    return pl.pallas_call(
        _mod_kernel,
        grid=(depth, nd // d),
        in_specs=[pl.BlockSpec((MOD_ROWS, d), lambda l, j: (0, 0)),
                  pl.BlockSpec((1, d, d), lambda l, j: (l, 0, j)),
                  pl.BlockSpec((1, 1, d), lambda l, j: (l, 0, j))],
        out_specs=pl.BlockSpec((1, MOD_ROWS, d), lambda l, j: (l, 0, j)),
        out_shape=jax.ShapeDtypeStruct((depth, MOD_ROWS, nd), F32),
        compiler_params=_params(("arbitrary", "arbitrary")),
        name="modulation",
    )(cc, ada_w, ada_b.reshape(depth, 1, nd))


def _rglru_in_kernel(x_ref, pe_ref, mod_ref, g_ref, w_ref, x0_ref, gate_ref, u_ref, *, tiles_per_seq):
    d = x_ref.shape[1]
    c = u_ref.shape[1]
    row = pl.program_id(0) // tiles_per_seq
    x = x_ref[...] + pe_ref[...]
    hx = _rms_mod(x, g_ref[...], _mod_chunk(mod_ref, row, 1, d), _mod_chunk(mod_ref, row, 0, d))
    z = jnp.dot(hx.astype(BF16), w_ref[...], preferred_element_type=F32)
    x0_ref[...] = x
    gate_ref[...] = jax.nn.gelu(z[:, :c]).astype(BF16)
    u_ref[...] = z[:, c:]


def _rglru_in(x2, pe, mod, g, w_in):
    n, d = x2.shape
    c = w_in.shape[1] // 2
    s = pe.shape[0]
    tps = s // TOKEN_TILE
    row = lambda i: (i, 0)
    return pl.pallas_call(
        functools.partial(_rglru_in_kernel, tiles_per_seq=tps),
        grid=(n // TOKEN_TILE,),
        in_specs=[pl.BlockSpec((TOKEN_TILE, d), row),
                  pl.BlockSpec((TOKEN_TILE, d), lambda i: (i % tps, 0)),
                  pl.BlockSpec(mod.shape, lambda i: (0, 0)),
                  pl.BlockSpec((1, d), lambda i: (0, 0)),
                  pl.BlockSpec(w_in.shape, lambda i: (0, 0))],
        out_specs=[pl.BlockSpec((TOKEN_TILE, d), row),
                   pl.BlockSpec((TOKEN_TILE, c), row),
                   pl.BlockSpec((TOKEN_TILE, c), row)],
        out_shape=[jax.ShapeDtypeStruct((n, d), F32),
                   jax.ShapeDtypeStruct((n, c), BF16),
                   jax.ShapeDtypeStruct((n, c), F32)],
        compiler_params=_params(("arbitrary",)),
        name="rglru_in",
    )(x2, pe, mod, g, w_in)


def _ctx_in_kernel(x_ref, mod_ref, g_ref, w_ref, u_ref, *, ctx_row):
    d = x_ref.shape[1]
    hx = _rms_mod(x_ref[...], g_ref[...], _mod_chunk(mod_ref, ctx_row, 1, d),
                  _mod_chunk(mod_ref, ctx_row, 0, d))
    u_ref[...] = jnp.dot(hx.astype(BF16), w_ref[...], preferred_element_type=F32)


def _ctx_in(c2, mod, g, w_u, ctx_row):
    n, d = c2.shape
    c = w_u.shape[1]
    return pl.pallas_call(
        functools.partial(_ctx_in_kernel, ctx_row=ctx_row),
        grid=(n // TOKEN_TILE,),
        in_specs=[pl.BlockSpec((TOKEN_TILE, d), lambda i: (i, 0)),
                  pl.BlockSpec(mod.shape, lambda i: (0, 0)),
                  pl.BlockSpec((1, d), lambda i: (0, 0)),
                  pl.BlockSpec(w_u.shape, lambda i: (0, 0))],
        out_specs=pl.BlockSpec((TOKEN_TILE, c), lambda i: (i, 0)),
        out_shape=jax.ShapeDtypeStruct((n, c), F32),
        compiler_params=_params(("arbitrary",)),
        name="ctx_in",
    )(c2, mod, g, w_u)


def _log_sigmoid(x):
    return jnp.minimum(x, 0.0) - jnp.log1p(jnp.exp(-jnp.abs(x)))


def _lru_scan_kernel(u_ref, up_ref, un_ref, cw_ref, cb_ref, wri_ref, rb_ref, ib_ref, lam_ref, h0_ref,
                     out_ref, ubuf, a_scr, b_scr, h_scr, *, n_batch, n_tiles, reset_first, emit_y):
    t_rows, c = u_ref.shape
    hb = c // RNN_HEADS
    d = pl.program_id(0)
    b = pl.program_id(1)
    j = pl.program_id(2)
    jj = j + d * (n_tiles - 1 - 2 * j)

    ubuf[SUBLANES:SUBLANES + t_rows, :] = u_ref[...]
    ubuf[0:SUBLANES, :] = jnp.where(jj == 0, 0.0, up_ref[...])
    ubuf[SUBLANES + t_rows:, :] = jnp.where(jj == n_tiles - 1, 0.0, un_ref[...])
    u = cb_ref[...]
    for k in range(CONV_WIDTH):
        u = u + cw_ref[k:k + 1, :] * ubuf[pl.ds(SUBLANES - CONV_PAD_LEFT + k, t_rows), :]

    log_lam = LRU_C * _log_sigmoid(lam_ref[0])
    rows = lax.broadcasted_iota(jnp.int32, (t_rows, 1), 0)
    first_row = jnp.where(j == 0, jnp.where(d == 0, 0, t_rows - 1), -1)
    for h in range(RNN_HEADS):
        sl = slice(h * hb, (h + 1) * hb)
        uh = u[:, sl]
        z = jnp.dot(uh.astype(BF16), wri_ref[0, h], preferred_element_type=F32)
        r = jax.nn.sigmoid(z[:, :hb] + rb_ref[0][:, sl])
        ig = jax.nn.sigmoid(z[:, hb:] + ib_ref[0][:, sl])
        log_a = r * log_lam[:, sl]
        a = jnp.exp(log_a)
        mult = jnp.sqrt((1.0 - a) * (1.0 + a))
        if reset_first:
            mult = jnp.where(rows == first_row, 1.0, mult)
        a_scr[:, sl] = a
        b_scr[:, sl] = mult * ig * uh

    @pl.when(j == 0)
    def _():
        h_scr[...] = h0_ref[pl.ds(d * n_batch + b, 1), :]

    def step(t, h):
        tt = t + d * (t_rows - 1 - 2 * t)
        h = a_scr[pl.ds(tt, 1), :] * h + b_scr[pl.ds(tt, 1), :]
        if emit_y:
            out_ref[0, pl.ds(tt, 1), :] = h
        return h

    h = lax.fori_loop(0, t_rows, step, h_scr[...], unroll=8)
    h_scr[...] = h
    if not emit_y:
        @pl.when(j == n_tiles - 1)
        def _():
            out_ref[pl.ds(d * n_batch + b, 1), :] = h


def _lru_scan(u, conv_w, conv_b, w_ri, r_b, i_b, lam, h0, *, n_batch, reset_first, emit_y):
    n, c = u.shape
    n_tiles = n // n_batch // TOKEN_TILE
    sub = TOKEN_TILE // SUBLANES
    n_sub = n // SUBLANES

    def tile(d, b, j):
        return b * n_tiles + j + d * (n_tiles - 1 - 2 * j)

    in_specs = [pl.BlockSpec((TOKEN_TILE, c), lambda d, b, j: (tile(d, b, j), 0)),
                pl.BlockSpec((SUBLANES, c), lambda d, b, j: (jnp.maximum(tile(d, b, j) * sub - 1, 0), 0)),
                pl.BlockSpec((SUBLANES, c), lambda d, b, j: (jnp.minimum((tile(d, b, j) + 1) * sub, n_sub - 1), 0)),
                pl.BlockSpec(conv_w.shape, lambda d, b, j: (0, 0)),
                pl.BlockSpec((1, c), lambda d, b, j: (0, 0)),
                pl.BlockSpec((1,) + w_ri.shape[1:], lambda d, b, j: (d, 0, 0, 0)),
                pl.BlockSpec((1, 1, c), lambda d, b, j: (d, 0, 0)),
                pl.BlockSpec((1, 1, c), lambda d, b, j: (d, 0, 0)),
                pl.BlockSpec((1, 1, c), lambda d, b, j: (d, 0, 0)),
                pl.BlockSpec(h0.shape, lambda d, b, j: (0, 0))]
    if emit_y:
        out_specs = pl.BlockSpec((1, TOKEN_TILE, c), lambda d, b, j: (d, tile(d, b, j), 0))
        out_shape = jax.ShapeDtypeStruct((2, n, c), F32)
    else:
        out_specs = pl.BlockSpec(h0.shape, lambda d, b, j: (0, 0))
        out_shape = jax.ShapeDtypeStruct(h0.shape, F32)
    return pl.pallas_call(
        functools.partial(_lru_scan_kernel, n_batch=n_batch, n_tiles=n_tiles,
                          reset_first=reset_first, emit_y=emit_y),
        grid=(2, n_batch, n_tiles),
        in_specs=in_specs,
        out_specs=out_specs,
        out_shape=out_shape,
        scratch_shapes=[pltpu.VMEM((TOKEN_TILE + 2 * SUBLANES, c), F32),
                        pltpu.VMEM((TOKEN_TILE, c), F32),
                        pltpu.VMEM((TOKEN_TILE, c), F32),
                        pltpu.VMEM((1, c), F32)],
        compiler_params=_params(("arbitrary", "arbitrary", "arbitrary")),
        name="lru_scan" if emit_y else "lru_scan_ctx",
    )(u, u, u, conv_w, conv_b.reshape(1, c), w_ri, r_b.reshape(2, 1, c), i_b.reshape(2, 1, c),
      lam.reshape(2, 1, c), h0)


def _route(logits, rb):
    e, t = logits.shape
    neg = -jnp.inf
    scores = jax.nn.sigmoid(logits)
    sel = scores + rb
    iota_g = lax.broadcasted_iota(jnp.int32, (N_GROUPS, t), 0).astype(F32)
    iota_e = lax.broadcasted_iota(jnp.int32, (e, t), 0).astype(F32)

    gs = jnp.full((N_GROUPS, t), neg, F32)
    for g in range(N_GROUPS):
        sg = sel[g * EXPERTS_PER_GROUP:(g + 1) * EXPERTS_PER_GROUP, :]
        m1 = jnp.max(sg, axis=0, keepdims=True)
        i1 = jnp.min(jnp.where(sg == m1, iota_g, float(EXPERTS_PER_GROUP)), axis=0, keepdims=True)
        m2 = jnp.max(jnp.where(iota_g == i1, neg, sg), axis=0, keepdims=True)
        gs = jnp.where(iota_g == float(g), m1 + m2, gs)

    keep = jnp.zeros((N_GROUPS, t), F32)
    for _ in range(TOPK_GROUPS):
        m = jnp.max(gs, axis=0, keepdims=True)
        idx = jnp.min(jnp.where(gs == m, iota_g, float(N_GROUPS)), axis=0, keepdims=True)
        hit = iota_g == idx
        keep = jnp.where(hit, 1.0, keep)
        gs = jnp.where(hit, neg, gs)

    masked = jnp.concatenate(
        [jnp.where(keep[g:g + 1, :] > 0.0, sel[g * EXPERTS_PER_GROUP:(g + 1) * EXPERTS_PER_GROUP, :], neg)
         for g in range(N_GROUPS)], axis=0)

    iota_k = lax.broadcasted_iota(jnp.int32, (TOP_K, t), 0)
    selmask = jnp.zeros((e, t), F32)
    eidx = jnp.zeros((TOP_K, t), F32)
    gw = jnp.zeros((TOP_K, t), F32)
    for k in range(TOP_K):
        m = jnp.max(masked, axis=0, keepdims=True)
        idx = jnp.min(jnp.where(masked == m, iota_e, float(e)), axis=0, keepdims=True)
        hit = iota_e == idx
        gk = jnp.sum(jnp.where(hit, scores, 0.0), axis=0, keepdims=True)
        masked = jnp.where(hit, neg, masked)
        selmask = jnp.where(hit, 1.0, selmask)
        eidx = jnp.where(iota_k == k, idx, eidx)
        gw = jnp.where(iota_k == k, gk, gw)
    gw = gw / jnp.sum(gw, axis=0, keepdims=True) * ROUTED_SCALE
    return eidx, gw, selmask


def _ffn_pre(x1, mod_ref, row, g2_ref, rwt_ref, rb_ref, x1_ref, hx2_ref, eidx_ref, gw_ref, mask_ref, cnt_ref):
    d = x1.shape[1]
    hx2 = _rms_mod(x1, g2_ref[...], _mod_chunk(mod_ref, row, 4, d), _mod_chunk(mod_ref, row, 3, d))
    x1_ref[...] = x1
    hx2_ref[...] = hx2
    logits = lax.dot_general(rwt_ref[...], hx2, (((1,), (1,)), ((), ())),
                             preferred_element_type=F32, precision=HIGHEST)
    eidx, gw, selmask = _route(logits, rb_ref[...])
    eidx_ref[...] = eidx.astype(jnp.int32)
    gw_ref[...] = gw
    mask_ref[...] = selmask

    @pl.when(pl.program_id(0) == 0)
    def _():
        cnt_ref[...] = jnp.zeros_like(cnt_ref)

    cnt_ref[...] += jnp.broadcast_to(_round_up_rows(jnp.sum(selmask, axis=1, keepdims=True)), cnt_ref.shape)


def _ffn_pre_specs(n, d):
    row = lambda i: (i, 0)
    col = lambda i: (0, i)
    out_specs = [pl.BlockSpec((TOKEN_TILE, d), row),
                 pl.BlockSpec((TOKEN_TILE, d), row),
                 pl.BlockSpec((TOP_K, TOKEN_TILE), col),
                 pl.BlockSpec((TOP_K, TOKEN_TILE), col),
                 pl.BlockSpec((N_EXPERTS, TOKEN_TILE), col),
                 pl.BlockSpec((N_EXPERTS, 128), lambda i: (0, 0))]
    out_shape = [jax.ShapeDtypeStruct((n, d), F32),
                 jax.ShapeDtypeStruct((n, d), F32),
                 jax.ShapeDtypeStruct((TOP_K, n), jnp.int32),
                 jax.ShapeDtypeStruct((TOP_K, n), F32),
                 jax.ShapeDtypeStruct((N_EXPERTS, n), F32),
                 jax.ShapeDtypeStruct((N_EXPERTS, 128), F32)]
    return out_specs, out_shape


def _rglru_out_kernel(y_ref, gate_ref, x0_ref, mod_ref, wout_ref, g2_ref, rwt_ref, rb_ref,
                      x1_ref, hx2_ref, eidx_ref, gw_ref, mask_ref, cnt_ref, *, tiles_per_seq):
    d = x0_ref.shape[1]
    row = pl.program_id(0) // tiles_per_seq
    yx = y_ref[0] + y_ref[1]
    v = gate_ref[...].astype(F32) * yx
    out = jnp.dot(v.astype(BF16), wout_ref[...], preferred_element_type=F32)
    x1 = x0_ref[...] + _mod_chunk(mod_ref, row, 2, d) * out
    _ffn_pre(x1, mod_ref, row, g2_ref, rwt_ref, rb_ref, x1_ref, hx2_ref, eidx_ref, gw_ref, mask_ref, cnt_ref)


def _rglru_out(y, gate, x0, mod, w_out, g2, rwt, rb, tiles_per_seq):
    n, d = x0.shape
    c = gate.shape[1]
    row = lambda i: (i, 0)
    const = lambda i: (0, 0)
    out_specs, out_shape = _ffn_pre_specs(n, d)
    return pl.pallas_call(
        functools.partial(_rglru_out_kernel, tiles_per_seq=tiles_per_seq),
        grid=(n // TOKEN_TILE,),
        in_specs=[pl.BlockSpec((2, TOKEN_TILE, c), lambda i: (0, i, 0)),
                  pl.BlockSpec((TOKEN_TILE, c), row),
                  pl.BlockSpec((TOKEN_TILE, d), row),
                  pl.BlockSpec(mod.shape, const),
                  pl.BlockSpec(w_out.shape, const),
                  pl.BlockSpec((1, d), const),
                  pl.BlockSpec(rwt.shape, const),
                  pl.BlockSpec(rb.shape, const)],
        out_specs=out_specs,
        out_shape=out_shape,
        compiler_params=_params(("arbitrary",)),
        name="rglru_out",
    )(y, gate, x0, mod, w_out, g2, rwt, rb)


def _sgu_kernel(x_ref, mod_ref, g_ref, win_ref, lng_ref, lnb_ref, ws_ref, bst_ref, wout_ref,
                g2_ref, rwt_ref, rb_ref,
                x1_ref, hx2_ref, eidx_ref, gw_ref, mask_ref, cnt_ref, m_scr, *, tiles_per_seq):
    t_rows, d = x_ref.shape
    w = wout_ref.shape[0]
    gd = w // SGU_HEADS
    row = pl.program_id(0) // tiles_per_seq
    x = x_ref[...]
    hx = _rms_mod(x, g_ref[...], _mod_chunk(mod_ref, row, 1, d), _mod_chunk(mod_ref, row, 0, d))
    z = jax.nn.gelu(jnp.dot(hx.astype(BF16), win_ref[...], preferred_element_type=F32))
    u = z[:, :w]
    v = z[:, w:]
    mu = jnp.mean(v, axis=-1, keepdims=True)
    vc = v - mu
    v = vc * lax.rsqrt(jnp.mean(vc * vc, axis=-1, keepdims=True) + NORM_EPS) * lng_ref[...] + lnb_ref[...]
    vb = v.astype(BF16)
    for ch in range(t_rows // CHUNK):
        rs = slice(ch * CHUNK, (ch + 1) * CHUNK)
        for g in range(SGU_HEADS):
            cs = slice(g * gd, (g + 1) * gd)
            sv = jnp.dot(ws_ref[g], vb[rs, cs], preferred_element_type=F32) + bst_ref[:, g:g + 1]
            m_scr[rs, cs] = (u[rs, cs] * sv).astype(BF16)
    out = jnp.dot(m_scr[...], wout_ref[...], preferred_element_type=F32)
    x1 = x + _mod_chunk(mod_ref, row, 2, d) * out
    _ffn_pre(x1, mod_ref, row, g2_ref, rwt_ref, rb_ref, x1_ref, hx2_ref, eidx_ref, gw_ref, mask_ref, cnt_ref)


def _sgu(x, mod, g, w_in, ln_g, ln_b, w_s, b_st, w_out, g2, rwt, rb, tiles_per_seq):
    n, d = x.shape
    w = w_out.shape[0]
    const = lambda i: (0, 0)
    out_specs, out_shape = _ffn_pre_specs(n, d)
    return pl.pallas_call(
        functools.partial(_sgu_kernel, tiles_per_seq=tiles_per_seq),
        grid=(n // TOKEN_TILE,),
        in_specs=[pl.BlockSpec((TOKEN_TILE, d), lambda i: (i, 0)),
                  pl.BlockSpec(mod.shape, const),
                  pl.BlockSpec((1, d), const),
                  pl.BlockSpec(w_in.shape, const),
                  pl.BlockSpec((1, w), const),
                  pl.BlockSpec((1, w), const),
                  pl.BlockSpec(w_s.shape, lambda i: (0, 0, 0)),
                  pl.BlockSpec(b_st.shape, const),
                  pl.BlockSpec(w_out.shape, const),
                  pl.BlockSpec((1, d), const),
                  pl.BlockSpec(rwt.shape, const),
                  pl.BlockSpec(rb.shape, const)],
        out_specs=out_specs,
        out_shape=out_shape,
        scratch_shapes=[pltpu.VMEM((TOKEN_TILE, w), BF16)],
        compiler_params=_params(("arbitrary",)),
        name="sgu",
    )(x, mod, g, w_in, ln_g, ln_b, w_s, b_st, w_out, g2, rwt, rb)


def _rank_kernel(mask_ref, eidx_ref, pstart_ref, jpos_ref, tab_ref, carry):
    e, t = mask_ref.shape

    @pl.when(pl.program_id(0) == 0)
    def _():
        carry[...] = jnp.zeros_like(carry)

    m = mask_ref[...]
    r = lax.broadcasted_iota(jnp.int32, (t, t), 0)
    c = lax.broadcasted_iota(jnp.int32, (t, t), 1)
    upper = jnp.where(r <= c, 1.0, 0.0).astype(BF16)
    incl = jnp.dot(m.astype(BF16), upper, preferred_element_type=F32)
    run = _round_up_rows(incl[:, t - 1:t])
    lanes = tab_ref.shape[2]
    re = lax.broadcasted_iota(jnp.int32, (e, e), 0)
    ce = lax.broadcasted_iota(jnp.int32, (e, e), 1)
    lower = jnp.where(ce < re, 1.0, 0.0).astype(BF16)
    tiles = jnp.broadcast_to(run * (1.0 / ROW_ALIGN), (e, lanes)).astype(BF16)
    lstart = jnp.dot(lower, tiles, preferred_element_type=F32)[:, 0:1] * float(ROW_ALIGN)
    pos = lstart + incl - m
    iota_e = lax.broadcasted_iota(jnp.int32, (e, t), 0)
    iota_k = lax.broadcasted_iota(jnp.int32, (TOP_K, t), 0)
    eidx = eidx_ref[...]
    jpos = jnp.zeros((TOP_K, t), F32)
    for k in range(TOP_K):
        jk = jnp.sum(jnp.where(iota_e == eidx[k:k + 1, :], pos, 0.0), axis=0, keepdims=True)
        jpos = jnp.where(iota_k == k, jk, jpos)
    jpos_ref[...] = jpos.astype(jnp.int32)
    lane = lax.broadcasted_iota(jnp.int32, (e, lanes), 1)
    tab_ref[0] = jnp.where(lane == 0, run, jnp.where(lane == 1, pstart_ref[...] + carry[...], 0.0))
    carry[...] = carry[...] + run


def _rank(mask_t, eidx_t, pstart):
    e, n = mask_t.shape
    n_tiles = n // TOKEN_TILE
    col = lambda i: (0, i)
    return pl.pallas_call(
        _rank_kernel,
        grid=(n_tiles,),
        in_specs=[pl.BlockSpec((e, TOKEN_TILE), col),
                  pl.BlockSpec((TOP_K, TOKEN_TILE), col),
                  pl.BlockSpec((e, 1), lambda i: (0, 0))],
        out_specs=[pl.BlockSpec((TOP_K, TOKEN_TILE), col),
                   pl.BlockSpec((1, e, LANES), lambda i: (i, 0, 0))],
        out_shape=[jax.ShapeDtypeStruct((TOP_K, n), jnp.int32),
                   jax.ShapeDtypeStruct((n_tiles, e, LANES), F32)],
        scratch_shapes=[pltpu.VMEM((e, 1), F32)],
        compiler_params=_params(("arbitrary",)),
        name="moe_rank",
    )(mask_t, eidx_t, pstart)


def _aligned(v):
    return pl.multiple_of(v, ROW_ALIGN)


def _run_copies(tab_ref, make_copy):
    def body(e, off):
        rows = _aligned(tab_ref[e])

        @pl.when(rows > 0)
        def _():
            make_copy(_aligned(off), _aligned(tab_ref[N_EXPERTS + e]), rows).start()

        return off + rows

    lax.fori_loop(0, N_EXPERTS, body, 0)


def _dispatch_kernel(tab_ref, ztab_ref, jpos_ref, x_ref, xs_hbm, sbuf, zbuf, sems, zsem):
    i = pl.program_id(0)
    slot = i % 2
    t = x_ref.shape[0]
    rows = sbuf.shape[1]

    def zero_copy(e):
        n = _aligned(ztab_ref[e])
        return pltpu.make_async_copy(zbuf.at[pl.ds(0, n)], xs_hbm.at[pl.ds(_aligned(ztab_ref[N_EXPERTS + e]), n)], zsem)

    def for_zero_runs(fn):
        def body(e, c):
            @pl.when(ztab_ref[e] > 0)
            def _():
                fn(zero_copy(e))
            return c
        lax.fori_loop(0, N_EXPERTS, body, 0)

        def tail(b, c):
            fn(pltpu.make_async_copy(zbuf, xs_hbm.at[pl.ds(pl.multiple_of(b * EXPERT_BLOCK, EXPERT_BLOCK),
                                                           EXPERT_BLOCK)], zsem))
            return c
        lax.fori_loop(ztab_ref[2 * N_EXPERTS], xs_hbm.shape[0] // EXPERT_BLOCK, tail, 0)

    @pl.when(i == 0)
    def _():
        zbuf[...] = jnp.zeros_like(zbuf)
        for_zero_runs(lambda cp: cp.start())

    jp = jpos_ref[...]
    xb = x_ref[...].astype(BF16)
    for r0 in range(0, rows, PERM_CHUNK):
        iota_j = lax.broadcasted_iota(jnp.int32, (PERM_CHUNK, t), 0) + r0
        p = jnp.zeros((PERM_CHUNK, t), F32)
        for k in range(TOP_K):
            p = jnp.where(iota_j == jp[k:k + 1, :], 1.0, p)
        sbuf[slot, r0:r0 + PERM_CHUNK, :] = jnp.dot(p.astype(BF16), xb, preferred_element_type=F32).astype(BF16)

    _run_copies(tab_ref, lambda loc, glob, n: pltpu.make_async_copy(
        sbuf.at[slot, pl.ds(loc, n)], xs_hbm.at[pl.ds(glob, n)], sems.at[slot]))

    def wait_rows(s, n):
        pltpu.make_async_copy(sbuf.at[s, pl.ds(0, n)], xs_hbm.at[pl.ds(0, n)], sems.at[s]).wait()

    @pl.when(i > 0)
    def _():
        wait_rows(1 - slot, _aligned(tab_ref[2 * N_EXPERTS + 1]))

    @pl.when(i == pl.num_programs(0) - 1)
    def _():
        wait_rows(slot, _aligned(tab_ref[2 * N_EXPERTS]))

    @pl.when(i == 0)
    def _():
        for_zero_runs(lambda cp: cp.wait())


def _dispatch(tab, ztab, jpos_t, hx2, xs_rows):
    n, d = hx2.shape
    col = lambda i: (0, i)
    return pl.pallas_call(
        _dispatch_kernel,
        grid=(n // TOKEN_TILE,),
        in_specs=[pl.BlockSpec((TAB_WIDTH,), lambda i: (i,), memory_space=pltpu.SMEM),
                  pl.BlockSpec(memory_space=pltpu.SMEM),
                  pl.BlockSpec((TOP_K, TOKEN_TILE), col),
                  pl.BlockSpec((TOKEN_TILE, d), lambda i: (i, 0))],
        out_specs=pl.BlockSpec(memory_space=pl.ANY),
        out_shape=jax.ShapeDtypeStruct((xs_rows, d), BF16),
        scratch_shapes=[pltpu.VMEM((2, SORTED_ROWS, d), BF16),
                        pltpu.VMEM((EXPERT_BLOCK, d), BF16),
                        pltpu.SemaphoreType.DMA((2,)),
                        pltpu.SemaphoreType.DMA],
        compiler_params=_params(("arbitrary",)),
        name="moe_dispatch",
    )(tab, ztab, jpos_t, hx2)


def _experts_kernel(be_ref, nb_ref, xs_ref, wgu_ref, wd_ref, ys_ref, wgu_b, wd_b):
    i = pl.program_id(0)
    ff = wd_ref.shape[1]
    used = i < nb_ref[0]
    new_expert = (i == 0) | (be_ref[i] != be_ref[jnp.maximum(i - 1, 0)])

    @pl.when(used & new_expert)
    def _():
        wgu_b[...] = wgu_ref[0].astype(BF16)
        wd_b[...] = wd_ref[0].astype(BF16)

    @pl.when(used)
    def _():
        h = jnp.dot(xs_ref[...], wgu_b[...], preferred_element_type=F32)
        a = _silu(h[:, :ff]) * h[:, ff:]
        ys_ref[...] = jnp.dot(a.astype(BF16), wd_b[...], preferred_element_type=F32).astype(BF16)

    @pl.when(jnp.logical_not(used))
    def _():
        ys_ref[...] = jnp.zeros_like(ys_ref)


def _experts(block_e, n_used, xs, w_gu, w_down):
    rows, d = xs.shape
    nb = rows // EXPERT_BLOCK
    blk = lambda i, be, nu: (jnp.maximum(jnp.minimum(i, nu[0] - 1), 0), 0)
    wmap = lambda i, be, nu: (be[i], 0, 0)
    return pl.pallas_call(
        _experts_kernel,
        grid_spec=pltpu.PrefetchScalarGridSpec(
            num_scalar_prefetch=2,
            grid=(nb,),
            in_specs=[pl.BlockSpec((EXPERT_BLOCK, d), blk),
                      pl.BlockSpec((1,) + w_gu.shape[1:], wmap),
                      pl.BlockSpec((1,) + w_down.shape[1:], wmap)],
            out_specs=pl.BlockSpec((EXPERT_BLOCK, d), lambda i, be, nu: (i, 0)),
            scratch_shapes=[pltpu.VMEM(w_gu.shape[1:], BF16), pltpu.VMEM(w_down.shape[1:], BF16)]),
        out_shape=jax.ShapeDtypeStruct((rows, d), BF16),
        compiler_params=_params(("arbitrary",)),
        name="moe_experts",
    )(block_e, n_used, xs, w_gu, w_down)


def _combine_kernel(tab_ref, jpos_ref, gw_ref, ys_hbm, hx2_ref, x1_ref, mod_ref, wsgu_ref, wsd_ref, fg_ref,
                    out_ref, ybuf, sem, *, tiles_per_seq, final_norm):
    t_rows, d = x1_ref.shape
    ff = wsd_ref.shape[0]
    rows = ybuf.shape[0]
    row = pl.program_id(0) // tiles_per_seq

    @pl.when(pl.program_id(0) == 0)
    def _():
        ybuf[...] = jnp.zeros_like(ybuf)

    _run_copies(tab_ref, lambda loc, glob, n: pltpu.make_async_copy(
        ys_hbm.at[pl.ds(glob, n)], ybuf.at[pl.ds(loc, n)], sem))

    hs = jnp.dot(hx2_ref[...].astype(BF16), wsgu_ref[...], preferred_element_type=F32)
    shared = jnp.dot((_silu(hs[:, :ff]) * hs[:, ff:]).astype(BF16), wsd_ref[...], preferred_element_type=F32)

    jp = jpos_ref[...]
    gw = gw_ref[...]
    iota_j = lax.broadcasted_iota(jnp.int32, (t_rows, rows), 1)
    g = jnp.zeros((t_rows, rows), F32)
    for k in range(TOP_K):
        g = jnp.where(iota_j == jp[:, k:k + 1], gw[:, k:k + 1], g)

    total = _aligned(tab_ref[2 * N_EXPERTS])
    pltpu.make_async_copy(ys_hbm.at[pl.ds(0, total)], ybuf.at[pl.ds(0, total)], sem).wait()
    routed = jnp.dot(g.astype(BF16), ybuf[...], preferred_element_type=F32)
    x2 = x1_ref[...] + _mod_chunk(mod_ref, row, 5, d) * (routed + shared)
    if final_norm:
        x2 = x2 * lax.rsqrt(jnp.mean(x2 * x2, axis=-1, keepdims=True) + NORM_EPS) * fg_ref[...]
    out_ref[...] = x2


def _combine(tab, jpos, gw, ys, hx2, x1, mod, ws_gu, ws_down, fg, tiles_per_seq, final_norm):
    n, d = x1.shape
    row = lambda i: (i, 0)
    const = lambda i: (0, 0)
    return pl.pallas_call(
        functools.partial(_combine_kernel, tiles_per_seq=tiles_per_seq, final_norm=final_norm),
        grid=(n // TOKEN_TILE,),
        in_specs=[pl.BlockSpec((TAB_WIDTH,), lambda i: (i,), memory_space=pltpu.SMEM),
                  pl.BlockSpec((TOKEN_TILE, TOP_K), row),
                  pl.BlockSpec((TOKEN_TILE, TOP_K), row),
                  pl.BlockSpec(memory_space=pl.ANY),
                  pl.BlockSpec((TOKEN_TILE, d), row),
                  pl.BlockSpec((TOKEN_TILE, d), row),
                  pl.BlockSpec(mod.shape, const),
                  pl.BlockSpec(ws_gu.shape, const),
                  pl.BlockSpec(ws_down.shape, const),
                  pl.BlockSpec((1, d), const)],
        out_specs=pl.BlockSpec((TOKEN_TILE, d), row),
        out_shape=jax.ShapeDtypeStruct((n, d), F32),
        scratch_shapes=[pltpu.VMEM((SORTED_ROWS, d), BF16), pltpu.SemaphoreType.DMA],
        compiler_params=_params(("arbitrary",)),
        name="moe_combine",
    )(tab, jpos, gw, ys, hx2, x1, mod, ws_gu, ws_down, fg)


def _moe(x1, hx2, eidx_t, gw_t, mask_t, cnt, mod, w_gu, w_down, ws_gu, ws_down, fg, tiles_per_seq, final_norm):
    n, d = x1.shape
    n_tiles = n // TOKEN_TILE
    counts = cnt[:, 0].astype(jnp.int32)
    padded = (counts + EXPERT_BLOCK - 1) // EXPERT_BLOCK * EXPERT_BLOCK
    pad_end = jnp.cumsum(padded)
    pad_start = pad_end - padded
    max_rows = n * TOP_K + n_tiles * N_EXPERTS * (ROW_ALIGN - 1) + N_EXPERTS * (EXPERT_BLOCK - ROW_ALIGN)
    n_blocks = (max_rows + EXPERT_BLOCK - 1) // EXPERT_BLOCK
    n_used = (pad_end[-1:] // EXPERT_BLOCK).astype(jnp.int32)
    block_start = jnp.arange(n_blocks, dtype=jnp.int32) * EXPERT_BLOCK
    block_e = jnp.minimum(jnp.sum((pad_end[None, :] <= block_start[:, None]).astype(jnp.int32), axis=1),
                          N_EXPERTS - 1)

    jpos_t, tab_f = _rank(mask_t, eidx_t, pad_start.astype(F32).reshape(N_EXPERTS, 1))
    run_len = tab_f[:, :, 0].astype(jnp.int32)
    run_start = tab_f[:, :, 1].astype(jnp.int32)
    total = jnp.sum(run_len, axis=1, keepdims=True)
    prev_total = jnp.concatenate([jnp.zeros((1, 1), jnp.int32), total[:-1]], axis=0)
    fill = jnp.zeros((n_tiles, TAB_WIDTH - 2 * N_EXPERTS - 2), jnp.int32)
    tab = jnp.concatenate([run_len, run_start, total, prev_total, fill], axis=1).reshape(-1)
    ztab = jnp.concatenate([padded - counts, pad_start + counts, n_used])

    xs = _dispatch(tab, ztab, jpos_t, hx2, n_blocks * EXPERT_BLOCK)
    ys = _experts(block_e, n_used, xs, w_gu, w_down)
    return _combine(tab, jpos_t.T, gw_t.T, ys, hx2, x1, mod, ws_gu, ws_down, fg, tiles_per_seq, final_norm)


def _sincos_2d(rows, d):
    quarter = d // 4
    omega = 1.0 / (POS_BASE ** (jnp.arange(quarter, dtype=F32) / quarter))

    def emb(n):
        p = jnp.arange(n, dtype=F32)[:, None] * omega[None, :]
        return jnp.concatenate([jnp.sin(p), jnp.cos(p)], axis=-1)

    er, ec = emb(rows), emb(GRID_W)
    pe = jnp.concatenate([jnp.broadcast_to(er[:, None, :], (rows, GRID_W, d // 2)),
                          jnp.broadcast_to(ec[None, :, :], (rows, GRID_W, d // 2))], axis=-1)
    return pe.reshape(rows * GRID_W, d)


def kernel(x, c, ctx, c_ctx, ada_w, ada_b, mix_norm_g, ffn_norm_g, a_w_in, a_conv_w, a_conv_b, a_gate_r_w, a_gate_r_b, a_gate_i_w, a_gate_i_b, a_lambda, a_w_out, b_w_in, b_ln_g, b_ln_b, b_w_s, b_b_s, b_w_out, router_w, router_b, moe_w_gu, moe_w_down, shared_w_gu, shared_w_down, final_norm_g):
    bsz, s, d = x.shape
    ctx_len = ctx.shape[1]
    depth = ada_w.shape[0]
    assert depth == 2 and bsz < MOD_ROWS and s % TOKEN_TILE == 0 and ctx_len % TOKEN_TILE == 0
    n = bsz * s
    tps = s // TOKEN_TILE
    ctx_row = bsz

    cc = jnp.zeros((MOD_ROWS, d), F32).at[:bsz].set(c).at[ctx_row].set(c_ctx)
    mod = _modulation(cc, ada_w, ada_b)
    pe = _sincos_2d(s // GRID_W, d)
    rc = a_w_in.shape[2] // 2

    w_in0 = a_w_in[0].astype(BF16)
    g_mix0 = mix_norm_g[0].reshape(1, d)
    x0, gate, ux = _rglru_in(x.reshape(n, d), pe, mod[0], g_mix0, w_in0)
    uc = _ctx_in(ctx.reshape(bsz * ctx_len, d), mod[0], g_mix0, w_in0[:, rc:], ctx_row)
    w_ri = jnp.concatenate([a_gate_r_w[0], a_gate_i_w[0]], axis=-1).astype(BF16)
    scan_args = (a_conv_w[0], a_conv_b[0], w_ri, a_gate_r_b[0], a_gate_i_b[0], a_lambda[0])
    h_ctx = _lru_scan(uc, *scan_args, jnp.zeros((2 * bsz, rc), F32), n_batch=bsz, reset_first=True, emit_y=False)
    y = _lru_scan(ux, *scan_args, h_ctx, n_batch=bsz, reset_first=False, emit_y=True)
    pre = _rglru_out(y, gate, x0, mod[0], a_w_out[0].astype(BF16), ffn_norm_g[0].reshape(1, d),
                     router_w[0].T, router_b[0].reshape(N_EXPERTS, 1), tps)
    x1 = _moe(*pre, mod[0], moe_w_gu[0], moe_w_down[0], shared_w_gu[0].astype(BF16),
              shared_w_down[0].astype(BF16), final_norm_g.reshape(1, d), tps, False)

    pre = _sgu(x1, mod[1], mix_norm_g[1].reshape(1, d), b_w_in[0].astype(BF16),
               b_ln_g[0].reshape(1, -1), b_ln_b[0].reshape(1, -1), b_w_s[0].astype(BF16), b_b_s[0].T,
               b_w_out[0].astype(BF16), ffn_norm_g[1].reshape(1, d),
               router_w[1].T, router_b[1].reshape(N_EXPERTS, 1), tps)
    out = _moe(*pre, mod[1], moe_w_gu[1], moe_w_down[1], shared_w_gu[1].astype(BF16),
               shared_w_down[1].astype(BF16), final_norm_g.reshape(1, d), tps, True)
    return out.reshape(bsz, s, d)
```

```python
import functools

import jax
import jax.numpy as jnp
from jax import lax
from jax.experimental import pallas as pl
from jax.experimental.pallas import tpu as pltpu

F32 = jnp.float32
BF16 = jnp.bfloat16
HIGHEST = lax.Precision.HIGHEST

GRID_W = 64
N_MOD = 6
NORM_EPS = 1e-6
POS_BASE = 10000.0
RNN_HEADS = 5
CONV_WIDTH = 4
CONV_PAD_LEFT = 2
LRU_C = 8.0
SGU_HEADS = 8
CHUNK = 128
N_EXPERTS = 64
TOP_K = 8
N_GROUPS = 8
TOPK_GROUPS = 4
EXPERTS_PER_GROUP = N_EXPERTS // N_GROUPS
ROUTED_SCALE = 2.5

SUBLANES = 8
ROW_ALIGN = 16
LANES = 128
MOD_ROWS = 8
TOKEN_TILE = 256
MIX_TILE = 512
EXPERT_BLOCK = 512
SORTED_ROWS = TOKEN_TILE * TOP_K + N_EXPERTS * ROW_ALIGN
PERM_CHUNK = 512
TAB_WIDTH = 256
VMEM_LIMIT = 56 * 1024 * 1024


def _params(semantics, vmem=VMEM_LIMIT):
    return pltpu.CompilerParams(dimension_semantics=semantics, vmem_limit_bytes=vmem)


def _silu(x):
    return x * jax.nn.sigmoid(x)


def _rms_mod(x, g, sc, sh):
    y = x * lax.rsqrt(jnp.mean(x * x, axis=-1, keepdims=True) + NORM_EPS)
    return (y * g) * (1.0 + sc) + sh


def _mod_chunk(mod_ref, row, k, d):
    return mod_ref[pl.ds(row, 1), k * d:(k + 1) * d]


def _round_up_rows(count):
    return jnp.ceil(count * (1.0 / ROW_ALIGN)) * float(ROW_ALIGN)


def _mod_kernel(cc_ref, w_ref, b_ref, o_ref):
    s = _silu(cc_ref[...])
    o_ref[0] = jnp.dot(s, w_ref[0], preferred_element_type=F32, precision=HIGHEST) + b_ref[0]


def _modulation(cc, ada_w, ada_b):
    depth, d, nd = ada_w.shape
    return pl.pallas_call(
        _mod_kernel,
        grid=(depth, nd // d),
        in_specs=[pl.BlockSpec((MOD_ROWS, d), lambda l, j: (0, 0)),
                  pl.BlockSpec((1, d, d), lambda l, j: (l, 0, j)),
                  pl.BlockSpec((1, 1, d), lambda l, j: (l, 0, j))],
        out_specs=pl.BlockSpec((1, MOD_ROWS, d), lambda l, j: (l, 0, j)),
        out_shape=jax.ShapeDtypeStruct((depth, MOD_ROWS, nd), F32),
        compiler_params=_params(("arbitrary", "arbitrary")),
        name="modulation",
    )(cc, ada_w, ada_b.reshape(depth, 1, nd))


def _rglru_in_kernel(x_ref, pe_ref, mod_ref, g_ref, w_ref, x0_ref, gate_ref, u_ref, *, tiles_per_seq):
    d = x_ref.shape[1]
    c = u_ref.shape[1]
    row = pl.program_id(0) // tiles_per_seq
    x = x_ref[...] + pe_ref[...]
    hx = _rms_mod(x, g_ref[...], _mod_chunk(mod_ref, row, 1, d), _mod_chunk(mod_ref, row, 0, d))
    z = jnp.dot(hx.astype(BF16), w_ref[...], preferred_element_type=F32)
    x0_ref[...] = x
    gate_ref[...] = jax.nn.gelu(z[:, :c]).astype(BF16)
    u_ref[...] = z[:, c:]


def _rglru_in(x2, pe, mod, g, w_in):
    n, d = x2.shape
    c = w_in.shape[1] // 2
    s = pe.shape[0]
    tps = s // MIX_TILE
    row = lambda i: (i, 0)
    return pl.pallas_call(
        functools.partial(_rglru_in_kernel, tiles_per_seq=tps),
        grid=(n // MIX_TILE,),
        in_specs=[pl.BlockSpec((MIX_TILE, d), row),
                  pl.BlockSpec((MIX_TILE, d), lambda i: (i % tps, 0)),
                  pl.BlockSpec(mod.shape, lambda i: (0, 0)),
                  pl.BlockSpec((1, d), lambda i: (0, 0)),
                  pl.BlockSpec(w_in.shape, lambda i: (0, 0))],
        out_specs=[pl.BlockSpec((MIX_TILE, d), row),
                   pl.BlockSpec((MIX_TILE, c), row),
                   pl.BlockSpec((MIX_TILE, c), row)],
        out_shape=[jax.ShapeDtypeStruct((n, d), F32),
                   jax.ShapeDtypeStruct((n, c), BF16),
                   jax.ShapeDtypeStruct((n, c), F32)],
        compiler_params=_params(("arbitrary",)),
        name="rglru_in",
    )(x2, pe, mod, g, w_in)


def _ctx_in_kernel(x_ref, mod_ref, g_ref, w_ref, u_ref, *, ctx_row):
    d = x_ref.shape[1]
    hx = _rms_mod(x_ref[...], g_ref[...], _mod_chunk(mod_ref, ctx_row, 1, d),
                  _mod_chunk(mod_ref, ctx_row, 0, d))
    u_ref[...] = jnp.dot(hx.astype(BF16), w_ref[...], preferred_element_type=F32)


def _ctx_in(c2, mod, g, w_u, ctx_row):
    n, d = c2.shape
    c = w_u.shape[1]
    return pl.pallas_call(
        functools.partial(_ctx_in_kernel, ctx_row=ctx_row),
        grid=(n // TOKEN_TILE,),
        in_specs=[pl.BlockSpec((TOKEN_TILE, d), lambda i: (i, 0)),
                  pl.BlockSpec(mod.shape, lambda i: (0, 0)),
                  pl.BlockSpec((1, d), lambda i: (0, 0)),
                  pl.BlockSpec(w_u.shape, lambda i: (0, 0))],
        out_specs=pl.BlockSpec((TOKEN_TILE, c), lambda i: (i, 0)),
        out_shape=jax.ShapeDtypeStruct((n, c), F32),
        compiler_params=_params(("arbitrary",)),
        name="ctx_in",
    )(c2, mod, g, w_u)


def _log_sigmoid(x):
    return jnp.minimum(x, 0.0) - jnp.log1p(jnp.exp(-jnp.abs(x)))


def _lru_scan_kernel(u_ref, up_ref, un_ref, cw_ref, cb_ref, wri_ref, rb_ref, ib_ref, lam_ref, h0_ref,
                     out_ref, ubuf, a_scr, b_scr, h_scr, *, n_batch, n_tiles, reset_first, emit_y):
    t_rows, c = u_ref.shape
    hb = c // RNN_HEADS
    d = pl.program_id(0)
    b = pl.program_id(1)
    j = pl.program_id(2)
    jj = j + d * (n_tiles - 1 - 2 * j)

    ubuf[SUBLANES:SUBLANES + t_rows, :] = u_ref[...]
    ubuf[0:SUBLANES, :] = jnp.where(jj == 0, 0.0, up_ref[...])
    ubuf[SUBLANES + t_rows:, :] = jnp.where(jj == n_tiles - 1, 0.0, un_ref[...])
    u = cb_ref[...]
    for k in range(CONV_WIDTH):
        u = u + cw_ref[k:k + 1, :] * ubuf[pl.ds(SUBLANES - CONV_PAD_LEFT + k, t_rows), :]

    log_lam = LRU_C * _log_sigmoid(lam_ref[0])
    rows = lax.broadcasted_iota(jnp.int32, (t_rows, 1), 0)
    first_row = jnp.where(j == 0, jnp.where(d == 0, 0, t_rows - 1), -1)
    for h in range(RNN_HEADS):
        sl = slice(h * hb, (h + 1) * hb)
        uh = u[:, sl]
        z = jnp.dot(uh.astype(BF16), wri_ref[0, h], preferred_element_type=F32)
        r = jax.nn.sigmoid(z[:, :hb] + rb_ref[0][:, sl])
        ig = jax.nn.sigmoid(z[:, hb:] + ib_ref[0][:, sl])
        log_a = r * log_lam[:, sl]
        a = jnp.exp(log_a)
        mult = jnp.sqrt((1.0 - a) * (1.0 + a))
        if reset_first:
            mult = jnp.where(rows == first_row, 1.0, mult)
        a_scr[:, sl] = a
        b_scr[:, sl] = mult * ig * uh

    @pl.when(j == 0)
    def _():
        h_scr[...] = h0_ref[pl.ds(d * n_batch + b, 1), :]

    def step(t, h):
        tt = t + d * (t_rows - 1 - 2 * t)
        h = a_scr[pl.ds(tt, 1), :] * h + b_scr[pl.ds(tt, 1), :]
        if emit_y:
            out_ref[0, pl.ds(tt, 1), :] = h
        return h

    h = lax.fori_loop(0, t_rows, step, h_scr[...], unroll=8)
    h_scr[...] = h
    if not emit_y:
        @pl.when(j == n_tiles - 1)
        def _():
            out_ref[pl.ds(d * n_batch + b, 1), :] = h


def _lru_scan(u, conv_w, conv_b, w_ri, r_b, i_b, lam, h0, *, n_batch, reset_first, emit_y):
    n, c = u.shape
    n_tiles = n // n_batch // TOKEN_TILE
    sub = TOKEN_TILE // SUBLANES
    n_sub = n // SUBLANES

    def tile(d, b, j):
        return b * n_tiles + j + d * (n_tiles - 1 - 2 * j)

    in_specs = [pl.BlockSpec((TOKEN_TILE, c), lambda d, b, j: (tile(d, b, j), 0)),
                pl.BlockSpec((SUBLANES, c), lambda d, b, j: (jnp.maximum(tile(d, b, j) * sub - 1, 0), 0)),
                pl.BlockSpec((SUBLANES, c), lambda d, b, j: (jnp.minimum((tile(d, b, j) + 1) * sub, n_sub - 1), 0)),
                pl.BlockSpec(conv_w.shape, lambda d, b, j: (0, 0)),
                pl.BlockSpec((1, c), lambda d, b, j: (0, 0)),
                pl.BlockSpec((1,) + w_ri.shape[1:], lambda d, b, j: (d, 0, 0, 0)),
                pl.BlockSpec((1, 1, c), lambda d, b, j: (d, 0, 0)),
                pl.BlockSpec((1, 1, c), lambda d, b, j: (d, 0, 0)),
                pl.BlockSpec((1, 1, c), lambda d, b, j: (d, 0, 0)),
                pl.BlockSpec(h0.shape, lambda d, b, j: (0, 0))]
    if emit_y:
        out_specs = pl.BlockSpec((1, TOKEN_TILE, c), lambda d, b, j: (d, tile(d, b, j), 0))
        out_shape = jax.ShapeDtypeStruct((2, n, c), F32)
    else:
        out_specs = pl.BlockSpec(h0.shape, lambda d, b, j: (0, 0))
        out_shape = jax.ShapeDtypeStruct(h0.shape, F32)
    return pl.pallas_call(
        functools.partial(_lru_scan_kernel, n_batch=n_batch, n_tiles=n_tiles,
                          reset_first=reset_first, emit_y=emit_y),
        grid=(2, n_batch, n_tiles),
        in_specs=in_specs,
        out_specs=out_specs,
        out_shape=out_shape,
        scratch_shapes=[pltpu.VMEM((TOKEN_TILE + 2 * SUBLANES, c), F32),
                        pltpu.VMEM((TOKEN_TILE, c), F32),
                        pltpu.VMEM((TOKEN_TILE, c), F32),
                        pltpu.VMEM((1, c), F32)],
        compiler_params=_params(("arbitrary", "arbitrary", "arbitrary")),
        name="lru_scan" if emit_y else "lru_scan_ctx",
    )(u, u, u, conv_w, conv_b.reshape(1, c), w_ri, r_b.reshape(2, 1, c), i_b.reshape(2, 1, c),
      lam.reshape(2, 1, c), h0)


def _route(logits, rb):
    e, t = logits.shape
    neg = -jnp.inf
    scores = jax.nn.sigmoid(logits)
    sel = scores + rb
    iota_g = lax.broadcasted_iota(jnp.int32, (N_GROUPS, t), 0).astype(F32)
    iota_e = lax.broadcasted_iota(jnp.int32, (e, t), 0).astype(F32)

    gs = jnp.full((N_GROUPS, t), neg, F32)
    for g in range(N_GROUPS):
        sg = sel[g * EXPERTS_PER_GROUP:(g + 1) * EXPERTS_PER_GROUP, :]
        m1 = jnp.max(sg, axis=0, keepdims=True)
        i1 = jnp.min(jnp.where(sg == m1, iota_g, float(EXPERTS_PER_GROUP)), axis=0, keepdims=True)
        m2 = jnp.max(jnp.where(iota_g == i1, neg, sg), axis=0, keepdims=True)
        gs = jnp.where(iota_g == float(g), m1 + m2, gs)

    keep = jnp.zeros((N_GROUPS, t), F32)
    for _ in range(TOPK_GROUPS):
        m = jnp.max(gs, axis=0, keepdims=True)
        idx = jnp.min(jnp.where(gs == m, iota_g, float(N_GROUPS)), axis=0, keepdims=True)
        hit = iota_g == idx
        keep = jnp.where(hit, 1.0, keep)
        gs = jnp.where(hit, neg, gs)

    masked = jnp.concatenate(
        [jnp.where(keep[g:g + 1, :] > 0.0, sel[g * EXPERTS_PER_GROUP:(g + 1) * EXPERTS_PER_GROUP, :], neg)
         for g in range(N_GROUPS)], axis=0)

    iota_k = lax.broadcasted_iota(jnp.int32, (TOP_K, t), 0)
    selmask = jnp.zeros((e, t), F32)
    eidx = jnp.zeros((TOP_K, t), F32)
    gw = jnp.zeros((TOP_K, t), F32)
    for k in range(TOP_K):
        m = jnp.max(masked, axis=0, keepdims=True)
        idx = jnp.min(jnp.where(masked == m, iota_e, float(e)), axis=0, keepdims=True)
        hit = iota_e == idx
        gk = jnp.sum(jnp.where(hit, scores, 0.0), axis=0, keepdims=True)
        masked = jnp.where(hit, neg, masked)
        selmask = jnp.where(hit, 1.0, selmask)
        eidx = jnp.where(iota_k == k, idx, eidx)
        gw = jnp.where(iota_k == k, gk, gw)
    gw = gw / jnp.sum(gw, axis=0, keepdims=True) * ROUTED_SCALE
    return eidx, gw, selmask


def _ffn_pre(x1, mod_ref, row, g2_ref, rwt_ref, rb_ref, x1_ref, hx2_ref, eidx_ref, gw_ref, mask_ref, cnt_ref):
    d = x1.shape[1]
    hx2 = _rms_mod(x1, g2_ref[...], _mod_chunk(mod_ref, row, 4, d), _mod_chunk(mod_ref, row, 3, d))
    x1_ref[...] = x1
    hx2_ref[...] = hx2
    logits = lax.dot_general(rwt_ref[...], hx2, (((1,), (1,)), ((), ())),
                             preferred_element_type=F32, precision=HIGHEST)
    eidx, gw, selmask = _route(logits, rb_ref[...])
    eidx_ref[...] = eidx.astype(jnp.int32)
    gw_ref[...] = gw
    mask_ref[...] = selmask

    @pl.when(pl.program_id(0) == 0)
    def _():
        cnt_ref[...] = jnp.zeros_like(cnt_ref)

    rows = jnp.zeros((selmask.shape[0], 1), F32)
    for t0 in range(0, selmask.shape[1], TOKEN_TILE):
        rows = rows + _round_up_rows(jnp.sum(selmask[:, t0:t0 + TOKEN_TILE], axis=1, keepdims=True))
    cnt_ref[...] += jnp.broadcast_to(rows, cnt_ref.shape)


def _ffn_pre_specs(n, d):
    row = lambda i: (i, 0)
    col = lambda i: (0, i)
    out_specs = [pl.BlockSpec((MIX_TILE, d), row),
                 pl.BlockSpec((MIX_TILE, d), row),
                 pl.BlockSpec((TOP_K, MIX_TILE), col),
                 pl.BlockSpec((TOP_K, MIX_TILE), col),
                 pl.BlockSpec((N_EXPERTS, MIX_TILE), col),
                 pl.BlockSpec((N_EXPERTS, LANES), lambda i: (0, 0))]
    out_shape = [jax.ShapeDtypeStruct((n, d), F32),
                 jax.ShapeDtypeStruct((n, d), F32),
                 jax.ShapeDtypeStruct((TOP_K, n), jnp.int32),
                 jax.ShapeDtypeStruct((TOP_K, n), F32),
                 jax.ShapeDtypeStruct((N_EXPERTS, n), F32),
                 jax.ShapeDtypeStruct((N_EXPERTS, 128), F32)]
    return out_specs, out_shape


def _rglru_out_kernel(y_ref, gate_ref, x0_ref, mod_ref, wout_ref, g2_ref, rwt_ref, rb_ref,
                      x1_ref, hx2_ref, eidx_ref, gw_ref, mask_ref, cnt_ref, *, tiles_per_seq):
    d = x0_ref.shape[1]
    row = pl.program_id(0) // tiles_per_seq
    yx = y_ref[0] + y_ref[1]
    v = gate_ref[...].astype(F32) * yx
    out = jnp.dot(v.astype(BF16), wout_ref[...], preferred_element_type=F32)
    x1 = x0_ref[...] + _mod_chunk(mod_ref, row, 2, d) * out
    _ffn_pre(x1, mod_ref, row, g2_ref, rwt_ref, rb_ref, x1_ref, hx2_ref, eidx_ref, gw_ref, mask_ref, cnt_ref)


def _rglru_out(y, gate, x0, mod, w_out, g2, rwt, rb, tiles_per_seq):
    n, d = x0.shape
    c = gate.shape[1]
    row = lambda i: (i, 0)
    const = lambda i: (0, 0)
    out_specs, out_shape = _ffn_pre_specs(n, d)
    return pl.pallas_call(
        functools.partial(_rglru_out_kernel, tiles_per_seq=tiles_per_seq),
        grid=(n // MIX_TILE,),
        in_specs=[pl.BlockSpec((2, MIX_TILE, c), lambda i: (0, i, 0)),
                  pl.BlockSpec((MIX_TILE, c), row),
                  pl.BlockSpec((MIX_TILE, d), row),
                  pl.BlockSpec(mod.shape, const),
                  pl.BlockSpec(w_out.shape, const),
                  pl.BlockSpec((1, d), const),
                  pl.BlockSpec(rwt.shape, const),
                  pl.BlockSpec(rb.shape, const)],
        out_specs=out_specs,
        out_shape=out_shape,
        compiler_params=_params(("arbitrary",)),
        name="rglru_out",
    )(y, gate, x0, mod, w_out, g2, rwt, rb)


def _sgu_kernel(x_ref, mod_ref, g_ref, win_ref, lng_ref, lnb_ref, ws_ref, bst_ref, wout_ref,
                g2_ref, rwt_ref, rb_ref,
                x1_ref, hx2_ref, eidx_ref, gw_ref, mask_ref, cnt_ref, m_scr, *, tiles_per_seq):
    t_rows, d = x_ref.shape
    w = wout_ref.shape[0]
    gd = w // SGU_HEADS
    row = pl.program_id(0) // tiles_per_seq
    x = x_ref[...]
    hx = _rms_mod(x, g_ref[...], _mod_chunk(mod_ref, row, 1, d), _mod_chunk(mod_ref, row, 0, d))
    z = jax.nn.gelu(jnp.dot(hx.astype(BF16), win_ref[...], preferred_element_type=F32))
    u = z[:, :w]
    v = z[:, w:]
    mu = jnp.mean(v, axis=-1, keepdims=True)
    vc = v - mu
    v = vc * lax.rsqrt(jnp.mean(vc * vc, axis=-1, keepdims=True) + NORM_EPS) * lng_ref[...] + lnb_ref[...]
    vb = v.astype(BF16)
    for ch in range(t_rows // CHUNK):
        rs = slice(ch * CHUNK, (ch + 1) * CHUNK)
        for g in range(SGU_HEADS):
            cs = slice(g * gd, (g + 1) * gd)
            sv = jnp.dot(ws_ref[g], vb[rs, cs], preferred_element_type=F32) + bst_ref[:, g:g + 1]
            m_scr[rs, cs] = (u[rs, cs] * sv).astype(BF16)
    out = jnp.dot(m_scr[...], wout_ref[...], preferred_element_type=F32)
    x1 = x + _mod_chunk(mod_ref, row, 2, d) * out
    _ffn_pre(x1, mod_ref, row, g2_ref, rwt_ref, rb_ref, x1_ref, hx2_ref, eidx_ref, gw_ref, mask_ref, cnt_ref)


def _sgu(x, mod, g, w_in, ln_g, ln_b, w_s, b_st, w_out, g2, rwt, rb, tiles_per_seq):
    n, d = x.shape
    w = w_out.shape[0]
    const = lambda i: (0, 0)
    out_specs, out_shape = _ffn_pre_specs(n, d)
    return pl.pallas_call(
        functools.partial(_sgu_kernel, tiles_per_seq=tiles_per_seq),
        grid=(n // MIX_TILE,),
        in_specs=[pl.BlockSpec((MIX_TILE, d), lambda i: (i, 0)),
                  pl.BlockSpec(mod.shape, const),
                  pl.BlockSpec((1, d), const),
                  pl.BlockSpec(w_in.shape, const, pipeline_mode=pl.Buffered(1)),
                  pl.BlockSpec((1, w), const),
                  pl.BlockSpec((1, w), const),
                  pl.BlockSpec(w_s.shape, lambda i: (0, 0, 0)),
                  pl.BlockSpec(b_st.shape, const),
                  pl.BlockSpec(w_out.shape, const, pipeline_mode=pl.Buffered(1)),
                  pl.BlockSpec((1, d), const),
                  pl.BlockSpec(rwt.shape, const),
                  pl.BlockSpec(rb.shape, const)],
        out_specs=out_specs,
        out_shape=out_shape,
        scratch_shapes=[pltpu.VMEM((MIX_TILE, w), BF16)],
        compiler_params=_params(("arbitrary",)),
        name="sgu",
    )(x, mod, g, w_in, ln_g, ln_b, w_s, b_st, w_out, g2, rwt, rb)


def _rank_kernel(mask_ref, eidx_ref, pstart_ref, jpos_ref, tab_ref, carry):
    e, t = mask_ref.shape

    @pl.when(pl.program_id(0) == 0)
    def _():
        carry[...] = jnp.zeros_like(carry)

    m = mask_ref[...]
    r = lax.broadcasted_iota(jnp.int32, (t, t), 0)
    c = lax.broadcasted_iota(jnp.int32, (t, t), 1)
    upper = jnp.where(r <= c, 1.0, 0.0).astype(BF16)
    incl = jnp.dot(m.astype(BF16), upper, preferred_element_type=F32)
    run = _round_up_rows(incl[:, t - 1:t])
    lanes = tab_ref.shape[2]
    re = lax.broadcasted_iota(jnp.int32, (e, e), 0)
    ce = lax.broadcasted_iota(jnp.int32, (e, e), 1)
    lower = jnp.where(ce < re, 1.0, 0.0).astype(BF16)
    tiles = jnp.broadcast_to(run * (1.0 / ROW_ALIGN), (e, lanes)).astype(BF16)
    lstart = jnp.dot(lower, tiles, preferred_element_type=F32)[:, 0:1] * float(ROW_ALIGN)
    pos = lstart + incl - m
    iota_e = lax.broadcasted_iota(jnp.int32, (e, t), 0)
    iota_k = lax.broadcasted_iota(jnp.int32, (TOP_K, t), 0)
    eidx = eidx_ref[...]
    jpos = jnp.zeros((TOP_K, t), F32)
    for k in range(TOP_K):
        jk = jnp.sum(jnp.where(iota_e == eidx[k:k + 1, :], pos, 0.0), axis=0, keepdims=True)
        jpos = jnp.where(iota_k == k, jk, jpos)
    jpos_ref[...] = jpos.astype(jnp.int32)
    lane = lax.broadcasted_iota(jnp.int32, (e, lanes), 1)
    tab_ref[0] = jnp.where(lane == 0, run, jnp.where(lane == 1, pstart_ref[...] + carry[...], 0.0))
    carry[...] = carry[...] + run


def _rank(mask_t, eidx_t, pstart):
    e, n = mask_t.shape
    n_tiles = n // TOKEN_TILE
    col = lambda i: (0, i)
    return pl.pallas_call(
        _rank_kernel,
        grid=(n_tiles,),
        in_specs=[pl.BlockSpec((e, TOKEN_TILE), col),
                  pl.BlockSpec((TOP_K, TOKEN_TILE), col),
                  pl.BlockSpec((e, 1), lambda i: (0, 0))],
        out_specs=[pl.BlockSpec((TOP_K, TOKEN_TILE), col),
                   pl.BlockSpec((1, e, LANES), lambda i: (i, 0, 0))],
        out_shape=[jax.ShapeDtypeStruct((TOP_K, n), jnp.int32),
                   jax.ShapeDtypeStruct((n_tiles, e, LANES), F32)],
        scratch_shapes=[pltpu.VMEM((e, 1), F32)],
        compiler_params=_params(("arbitrary",)),
        name="moe_rank",
    )(mask_t, eidx_t, pstart)


def _aligned(v):
    return pl.multiple_of(v, ROW_ALIGN)


def _run_copies(tab_ref, make_copy):
    def body(e, off):
        rows = _aligned(tab_ref[e])

        @pl.when(rows > 0)
        def _():
            make_copy(_aligned(off), _aligned(tab_ref[N_EXPERTS + e]), rows).start()

        return off + rows

    lax.fori_loop(0, N_EXPERTS, body, 0)


def _dispatch_kernel(tab_ref, ztab_ref, jpos_ref, x_ref, xs_hbm, sbuf, zbuf, sems, zsem):
    i = pl.program_id(0)
    slot = i % 2
    t = x_ref.shape[0]
    rows = sbuf.shape[1]

    def zero_copy(e):
        n = _aligned(ztab_ref[e])
        return pltpu.make_async_copy(zbuf.at[pl.ds(0, n)], xs_hbm.at[pl.ds(_aligned(ztab_ref[N_EXPERTS + e]), n)], zsem)

    def for_zero_runs(fn):
        def body(e, c):
            @pl.when(ztab_ref[e] > 0)
            def _():
                fn(zero_copy(e))
            return c
        lax.fori_loop(0, N_EXPERTS, body, 0)

        def tail(b, c):
            fn(pltpu.make_async_copy(zbuf, xs_hbm.at[pl.ds(pl.multiple_of(b * EXPERT_BLOCK, EXPERT_BLOCK),
                                                           EXPERT_BLOCK)], zsem))
            return c
        lax.fori_loop(ztab_ref[2 * N_EXPERTS], xs_hbm.shape[0] // EXPERT_BLOCK, tail, 0)

    @pl.when(i == 0)
    def _():
        zbuf[...] = jnp.zeros_like(zbuf)
        for_zero_runs(lambda cp: cp.start())

    jp = jpos_ref[...]
    xb = x_ref[...].astype(BF16)
    for r0 in range(0, rows, PERM_CHUNK):
        iota_j = lax.broadcasted_iota(jnp.int32, (PERM_CHUNK, t), 0) + r0
        p = jnp.zeros((PERM_CHUNK, t), F32)
        for k in range(TOP_K):
            p = jnp.where(iota_j == jp[k:k + 1, :], 1.0, p)
        sbuf[slot, r0:r0 + PERM_CHUNK, :] = jnp.dot(p.astype(BF16), xb, preferred_element_type=F32).astype(BF16)

    _run_copies(tab_ref, lambda loc, glob, n: pltpu.make_async_copy(
        sbuf.at[slot, pl.ds(loc, n)], xs_hbm.at[pl.ds(glob, n)], sems.at[slot]))

    def wait_rows(s, n):
        pltpu.make_async_copy(sbuf.at[s, pl.ds(0, n)], xs_hbm.at[pl.ds(0, n)], sems.at[s]).wait()

    @pl.when(i > 0)
    def _():
        wait_rows(1 - slot, _aligned(tab_ref[2 * N_EXPERTS + 1]))

    @pl.when(i == pl.num_programs(0) - 1)
    def _():
        wait_rows(slot, _aligned(tab_ref[2 * N_EXPERTS]))

    @pl.when(i == 0)
    def _():
        for_zero_runs(lambda cp: cp.wait())


def _dispatch(tab, ztab, jpos_t, hx2, xs_rows):
    n, d = hx2.shape
    col = lambda i: (0, i)
    return pl.pallas_call(
        _dispatch_kernel,
        grid=(n // TOKEN_TILE,),
        in_specs=[pl.BlockSpec((TAB_WIDTH,), lambda i: (i,), memory_space=pltpu.SMEM),
                  pl.BlockSpec(memory_space=pltpu.SMEM),
                  pl.BlockSpec((TOP_K, TOKEN_TILE), col),
                  pl.BlockSpec((TOKEN_TILE, d), lambda i: (i, 0))],
        out_specs=pl.BlockSpec(memory_space=pl.ANY),
        out_shape=jax.ShapeDtypeStruct((xs_rows, d), BF16),
        scratch_shapes=[pltpu.VMEM((2, SORTED_ROWS, d), BF16),
                        pltpu.VMEM((EXPERT_BLOCK, d), BF16),
                        pltpu.SemaphoreType.DMA((2,)),
                        pltpu.SemaphoreType.DMA],
        compiler_params=_params(("arbitrary",)),
        name="moe_dispatch",
    )(tab, ztab, jpos_t, hx2)


def _experts_kernel(be_ref, nb_ref, xs_ref, wgu_ref, wd_ref, ys_ref, wgu_b, wd_b):
    i = pl.program_id(0)
    ff = wd_b.shape[0]
    used = i < nb_ref[0]
    new_expert = (i == 0) | (be_ref[i] != be_ref[jnp.maximum(i - 1, 0)])

    @pl.when(used & new_expert)
    def _():
        wgu_b[...] = wgu_ref[0, 0].astype(BF16)
        wd_b[...] = wd_ref[0, 0].astype(BF16)

    @pl.when(used)
    def _():
        h = jnp.dot(xs_ref[...], wgu_b[...], preferred_element_type=F32)
        a = _silu(h[:, :ff]) * h[:, ff:]
        ys_ref[...] = jnp.dot(a.astype(BF16), wd_b[...], preferred_element_type=F32).astype(BF16)

    @pl.when(jnp.logical_not(used))
    def _():
        ys_ref[...] = jnp.zeros_like(ys_ref)


def _experts(block_e, n_used, xs, w_gu, w_down, layer):
    rows, d = xs.shape
    nb = rows // EXPERT_BLOCK
    blk = lambda i, be, nu: (jnp.maximum(jnp.minimum(i, nu[0] - 1), 0), 0)
    wmap = lambda i, be, nu: (layer, be[i], 0, 0)
    return pl.pallas_call(
        _experts_kernel,
        grid_spec=pltpu.PrefetchScalarGridSpec(
            num_scalar_prefetch=2,
            grid=(nb,),
            in_specs=[pl.BlockSpec((EXPERT_BLOCK, d), blk),
                      pl.BlockSpec((1, 1) + w_gu.shape[2:], wmap),
                      pl.BlockSpec((1, 1) + w_down.shape[2:], wmap)],
            out_specs=pl.BlockSpec((EXPERT_BLOCK, d), lambda i, be, nu: (i, 0)),
            scratch_shapes=[pltpu.VMEM(w_gu.shape[2:], BF16), pltpu.VMEM(w_down.shape[2:], BF16)]),
        out_shape=jax.ShapeDtypeStruct((rows, d), BF16),
        compiler_params=_params(("arbitrary",)),
        name="moe_experts",
    )(block_e, n_used, xs, w_gu, w_down)


def _combine_kernel(tab_ref, jpos_ref, gw_ref, ys_hbm, hx2_ref, x1_ref, mod_ref, wsgu_ref, wsd_ref, fg_ref,
                    out_ref, ybuf, sem, *, tiles_per_seq, final_norm):
    t_rows, d = x1_ref.shape
    ff = wsd_ref.shape[0]
    rows = ybuf.shape[0]
    row = pl.program_id(0) // tiles_per_seq

    @pl.when(pl.program_id(0) == 0)
    def _():
        ybuf[...] = jnp.zeros_like(ybuf)

    _run_copies(tab_ref, lambda loc, glob, n: pltpu.make_async_copy(
        ys_hbm.at[pl.ds(glob, n)], ybuf.at[pl.ds(loc, n)], sem))

    hs = jnp.dot(hx2_ref[...].astype(BF16), wsgu_ref[...], preferred_element_type=F32)
    shared = jnp.dot((_silu(hs[:, :ff]) * hs[:, ff:]).astype(BF16), wsd_ref[...], preferred_element_type=F32)

    jp = jpos_ref[...]
    gw = gw_ref[...]
    iota_j = lax.broadcasted_iota(jnp.int32, (t_rows, rows), 1)
    g = jnp.zeros((t_rows, rows), F32)
    for k in range(TOP_K):
        g = jnp.where(iota_j == jp[:, k:k + 1], gw[:, k:k + 1], g)

    total = _aligned(tab_ref[2 * N_EXPERTS])
    pltpu.make_async_copy(ys_hbm.at[pl.ds(0, total)], ybuf.at[pl.ds(0, total)], sem).wait()
    routed = jnp.dot(g.astype(BF16), ybuf[...], preferred_element_type=F32)
    x2 = x1_ref[...] + _mod_chunk(mod_ref, row, 5, d) * (routed + shared)
    if final_norm:
        x2 = x2 * lax.rsqrt(jnp.mean(x2 * x2, axis=-1, keepdims=True) + NORM_EPS) * fg_ref[...]
    out_ref[...] = x2


def _combine(tab, jpos, gw, ys, hx2, x1, mod, ws_gu, ws_down, fg, tiles_per_seq, final_norm):
    n, d = x1.shape
    row = lambda i: (i, 0)
    const = lambda i: (0, 0)
    return pl.pallas_call(
        functools.partial(_combine_kernel, tiles_per_seq=tiles_per_seq, final_norm=final_norm),
        grid=(n // TOKEN_TILE,),
        in_specs=[pl.BlockSpec((TAB_WIDTH,), lambda i: (i,), memory_space=pltpu.SMEM),
                  pl.BlockSpec((TOKEN_TILE, TOP_K), row),
                  pl.BlockSpec((TOKEN_TILE, TOP_K), row),
                  pl.BlockSpec(memory_space=pl.ANY),
                  pl.BlockSpec((TOKEN_TILE, d), row),
                  pl.BlockSpec((TOKEN_TILE, d), row),
                  pl.BlockSpec(mod.shape, const),
                  pl.BlockSpec(ws_gu.shape, const),
                  pl.BlockSpec(ws_down.shape, const),
                  pl.BlockSpec((1, d), const)],
        out_specs=pl.BlockSpec((TOKEN_TILE, d), row),
        out_shape=jax.ShapeDtypeStruct((n, d), F32),
        scratch_shapes=[pltpu.VMEM((SORTED_ROWS, d), BF16), pltpu.SemaphoreType.DMA],
        compiler_params=_params(("arbitrary",)),
        name="moe_combine",
    )(tab, jpos, gw, ys, hx2, x1, mod, ws_gu, ws_down, fg)


def _moe(x1, hx2, eidx_t, gw_t, mask_t, cnt, mod, w_gu, w_down, layer, ws_gu, ws_down, fg, tiles_per_seq,
         final_norm):
    n, d = x1.shape
    n_tiles = n // TOKEN_TILE
    counts = cnt[:, 0].astype(jnp.int32)
    padded = (counts + EXPERT_BLOCK - 1) // EXPERT_BLOCK * EXPERT_BLOCK
    pad_end = jnp.cumsum(padded)
    pad_start = pad_end - padded
    max_rows = n * TOP_K + n_tiles * N_EXPERTS * (ROW_ALIGN - 1) + N_EXPERTS * (EXPERT_BLOCK - ROW_ALIGN)
    n_blocks = (max_rows + EXPERT_BLOCK - 1) // EXPERT_BLOCK
    n_used = (pad_end[-1:] // EXPERT_BLOCK).astype(jnp.int32)
    block_start = jnp.arange(n_blocks, dtype=jnp.int32) * EXPERT_BLOCK
    block_e = jnp.minimum(jnp.sum((pad_end[None, :] <= block_start[:, None]).astype(jnp.int32), axis=1),
                          N_EXPERTS - 1)

    jpos_t, tab_f = _rank(mask_t, eidx_t, pad_start.astype(F32).reshape(N_EXPERTS, 1))
    run_len = tab_f[:, :, 0].astype(jnp.int32)
    run_start = tab_f[:, :, 1].astype(jnp.int32)
    total = jnp.sum(run_len, axis=1, keepdims=True)
    prev_total = jnp.concatenate([jnp.zeros((1, 1), jnp.int32), total[:-1]], axis=0)
    fill = jnp.zeros((n_tiles, TAB_WIDTH - 2 * N_EXPERTS - 2), jnp.int32)
    tab = jnp.concatenate([run_len, run_start, total, prev_total, fill], axis=1).reshape(-1)
    ztab = jnp.concatenate([padded - counts, pad_start + counts, n_used])

    xs = _dispatch(tab, ztab, jpos_t, hx2, n_blocks * EXPERT_BLOCK)
    ys = _experts(block_e, n_used, xs, w_gu, w_down, layer)
    return _combine(tab, jpos_t.T, gw_t.T, ys, hx2, x1, mod, ws_gu, ws_down, fg, tiles_per_seq, final_norm)


def _sincos_2d(rows, d):
    quarter = d // 4
    omega = 1.0 / (POS_BASE ** (jnp.arange(quarter, dtype=F32) / quarter))

    def emb(n):
        p = jnp.arange(n, dtype=F32)[:, None] * omega[None, :]
        return jnp.concatenate([jnp.sin(p), jnp.cos(p)], axis=-1)

    er, ec = emb(rows), emb(GRID_W)
    pe = jnp.concatenate([jnp.broadcast_to(er[:, None, :], (rows, GRID_W, d // 2)),
                          jnp.broadcast_to(ec[None, :, :], (rows, GRID_W, d // 2))], axis=-1)
    return pe.reshape(rows * GRID_W, d)


def kernel(x, c, ctx, c_ctx, ada_w, ada_b, mix_norm_g, ffn_norm_g, a_w_in, a_conv_w, a_conv_b, a_gate_r_w, a_gate_r_b, a_gate_i_w, a_gate_i_b, a_lambda, a_w_out, b_w_in, b_ln_g, b_ln_b, b_w_s, b_b_s, b_w_out, router_w, router_b, moe_w_gu, moe_w_down, shared_w_gu, shared_w_down, final_norm_g):
    bsz, s, d = x.shape
    ctx_len = ctx.shape[1]
    depth = ada_w.shape[0]
    assert depth == 2 and bsz < MOD_ROWS and s % MIX_TILE == 0 and MIX_TILE % TOKEN_TILE == 0
    assert ctx_len % TOKEN_TILE == 0
    n = bsz * s
    tps = s // TOKEN_TILE
    ctx_row = bsz

    cc = jnp.zeros((MOD_ROWS, d), F32).at[:bsz].set(c).at[ctx_row].set(c_ctx)
    mod = _modulation(cc, ada_w, ada_b)
    pe = _sincos_2d(s // GRID_W, d)
    rc = a_w_in.shape[2] // 2

    w_in0 = a_w_in[0].astype(BF16)
    g_mix0 = mix_norm_g[0].reshape(1, d)
    x0, gate, ux = _rglru_in(x.reshape(n, d), pe, mod[0], g_mix0, w_in0)
    uc = _ctx_in(ctx.reshape(bsz * ctx_len, d), mod[0], g_mix0, w_in0[:, rc:], ctx_row)
    w_ri = jnp.concatenate([a_gate_r_w[0], a_gate_i_w[0]], axis=-1).astype(BF16)
    scan_args = (a_conv_w[0], a_conv_b[0], w_ri, a_gate_r_b[0], a_gate_i_b[0], a_lambda[0])
    h_ctx = _lru_scan(uc, *scan_args, jnp.zeros((2 * bsz, rc), F32), n_batch=bsz, reset_first=True, emit_y=False)
    y = _lru_scan(ux, *scan_args, h_ctx, n_batch=bsz, reset_first=False, emit_y=True)
    pre = _rglru_out(y, gate, x0, mod[0], a_w_out[0].astype(BF16), ffn_norm_g[0].reshape(1, d),
                     router_w[0].T, router_b[0].reshape(N_EXPERTS, 1), s // MIX_TILE)
    x1 = _moe(*pre, mod[0], moe_w_gu, moe_w_down, 0, shared_w_gu[0].astype(BF16),
              shared_w_down[0].astype(BF16), final_norm_g.reshape(1, d), tps, False)

    pre = _sgu(x1, mod[1], mix_norm_g[1].reshape(1, d), b_w_in[0].astype(BF16),
               b_ln_g[0].reshape(1, -1), b_ln_b[0].reshape(1, -1), b_w_s[0].astype(BF16), b_b_s[0].T,
               b_w_out[0].astype(BF16), ffn_norm_g[1].reshape(1, d),
               router_w[1].T, router_b[1].reshape(N_EXPERTS, 1), s // MIX_TILE)
    out = _moe(*pre, mod[1], moe_w_gu, moe_w_down, 1, shared_w_gu[1].astype(BF16),
               shared_w_down[1].astype(BF16), final_norm_g.reshape(1, d), tps, True)
    return out.reshape(bsz, s, d)
```

```python
import functools

import jax
import jax.numpy as jnp
from jax import lax
from jax.experimental import pallas as pl
from jax.experimental.pallas import tpu as pltpu

F32 = jnp.float32
BF16 = jnp.bfloat16
HIGHEST = lax.Precision.HIGHEST

GRID_W = 64
N_MOD = 6
NORM_EPS = 1e-6
POS_BASE = 10000.0
RNN_HEADS = 5
CONV_WIDTH = 4
CONV_PAD_LEFT = 2
LRU_C = 8.0
SGU_HEADS = 8
CHUNK = 128
N_EXPERTS = 64
TOP_K = 8
N_GROUPS = 8
TOPK_GROUPS = 4
EXPERTS_PER_GROUP = N_EXPERTS // N_GROUPS
ROUTED_SCALE = 2.5

SUBLANES = 8
ROW_ALIGN = 16
LANES = 128
MOD_ROWS = 8
TOKEN_TILE = 256
MIX_TILE = 512
MIX_SUB = 256
SGU_ROWS = MIX_TILE
EXPERT_BLOCK = 512
SORTED_ROWS = TOKEN_TILE * TOP_K + N_EXPERTS * ROW_ALIGN
PERM_CHUNK = 512
TAB_WIDTH = 256
VMEM_LIMIT = 56 * 1024 * 1024


def _params(semantics, vmem=VMEM_LIMIT):
    return pltpu.CompilerParams(dimension_semantics=semantics, vmem_limit_bytes=vmem)


def _silu(x):
    return x * jax.nn.sigmoid(x)


def _rms_mod(x, g, sc, sh):
    y = x * lax.rsqrt(jnp.mean(x * x, axis=-1, keepdims=True) + NORM_EPS)
    return (y * g) * (1.0 + sc) + sh


def _mod_chunk(mod_ref, row, k, d):
    return mod_ref[pl.ds(row, 1), k * d:(k + 1) * d]


def _round_up_rows(count):
    return jnp.ceil(count * (1.0 / ROW_ALIGN)) * float(ROW_ALIGN)


def _mod_kernel(cc_ref, w_ref, b_ref, o_ref):
    s = _silu(cc_ref[...])
    o_ref[0] = jnp.dot(s, w_ref[0], preferred_element_type=F32, precision=HIGHEST) + b_ref[0]


def _modulation(cc, ada_w, ada_b):
    depth, d, nd = ada_w.shape
    return pl.pallas_call(
        _mod_kernel,
        grid=(depth, nd // d),
        in_specs=[pl.BlockSpec((MOD_ROWS, d), lambda l, j: (0, 0)),
                  pl.BlockSpec((1, d, d), lambda l, j: (l, 0, j)),
                  pl.BlockSpec((1, 1, d), lambda l, j: (l, 0, j))],
        out_specs=pl.BlockSpec((1, MOD_ROWS, d), lambda l, j: (l, 0, j)),
        out_shape=jax.ShapeDtypeStruct((depth, MOD_ROWS, nd), F32),
        compiler_params=_params(("arbitrary", "arbitrary")),
        name="modulation",
    )(cc, ada_w, ada_b.reshape(depth, 1, nd))


def _rglru_in_kernel(x_ref, pe_ref, mod_ref, g_ref, w_ref, x0_ref, gate_ref, u_ref, *, tiles_per_seq):
    d = x_ref.shape[1]
    c = u_ref.shape[1]
    row = pl.program_id(0) // tiles_per_seq
    x = x_ref[...] + pe_ref[...]
    hx = _rms_mod(x, g_ref[...], _mod_chunk(mod_ref, row, 1, d), _mod_chunk(mod_ref, row, 0, d))
    z = jnp.dot(hx.astype(BF16), w_ref[...], preferred_element_type=F32)
    x0_ref[...] = x
    gate_ref[...] = jax.nn.gelu(z[:, :c]).astype(BF16)
    u_ref[...] = z[:, c:]


def _rglru_in(x2, pe, mod, g, w_in):
    n, d = x2.shape
    c = w_in.shape[1] // 2
    s = pe.shape[0]
    tps = s // MIX_TILE
    row = lambda i: (i, 0)
    return pl.pallas_call(
        functools.partial(_rglru_in_kernel, tiles_per_seq=tps),
        grid=(n // MIX_TILE,),
        in_specs=[pl.BlockSpec((MIX_TILE, d), row),
                  pl.BlockSpec((MIX_TILE, d), lambda i: (i % tps, 0)),
                  pl.BlockSpec(mod.shape, lambda i: (0, 0)),
                  pl.BlockSpec((1, d), lambda i: (0, 0)),
                  pl.BlockSpec(w_in.shape, lambda i: (0, 0))],
        out_specs=[pl.BlockSpec((MIX_TILE, d), row),
                   pl.BlockSpec((MIX_TILE, c), row),
                   pl.BlockSpec((MIX_TILE, c), row)],
        out_shape=[jax.ShapeDtypeStruct((n, d), F32),
                   jax.ShapeDtypeStruct((n, c), BF16),
                   jax.ShapeDtypeStruct((n, c), F32)],
        compiler_params=_params(("arbitrary",)),
        name="rglru_in",
    )(x2, pe, mod, g, w_in)


def _ctx_in_kernel(x_ref, mod_ref, g_ref, w_ref, u_ref, *, ctx_row):
    d = x_ref.shape[1]
    hx = _rms_mod(x_ref[...], g_ref[...], _mod_chunk(mod_ref, ctx_row, 1, d),
                  _mod_chunk(mod_ref, ctx_row, 0, d))
    u_ref[...] = jnp.dot(hx.astype(BF16), w_ref[...], preferred_element_type=F32)


def _ctx_in(c2, mod, g, w_u, ctx_row):
    n, d = c2.shape
    c = w_u.shape[1]
    return pl.pallas_call(
        functools.partial(_ctx_in_kernel, ctx_row=ctx_row),
        grid=(n // TOKEN_TILE,),
        in_specs=[pl.BlockSpec((TOKEN_TILE, d), lambda i: (i, 0)),
                  pl.BlockSpec(mod.shape, lambda i: (0, 0)),
                  pl.BlockSpec((1, d), lambda i: (0, 0)),
                  pl.BlockSpec(w_u.shape, lambda i: (0, 0))],
        out_specs=pl.BlockSpec((TOKEN_TILE, c), lambda i: (i, 0)),
        out_shape=jax.ShapeDtypeStruct((n, c), F32),
        compiler_params=_params(("arbitrary",)),
        name="ctx_in",
    )(c2, mod, g, w_u)


def _log_sigmoid(x):
    return jnp.minimum(x, 0.0) - jnp.log1p(jnp.exp(-jnp.abs(x)))


def _lru_scan_kernel(*refs, n_tiles, reverse, conv_done, reset_first, emit_y):
    if conv_done:
        u_ref, wri_ref, rb_ref, ib_ref, lam_ref, h0_ref, out_ref, a_scr, b_scr, h_scr = refs
    else:
        (u_ref, up_ref, un_ref, cw_ref, cb_ref, wri_ref, rb_ref, ib_ref, lam_ref, h0_ref,
         out_ref, uc_ref, ubuf, a_scr, b_scr, h_scr) = refs
    t_rows, c = u_ref.shape
    hb = c // RNN_HEADS
    b = pl.program_id(0)
    j = pl.program_id(1)
    jj = n_tiles - 1 - j if reverse else j

    if conv_done:
        u = u_ref[...]
    else:
        ubuf[SUBLANES:SUBLANES + t_rows, :] = u_ref[...]
        ubuf[0:SUBLANES, :] = jnp.where(jj == 0, 0.0, up_ref[...])
        ubuf[SUBLANES + t_rows:, :] = jnp.where(jj == n_tiles - 1, 0.0, un_ref[...])
        u = cb_ref[...]
        for k in range(CONV_WIDTH):
            u = u + cw_ref[k:k + 1, :] * ubuf[pl.ds(SUBLANES - CONV_PAD_LEFT + k, t_rows), :]
        uc_ref[...] = u

    log_lam = LRU_C * _log_sigmoid(lam_ref[...])
    rows = lax.broadcasted_iota(jnp.int32, (t_rows, 1), 0)
    first_row = jnp.where(j == 0, t_rows - 1 if reverse else 0, -1)
    for h in range(RNN_HEADS):
        sl = slice(h * hb, (h + 1) * hb)
        uh = u[:, sl]
        z = jnp.dot(uh.astype(BF16), wri_ref[h], preferred_element_type=F32)
        r = jax.nn.sigmoid(z[:, :hb] + rb_ref[:, sl])
        ig = jax.nn.sigmoid(z[:, hb:] + ib_ref[:, sl])
        log_a = r * log_lam[:, sl]
        a = jnp.exp(log_a)
        mult = jnp.sqrt((1.0 - a) * (1.0 + a))
        if reset_first:
            mult = jnp.where(rows == first_row, 1.0, mult)
        a_scr[:, sl] = a
        b_scr[:, sl] = mult * ig * uh

    @pl.when(j == 0)
    def _():
        h_scr[...] = h0_ref[pl.ds(b, 1), :]

    n_groups = t_rows // SUBLANES

    def group(g, h):
        base = pl.multiple_of((n_groups - 1 - g if reverse else g) * SUBLANES, SUBLANES)
        for s in range(SUBLANES):
            r = base + (SUBLANES - 1 - s if reverse else s)
            h = a_scr[pl.ds(r, 1), :] * h + b_scr[pl.ds(r, 1), :]
            if emit_y:
                out_ref[pl.ds(r, 1), :] = h
        return h

    h = lax.fori_loop(0, n_groups, group, h_scr[...])
    h_scr[...] = h
    if not emit_y:
        @pl.when(j == n_tiles - 1)
        def _():
            out_ref[pl.ds(b, 1), :] = h


def _lru_scan(u, conv, w_ri, r_b, i_b, lam, h0, *, reverse, reset_first, emit_y):
    n, c = u.shape
    n_batch = h0.shape[0]
    n_tiles = n // n_batch // TOKEN_TILE
    sub = TOKEN_TILE // SUBLANES
    n_sub = n // SUBLANES
    const = lambda b, j: (0, 0)

    def tile(b, j):
        return b * n_tiles + (n_tiles - 1 - j if reverse else j)

    tile_spec = pl.BlockSpec((TOKEN_TILE, c), lambda b, j: (tile(b, j), 0))
    in_specs = [tile_spec]
    args = [u]
    scratch = []
    if conv is not None:
        conv_w, conv_b = conv
        in_specs += [pl.BlockSpec((SUBLANES, c), lambda b, j: (jnp.maximum(tile(b, j) * sub - 1, 0), 0)),
                     pl.BlockSpec((SUBLANES, c), lambda b, j: (jnp.minimum((tile(b, j) + 1) * sub, n_sub - 1), 0)),
                     pl.BlockSpec(conv_w.shape, const),
                     pl.BlockSpec((1, c), const)]
        args += [u, u, conv_w, conv_b.reshape(1, c)]
        scratch = [pltpu.VMEM((TOKEN_TILE + 2 * SUBLANES, c), F32)]
    in_specs += [pl.BlockSpec(w_ri.shape, lambda b, j: (0, 0, 0)),
                 pl.BlockSpec((1, c), const), pl.BlockSpec((1, c), const), pl.BlockSpec((1, c), const),
                 pl.BlockSpec(h0.shape, const)]
    args += [w_ri, r_b.reshape(1, c), i_b.reshape(1, c), lam.reshape(1, c), h0]
    if emit_y:
        out_specs = [tile_spec]
        out_shape = [jax.ShapeDtypeStruct((n, c), F32)]
    else:
        out_specs = [pl.BlockSpec(h0.shape, const)]
        out_shape = [jax.ShapeDtypeStruct(h0.shape, F32)]
    if conv is not None:
        out_specs.append(tile_spec)
        out_shape.append(jax.ShapeDtypeStruct((n, c), F32))
    name = ("lru_scan" if emit_y else "lru_ctx") + ("_rev" if reverse else "_fwd")
    outs = pl.pallas_call(
        functools.partial(_lru_scan_kernel, n_tiles=n_tiles, reverse=reverse, conv_done=conv is None,
                          reset_first=reset_first, emit_y=emit_y),
        grid=(n_batch, n_tiles),
        in_specs=in_specs,
        out_specs=out_specs,
        out_shape=out_shape,
        scratch_shapes=scratch + [pltpu.VMEM((TOKEN_TILE, c), F32),
                                  pltpu.VMEM((TOKEN_TILE, c), F32),
                                  pltpu.VMEM((1, c), F32)],
        compiler_params=_params(("arbitrary", "arbitrary")),
        name=name,
    )(*args)
    return outs if conv is not None else outs[0]


def _route(logits, rb):
    e, t = logits.shape
    neg = -jnp.inf
    scores = jax.nn.sigmoid(logits)
    sel = scores + rb
    iota_g = lax.broadcasted_iota(jnp.int32, (N_GROUPS, t), 0).astype(F32)
    iota_e = lax.broadcasted_iota(jnp.int32, (e, t), 0).astype(F32)

    gs = jnp.full((N_GROUPS, t), neg, F32)
    for g in range(N_GROUPS):
        sg = sel[g * EXPERTS_PER_GROUP:(g + 1) * EXPERTS_PER_GROUP, :]
        m1 = jnp.max(sg, axis=0, keepdims=True)
        i1 = jnp.min(jnp.where(sg == m1, iota_g, float(EXPERTS_PER_GROUP)), axis=0, keepdims=True)
        m2 = jnp.max(jnp.where(iota_g == i1, neg, sg), axis=0, keepdims=True)
        gs = jnp.where(iota_g == float(g), m1 + m2, gs)

    keep = jnp.zeros((N_GROUPS, t), F32)
    for _ in range(TOPK_GROUPS):
        m = jnp.max(gs, axis=0, keepdims=True)
        idx = jnp.min(jnp.where(gs == m, iota_g, float(N_GROUPS)), axis=0, keepdims=True)
        hit = iota_g == idx
        keep = jnp.where(hit, 1.0, keep)
        gs = jnp.where(hit, neg, gs)

    masked = jnp.concatenate(
        [jnp.where(keep[g:g + 1, :] > 0.0, sel[g * EXPERTS_PER_GROUP:(g + 1) * EXPERTS_PER_GROUP, :], neg)
         for g in range(N_GROUPS)], axis=0)

    iota_k = lax.broadcasted_iota(jnp.int32, (TOP_K, t), 0)
    selmask = jnp.zeros((e, t), F32)
    eidx = jnp.zeros((TOP_K, t), F32)
    gw = jnp.zeros((TOP_K, t), F32)
    for k in range(TOP_K):
        m = jnp.max(masked, axis=0, keepdims=True)
        idx = jnp.min(jnp.where(masked == m, iota_e, float(e)), axis=0, keepdims=True)
        hit = iota_e == idx
        gk = jnp.sum(jnp.where(hit, scores, 0.0), axis=0, keepdims=True)
        masked = jnp.where(hit, neg, masked)
        selmask = jnp.where(hit, 1.0, selmask)
        eidx = jnp.where(iota_k == k, idx, eidx)
        gw = jnp.where(iota_k == k, gk, gw)
    gw = gw / jnp.sum(gw, axis=0, keepdims=True) * ROUTED_SCALE
    return eidx, gw, selmask


def _ffn_pre(x1, mod_ref, row, g2_ref, rwt_ref, rb_ref, hx2_ref, eidx_ref, gw_ref, mask_ref, cnt_ref):
    d = x1.shape[1]
    hx2 = _rms_mod(x1, g2_ref[...], _mod_chunk(mod_ref, row, 4, d), _mod_chunk(mod_ref, row, 3, d))
    hx2_ref[...] = hx2
    logits = lax.dot_general(rwt_ref[...], hx2, (((1,), (1,)), ((), ())),
                             preferred_element_type=F32, precision=HIGHEST)
    eidx, gw, selmask = _route(logits, rb_ref[...])
    eidx_ref[...] = eidx.astype(jnp.int32)
    gw_ref[...] = gw
    mask_ref[...] = selmask

    @pl.when(pl.program_id(0) == 0)
    def _():
        cnt_ref[...] = jnp.zeros_like(cnt_ref)

    rows = jnp.zeros((selmask.shape[0], 1), F32)
    for t0 in range(0, selmask.shape[1], TOKEN_TILE):
        rows = rows + _round_up_rows(jnp.sum(selmask[:, t0:t0 + TOKEN_TILE], axis=1, keepdims=True))
    cnt_ref[...] += jnp.broadcast_to(rows, cnt_ref.shape)


def _ffn_pre_specs(n, d):
    row = lambda i: (i, 0)
    col = lambda i: (0, i)
    out_specs = [pl.BlockSpec((MIX_TILE, d), row),
                 pl.BlockSpec((MIX_TILE, d), row),
                 pl.BlockSpec((TOP_K, MIX_TILE), col),
                 pl.BlockSpec((TOP_K, MIX_TILE), col),
                 pl.BlockSpec((N_EXPERTS, MIX_TILE), col),
                 pl.BlockSpec((N_EXPERTS, LANES), lambda i: (0, 0))]
    out_shape = [jax.ShapeDtypeStruct((n, d), F32),
                 jax.ShapeDtypeStruct((n, d), F32),
                 jax.ShapeDtypeStruct((TOP_K, n), jnp.int32),
                 jax.ShapeDtypeStruct((TOP_K, n), F32),
                 jax.ShapeDtypeStruct((N_EXPERTS, n), F32),
                 jax.ShapeDtypeStruct((N_EXPERTS, 128), F32)]
    return out_specs, out_shape


def _rglru_out_kernel(yf_ref, yr_ref, gate_ref, x0_ref, mod_ref, wout_ref, g2_ref, rwt_ref, rb_ref,
                      x1_ref, hx2_ref, eidx_ref, gw_ref, mask_ref, cnt_ref, *, tiles_per_seq):
    d = x0_ref.shape[1]
    row = pl.program_id(0) // tiles_per_seq
    yx = yf_ref[...] + yr_ref[...]
    v = gate_ref[...].astype(F32) * yx
    out = jnp.dot(v.astype(BF16), wout_ref[...], preferred_element_type=F32)
    x1 = x0_ref[...] + _mod_chunk(mod_ref, row, 2, d) * out
    x1_ref[...] = x1
    _ffn_pre(x1, mod_ref, row, g2_ref, rwt_ref, rb_ref, hx2_ref, eidx_ref, gw_ref, mask_ref, cnt_ref)


def _rglru_out(y_fwd, y_rev, gate, x0, mod, w_out, g2, rwt, rb, tiles_per_seq):
    n, d = x0.shape
    c = gate.shape[1]
    row = lambda i: (i, 0)
    const = lambda i: (0, 0)
    out_specs, out_shape = _ffn_pre_specs(n, d)
    return pl.pallas_call(
        functools.partial(_rglru_out_kernel, tiles_per_seq=tiles_per_seq),
        grid=(n // MIX_TILE,),
        in_specs=[pl.BlockSpec((MIX_TILE, c), row),
                  pl.BlockSpec((MIX_TILE, c), row),
                  pl.BlockSpec((MIX_TILE, c), row),
                  pl.BlockSpec((MIX_TILE, d), row),
                  pl.BlockSpec(mod.shape, const),
                  pl.BlockSpec(w_out.shape, const),
                  pl.BlockSpec((1, d), const),
                  pl.BlockSpec(rwt.shape, const),
                  pl.BlockSpec(rb.shape, const)],
        out_specs=out_specs,
        out_shape=out_shape,
        compiler_params=_params(("arbitrary",)),
        name="rglru_out",
    )(y_fwd, y_rev, gate, x0, mod, w_out, g2, rwt, rb)


def _sgu_kernel(x_ref, mod_ref, g_ref, win_ref, lng_ref, lnb_ref, ws_ref, bst_ref, wout_ref,
                g2_ref, rwt_ref, rb_ref,
                x1_ref, hx2_ref, eidx_ref, gw_ref, mask_ref, cnt_ref, m_scr, *, tiles_per_seq):
    t_rows, d = x_ref.shape
    w = wout_ref.shape[0]
    gd = w // SGU_HEADS
    row = pl.program_id(0) // tiles_per_seq
    for r0 in range(0, t_rows, SGU_ROWS):
        x = x_ref[r0:r0 + SGU_ROWS, :]
        hx = _rms_mod(x, g_ref[...], _mod_chunk(mod_ref, row, 1, d), _mod_chunk(mod_ref, row, 0, d))
        z = jax.nn.gelu(jnp.dot(hx.astype(BF16), win_ref[...], preferred_element_type=F32))
        u = z[:, :w]
        v = z[:, w:]
        mu = jnp.mean(v, axis=-1, keepdims=True)
        vc = v - mu
        v = vc * lax.rsqrt(jnp.mean(vc * vc, axis=-1, keepdims=True) + NORM_EPS) * lng_ref[...] + lnb_ref[...]
        vb = v.astype(BF16)
        for ch in range(SGU_ROWS // CHUNK):
            rs = slice(ch * CHUNK, (ch + 1) * CHUNK)
            for g in range(SGU_HEADS):
                cs = slice(g * gd, (g + 1) * gd)
                sv = jnp.dot(ws_ref[g], vb[rs, cs], preferred_element_type=F32) + bst_ref[:, g:g + 1]
                m_scr[r0 + ch * CHUNK:r0 + (ch + 1) * CHUNK, cs] = (u[rs, cs] * sv).astype(BF16)
        out = jnp.dot(m_scr[r0:r0 + SGU_ROWS, :], wout_ref[...], preferred_element_type=F32)
        x1_ref[r0:r0 + SGU_ROWS, :] = x + _mod_chunk(mod_ref, row, 2, d) * out
    _ffn_pre(x1_ref[...], mod_ref, row, g2_ref, rwt_ref, rb_ref, hx2_ref, eidx_ref, gw_ref, mask_ref, cnt_ref)


def _sgu(x, mod, g, w_in, ln_g, ln_b, w_s, b_st, w_out, g2, rwt, rb, tiles_per_seq):
    n, d = x.shape
    w = w_out.shape[0]
    const = lambda i: (0, 0)
    out_specs, out_shape = _ffn_pre_specs(n, d)
    return pl.pallas_call(
        functools.partial(_sgu_kernel, tiles_per_seq=tiles_per_seq),
        grid=(n // MIX_TILE,),
        in_specs=[pl.BlockSpec((MIX_TILE, d), lambda i: (i, 0)),
                  pl.BlockSpec(mod.shape, const),
                  pl.BlockSpec((1, d), const),
                  pl.BlockSpec(w_in.shape, const, pipeline_mode=pl.Buffered(1)),
                  pl.BlockSpec((1, w), const),
                  pl.BlockSpec((1, w), const),
                  pl.BlockSpec(w_s.shape, lambda i: (0, 0, 0)),
                  pl.BlockSpec(b_st.shape, const),
                  pl.BlockSpec(w_out.shape, const, pipeline_mode=pl.Buffered(1)),
                  pl.BlockSpec((1, d), const),
                  pl.BlockSpec(rwt.shape, const),
                  pl.BlockSpec(rb.shape, const)],
        out_specs=out_specs,
        out_shape=out_shape,
        scratch_shapes=[pltpu.VMEM((MIX_TILE, w), BF16)],
        compiler_params=_params(("arbitrary",)),
        name="sgu",
    )(x, mod, g, w_in, ln_g, ln_b, w_s, b_st, w_out, g2, rwt, rb)


def _rank_kernel(mask_ref, eidx_ref, pstart_ref, jpos_ref, tab_ref, carry):
    e, t = mask_ref.shape

    @pl.when(pl.program_id(0) == 0)
    def _():
        carry[...] = jnp.zeros_like(carry)

    m = mask_ref[...]
    r = lax.broadcasted_iota(jnp.int32, (t, t), 0)
    c = lax.broadcasted_iota(jnp.int32, (t, t), 1)
    upper = jnp.where(r <= c, 1.0, 0.0).astype(BF16)
    incl = jnp.dot(m.astype(BF16), upper, preferred_element_type=F32)
    run = _round_up_rows(incl[:, t - 1:t])
    lanes = tab_ref.shape[2]
    re = lax.broadcasted_iota(jnp.int32, (e, e), 0)
    ce = lax.broadcasted_iota(jnp.int32, (e, e), 1)
    lower = jnp.where(ce < re, 1.0, 0.0).astype(BF16)
    tiles = jnp.broadcast_to(run * (1.0 / ROW_ALIGN), (e, lanes)).astype(BF16)
    lstart = jnp.dot(lower, tiles, preferred_element_type=F32)[:, 0:1] * float(ROW_ALIGN)
    pos = lstart + incl - m
    iota_e = lax.broadcasted_iota(jnp.int32, (e, t), 0)
    iota_k = lax.broadcasted_iota(jnp.int32, (TOP_K, t), 0)
    eidx = eidx_ref[...]
    jpos = jnp.zeros((TOP_K, t), F32)
    for k in range(TOP_K):
        jk = jnp.sum(jnp.where(iota_e == eidx[k:k + 1, :], pos, 0.0), axis=0, keepdims=True)
        jpos = jnp.where(iota_k == k, jk, jpos)
    jpos_ref[...] = jpos.astype(jnp.int32)
    lane = lax.broadcasted_iota(jnp.int32, (e, lanes), 1)
    tab_ref[0] = jnp.where(lane == 0, run, jnp.where(lane == 1, pstart_ref[...] + carry[...], 0.0))
    carry[...] = carry[...] + run


def _rank(mask_t, eidx_t, pstart):
    e, n = mask_t.shape
    n_tiles = n // TOKEN_TILE
    col = lambda i: (0, i)
    return pl.pallas_call(
        _rank_kernel,
        grid=(n_tiles,),
        in_specs=[pl.BlockSpec((e, TOKEN_TILE), col),
                  pl.BlockSpec((TOP_K, TOKEN_TILE), col),
                  pl.BlockSpec((e, 1), lambda i: (0, 0))],
        out_specs=[pl.BlockSpec((TOP_K, TOKEN_TILE), col),
                   pl.BlockSpec((1, e, LANES), lambda i: (i, 0, 0))],
        out_shape=[jax.ShapeDtypeStruct((TOP_K, n), jnp.int32),
                   jax.ShapeDtypeStruct((n_tiles, e, LANES), F32)],
        scratch_shapes=[pltpu.VMEM((e, 1), F32)],
        compiler_params=_params(("arbitrary",)),
        name="moe_rank",
    )(mask_t, eidx_t, pstart)


def _aligned(v):
    return pl.multiple_of(v, ROW_ALIGN)


def _run_copies(tab_ref, make_copy):
    def body(e, off):
        rows = _aligned(tab_ref[e])

        @pl.when(rows > 0)
        def _():
            make_copy(_aligned(off), _aligned(tab_ref[N_EXPERTS + e]), rows).start()

        return off + rows

    lax.fori_loop(0, N_EXPERTS, body, 0)


def _dispatch_kernel(tab_ref, ztab_ref, jpos_ref, x_ref, xs_hbm, sbuf, zbuf, sems, zsem):
    i = pl.program_id(0)
    slot = i % 2
    t = x_ref.shape[0]
    rows = sbuf.shape[1]

    def zero_copy(e):
        n = _aligned(ztab_ref[e])
        return pltpu.make_async_copy(zbuf.at[pl.ds(0, n)], xs_hbm.at[pl.ds(_aligned(ztab_ref[N_EXPERTS + e]), n)], zsem)

    def for_zero_runs(fn):
        def body(e, c):
            @pl.when(ztab_ref[e] > 0)
            def _():
                fn(zero_copy(e))
            return c
        lax.fori_loop(0, N_EXPERTS, body, 0)

        def tail(b, c):
            fn(pltpu.make_async_copy(zbuf, xs_hbm.at[pl.ds(pl.multiple_of(b * EXPERT_BLOCK, EXPERT_BLOCK),
                                                           EXPERT_BLOCK)], zsem))
            return c
        lax.fori_loop(ztab_ref[2 * N_EXPERTS], xs_hbm.shape[0] // EXPERT_BLOCK, tail, 0)

    @pl.when(i == 0)
    def _():
        zbuf[...] = jnp.zeros_like(zbuf)
        for_zero_runs(lambda cp: cp.start())

    jp = jpos_ref[...]
    xb = x_ref[...].astype(BF16)
    for r0 in range(0, rows, PERM_CHUNK):
        iota_j = lax.broadcasted_iota(jnp.int32, (PERM_CHUNK, t), 0) + r0
        p = jnp.zeros((PERM_CHUNK, t), F32)
        for k in range(TOP_K):
            p = jnp.where(iota_j == jp[k:k + 1, :], 1.0, p)
        sbuf[slot, r0:r0 + PERM_CHUNK, :] = jnp.dot(p.astype(BF16), xb, preferred_element_type=F32).astype(BF16)

    _run_copies(tab_ref, lambda loc, glob, n: pltpu.make_async_copy(
        sbuf.at[slot, pl.ds(loc, n)], xs_hbm.at[pl.ds(glob, n)], sems.at[slot]))

    def wait_rows(s, n):
        pltpu.make_async_copy(sbuf.at[s, pl.ds(0, n)], xs_hbm.at[pl.ds(0, n)], sems.at[s]).wait()

    @pl.when(i > 0)
    def _():
        wait_rows(1 - slot, _aligned(tab_ref[2 * N_EXPERTS + 1]))

    @pl.when(i == pl.num_programs(0) - 1)
    def _():
        wait_rows(slot, _aligned(tab_ref[2 * N_EXPERTS]))

    @pl.when(i == 0)
    def _():
        for_zero_runs(lambda cp: cp.wait())


def _dispatch(tab, ztab, jpos_t, hx2, xs_rows):
    n, d = hx2.shape
    col = lambda i: (0, i)
    return pl.pallas_call(
        _dispatch_kernel,
        grid=(n // TOKEN_TILE,),
        in_specs=[pl.BlockSpec((TAB_WIDTH,), lambda i: (i,), memory_space=pltpu.SMEM),
                  pl.BlockSpec(memory_space=pltpu.SMEM),
                  pl.BlockSpec((TOP_K, TOKEN_TILE), col),
                  pl.BlockSpec((TOKEN_TILE, d), lambda i: (i, 0))],
        out_specs=pl.BlockSpec(memory_space=pl.ANY),
        out_shape=jax.ShapeDtypeStruct((xs_rows, d), BF16),
        scratch_shapes=[pltpu.VMEM((2, SORTED_ROWS, d), BF16),
                        pltpu.VMEM((EXPERT_BLOCK, d), BF16),
                        pltpu.SemaphoreType.DMA((2,)),
                        pltpu.SemaphoreType.DMA],
        compiler_params=_params(("arbitrary",)),
        name="moe_dispatch",
    )(tab, ztab, jpos_t, hx2)


def _experts_kernel(be_ref, nb_ref, xs_ref, wgu_ref, wd_ref, ys_ref, wgu_b, wd_b):
    i = pl.program_id(0)
    ff = wd_b.shape[0]
    used = i < nb_ref[0]
    new_expert = (i == 0) | (be_ref[i] != be_ref[jnp.maximum(i - 1, 0)])

    @pl.when(used & new_expert)
    def _():
        wgu_b[...] = wgu_ref[0, 0].astype(BF16)
        wd_b[...] = wd_ref[0, 0].astype(BF16)

    @pl.when(used)
    def _():
        starts = range(0, xs_ref.shape[0], MIX_SUB)
        hs = [jnp.dot(xs_ref[r0:r0 + MIX_SUB, :], wgu_b[...], preferred_element_type=F32) for r0 in starts]
        for r0, h in zip(starts, hs):
            a = _silu(h[:, :ff]) * h[:, ff:]
            ys_ref[r0:r0 + MIX_SUB, :] = jnp.dot(a.astype(BF16), wd_b[...],
                                                 preferred_element_type=F32).astype(BF16)

    @pl.when(jnp.logical_not(used))
    def _():
        ys_ref[...] = jnp.zeros_like(ys_ref)


def _experts(block_e, n_used, xs, w_gu, w_down, layer):
    rows, d = xs.shape
    nb = rows // EXPERT_BLOCK
    blk = lambda i, be, nu: (jnp.maximum(jnp.minimum(i, nu[0] - 1), 0), 0)
    wmap = lambda i, be, nu: (layer, be[i], 0, 0)
    return pl.pallas_call(
        _experts_kernel,
        grid_spec=pltpu.PrefetchScalarGridSpec(
            num_scalar_prefetch=2,
            grid=(nb,),
            in_specs=[pl.BlockSpec((EXPERT_BLOCK, d), blk),
                      pl.BlockSpec((1, 1) + w_gu.shape[2:], wmap),
                      pl.BlockSpec((1, 1) + w_down.shape[2:], wmap)],
            out_specs=pl.BlockSpec((EXPERT_BLOCK, d), lambda i, be, nu: (i, 0)),
            scratch_shapes=[pltpu.VMEM(w_gu.shape[2:], BF16), pltpu.VMEM(w_down.shape[2:], BF16)]),
        out_shape=jax.ShapeDtypeStruct((rows, d), BF16),
        compiler_params=_params(("arbitrary",)),
        name="moe_experts",
    )(block_e, n_used, xs, w_gu, w_down)


def _combine_kernel(tab_ref, jpos_ref, gw_ref, ys_hbm, hx2_ref, x1_ref, mod_ref, wsgu_ref, wsd_ref, fg_ref,
                    out_ref, ybuf, sem, *, tiles_per_seq, final_norm):
    t_rows, d = x1_ref.shape
    ff = wsd_ref.shape[0]
    rows = ybuf.shape[0]
    row = pl.program_id(0) // tiles_per_seq

    @pl.when(pl.program_id(0) == 0)
    def _():
        ybuf[...] = jnp.zeros_like(ybuf)

    _run_copies(tab_ref, lambda loc, glob, n: pltpu.make_async_copy(
        ys_hbm.at[pl.ds(glob, n)], ybuf.at[pl.ds(loc, n)], sem))

    hs = jnp.dot(hx2_ref[...].astype(BF16), wsgu_ref[...], preferred_element_type=F32)
    shared = jnp.dot((_silu(hs[:, :ff]) * hs[:, ff:]).astype(BF16), wsd_ref[...], preferred_element_type=F32)

    jp = jpos_ref[...]
    gw = gw_ref[...]
    iota_j = lax.broadcasted_iota(jnp.int32, (t_rows, rows), 1)
    g = jnp.zeros((t_rows, rows), F32)
    for k in range(TOP_K):
        g = jnp.where(iota_j == jp[:, k:k + 1], gw[:, k:k + 1], g)

    total = _aligned(tab_ref[2 * N_EXPERTS])
    pltpu.make_async_copy(ys_hbm.at[pl.ds(0, total)], ybuf.at[pl.ds(0, total)], sem).wait()
    routed = jnp.dot(g.astype(BF16), ybuf[...], preferred_element_type=F32)
    x2 = x1_ref[...] + _mod_chunk(mod_ref, row, 5, d) * (routed + shared)
    if final_norm:
        x2 = x2 * lax.rsqrt(jnp.mean(x2 * x2, axis=-1, keepdims=True) + NORM_EPS) * fg_ref[...]
    out_ref[...] = x2


def _combine(tab, jpos, gw, ys, hx2, x1, mod, ws_gu, ws_down, fg, tiles_per_seq, final_norm):
    n, d = x1.shape
    row = lambda i: (i, 0)
    const = lambda i: (0, 0)
    return pl.pallas_call(
        functools.partial(_combine_kernel, tiles_per_seq=tiles_per_seq, final_norm=final_norm),
        grid=(n // TOKEN_TILE,),
        in_specs=[pl.BlockSpec((TAB_WIDTH,), lambda i: (i,), memory_space=pltpu.SMEM),
                  pl.BlockSpec((TOKEN_TILE, TOP_K), row),
                  pl.BlockSpec((TOKEN_TILE, TOP_K), row),
                  pl.BlockSpec(memory_space=pl.ANY),
                  pl.BlockSpec((TOKEN_TILE, d), row),
                  pl.BlockSpec((TOKEN_TILE, d), row),
                  pl.BlockSpec(mod.shape, const),
                  pl.BlockSpec(ws_gu.shape, const),
                  pl.BlockSpec(ws_down.shape, const),
                  pl.BlockSpec((1, d), const)],
        out_specs=pl.BlockSpec((TOKEN_TILE, d), row),
        out_shape=jax.ShapeDtypeStruct((n, d), F32),
        scratch_shapes=[pltpu.VMEM((SORTED_ROWS, d), BF16), pltpu.SemaphoreType.DMA],
        compiler_params=_params(("arbitrary",)),
        name="moe_combine",
    )(tab, jpos, gw, ys, hx2, x1, mod, ws_gu, ws_down, fg)


def _moe(x1, hx2, eidx_t, gw_t, mask_t, cnt, mod, w_gu, w_down, layer, ws_gu, ws_down, fg, tiles_per_seq,
         final_norm):
    n, d = x1.shape
    n_tiles = n // TOKEN_TILE
    counts = cnt[:, 0].astype(jnp.int32)
    padded = (counts + EXPERT_BLOCK - 1) // EXPERT_BLOCK * EXPERT_BLOCK
    pad_end = jnp.cumsum(padded)
    pad_start = pad_end - padded
    max_rows = n * TOP_K + n_tiles * N_EXPERTS * (ROW_ALIGN - 1) + N_EXPERTS * (EXPERT_BLOCK - ROW_ALIGN)
    n_blocks = (max_rows + EXPERT_BLOCK - 1) // EXPERT_BLOCK
    n_used = (pad_end[-1:] // EXPERT_BLOCK).astype(jnp.int32)
    block_start = jnp.arange(n_blocks, dtype=jnp.int32) * EXPERT_BLOCK
    block_e = jnp.minimum(jnp.sum((pad_end[None, :] <= block_start[:, None]).astype(jnp.int32), axis=1),
                          N_EXPERTS - 1)

    jpos_t, tab_f = _rank(mask_t, eidx_t, pad_start.astype(F32).reshape(N_EXPERTS, 1))
    run_len = tab_f[:, :, 0].astype(jnp.int32)
    run_start = tab_f[:, :, 1].astype(jnp.int32)
    total = jnp.sum(run_len, axis=1, keepdims=True)
    prev_total = jnp.concatenate([jnp.zeros((1, 1), jnp.int32), total[:-1]], axis=0)
    fill = jnp.zeros((n_tiles, TAB_WIDTH - 2 * N_EXPERTS - 2), jnp.int32)
    tab = jnp.concatenate([run_len, run_start, total, prev_total, fill], axis=1).reshape(-1)
    ztab = jnp.concatenate([padded - counts, pad_start + counts, n_used])

    xs = _dispatch(tab, ztab, jpos_t, hx2, n_blocks * EXPERT_BLOCK)
    ys = _experts(block_e, n_used, xs, w_gu, w_down, layer)
    return _combine(tab, jpos_t.T, gw_t.T, ys, hx2, x1, mod, ws_gu, ws_down, fg, tiles_per_seq, final_norm)


def _sincos_2d(rows, d):
    quarter = d // 4
    omega = 1.0 / (POS_BASE ** (jnp.arange(quarter, dtype=F32) / quarter))

    def emb(n):
        p = jnp.arange(n, dtype=F32)[:, None] * omega[None, :]
        return jnp.concatenate([jnp.sin(p), jnp.cos(p)], axis=-1)

    er, ec = emb(rows), emb(GRID_W)
    pe = jnp.concatenate([jnp.broadcast_to(er[:, None, :], (rows, GRID_W, d // 2)),
                          jnp.broadcast_to(ec[None, :, :], (rows, GRID_W, d // 2))], axis=-1)
    return pe.reshape(rows * GRID_W, d)


def kernel(x, c, ctx, c_ctx, ada_w, ada_b, mix_norm_g, ffn_norm_g, a_w_in, a_conv_w, a_conv_b, a_gate_r_w, a_gate_r_b, a_gate_i_w, a_gate_i_b, a_lambda, a_w_out, b_w_in, b_ln_g, b_ln_b, b_w_s, b_b_s, b_w_out, router_w, router_b, moe_w_gu, moe_w_down, shared_w_gu, shared_w_down, final_norm_g):
    bsz, s, d = x.shape
    ctx_len = ctx.shape[1]
    depth = ada_w.shape[0]
    assert depth == 2 and bsz < MOD_ROWS and s % MIX_TILE == 0 and MIX_TILE % TOKEN_TILE == 0
    assert ctx_len % TOKEN_TILE == 0
    n = bsz * s
    tps = s // TOKEN_TILE
    ctx_row = bsz

    cc = jnp.zeros((MOD_ROWS, d), F32).at[:bsz].set(c).at[ctx_row].set(c_ctx)
    mod = _modulation(cc, ada_w, ada_b)
    pe = _sincos_2d(s // GRID_W, d)
    rc = a_w_in.shape[2] // 2

    w_in0 = a_w_in[0].astype(BF16)
    g_mix0 = mix_norm_g[0].reshape(1, d)
    x0, gate, ux = _rglru_in(x.reshape(n, d), pe, mod[0], g_mix0, w_in0)
    uc = _ctx_in(ctx.reshape(bsz * ctx_len, d), mod[0], g_mix0, w_in0[:, rc:], ctx_row)
    w_ri = jnp.concatenate([a_gate_r_w[0], a_gate_i_w[0]], axis=-1).astype(BF16)
    conv = (a_conv_w[0], a_conv_b[0])
    gates = [(w_ri[k], a_gate_r_b[0, k], a_gate_i_b[0, k], a_lambda[0, k]) for k in range(2)]
    h_zero = jnp.zeros((bsz, rc), F32)
    h_fwd, uc = _lru_scan(uc, conv, *gates[0], h_zero, reverse=False, reset_first=True, emit_y=False)
    h_rev = _lru_scan(uc, None, *gates[1], h_zero, reverse=True, reset_first=True, emit_y=False)
    y_fwd, ux = _lru_scan(ux, conv, *gates[0], h_fwd, reverse=False, reset_first=False, emit_y=True)
    y_rev = _lru_scan(ux, None, *gates[1], h_rev, reverse=True, reset_first=False, emit_y=True)
    pre = _rglru_out(y_fwd, y_rev, gate, x0, mod[0], a_w_out[0].astype(BF16), ffn_norm_g[0].reshape(1, d),
                     router_w[0].T, router_b[0].reshape(N_EXPERTS, 1), s // MIX_TILE)
    x1 = _moe(*pre, mod[0], moe_w_gu, moe_w_down, 0, shared_w_gu[0].astype(BF16),
              shared_w_down[0].astype(BF16), final_norm_g.reshape(1, d), tps, False)

    pre = _sgu(x1, mod[1], mix_norm_g[1].reshape(1, d), b_w_in[0].astype(BF16),
               b_ln_g[0].reshape(1, -1), b_ln_b[0].reshape(1, -1), b_w_s[0].astype(BF16), b_b_s[0].T,
               b_w_out[0].astype(BF16), ffn_norm_g[1].reshape(1, d),
               router_w[1].T, router_b[1].reshape(N_EXPERTS, 1), s // MIX_TILE)
    out = _moe(*pre, mod[1], moe_w_gu, moe_w_down, 1, shared_w_gu[1].astype(BF16),
               shared_w_down[1].astype(BF16), final_norm_g.reshape(1, d), tps, True)
    return out.reshape(bsz, s, d)
```

```python
import functools

import jax
import jax.numpy as jnp
from jax import lax
from jax.experimental import pallas as pl
from jax.experimental.pallas import tpu as pltpu

F32 = jnp.float32
BF16 = jnp.bfloat16
HIGHEST = lax.Precision.HIGHEST

GRID_W = 64
N_MOD = 6
NORM_EPS = 1e-6
POS_BASE = 10000.0
RNN_HEADS = 5
CONV_WIDTH = 4
CONV_PAD_LEFT = 2
LRU_C = 8.0
SGU_HEADS = 8
CHUNK = 128
N_EXPERTS = 64
TOP_K = 8
N_GROUPS = 8
TOPK_GROUPS = 4
EXPERTS_PER_GROUP = N_EXPERTS // N_GROUPS
ROUTED_SCALE = 2.5

SUBLANES = 8
ROW_ALIGN = 16
LANES = 128
MOD_ROWS = 8
TOKEN_TILE = 256
MIX_TILE = 512
MIX_SUB = 256
EXPERT_BLOCK = 512
SORTED_ROWS = TOKEN_TILE * TOP_K + N_EXPERTS * ROW_ALIGN
PERM_CHUNK = 512
TAB_WIDTH = 256
VMEM_LIMIT = 56 * 1024 * 1024


def _params(semantics, vmem=VMEM_LIMIT):
    return pltpu.CompilerParams(dimension_semantics=semantics, vmem_limit_bytes=vmem)


def _silu(x):
    return x * jax.nn.sigmoid(x)


def _rms_mod(x, g, sc, sh):
    y = x * lax.rsqrt(jnp.mean(x * x, axis=-1, keepdims=True) + NORM_EPS)
    return (y * g) * (1.0 + sc) + sh


def _mod_chunk(mod_ref, row, k, d):
    return mod_ref[pl.ds(row, 1), k * d:(k + 1) * d]


def _round_up_rows(count):
    return jnp.maximum(jnp.ceil(count * (1.0 / ROW_ALIGN)), 1.0) * float(ROW_ALIGN)


def _mod_kernel(cc_ref, w_ref, b_ref, o_ref):
    s = _silu(cc_ref[...])
    o_ref[0] = jnp.dot(s, w_ref[0], preferred_element_type=F32, precision=HIGHEST) + b_ref[0]


def _modulation(cc, ada_w, ada_b):
    depth, d, nd = ada_w.shape
    return pl.pallas_call(
        _mod_kernel,
        grid=(depth, nd // d),
        in_specs=[pl.BlockSpec((MOD_ROWS, d), lambda l, j: (0, 0)),
                  pl.BlockSpec((1, d, d), lambda l, j: (l, 0, j)),
                  pl.BlockSpec((1, 1, d), lambda l, j: (l, 0, j))],
        out_specs=pl.BlockSpec((1, MOD_ROWS, d), lambda l, j: (l, 0, j)),
        out_shape=jax.ShapeDtypeStruct((depth, MOD_ROWS, nd), F32),
        compiler_params=_params(("arbitrary", "arbitrary")),
        name="modulation",
    )(cc, ada_w, ada_b.reshape(depth, 1, nd))


def _rglru_in_kernel(x_ref, pe_ref, mod_ref, g_ref, w_ref, x0_ref, gate_ref, u_ref, *, tiles_per_seq):
    d = x_ref.shape[1]
    c = u_ref.shape[1]
    row = pl.program_id(0) // tiles_per_seq
    x = x_ref[...] + pe_ref[...]
    hx = _rms_mod(x, g_ref[...], _mod_chunk(mod_ref, row, 1, d), _mod_chunk(mod_ref, row, 0, d))
    z = jnp.dot(hx.astype(BF16), w_ref[...], preferred_element_type=F32)
    x0_ref[...] = x
    gate_ref[...] = jax.nn.gelu(z[:, :c]).astype(BF16)
    u_ref[...] = z[:, c:]


def _rglru_in(x2, pe, mod, g, w_in):
    n, d = x2.shape
    c = w_in.shape[1] // 2
    s = pe.shape[0]
    tps = s // MIX_TILE
    row = lambda i: (i, 0)
    return pl.pallas_call(
        functools.partial(_rglru_in_kernel, tiles_per_seq=tps),
        grid=(n // MIX_TILE,),
        in_specs=[pl.BlockSpec((MIX_TILE, d), row),
                  pl.BlockSpec((MIX_TILE, d), lambda i: (i % tps, 0)),
                  pl.BlockSpec(mod.shape, lambda i: (0, 0)),
                  pl.BlockSpec((1, d), lambda i: (0, 0)),
                  pl.BlockSpec(w_in.shape, lambda i: (0, 0))],
        out_specs=[pl.BlockSpec((MIX_TILE, d), row),
                   pl.BlockSpec((MIX_TILE, c), row),
                   pl.BlockSpec((MIX_TILE, c), row)],
        out_shape=[jax.ShapeDtypeStruct((n, d), F32),
                   jax.ShapeDtypeStruct((n, c), BF16),
                   jax.ShapeDtypeStruct((n, c), F32)],
        compiler_params=_params(("arbitrary",)),
        name="rglru_in",
    )(x2, pe, mod, g, w_in)


def _ctx_in_kernel(x_ref, mod_ref, g_ref, w_ref, u_ref, *, ctx_row):
    d = x_ref.shape[1]
    hx = _rms_mod(x_ref[...], g_ref[...], _mod_chunk(mod_ref, ctx_row, 1, d),
                  _mod_chunk(mod_ref, ctx_row, 0, d))
    u_ref[...] = jnp.dot(hx.astype(BF16), w_ref[...], preferred_element_type=F32)


def _ctx_in(c2, mod, g, w_u, ctx_row):
    n, d = c2.shape
    c = w_u.shape[1]
    return pl.pallas_call(
        functools.partial(_ctx_in_kernel, ctx_row=ctx_row),
        grid=(n // TOKEN_TILE,),
        in_specs=[pl.BlockSpec((TOKEN_TILE, d), lambda i: (i, 0)),
                  pl.BlockSpec(mod.shape, lambda i: (0, 0)),
                  pl.BlockSpec((1, d), lambda i: (0, 0)),
                  pl.BlockSpec(w_u.shape, lambda i: (0, 0))],
        out_specs=pl.BlockSpec((TOKEN_TILE, c), lambda i: (i, 0)),
        out_shape=jax.ShapeDtypeStruct((n, c), F32),
        compiler_params=_params(("arbitrary",)),
        name="ctx_in",
    )(c2, mod, g, w_u)


def _log_sigmoid(x):
    return jnp.minimum(x, 0.0) - jnp.log1p(jnp.exp(-jnp.abs(x)))


def _lru_scan_kernel(*refs, n_tiles, reverse, conv_done, reset_first, emit_y):
    if conv_done:
        u_ref, wri_ref, rb_ref, ib_ref, lam_ref, h0_ref, out_ref, a_scr, b_scr, h_scr = refs
    else:
        (u_ref, up_ref, un_ref, cw_ref, cb_ref, wri_ref, rb_ref, ib_ref, lam_ref, h0_ref,
         out_ref, uc_ref, ubuf, a_scr, b_scr, h_scr) = refs
    t_rows, c = u_ref.shape
    hb = c // RNN_HEADS
    b = pl.program_id(0)
    j = pl.program_id(1)
    jj = n_tiles - 1 - j if reverse else j

    if conv_done:
        u = u_ref[...]
    else:
        ubuf[SUBLANES:SUBLANES + t_rows, :] = u_ref[...]
        ubuf[0:SUBLANES, :] = jnp.where(jj == 0, 0.0, up_ref[...])
        ubuf[SUBLANES + t_rows:, :] = jnp.where(jj == n_tiles - 1, 0.0, un_ref[...])
        u = cb_ref[...]
        for k in range(CONV_WIDTH):
            u = u + cw_ref[k:k + 1, :] * ubuf[pl.ds(SUBLANES - CONV_PAD_LEFT + k, t_rows), :]
        uc_ref[...] = u

    log_lam = LRU_C * _log_sigmoid(lam_ref[...])
    rows = lax.broadcasted_iota(jnp.int32, (t_rows, 1), 0)
    first_row = jnp.where(j == 0, t_rows - 1 if reverse else 0, -1)
    for h in range(RNN_HEADS):
        sl = slice(h * hb, (h + 1) * hb)
        uh = u[:, sl]
        z = jnp.dot(uh.astype(BF16), wri_ref[h], preferred_element_type=F32)
        r = jax.nn.sigmoid(z[:, :hb] + rb_ref[:, sl])
        ig = jax.nn.sigmoid(z[:, hb:] + ib_ref[:, sl])
        log_a = r * log_lam[:, sl]
        a = jnp.exp(log_a)
        mult = jnp.sqrt((1.0 - a) * (1.0 + a))
        if reset_first:
            mult = jnp.where(rows == first_row, 1.0, mult)
        a_scr[:, sl] = a
        b_scr[:, sl] = mult * ig * uh

    @pl.when(j == 0)
    def _():
        h_scr[...] = h0_ref[pl.ds(b, 1), :]

    n_groups = t_rows // SUBLANES

    def group(g, h):
        base = pl.multiple_of((n_groups - 1 - g if reverse else g) * SUBLANES, SUBLANES)
        for s in range(SUBLANES):
            r = base + (SUBLANES - 1 - s if reverse else s)
            h = a_scr[pl.ds(r, 1), :] * h + b_scr[pl.ds(r, 1), :]
            if emit_y:
                out_ref[pl.ds(r, 1), :] = h
        return h

    h = lax.fori_loop(0, n_groups, group, h_scr[...])
    h_scr[...] = h
    if not emit_y:
        @pl.when(j == n_tiles - 1)
        def _():
            out_ref[pl.ds(b, 1), :] = h


def _lru_scan(u, conv, w_ri, r_b, i_b, lam, h0, *, reverse, reset_first, emit_y):
    n, c = u.shape
    n_batch = h0.shape[0]
    n_tiles = n // n_batch // TOKEN_TILE
    sub = TOKEN_TILE // SUBLANES
    n_sub = n // SUBLANES
    const = lambda b, j: (0, 0)

    def tile(b, j):
        return b * n_tiles + (n_tiles - 1 - j if reverse else j)

    tile_spec = pl.BlockSpec((TOKEN_TILE, c), lambda b, j: (tile(b, j), 0))
    in_specs = [tile_spec]
    args = [u]
    scratch = []
    if conv is not None:
        conv_w, conv_b = conv
        in_specs += [pl.BlockSpec((SUBLANES, c), lambda b, j: (jnp.maximum(tile(b, j) * sub - 1, 0), 0)),
                     pl.BlockSpec((SUBLANES, c), lambda b, j: (jnp.minimum((tile(b, j) + 1) * sub, n_sub - 1), 0)),
                     pl.BlockSpec(conv_w.shape, const),
                     pl.BlockSpec((1, c), const)]
        args += [u, u, conv_w, conv_b.reshape(1, c)]
        scratch = [pltpu.VMEM((TOKEN_TILE + 2 * SUBLANES, c), F32)]
    in_specs += [pl.BlockSpec(w_ri.shape, lambda b, j: (0, 0, 0)),
                 pl.BlockSpec((1, c), const), pl.BlockSpec((1, c), const), pl.BlockSpec((1, c), const),
                 pl.BlockSpec(h0.shape, const)]
    args += [w_ri, r_b.reshape(1, c), i_b.reshape(1, c), lam.reshape(1, c), h0]
    if emit_y:
        out_specs = [tile_spec]
        out_shape = [jax.ShapeDtypeStruct((n, c), F32)]
    else:
        out_specs = [pl.BlockSpec(h0.shape, const)]
        out_shape = [jax.ShapeDtypeStruct(h0.shape, F32)]
    if conv is not None:
        out_specs.append(tile_spec)
        out_shape.append(jax.ShapeDtypeStruct((n, c), F32))
    name = ("lru_scan" if emit_y else "lru_ctx") + ("_rev" if reverse else "_fwd")
    outs = pl.pallas_call(
        functools.partial(_lru_scan_kernel, n_tiles=n_tiles, reverse=reverse, conv_done=conv is None,
                          reset_first=reset_first, emit_y=emit_y),
        grid=(n_batch, n_tiles),
        in_specs=in_specs,
        out_specs=out_specs,
        out_shape=out_shape,
        scratch_shapes=scratch + [pltpu.VMEM((TOKEN_TILE, c), F32),
                                  pltpu.VMEM((TOKEN_TILE, c), F32),
                                  pltpu.VMEM((1, c), F32)],
        compiler_params=_params(("arbitrary", "arbitrary")),
        name=name,
    )(*args)
    return outs if conv is not None else outs[0]


def _route(logits, rb):
    e, t = logits.shape
    neg = -jnp.inf
    scores = jax.nn.sigmoid(logits)
    sel = scores + rb
    iota_g = lax.broadcasted_iota(jnp.int32, (N_GROUPS, t), 0).astype(F32)
    iota_e = lax.broadcasted_iota(jnp.int32, (e, t), 0).astype(F32)

    gs = jnp.full((N_GROUPS, t), neg, F32)
    for g in range(N_GROUPS):
        sg = sel[g * EXPERTS_PER_GROUP:(g + 1) * EXPERTS_PER_GROUP, :]
        m1 = jnp.max(sg, axis=0, keepdims=True)
        i1 = jnp.min(jnp.where(sg == m1, iota_g, float(EXPERTS_PER_GROUP)), axis=0, keepdims=True)
        m2 = jnp.max(jnp.where(iota_g == i1, neg, sg), axis=0, keepdims=True)
        gs = jnp.where(iota_g == float(g), m1 + m2, gs)

    keep = jnp.zeros((N_GROUPS, t), F32)
    for _ in range(TOPK_GROUPS):
        m = jnp.max(gs, axis=0, keepdims=True)
        idx = jnp.min(jnp.where(gs == m, iota_g, float(N_GROUPS)), axis=0, keepdims=True)
        hit = iota_g == idx
        keep = jnp.where(hit, 1.0, keep)
        gs = jnp.where(hit, neg, gs)

    masked = jnp.concatenate(
        [jnp.where(keep[g:g + 1, :] > 0.0, sel[g * EXPERTS_PER_GROUP:(g + 1) * EXPERTS_PER_GROUP, :], neg)
         for g in range(N_GROUPS)], axis=0)

    iota_k = lax.broadcasted_iota(jnp.int32, (TOP_K, t), 0)
    selmask = jnp.zeros((e, t), F32)
    eidx = jnp.zeros((TOP_K, t), F32)
    gw = jnp.zeros((TOP_K, t), F32)
    for k in range(TOP_K):
        m = jnp.max(masked, axis=0, keepdims=True)
        idx = jnp.min(jnp.where(masked == m, iota_e, float(e)), axis=0, keepdims=True)
        hit = iota_e == idx
        gk = jnp.sum(jnp.where(hit, scores, 0.0), axis=0, keepdims=True)
        masked = jnp.where(hit, neg, masked)
        selmask = jnp.where(hit, 1.0, selmask)
        eidx = jnp.where(iota_k == k, idx, eidx)
        gw = jnp.where(iota_k == k, gk, gw)
    gw = gw / jnp.sum(gw, axis=0, keepdims=True) * ROUTED_SCALE
    return eidx, gw, selmask


def _ffn_pre(x1, mod_ref, row, g2_ref, rwt_ref, rb_ref, hx2_ref, eidx_ref, gw_ref, mask_ref, cnt_ref):
    d = x1.shape[1]
    hx2 = _rms_mod(x1, g2_ref[...], _mod_chunk(mod_ref, row, 4, d), _mod_chunk(mod_ref, row, 3, d))
    hx2_ref[...] = hx2
    logits = lax.dot_general(rwt_ref[...], hx2, (((1,), (1,)), ((), ())),
                             preferred_element_type=F32, precision=HIGHEST)
    eidx, gw, selmask = _route(logits, rb_ref[...])
    eidx_ref[...] = eidx.astype(jnp.int32)
    gw_ref[...] = gw
    mask_ref[...] = selmask

    @pl.when(pl.program_id(0) == 0)
    def _():
        cnt_ref[...] = jnp.zeros_like(cnt_ref)

    rows = jnp.zeros((selmask.shape[0], 1), F32)
    for t0 in range(0, selmask.shape[1], TOKEN_TILE):
        rows = rows + _round_up_rows(jnp.sum(selmask[:, t0:t0 + TOKEN_TILE], axis=1, keepdims=True))
    cnt_ref[...] += jnp.broadcast_to(rows, cnt_ref.shape)


def _ffn_pre_specs(n, d):
    row = lambda i: (i, 0)
    col = lambda i: (0, i)
    out_specs = [pl.BlockSpec((MIX_TILE, d), row),
                 pl.BlockSpec((MIX_TILE, d), row),
                 pl.BlockSpec((TOP_K, MIX_TILE), col),
                 pl.BlockSpec((TOP_K, MIX_TILE), col),
                 pl.BlockSpec((N_EXPERTS, MIX_TILE), col),
                 pl.BlockSpec((N_EXPERTS, LANES), lambda i: (0, 0))]
    out_shape = [jax.ShapeDtypeStruct((n, d), F32),
                 jax.ShapeDtypeStruct((n, d), F32),
                 jax.ShapeDtypeStruct((TOP_K, n), jnp.int32),
                 jax.ShapeDtypeStruct((TOP_K, n), F32),
                 jax.ShapeDtypeStruct((N_EXPERTS, n), F32),
                 jax.ShapeDtypeStruct((N_EXPERTS, 128), F32)]
    return out_specs, out_shape


def _rglru_out_kernel(yf_ref, yr_ref, gate_ref, x0_ref, mod_ref, wout_ref, g2_ref, rwt_ref, rb_ref,
                      x1_ref, hx2_ref, eidx_ref, gw_ref, mask_ref, cnt_ref, *, tiles_per_seq):
    d = x0_ref.shape[1]
    row = pl.program_id(0) // tiles_per_seq
    yx = yf_ref[...] + yr_ref[...]
    v = gate_ref[...].astype(F32) * yx
    out = jnp.dot(v.astype(BF16), wout_ref[...], preferred_element_type=F32)
    x1 = x0_ref[...] + _mod_chunk(mod_ref, row, 2, d) * out
    x1_ref[...] = x1
    _ffn_pre(x1, mod_ref, row, g2_ref, rwt_ref, rb_ref, hx2_ref, eidx_ref, gw_ref, mask_ref, cnt_ref)


def _rglru_out(y_fwd, y_rev, gate, x0, mod, w_out, g2, rwt, rb, tiles_per_seq):
    n, d = x0.shape
    c = gate.shape[1]
    row = lambda i: (i, 0)
    const = lambda i: (0, 0)
    out_specs, out_shape = _ffn_pre_specs(n, d)
    return pl.pallas_call(
        functools.partial(_rglru_out_kernel, tiles_per_seq=tiles_per_seq),
        grid=(n // MIX_TILE,),
        in_specs=[pl.BlockSpec((MIX_TILE, c), row),
                  pl.BlockSpec((MIX_TILE, c), row),
                  pl.BlockSpec((MIX_TILE, c), row),
                  pl.BlockSpec((MIX_TILE, d), row),
                  pl.BlockSpec(mod.shape, const),
                  pl.BlockSpec(w_out.shape, const),
                  pl.BlockSpec((1, d), const),
                  pl.BlockSpec(rwt.shape, const),
                  pl.BlockSpec(rb.shape, const)],
        out_specs=out_specs,
        out_shape=out_shape,
        compiler_params=_params(("arbitrary",)),
        name="rglru_out",
    )(y_fwd, y_rev, gate, x0, mod, w_out, g2, rwt, rb)


def _sgu_kernel(x_ref, mod_ref, g_ref, win_ref, lng_ref, lnb_ref, ws_ref, bst_ref, wout_ref,
                g2_ref, rwt_ref, rb_ref,
                x1_ref, hx2_ref, eidx_ref, gw_ref, mask_ref, cnt_ref, m_scr, *, tiles_per_seq):
    t_rows, d = x_ref.shape
    w = wout_ref.shape[0]
    gd = w // SGU_HEADS
    row = pl.program_id(0) // tiles_per_seq
    x = x_ref[...]
    hx = _rms_mod(x, g_ref[...], _mod_chunk(mod_ref, row, 1, d), _mod_chunk(mod_ref, row, 0, d))
    z = jax.nn.gelu(jnp.dot(hx.astype(BF16), win_ref[...], preferred_element_type=F32))
    u = z[:, :w]
    v = z[:, w:]
    mu = jnp.mean(v, axis=-1, keepdims=True)
    vc = v - mu
    v = vc * lax.rsqrt(jnp.mean(vc * vc, axis=-1, keepdims=True) + NORM_EPS) * lng_ref[...] + lnb_ref[...]
    vb = v.astype(BF16)
    for ch in range(t_rows // CHUNK):
        rs = slice(ch * CHUNK, (ch + 1) * CHUNK)
        for g in range(SGU_HEADS):
            cs = slice(g * gd, (g + 1) * gd)
            sv = jnp.dot(ws_ref[g], vb[rs, cs], preferred_element_type=F32) + bst_ref[:, g:g + 1]
            m_scr[rs, cs] = (u[rs, cs] * sv).astype(BF16)
    out = jnp.dot(m_scr[...], wout_ref[...], preferred_element_type=F32)
    x1 = x + _mod_chunk(mod_ref, row, 2, d) * out
    x1_ref[...] = x1
    _ffn_pre(x1, mod_ref, row, g2_ref, rwt_ref, rb_ref, hx2_ref, eidx_ref, gw_ref, mask_ref, cnt_ref)


def _sgu(x, mod, g, w_in, ln_g, ln_b, w_s, b_st, w_out, g2, rwt, rb, tiles_per_seq):
    n, d = x.shape
    w = w_out.shape[0]
    const = lambda i: (0, 0)
    out_specs, out_shape = _ffn_pre_specs(n, d)
    return pl.pallas_call(
        functools.partial(_sgu_kernel, tiles_per_seq=tiles_per_seq),
        grid=(n // MIX_TILE,),
        in_specs=[pl.BlockSpec((MIX_TILE, d), lambda i: (i, 0)),
                  pl.BlockSpec(mod.shape, const),
                  pl.BlockSpec((1, d), const),
                  pl.BlockSpec(w_in.shape, const, pipeline_mode=pl.Buffered(1)),
                  pl.BlockSpec((1, w), const),
                  pl.BlockSpec((1, w), const),
                  pl.BlockSpec(w_s.shape, lambda i: (0, 0, 0)),
                  pl.BlockSpec(b_st.shape, const),
                  pl.BlockSpec(w_out.shape, const, pipeline_mode=pl.Buffered(1)),
                  pl.BlockSpec((1, d), const),
                  pl.BlockSpec(rwt.shape, const),
                  pl.BlockSpec(rb.shape, const)],
        out_specs=out_specs,
        out_shape=out_shape,
        scratch_shapes=[pltpu.VMEM((MIX_TILE, w), BF16)],
        compiler_params=_params(("arbitrary",)),
        name="sgu",
    )(x, mod, g, w_in, ln_g, ln_b, w_s, b_st, w_out, g2, rwt, rb)


def _rank_kernel(mask_ref, eidx_ref, pstart_ref, jpos_ref, tab_ref, carry):
    e, t = mask_ref.shape

    @pl.when(pl.program_id(0) == 0)
    def _():
        carry[...] = jnp.zeros_like(carry)

    m = mask_ref[...]
    r = lax.broadcasted_iota(jnp.int32, (t, t), 0)
    c = lax.broadcasted_iota(jnp.int32, (t, t), 1)
    upper = jnp.where(r <= c, 1.0, 0.0).astype(BF16)
    incl = jnp.dot(m.astype(BF16), upper, preferred_element_type=F32)
    run = _round_up_rows(incl[:, t - 1:t])
    lanes = tab_ref.shape[2]
    re = lax.broadcasted_iota(jnp.int32, (e, e), 0)
    ce = lax.broadcasted_iota(jnp.int32, (e, e), 1)
    lower = jnp.where(ce < re, 1.0, 0.0).astype(BF16)
    tiles = jnp.broadcast_to(run * (1.0 / ROW_ALIGN), (e, lanes)).astype(BF16)
    lstart = jnp.dot(lower, tiles, preferred_element_type=F32)[:, 0:1] * float(ROW_ALIGN)
    pos = lstart + incl - m
    iota_e = lax.broadcasted_iota(jnp.int32, (e, t), 0)
    iota_k = lax.broadcasted_iota(jnp.int32, (TOP_K, t), 0)
    eidx = eidx_ref[...]
    jpos = jnp.zeros((TOP_K, t), F32)
    for k in range(TOP_K):
        jk = jnp.sum(jnp.where(iota_e == eidx[k:k + 1, :], pos, 0.0), axis=0, keepdims=True)
        jpos = jnp.where(iota_k == k, jk, jpos)
    jpos_ref[...] = jpos.astype(jnp.int32)
    lane = lax.broadcasted_iota(jnp.int32, (e, lanes), 1)
    tab_ref[0] = jnp.where(lane == 0, run, jnp.where(lane == 1, pstart_ref[...] + carry[...], 0.0))
    carry[...] = carry[...] + run


def _rank(mask_t, eidx_t, pstart):
    e, n = mask_t.shape
    n_tiles = n // TOKEN_TILE
    col = lambda i: (0, i)
    return pl.pallas_call(
        _rank_kernel,
        grid=(n_tiles,),
        in_specs=[pl.BlockSpec((e, TOKEN_TILE), col),
                  pl.BlockSpec((TOP_K, TOKEN_TILE), col),
                  pl.BlockSpec((e, 1), lambda i: (0, 0))],
        out_specs=[pl.BlockSpec((TOP_K, TOKEN_TILE), col),
                   pl.BlockSpec((1, e, LANES), lambda i: (i, 0, 0))],
        out_shape=[jax.ShapeDtypeStruct((TOP_K, n), jnp.int32),
                   jax.ShapeDtypeStruct((n_tiles, e, LANES), F32)],
        scratch_shapes=[pltpu.VMEM((e, 1), F32)],
        compiler_params=_params(("arbitrary",)),
        name="moe_rank",
    )(mask_t, eidx_t, pstart)


def _aligned(v):
    return pl.multiple_of(v, ROW_ALIGN)


def _run_copies(tab_ref, make_copy):
    def body(e, off):
        rows = _aligned(tab_ref[e])
        make_copy(_aligned(off), _aligned(tab_ref[N_EXPERTS + e]), rows).start()
        return off + rows

    lax.fori_loop(0, N_EXPERTS, body, 0, unroll=True)


def _dispatch_kernel(tab_ref, ztab_ref, jpos_ref, x_ref, xs_hbm, sbuf, zbuf, sems, zsem):
    i = pl.program_id(0)
    slot = i % 2
    t = x_ref.shape[0]
    rows = sbuf.shape[1]

    def zero_copy(e):
        n = _aligned(ztab_ref[e])
        return pltpu.make_async_copy(zbuf.at[pl.ds(0, n)], xs_hbm.at[pl.ds(_aligned(ztab_ref[N_EXPERTS + e]), n)], zsem)

    def for_zero_runs(fn):
        def body(e, c):
            @pl.when(ztab_ref[e] > 0)
            def _():
                fn(zero_copy(e))
            return c
        lax.fori_loop(0, N_EXPERTS, body, 0)

        def tail(b, c):
            fn(pltpu.make_async_copy(zbuf, xs_hbm.at[pl.ds(pl.multiple_of(b * EXPERT_BLOCK, EXPERT_BLOCK),
                                                           EXPERT_BLOCK)], zsem))
            return c
        lax.fori_loop(ztab_ref[2 * N_EXPERTS], xs_hbm.shape[0] // EXPERT_BLOCK, tail, 0)

    @pl.when(i == 0)
    def _():
        zbuf[...] = jnp.zeros_like(zbuf)
        for_zero_runs(lambda cp: cp.start())

    jp = jpos_ref[...]
    xb = x_ref[...].astype(BF16)
    for r0 in range(0, rows, PERM_CHUNK):
        iota_j = lax.broadcasted_iota(jnp.int32, (PERM_CHUNK, t), 0) + r0
        p = jnp.zeros((PERM_CHUNK, t), F32)
        for k in range(TOP_K):
            p = jnp.where(iota_j == jp[k:k + 1, :], 1.0, p)
        sbuf[slot, r0:r0 + PERM_CHUNK, :] = jnp.dot(p.astype(BF16), xb, preferred_element_type=F32).astype(BF16)

    _run_copies(tab_ref, lambda loc, glob, n: pltpu.make_async_copy(
        sbuf.at[slot, pl.ds(loc, n)], xs_hbm.at[pl.ds(glob, n)], sems.at[slot]))

    def wait_rows(s, n):
        pltpu.make_async_copy(sbuf.at[s, pl.ds(0, n)], xs_hbm.at[pl.ds(0, n)], sems.at[s]).wait()

    @pl.when(i > 0)
    def _():
        wait_rows(1 - slot, _aligned(tab_ref[2 * N_EXPERTS + 1]))

    @pl.when(i == pl.num_programs(0) - 1)
    def _():
        wait_rows(slot, _aligned(tab_ref[2 * N_EXPERTS]))

    @pl.when(i == 0)
    def _():
        for_zero_runs(lambda cp: cp.wait())


def _dispatch(tab, ztab, jpos_t, hx2, xs_rows):
    n, d = hx2.shape
    col = lambda i: (0, i)
    return pl.pallas_call(
        _dispatch_kernel,
        grid=(n // TOKEN_TILE,),
        in_specs=[pl.BlockSpec((TAB_WIDTH,), lambda i: (i,), memory_space=pltpu.SMEM),
                  pl.BlockSpec(memory_space=pltpu.SMEM),
                  pl.BlockSpec((TOP_K, TOKEN_TILE), col),
                  pl.BlockSpec((TOKEN_TILE, d), lambda i: (i, 0))],
        out_specs=pl.BlockSpec(memory_space=pl.ANY),
        out_shape=jax.ShapeDtypeStruct((xs_rows, d), BF16),
        scratch_shapes=[pltpu.VMEM((2, SORTED_ROWS, d), BF16),
                        pltpu.VMEM((EXPERT_BLOCK, d), BF16),
                        pltpu.SemaphoreType.DMA((2,)),
                        pltpu.SemaphoreType.DMA],
        compiler_params=_params(("arbitrary",)),
        name="moe_dispatch",
    )(tab, ztab, jpos_t, hx2)


def _experts_kernel(be_ref, nb_ref, xs_ref, wgu_ref, wd_ref, ys_ref, wgu_b, wd_b):
    i = pl.program_id(0)
    ff = wd_b.shape[0]
    used = i < nb_ref[0]
    new_expert = (i == 0) | (be_ref[i] != be_ref[jnp.maximum(i - 1, 0)])

    @pl.when(used & new_expert)
    def _():
        wgu_b[...] = wgu_ref[0, 0].astype(BF16)
        wd_b[...] = wd_ref[0, 0].astype(BF16)

    @pl.when(used)
    def _():
        starts = range(0, xs_ref.shape[0], MIX_SUB)
        hs = [jnp.dot(xs_ref[r0:r0 + MIX_SUB, :], wgu_b[...], preferred_element_type=F32) for r0 in starts]
        for r0, h in zip(starts, hs):
            a = _silu(h[:, :ff]) * h[:, ff:]
            ys_ref[r0:r0 + MIX_SUB, :] = jnp.dot(a.astype(BF16), wd_b[...],
                                                 preferred_element_type=F32).astype(BF16)

    @pl.when(jnp.logical_not(used))
    def _():
        ys_ref[...] = jnp.zeros_like(ys_ref)


def _experts(block_e, n_used, xs, w_gu, w_down, layer):
    rows, d = xs.shape
    nb = rows // EXPERT_BLOCK
    blk = lambda i, be, nu: (jnp.maximum(jnp.minimum(i, nu[0] - 1), 0), 0)
    wmap = lambda i, be, nu: (layer, be[i], 0, 0)
    return pl.pallas_call(
        _experts_kernel,
        grid_spec=pltpu.PrefetchScalarGridSpec(
            num_scalar_prefetch=2,
            grid=(nb,),
            in_specs=[pl.BlockSpec((EXPERT_BLOCK, d), blk),
                      pl.BlockSpec((1, 1) + w_gu.shape[2:], wmap),
                      pl.BlockSpec((1, 1) + w_down.shape[2:], wmap)],
            out_specs=pl.BlockSpec((EXPERT_BLOCK, d), lambda i, be, nu: (i, 0)),
            scratch_shapes=[pltpu.VMEM(w_gu.shape[2:], BF16), pltpu.VMEM(w_down.shape[2:], BF16)]),
        out_shape=jax.ShapeDtypeStruct((rows, d), BF16),
        compiler_params=_params(("arbitrary",)),
        name="moe_experts",
    )(block_e, n_used, xs, w_gu, w_down)


def _combine_kernel(tab_ref, jpos_ref, gw_ref, ys_hbm, hx2_ref, x1_ref, mod_ref, wsgu_ref, wsd_ref, fg_ref,
                    out_ref, ybuf, sem, *, tiles_per_seq, final_norm):
    t_rows, d = x1_ref.shape
    ff = wsd_ref.shape[0]
    rows = ybuf.shape[0]
    row = pl.program_id(0) // tiles_per_seq

    @pl.when(pl.program_id(0) == 0)
    def _():
        ybuf[...] = jnp.zeros_like(ybuf)

    _run_copies(tab_ref, lambda loc, glob, n: pltpu.make_async_copy(
        ys_hbm.at[pl.ds(glob, n)], ybuf.at[pl.ds(loc, n)], sem))

    hs = jnp.dot(hx2_ref[...].astype(BF16), wsgu_ref[...], preferred_element_type=F32)
    shared = jnp.dot((_silu(hs[:, :ff]) * hs[:, ff:]).astype(BF16), wsd_ref[...], preferred_element_type=F32)

    jp = jpos_ref[...]
    gw = gw_ref[...]
    iota_j = lax.broadcasted_iota(jnp.int32, (t_rows, rows), 1)
    g = jnp.zeros((t_rows, rows), F32)
    for k in range(TOP_K):
        g = jnp.where(iota_j == jp[:, k:k + 1], gw[:, k:k + 1], g)

    total = _aligned(tab_ref[2 * N_EXPERTS])
    pltpu.make_async_copy(ys_hbm.at[pl.ds(0, total)], ybuf.at[pl.ds(0, total)], sem).wait()
    routed = jnp.dot(g.astype(BF16), ybuf[...], preferred_element_type=F32)
    x2 = x1_ref[...] + _mod_chunk(mod_ref, row, 5, d) * (routed + shared)
    if final_norm:
        x2 = x2 * lax.rsqrt(jnp.mean(x2 * x2, axis=-1, keepdims=True) + NORM_EPS) * fg_ref[...]
    out_ref[...] = x2


def _combine(tab, jpos, gw, ys, hx2, x1, mod, ws_gu, ws_down, fg, tiles_per_seq, final_norm):
    n, d = x1.shape
    row = lambda i: (i, 0)
    const = lambda i: (0, 0)
    return pl.pallas_call(
        functools.partial(_combine_kernel, tiles_per_seq=tiles_per_seq, final_norm=final_norm),
        grid=(n // TOKEN_TILE,),
        in_specs=[pl.BlockSpec((TAB_WIDTH,), lambda i: (i,), memory_space=pltpu.SMEM),
                  pl.BlockSpec((TOKEN_TILE, TOP_K), row),
                  pl.BlockSpec((TOKEN_TILE, TOP_K), row),
                  pl.BlockSpec(memory_space=pl.ANY),
                  pl.BlockSpec((TOKEN_TILE, d), row),
                  pl.BlockSpec((TOKEN_TILE, d), row),
                  pl.BlockSpec(mod.shape, const),
                  pl.BlockSpec(ws_gu.shape, const),
                  pl.BlockSpec(ws_down.shape, const),
                  pl.BlockSpec((1, d), const)],
        out_specs=pl.BlockSpec((TOKEN_TILE, d), row),
        out_shape=jax.ShapeDtypeStruct((n, d), F32),
        scratch_shapes=[pltpu.VMEM((SORTED_ROWS, d), BF16), pltpu.SemaphoreType.DMA],
        compiler_params=_params(("arbitrary",)),
        name="moe_combine",
    )(tab, jpos, gw, ys, hx2, x1, mod, ws_gu, ws_down, fg)


def _moe(x1, hx2, eidx_t, gw_t, mask_t, cnt, mod, w_gu, w_down, layer, ws_gu, ws_down, fg, tiles_per_seq,
         final_norm):
    n, d = x1.shape
    n_tiles = n // TOKEN_TILE
    counts = cnt[:, 0].astype(jnp.int32)
    padded = (counts + EXPERT_BLOCK - 1) // EXPERT_BLOCK * EXPERT_BLOCK
    pad_end = jnp.cumsum(padded)
    pad_start = pad_end - padded
    max_rows = n * TOP_K + n_tiles * N_EXPERTS * ROW_ALIGN + N_EXPERTS * (EXPERT_BLOCK - ROW_ALIGN)
    n_blocks = (max_rows + EXPERT_BLOCK - 1) // EXPERT_BLOCK
    n_used = (pad_end[-1:] // EXPERT_BLOCK).astype(jnp.int32)
    block_start = jnp.arange(n_blocks, dtype=jnp.int32) * EXPERT_BLOCK
    block_e = jnp.minimum(jnp.sum((pad_end[None, :] <= block_start[:, None]).astype(jnp.int32), axis=1),
                          N_EXPERTS - 1)

    jpos_t, tab_f = _rank(mask_t, eidx_t, pad_start.astype(F32).reshape(N_EXPERTS, 1))
    run_len = tab_f[:, :, 0].astype(jnp.int32)
    run_start = tab_f[:, :, 1].astype(jnp.int32)
    total = jnp.sum(run_len, axis=1, keepdims=True)
    prev_total = jnp.concatenate([jnp.zeros((1, 1), jnp.int32), total[:-1]], axis=0)
    fill = jnp.zeros((n_tiles, TAB_WIDTH - 2 * N_EXPERTS - 2), jnp.int32)
    tab = jnp.concatenate([run_len, run_start, total, prev_total, fill], axis=1).reshape(-1)
    ztab = jnp.concatenate([padded - counts, pad_start + counts, n_used])

    xs = _dispatch(tab, ztab, jpos_t, hx2, n_blocks * EXPERT_BLOCK)
    ys = _experts(block_e, n_used, xs, w_gu, w_down, layer)
    return _combine(tab, jpos_t.T, gw_t.T, ys, hx2, x1, mod, ws_gu, ws_down, fg, tiles_per_seq, final_norm)


def _sincos_2d(rows, d):
    quarter = d // 4
    omega = 1.0 / (POS_BASE ** (jnp.arange(quarter, dtype=F32) / quarter))

    def emb(n):
        p = jnp.arange(n, dtype=F32)[:, None] * omega[None, :]
        return jnp.concatenate([jnp.sin(p), jnp.cos(p)], axis=-1)

    er, ec = emb(rows), emb(GRID_W)
    pe = jnp.concatenate([jnp.broadcast_to(er[:, None, :], (rows, GRID_W, d // 2)),
                          jnp.broadcast_to(ec[None, :, :], (rows, GRID_W, d // 2))], axis=-1)
    return pe.reshape(rows * GRID_W, d)


def kernel(x, c, ctx, c_ctx, ada_w, ada_b, mix_norm_g, ffn_norm_g, a_w_in, a_conv_w, a_conv_b, a_gate_r_w, a_gate_r_b, a_gate_i_w, a_gate_i_b, a_lambda, a_w_out, b_w_in, b_ln_g, b_ln_b, b_w_s, b_b_s, b_w_out, router_w, router_b, moe_w_gu, moe_w_down, shared_w_gu, shared_w_down, final_norm_g):
    bsz, s, d = x.shape
    ctx_len = ctx.shape[1]
    depth = ada_w.shape[0]
    assert depth == 2 and bsz < MOD_ROWS and s % MIX_TILE == 0 and MIX_TILE % TOKEN_TILE == 0
    assert ctx_len % TOKEN_TILE == 0
    n = bsz * s
    tps = s // TOKEN_TILE
    ctx_row = bsz

    cc = jnp.zeros((MOD_ROWS, d), F32).at[:bsz].set(c).at[ctx_row].set(c_ctx)
    mod = _modulation(cc, ada_w, ada_b)
    pe = _sincos_2d(s // GRID_W, d)
    rc = a_w_in.shape[2] // 2

    w_in0 = a_w_in[0].astype(BF16)
    g_mix0 = mix_norm_g[0].reshape(1, d)
    x0, gate, ux = _rglru_in(x.reshape(n, d), pe, mod[0], g_mix0, w_in0)
    uc = _ctx_in(ctx.reshape(bsz * ctx_len, d), mod[0], g_mix0, w_in0[:, rc:], ctx_row)
    w_ri = jnp.concatenate([a_gate_r_w[0], a_gate_i_w[0]], axis=-1).astype(BF16)
    conv = (a_conv_w[0], a_conv_b[0])
    gates = [(w_ri[k], a_gate_r_b[0, k], a_gate_i_b[0, k], a_lambda[0, k]) for k in range(2)]
    h_zero = jnp.zeros((bsz, rc), F32)
    h_fwd, uc = _lru_scan(uc, conv, *gates[0], h_zero, reverse=False, reset_first=True, emit_y=False)
    h_rev = _lru_scan(uc, None, *gates[1], h_zero, reverse=True, reset_first=True, emit_y=False)
    y_fwd, ux = _lru_scan(ux, conv, *gates[0], h_fwd, reverse=False, reset_first=False, emit_y=True)
    y_rev = _lru_scan(ux, None, *gates[1], h_rev, reverse=True, reset_first=False, emit_y=True)
    pre = _rglru_out(y_fwd, y_rev, gate, x0, mod[0], a_w_out[0].astype(BF16), ffn_norm_g[0].reshape(1, d),
                     router_w[0].T, router_b[0].reshape(N_EXPERTS, 1), s // MIX_TILE)
    x1 = _moe(*pre, mod[0], moe_w_gu, moe_w_down, 0, shared_w_gu[0].astype(BF16),
              shared_w_down[0].astype(BF16), final_norm_g.reshape(1, d), tps, False)

    pre = _sgu(x1, mod[1], mix_norm_g[1].reshape(1, d), b_w_in[0].astype(BF16),
               b_ln_g[0].reshape(1, -1), b_ln_b[0].reshape(1, -1), b_w_s[0].astype(BF16), b_b_s[0].T,
               b_w_out[0].astype(BF16), ffn_norm_g[1].reshape(1, d),
               router_w[1].T, router_b[1].reshape(N_EXPERTS, 1), s // MIX_TILE)
    out = _moe(*pre, mod[1], moe_w_gu, moe_w_down, 1, shared_w_gu[1].astype(BF16),
               shared_w_down[1].astype(BF16), final_norm_g.reshape(1, d), tps, True)
    return out.reshape(bsz, s, d)
```

```python
import functools

import jax
import jax.numpy as jnp
from jax import lax
from jax.experimental import pallas as pl
from jax.experimental.pallas import tpu as pltpu

F32 = jnp.float32
BF16 = jnp.bfloat16
HIGHEST = lax.Precision.HIGHEST

GRID_W = 64
N_MOD = 6
NORM_EPS = 1e-6
POS_BASE = 10000.0
RNN_HEADS = 5
CONV_WIDTH = 4
CONV_PAD_LEFT = 2
LRU_C = 8.0
SGU_HEADS = 8
CHUNK = 128
N_EXPERTS = 64
TOP_K = 8
N_GROUPS = 8
TOPK_GROUPS = 4
EXPERTS_PER_GROUP = N_EXPERTS // N_GROUPS
ROUTED_SCALE = 2.5

SUBLANES = 8
ROW_ALIGN = 16
LANES = 128
MOD_ROWS = 8
TOKEN_TILE = 256
MIX_TILE = 512
MIX_SUB = 256
EXPERT_BLOCK = 512
SORTED_ROWS = TOKEN_TILE * TOP_K + N_EXPERTS * ROW_ALIGN
PERM_CHUNK = 512
TAB_WIDTH = 256
VMEM_LIMIT = 56 * 1024 * 1024


def _params(semantics, vmem=VMEM_LIMIT):
    return pltpu.CompilerParams(dimension_semantics=semantics, vmem_limit_bytes=vmem)


def _silu(x):
    return x * jax.nn.sigmoid(x)


def _rms_mod(x, g, sc, sh):
    y = x * lax.rsqrt(jnp.mean(x * x, axis=-1, keepdims=True) + NORM_EPS)
    return (y * g) * (1.0 + sc) + sh


def _mod_chunk(mod_ref, row, k, d):
    return mod_ref[pl.ds(row, 1), k * d:(k + 1) * d]


def _round_up_rows(count):
    return jnp.maximum(jnp.ceil(count * (1.0 / ROW_ALIGN)), 1.0) * float(ROW_ALIGN)


def _mod_kernel(cc_ref, w_ref, b_ref, o_ref):
    s = _silu(cc_ref[...])
    o_ref[0] = jnp.dot(s, w_ref[0], preferred_element_type=F32, precision=HIGHEST) + b_ref[0]


def _modulation(cc, ada_w, ada_b):
    depth, d, nd = ada_w.shape
    return pl.pallas_call(
        _mod_kernel,
        grid=(depth, nd // d),
        in_specs=[pl.BlockSpec((MOD_ROWS, d), lambda l, j: (0, 0)),
                  pl.BlockSpec((1, d, d), lambda l, j: (l, 0, j)),
                  pl.BlockSpec((1, 1, d), lambda l, j: (l, 0, j))],
        out_specs=pl.BlockSpec((1, MOD_ROWS, d), lambda l, j: (l, 0, j)),
        out_shape=jax.ShapeDtypeStruct((depth, MOD_ROWS, nd), F32),
        compiler_params=_params(("arbitrary", "arbitrary")),
        name="modulation",
    )(cc, ada_w, ada_b.reshape(depth, 1, nd))


def _rglru_in_kernel(x_ref, pe_ref, mod_ref, g_ref, w_ref, x0_ref, gate_ref, u_ref, *, tiles_per_seq):
    d = x_ref.shape[1]
    c = u_ref.shape[1]
    row = pl.program_id(0) // tiles_per_seq
    x = x_ref[...] + pe_ref[...]
    hx = _rms_mod(x, g_ref[...], _mod_chunk(mod_ref, row, 1, d), _mod_chunk(mod_ref, row, 0, d))
    z = jnp.dot(hx.astype(BF16), w_ref[...], preferred_element_type=F32)
    x0_ref[...] = x
    gate_ref[...] = jax.nn.gelu(z[:, :c]).astype(BF16)
    u_ref[...] = z[:, c:]


def _rglru_in(x2, pe, mod, g, w_in):
    n, d = x2.shape
    c = w_in.shape[1] // 2
    s = pe.shape[0]
    tps = s // MIX_TILE
    row = lambda i: (i, 0)
    return pl.pallas_call(
        functools.partial(_rglru_in_kernel, tiles_per_seq=tps),
        grid=(n // MIX_TILE,),
        in_specs=[pl.BlockSpec((MIX_TILE, d), row),
                  pl.BlockSpec((MIX_TILE, d), lambda i: (i % tps, 0)),
                  pl.BlockSpec(mod.shape, lambda i: (0, 0)),
                  pl.BlockSpec((1, d), lambda i: (0, 0)),
                  pl.BlockSpec(w_in.shape, lambda i: (0, 0))],
        out_specs=[pl.BlockSpec((MIX_TILE, d), row),
                   pl.BlockSpec((MIX_TILE, c), row),
                   pl.BlockSpec((MIX_TILE, c), row)],
        out_shape=[jax.ShapeDtypeStruct((n, d), F32),
                   jax.ShapeDtypeStruct((n, c), BF16),
                   jax.ShapeDtypeStruct((n, c), F32)],
        compiler_params=_params(("arbitrary",)),
        name="rglru_in",
    )(x2, pe, mod, g, w_in)


def _ctx_in_kernel(x_ref, mod_ref, g_ref, w_ref, u_ref, *, ctx_row):
    d = x_ref.shape[1]
    hx = _rms_mod(x_ref[...], g_ref[...], _mod_chunk(mod_ref, ctx_row, 1, d),
                  _mod_chunk(mod_ref, ctx_row, 0, d))
    u_ref[...] = jnp.dot(hx.astype(BF16), w_ref[...], preferred_element_type=F32)


def _ctx_in(c2, mod, g, w_u, ctx_row):
    n, d = c2.shape
    c = w_u.shape[1]
    return pl.pallas_call(
        functools.partial(_ctx_in_kernel, ctx_row=ctx_row),
        grid=(n // TOKEN_TILE,),
        in_specs=[pl.BlockSpec((TOKEN_TILE, d), lambda i: (i, 0)),
                  pl.BlockSpec(mod.shape, lambda i: (0, 0)),
                  pl.BlockSpec((1, d), lambda i: (0, 0)),
                  pl.BlockSpec(w_u.shape, lambda i: (0, 0))],
        out_specs=pl.BlockSpec((TOKEN_TILE, c), lambda i: (i, 0)),
        out_shape=jax.ShapeDtypeStruct((n, c), F32),
        compiler_params=_params(("arbitrary",)),
        name="ctx_in",
    )(c2, mod, g, w_u)


def _log_sigmoid(x):
    return jnp.minimum(x, 0.0) - jnp.log1p(jnp.exp(-jnp.abs(x)))


def _lru_scan_kernel(*refs, n_tiles, reverse, conv_done, reset_first, emit_y):
    if conv_done:
        u_ref, wri_ref, rb_ref, ib_ref, lam_ref, h0_ref, out_ref, a_scr, b_scr, h_scr = refs
    else:
        (u_ref, up_ref, un_ref, cw_ref, cb_ref, wri_ref, rb_ref, ib_ref, lam_ref, h0_ref,
         out_ref, uc_ref, ubuf, a_scr, b_scr, h_scr) = refs
    t_rows, c = u_ref.shape
    hb = c // RNN_HEADS
    b = pl.program_id(0)
    j = pl.program_id(1)
    jj = n_tiles - 1 - j if reverse else j

    if conv_done:
        u = u_ref[...]
    else:
        ubuf[SUBLANES:SUBLANES + t_rows, :] = u_ref[...]
        ubuf[0:SUBLANES, :] = jnp.where(jj == 0, 0.0, up_ref[...])
        ubuf[SUBLANES + t_rows:, :] = jnp.where(jj == n_tiles - 1, 0.0, un_ref[...])
        u = cb_ref[...]
        for k in range(CONV_WIDTH):
            u = u + cw_ref[k:k + 1, :] * ubuf[pl.ds(SUBLANES - CONV_PAD_LEFT + k, t_rows), :]
        uc_ref[...] = u

    log_lam = LRU_C * _log_sigmoid(lam_ref[...])
    rows = lax.broadcasted_iota(jnp.int32, (t_rows, 1), 0)
    first_row = jnp.where(j == 0, t_rows - 1 if reverse else 0, -1)
    for h in range(RNN_HEADS):
        sl = slice(h * hb, (h + 1) * hb)
        uh = u[:, sl]
        z = jnp.dot(uh.astype(BF16), wri_ref[h], preferred_element_type=F32)
        r = jax.nn.sigmoid(z[:, :hb] + rb_ref[:, sl])
        ig = jax.nn.sigmoid(z[:, hb:] + ib_ref[:, sl])
        log_a = r * log_lam[:, sl]
        a = jnp.exp(log_a)
        mult = jnp.sqrt((1.0 - a) * (1.0 + a))
        if reset_first:
            mult = jnp.where(rows == first_row, 1.0, mult)
        a_scr[:, sl] = a
        b_scr[:, sl] = mult * ig * uh

    @pl.when(j == 0)
    def _():
        h_scr[...] = h0_ref[pl.ds(b, 1), :]

    n_groups = t_rows // SUBLANES

    def group(g, h):
        base = pl.multiple_of((n_groups - 1 - g if reverse else g) * SUBLANES, SUBLANES)
        for s in range(SUBLANES):
            r = base + (SUBLANES - 1 - s if reverse else s)
            h = a_scr[pl.ds(r, 1), :] * h + b_scr[pl.ds(r, 1), :]
            if emit_y:
                out_ref[pl.ds(r, 1), :] = h
        return h

    h = lax.fori_loop(0, n_groups, group, h_scr[...])
    h_scr[...] = h
    if not emit_y:
        @pl.when(j == n_tiles - 1)
        def _():
            out_ref[pl.ds(b, 1), :] = h


def _lru_scan(u, conv, w_ri, r_b, i_b, lam, h0, *, reverse, reset_first, emit_y):
    n, c = u.shape
    n_batch = h0.shape[0]
    n_tiles = n // n_batch // TOKEN_TILE
    sub = TOKEN_TILE // SUBLANES
    n_sub = n // SUBLANES
    const = lambda b, j: (0, 0)

    def tile(b, j):
        return b * n_tiles + (n_tiles - 1 - j if reverse else j)

    tile_spec = pl.BlockSpec((TOKEN_TILE, c), lambda b, j: (tile(b, j), 0))
    in_specs = [tile_spec]
    args = [u]
    scratch = []
    if conv is not None:
        conv_w, conv_b = conv
        in_specs += [pl.BlockSpec((SUBLANES, c), lambda b, j: (jnp.maximum(tile(b, j) * sub - 1, 0), 0)),
                     pl.BlockSpec((SUBLANES, c), lambda b, j: (jnp.minimum((tile(b, j) + 1) * sub, n_sub - 1), 0)),
                     pl.BlockSpec(conv_w.shape, const),
                     pl.BlockSpec((1, c), const)]
        args += [u, u, conv_w, conv_b.reshape(1, c)]
        scratch = [pltpu.VMEM((TOKEN_TILE + 2 * SUBLANES, c), F32)]
    in_specs += [pl.BlockSpec(w_ri.shape, lambda b, j: (0, 0, 0)),
                 pl.BlockSpec((1, c), const), pl.BlockSpec((1, c), const), pl.BlockSpec((1, c), const),
                 pl.BlockSpec(h0.shape, const)]
    args += [w_ri, r_b.reshape(1, c), i_b.reshape(1, c), lam.reshape(1, c), h0]
    if emit_y:
        out_specs = [tile_spec]
        out_shape = [jax.ShapeDtypeStruct((n, c), F32)]
    else:
        out_specs = [pl.BlockSpec(h0.shape, const)]
        out_shape = [jax.ShapeDtypeStruct(h0.shape, F32)]
    if conv is not None:
        out_specs.append(tile_spec)
        out_shape.append(jax.ShapeDtypeStruct((n, c), F32))
    name = ("lru_scan" if emit_y else "lru_ctx") + ("_rev" if reverse else "_fwd")
    outs = pl.pallas_call(
        functools.partial(_lru_scan_kernel, n_tiles=n_tiles, reverse=reverse, conv_done=conv is None,
                          reset_first=reset_first, emit_y=emit_y),
        grid=(n_batch, n_tiles),
        in_specs=in_specs,
        out_specs=out_specs,
        out_shape=out_shape,
        scratch_shapes=scratch + [pltpu.VMEM((TOKEN_TILE, c), F32),
                                  pltpu.VMEM((TOKEN_TILE, c), F32),
                                  pltpu.VMEM((1, c), F32)],
        compiler_params=_params(("arbitrary", "arbitrary")),
        name=name,
    )(*args)
    return outs if conv is not None else outs[0]


def _route(logits, rb):
    e, t = logits.shape
    neg = -jnp.inf
    scores = jax.nn.sigmoid(logits)
    sel = scores + rb
    iota_g = lax.broadcasted_iota(jnp.int32, (N_GROUPS, t), 0).astype(F32)
    iota_e = lax.broadcasted_iota(jnp.int32, (e, t), 0).astype(F32)

    gs = jnp.full((N_GROUPS, t), neg, F32)
    for g in range(N_GROUPS):
        sg = sel[g * EXPERTS_PER_GROUP:(g + 1) * EXPERTS_PER_GROUP, :]
        m1 = jnp.max(sg, axis=0, keepdims=True)
        i1 = jnp.min(jnp.where(sg == m1, iota_g, float(EXPERTS_PER_GROUP)), axis=0, keepdims=True)
        m2 = jnp.max(jnp.where(iota_g == i1, neg, sg), axis=0, keepdims=True)
        gs = jnp.where(iota_g == float(g), m1 + m2, gs)

    keep = jnp.zeros((N_GROUPS, t), F32)
    for _ in range(TOPK_GROUPS):
        m = jnp.max(gs, axis=0, keepdims=True)
        idx = jnp.min(jnp.where(gs == m, iota_g, float(N_GROUPS)), axis=0, keepdims=True)
        hit = iota_g == idx
        keep = jnp.where(hit, 1.0, keep)
        gs = jnp.where(hit, neg, gs)

    masked = jnp.concatenate(
        [jnp.where(keep[g:g + 1, :] > 0.0, sel[g * EXPERTS_PER_GROUP:(g + 1) * EXPERTS_PER_GROUP, :], neg)
         for g in range(N_GROUPS)], axis=0)

    iota_k = lax.broadcasted_iota(jnp.int32, (TOP_K, t), 0)
    selmask = jnp.zeros((e, t), F32)
    eidx = jnp.zeros((TOP_K, t), F32)
    gw = jnp.zeros((TOP_K, t), F32)
    for k in range(TOP_K):
        m = jnp.max(masked, axis=0, keepdims=True)
        idx = jnp.min(jnp.where(masked == m, iota_e, float(e)), axis=0, keepdims=True)
        hit = iota_e == idx
        gk = jnp.sum(jnp.where(hit, scores, 0.0), axis=0, keepdims=True)
        masked = jnp.where(hit, neg, masked)
        selmask = jnp.where(hit, 1.0, selmask)
        eidx = jnp.where(iota_k == k, idx, eidx)
        gw = jnp.where(iota_k == k, gk, gw)
    gw = gw / jnp.sum(gw, axis=0, keepdims=True) * ROUTED_SCALE
    return eidx, gw, selmask


def _ffn_pre(x1, mod_ref, row, g2_ref, rwt_ref, rb_ref, hx2_ref, eidx_ref, gw_ref, mask_ref, cnt_ref):
    d = x1.shape[1]
    hx2 = _rms_mod(x1, g2_ref[...], _mod_chunk(mod_ref, row, 4, d), _mod_chunk(mod_ref, row, 3, d))
    hx2_ref[...] = hx2
    logits = lax.dot_general(rwt_ref[...], hx2, (((1,), (1,)), ((), ())),
                             preferred_element_type=F32, precision=HIGHEST)
    eidx, gw, selmask = _route(logits, rb_ref[...])
    eidx_ref[...] = eidx.astype(jnp.int32)
    gw_ref[...] = gw
    mask_ref[...] = selmask

    @pl.when(pl.program_id(0) == 0)
    def _():
        cnt_ref[...] = jnp.zeros_like(cnt_ref)

    rows = jnp.zeros((selmask.shape[0], 1), F32)
    for t0 in range(0, selmask.shape[1], TOKEN_TILE):
        rows = rows + _round_up_rows(jnp.sum(selmask[:, t0:t0 + TOKEN_TILE], axis=1, keepdims=True))
    cnt_ref[...] += jnp.broadcast_to(rows, cnt_ref.shape)


def _ffn_pre_specs(n, d):
    row = lambda i: (i, 0)
    col = lambda i: (0, i)
    out_specs = [pl.BlockSpec((MIX_TILE, d), row),
                 pl.BlockSpec((MIX_TILE, d), row),
                 pl.BlockSpec((TOP_K, MIX_TILE), col),
                 pl.BlockSpec((TOP_K, MIX_TILE), col),
                 pl.BlockSpec((N_EXPERTS, MIX_TILE), col),
                 pl.BlockSpec((N_EXPERTS, LANES), lambda i: (0, 0))]
    out_shape = [jax.ShapeDtypeStruct((n, d), F32),
                 jax.ShapeDtypeStruct((n, d), F32),
                 jax.ShapeDtypeStruct((TOP_K, n), jnp.int32),
                 jax.ShapeDtypeStruct((TOP_K, n), F32),
                 jax.ShapeDtypeStruct((N_EXPERTS, n), F32),
                 jax.ShapeDtypeStruct((N_EXPERTS, 128), F32)]
    return out_specs, out_shape


def _rglru_out_kernel(yf_ref, yr_ref, gate_ref, x0_ref, mod_ref, wout_ref, g2_ref, rwt_ref, rb_ref,
                      x1_ref, hx2_ref, eidx_ref, gw_ref, mask_ref, cnt_ref, *, tiles_per_seq):
    d = x0_ref.shape[1]
    row = pl.program_id(0) // tiles_per_seq
    yx = yf_ref[...] + yr_ref[...]
    v = gate_ref[...].astype(F32) * yx
    out = jnp.dot(v.astype(BF16), wout_ref[...], preferred_element_type=F32)
    x1 = x0_ref[...] + _mod_chunk(mod_ref, row, 2, d) * out
    x1_ref[...] = x1
    _ffn_pre(x1, mod_ref, row, g2_ref, rwt_ref, rb_ref, hx2_ref, eidx_ref, gw_ref, mask_ref, cnt_ref)


def _rglru_out(y_fwd, y_rev, gate, x0, mod, w_out, g2, rwt, rb, tiles_per_seq):
    n, d = x0.shape
    c = gate.shape[1]
    row = lambda i: (i, 0)
    const = lambda i: (0, 0)
    out_specs, out_shape = _ffn_pre_specs(n, d)
    return pl.pallas_call(
        functools.partial(_rglru_out_kernel, tiles_per_seq=tiles_per_seq),
        grid=(n // MIX_TILE,),
        in_specs=[pl.BlockSpec((MIX_TILE, c), row),
                  pl.BlockSpec((MIX_TILE, c), row),
                  pl.BlockSpec((MIX_TILE, c), row),
                  pl.BlockSpec((MIX_TILE, d), row),
                  pl.BlockSpec(mod.shape, const),
                  pl.BlockSpec(w_out.shape, const),
                  pl.BlockSpec((1, d), const),
                  pl.BlockSpec(rwt.shape, const),
                  pl.BlockSpec(rb.shape, const)],
        out_specs=out_specs,
        out_shape=out_shape,
        compiler_params=_params(("arbitrary",)),
        name="rglru_out",
    )(y_fwd, y_rev, gate, x0, mod, w_out, g2, rwt, rb)


def _sgu_kernel(x_ref, mod_ref, g_ref, win_ref, lng_ref, lnb_ref, ws_ref, bst_ref, wout_ref,
                g2_ref, rwt_ref, rb_ref,
                x1_ref, hx2_ref, eidx_ref, gw_ref, mask_ref, cnt_ref, m_scr, *, tiles_per_seq):
    t_rows, d = x_ref.shape
    w = wout_ref.shape[0]
    gd = w // SGU_HEADS
    row = pl.program_id(0) // tiles_per_seq
    x = x_ref[...]
    hx = _rms_mod(x, g_ref[...], _mod_chunk(mod_ref, row, 1, d), _mod_chunk(mod_ref, row, 0, d))
    z = jax.nn.gelu(jnp.dot(hx.astype(BF16), win_ref[...], preferred_element_type=F32))
    u = z[:, :w]
    v = z[:, w:]
    mu = jnp.mean(v, axis=-1, keepdims=True)
    vc = v - mu
    v = vc * lax.rsqrt(jnp.mean(vc * vc, axis=-1, keepdims=True) + NORM_EPS) * lng_ref[...] + lnb_ref[...]
    vb = v.astype(BF16)
    for ch in range(t_rows // CHUNK):
        rs = slice(ch * CHUNK, (ch + 1) * CHUNK)
        for g in range(SGU_HEADS):
            cs = slice(g * gd, (g + 1) * gd)
            sv = jnp.dot(ws_ref[g], vb[rs, cs], preferred_element_type=F32) + bst_ref[:, g:g + 1]
            m_scr[rs, cs] = (u[rs, cs] * sv).astype(BF16)
    out = jnp.dot(m_scr[...], wout_ref[...], preferred_element_type=F32)
    x1 = x + _mod_chunk(mod_ref, row, 2, d) * out
    x1_ref[...] = x1
    _ffn_pre(x1, mod_ref, row, g2_ref, rwt_ref, rb_ref, hx2_ref, eidx_ref, gw_ref, mask_ref, cnt_ref)


def _sgu(x, mod, g, w_in, ln_g, ln_b, w_s, b_st, w_out, g2, rwt, rb, tiles_per_seq):
    n, d = x.shape
    w = w_out.shape[0]
    const = lambda i: (0, 0)
    out_specs, out_shape = _ffn_pre_specs(n, d)
    return pl.pallas_call(
        functools.partial(_sgu_kernel, tiles_per_seq=tiles_per_seq),
        grid=(n // MIX_TILE,),
        in_specs=[pl.BlockSpec((MIX_TILE, d), lambda i: (i, 0)),
                  pl.BlockSpec(mod.shape, const),
                  pl.BlockSpec((1, d), const),
                  pl.BlockSpec(w_in.shape, const, pipeline_mode=pl.Buffered(1)),
                  pl.BlockSpec((1, w), const),
                  pl.BlockSpec((1, w), const),
                  pl.BlockSpec(w_s.shape, lambda i: (0, 0, 0)),
                  pl.BlockSpec(b_st.shape, const),
                  pl.BlockSpec(w_out.shape, const, pipeline_mode=pl.Buffered(1)),
                  pl.BlockSpec((1, d), const),
                  pl.BlockSpec(rwt.shape, const),
                  pl.BlockSpec(rb.shape, const)],
        out_specs=out_specs,
        out_shape=out_shape,
        scratch_shapes=[pltpu.VMEM((MIX_TILE, w), BF16)],
        compiler_params=_params(("arbitrary",)),
        name="sgu",
    )(x, mod, g, w_in, ln_g, ln_b, w_s, b_st, w_out, g2, rwt, rb)


def _rank_kernel(mask_ref, eidx_ref, pstart_ref, jpos_ref, tab_ref, carry):
    e, t = mask_ref.shape

    @pl.when(pl.program_id(0) == 0)
    def _():
        carry[...] = jnp.zeros_like(carry)

    m = mask_ref[...]
    r = lax.broadcasted_iota(jnp.int32, (t, t), 0)
    c = lax.broadcasted_iota(jnp.int32, (t, t), 1)
    upper = jnp.where(r <= c, 1.0, 0.0).astype(BF16)
    incl = jnp.dot(m.astype(BF16), upper, preferred_element_type=F32)
    run = _round_up_rows(incl[:, t - 1:t])
    lanes = tab_ref.shape[2]
    re = lax.broadcasted_iota(jnp.int32, (e, e), 0)
    ce = lax.broadcasted_iota(jnp.int32, (e, e), 1)
    lower = jnp.where(ce < re, 1.0, 0.0).astype(BF16)
    tiles = jnp.broadcast_to(run * (1.0 / ROW_ALIGN), (e, lanes)).astype(BF16)
    lstart = jnp.dot(lower, tiles, preferred_element_type=F32)[:, 0:1] * float(ROW_ALIGN)
    pos = lstart + incl - m
    iota_e = lax.broadcasted_iota(jnp.int32, (e, t), 0)
    iota_k = lax.broadcasted_iota(jnp.int32, (TOP_K, t), 0)
    eidx = eidx_ref[...]
    jpos = jnp.zeros((TOP_K, t), F32)
    for k in range(TOP_K):
        jk = jnp.sum(jnp.where(iota_e == eidx[k:k + 1, :], pos, 0.0), axis=0, keepdims=True)
        jpos = jnp.where(iota_k == k, jk, jpos)
    jpos_ref[...] = jpos.astype(jnp.int32)
    lane = lax.broadcasted_iota(jnp.int32, (e, lanes), 1)
    tab_ref[0] = jnp.where(lane == 0, run, jnp.where(lane == 1, pstart_ref[...] + carry[...], 0.0))
    carry[...] = carry[...] + run


def _rank(mask_t, eidx_t, pstart):
    e, n = mask_t.shape
    n_tiles = n // TOKEN_TILE
    col = lambda i: (0, i)
    return pl.pallas_call(
        _rank_kernel,
        grid=(n_tiles,),
        in_specs=[pl.BlockSpec((e, TOKEN_TILE), col),
                  pl.BlockSpec((TOP_K, TOKEN_TILE), col),
                  pl.BlockSpec((e, 1), lambda i: (0, 0))],
        out_specs=[pl.BlockSpec((TOP_K, TOKEN_TILE), col),
                   pl.BlockSpec((1, e, LANES), lambda i: (i, 0, 0))],
        out_shape=[jax.ShapeDtypeStruct((TOP_K, n), jnp.int32),
                   jax.ShapeDtypeStruct((n_tiles, e, LANES), F32)],
        scratch_shapes=[pltpu.VMEM((e, 1), F32)],
        compiler_params=_params(("arbitrary",)),
        name="moe_rank",
    )(mask_t, eidx_t, pstart)


def _aligned(v):
    return pl.multiple_of(v, ROW_ALIGN)


def _run_copies(tab_ref, make_copy, unroll):
    def body(e, off):
        rows = _aligned(tab_ref[e])
        make_copy(_aligned(off), _aligned(tab_ref[N_EXPERTS + e]), rows).start()
        return off + rows

    lax.fori_loop(0, N_EXPERTS, body, 0, unroll=unroll)


def _dispatch_kernel(tab_ref, ztab_ref, jpos_ref, x_ref, xs_hbm, sbuf, zbuf, sems, zsem):
    i = pl.program_id(0)
    slot = i % 2
    t = x_ref.shape[0]
    rows = sbuf.shape[1]

    def zero_copy(e):
        n = _aligned(ztab_ref[e])
        return pltpu.make_async_copy(zbuf.at[pl.ds(0, n)], xs_hbm.at[pl.ds(_aligned(ztab_ref[N_EXPERTS + e]), n)], zsem)

    def for_zero_runs(fn):
        def body(e, c):
            @pl.when(ztab_ref[e] > 0)
            def _():
                fn(zero_copy(e))
            return c
        lax.fori_loop(0, N_EXPERTS, body, 0)

        def tail(b, c):
            fn(pltpu.make_async_copy(zbuf, xs_hbm.at[pl.ds(pl.multiple_of(b * EXPERT_BLOCK, EXPERT_BLOCK),
                                                           EXPERT_BLOCK)], zsem))
            return c
        lax.fori_loop(ztab_ref[2 * N_EXPERTS], xs_hbm.shape[0] // EXPERT_BLOCK, tail, 0)

    @pl.when(i == 0)
    def _():
        zbuf[...] = jnp.zeros_like(zbuf)
        for_zero_runs(lambda cp: cp.start())

    jp = jpos_ref[...]
    xb = x_ref[...].astype(BF16)
    total = tab_ref[2 * N_EXPERTS]

    def permute(r0):
        iota_j = lax.broadcasted_iota(jnp.int32, (PERM_CHUNK, t), 0) + r0
        p = jnp.zeros((PERM_CHUNK, t), F32)
        for k in range(TOP_K):
            p = jnp.where(iota_j == jp[k:k + 1, :], 1.0, p)
        sbuf[slot, r0:r0 + PERM_CHUNK, :] = jnp.dot(p.astype(BF16), xb, preferred_element_type=F32).astype(BF16)

    for r0 in range(0, rows, PERM_CHUNK):
        if r0 < t * TOP_K + PERM_CHUNK:
            permute(r0)
        else:
            pl.when(total > r0)(functools.partial(permute, r0))

    _run_copies(tab_ref, lambda loc, glob, n: pltpu.make_async_copy(
        sbuf.at[slot, pl.ds(loc, n)], xs_hbm.at[pl.ds(glob, n)], sems.at[slot]), unroll=True)

    def wait_rows(s, n):
        pltpu.make_async_copy(sbuf.at[s, pl.ds(0, n)], xs_hbm.at[pl.ds(0, n)], sems.at[s]).wait()

    @pl.when(i > 0)
    def _():
        wait_rows(1 - slot, _aligned(tab_ref[2 * N_EXPERTS + 1]))

    @pl.when(i == pl.num_programs(0) - 1)
    def _():
        wait_rows(slot, _aligned(tab_ref[2 * N_EXPERTS]))

    @pl.when(i == 0)
    def _():
        for_zero_runs(lambda cp: cp.wait())


def _dispatch(tab, ztab, jpos_t, hx2, xs_rows):
    n, d = hx2.shape
    col = lambda i: (0, i)
    return pl.pallas_call(
        _dispatch_kernel,
        grid=(n // TOKEN_TILE,),
        in_specs=[pl.BlockSpec((TAB_WIDTH,), lambda i: (i,), memory_space=pltpu.SMEM),
                  pl.BlockSpec(memory_space=pltpu.SMEM),
                  pl.BlockSpec((TOP_K, TOKEN_TILE), col),
                  pl.BlockSpec((TOKEN_TILE, d), lambda i: (i, 0))],
        out_specs=pl.BlockSpec(memory_space=pl.ANY),
        out_shape=jax.ShapeDtypeStruct((xs_rows, d), BF16),
        scratch_shapes=[pltpu.VMEM((2, SORTED_ROWS, d), BF16),
                        pltpu.VMEM((EXPERT_BLOCK, d), BF16),
                        pltpu.SemaphoreType.DMA((2,)),
                        pltpu.SemaphoreType.DMA],
        compiler_params=_params(("arbitrary",)),
        name="moe_dispatch",
    )(tab, ztab, jpos_t, hx2)


def _experts_kernel(be_ref, nb_ref, xs_ref, wgu_ref, wd_ref, ys_ref, wgu_b, wd_b):
    i = pl.program_id(0)
    ff = wd_b.shape[0]
    used = i < nb_ref[0]
    new_expert = (i == 0) | (be_ref[i] != be_ref[jnp.maximum(i - 1, 0)])

    @pl.when(used & new_expert)
    def _():
        wgu_b[...] = wgu_ref[0, 0].astype(BF16)
        wd_b[...] = wd_ref[0, 0].astype(BF16)

    @pl.when(used)
    def _():
        starts = range(0, xs_ref.shape[0], MIX_SUB)
        hs = [jnp.dot(xs_ref[r0:r0 + MIX_SUB, :], wgu_b[...], preferred_element_type=F32) for r0 in starts]
        for r0, h in zip(starts, hs):
            a = _silu(h[:, :ff]) * h[:, ff:]
            ys_ref[r0:r0 + MIX_SUB, :] = jnp.dot(a.astype(BF16), wd_b[...],
                                                 preferred_element_type=F32).astype(BF16)

    @pl.when(jnp.logical_not(used))
    def _():
        ys_ref[...] = jnp.zeros_like(ys_ref)


def _experts(block_e, n_used, xs, w_gu, w_down, layer):
    rows, d = xs.shape
    nb = rows // EXPERT_BLOCK
    blk = lambda i, be, nu: (jnp.maximum(jnp.minimum(i, nu[0] - 1), 0), 0)
    wmap = lambda i, be, nu: (layer, be[i], 0, 0)
    return pl.pallas_call(
        _experts_kernel,
        grid_spec=pltpu.PrefetchScalarGridSpec(
            num_scalar_prefetch=2,
            grid=(nb,),
            in_specs=[pl.BlockSpec((EXPERT_BLOCK, d), blk),
                      pl.BlockSpec((1, 1) + w_gu.shape[2:], wmap),
                      pl.BlockSpec((1, 1) + w_down.shape[2:], wmap)],
            out_specs=pl.BlockSpec((EXPERT_BLOCK, d), lambda i, be, nu: (i, 0)),
            scratch_shapes=[pltpu.VMEM(w_gu.shape[2:], BF16), pltpu.VMEM(w_down.shape[2:], BF16)]),
        out_shape=jax.ShapeDtypeStruct((rows, d), BF16),
        compiler_params=_params(("arbitrary",)),
        name="moe_experts",
    )(block_e, n_used, xs, w_gu, w_down)


def _combine_kernel(tab_ref, jpos_ref, gw_ref, ys_hbm, hx2_ref, x1_ref, mod_ref, wsgu_ref, wsd_ref, fg_ref,
                    out_ref, ybuf, sem, *, tiles_per_seq, final_norm):
    t_rows, d = x1_ref.shape
    ff = wsd_ref.shape[0]
    rows = ybuf.shape[0]
    row = pl.program_id(0) // tiles_per_seq

    @pl.when(pl.program_id(0) == 0)
    def _():
        ybuf[...] = jnp.zeros_like(ybuf)

    _run_copies(tab_ref, lambda loc, glob, n: pltpu.make_async_copy(
        ys_hbm.at[pl.ds(glob, n)], ybuf.at[pl.ds(loc, n)], sem), unroll=False)

    hs = jnp.dot(hx2_ref[...].astype(BF16), wsgu_ref[...], preferred_element_type=F32)
    shared = jnp.dot((_silu(hs[:, :ff]) * hs[:, ff:]).astype(BF16), wsd_ref[...], preferred_element_type=F32)

    jp = jpos_ref[...]
    gw = gw_ref[...]

    def gated_sum(c0, c1):
        iota_j = lax.broadcasted_iota(jnp.int32, (t_rows, c1 - c0), 1) + c0
        g = jnp.zeros((t_rows, c1 - c0), F32)
        for k in range(TOP_K):
            g = jnp.where(iota_j == jp[:, k:k + 1], gw[:, k:k + 1], g)
        return jnp.dot(g.astype(BF16), ybuf[c0:c1, :], preferred_element_type=F32)

    total = _aligned(tab_ref[2 * N_EXPERTS])
    pltpu.make_async_copy(ys_hbm.at[pl.ds(0, total)], ybuf.at[pl.ds(0, total)], sem).wait()
    routed = gated_sum(0, rows)
    x2 = x1_ref[...] + _mod_chunk(mod_ref, row, 5, d) * (routed + shared)
    if final_norm:
        x2 = x2 * lax.rsqrt(jnp.mean(x2 * x2, axis=-1, keepdims=True) + NORM_EPS) * fg_ref[...]
    out_ref[...] = x2


def _combine(tab, jpos, gw, ys, hx2, x1, mod, ws_gu, ws_down, fg, tiles_per_seq, final_norm):
    n, d = x1.shape
    row = lambda i: (i, 0)
    const = lambda i: (0, 0)
    return pl.pallas_call(
        functools.partial(_combine_kernel, tiles_per_seq=tiles_per_seq, final_norm=final_norm),
        grid=(n // TOKEN_TILE,),
        in_specs=[pl.BlockSpec((TAB_WIDTH,), lambda i: (i,), memory_space=pltpu.SMEM),
                  pl.BlockSpec((TOKEN_TILE, TOP_K), row),
                  pl.BlockSpec((TOKEN_TILE, TOP_K), row),
                  pl.BlockSpec(memory_space=pl.ANY),
                  pl.BlockSpec((TOKEN_TILE, d), row),
                  pl.BlockSpec((TOKEN_TILE, d), row),
                  pl.BlockSpec(mod.shape, const),
                  pl.BlockSpec(ws_gu.shape, const),
                  pl.BlockSpec(ws_down.shape, const),
                  pl.BlockSpec((1, d), const)],
        out_specs=pl.BlockSpec((TOKEN_TILE, d), row),
        out_shape=jax.ShapeDtypeStruct((n, d), F32),
        scratch_shapes=[pltpu.VMEM((SORTED_ROWS, d), BF16), pltpu.SemaphoreType.DMA],
        compiler_params=_params(("arbitrary",)),
        name="moe_combine",
    )(tab, jpos, gw, ys, hx2, x1, mod, ws_gu, ws_down, fg)


def _moe(x1, hx2, eidx_t, gw_t, mask_t, cnt, mod, w_gu, w_down, layer, ws_gu, ws_down, fg, tiles_per_seq,
         final_norm):
    n, d = x1.shape
    n_tiles = n // TOKEN_TILE
    counts = cnt[:, 0].astype(jnp.int32)
    padded = (counts + EXPERT_BLOCK - 1) // EXPERT_BLOCK * EXPERT_BLOCK
    pad_end = jnp.cumsum(padded)
    pad_start = pad_end - padded
    max_rows = n * TOP_K + n_tiles * N_EXPERTS * ROW_ALIGN + N_EXPERTS * (EXPERT_BLOCK - ROW_ALIGN)
    n_blocks = (max_rows + EXPERT_BLOCK - 1) // EXPERT_BLOCK
    n_used = (pad_end[-1:] // EXPERT_BLOCK).astype(jnp.int32)
    block_start = jnp.arange(n_blocks, dtype=jnp.int32) * EXPERT_BLOCK
    block_e = jnp.minimum(jnp.sum((pad_end[None, :] <= block_start[:, None]).astype(jnp.int32), axis=1),
                          N_EXPERTS - 1)

    jpos_t, tab_f = _rank(mask_t, eidx_t, pad_start.astype(F32).reshape(N_EXPERTS, 1))
    run_len = tab_f[:, :, 0].astype(jnp.int32)
    run_start = tab_f[:, :, 1].astype(jnp.int32)
    total = jnp.sum(run_len, axis=1, keepdims=True)
    prev_total = jnp.concatenate([jnp.zeros((1, 1), jnp.int32), total[:-1]], axis=0)
    fill = jnp.zeros((n_tiles, TAB_WIDTH - 2 * N_EXPERTS - 2), jnp.int32)
    tab = jnp.concatenate([run_len, run_start, total, prev_total, fill], axis=1).reshape(-1)
    ztab = jnp.concatenate([padded - counts, pad_start + counts, n_used])

    xs = _dispatch(tab, ztab, jpos_t, hx2, n_blocks * EXPERT_BLOCK)
    ys = _experts(block_e, n_used, xs, w_gu, w_down, layer)
    return _combine(tab, jpos_t.T, gw_t.T, ys, hx2, x1, mod, ws_gu, ws_down, fg, tiles_per_seq, final_norm)


def _sincos_2d(rows, d):
    quarter = d // 4
    omega = 1.0 / (POS_BASE ** (jnp.arange(quarter, dtype=F32) / quarter))

    def emb(n):
        p = jnp.arange(n, dtype=F32)[:, None] * omega[None, :]
        return jnp.concatenate([jnp.sin(p), jnp.cos(p)], axis=-1)

    er, ec = emb(rows), emb(GRID_W)
    pe = jnp.concatenate([jnp.broadcast_to(er[:, None, :], (rows, GRID_W, d // 2)),
                          jnp.broadcast_to(ec[None, :, :], (rows, GRID_W, d // 2))], axis=-1)
    return pe.reshape(rows * GRID_W, d)


def kernel(x, c, ctx, c_ctx, ada_w, ada_b, mix_norm_g, ffn_norm_g, a_w_in, a_conv_w, a_conv_b, a_gate_r_w, a_gate_r_b, a_gate_i_w, a_gate_i_b, a_lambda, a_w_out, b_w_in, b_ln_g, b_ln_b, b_w_s, b_b_s, b_w_out, router_w, router_b, moe_w_gu, moe_w_down, shared_w_gu, shared_w_down, final_norm_g):
    bsz, s, d = x.shape
    ctx_len = ctx.shape[1]
    depth = ada_w.shape[0]
    assert depth == 2 and bsz < MOD_ROWS and s % MIX_TILE == 0 and MIX_TILE % TOKEN_TILE == 0
    assert ctx_len % TOKEN_TILE == 0
    n = bsz * s
    tps = s // TOKEN_TILE
    ctx_row = bsz

    cc = jnp.zeros((MOD_ROWS, d), F32).at[:bsz].set(c).at[ctx_row].set(c_ctx)
    mod = _modulation(cc, ada_w, ada_b)
    pe = _sincos_2d(s // GRID_W, d)
    rc = a_w_in.shape[2] // 2

    w_in0 = a_w_in[0].astype(BF16)
    g_mix0 = mix_norm_g[0].reshape(1, d)
    x0, gate, ux = _rglru_in(x.reshape(n, d), pe, mod[0], g_mix0, w_in0)
    uc = _ctx_in(ctx.reshape(bsz * ctx_len, d), mod[0], g_mix0, w_in0[:, rc:], ctx_row)
    w_ri = jnp.concatenate([a_gate_r_w[0], a_gate_i_w[0]], axis=-1).astype(BF16)
    conv = (a_conv_w[0], a_conv_b[0])
    gates = [(w_ri[k], a_gate_r_b[0, k], a_gate_i_b[0, k], a_lambda[0, k]) for k in range(2)]
    h_zero = jnp.zeros((bsz, rc), F32)
    h_fwd, uc = _lru_scan(uc, conv, *gates[0], h_zero, reverse=False, reset_first=True, emit_y=False)
    h_rev = _lru_scan(uc, None, *gates[1], h_zero, reverse=True, reset_first=True, emit_y=False)
    y_fwd, ux = _lru_scan(ux, conv, *gates[0], h_fwd, reverse=False, reset_first=False, emit_y=True)
    y_rev = _lru_scan(ux, None, *gates[1], h_rev, reverse=True, reset_first=False, emit_y=True)
    pre = _rglru_out(y_fwd, y_rev, gate, x0, mod[0], a_w_out[0].astype(BF16), ffn_norm_g[0].reshape(1, d),
                     router_w[0].T, router_b[0].reshape(N_EXPERTS, 1), s // MIX_TILE)
    x1 = _moe(*pre, mod[0], moe_w_gu, moe_w_down, 0, shared_w_gu[0].astype(BF16),
              shared_w_down[0].astype(BF16), final_norm_g.reshape(1, d), tps, False)

    pre = _sgu(x1, mod[1], mix_norm_g[1].reshape(1, d), b_w_in[0].astype(BF16),
               b_ln_g[0].reshape(1, -1), b_ln_b[0].reshape(1, -1), b_w_s[0].astype(BF16), b_b_s[0].T,
               b_w_out[0].astype(BF16), ffn_norm_g[1].reshape(1, d),
               router_w[1].T, router_b[1].reshape(N_EXPERTS, 1), s // MIX_TILE)
    out = _moe(*pre, mod[1], moe_w_gu, moe_w_down, 1, shared_w_gu[1].astype(BF16),
               shared_w_down[1].astype(BF16), final_norm_g.reshape(1, d), tps, True)
    return out.reshape(bsz, s, d)
```

```python
import functools

import jax
import jax.numpy as jnp
from jax import lax
from jax.experimental import pallas as pl
from jax.experimental.pallas import tpu as pltpu

F32 = jnp.float32
BF16 = jnp.bfloat16
HIGHEST = lax.Precision.HIGHEST

GRID_W = 64
N_MOD = 6
NORM_EPS = 1e-6
POS_BASE = 10000.0
RNN_HEADS = 5
CONV_WIDTH = 4
CONV_PAD_LEFT = 2
LRU_C = 8.0
SGU_HEADS = 8
CHUNK = 128
N_EXPERTS = 64
TOP_K = 8
N_GROUPS = 8
TOPK_GROUPS = 4
EXPERTS_PER_GROUP = N_EXPERTS // N_GROUPS
ROUTED_SCALE = 2.5

SUBLANES = 8
ROW_ALIGN = 16
LANES = 128
MOD_ROWS = 8
TOKEN_TILE = 256
MIX_TILE = 512
MIX_SUB = 256
EXPERT_BLOCK = 512
SORTED_ROWS = TOKEN_TILE * TOP_K + N_EXPERTS * ROW_ALIGN
PERM_CHUNK = 512
TAB_WIDTH = 256
VMEM_LIMIT = 56 * 1024 * 1024


def _params(semantics, vmem=VMEM_LIMIT):
    return pltpu.CompilerParams(dimension_semantics=semantics, vmem_limit_bytes=vmem)


def _silu(x):
    return x * jax.nn.sigmoid(x)


def _rms_mod(x, g, sc, sh):
    y = x * lax.rsqrt(jnp.mean(x * x, axis=-1, keepdims=True) + NORM_EPS)
    return (y * g) * (1.0 + sc) + sh


def _mod_chunk(mod_ref, row, k, d):
    return mod_ref[pl.ds(row, 1), k * d:(k + 1) * d]


def _round_up_rows(count):
    return jnp.maximum(jnp.ceil(count * (1.0 / ROW_ALIGN)), 1.0) * float(ROW_ALIGN)


def _mod_kernel(cc_ref, w_ref, b_ref, o_ref):
    s = _silu(cc_ref[...])
    o_ref[0] = jnp.dot(s, w_ref[0], preferred_element_type=F32, precision=HIGHEST) + b_ref[0]


def _modulation(cc, ada_w, ada_b):
    depth, d, nd = ada_w.shape
    return pl.pallas_call(
        _mod_kernel,
        grid=(depth, nd // d),
        in_specs=[pl.BlockSpec((MOD_ROWS, d), lambda l, j: (0, 0)),
                  pl.BlockSpec((1, d, d), lambda l, j: (l, 0, j)),
                  pl.BlockSpec((1, 1, d), lambda l, j: (l, 0, j))],
        out_specs=pl.BlockSpec((1, MOD_ROWS, d), lambda l, j: (l, 0, j)),
        out_shape=jax.ShapeDtypeStruct((depth, MOD_ROWS, nd), F32),
        compiler_params=_params(("arbitrary", "arbitrary")),
        name="modulation",
    )(cc, ada_w, ada_b.reshape(depth, 1, nd))


def _rglru_in_kernel(x_ref, pe_ref, mod_ref, g_ref, w_ref, x0_ref, gate_ref, u_ref, *, tiles_per_seq):
    d = x_ref.shape[1]
    c = u_ref.shape[1]
    row = pl.program_id(0) // tiles_per_seq
    x = x_ref[...] + pe_ref[...]
    hx = _rms_mod(x, g_ref[...], _mod_chunk(mod_ref, row, 1, d), _mod_chunk(mod_ref, row, 0, d))
    z = jnp.dot(hx.astype(BF16), w_ref[...], preferred_element_type=F32)
    x0_ref[...] = x
    gate_ref[...] = jax.nn.gelu(z[:, :c]).astype(BF16)
    u_ref[...] = z[:, c:]


def _rglru_in(x2, pe, mod, g, w_in):
    n, d = x2.shape
    c = w_in.shape[1] // 2
    s = pe.shape[0]
    tps = s // MIX_TILE
    row = lambda i: (i, 0)
    return pl.pallas_call(
        functools.partial(_rglru_in_kernel, tiles_per_seq=tps),
        grid=(n // MIX_TILE,),
        in_specs=[pl.BlockSpec((MIX_TILE, d), row),
                  pl.BlockSpec((MIX_TILE, d), lambda i: (i % tps, 0)),
                  pl.BlockSpec(mod.shape, lambda i: (0, 0)),
                  pl.BlockSpec((1, d), lambda i: (0, 0)),
                  pl.BlockSpec(w_in.shape, lambda i: (0, 0))],
        out_specs=[pl.BlockSpec((MIX_TILE, d), row),
                   pl.BlockSpec((MIX_TILE, c), row),
                   pl.BlockSpec((MIX_TILE, c), row)],
        out_shape=[jax.ShapeDtypeStruct((n, d), F32),
                   jax.ShapeDtypeStruct((n, c), BF16),
                   jax.ShapeDtypeStruct((n, c), F32)],
        compiler_params=_params(("arbitrary",)),
        name="rglru_in",
    )(x2, pe, mod, g, w_in)


def _ctx_in_kernel(x_ref, mod_ref, g_ref, w_ref, u_ref, *, ctx_row):
    d = x_ref.shape[1]
    hx = _rms_mod(x_ref[...], g_ref[...], _mod_chunk(mod_ref, ctx_row, 1, d),
                  _mod_chunk(mod_ref, ctx_row, 0, d))
    u_ref[...] = jnp.dot(hx.astype(BF16), w_ref[...], preferred_element_type=F32)


def _ctx_in(c2, mod, g, w_u, ctx_row):
    n, d = c2.shape
    c = w_u.shape[1]
    return pl.pallas_call(
        functools.partial(_ctx_in_kernel, ctx_row=ctx_row),
        grid=(n // TOKEN_TILE,),
        in_specs=[pl.BlockSpec((TOKEN_TILE, d), lambda i: (i, 0)),
                  pl.BlockSpec(mod.shape, lambda i: (0, 0)),
                  pl.BlockSpec((1, d), lambda i: (0, 0)),
                  pl.BlockSpec(w_u.shape, lambda i: (0, 0))],
        out_specs=pl.BlockSpec((TOKEN_TILE, c), lambda i: (i, 0)),
        out_shape=jax.ShapeDtypeStruct((n, c), F32),
        compiler_params=_params(("arbitrary",)),
        name="ctx_in",
    )(c2, mod, g, w_u)


def _log_sigmoid(x):
    return jnp.minimum(x, 0.0) - jnp.log1p(jnp.exp(-jnp.abs(x)))


def _lru_scan_kernel(*refs, n_tiles, reverse, conv_done, reset_first, emit_y):
    if conv_done:
        u_ref, wri_ref, rb_ref, ib_ref, lam_ref, h0_ref, out_ref, a_scr, b_scr, h_scr = refs
    else:
        (u_ref, up_ref, un_ref, cw_ref, cb_ref, wri_ref, rb_ref, ib_ref, lam_ref, h0_ref,
         out_ref, uc_ref, ubuf, a_scr, b_scr, h_scr) = refs
    t_rows, c = u_ref.shape
    hb = c // RNN_HEADS
    b = pl.program_id(0)
    j = pl.program_id(1)
    jj = n_tiles - 1 - j if reverse else j

    if conv_done:
        u = u_ref[...]
    else:
        ubuf[SUBLANES:SUBLANES + t_rows, :] = u_ref[...]
        ubuf[0:SUBLANES, :] = jnp.where(jj == 0, 0.0, up_ref[...])
        ubuf[SUBLANES + t_rows:, :] = jnp.where(jj == n_tiles - 1, 0.0, un_ref[...])
        u = cb_ref[...]
        for k in range(CONV_WIDTH):
            u = u + cw_ref[k:k + 1, :] * ubuf[pl.ds(SUBLANES - CONV_PAD_LEFT + k, t_rows), :]
        uc_ref[...] = u

    log_lam = LRU_C * _log_sigmoid(lam_ref[...])
    rows = lax.broadcasted_iota(jnp.int32, (t_rows, 1), 0)
    first_row = jnp.where(j == 0, t_rows - 1 if reverse else 0, -1)
    for h in range(RNN_HEADS):
        sl = slice(h * hb, (h + 1) * hb)
        uh = u[:, sl]
        z = jnp.dot(uh.astype(BF16), wri_ref[h], preferred_element_type=F32)
        r = jax.nn.sigmoid(z[:, :hb] + rb_ref[:, sl])
        ig = jax.nn.sigmoid(z[:, hb:] + ib_ref[:, sl])
        log_a = r * log_lam[:, sl]
        a = jnp.exp(log_a)
        mult = jnp.sqrt((1.0 - a) * (1.0 + a))
        if reset_first:
            mult = jnp.where(rows == first_row, 1.0, mult)
        a_scr[:, sl] = a
        b_scr[:, sl] = mult * ig * uh

    @pl.when(j == 0)
    def _():
        h_scr[...] = h0_ref[pl.ds(b, 1), :]

    n_groups = t_rows // SUBLANES

    def group(g, h):
        base = pl.multiple_of((n_groups - 1 - g if reverse else g) * SUBLANES, SUBLANES)
        for s in range(SUBLANES):
            r = base + (SUBLANES - 1 - s if reverse else s)
            h = a_scr[pl.ds(r, 1), :] * h + b_scr[pl.ds(r, 1), :]
            if emit_y:
                out_ref[pl.ds(r, 1), :] = h
        return h

    h = lax.fori_loop(0, n_groups, group, h_scr[...])
    h_scr[...] = h
    if not emit_y:
        @pl.when(j == n_tiles - 1)
        def _():
            out_ref[pl.ds(b, 1), :] = h


def _lru_scan(u, conv, w_ri, r_b, i_b, lam, h0, *, reverse, reset_first, emit_y):
    n, c = u.shape
    n_batch = h0.shape[0]
    n_tiles = n // n_batch // TOKEN_TILE
    sub = TOKEN_TILE // SUBLANES
    n_sub = n // SUBLANES
    const = lambda b, j: (0, 0)

    def tile(b, j):
        return b * n_tiles + (n_tiles - 1 - j if reverse else j)

    tile_spec = pl.BlockSpec((TOKEN_TILE, c), lambda b, j: (tile(b, j), 0))
    in_specs = [tile_spec]
    args = [u]
    scratch = []
    if conv is not None:
        conv_w, conv_b = conv
        in_specs += [pl.BlockSpec((SUBLANES, c), lambda b, j: (jnp.maximum(tile(b, j) * sub - 1, 0), 0)),
                     pl.BlockSpec((SUBLANES, c), lambda b, j: (jnp.minimum((tile(b, j) + 1) * sub, n_sub - 1), 0)),
                     pl.BlockSpec(conv_w.shape, const),
                     pl.BlockSpec((1, c), const)]
        args += [u, u, conv_w, conv_b.reshape(1, c)]
        scratch = [pltpu.VMEM((TOKEN_TILE + 2 * SUBLANES, c), F32)]
    in_specs += [pl.BlockSpec(w_ri.shape, lambda b, j: (0, 0, 0)),
                 pl.BlockSpec((1, c), const), pl.BlockSpec((1, c), const), pl.BlockSpec((1, c), const),
                 pl.BlockSpec(h0.shape, const)]
    args += [w_ri, r_b.reshape(1, c), i_b.reshape(1, c), lam.reshape(1, c), h0]
    if emit_y:
        out_specs = [tile_spec]
        out_shape = [jax.ShapeDtypeStruct((n, c), F32)]
    else:
        out_specs = [pl.BlockSpec(h0.shape, const)]
        out_shape = [jax.ShapeDtypeStruct(h0.shape, F32)]
    if conv is not None:
        out_specs.append(tile_spec)
        out_shape.append(jax.ShapeDtypeStruct((n, c), F32))
    name = ("lru_scan" if emit_y else "lru_ctx") + ("_rev" if reverse else "_fwd")
    outs = pl.pallas_call(
        functools.partial(_lru_scan_kernel, n_tiles=n_tiles, reverse=reverse, conv_done=conv is None,
                          reset_first=reset_first, emit_y=emit_y),
        grid=(n_batch, n_tiles),
        in_specs=in_specs,
        out_specs=out_specs,
        out_shape=out_shape,
        scratch_shapes=scratch + [pltpu.VMEM((TOKEN_TILE, c), F32),
                                  pltpu.VMEM((TOKEN_TILE, c), F32),
                                  pltpu.VMEM((1, c), F32)],
        compiler_params=_params(("arbitrary", "arbitrary")),
        name=name,
    )(*args)
    return outs if conv is not None else outs[0]


def _route(logits, rb):
    e, t = logits.shape
    neg = -jnp.inf
    scores = jax.nn.sigmoid(logits)
    sel = scores + rb
    iota_g = lax.broadcasted_iota(jnp.int32, (N_GROUPS, t), 0).astype(F32)
    iota_e = lax.broadcasted_iota(jnp.int32, (e, t), 0).astype(F32)

    gs = jnp.full((N_GROUPS, t), neg, F32)
    for g in range(N_GROUPS):
        sg = sel[g * EXPERTS_PER_GROUP:(g + 1) * EXPERTS_PER_GROUP, :]
        m1 = jnp.max(sg, axis=0, keepdims=True)
        i1 = jnp.min(jnp.where(sg == m1, iota_g, float(EXPERTS_PER_GROUP)), axis=0, keepdims=True)
        m2 = jnp.max(jnp.where(iota_g == i1, neg, sg), axis=0, keepdims=True)
        gs = jnp.where(iota_g == float(g), m1 + m2, gs)

    keep = jnp.zeros((N_GROUPS, t), F32)
    for _ in range(TOPK_GROUPS):
        m = jnp.max(gs, axis=0, keepdims=True)
        idx = jnp.min(jnp.where(gs == m, iota_g, float(N_GROUPS)), axis=0, keepdims=True)
        hit = iota_g == idx
        keep = jnp.where(hit, 1.0, keep)
        gs = jnp.where(hit, neg, gs)

    masked = jnp.concatenate(
        [jnp.where(keep[g:g + 1, :] > 0.0, sel[g * EXPERTS_PER_GROUP:(g + 1) * EXPERTS_PER_GROUP, :], neg)
         for g in range(N_GROUPS)], axis=0)

    iota_k = lax.broadcasted_iota(jnp.int32, (TOP_K, t), 0)
    selmask = jnp.zeros((e, t), F32)
    eidx = jnp.zeros((TOP_K, t), F32)
    gw = jnp.zeros((TOP_K, t), F32)
    for k in range(TOP_K):
        m = jnp.max(masked, axis=0, keepdims=True)
        idx = jnp.min(jnp.where(masked == m, iota_e, float(e)), axis=0, keepdims=True)
        hit = iota_e == idx
        gk = jnp.sum(jnp.where(hit, scores, 0.0), axis=0, keepdims=True)
        masked = jnp.where(hit, neg, masked)
        selmask = jnp.where(hit, 1.0, selmask)
        eidx = jnp.where(iota_k == k, idx, eidx)
        gw = jnp.where(iota_k == k, gk, gw)
    gw = gw / jnp.sum(gw, axis=0, keepdims=True) * ROUTED_SCALE
    return eidx, gw, selmask


def _ffn_pre(x1, mod_ref, row, g2_ref, rwt_ref, rb_ref, hx2_ref, eidx_ref, gw_ref, mask_ref, cnt_ref):
    d = x1.shape[1]
    hx2 = _rms_mod(x1, g2_ref[...], _mod_chunk(mod_ref, row, 4, d), _mod_chunk(mod_ref, row, 3, d))
    hx2_ref[...] = hx2
    logits = lax.dot_general(rwt_ref[...], hx2, (((1,), (1,)), ((), ())),
                             preferred_element_type=F32, precision=HIGHEST)
    eidx, gw, selmask = _route(logits, rb_ref[...])
    eidx_ref[...] = eidx.astype(jnp.int32)
    gw_ref[...] = gw
    mask_ref[...] = selmask

    @pl.when(pl.program_id(0) == 0)
    def _():
        cnt_ref[...] = jnp.zeros_like(cnt_ref)

    rows = jnp.zeros((selmask.shape[0], 1), F32)
    for t0 in range(0, selmask.shape[1], TOKEN_TILE):
        rows = rows + _round_up_rows(jnp.sum(selmask[:, t0:t0 + TOKEN_TILE], axis=1, keepdims=True))
    cnt_ref[...] += jnp.broadcast_to(rows, cnt_ref.shape)


def _ffn_pre_specs(n, d):
    row = lambda i: (i, 0)
    col = lambda i: (0, i)
    out_specs = [pl.BlockSpec((MIX_TILE, d), row),
                 pl.BlockSpec((MIX_TILE, d), row),
                 pl.BlockSpec((TOP_K, MIX_TILE), col),
                 pl.BlockSpec((TOP_K, MIX_TILE), col),
                 pl.BlockSpec((N_EXPERTS, MIX_TILE), col),
                 pl.BlockSpec((N_EXPERTS, LANES), lambda i: (0, 0))]
    out_shape = [jax.ShapeDtypeStruct((n, d), F32),
                 jax.ShapeDtypeStruct((n, d), F32),
                 jax.ShapeDtypeStruct((TOP_K, n), jnp.int32),
                 jax.ShapeDtypeStruct((TOP_K, n), F32),
                 jax.ShapeDtypeStruct((N_EXPERTS, n), F32),
                 jax.ShapeDtypeStruct((N_EXPERTS, 128), F32)]
    return out_specs, out_shape


def _rglru_out_kernel(yf_ref, yr_ref, gate_ref, x0_ref, mod_ref, wout_ref, g2_ref, rwt_ref, rb_ref,
                      x1_ref, hx2_ref, eidx_ref, gw_ref, mask_ref, cnt_ref, *, tiles_per_seq):
    d = x0_ref.shape[1]
    row = pl.program_id(0) // tiles_per_seq
    yx = yf_ref[...] + yr_ref[...]
    v = gate_ref[...].astype(F32) * yx
    out = jnp.dot(v.astype(BF16), wout_ref[...], preferred_element_type=F32)
    x1 = x0_ref[...] + _mod_chunk(mod_ref, row, 2, d) * out
    x1_ref[...] = x1
    _ffn_pre(x1, mod_ref, row, g2_ref, rwt_ref, rb_ref, hx2_ref, eidx_ref, gw_ref, mask_ref, cnt_ref)


def _rglru_out(y_fwd, y_rev, gate, x0, mod, w_out, g2, rwt, rb, tiles_per_seq):
    n, d = x0.shape
    c = gate.shape[1]
    row = lambda i: (i, 0)
    const = lambda i: (0, 0)
    out_specs, out_shape = _ffn_pre_specs(n, d)
    return pl.pallas_call(
        functools.partial(_rglru_out_kernel, tiles_per_seq=tiles_per_seq),
        grid=(n // MIX_TILE,),
        in_specs=[pl.BlockSpec((MIX_TILE, c), row),
                  pl.BlockSpec((MIX_TILE, c), row),
                  pl.BlockSpec((MIX_TILE, c), row),
                  pl.BlockSpec((MIX_TILE, d), row),
                  pl.BlockSpec(mod.shape, const),
                  pl.BlockSpec(w_out.shape, const),
                  pl.BlockSpec((1, d), const),
                  pl.BlockSpec(rwt.shape, const),
                  pl.BlockSpec(rb.shape, const)],
        out_specs=out_specs,
        out_shape=out_shape,
        compiler_params=_params(("arbitrary",)),
        name="rglru_out",
    )(y_fwd, y_rev, gate, x0, mod, w_out, g2, rwt, rb)


def _sgu_kernel(x_ref, mod_ref, g_ref, win_ref, lng_ref, lnb_ref, ws_ref, bst_ref, wout_ref,
                g2_ref, rwt_ref, rb_ref,
                x1_ref, hx2_ref, eidx_ref, gw_ref, mask_ref, cnt_ref, m_scr, *, tiles_per_seq):
    t_rows, d = x_ref.shape
    w = wout_ref.shape[0]
    gd = w // SGU_HEADS
    row = pl.program_id(0) // tiles_per_seq
    x = x_ref[...]
    hx = _rms_mod(x, g_ref[...], _mod_chunk(mod_ref, row, 1, d), _mod_chunk(mod_ref, row, 0, d))
    z = jax.nn.gelu(jnp.dot(hx.astype(BF16), win_ref[...], preferred_element_type=F32))
    u = z[:, :w]
    v = z[:, w:]
    mu = jnp.mean(v, axis=-1, keepdims=True)
    vc = v - mu
    v = vc * lax.rsqrt(jnp.mean(vc * vc, axis=-1, keepdims=True) + NORM_EPS) * lng_ref[...] + lnb_ref[...]
    vb = v.astype(BF16)
    for ch in range(t_rows // CHUNK):
        rs = slice(ch * CHUNK, (ch + 1) * CHUNK)
        for g in range(SGU_HEADS):
            cs = slice(g * gd, (g + 1) * gd)
            sv = jnp.dot(ws_ref[g], vb[rs, cs], preferred_element_type=F32) + bst_ref[:, g:g + 1]
            m_scr[rs, cs] = (u[rs, cs] * sv).astype(BF16)
    out = jnp.dot(m_scr[...], wout_ref[...], preferred_element_type=F32)
    x1 = x + _mod_chunk(mod_ref, row, 2, d) * out
    x1_ref[...] = x1
    _ffn_pre(x1, mod_ref, row, g2_ref, rwt_ref, rb_ref, hx2_ref, eidx_ref, gw_ref, mask_ref, cnt_ref)


def _sgu(x, mod, g, w_in, ln_g, ln_b, w_s, b_st, w_out, g2, rwt, rb, tiles_per_seq):
    n, d = x.shape
    w = w_out.shape[0]
    const = lambda i: (0, 0)
    out_specs, out_shape = _ffn_pre_specs(n, d)
    return pl.pallas_call(
        functools.partial(_sgu_kernel, tiles_per_seq=tiles_per_seq),
        grid=(n // MIX_TILE,),
        in_specs=[pl.BlockSpec((MIX_TILE, d), lambda i: (i, 0)),
                  pl.BlockSpec(mod.shape, const),
                  pl.BlockSpec((1, d), const),
                  pl.BlockSpec(w_in.shape, const, pipeline_mode=pl.Buffered(1)),
                  pl.BlockSpec((1, w), const),
                  pl.BlockSpec((1, w), const),
                  pl.BlockSpec(w_s.shape, lambda i: (0, 0, 0)),
                  pl.BlockSpec(b_st.shape, const),
                  pl.BlockSpec(w_out.shape, const, pipeline_mode=pl.Buffered(1)),
                  pl.BlockSpec((1, d), const),
                  pl.BlockSpec(rwt.shape, const),
                  pl.BlockSpec(rb.shape, const)],
        out_specs=out_specs,
        out_shape=out_shape,
        scratch_shapes=[pltpu.VMEM((MIX_TILE, w), BF16)],
        compiler_params=_params(("arbitrary",)),
        name="sgu",
    )(x, mod, g, w_in, ln_g, ln_b, w_s, b_st, w_out, g2, rwt, rb)


def _rank_kernel(mask_ref, eidx_ref, pstart_ref, jpos_ref, tab_ref, carry):
    e, t = mask_ref.shape

    @pl.when(pl.program_id(0) == 0)
    def _():
        carry[...] = jnp.zeros_like(carry)

    m = mask_ref[...]
    r = lax.broadcasted_iota(jnp.int32, (t, t), 0)
    c = lax.broadcasted_iota(jnp.int32, (t, t), 1)
    upper = jnp.where(r <= c, 1.0, 0.0).astype(BF16)
    incl = jnp.dot(m.astype(BF16), upper, preferred_element_type=F32)
    run = _round_up_rows(incl[:, t - 1:t])
    lanes = tab_ref.shape[2]
    re = lax.broadcasted_iota(jnp.int32, (e, e), 0)
    ce = lax.broadcasted_iota(jnp.int32, (e, e), 1)
    lower = jnp.where(ce < re, 1.0, 0.0).astype(BF16)
    tiles = jnp.broadcast_to(run * (1.0 / ROW_ALIGN), (e, lanes)).astype(BF16)
    lstart = jnp.dot(lower, tiles, preferred_element_type=F32)[:, 0:1] * float(ROW_ALIGN)
    pos = lstart + incl - m
    iota_e = lax.broadcasted_iota(jnp.int32, (e, t), 0)
    iota_k = lax.broadcasted_iota(jnp.int32, (TOP_K, t), 0)
    eidx = eidx_ref[...]
    jpos = jnp.zeros((TOP_K, t), F32)
    for k in range(TOP_K):
        jk = jnp.sum(jnp.where(iota_e == eidx[k:k + 1, :], pos, 0.0), axis=0, keepdims=True)
        jpos = jnp.where(iota_k == k, jk, jpos)
    jpos_ref[...] = jpos.astype(jnp.int32)
    lane = lax.broadcasted_iota(jnp.int32, (e, lanes), 1)
    tab_ref[0] = jnp.where(lane == 0, run, jnp.where(lane == 1, pstart_ref[...] + carry[...], 0.0))
    carry[...] = carry[...] + run


def _rank(mask_t, eidx_t, pstart):
    e, n = mask_t.shape
    n_tiles = n // TOKEN_TILE
    col = lambda i: (0, i)
    return pl.pallas_call(
        _rank_kernel,
        grid=(n_tiles,),
        in_specs=[pl.BlockSpec((e, TOKEN_TILE), col),
                  pl.BlockSpec((TOP_K, TOKEN_TILE), col),
                  pl.BlockSpec((e, 1), lambda i: (0, 0))],
        out_specs=[pl.BlockSpec((TOP_K, TOKEN_TILE), col),
                   pl.BlockSpec((1, e, LANES), lambda i: (i, 0, 0))],
        out_shape=[jax.ShapeDtypeStruct((TOP_K, n), jnp.int32),
                   jax.ShapeDtypeStruct((n_tiles, e, LANES), F32)],
        scratch_shapes=[pltpu.VMEM((e, 1), F32)],
        compiler_params=_params(("arbitrary",)),
        name="moe_rank",
    )(mask_t, eidx_t, pstart)


def _aligned(v):
    return pl.multiple_of(v, ROW_ALIGN)


def _run_copies(tab_ref, make_copy, unroll):
    def body(e, off):
        rows = _aligned(tab_ref[e])
        make_copy(_aligned(off), _aligned(tab_ref[N_EXPERTS + e]), rows).start()
        return off + rows

    lax.fori_loop(0, N_EXPERTS, body, 0, unroll=unroll)


def _dispatch_kernel(tab_ref, ztab_ref, jpos_ref, x_ref, xs_hbm, sbuf, zbuf, sems, zsem):
    i = pl.program_id(0)
    slot = i % 2
    t = x_ref.shape[0]
    rows = sbuf.shape[1]

    def zero_copy(e):
        n = _aligned(ztab_ref[e])
        return pltpu.make_async_copy(zbuf.at[pl.ds(0, n)], xs_hbm.at[pl.ds(_aligned(ztab_ref[N_EXPERTS + e]), n)], zsem)

    def for_zero_runs(fn):
        def body(e, c):
            @pl.when(ztab_ref[e] > 0)
            def _():
                fn(zero_copy(e))
            return c
        lax.fori_loop(0, N_EXPERTS, body, 0)

        def tail(b, c):
            fn(pltpu.make_async_copy(zbuf, xs_hbm.at[pl.ds(pl.multiple_of(b * EXPERT_BLOCK, EXPERT_BLOCK),
                                                           EXPERT_BLOCK)], zsem))
            return c
        lax.fori_loop(ztab_ref[2 * N_EXPERTS], xs_hbm.shape[0] // EXPERT_BLOCK, tail, 0)

    @pl.when(i == 0)
    def _():
        zbuf[...] = jnp.zeros_like(zbuf)
        for_zero_runs(lambda cp: cp.start())

    jp = jpos_ref[...]
    xb = x_ref[...].astype(BF16)
    total = tab_ref[2 * N_EXPERTS]

    def permute(r0):
        iota_j = lax.broadcasted_iota(jnp.int32, (PERM_CHUNK, t), 0) + r0
        p = jnp.zeros((PERM_CHUNK, t), F32)
        for k in range(TOP_K):
            p = jnp.where(iota_j == jp[k:k + 1, :], 1.0, p)
        sbuf[slot, r0:r0 + PERM_CHUNK, :] = jnp.dot(p.astype(BF16), xb, preferred_element_type=F32).astype(BF16)

    for r0 in range(0, rows, PERM_CHUNK):
        if r0 < t * TOP_K + PERM_CHUNK:
            permute(r0)
        else:
            pl.when(total > r0)(functools.partial(permute, r0))

    _run_copies(tab_ref, lambda loc, glob, n: pltpu.make_async_copy(
        sbuf.at[slot, pl.ds(loc, n)], xs_hbm.at[pl.ds(glob, n)], sems.at[slot]), unroll=True)

    def wait_rows(s, n):
        pltpu.make_async_copy(sbuf.at[s, pl.ds(0, n)], xs_hbm.at[pl.ds(0, n)], sems.at[s]).wait()

    @pl.when(i > 0)
    def _():
        wait_rows(1 - slot, _aligned(tab_ref[2 * N_EXPERTS + 1]))

    @pl.when(i == pl.num_programs(0) - 1)
    def _():
        wait_rows(slot, _aligned(tab_ref[2 * N_EXPERTS]))

    @pl.when(i == 0)
    def _():
        for_zero_runs(lambda cp: cp.wait())


def _dispatch(tab, ztab, jpos_t, hx2, xs_rows):
    n, d = hx2.shape
    col = lambda i: (0, i)
    return pl.pallas_call(
        _dispatch_kernel,
        grid=(n // TOKEN_TILE,),
        in_specs=[pl.BlockSpec((TAB_WIDTH,), lambda i: (i,), memory_space=pltpu.SMEM),
                  pl.BlockSpec(memory_space=pltpu.SMEM),
                  pl.BlockSpec((TOP_K, TOKEN_TILE), col),
                  pl.BlockSpec((TOKEN_TILE, d), lambda i: (i, 0))],
        out_specs=pl.BlockSpec(memory_space=pl.ANY),
        out_shape=jax.ShapeDtypeStruct((xs_rows, d), BF16),
        scratch_shapes=[pltpu.VMEM((2, SORTED_ROWS, d), BF16),
                        pltpu.VMEM((EXPERT_BLOCK, d), BF16),
                        pltpu.SemaphoreType.DMA((2,)),
                        pltpu.SemaphoreType.DMA],
        compiler_params=_params(("arbitrary",)),
        name="moe_dispatch",
    )(tab, ztab, jpos_t, hx2)


def _experts_kernel(be_ref, nb_ref, xs_ref, wgu_ref, wd_ref, ys_ref, wgu_b, wd_b):
    i = pl.program_id(0)
    ff = wd_b.shape[0]
    used = i < nb_ref[0]
    new_expert = (i == 0) | (be_ref[i] != be_ref[jnp.maximum(i - 1, 0)])

    @pl.when(used & new_expert)
    def _():
        wgu_b[...] = wgu_ref[0, 0].astype(BF16)
        wd_b[...] = wd_ref[0, 0].astype(BF16)

    @pl.when(used)
    def _():
        starts = range(0, xs_ref.shape[0], MIX_SUB)
        hs = [jnp.dot(xs_ref[r0:r0 + MIX_SUB, :], wgu_b[...], preferred_element_type=F32) for r0 in starts]
        for r0, h in zip(starts, hs):
            a = _silu(h[:, :ff]) * h[:, ff:]
            ys_ref[r0:r0 + MIX_SUB, :] = jnp.dot(a.astype(BF16), wd_b[...],
                                                 preferred_element_type=F32).astype(BF16)

    @pl.when(jnp.logical_not(used))
    def _():
        ys_ref[...] = jnp.zeros_like(ys_ref)


def _experts(block_e, n_used, xs, w_gu, w_down, layer):
    rows, d = xs.shape
    nb = rows // EXPERT_BLOCK
    blk = lambda i, be, nu: (jnp.maximum(jnp.minimum(i, nu[0] - 1), 0), 0)
    wmap = lambda i, be, nu: (layer, be[i], 0, 0)
    return pl.pallas_call(
        _experts_kernel,
        grid_spec=pltpu.PrefetchScalarGridSpec(
            num_scalar_prefetch=2,
            grid=(nb,),
            in_specs=[pl.BlockSpec((EXPERT_BLOCK, d), blk),
                      pl.BlockSpec((1, 1) + w_gu.shape[2:], wmap),
                      pl.BlockSpec((1, 1) + w_down.shape[2:], wmap)],
            out_specs=pl.BlockSpec((EXPERT_BLOCK, d), lambda i, be, nu: (i, 0)),
            scratch_shapes=[pltpu.VMEM(w_gu.shape[2:], BF16), pltpu.VMEM(w_down.shape[2:], BF16)]),
        out_shape=jax.ShapeDtypeStruct((rows, d), BF16),
        compiler_params=_params(("arbitrary",)),
        name="moe_experts",
    )(block_e, n_used, xs, w_gu, w_down)


def _combine_kernel(tab_ref, jpos_ref, gw_ref, ys_hbm, hx2_ref, x1_ref, mod_ref, wsgu_ref, wsd_ref, fg_ref,
                    out_ref, ybuf, sem, *, tiles_per_seq, final_norm):
    t_rows, d = x1_ref.shape
    ff = wsd_ref.shape[0]
    rows = ybuf.shape[0]
    row = pl.program_id(0) // tiles_per_seq

    @pl.when(pl.program_id(0) == 0)
    def _():
        ybuf[...] = jnp.zeros_like(ybuf)

    _run_copies(tab_ref, lambda loc, glob, n: pltpu.make_async_copy(
        ys_hbm.at[pl.ds(glob, n)], ybuf.at[pl.ds(loc, n)], sem), unroll=False)

    hs = jnp.dot(hx2_ref[...].astype(BF16), wsgu_ref[...], preferred_element_type=F32)
    shared = jnp.dot((_silu(hs[:, :ff]) * hs[:, ff:]).astype(BF16), wsd_ref[...], preferred_element_type=F32)

    jp = jpos_ref[...]
    gw = gw_ref[...]

    iota_j = lax.broadcasted_iota(jnp.int32, (t_rows, rows), 1)
    g = jnp.zeros((t_rows, rows), F32)
    for k in range(TOP_K):
        g = jnp.where(iota_j == jp[:, k:k + 1], gw[:, k:k + 1], g)
    gb = g.astype(BF16)

    total = _aligned(tab_ref[2 * N_EXPERTS])
    pltpu.make_async_copy(ys_hbm.at[pl.ds(0, total)], ybuf.at[pl.ds(0, total)], sem).wait()
    routed = jnp.dot(gb, ybuf[...], preferred_element_type=F32)
    x2 = x1_ref[...] + _mod_chunk(mod_ref, row, 5, d) * (routed + shared)
    if final_norm:
        x2 = x2 * lax.rsqrt(jnp.mean(x2 * x2, axis=-1, keepdims=True) + NORM_EPS) * fg_ref[...]
    out_ref[...] = x2


def _combine(tab, jpos, gw, ys, hx2, x1, mod, ws_gu, ws_down, fg, tiles_per_seq, final_norm):
    n, d = x1.shape
    row = lambda i: (i, 0)
    const = lambda i: (0, 0)
    return pl.pallas_call(
        functools.partial(_combine_kernel, tiles_per_seq=tiles_per_seq, final_norm=final_norm),
        grid=(n // TOKEN_TILE,),
        in_specs=[pl.BlockSpec((TAB_WIDTH,), lambda i: (i,), memory_space=pltpu.SMEM),
                  pl.BlockSpec((TOKEN_TILE, TOP_K), row),
                  pl.BlockSpec((TOKEN_TILE, TOP_K), row),
                  pl.BlockSpec(memory_space=pl.ANY),
                  pl.BlockSpec((TOKEN_TILE, d), row),
                  pl.BlockSpec((TOKEN_TILE, d), row),
                  pl.BlockSpec(mod.shape, const),
                  pl.BlockSpec(ws_gu.shape, const),
                  pl.BlockSpec(ws_down.shape, const),
                  pl.BlockSpec((1, d), const)],
        out_specs=pl.BlockSpec((TOKEN_TILE, d), row),
        out_shape=jax.ShapeDtypeStruct((n, d), F32),
        scratch_shapes=[pltpu.VMEM((SORTED_ROWS, d), BF16), pltpu.SemaphoreType.DMA],
        compiler_params=_params(("arbitrary",)),
        name="moe_combine",
    )(tab, jpos, gw, ys, hx2, x1, mod, ws_gu, ws_down, fg)


def _moe(x1, hx2, eidx_t, gw_t, mask_t, cnt, mod, w_gu, w_down, layer, ws_gu, ws_down, fg, tiles_per_seq,
         final_norm):
    n, d = x1.shape
    n_tiles = n // TOKEN_TILE
    counts = cnt[:, 0].astype(jnp.int32)
    padded = (counts + EXPERT_BLOCK - 1) // EXPERT_BLOCK * EXPERT_BLOCK
    pad_end = jnp.cumsum(padded)
    pad_start = pad_end - padded
    max_rows = n * TOP_K + n_tiles * N_EXPERTS * ROW_ALIGN + N_EXPERTS * (EXPERT_BLOCK - ROW_ALIGN)
    n_blocks = (max_rows + EXPERT_BLOCK - 1) // EXPERT_BLOCK
    n_used = (pad_end[-1:] // EXPERT_BLOCK).astype(jnp.int32)
    block_start = jnp.arange(n_blocks, dtype=jnp.int32) * EXPERT_BLOCK
    block_e = jnp.minimum(jnp.sum((pad_end[None, :] <= block_start[:, None]).astype(jnp.int32), axis=1),
                          N_EXPERTS - 1)

    jpos_t, tab_f = _rank(mask_t, eidx_t, pad_start.astype(F32).reshape(N_EXPERTS, 1))
    run_len = tab_f[:, :, 0].astype(jnp.int32)
    run_start = tab_f[:, :, 1].astype(jnp.int32)
    total = jnp.sum(run_len, axis=1, keepdims=True)
    prev_total = jnp.concatenate([jnp.zeros((1, 1), jnp.int32), total[:-1]], axis=0)
    fill = jnp.zeros((n_tiles, TAB_WIDTH - 2 * N_EXPERTS - 2), jnp.int32)
    tab = jnp.concatenate([run_len, run_start, total, prev_total, fill], axis=1).reshape(-1)
    ztab = jnp.concatenate([padded - counts, pad_start + counts, n_used])

    xs = _dispatch(tab, ztab, jpos_t, hx2, n_blocks * EXPERT_BLOCK)
    ys = _experts(block_e, n_used, xs, w_gu, w_down, layer)
    return _combine(tab, jpos_t.T, gw_t.T, ys, hx2, x1, mod, ws_gu, ws_down, fg, tiles_per_seq, final_norm)


def _sincos_2d(rows, d):
    quarter = d // 4
    omega = 1.0 / (POS_BASE ** (jnp.arange(quarter, dtype=F32) / quarter))

    def emb(n):
        p = jnp.arange(n, dtype=F32)[:, None] * omega[None, :]
        return jnp.concatenate([jnp.sin(p), jnp.cos(p)], axis=-1)

    er, ec = emb(rows), emb(GRID_W)
    pe = jnp.concatenate([jnp.broadcast_to(er[:, None, :], (rows, GRID_W, d // 2)),
                          jnp.broadcast_to(ec[None, :, :], (rows, GRID_W, d // 2))], axis=-1)
    return pe.reshape(rows * GRID_W, d)


def kernel(x, c, ctx, c_ctx, ada_w, ada_b, mix_norm_g, ffn_norm_g, a_w_in, a_conv_w, a_conv_b, a_gate_r_w, a_gate_r_b, a_gate_i_w, a_gate_i_b, a_lambda, a_w_out, b_w_in, b_ln_g, b_ln_b, b_w_s, b_b_s, b_w_out, router_w, router_b, moe_w_gu, moe_w_down, shared_w_gu, shared_w_down, final_norm_g):
    bsz, s, d = x.shape
    ctx_len = ctx.shape[1]
    depth = ada_w.shape[0]
    assert depth == 2 and bsz < MOD_ROWS and s % MIX_TILE == 0 and MIX_TILE % TOKEN_TILE == 0
    assert ctx_len % TOKEN_TILE == 0
    n = bsz * s
    tps = s // TOKEN_TILE
    ctx_row = bsz

    cc = jnp.zeros((MOD_ROWS, d), F32).at[:bsz].set(c).at[ctx_row].set(c_ctx)
    mod = _modulation(cc, ada_w, ada_b)
    pe = _sincos_2d(s // GRID_W, d)
    rc = a_w_in.shape[2] // 2

    w_in0 = a_w_in[0].astype(BF16)
    g_mix0 = mix_norm_g[0].reshape(1, d)
    x0, gate, ux = _rglru_in(x.reshape(n, d), pe, mod[0], g_mix0, w_in0)
    uc = _ctx_in(ctx.reshape(bsz * ctx_len, d), mod[0], g_mix0, w_in0[:, rc:], ctx_row)
    w_ri = jnp.concatenate([a_gate_r_w[0], a_gate_i_w[0]], axis=-1).astype(BF16)
    conv = (a_conv_w[0], a_conv_b[0])
    gates = [(w_ri[k], a_gate_r_b[0, k], a_gate_i_b[0, k], a_lambda[0, k]) for k in range(2)]
    h_zero = jnp.zeros((bsz, rc), F32)
    h_fwd, uc = _lru_scan(uc, conv, *gates[0], h_zero, reverse=False, reset_first=True, emit_y=False)
    h_rev = _lru_scan(uc, None, *gates[1], h_zero, reverse=True, reset_first=True, emit_y=False)
    y_fwd, ux = _lru_scan(ux, conv, *gates[0], h_fwd, reverse=False, reset_first=False, emit_y=True)
    y_rev = _lru_scan(ux, None, *gates[1], h_rev, reverse=True, reset_first=False, emit_y=True)
    pre = _rglru_out(y_fwd, y_rev, gate, x0, mod[0], a_w_out[0].astype(BF16), ffn_norm_g[0].reshape(1, d),
                     router_w[0].T, router_b[0].reshape(N_EXPERTS, 1), s // MIX_TILE)
    x1 = _moe(*pre, mod[0], moe_w_gu, moe_w_down, 0, shared_w_gu[0].astype(BF16),
              shared_w_down[0].astype(BF16), final_norm_g.reshape(1, d), tps, False)

    pre = _sgu(x1, mod[1], mix_norm_g[1].reshape(1, d), b_w_in[0].astype(BF16),
               b_ln_g[0].reshape(1, -1), b_ln_b[0].reshape(1, -1), b_w_s[0].astype(BF16), b_b_s[0].T,
               b_w_out[0].astype(BF16), ffn_norm_g[1].reshape(1, d),
               router_w[1].T, router_b[1].reshape(N_EXPERTS, 1), s // MIX_TILE)
    out = _moe(*pre, mod[1], moe_w_gu, moe_w_down, 1, shared_w_gu[1].astype(BF16),
               shared_w_down[1].astype(BF16), final_norm_g.reshape(1, d), tps, True)
    return out.reshape(bsz, s, d)
```

```python
import functools

import jax
import jax.numpy as jnp
from jax import lax
from jax.experimental import pallas as pl
from jax.experimental.pallas import tpu as pltpu

F32 = jnp.float32
BF16 = jnp.bfloat16
HIGHEST = lax.Precision.HIGHEST

GRID_W = 64
N_MOD = 6
NORM_EPS = 1e-6
POS_BASE = 10000.0
RNN_HEADS = 5
CONV_WIDTH = 4
CONV_PAD_LEFT = 2
LRU_C = 8.0
SGU_HEADS = 8
CHUNK = 128
N_EXPERTS = 64
TOP_K = 8
N_GROUPS = 8
TOPK_GROUPS = 4
EXPERTS_PER_GROUP = N_EXPERTS // N_GROUPS
ROUTED_SCALE = 2.5

SUBLANES = 8
ROW_ALIGN = 16
LANES = 128
MOD_ROWS = 8
TOKEN_TILE = 256
MIX_TILE = 512
MIX_SUB = 256
EXPERT_BLOCK = 512
SORTED_ROWS = TOKEN_TILE * TOP_K + N_EXPERTS * ROW_ALIGN
PERM_CHUNK = 512
TAB_WIDTH = 256
POS_RADIX = 64
VMEM_LIMIT = 56 * 1024 * 1024


def _params(semantics, vmem=VMEM_LIMIT):
    return pltpu.CompilerParams(dimension_semantics=semantics, vmem_limit_bytes=vmem)


def _silu(x):
    return x * jax.nn.sigmoid(x)


def _rms_mod(x, g, sc, sh):
    y = x * lax.rsqrt(jnp.mean(x * x, axis=-1, keepdims=True) + NORM_EPS)
    return (y * g) * (1.0 + sc) + sh


def _mod_chunk(mod_ref, row, k, d):
    return mod_ref[pl.ds(row, 1), k * d:(k + 1) * d]


def _round_up_rows(count):
    return jnp.maximum(jnp.ceil(count * (1.0 / ROW_ALIGN)), 1.0) * float(ROW_ALIGN)


def _mod_kernel(cc_ref, w_ref, b_ref, o_ref):
    s = _silu(cc_ref[...])
    o_ref[0] = jnp.dot(s, w_ref[0], preferred_element_type=F32, precision=HIGHEST) + b_ref[0]


def _modulation(cc, ada_w, ada_b):
    depth, d, nd = ada_w.shape
    return pl.pallas_call(
        _mod_kernel,
        grid=(depth, nd // d),
        in_specs=[pl.BlockSpec((MOD_ROWS, d), lambda l, j: (0, 0)),
                  pl.BlockSpec((1, d, d), lambda l, j: (l, 0, j)),
                  pl.BlockSpec((1, 1, d), lambda l, j: (l, 0, j))],
        out_specs=pl.BlockSpec((1, MOD_ROWS, d), lambda l, j: (l, 0, j)),
        out_shape=jax.ShapeDtypeStruct((depth, MOD_ROWS, nd), F32),
        compiler_params=_params(("arbitrary", "arbitrary")),
        name="modulation",
    )(cc, ada_w, ada_b.reshape(depth, 1, nd))


def _rglru_in_kernel(x_ref, pe_ref, mod_ref, g_ref, w_ref, x0_ref, gate_ref, u_ref, *, tiles_per_seq):
    d = x_ref.shape[1]
    c = u_ref.shape[1]
    row = pl.program_id(0) // tiles_per_seq
    x = x_ref[...] + pe_ref[...]
    hx = _rms_mod(x, g_ref[...], _mod_chunk(mod_ref, row, 1, d), _mod_chunk(mod_ref, row, 0, d))
    z = jnp.dot(hx.astype(BF16), w_ref[...], preferred_element_type=F32)
    x0_ref[...] = x
    gate_ref[...] = jax.nn.gelu(z[:, :c]).astype(BF16)
    u_ref[...] = z[:, c:]


def _rglru_in(x2, pe, mod, g, w_in):
    n, d = x2.shape
    c = w_in.shape[1] // 2
    s = pe.shape[0]
    tps = s // MIX_TILE
    row = lambda i: (i, 0)
    return pl.pallas_call(
        functools.partial(_rglru_in_kernel, tiles_per_seq=tps),
        grid=(n // MIX_TILE,),
        in_specs=[pl.BlockSpec((MIX_TILE, d), row),
                  pl.BlockSpec((MIX_TILE, d), lambda i: (i % tps, 0)),
                  pl.BlockSpec(mod.shape, lambda i: (0, 0)),
                  pl.BlockSpec((1, d), lambda i: (0, 0)),
                  pl.BlockSpec(w_in.shape, lambda i: (0, 0))],
        out_specs=[pl.BlockSpec((MIX_TILE, d), row),
                   pl.BlockSpec((MIX_TILE, c), row),
                   pl.BlockSpec((MIX_TILE, c), row)],
        out_shape=[jax.ShapeDtypeStruct((n, d), F32),
                   jax.ShapeDtypeStruct((n, c), BF16),
                   jax.ShapeDtypeStruct((n, c), F32)],
        compiler_params=_params(("arbitrary",)),
        name="rglru_in",
    )(x2, pe, mod, g, w_in)


def _ctx_in_kernel(x_ref, mod_ref, g_ref, w_ref, u_ref, *, ctx_row):
    d = x_ref.shape[1]
    hx = _rms_mod(x_ref[...], g_ref[...], _mod_chunk(mod_ref, ctx_row, 1, d),
                  _mod_chunk(mod_ref, ctx_row, 0, d))
    u_ref[...] = jnp.dot(hx.astype(BF16), w_ref[...], preferred_element_type=F32)


def _ctx_in(c2, mod, g, w_u, ctx_row):
    n, d = c2.shape
    c = w_u.shape[1]
    return pl.pallas_call(
        functools.partial(_ctx_in_kernel, ctx_row=ctx_row),
        grid=(n // TOKEN_TILE,),
        in_specs=[pl.BlockSpec((TOKEN_TILE, d), lambda i: (i, 0)),
                  pl.BlockSpec(mod.shape, lambda i: (0, 0)),
                  pl.BlockSpec((1, d), lambda i: (0, 0)),
                  pl.BlockSpec(w_u.shape, lambda i: (0, 0))],
        out_specs=pl.BlockSpec((TOKEN_TILE, c), lambda i: (i, 0)),
        out_shape=jax.ShapeDtypeStruct((n, c), F32),
        compiler_params=_params(("arbitrary",)),
        name="ctx_in",
    )(c2, mod, g, w_u)


def _log_sigmoid(x):
    return jnp.minimum(x, 0.0) - jnp.log1p(jnp.exp(-jnp.abs(x)))


def _lru_scan_kernel(*refs, n_tiles, reverse, conv_done, reset_first, emit_y):
    if conv_done:
        u_ref, wri_ref, rb_ref, ib_ref, lam_ref, h0_ref, out_ref, a_scr, b_scr, h_scr = refs
    else:
        (u_ref, up_ref, un_ref, cw_ref, cb_ref, wri_ref, rb_ref, ib_ref, lam_ref, h0_ref,
         out_ref, uc_ref, ubuf, a_scr, b_scr, h_scr) = refs
    t_rows, c = u_ref.shape
    hb = c // RNN_HEADS
    b = pl.program_id(0)
    j = pl.program_id(1)
    jj = n_tiles - 1 - j if reverse else j

    if conv_done:
        u = u_ref[...]
    else:
        ubuf[SUBLANES:SUBLANES + t_rows, :] = u_ref[...]
        ubuf[0:SUBLANES, :] = jnp.where(jj == 0, 0.0, up_ref[...])
        ubuf[SUBLANES + t_rows:, :] = jnp.where(jj == n_tiles - 1, 0.0, un_ref[...])
        u = cb_ref[...]
        for k in range(CONV_WIDTH):
            u = u + cw_ref[k:k + 1, :] * ubuf[pl.ds(SUBLANES - CONV_PAD_LEFT + k, t_rows), :]
        uc_ref[...] = u

    log_lam = LRU_C * _log_sigmoid(lam_ref[...])
    rows = lax.broadcasted_iota(jnp.int32, (t_rows, 1), 0)
    first_row = jnp.where(j == 0, t_rows - 1 if reverse else 0, -1)
    for h in range(RNN_HEADS):
        sl = slice(h * hb, (h + 1) * hb)
        uh = u[:, sl]
        z = jnp.dot(uh.astype(BF16), wri_ref[h], preferred_element_type=F32)
        r = jax.nn.sigmoid(z[:, :hb] + rb_ref[:, sl])
        ig = jax.nn.sigmoid(z[:, hb:] + ib_ref[:, sl])
        log_a = r * log_lam[:, sl]
        a = jnp.exp(log_a)
        mult = jnp.sqrt((1.0 - a) * (1.0 + a))
        if reset_first:
            mult = jnp.where(rows == first_row, 1.0, mult)
        a_scr[:, sl] = a
        b_scr[:, sl] = mult * ig * uh

    @pl.when(j == 0)
    def _():
        h_scr[...] = h0_ref[pl.ds(b, 1), :]

    n_groups = t_rows // SUBLANES

    def group(g, h):
        base = pl.multiple_of((n_groups - 1 - g if reverse else g) * SUBLANES, SUBLANES)
        for s in range(SUBLANES):
            r = base + (SUBLANES - 1 - s if reverse else s)
            h = a_scr[pl.ds(r, 1), :] * h + b_scr[pl.ds(r, 1), :]
            if emit_y:
                out_ref[pl.ds(r, 1), :] = h
        return h

    h = lax.fori_loop(0, n_groups, group, h_scr[...])
    h_scr[...] = h
    if not emit_y:
        @pl.when(j == n_tiles - 1)
        def _():
            out_ref[pl.ds(b, 1), :] = h


def _lru_scan(u, conv, w_ri, r_b, i_b, lam, h0, *, reverse, reset_first, emit_y):
    n, c = u.shape
    n_batch = h0.shape[0]
    n_tiles = n // n_batch // TOKEN_TILE
    sub = TOKEN_TILE // SUBLANES
    n_sub = n // SUBLANES
    const = lambda b, j: (0, 0)

    def tile(b, j):
        return b * n_tiles + (n_tiles - 1 - j if reverse else j)

    tile_spec = pl.BlockSpec((TOKEN_TILE, c), lambda b, j: (tile(b, j), 0))
    in_specs = [tile_spec]
    args = [u]
    scratch = []
    if conv is not None:
        conv_w, conv_b = conv
        in_specs += [pl.BlockSpec((SUBLANES, c), lambda b, j: (jnp.maximum(tile(b, j) * sub - 1, 0), 0)),
                     pl.BlockSpec((SUBLANES, c), lambda b, j: (jnp.minimum((tile(b, j) + 1) * sub, n_sub - 1), 0)),
                     pl.BlockSpec(conv_w.shape, const),
                     pl.BlockSpec((1, c), const)]
        args += [u, u, conv_w, conv_b.reshape(1, c)]
        scratch = [pltpu.VMEM((TOKEN_TILE + 2 * SUBLANES, c), F32)]
    in_specs += [pl.BlockSpec(w_ri.shape, lambda b, j: (0, 0, 0)),
                 pl.BlockSpec((1, c), const), pl.BlockSpec((1, c), const), pl.BlockSpec((1, c), const),
                 pl.BlockSpec(h0.shape, const)]
    args += [w_ri, r_b.reshape(1, c), i_b.reshape(1, c), lam.reshape(1, c), h0]
    if emit_y:
        out_specs = [tile_spec]
        out_shape = [jax.ShapeDtypeStruct((n, c), F32)]
    else:
        out_specs = [pl.BlockSpec(h0.shape, const)]
        out_shape = [jax.ShapeDtypeStruct(h0.shape, F32)]
    if conv is not None:
        out_specs.append(tile_spec)
        out_shape.append(jax.ShapeDtypeStruct((n, c), F32))
    name = ("lru_scan" if emit_y else "lru_ctx") + ("_rev" if reverse else "_fwd")
    outs = pl.pallas_call(
        functools.partial(_lru_scan_kernel, n_tiles=n_tiles, reverse=reverse, conv_done=conv is None,
                          reset_first=reset_first, emit_y=emit_y),
        grid=(n_batch, n_tiles),
        in_specs=in_specs,
        out_specs=out_specs,
        out_shape=out_shape,
        scratch_shapes=scratch + [pltpu.VMEM((TOKEN_TILE, c), F32),
                                  pltpu.VMEM((TOKEN_TILE, c), F32),
                                  pltpu.VMEM((1, c), F32)],
        compiler_params=_params(("arbitrary", "arbitrary")),
        name=name,
    )(*args)
    return outs if conv is not None else outs[0]


def _route(logits, rb):
    e, t = logits.shape
    neg = -jnp.inf
    scores = jax.nn.sigmoid(logits)
    sel = scores + rb
    iota_g = lax.broadcasted_iota(jnp.int32, (N_GROUPS, t), 0).astype(F32)
    iota_e = lax.broadcasted_iota(jnp.int32, (e, t), 0).astype(F32)

    gs = jnp.full((N_GROUPS, t), neg, F32)
    for g in range(N_GROUPS):
        sg = sel[g * EXPERTS_PER_GROUP:(g + 1) * EXPERTS_PER_GROUP, :]
        m1 = jnp.max(sg, axis=0, keepdims=True)
        i1 = jnp.min(jnp.where(sg == m1, iota_g, float(EXPERTS_PER_GROUP)), axis=0, keepdims=True)
        m2 = jnp.max(jnp.where(iota_g == i1, neg, sg), axis=0, keepdims=True)
        gs = jnp.where(iota_g == float(g), m1 + m2, gs)

    keep = jnp.zeros((N_GROUPS, t), F32)
    for _ in range(TOPK_GROUPS):
        m = jnp.max(gs, axis=0, keepdims=True)
        idx = jnp.min(jnp.where(gs == m, iota_g, float(N_GROUPS)), axis=0, keepdims=True)
        hit = iota_g == idx
        keep = jnp.where(hit, 1.0, keep)
        gs = jnp.where(hit, neg, gs)

    masked = jnp.concatenate(
        [jnp.where(keep[g:g + 1, :] > 0.0, sel[g * EXPERTS_PER_GROUP:(g + 1) * EXPERTS_PER_GROUP, :], neg)
         for g in range(N_GROUPS)], axis=0)

    iota_k = lax.broadcasted_iota(jnp.int32, (TOP_K, t), 0)
    selmask = jnp.zeros((e, t), F32)
    eidx = jnp.zeros((TOP_K, t), F32)
    gw = jnp.zeros((TOP_K, t), F32)
    for k in range(TOP_K):
        m = jnp.max(masked, axis=0, keepdims=True)
        idx = jnp.min(jnp.where(masked == m, iota_e, float(e)), axis=0, keepdims=True)
        hit = iota_e == idx
        gk = jnp.sum(jnp.where(hit, scores, 0.0), axis=0, keepdims=True)
        masked = jnp.where(hit, neg, masked)
        selmask = jnp.where(hit, 1.0, selmask)
        eidx = jnp.where(iota_k == k, idx, eidx)
        gw = jnp.where(iota_k == k, gk, gw)
    gw = gw / jnp.sum(gw, axis=0, keepdims=True) * ROUTED_SCALE
    return eidx, gw, selmask


def _ffn_pre(x1, mod_ref, row, g2_ref, rwt_ref, rb_ref, hx2_ref, eidx_ref, gw_ref, mask_ref, cnt_ref):
    d = x1.shape[1]
    hx2 = _rms_mod(x1, g2_ref[...], _mod_chunk(mod_ref, row, 4, d), _mod_chunk(mod_ref, row, 3, d))
    hx2_ref[...] = hx2
    logits = lax.dot_general(rwt_ref[...], hx2, (((1,), (1,)), ((), ())),
                             preferred_element_type=F32, precision=HIGHEST)
    eidx, gw, selmask = _route(logits, rb_ref[...])
    eidx_ref[...] = eidx.astype(jnp.int32)
    gw_ref[...] = gw
    mask_ref[...] = selmask

    @pl.when(pl.program_id(0) == 0)
    def _():
        cnt_ref[...] = jnp.zeros_like(cnt_ref)

    rows = jnp.zeros((selmask.shape[0], 1), F32)
    for t0 in range(0, selmask.shape[1], TOKEN_TILE):
        rows = rows + _round_up_rows(jnp.sum(selmask[:, t0:t0 + TOKEN_TILE], axis=1, keepdims=True))
    cnt_ref[...] += jnp.broadcast_to(rows, cnt_ref.shape)


def _ffn_pre_specs(n, d):
    row = lambda i: (i, 0)
    col = lambda i: (0, i)
    out_specs = [pl.BlockSpec((MIX_TILE, d), row),
                 pl.BlockSpec((MIX_TILE, d), row),
                 pl.BlockSpec((TOP_K, MIX_TILE), col),
                 pl.BlockSpec((TOP_K, MIX_TILE), col),
                 pl.BlockSpec((N_EXPERTS, MIX_TILE), col),
                 pl.BlockSpec((N_EXPERTS, LANES), lambda i: (0, 0))]
    out_shape = [jax.ShapeDtypeStruct((n, d), F32),
                 jax.ShapeDtypeStruct((n, d), F32),
                 jax.ShapeDtypeStruct((TOP_K, n), jnp.int32),
                 jax.ShapeDtypeStruct((TOP_K, n), F32),
                 jax.ShapeDtypeStruct((N_EXPERTS, n), F32),
                 jax.ShapeDtypeStruct((N_EXPERTS, 128), F32)]
    return out_specs, out_shape


def _rglru_out_kernel(yf_ref, yr_ref, gate_ref, x0_ref, mod_ref, wout_ref, g2_ref, rwt_ref, rb_ref,
                      x1_ref, hx2_ref, eidx_ref, gw_ref, mask_ref, cnt_ref, *, tiles_per_seq):
    d = x0_ref.shape[1]
    row = pl.program_id(0) // tiles_per_seq
    yx = yf_ref[...] + yr_ref[...]
    v = gate_ref[...].astype(F32) * yx
    out = jnp.dot(v.astype(BF16), wout_ref[...], preferred_element_type=F32)
    x1 = x0_ref[...] + _mod_chunk(mod_ref, row, 2, d) * out
    x1_ref[...] = x1
    _ffn_pre(x1, mod_ref, row, g2_ref, rwt_ref, rb_ref, hx2_ref, eidx_ref, gw_ref, mask_ref, cnt_ref)


def _rglru_out(y_fwd, y_rev, gate, x0, mod, w_out, g2, rwt, rb, tiles_per_seq):
    n, d = x0.shape
    c = gate.shape[1]
    row = lambda i: (i, 0)
    const = lambda i: (0, 0)
    out_specs, out_shape = _ffn_pre_specs(n, d)
    return pl.pallas_call(
        functools.partial(_rglru_out_kernel, tiles_per_seq=tiles_per_seq),
        grid=(n // MIX_TILE,),
        in_specs=[pl.BlockSpec((MIX_TILE, c), row),
                  pl.BlockSpec((MIX_TILE, c), row),
                  pl.BlockSpec((MIX_TILE, c), row),
                  pl.BlockSpec((MIX_TILE, d), row),
                  pl.BlockSpec(mod.shape, const),
                  pl.BlockSpec(w_out.shape, const),
                  pl.BlockSpec((1, d), const),
                  pl.BlockSpec(rwt.shape, const),
                  pl.BlockSpec(rb.shape, const)],
        out_specs=out_specs,
        out_shape=out_shape,
        compiler_params=_params(("arbitrary",)),
        name="rglru_out",
    )(y_fwd, y_rev, gate, x0, mod, w_out, g2, rwt, rb)


def _sgu_kernel(x_ref, mod_ref, g_ref, win_ref, lng_ref, lnb_ref, ws_ref, bst_ref, wout_ref,
                g2_ref, rwt_ref, rb_ref,
                x1_ref, hx2_ref, eidx_ref, gw_ref, mask_ref, cnt_ref, m_scr, *, tiles_per_seq):
    t_rows, d = x_ref.shape
    w = wout_ref.shape[0]
    gd = w // SGU_HEADS
    row = pl.program_id(0) // tiles_per_seq
    x = x_ref[...]
    hx = _rms_mod(x, g_ref[...], _mod_chunk(mod_ref, row, 1, d), _mod_chunk(mod_ref, row, 0, d))
    z = jax.nn.gelu(jnp.dot(hx.astype(BF16), win_ref[...], preferred_element_type=F32))
    u = z[:, :w]
    v = z[:, w:]
    mu = jnp.mean(v, axis=-1, keepdims=True)
    vc = v - mu
    v = vc * lax.rsqrt(jnp.mean(vc * vc, axis=-1, keepdims=True) + NORM_EPS) * lng_ref[...] + lnb_ref[...]
    vb = v.astype(BF16)
    for ch in range(t_rows // CHUNK):
        rs = slice(ch * CHUNK, (ch + 1) * CHUNK)
        for g in range(SGU_HEADS):
            cs = slice(g * gd, (g + 1) * gd)
            sv = jnp.dot(ws_ref[g], vb[rs, cs], preferred_element_type=F32) + bst_ref[:, g:g + 1]
            m_scr[rs, cs] = (u[rs, cs] * sv).astype(BF16)
    out = jnp.dot(m_scr[...], wout_ref[...], preferred_element_type=F32)
    x1 = x + _mod_chunk(mod_ref, row, 2, d) * out
    x1_ref[...] = x1
    _ffn_pre(x1, mod_ref, row, g2_ref, rwt_ref, rb_ref, hx2_ref, eidx_ref, gw_ref, mask_ref, cnt_ref)


def _sgu(x, mod, g, w_in, ln_g, ln_b, w_s, b_st, w_out, g2, rwt, rb, tiles_per_seq):
    n, d = x.shape
    w = w_out.shape[0]
    const = lambda i: (0, 0)
    out_specs, out_shape = _ffn_pre_specs(n, d)
    return pl.pallas_call(
        functools.partial(_sgu_kernel, tiles_per_seq=tiles_per_seq),
        grid=(n // MIX_TILE,),
        in_specs=[pl.BlockSpec((MIX_TILE, d), lambda i: (i, 0)),
                  pl.BlockSpec(mod.shape, const),
                  pl.BlockSpec((1, d), const),
                  pl.BlockSpec(w_in.shape, const, pipeline_mode=pl.Buffered(1)),
                  pl.BlockSpec((1, w), const),
                  pl.BlockSpec((1, w), const),
                  pl.BlockSpec(w_s.shape, lambda i: (0, 0, 0)),
                  pl.BlockSpec(b_st.shape, const),
                  pl.BlockSpec(w_out.shape, const, pipeline_mode=pl.Buffered(1)),
                  pl.BlockSpec((1, d), const),
                  pl.BlockSpec(rwt.shape, const),
                  pl.BlockSpec(rb.shape, const)],
        out_specs=out_specs,
        out_shape=out_shape,
        scratch_shapes=[pltpu.VMEM((MIX_TILE, w), BF16)],
        compiler_params=_params(("arbitrary",)),
        name="sgu",
    )(x, mod, g, w_in, ln_g, ln_b, w_s, b_st, w_out, g2, rwt, rb)


def _rank_kernel(mask_ref, eidx_ref, gw_ref, pstart_ref, hi_ref, lo_ref, gate_ref, tab_ref, tabt_ref, carry):
    e, t = mask_ref.shape

    @pl.when(pl.program_id(0) == 0)
    def _():
        carry[...] = jnp.zeros_like(carry)

    m = mask_ref[...]
    r = lax.broadcasted_iota(jnp.int32, (t, t), 0)
    c = lax.broadcasted_iota(jnp.int32, (t, t), 1)
    upper = jnp.where(r <= c, 1.0, 0.0).astype(BF16)
    incl = jnp.dot(m.astype(BF16), upper, preferred_element_type=F32)
    run = _round_up_rows(incl[:, t - 1:t])
    lanes = tab_ref.shape[2]
    re = lax.broadcasted_iota(jnp.int32, (e, e), 0)
    ce = lax.broadcasted_iota(jnp.int32, (e, e), 1)
    lower = jnp.where(ce < re, 1.0, 0.0).astype(BF16)
    tiles = jnp.broadcast_to(run * (1.0 / ROW_ALIGN), (e, lanes)).astype(BF16)
    lstart = jnp.dot(lower, tiles, preferred_element_type=F32)[:, 0:1] * float(ROW_ALIGN)
    pos = jnp.where(m > 0.0, lstart + incl - m, float(POS_RADIX * POS_RADIX - 1))
    hi = jnp.floor(pos * (1.0 / POS_RADIX))
    hi_ref[...] = hi.astype(BF16)
    lo_ref[...] = (pos - hi * float(POS_RADIX)).astype(BF16)
    iota_e = lax.broadcasted_iota(jnp.int32, (e, t), 0)
    eidx = eidx_ref[...]
    gw = gw_ref[...]
    gate = jnp.zeros((e, t), F32)
    for k in range(TOP_K):
        gate = jnp.where(iota_e == eidx[k:k + 1, :], gw[k:k + 1, :], gate)
    gate_ref[...] = gate.astype(BF16)
    lane = lax.broadcasted_iota(jnp.int32, (e, lanes), 1)
    tab_ref[0] = jnp.where(lane == 0, run, jnp.where(lane == 1, pstart_ref[...] + carry[...],
                                                     jnp.where(lane == 2, lstart, 0.0)))
    diag = lax.broadcasted_iota(jnp.int32, (e, lanes), 0) == lane
    row_of = lambda col: jnp.sum(jnp.where(diag, jnp.broadcast_to(col, (e, lanes)), 0.0), axis=0, keepdims=True)
    sub = lax.broadcasted_iota(jnp.int32, (SUBLANES, lanes), 0)
    tabt_ref[0] = jnp.where(sub == 0, row_of(lstart), jnp.where(sub == 1, row_of(lstart + run), 0.0))
    carry[...] = carry[...] + run


def _rank(mask_t, eidx_t, gw_t, pstart):
    e, n = mask_t.shape
    n_tiles = n // TOKEN_TILE
    col = lambda i: (0, i)
    return pl.pallas_call(
        _rank_kernel,
        grid=(n_tiles,),
        in_specs=[pl.BlockSpec((e, TOKEN_TILE), col),
                  pl.BlockSpec((TOP_K, TOKEN_TILE), col),
                  pl.BlockSpec((TOP_K, TOKEN_TILE), col),
                  pl.BlockSpec((e, 1), lambda i: (0, 0))],
        out_specs=[pl.BlockSpec((e, TOKEN_TILE), col),
                   pl.BlockSpec((e, TOKEN_TILE), col),
                   pl.BlockSpec((e, TOKEN_TILE), col),
                   pl.BlockSpec((1, e, LANES), lambda i: (i, 0, 0)),
                   pl.BlockSpec((1, SUBLANES, LANES), lambda i: (i, 0, 0))],
        out_shape=[jax.ShapeDtypeStruct((e, n), BF16),
                   jax.ShapeDtypeStruct((e, n), BF16),
                   jax.ShapeDtypeStruct((e, n), BF16),
                   jax.ShapeDtypeStruct((n_tiles, e, LANES), F32),
                   jax.ShapeDtypeStruct((n_tiles, SUBLANES, LANES), F32)],
        scratch_shapes=[pltpu.VMEM((e, 1), F32)],
        compiler_params=_params(("arbitrary",)),
        name="moe_rank",
    )(mask_t, eidx_t, gw_t, pstart)


def _aligned(v):
    return pl.multiple_of(v, ROW_ALIGN)


def _run_copies(tab_ref, make_copy, unroll):
    def body(e, off):
        rows = _aligned(tab_ref[e])
        make_copy(_aligned(off), _aligned(tab_ref[N_EXPERTS + e]), rows).start()
        return off + rows

    lax.fori_loop(0, N_EXPERTS, body, 0, unroll=unroll)


def _dispatch_kernel(tab_ref, ztab_ref, tabt_ref, hi_ref, lo_ref, x_ref, xs_hbm, sbuf, zbuf, sems, zsem):
    i = pl.program_id(0)
    slot = i % 2
    t = x_ref.shape[0]
    rows = sbuf.shape[1]

    def zero_copy(e):
        n = _aligned(ztab_ref[e])
        return pltpu.make_async_copy(zbuf.at[pl.ds(0, n)], xs_hbm.at[pl.ds(_aligned(ztab_ref[N_EXPERTS + e]), n)], zsem)

    def for_zero_runs(fn):
        def body(e, c):
            @pl.when(ztab_ref[e] > 0)
            def _():
                fn(zero_copy(e))
            return c
        lax.fori_loop(0, N_EXPERTS, body, 0)

        def tail(b, c):
            fn(pltpu.make_async_copy(zbuf, xs_hbm.at[pl.ds(pl.multiple_of(b * EXPERT_BLOCK, EXPERT_BLOCK),
                                                           EXPERT_BLOCK)], zsem))
            return c
        lax.fori_loop(ztab_ref[2 * N_EXPERTS], xs_hbm.shape[0] // EXPERT_BLOCK, tail, 0)

    @pl.when(i == 0)
    def _():
        zbuf[...] = jnp.zeros_like(zbuf)
        for_zero_runs(lambda cp: cp.start())

    xb = x_ref[...].astype(BF16)
    total = tab_ref[2 * N_EXPERTS]
    run_lo = tabt_ref[0, 0:1, 0:N_EXPERTS]
    run_hi = tabt_ref[0, 1:2, 0:N_EXPERTS]

    def permute(r0):
        j_e = (lax.broadcasted_iota(jnp.int32, (PERM_CHUNK, N_EXPERTS), 0) + r0).astype(F32)
        owner = jnp.where((j_e >= run_lo) & (j_e < run_hi), 1.0, 0.0).astype(BF16)
        pos = (jnp.dot(owner, hi_ref[...], preferred_element_type=F32) * float(POS_RADIX)
               + jnp.dot(owner, lo_ref[...], preferred_element_type=F32))
        j_t = (lax.broadcasted_iota(jnp.int32, (PERM_CHUNK, t), 0) + r0).astype(F32)
        p = jnp.where(pos == j_t, 1.0, 0.0).astype(BF16)
        sbuf[slot, r0:r0 + PERM_CHUNK, :] = jnp.dot(p, xb, preferred_element_type=F32).astype(BF16)

    for r0 in range(0, rows, PERM_CHUNK):
        if r0 < t * TOP_K + PERM_CHUNK:
            permute(r0)
        else:
            pl.when(total > r0)(functools.partial(permute, r0))

    _run_copies(tab_ref, lambda loc, glob, n: pltpu.make_async_copy(
        sbuf.at[slot, pl.ds(loc, n)], xs_hbm.at[pl.ds(glob, n)], sems.at[slot]), unroll=True)

    def wait_rows(s, n):
        pltpu.make_async_copy(sbuf.at[s, pl.ds(0, n)], xs_hbm.at[pl.ds(0, n)], sems.at[s]).wait()

    @pl.when(i > 0)
    def _():
        wait_rows(1 - slot, _aligned(tab_ref[2 * N_EXPERTS + 1]))

    @pl.when(i == pl.num_programs(0) - 1)
    def _():
        wait_rows(slot, _aligned(tab_ref[2 * N_EXPERTS]))

    @pl.when(i == 0)
    def _():
        for_zero_runs(lambda cp: cp.wait())


def _dispatch(tab, ztab, tab_t, pos_hi, pos_lo, hx2, xs_rows):
    n, d = hx2.shape
    col = lambda i: (0, i)
    return pl.pallas_call(
        _dispatch_kernel,
        grid=(n // TOKEN_TILE,),
        in_specs=[pl.BlockSpec((TAB_WIDTH,), lambda i: (i,), memory_space=pltpu.SMEM),
                  pl.BlockSpec(memory_space=pltpu.SMEM),
                  pl.BlockSpec((1, SUBLANES, LANES), lambda i: (i, 0, 0)),
                  pl.BlockSpec((N_EXPERTS, TOKEN_TILE), col),
                  pl.BlockSpec((N_EXPERTS, TOKEN_TILE), col),
                  pl.BlockSpec((TOKEN_TILE, d), lambda i: (i, 0))],
        out_specs=pl.BlockSpec(memory_space=pl.ANY),
        out_shape=jax.ShapeDtypeStruct((xs_rows, d), BF16),
        scratch_shapes=[pltpu.VMEM((2, SORTED_ROWS, d), BF16),
                        pltpu.VMEM((EXPERT_BLOCK, d), BF16),
                        pltpu.SemaphoreType.DMA((2,)),
                        pltpu.SemaphoreType.DMA],
        compiler_params=_params(("arbitrary",)),
        name="moe_dispatch",
    )(tab, ztab, tab_t, pos_hi, pos_lo, hx2)


def _experts_kernel(be_ref, nb_ref, xs_ref, wgu_ref, wd_ref, ys_ref, wgu_b, wd_b):
    i = pl.program_id(0)
    ff = wd_b.shape[0]
    used = i < nb_ref[0]
    new_expert = (i == 0) | (be_ref[i] != be_ref[jnp.maximum(i - 1, 0)])

    @pl.when(used & new_expert)
    def _():
        wgu_b[...] = wgu_ref[0, 0].astype(BF16)
        wd_b[...] = wd_ref[0, 0].astype(BF16)

    @pl.when(used)
    def _():
        starts = range(0, xs_ref.shape[0], MIX_SUB)
        hs = [jnp.dot(xs_ref[r0:r0 + MIX_SUB, :], wgu_b[...], preferred_element_type=F32) for r0 in starts]
        for r0, h in zip(starts, hs):
            a = _silu(h[:, :ff]) * h[:, ff:]
            ys_ref[r0:r0 + MIX_SUB, :] = jnp.dot(a.astype(BF16), wd_b[...],
                                                 preferred_element_type=F32).astype(BF16)

    @pl.when(jnp.logical_not(used))
    def _():
        ys_ref[...] = jnp.zeros_like(ys_ref)


def _experts(block_e, n_used, xs, w_gu, w_down, layer):
    rows, d = xs.shape
    nb = rows // EXPERT_BLOCK
    blk = lambda i, be, nu: (jnp.maximum(jnp.minimum(i, nu[0] - 1), 0), 0)
    wmap = lambda i, be, nu: (layer, be[i], 0, 0)
    return pl.pallas_call(
        _experts_kernel,
        grid_spec=pltpu.PrefetchScalarGridSpec(
            num_scalar_prefetch=2,
            grid=(nb,),
            in_specs=[pl.BlockSpec((EXPERT_BLOCK, d), blk),
                      pl.BlockSpec((1, 1) + w_gu.shape[2:], wmap),
                      pl.BlockSpec((1, 1) + w_down.shape[2:], wmap)],
            out_specs=pl.BlockSpec((EXPERT_BLOCK, d), lambda i, be, nu: (i, 0)),
            scratch_shapes=[pltpu.VMEM(w_gu.shape[2:], BF16), pltpu.VMEM(w_down.shape[2:], BF16)]),
        out_shape=jax.ShapeDtypeStruct((rows, d), BF16),
        compiler_params=_params(("arbitrary",)),
        name="moe_experts",
    )(block_e, n_used, xs, w_gu, w_down)


def _combine_kernel(tab_ref, tabv_ref, hi_ref, lo_ref, gate_ref, ys_hbm, hx2_ref, x1_ref, mod_ref, wsgu_ref,
                    wsd_ref, fg_ref, out_ref, ybuf, sem, *, tiles_per_seq, final_norm):
    t_rows, d = x1_ref.shape
    ff = wsd_ref.shape[0]
    rows = ybuf.shape[0]
    row = pl.program_id(0) // tiles_per_seq

    @pl.when(pl.program_id(0) == 0)
    def _():
        ybuf[...] = jnp.zeros_like(ybuf)

    _run_copies(tab_ref, lambda loc, glob, n: pltpu.make_async_copy(
        ys_hbm.at[pl.ds(glob, n)], ybuf.at[pl.ds(loc, n)], sem), unroll=False)

    hs = jnp.dot(hx2_ref[...].astype(BF16), wsgu_ref[...], preferred_element_type=F32)
    shared = jnp.dot((_silu(hs[:, :ff]) * hs[:, ff:]).astype(BF16), wsd_ref[...], preferred_element_type=F32)

    run_lo = tabv_ref[0, :, 2:3]
    run_hi = run_lo + tabv_ref[0, :, 0:1]
    j_e = lax.broadcasted_iota(jnp.int32, (N_EXPERTS, rows), 1).astype(F32)
    owner = jnp.where((j_e >= run_lo) & (j_e < run_hi), 1.0, 0.0).astype(BF16)
    pos = (jnp.dot(hi_ref[...], owner, preferred_element_type=F32) * float(POS_RADIX)
           + jnp.dot(lo_ref[...], owner, preferred_element_type=F32))
    gates = jnp.dot(gate_ref[...], owner, preferred_element_type=F32)
    j_t = lax.broadcasted_iota(jnp.int32, (t_rows, rows), 1).astype(F32)
    gb = jnp.where(pos == j_t, gates, 0.0).astype(BF16)

    total = _aligned(tab_ref[2 * N_EXPERTS])
    pltpu.make_async_copy(ys_hbm.at[pl.ds(0, total)], ybuf.at[pl.ds(0, total)], sem).wait()
    routed = jnp.dot(gb, ybuf[...], preferred_element_type=F32)
    x2 = x1_ref[...] + _mod_chunk(mod_ref, row, 5, d) * (routed + shared)
    if final_norm:
        x2 = x2 * lax.rsqrt(jnp.mean(x2 * x2, axis=-1, keepdims=True) + NORM_EPS) * fg_ref[...]
    out_ref[...] = x2


def _combine(tab, tab_v, pos_hi, pos_lo, gate, ys, hx2, x1, mod, ws_gu, ws_down, fg, tiles_per_seq, final_norm):
    n, d = x1.shape
    row = lambda i: (i, 0)
    const = lambda i: (0, 0)
    return pl.pallas_call(
        functools.partial(_combine_kernel, tiles_per_seq=tiles_per_seq, final_norm=final_norm),
        grid=(n // TOKEN_TILE,),
        in_specs=[pl.BlockSpec((TAB_WIDTH,), lambda i: (i,), memory_space=pltpu.SMEM),
                  pl.BlockSpec((1, N_EXPERTS, LANES), lambda i: (i, 0, 0)),
                  pl.BlockSpec((TOKEN_TILE, N_EXPERTS), row),
                  pl.BlockSpec((TOKEN_TILE, N_EXPERTS), row),
                  pl.BlockSpec((TOKEN_TILE, N_EXPERTS), row),
                  pl.BlockSpec(memory_space=pl.ANY),
                  pl.BlockSpec((TOKEN_TILE, d), row),
                  pl.BlockSpec((TOKEN_TILE, d), row),
                  pl.BlockSpec(mod.shape, const),
                  pl.BlockSpec(ws_gu.shape, const),
                  pl.BlockSpec(ws_down.shape, const),
                  pl.BlockSpec((1, d), const)],
        out_specs=pl.BlockSpec((TOKEN_TILE, d), row),
        out_shape=jax.ShapeDtypeStruct((n, d), F32),
        scratch_shapes=[pltpu.VMEM((SORTED_ROWS, d), BF16), pltpu.SemaphoreType.DMA],
        compiler_params=_params(("arbitrary",)),
        name="moe_combine",
    )(tab, tab_v, pos_hi, pos_lo, gate, ys, hx2, x1, mod, ws_gu, ws_down, fg)


def _moe(x1, hx2, eidx_t, gw_t, mask_t, cnt, mod, w_gu, w_down, layer, ws_gu, ws_down, fg, tiles_per_seq,
         final_norm):
    n, d = x1.shape
    n_tiles = n // TOKEN_TILE
    counts = cnt[:, 0].astype(jnp.int32)
    padded = (counts + EXPERT_BLOCK - 1) // EXPERT_BLOCK * EXPERT_BLOCK
    pad_end = jnp.cumsum(padded)
    pad_start = pad_end - padded
    max_rows = n * TOP_K + n_tiles * N_EXPERTS * ROW_ALIGN + N_EXPERTS * (EXPERT_BLOCK - ROW_ALIGN)
    n_blocks = (max_rows + EXPERT_BLOCK - 1) // EXPERT_BLOCK
    n_used = (pad_end[-1:] // EXPERT_BLOCK).astype(jnp.int32)
    block_start = jnp.arange(n_blocks, dtype=jnp.int32) * EXPERT_BLOCK
    block_e = jnp.minimum(jnp.sum((pad_end[None, :] <= block_start[:, None]).astype(jnp.int32), axis=1),
                          N_EXPERTS - 1)

    pos_hi, pos_lo, gate, tab_f, tab_t = _rank(mask_t, eidx_t, gw_t, pad_start.astype(F32).reshape(N_EXPERTS, 1))
    run_len = tab_f[:, :, 0].astype(jnp.int32)
    run_start = tab_f[:, :, 1].astype(jnp.int32)
    total = jnp.sum(run_len, axis=1, keepdims=True)
    prev_total = jnp.concatenate([jnp.zeros((1, 1), jnp.int32), total[:-1]], axis=0)
    fill = jnp.zeros((n_tiles, TAB_WIDTH - 2 * N_EXPERTS - 2), jnp.int32)
    tab = jnp.concatenate([run_len, run_start, total, prev_total, fill], axis=1).reshape(-1)
    ztab = jnp.concatenate([padded - counts, pad_start + counts, n_used])

    xs = _dispatch(tab, ztab, tab_t, pos_hi, pos_lo, hx2, n_blocks * EXPERT_BLOCK)
    ys = _experts(block_e, n_used, xs, w_gu, w_down, layer)
    return _combine(tab, tab_f, pos_hi.T, pos_lo.T, gate.T, ys, hx2, x1, mod, ws_gu, ws_down, fg,
                    tiles_per_seq, final_norm)


def _sincos_2d(rows, d):
    quarter = d // 4
    omega = 1.0 / (POS_BASE ** (jnp.arange(quarter, dtype=F32) / quarter))

    def emb(n):
        p = jnp.arange(n, dtype=F32)[:, None] * omega[None, :]
        return jnp.concatenate([jnp.sin(p), jnp.cos(p)], axis=-1)

    er, ec = emb(rows), emb(GRID_W)
    pe = jnp.concatenate([jnp.broadcast_to(er[:, None, :], (rows, GRID_W, d // 2)),
                          jnp.broadcast_to(ec[None, :, :], (rows, GRID_W, d // 2))], axis=-1)
    return pe.reshape(rows * GRID_W, d)


def kernel(x, c, ctx, c_ctx, ada_w, ada_b, mix_norm_g, ffn_norm_g, a_w_in, a_conv_w, a_conv_b, a_gate_r_w, a_gate_r_b, a_gate_i_w, a_gate_i_b, a_lambda, a_w_out, b_w_in, b_ln_g, b_ln_b, b_w_s, b_b_s, b_w_out, router_w, router_b, moe_w_gu, moe_w_down, shared_w_gu, shared_w_down, final_norm_g):
    bsz, s, d = x.shape
    ctx_len = ctx.shape[1]
    depth = ada_w.shape[0]
    assert depth == 2 and bsz < MOD_ROWS and s % MIX_TILE == 0 and MIX_TILE % TOKEN_TILE == 0
    assert ctx_len % TOKEN_TILE == 0
    n = bsz * s
    tps = s // TOKEN_TILE
    ctx_row = bsz

    cc = jnp.zeros((MOD_ROWS, d), F32).at[:bsz].set(c).at[ctx_row].set(c_ctx)
    mod = _modulation(cc, ada_w, ada_b)
    pe = _sincos_2d(s // GRID_W, d)
    rc = a_w_in.shape[2] // 2

    w_in0 = a_w_in[0].astype(BF16)
    g_mix0 = mix_norm_g[0].reshape(1, d)
    x0, gate, ux = _rglru_in(x.reshape(n, d), pe, mod[0], g_mix0, w_in0)
    uc = _ctx_in(ctx.reshape(bsz * ctx_len, d), mod[0], g_mix0, w_in0[:, rc:], ctx_row)
    w_ri = jnp.concatenate([a_gate_r_w[0], a_gate_i_w[0]], axis=-1).astype(BF16)
    conv = (a_conv_w[0], a_conv_b[0])
    gates = [(w_ri[k], a_gate_r_b[0, k], a_gate_i_b[0, k], a_lambda[0, k]) for k in range(2)]
    h_zero = jnp.zeros((bsz, rc), F32)
    h_fwd, uc = _lru_scan(uc, conv, *gates[0], h_zero, reverse=False, reset_first=True, emit_y=False)
    h_rev = _lru_scan(uc, None, *gates[1], h_zero, reverse=True, reset_first=True, emit_y=False)
    y_fwd, ux = _lru_scan(ux, conv, *gates[0], h_fwd, reverse=False, reset_first=False, emit_y=True)
    y_rev = _lru_scan(ux, None, *gates[1], h_rev, reverse=True, reset_first=False, emit_y=True)
    pre = _rglru_out(y_fwd, y_rev, gate, x0, mod[0], a_w_out[0].astype(BF16), ffn_norm_g[0].reshape(1, d),
                     router_w[0].T, router_b[0].reshape(N_EXPERTS, 1), s // MIX_TILE)
    x1 = _moe(*pre, mod[0], moe_w_gu, moe_w_down, 0, shared_w_gu[0].astype(BF16),
              shared_w_down[0].astype(BF16), final_norm_g.reshape(1, d), tps, False)

    pre = _sgu(x1, mod[1], mix_norm_g[1].reshape(1, d), b_w_in[0].astype(BF16),
               b_ln_g[0].reshape(1, -1), b_ln_b[0].reshape(1, -1), b_w_s[0].astype(BF16), b_b_s[0].T,
               b_w_out[0].astype(BF16), ffn_norm_g[1].reshape(1, d),
               router_w[1].T, router_b[1].reshape(N_EXPERTS, 1), s // MIX_TILE)
    out = _moe(*pre, mod[1], moe_w_gu, moe_w_down, 1, shared_w_gu[1].astype(BF16),
               shared_w_down[1].astype(BF16), final_norm_g.reshape(1, d), tps, True)
    return out.reshape(bsz, s, d)
```

```python
import functools

import jax
import jax.numpy as jnp
from jax import lax
from jax.experimental import pallas as pl
from jax.experimental.pallas import tpu as pltpu

F32 = jnp.float32
BF16 = jnp.bfloat16
HIGHEST = lax.Precision.HIGHEST

GRID_W = 64
N_MOD = 6
NORM_EPS = 1e-6
POS_BASE = 10000.0
RNN_HEADS = 5
CONV_WIDTH = 4
CONV_PAD_LEFT = 2
LRU_C = 8.0
SGU_HEADS = 8
CHUNK = 128
N_EXPERTS = 64
TOP_K = 8
N_GROUPS = 8
TOPK_GROUPS = 4
EXPERTS_PER_GROUP = N_EXPERTS // N_GROUPS
ROUTED_SCALE = 2.5

SUBLANES = 8
ROW_ALIGN = 16
LANES = 128
MOD_ROWS = 8
TOKEN_TILE = 256
MIX_TILE = 512
MIX_SUB = 256
EXPERT_BLOCK = 512
SORTED_ROWS = TOKEN_TILE * TOP_K + N_EXPERTS * ROW_ALIGN
PERM_CHUNK = 512
TAB_WIDTH = 256
TAB_LOCAL = 2 * N_EXPERTS + 2
POS_RADIX = 64
VMEM_LIMIT = 56 * 1024 * 1024


def _params(semantics, vmem=VMEM_LIMIT):
    return pltpu.CompilerParams(dimension_semantics=semantics, vmem_limit_bytes=vmem)


def _silu(x):
    return x * jax.nn.sigmoid(x)


def _rms_mod(x, g, sc, sh):
    y = x * lax.rsqrt(jnp.mean(x * x, axis=-1, keepdims=True) + NORM_EPS)
    return (y * g) * (1.0 + sc) + sh


def _mod_chunk(mod_ref, row, k, d):
    return mod_ref[pl.ds(row, 1), k * d:(k + 1) * d]


def _round_up_rows(count):
    return jnp.maximum(jnp.ceil(count * (1.0 / ROW_ALIGN)), 1.0) * float(ROW_ALIGN)


def _mod_kernel(cc_ref, w_ref, b_ref, o_ref):
    s = _silu(cc_ref[...])
    o_ref[0] = jnp.dot(s, w_ref[0], preferred_element_type=F32, precision=HIGHEST) + b_ref[0]


def _modulation(cc, ada_w, ada_b):
    depth, d, nd = ada_w.shape
    return pl.pallas_call(
        _mod_kernel,
        grid=(depth, nd // d),
        in_specs=[pl.BlockSpec((MOD_ROWS, d), lambda l, j: (0, 0)),
                  pl.BlockSpec((1, d, d), lambda l, j: (l, 0, j)),
                  pl.BlockSpec((1, 1, d), lambda l, j: (l, 0, j))],
        out_specs=pl.BlockSpec((1, MOD_ROWS, d), lambda l, j: (l, 0, j)),
        out_shape=jax.ShapeDtypeStruct((depth, MOD_ROWS, nd), F32),
        compiler_params=_params(("arbitrary", "arbitrary")),
        name="modulation",
    )(cc, ada_w, ada_b.reshape(depth, 1, nd))


def _rglru_in_kernel(x_ref, pe_ref, mod_ref, g_ref, w_ref, x0_ref, gate_ref, u_ref, *, tiles_per_seq):
    d = x_ref.shape[1]
    c = u_ref.shape[1]
    row = pl.program_id(0) // tiles_per_seq
    x = x_ref[...] + pe_ref[...]
    hx = _rms_mod(x, g_ref[...], _mod_chunk(mod_ref, row, 1, d), _mod_chunk(mod_ref, row, 0, d))
    z = jnp.dot(hx.astype(BF16), w_ref[...], preferred_element_type=F32)
    x0_ref[...] = x
    gate_ref[...] = jax.nn.gelu(z[:, :c]).astype(BF16)
    u_ref[...] = z[:, c:]


def _rglru_in(x2, pe, mod, g, w_in):
    n, d = x2.shape
    c = w_in.shape[1] // 2
    s = pe.shape[0]
    tps = s // MIX_TILE
    row = lambda i: (i, 0)
    return pl.pallas_call(
        functools.partial(_rglru_in_kernel, tiles_per_seq=tps),
        grid=(n // MIX_TILE,),
        in_specs=[pl.BlockSpec((MIX_TILE, d), row),
                  pl.BlockSpec((MIX_TILE, d), lambda i: (i % tps, 0)),
                  pl.BlockSpec(mod.shape, lambda i: (0, 0)),
                  pl.BlockSpec((1, d), lambda i: (0, 0)),
                  pl.BlockSpec(w_in.shape, lambda i: (0, 0))],
        out_specs=[pl.BlockSpec((MIX_TILE, d), row),
                   pl.BlockSpec((MIX_TILE, c), row),
                   pl.BlockSpec((MIX_TILE, c), row)],
        out_shape=[jax.ShapeDtypeStruct((n, d), F32),
                   jax.ShapeDtypeStruct((n, c), BF16),
                   jax.ShapeDtypeStruct((n, c), F32)],
        compiler_params=_params(("arbitrary",)),
        name="rglru_in",
    )(x2, pe, mod, g, w_in)


def _ctx_in_kernel(x_ref, mod_ref, g_ref, w_ref, u_ref, *, ctx_row):
    d = x_ref.shape[1]
    hx = _rms_mod(x_ref[...], g_ref[...], _mod_chunk(mod_ref, ctx_row, 1, d),
                  _mod_chunk(mod_ref, ctx_row, 0, d))
    u_ref[...] = jnp.dot(hx.astype(BF16), w_ref[...], preferred_element_type=F32)


def _ctx_in(c2, mod, g, w_u, ctx_row):
    n, d = c2.shape
    c = w_u.shape[1]
    return pl.pallas_call(
        functools.partial(_ctx_in_kernel, ctx_row=ctx_row),
        grid=(n // TOKEN_TILE,),
        in_specs=[pl.BlockSpec((TOKEN_TILE, d), lambda i: (i, 0)),
                  pl.BlockSpec(mod.shape, lambda i: (0, 0)),
                  pl.BlockSpec((1, d), lambda i: (0, 0)),
                  pl.BlockSpec(w_u.shape, lambda i: (0, 0))],
        out_specs=pl.BlockSpec((TOKEN_TILE, c), lambda i: (i, 0)),
        out_shape=jax.ShapeDtypeStruct((n, c), F32),
        compiler_params=_params(("arbitrary",)),
        name="ctx_in",
    )(c2, mod, g, w_u)


def _log_sigmoid(x):
    return jnp.minimum(x, 0.0) - jnp.log1p(jnp.exp(-jnp.abs(x)))


def _lru_scan_kernel(*refs, n_tiles, reverse, conv_done, reset_first, emit_y):
    if conv_done:
        u_ref, wri_ref, rb_ref, ib_ref, lam_ref, h0_ref, out_ref, a_scr, b_scr, h_scr = refs
    else:
        (u_ref, up_ref, un_ref, cw_ref, cb_ref, wri_ref, rb_ref, ib_ref, lam_ref, h0_ref,
         out_ref, uc_ref, ubuf, a_scr, b_scr, h_scr) = refs
    t_rows, c = u_ref.shape
    hb = c // RNN_HEADS
    b = pl.program_id(0)
    j = pl.program_id(1)
    jj = n_tiles - 1 - j if reverse else j

    if conv_done:
        u = u_ref[...]
    else:
        ubuf[SUBLANES:SUBLANES + t_rows, :] = u_ref[...]
        ubuf[0:SUBLANES, :] = jnp.where(jj == 0, 0.0, up_ref[...])
        ubuf[SUBLANES + t_rows:, :] = jnp.where(jj == n_tiles - 1, 0.0, un_ref[...])
        u = cb_ref[...]
        for k in range(CONV_WIDTH):
            u = u + cw_ref[k:k + 1, :] * ubuf[pl.ds(SUBLANES - CONV_PAD_LEFT + k, t_rows), :]
        uc_ref[...] = u

    log_lam = LRU_C * _log_sigmoid(lam_ref[...])
    rows = lax.broadcasted_iota(jnp.int32, (t_rows, 1), 0)
    first_row = jnp.where(j == 0, t_rows - 1 if reverse else 0, -1)
    for h in range(RNN_HEADS):
        sl = slice(h * hb, (h + 1) * hb)
        uh = u[:, sl]
        z = jnp.dot(uh.astype(BF16), wri_ref[h], preferred_element_type=F32)
        r = jax.nn.sigmoid(z[:, :hb] + rb_ref[:, sl])
        ig = jax.nn.sigmoid(z[:, hb:] + ib_ref[:, sl])
        log_a = r * log_lam[:, sl]
        a = jnp.exp(log_a)
        mult = jnp.sqrt((1.0 - a) * (1.0 + a))
        if reset_first:
            mult = jnp.where(rows == first_row, 1.0, mult)
        a_scr[:, sl] = a
        b_scr[:, sl] = mult * ig * uh

    @pl.when(j == 0)
    def _():
        h_scr[...] = h0_ref[pl.ds(b, 1), :]

    n_groups = t_rows // SUBLANES

    def group(g, h):
        base = pl.multiple_of((n_groups - 1 - g if reverse else g) * SUBLANES, SUBLANES)
        for s in range(SUBLANES):
            r = base + (SUBLANES - 1 - s if reverse else s)
            h = a_scr[pl.ds(r, 1), :] * h + b_scr[pl.ds(r, 1), :]
            if emit_y:
                out_ref[pl.ds(r, 1), :] = h
        return h

    h = lax.fori_loop(0, n_groups, group, h_scr[...])
    h_scr[...] = h
    if not emit_y:
        @pl.when(j == n_tiles - 1)
        def _():
            out_ref[pl.ds(b, 1), :] = h


def _lru_scan(u, conv, w_ri, r_b, i_b, lam, h0, *, reverse, reset_first, emit_y):
    n, c = u.shape
    n_batch = h0.shape[0]
    n_tiles = n // n_batch // TOKEN_TILE
    sub = TOKEN_TILE // SUBLANES
    n_sub = n // SUBLANES
    const = lambda b, j: (0, 0)

    def tile(b, j):
        return b * n_tiles + (n_tiles - 1 - j if reverse else j)

    tile_spec = pl.BlockSpec((TOKEN_TILE, c), lambda b, j: (tile(b, j), 0))
    in_specs = [tile_spec]
    args = [u]
    scratch = []
    if conv is not None:
        conv_w, conv_b = conv
        in_specs += [pl.BlockSpec((SUBLANES, c), lambda b, j: (jnp.maximum(tile(b, j) * sub - 1, 0), 0)),
                     pl.BlockSpec((SUBLANES, c), lambda b, j: (jnp.minimum((tile(b, j) + 1) * sub, n_sub - 1), 0)),
                     pl.BlockSpec(conv_w.shape, const),
                     pl.BlockSpec((1, c), const)]
        args += [u, u, conv_w, conv_b.reshape(1, c)]
        scratch = [pltpu.VMEM((TOKEN_TILE + 2 * SUBLANES, c), F32)]
    in_specs += [pl.BlockSpec(w_ri.shape, lambda b, j: (0, 0, 0)),
                 pl.BlockSpec((1, c), const), pl.BlockSpec((1, c), const), pl.BlockSpec((1, c), const),
                 pl.BlockSpec(h0.shape, const)]
    args += [w_ri, r_b.reshape(1, c), i_b.reshape(1, c), lam.reshape(1, c), h0]
    if emit_y:
        out_specs = [tile_spec]
        out_shape = [jax.ShapeDtypeStruct((n, c), F32)]
    else:
        out_specs = [pl.BlockSpec(h0.shape, const)]
        out_shape = [jax.ShapeDtypeStruct(h0.shape, F32)]
    if conv is not None:
        out_specs.append(tile_spec)
        out_shape.append(jax.ShapeDtypeStruct((n, c), F32))
    name = ("lru_scan" if emit_y else "lru_ctx") + ("_rev" if reverse else "_fwd")
    outs = pl.pallas_call(
        functools.partial(_lru_scan_kernel, n_tiles=n_tiles, reverse=reverse, conv_done=conv is None,
                          reset_first=reset_first, emit_y=emit_y),
        grid=(n_batch, n_tiles),
        in_specs=in_specs,
        out_specs=out_specs,
        out_shape=out_shape,
        scratch_shapes=scratch + [pltpu.VMEM((TOKEN_TILE, c), F32),
                                  pltpu.VMEM((TOKEN_TILE, c), F32),
                                  pltpu.VMEM((1, c), F32)],
        compiler_params=_params(("arbitrary", "arbitrary")),
        name=name,
    )(*args)
    return outs if conv is not None else outs[0]


def _route(logits, rb):
    e, t = logits.shape
    neg = -jnp.inf
    scores = jax.nn.sigmoid(logits)
    sel = scores + rb
    iota_g = lax.broadcasted_iota(jnp.int32, (N_GROUPS, t), 0).astype(F32)
    iota_e = lax.broadcasted_iota(jnp.int32, (e, t), 0).astype(F32)

    gs = jnp.full((N_GROUPS, t), neg, F32)
    for g in range(N_GROUPS):
        sg = sel[g * EXPERTS_PER_GROUP:(g + 1) * EXPERTS_PER_GROUP, :]
        m1 = jnp.max(sg, axis=0, keepdims=True)
        i1 = jnp.min(jnp.where(sg == m1, iota_g, float(EXPERTS_PER_GROUP)), axis=0, keepdims=True)
        m2 = jnp.max(jnp.where(iota_g == i1, neg, sg), axis=0, keepdims=True)
        gs = jnp.where(iota_g == float(g), m1 + m2, gs)

    keep = jnp.zeros((N_GROUPS, t), F32)
    for _ in range(TOPK_GROUPS):
        m = jnp.max(gs, axis=0, keepdims=True)
        idx = jnp.min(jnp.where(gs == m, iota_g, float(N_GROUPS)), axis=0, keepdims=True)
        hit = iota_g == idx
        keep = jnp.where(hit, 1.0, keep)
        gs = jnp.where(hit, neg, gs)

    masked = jnp.concatenate(
        [jnp.where(keep[g:g + 1, :] > 0.0, sel[g * EXPERTS_PER_GROUP:(g + 1) * EXPERTS_PER_GROUP, :], neg)
         for g in range(N_GROUPS)], axis=0)

    iota_k = lax.broadcasted_iota(jnp.int32, (TOP_K, t), 0)
    selmask = jnp.zeros((e, t), F32)
    eidx = jnp.zeros((TOP_K, t), F32)
    gw = jnp.zeros((TOP_K, t), F32)
    for k in range(TOP_K):
        m = jnp.max(masked, axis=0, keepdims=True)
        idx = jnp.min(jnp.where(masked == m, iota_e, float(e)), axis=0, keepdims=True)
        hit = iota_e == idx
        gk = jnp.sum(jnp.where(hit, scores, 0.0), axis=0, keepdims=True)
        masked = jnp.where(hit, neg, masked)
        selmask = jnp.where(hit, 1.0, selmask)
        eidx = jnp.where(iota_k == k, idx, eidx)
        gw = jnp.where(iota_k == k, gk, gw)
    gw = gw / jnp.sum(gw, axis=0, keepdims=True) * ROUTED_SCALE
    return eidx, gw, selmask


def _ffn_pre(x1, mod_ref, row, g2_ref, rwt_ref, rb_ref, hx2_ref, eidx_ref, gw_ref, mask_ref, cnt_ref):
    d = x1.shape[1]
    hx2 = _rms_mod(x1, g2_ref[...], _mod_chunk(mod_ref, row, 4, d), _mod_chunk(mod_ref, row, 3, d))
    hx2_ref[...] = hx2
    logits = lax.dot_general(rwt_ref[...], hx2, (((1,), (1,)), ((), ())),
                             preferred_element_type=F32, precision=HIGHEST)
    eidx, gw, selmask = _route(logits, rb_ref[...])
    eidx_ref[...] = eidx.astype(jnp.int32)
    gw_ref[...] = gw
    mask_ref[...] = selmask

    @pl.when(pl.program_id(0) == 0)
    def _():
        cnt_ref[...] = jnp.zeros_like(cnt_ref)

    rows = jnp.zeros((selmask.shape[0], 1), F32)
    for t0 in range(0, selmask.shape[1], TOKEN_TILE):
        rows = rows + _round_up_rows(jnp.sum(selmask[:, t0:t0 + TOKEN_TILE], axis=1, keepdims=True))
    cnt_ref[...] += jnp.broadcast_to(rows, cnt_ref.shape)


def _ffn_pre_specs(n, d):
    row = lambda i: (i, 0)
    col = lambda i: (0, i)
    out_specs = [pl.BlockSpec((MIX_TILE, d), row),
                 pl.BlockSpec((MIX_TILE, d), row),
                 pl.BlockSpec((TOP_K, MIX_TILE), col),
                 pl.BlockSpec((TOP_K, MIX_TILE), col),
                 pl.BlockSpec((N_EXPERTS, MIX_TILE), col),
                 pl.BlockSpec((N_EXPERTS, LANES), lambda i: (0, 0))]
    out_shape = [jax.ShapeDtypeStruct((n, d), F32),
                 jax.ShapeDtypeStruct((n, d), F32),
                 jax.ShapeDtypeStruct((TOP_K, n), jnp.int32),
                 jax.ShapeDtypeStruct((TOP_K, n), F32),
                 jax.ShapeDtypeStruct((N_EXPERTS, n), F32),
                 jax.ShapeDtypeStruct((N_EXPERTS, 128), F32)]
    return out_specs, out_shape


def _rglru_out_kernel(yf_ref, yr_ref, gate_ref, x0_ref, mod_ref, wout_ref, g2_ref, rwt_ref, rb_ref,
                      x1_ref, hx2_ref, eidx_ref, gw_ref, mask_ref, cnt_ref, *, tiles_per_seq):
    d = x0_ref.shape[1]
    row = pl.program_id(0) // tiles_per_seq
    yx = yf_ref[...] + yr_ref[...]
    v = gate_ref[...].astype(F32) * yx
    out = jnp.dot(v.astype(BF16), wout_ref[...], preferred_element_type=F32)
    x1 = x0_ref[...] + _mod_chunk(mod_ref, row, 2, d) * out
    x1_ref[...] = x1
    _ffn_pre(x1, mod_ref, row, g2_ref, rwt_ref, rb_ref, hx2_ref, eidx_ref, gw_ref, mask_ref, cnt_ref)


def _rglru_out(y_fwd, y_rev, gate, x0, mod, w_out, g2, rwt, rb, tiles_per_seq):
    n, d = x0.shape
    c = gate.shape[1]
    row = lambda i: (i, 0)
    const = lambda i: (0, 0)
    out_specs, out_shape = _ffn_pre_specs(n, d)
    return pl.pallas_call(
        functools.partial(_rglru_out_kernel, tiles_per_seq=tiles_per_seq),
        grid=(n // MIX_TILE,),
        in_specs=[pl.BlockSpec((MIX_TILE, c), row),
                  pl.BlockSpec((MIX_TILE, c), row),
                  pl.BlockSpec((MIX_TILE, c), row),
                  pl.BlockSpec((MIX_TILE, d), row),
                  pl.BlockSpec(mod.shape, const),
                  pl.BlockSpec(w_out.shape, const),
                  pl.BlockSpec((1, d), const),
                  pl.BlockSpec(rwt.shape, const),
                  pl.BlockSpec(rb.shape, const)],
        out_specs=out_specs,
        out_shape=out_shape,
        compiler_params=_params(("arbitrary",)),
        name="rglru_out",
    )(y_fwd, y_rev, gate, x0, mod, w_out, g2, rwt, rb)


def _sgu_kernel(x_ref, mod_ref, g_ref, win_ref, lng_ref, lnb_ref, ws_ref, bst_ref, wout_ref,
                g2_ref, rwt_ref, rb_ref,
                x1_ref, hx2_ref, eidx_ref, gw_ref, mask_ref, cnt_ref, m_scr, *, tiles_per_seq):
    t_rows, d = x_ref.shape
    w = wout_ref.shape[0]
    gd = w // SGU_HEADS
    row = pl.program_id(0) // tiles_per_seq
    x = x_ref[...]
    hx = _rms_mod(x, g_ref[...], _mod_chunk(mod_ref, row, 1, d), _mod_chunk(mod_ref, row, 0, d))
    z = jax.nn.gelu(jnp.dot(hx.astype(BF16), win_ref[...], preferred_element_type=F32))
    u = z[:, :w]
    v = z[:, w:]
    mu = jnp.mean(v, axis=-1, keepdims=True)
    vc = v - mu
    v = vc * lax.rsqrt(jnp.mean(vc * vc, axis=-1, keepdims=True) + NORM_EPS) * lng_ref[...] + lnb_ref[...]
    vb = v.astype(BF16)
    for ch in range(t_rows // CHUNK):
        rs = slice(ch * CHUNK, (ch + 1) * CHUNK)
        for g in range(SGU_HEADS):
            cs = slice(g * gd, (g + 1) * gd)
            sv = jnp.dot(ws_ref[g], vb[rs, cs], preferred_element_type=F32) + bst_ref[:, g:g + 1]
            m_scr[rs, cs] = (u[rs, cs] * sv).astype(BF16)
    out = jnp.dot(m_scr[...], wout_ref[...], preferred_element_type=F32)
    x1 = x + _mod_chunk(mod_ref, row, 2, d) * out
    x1_ref[...] = x1
    _ffn_pre(x1, mod_ref, row, g2_ref, rwt_ref, rb_ref, hx2_ref, eidx_ref, gw_ref, mask_ref, cnt_ref)


def _sgu(x, mod, g, w_in, ln_g, ln_b, w_s, b_st, w_out, g2, rwt, rb, tiles_per_seq):
    n, d = x.shape
    w = w_out.shape[0]
    const = lambda i: (0, 0)
    out_specs, out_shape = _ffn_pre_specs(n, d)
    return pl.pallas_call(
        functools.partial(_sgu_kernel, tiles_per_seq=tiles_per_seq),
        grid=(n // MIX_TILE,),
        in_specs=[pl.BlockSpec((MIX_TILE, d), lambda i: (i, 0)),
                  pl.BlockSpec(mod.shape, const),
                  pl.BlockSpec((1, d), const),
                  pl.BlockSpec(w_in.shape, const, pipeline_mode=pl.Buffered(1)),
                  pl.BlockSpec((1, w), const),
                  pl.BlockSpec((1, w), const),
                  pl.BlockSpec(w_s.shape, lambda i: (0, 0, 0)),
                  pl.BlockSpec(b_st.shape, const),
                  pl.BlockSpec(w_out.shape, const, pipeline_mode=pl.Buffered(1)),
                  pl.BlockSpec((1, d), const),
                  pl.BlockSpec(rwt.shape, const),
                  pl.BlockSpec(rb.shape, const)],
        out_specs=out_specs,
        out_shape=out_shape,
        scratch_shapes=[pltpu.VMEM((MIX_TILE, w), BF16)],
        compiler_params=_params(("arbitrary",)),
        name="sgu",
    )(x, mod, g, w_in, ln_g, ln_b, w_s, b_st, w_out, g2, rwt, rb)


def _rank_kernel(mask_ref, eidx_ref, gw_ref, pstart_ref, hi_ref, lo_ref, gate_ref, tab_ref, tabt_ref, carry):
    e, t = mask_ref.shape

    @pl.when(pl.program_id(0) == 0)
    def _():
        carry[...] = jnp.zeros_like(carry)

    m = mask_ref[...]
    r = lax.broadcasted_iota(jnp.int32, (t, t), 0)
    c = lax.broadcasted_iota(jnp.int32, (t, t), 1)
    upper = jnp.where(r <= c, 1.0, 0.0).astype(BF16)
    incl = jnp.dot(m.astype(BF16), upper, preferred_element_type=F32)
    run = _round_up_rows(incl[:, t - 1:t])
    lanes = tab_ref.shape[2]
    re = lax.broadcasted_iota(jnp.int32, (e, e), 0)
    ce = lax.broadcasted_iota(jnp.int32, (e, e), 1)
    lower = jnp.where(ce < re, 1.0, 0.0).astype(BF16)
    tiles = jnp.broadcast_to(run * (1.0 / ROW_ALIGN), (e, lanes)).astype(BF16)
    lstart = jnp.dot(lower, tiles, preferred_element_type=F32)[:, 0:1] * float(ROW_ALIGN)
    pos = jnp.where(m > 0.0, lstart + incl - m, float(POS_RADIX * POS_RADIX - 1))
    hi = jnp.floor(pos * (1.0 / POS_RADIX))
    hi_ref[...] = hi.astype(BF16)
    lo_ref[...] = (pos - hi * float(POS_RADIX)).astype(BF16)
    iota_e = lax.broadcasted_iota(jnp.int32, (e, t), 0)
    eidx = eidx_ref[...]
    gw = gw_ref[...]
    gate = jnp.zeros((e, t), F32)
    for k in range(TOP_K):
        gate = jnp.where(iota_e == eidx[k:k + 1, :], gw[k:k + 1, :], gate)
    gate_ref[...] = gate.astype(BF16)
    lane = lax.broadcasted_iota(jnp.int32, (e, lanes), 1)
    tab_ref[0] = jnp.where(lane == 0, run, jnp.where(lane == 1, pstart_ref[...] + carry[...],
                                                     jnp.where(lane == 2, lstart, 0.0)))
    diag = lax.broadcasted_iota(jnp.int32, (e, lanes), 0) == lane
    row_of = lambda col: jnp.sum(jnp.where(diag, jnp.broadcast_to(col, (e, lanes)), 0.0), axis=0, keepdims=True)
    sub = lax.broadcasted_iota(jnp.int32, (SUBLANES, lanes), 0)
    tabt_ref[0] = jnp.where(sub == 0, row_of(lstart), jnp.where(sub == 1, row_of(lstart + run), 0.0))
    carry[...] = carry[...] + run


def _rank(mask_t, eidx_t, gw_t, pstart):
    e, n = mask_t.shape
    n_tiles = n // TOKEN_TILE
    col = lambda i: (0, i)
    return pl.pallas_call(
        _rank_kernel,
        grid=(n_tiles,),
        in_specs=[pl.BlockSpec((e, TOKEN_TILE), col),
                  pl.BlockSpec((TOP_K, TOKEN_TILE), col),
                  pl.BlockSpec((TOP_K, TOKEN_TILE), col),
                  pl.BlockSpec((e, 1), lambda i: (0, 0))],
        out_specs=[pl.BlockSpec((e, TOKEN_TILE), col),
                   pl.BlockSpec((e, TOKEN_TILE), col),
                   pl.BlockSpec((e, TOKEN_TILE), col),
                   pl.BlockSpec((1, e, LANES), lambda i: (i, 0, 0)),
                   pl.BlockSpec((1, SUBLANES, LANES), lambda i: (i, 0, 0))],
        out_shape=[jax.ShapeDtypeStruct((e, n), BF16),
                   jax.ShapeDtypeStruct((e, n), BF16),
                   jax.ShapeDtypeStruct((e, n), BF16),
                   jax.ShapeDtypeStruct((n_tiles, e, LANES), F32),
                   jax.ShapeDtypeStruct((n_tiles, SUBLANES, LANES), F32)],
        scratch_shapes=[pltpu.VMEM((e, 1), F32)],
        compiler_params=_params(("arbitrary",)),
        name="moe_rank",
    )(mask_t, eidx_t, gw_t, pstart)


def _aligned(v):
    return pl.multiple_of(v, ROW_ALIGN)


def _run_copies(tab_ref, make_copy, unroll):
    def body(pair, carry):
        for priority in range(2):
            e = 2 * pair + priority
            make_copy(_aligned(tab_ref[TAB_LOCAL + e]), _aligned(tab_ref[N_EXPERTS + e]),
                      _aligned(tab_ref[e])).start(priority=priority)
        return carry

    lax.fori_loop(0, N_EXPERTS // 2, body, 0, unroll=unroll)


def _dispatch_kernel(tab_ref, ztab_ref, tabt_ref, hi_ref, lo_ref, x_ref, xs_hbm, sbuf, zbuf, sems, zsem):
    i = pl.program_id(0)
    slot = i % 2
    t = x_ref.shape[0]
    rows = sbuf.shape[1]

    def zero_copy(e):
        n = _aligned(ztab_ref[e])
        return pltpu.make_async_copy(zbuf.at[pl.ds(0, n)], xs_hbm.at[pl.ds(_aligned(ztab_ref[N_EXPERTS + e]), n)], zsem)

    def for_zero_runs(fn):
        def body(e, c):
            @pl.when(ztab_ref[e] > 0)
            def _():
                fn(zero_copy(e))
            return c
        lax.fori_loop(0, N_EXPERTS, body, 0)

        def tail(b, c):
            fn(pltpu.make_async_copy(zbuf, xs_hbm.at[pl.ds(pl.multiple_of(b * EXPERT_BLOCK, EXPERT_BLOCK),
                                                           EXPERT_BLOCK)], zsem))
            return c
        lax.fori_loop(ztab_ref[2 * N_EXPERTS], xs_hbm.shape[0] // EXPERT_BLOCK, tail, 0)

    @pl.when(i == 0)
    def _():
        zbuf[...] = jnp.zeros_like(zbuf)
        for_zero_runs(lambda cp: cp.start())

    xb = x_ref[...].astype(BF16)
    total = tab_ref[2 * N_EXPERTS]
    run_lo = tabt_ref[0, 0:1, 0:N_EXPERTS]
    run_hi = tabt_ref[0, 1:2, 0:N_EXPERTS]

    def permute(r0):
        j_e = (lax.broadcasted_iota(jnp.int32, (PERM_CHUNK, N_EXPERTS), 0) + r0).astype(F32)
        owner = jnp.where((j_e >= run_lo) & (j_e < run_hi), 1.0, 0.0).astype(BF16)
        pos = (jnp.dot(owner, hi_ref[...], preferred_element_type=F32) * float(POS_RADIX)
               + jnp.dot(owner, lo_ref[...], preferred_element_type=F32))
        j_t = (lax.broadcasted_iota(jnp.int32, (PERM_CHUNK, t), 0) + r0).astype(F32)
        p = jnp.where(pos == j_t, 1.0, 0.0).astype(BF16)
        sbuf[slot, r0:r0 + PERM_CHUNK, :] = jnp.dot(p, xb, preferred_element_type=F32).astype(BF16)

    for r0 in range(0, rows, PERM_CHUNK):
        if r0 < t * TOP_K + PERM_CHUNK:
            permute(r0)
        else:
            pl.when(total > r0)(functools.partial(permute, r0))

    _run_copies(tab_ref, lambda loc, glob, n: pltpu.make_async_copy(
        sbuf.at[slot, pl.ds(loc, n)], xs_hbm.at[pl.ds(glob, n)], sems.at[slot]), unroll=True)

    def wait_rows(s, n):
        pltpu.make_async_copy(sbuf.at[s, pl.ds(0, n)], xs_hbm.at[pl.ds(0, n)], sems.at[s]).wait()

    @pl.when(i > 0)
    def _():
        wait_rows(1 - slot, _aligned(tab_ref[2 * N_EXPERTS + 1]))

    @pl.when(i == pl.num_programs(0) - 1)
    def _():
        wait_rows(slot, _aligned(tab_ref[2 * N_EXPERTS]))

    @pl.when(i == 0)
    def _():
        for_zero_runs(lambda cp: cp.wait())


def _dispatch(tab, ztab, tab_t, pos_hi, pos_lo, hx2, xs_rows):
    n, d = hx2.shape
    col = lambda i: (0, i)
    return pl.pallas_call(
        _dispatch_kernel,
        grid=(n // TOKEN_TILE,),
        in_specs=[pl.BlockSpec((TAB_WIDTH,), lambda i: (i,), memory_space=pltpu.SMEM),
                  pl.BlockSpec(memory_space=pltpu.SMEM),
                  pl.BlockSpec((1, SUBLANES, LANES), lambda i: (i, 0, 0)),
                  pl.BlockSpec((N_EXPERTS, TOKEN_TILE), col),
                  pl.BlockSpec((N_EXPERTS, TOKEN_TILE), col),
                  pl.BlockSpec((TOKEN_TILE, d), lambda i: (i, 0))],
        out_specs=pl.BlockSpec(memory_space=pl.ANY),
        out_shape=jax.ShapeDtypeStruct((xs_rows, d), BF16),
        scratch_shapes=[pltpu.VMEM((2, SORTED_ROWS, d), BF16),
                        pltpu.VMEM((EXPERT_BLOCK, d), BF16),
                        pltpu.SemaphoreType.DMA((2,)),
                        pltpu.SemaphoreType.DMA],
        compiler_params=_params(("arbitrary",)),
        name="moe_dispatch",
    )(tab, ztab, tab_t, pos_hi, pos_lo, hx2)


def _experts_kernel(be_ref, nb_ref, xs_ref, wgu_ref, wd_ref, ys_ref, wgu_b, wd_b):
    i = pl.program_id(0)
    ff = wd_b.shape[0]
    used = i < nb_ref[0]
    new_expert = (i == 0) | (be_ref[i] != be_ref[jnp.maximum(i - 1, 0)])

    @pl.when(used & new_expert)
    def _():
        wgu_b[...] = wgu_ref[0, 0].astype(BF16)
        wd_b[...] = wd_ref[0, 0].astype(BF16)

    @pl.when(used)
    def _():
        starts = range(0, xs_ref.shape[0], MIX_SUB)
        hs = [jnp.dot(xs_ref[r0:r0 + MIX_SUB, :], wgu_b[...], preferred_element_type=F32) for r0 in starts]
        for r0, h in zip(starts, hs):
            a = _silu(h[:, :ff]) * h[:, ff:]
            ys_ref[r0:r0 + MIX_SUB, :] = jnp.dot(a.astype(BF16), wd_b[...],
                                                 preferred_element_type=F32).astype(BF16)

    @pl.when(jnp.logical_not(used))
    def _():
        ys_ref[...] = jnp.zeros_like(ys_ref)


def _experts(block_e, n_used, xs, w_gu, w_down, layer):
    rows, d = xs.shape
    nb = rows // EXPERT_BLOCK
    blk = lambda i, be, nu: (jnp.maximum(jnp.minimum(i, nu[0] - 1), 0), 0)
    wmap = lambda i, be, nu: (layer, be[i], 0, 0)
    return pl.pallas_call(
        _experts_kernel,
        grid_spec=pltpu.PrefetchScalarGridSpec(
            num_scalar_prefetch=2,
            grid=(nb,),
            in_specs=[pl.BlockSpec((EXPERT_BLOCK, d), blk),
                      pl.BlockSpec((1, 1) + w_gu.shape[2:], wmap),
                      pl.BlockSpec((1, 1) + w_down.shape[2:], wmap)],
            out_specs=pl.BlockSpec((EXPERT_BLOCK, d), lambda i, be, nu: (i, 0)),
            scratch_shapes=[pltpu.VMEM(w_gu.shape[2:], BF16), pltpu.VMEM(w_down.shape[2:], BF16)]),
        out_shape=jax.ShapeDtypeStruct((rows, d), BF16),
        compiler_params=_params(("arbitrary",)),
        name="moe_experts",
    )(block_e, n_used, xs, w_gu, w_down)


def _combine_kernel(tab_ref, tabv_ref, hi_ref, lo_ref, gate_ref, ys_hbm, hx2_ref, x1_ref, mod_ref, wsgu_ref,
                    wsd_ref, fg_ref, out_ref, ybuf, sem, *, tiles_per_seq, final_norm):
    t_rows, d = x1_ref.shape
    ff = wsd_ref.shape[0]
    rows = ybuf.shape[0]
    row = pl.program_id(0) // tiles_per_seq

    @pl.when(pl.program_id(0) == 0)
    def _():
        ybuf[...] = jnp.zeros_like(ybuf)

    _run_copies(tab_ref, lambda loc, glob, n: pltpu.make_async_copy(
        ys_hbm.at[pl.ds(glob, n)], ybuf.at[pl.ds(loc, n)], sem), unroll=False)

    hs = jnp.dot(hx2_ref[...].astype(BF16), wsgu_ref[...], preferred_element_type=F32)
    shared = jnp.dot((_silu(hs[:, :ff]) * hs[:, ff:]).astype(BF16), wsd_ref[...], preferred_element_type=F32)

    run_lo = tabv_ref[0, :, 2:3]
    run_hi = run_lo + tabv_ref[0, :, 0:1]
    j_e = lax.broadcasted_iota(jnp.int32, (N_EXPERTS, rows), 1).astype(F32)
    owner = jnp.where((j_e >= run_lo) & (j_e < run_hi), 1.0, 0.0).astype(BF16)
    pos = (jnp.dot(hi_ref[...], owner, preferred_element_type=F32) * float(POS_RADIX)
           + jnp.dot(lo_ref[...], owner, preferred_element_type=F32))
    gates = jnp.dot(gate_ref[...], owner, preferred_element_type=F32)
    j_t = lax.broadcasted_iota(jnp.int32, (t_rows, rows), 1).astype(F32)
    gb = jnp.where(pos == j_t, gates, 0.0).astype(BF16)

    total = _aligned(tab_ref[2 * N_EXPERTS])
    pltpu.make_async_copy(ys_hbm.at[pl.ds(0, total)], ybuf.at[pl.ds(0, total)], sem).wait()
    routed = jnp.dot(gb, ybuf[...], preferred_element_type=F32)
    x2 = x1_ref[...] + _mod_chunk(mod_ref, row, 5, d) * (routed + shared)
    if final_norm:
        x2 = x2 * lax.rsqrt(jnp.mean(x2 * x2, axis=-1, keepdims=True) + NORM_EPS) * fg_ref[...]
    out_ref[...] = x2


def _combine(tab, tab_v, pos_hi, pos_lo, gate, ys, hx2, x1, mod, ws_gu, ws_down, fg, tiles_per_seq, final_norm):
    n, d = x1.shape
    row = lambda i: (i, 0)
    const = lambda i: (0, 0)
    return pl.pallas_call(
        functools.partial(_combine_kernel, tiles_per_seq=tiles_per_seq, final_norm=final_norm),
        grid=(n // TOKEN_TILE,),
        in_specs=[pl.BlockSpec((TAB_WIDTH,), lambda i: (i,), memory_space=pltpu.SMEM),
                  pl.BlockSpec((1, N_EXPERTS, LANES), lambda i: (i, 0, 0)),
                  pl.BlockSpec((TOKEN_TILE, N_EXPERTS), row),
                  pl.BlockSpec((TOKEN_TILE, N_EXPERTS), row),
                  pl.BlockSpec((TOKEN_TILE, N_EXPERTS), row),
                  pl.BlockSpec(memory_space=pl.ANY),
                  pl.BlockSpec((TOKEN_TILE, d), row),
                  pl.BlockSpec((TOKEN_TILE, d), row),
                  pl.BlockSpec(mod.shape, const),
                  pl.BlockSpec(ws_gu.shape, const),
                  pl.BlockSpec(ws_down.shape, const),
                  pl.BlockSpec((1, d), const)],
        out_specs=pl.BlockSpec((TOKEN_TILE, d), row),
        out_shape=jax.ShapeDtypeStruct((n, d), F32),
        scratch_shapes=[pltpu.VMEM((SORTED_ROWS, d), BF16), pltpu.SemaphoreType.DMA],
        compiler_params=_params(("arbitrary",)),
        name="moe_combine",
    )(tab, tab_v, pos_hi, pos_lo, gate, ys, hx2, x1, mod, ws_gu, ws_down, fg)


def _moe(x1, hx2, eidx_t, gw_t, mask_t, cnt, mod, w_gu, w_down, layer, ws_gu, ws_down, fg, tiles_per_seq,
         final_norm):
    n, d = x1.shape
    n_tiles = n // TOKEN_TILE
    counts = cnt[:, 0].astype(jnp.int32)
    padded = (counts + EXPERT_BLOCK - 1) // EXPERT_BLOCK * EXPERT_BLOCK
    pad_end = jnp.cumsum(padded)
    pad_start = pad_end - padded
    max_rows = n * TOP_K + n_tiles * N_EXPERTS * ROW_ALIGN + N_EXPERTS * (EXPERT_BLOCK - ROW_ALIGN)
    n_blocks = (max_rows + EXPERT_BLOCK - 1) // EXPERT_BLOCK
    n_used = (pad_end[-1:] // EXPERT_BLOCK).astype(jnp.int32)
    block_start = jnp.arange(n_blocks, dtype=jnp.int32) * EXPERT_BLOCK
    block_e = jnp.minimum(jnp.sum((pad_end[None, :] <= block_start[:, None]).astype(jnp.int32), axis=1),
                          N_EXPERTS - 1)

    pos_hi, pos_lo, gate, tab_f, tab_t = _rank(mask_t, eidx_t, gw_t, pad_start.astype(F32).reshape(N_EXPERTS, 1))
    run_len = tab_f[:, :, 0].astype(jnp.int32)
    run_start = tab_f[:, :, 1].astype(jnp.int32)
    total = jnp.sum(run_len, axis=1, keepdims=True)
    prev_total = jnp.concatenate([jnp.zeros((1, 1), jnp.int32), total[:-1]], axis=0)
    run_local = tab_f[:, :, 2].astype(jnp.int32)
    fill = jnp.zeros((n_tiles, TAB_WIDTH - 3 * N_EXPERTS - 2), jnp.int32)
    tab = jnp.concatenate([run_len, run_start, total, prev_total, run_local, fill], axis=1).reshape(-1)
    ztab = jnp.concatenate([padded - counts, pad_start + counts, n_used])

    xs = _dispatch(tab, ztab, tab_t, pos_hi, pos_lo, hx2, n_blocks * EXPERT_BLOCK)
    ys = _experts(block_e, n_used, xs, w_gu, w_down, layer)
    return _combine(tab, tab_f, pos_hi.T, pos_lo.T, gate.T, ys, hx2, x1, mod, ws_gu, ws_down, fg,
                    tiles_per_seq, final_norm)


def _sincos_2d(rows, d):
    quarter = d // 4
    omega = 1.0 / (POS_BASE ** (jnp.arange(quarter, dtype=F32) / quarter))

    def emb(n):
        p = jnp.arange(n, dtype=F32)[:, None] * omega[None, :]
        return jnp.concatenate([jnp.sin(p), jnp.cos(p)], axis=-1)

    er, ec = emb(rows), emb(GRID_W)
    pe = jnp.concatenate([jnp.broadcast_to(er[:, None, :], (rows, GRID_W, d // 2)),
                          jnp.broadcast_to(ec[None, :, :], (rows, GRID_W, d // 2))], axis=-1)
    return pe.reshape(rows * GRID_W, d)


def kernel(x, c, ctx, c_ctx, ada_w, ada_b, mix_norm_g, ffn_norm_g, a_w_in, a_conv_w, a_conv_b, a_gate_r_w, a_gate_r_b, a_gate_i_w, a_gate_i_b, a_lambda, a_w_out, b_w_in, b_ln_g, b_ln_b, b_w_s, b_b_s, b_w_out, router_w, router_b, moe_w_gu, moe_w_down, shared_w_gu, shared_w_down, final_norm_g):
    bsz, s, d = x.shape
    ctx_len = ctx.shape[1]
    depth = ada_w.shape[0]
    assert depth == 2 and bsz < MOD_ROWS and s % MIX_TILE == 0 and MIX_TILE % TOKEN_TILE == 0
    assert ctx_len % TOKEN_TILE == 0
    n = bsz * s
    tps = s // TOKEN_TILE
    ctx_row = bsz

    cc = jnp.zeros((MOD_ROWS, d), F32).at[:bsz].set(c).at[ctx_row].set(c_ctx)
    mod = _modulation(cc, ada_w, ada_b)
    pe = _sincos_2d(s // GRID_W, d)
    rc = a_w_in.shape[2] // 2

    w_in0 = a_w_in[0].astype(BF16)
    g_mix0 = mix_norm_g[0].reshape(1, d)
    x0, gate, ux = _rglru_in(x.reshape(n, d), pe, mod[0], g_mix0, w_in0)
    uc = _ctx_in(ctx.reshape(bsz * ctx_len, d), mod[0], g_mix0, w_in0[:, rc:], ctx_row)
    w_ri = jnp.concatenate([a_gate_r_w[0], a_gate_i_w[0]], axis=-1).astype(BF16)
    conv = (a_conv_w[0], a_conv_b[0])
    gates = [(w_ri[k], a_gate_r_b[0, k], a_gate_i_b[0, k], a_lambda[0, k]) for k in range(2)]
    h_zero = jnp.zeros((bsz, rc), F32)
    h_fwd, uc = _lru_scan(uc, conv, *gates[0], h_zero, reverse=False, reset_first=True, emit_y=False)
    h_rev = _lru_scan(uc, None, *gates[1], h_zero, reverse=True, reset_first=True, emit_y=False)
    y_fwd, ux = _lru_scan(ux, conv, *gates[0], h_fwd, reverse=False, reset_first=False, emit_y=True)
    y_rev = _lru_scan(ux, None, *gates[1], h_rev, reverse=True, reset_first=False, emit_y=True)
    pre = _rglru_out(y_fwd, y_rev, gate, x0, mod[0], a_w_out[0].astype(BF16), ffn_norm_g[0].reshape(1, d),
                     router_w[0].T, router_b[0].reshape(N_EXPERTS, 1), s // MIX_TILE)
    x1 = _moe(*pre, mod[0], moe_w_gu, moe_w_down, 0, shared_w_gu[0].astype(BF16),
              shared_w_down[0].astype(BF16), final_norm_g.reshape(1, d), tps, False)

    pre = _sgu(x1, mod[1], mix_norm_g[1].reshape(1, d), b_w_in[0].astype(BF16),
               b_ln_g[0].reshape(1, -1), b_ln_b[0].reshape(1, -1), b_w_s[0].astype(BF16), b_b_s[0].T,
               b_w_out[0].astype(BF16), ffn_norm_g[1].reshape(1, d),
               router_w[1].T, router_b[1].reshape(N_EXPERTS, 1), s // MIX_TILE)
    out = _moe(*pre, mod[1], moe_w_gu, moe_w_down, 1, shared_w_gu[1].astype(BF16),
               shared_w_down[1].astype(BF16), final_norm_g.reshape(1, d), tps, True)
    return out.reshape(bsz, s, d)
```

```python
import functools

import jax
import jax.numpy as jnp
from jax import lax
from jax.experimental import pallas as pl
from jax.experimental.pallas import tpu as pltpu

F32 = jnp.float32
BF16 = jnp.bfloat16
HIGHEST = lax.Precision.HIGHEST

GRID_W = 64
N_MOD = 6
NORM_EPS = 1e-6
POS_BASE = 10000.0
RNN_HEADS = 5
CONV_WIDTH = 4
CONV_PAD_LEFT = 2
LRU_C = 8.0
SGU_HEADS = 8
CHUNK = 128
N_EXPERTS = 64
TOP_K = 8
N_GROUPS = 8
TOPK_GROUPS = 4
EXPERTS_PER_GROUP = N_EXPERTS // N_GROUPS
ROUTED_SCALE = 2.5

SUBLANES = 8
ROW_ALIGN = 16
LANES = 128
MOD_ROWS = 8
TOKEN_TILE = 256
MIX_TILE = 512
MIX_SUB = 256
EXPERT_BLOCK = 512
SORTED_ROWS = TOKEN_TILE * TOP_K + N_EXPERTS * ROW_ALIGN
PERM_CHUNK = 512
TAB_WIDTH = 256
TAB_LOCAL = 2 * N_EXPERTS + 2
POS_RADIX = 64
VMEM_LIMIT = 56 * 1024 * 1024


def _params(semantics, vmem=VMEM_LIMIT):
    return pltpu.CompilerParams(dimension_semantics=semantics, vmem_limit_bytes=vmem)


def _silu(x):
    return x * jax.nn.sigmoid(x)


def _rms_mod(x, g, sc, sh):
    y = x * lax.rsqrt(jnp.mean(x * x, axis=-1, keepdims=True) + NORM_EPS)
    return (y * g) * (1.0 + sc) + sh


def _mod_chunk(mod_ref, row, k, d):
    return mod_ref[pl.ds(row, 1), k * d:(k + 1) * d]


def _round_up_rows(count):
    return jnp.maximum(jnp.ceil(count * (1.0 / ROW_ALIGN)), 1.0) * float(ROW_ALIGN)


def _mod_kernel(cc_ref, w_ref, b_ref, o_ref):
    s = _silu(cc_ref[...])
    o_ref[0] = jnp.dot(s, w_ref[0], preferred_element_type=F32, precision=HIGHEST) + b_ref[0]


def _modulation(cc, ada_w, ada_b):
    depth, d, nd = ada_w.shape
    return pl.pallas_call(
        _mod_kernel,
        grid=(depth, nd // d),
        in_specs=[pl.BlockSpec((MOD_ROWS, d), lambda l, j: (0, 0)),
                  pl.BlockSpec((1, d, d), lambda l, j: (l, 0, j)),
                  pl.BlockSpec((1, 1, d), lambda l, j: (l, 0, j))],
        out_specs=pl.BlockSpec((1, MOD_ROWS, d), lambda l, j: (l, 0, j)),
        out_shape=jax.ShapeDtypeStruct((depth, MOD_ROWS, nd), F32),
        compiler_params=_params(("arbitrary", "arbitrary")),
        name="modulation",
    )(cc, ada_w, ada_b.reshape(depth, 1, nd))


def _rglru_in_kernel(x_ref, pe_ref, mod_ref, g_ref, w_ref, x0_ref, gate_ref, u_ref, *, tiles_per_seq):
    d = x_ref.shape[1]
    c = u_ref.shape[1]
    row = pl.program_id(0) // tiles_per_seq
    x = x_ref[...] + pe_ref[...]
    hx = _rms_mod(x, g_ref[...], _mod_chunk(mod_ref, row, 1, d), _mod_chunk(mod_ref, row, 0, d))
    z = jnp.dot(hx.astype(BF16), w_ref[...], preferred_element_type=F32)
    x0_ref[...] = x
    gate_ref[...] = jax.nn.gelu(z[:, :c]).astype(BF16)
    u_ref[...] = z[:, c:]


def _rglru_in(x2, pe, mod, g, w_in):
    n, d = x2.shape
    c = w_in.shape[1] // 2
    s = pe.shape[0]
    tps = s // MIX_TILE
    row = lambda i: (i, 0)
    return pl.pallas_call(
        functools.partial(_rglru_in_kernel, tiles_per_seq=tps),
        grid=(n // MIX_TILE,),
        in_specs=[pl.BlockSpec((MIX_TILE, d), row),
                  pl.BlockSpec((MIX_TILE, d), lambda i: (i % tps, 0)),
                  pl.BlockSpec(mod.shape, lambda i: (0, 0)),
                  pl.BlockSpec((1, d), lambda i: (0, 0)),
                  pl.BlockSpec(w_in.shape, lambda i: (0, 0))],
        out_specs=[pl.BlockSpec((MIX_TILE, d), row),
                   pl.BlockSpec((MIX_TILE, c), row),
                   pl.BlockSpec((MIX_TILE, c), row)],
        out_shape=[jax.ShapeDtypeStruct((n, d), F32),
                   jax.ShapeDtypeStruct((n, c), BF16),
                   jax.ShapeDtypeStruct((n, c), F32)],
        compiler_params=_params(("arbitrary",)),
        name="rglru_in",
    )(x2, pe, mod, g, w_in)


def _ctx_in_kernel(x_ref, mod_ref, g_ref, w_ref, u_ref, *, ctx_row):
    d = x_ref.shape[1]
    hx = _rms_mod(x_ref[...], g_ref[...], _mod_chunk(mod_ref, ctx_row, 1, d),
                  _mod_chunk(mod_ref, ctx_row, 0, d))
    u_ref[...] = jnp.dot(hx.astype(BF16), w_ref[...], preferred_element_type=F32)


def _ctx_in(c2, mod, g, w_u, ctx_row):
    n, d = c2.shape
    c = w_u.shape[1]
    return pl.pallas_call(
        functools.partial(_ctx_in_kernel, ctx_row=ctx_row),
        grid=(n // TOKEN_TILE,),
        in_specs=[pl.BlockSpec((TOKEN_TILE, d), lambda i: (i, 0)),
                  pl.BlockSpec(mod.shape, lambda i: (0, 0)),
                  pl.BlockSpec((1, d), lambda i: (0, 0)),
                  pl.BlockSpec(w_u.shape, lambda i: (0, 0))],
        out_specs=pl.BlockSpec((TOKEN_TILE, c), lambda i: (i, 0)),
        out_shape=jax.ShapeDtypeStruct((n, c), F32),
        compiler_params=_params(("arbitrary",)),
        name="ctx_in",
    )(c2, mod, g, w_u)


def _log_sigmoid(x):
    return jnp.minimum(x, 0.0) - jnp.log1p(jnp.exp(-jnp.abs(x)))


def _lru_scan_kernel(*refs, n_tiles, reverse, conv_done, reset_first, emit_y):
    if conv_done:
        u_ref, wri_ref, rb_ref, ib_ref, lam_ref, h0_ref, out_ref, a_scr, b_scr, h_scr = refs
    else:
        (u_ref, up_ref, un_ref, cw_ref, cb_ref, wri_ref, rb_ref, ib_ref, lam_ref, h0_ref,
         out_ref, uc_ref, ubuf, a_scr, b_scr, h_scr) = refs
    t_rows, c = u_ref.shape
    hb = c // RNN_HEADS
    b = pl.program_id(0)
    j = pl.program_id(1)
    jj = n_tiles - 1 - j if reverse else j

    if conv_done:
        u = u_ref[...]
    else:
        ubuf[SUBLANES:SUBLANES + t_rows, :] = u_ref[...]
        ubuf[0:SUBLANES, :] = jnp.where(jj == 0, 0.0, up_ref[...])
        ubuf[SUBLANES + t_rows:, :] = jnp.where(jj == n_tiles - 1, 0.0, un_ref[...])
        u = cb_ref[...]
        for k in range(CONV_WIDTH):
            u = u + cw_ref[k:k + 1, :] * ubuf[pl.ds(SUBLANES - CONV_PAD_LEFT + k, t_rows), :]
        uc_ref[...] = u

    log_lam = LRU_C * _log_sigmoid(lam_ref[...])
    rows = lax.broadcasted_iota(jnp.int32, (t_rows, 1), 0)
    first_row = jnp.where(j == 0, t_rows - 1 if reverse else 0, -1)
    for h in range(RNN_HEADS):
        sl = slice(h * hb, (h + 1) * hb)
        uh = u[:, sl]
        z = jnp.dot(uh.astype(BF16), wri_ref[h], preferred_element_type=F32)
        r = jax.nn.sigmoid(z[:, :hb] + rb_ref[:, sl])
        ig = jax.nn.sigmoid(z[:, hb:] + ib_ref[:, sl])
        log_a = r * log_lam[:, sl]
        a = jnp.exp(log_a)
        mult = jnp.sqrt((1.0 - a) * (1.0 + a))
        if reset_first:
            mult = jnp.where(rows == first_row, 1.0, mult)
        a_scr[:, sl] = a
        b_scr[:, sl] = mult * ig * uh

    @pl.when(j == 0)
    def _():
        h_scr[...] = h0_ref[pl.ds(b, 1), :]

    n_groups = t_rows // SUBLANES

    def group(g, h):
        base = pl.multiple_of((n_groups - 1 - g if reverse else g) * SUBLANES, SUBLANES)
        for s in range(SUBLANES):
            r = base + (SUBLANES - 1 - s if reverse else s)
            h = a_scr[pl.ds(r, 1), :] * h + b_scr[pl.ds(r, 1), :]
            if emit_y:
                out_ref[pl.ds(r, 1), :] = h
        return h

    h = lax.fori_loop(0, n_groups, group, h_scr[...])
    h_scr[...] = h
    if not emit_y:
        @pl.when(j == n_tiles - 1)
        def _():
            out_ref[pl.ds(b, 1), :] = h


def _lru_scan(u, conv, w_ri, r_b, i_b, lam, h0, *, reverse, reset_first, emit_y):
    n, c = u.shape
    n_batch = h0.shape[0]
    n_tiles = n // n_batch // TOKEN_TILE
    sub = TOKEN_TILE // SUBLANES
    n_sub = n // SUBLANES
    const = lambda b, j: (0, 0)

    def tile(b, j):
        return b * n_tiles + (n_tiles - 1 - j if reverse else j)

    tile_spec = pl.BlockSpec((TOKEN_TILE, c), lambda b, j: (tile(b, j), 0))
    in_specs = [tile_spec]
    args = [u]
    scratch = []
    if conv is not None:
        conv_w, conv_b = conv
        in_specs += [pl.BlockSpec((SUBLANES, c), lambda b, j: (jnp.maximum(tile(b, j) * sub - 1, 0), 0)),
                     pl.BlockSpec((SUBLANES, c), lambda b, j: (jnp.minimum((tile(b, j) + 1) * sub, n_sub - 1), 0)),
                     pl.BlockSpec(conv_w.shape, const),
                     pl.BlockSpec((1, c), const)]
        args += [u, u, conv_w, conv_b.reshape(1, c)]
        scratch = [pltpu.VMEM((TOKEN_TILE + 2 * SUBLANES, c), F32)]
    in_specs += [pl.BlockSpec(w_ri.shape, lambda b, j: (0, 0, 0)),
                 pl.BlockSpec((1, c), const), pl.BlockSpec((1, c), const), pl.BlockSpec((1, c), const),
                 pl.BlockSpec(h0.shape, const)]
    args += [w_ri, r_b.reshape(1, c), i_b.reshape(1, c), lam.reshape(1, c), h0]
    if emit_y:
        out_specs = [tile_spec]
        out_shape = [jax.ShapeDtypeStruct((n, c), F32)]
    else:
        out_specs = [pl.BlockSpec(h0.shape, const)]
        out_shape = [jax.ShapeDtypeStruct(h0.shape, F32)]
    if conv is not None:
        out_specs.append(tile_spec)
        out_shape.append(jax.ShapeDtypeStruct((n, c), F32))
    name = ("lru_scan" if emit_y else "lru_ctx") + ("_rev" if reverse else "_fwd")
    outs = pl.pallas_call(
        functools.partial(_lru_scan_kernel, n_tiles=n_tiles, reverse=reverse, conv_done=conv is None,
                          reset_first=reset_first, emit_y=emit_y),
        grid=(n_batch, n_tiles),
        in_specs=in_specs,
        out_specs=out_specs,
        out_shape=out_shape,
        scratch_shapes=scratch + [pltpu.VMEM((TOKEN_TILE, c), F32),
                                  pltpu.VMEM((TOKEN_TILE, c), F32),
                                  pltpu.VMEM((1, c), F32)],
        compiler_params=_params(("arbitrary", "arbitrary")),
        name=name,
    )(*args)
    return outs if conv is not None else outs[0]


def _route(logits, rb):
    e, t = logits.shape
    neg = -jnp.inf
    scores = jax.nn.sigmoid(logits)
    sel = scores + rb
    iota_g = lax.broadcasted_iota(jnp.int32, (N_GROUPS, t), 0).astype(F32)
    iota_e = lax.broadcasted_iota(jnp.int32, (e, t), 0).astype(F32)

    gs = jnp.full((N_GROUPS, t), neg, F32)
    for g in range(N_GROUPS):
        sg = sel[g * EXPERTS_PER_GROUP:(g + 1) * EXPERTS_PER_GROUP, :]
        m1 = jnp.max(sg, axis=0, keepdims=True)
        i1 = jnp.min(jnp.where(sg == m1, iota_g, float(EXPERTS_PER_GROUP)), axis=0, keepdims=True)
        m2 = jnp.max(jnp.where(iota_g == i1, neg, sg), axis=0, keepdims=True)
        gs = jnp.where(iota_g == float(g), m1 + m2, gs)

    keep = jnp.zeros((N_GROUPS, t), F32)
    for _ in range(TOPK_GROUPS):
        m = jnp.max(gs, axis=0, keepdims=True)
        idx = jnp.min(jnp.where(gs == m, iota_g, float(N_GROUPS)), axis=0, keepdims=True)
        hit = iota_g == idx
        keep = jnp.where(hit, 1.0, keep)
        gs = jnp.where(hit, neg, gs)

    masked = jnp.concatenate(
        [jnp.where(keep[g:g + 1, :] > 0.0, sel[g * EXPERTS_PER_GROUP:(g + 1) * EXPERTS_PER_GROUP, :], neg)
         for g in range(N_GROUPS)], axis=0)

    iota_k = lax.broadcasted_iota(jnp.int32, (TOP_K, t), 0)
    selmask = jnp.zeros((e, t), F32)
    eidx = jnp.zeros((TOP_K, t), F32)
    gw = jnp.zeros((TOP_K, t), F32)
    for k in range(TOP_K):
        m = jnp.max(masked, axis=0, keepdims=True)
        idx = jnp.min(jnp.where(masked == m, iota_e, float(e)), axis=0, keepdims=True)
        hit = iota_e == idx
        gk = jnp.sum(jnp.where(hit, scores, 0.0), axis=0, keepdims=True)
        masked = jnp.where(hit, neg, masked)
        selmask = jnp.where(hit, 1.0, selmask)
        eidx = jnp.where(iota_k == k, idx, eidx)
        gw = jnp.where(iota_k == k, gk, gw)
    gw = gw / jnp.sum(gw, axis=0, keepdims=True) * ROUTED_SCALE
    return eidx, gw, selmask


def _ffn_pre(x1, mod_ref, row, g2_ref, rwt_ref, rb_ref, hx2_ref, eidx_ref, gw_ref, mask_ref, cnt_ref):
    d = x1.shape[1]
    hx2 = _rms_mod(x1, g2_ref[...], _mod_chunk(mod_ref, row, 4, d), _mod_chunk(mod_ref, row, 3, d))
    hx2_ref[...] = hx2
    logits = lax.dot_general(rwt_ref[...], hx2, (((1,), (1,)), ((), ())),
                             preferred_element_type=F32, precision=HIGHEST)
    eidx, gw, selmask = _route(logits, rb_ref[...])
    eidx_ref[...] = eidx.astype(jnp.int32)
    gw_ref[...] = gw
    mask_ref[...] = selmask

    @pl.when(pl.program_id(0) == 0)
    def _():
        cnt_ref[...] = jnp.zeros_like(cnt_ref)

    rows = jnp.zeros((selmask.shape[0], 1), F32)
    for t0 in range(0, selmask.shape[1], TOKEN_TILE):
        rows = rows + _round_up_rows(jnp.sum(selmask[:, t0:t0 + TOKEN_TILE], axis=1, keepdims=True))
    cnt_ref[...] += jnp.broadcast_to(rows, cnt_ref.shape)


def _ffn_pre_specs(n, d):
    row = lambda i: (i, 0)
    col = lambda i: (0, i)
    out_specs = [pl.BlockSpec((MIX_TILE, d), row),
                 pl.BlockSpec((MIX_TILE, d), row),
                 pl.BlockSpec((TOP_K, MIX_TILE), col),
                 pl.BlockSpec((TOP_K, MIX_TILE), col),
                 pl.BlockSpec((N_EXPERTS, MIX_TILE), col),
                 pl.BlockSpec((N_EXPERTS, LANES), lambda i: (0, 0))]
    out_shape = [jax.ShapeDtypeStruct((n, d), F32),
                 jax.ShapeDtypeStruct((n, d), F32),
                 jax.ShapeDtypeStruct((TOP_K, n), jnp.int32),
                 jax.ShapeDtypeStruct((TOP_K, n), F32),
                 jax.ShapeDtypeStruct((N_EXPERTS, n), F32),
                 jax.ShapeDtypeStruct((N_EXPERTS, 128), F32)]
    return out_specs, out_shape


def _rglru_out_kernel(yf_ref, yr_ref, gate_ref, x0_ref, mod_ref, wout_ref, g2_ref, rwt_ref, rb_ref,
                      x1_ref, hx2_ref, eidx_ref, gw_ref, mask_ref, cnt_ref, *, tiles_per_seq):
    d = x0_ref.shape[1]
    row = pl.program_id(0) // tiles_per_seq
    yx = yf_ref[...] + yr_ref[...]
    v = gate_ref[...].astype(F32) * yx
    out = jnp.dot(v.astype(BF16), wout_ref[...], preferred_element_type=F32)
    x1 = x0_ref[...] + _mod_chunk(mod_ref, row, 2, d) * out
    x1_ref[...] = x1
    _ffn_pre(x1, mod_ref, row, g2_ref, rwt_ref, rb_ref, hx2_ref, eidx_ref, gw_ref, mask_ref, cnt_ref)


def _rglru_out(y_fwd, y_rev, gate, x0, mod, w_out, g2, rwt, rb, tiles_per_seq):
    n, d = x0.shape
    c = gate.shape[1]
    row = lambda i: (i, 0)
    const = lambda i: (0, 0)
    out_specs, out_shape = _ffn_pre_specs(n, d)
    return pl.pallas_call(
        functools.partial(_rglru_out_kernel, tiles_per_seq=tiles_per_seq),
        grid=(n // MIX_TILE,),
        in_specs=[pl.BlockSpec((MIX_TILE, c), row),
                  pl.BlockSpec((MIX_TILE, c), row),
                  pl.BlockSpec((MIX_TILE, c), row),
                  pl.BlockSpec((MIX_TILE, d), row),
                  pl.BlockSpec(mod.shape, const),
                  pl.BlockSpec(w_out.shape, const),
                  pl.BlockSpec((1, d), const),
                  pl.BlockSpec(rwt.shape, const),
                  pl.BlockSpec(rb.shape, const)],
        out_specs=out_specs,
        out_shape=out_shape,
        compiler_params=_params(("arbitrary",)),
        name="rglru_out",
    )(y_fwd, y_rev, gate, x0, mod, w_out, g2, rwt, rb)


def _sgu_kernel(x_ref, mod_ref, g_ref, win_ref, lng_ref, lnb_ref, ws_ref, bst_ref, wout_ref,
                g2_ref, rwt_ref, rb_ref,
                x1_ref, hx2_ref, eidx_ref, gw_ref, mask_ref, cnt_ref, m_scr, *, tiles_per_seq):
    t_rows, d = x_ref.shape
    w = wout_ref.shape[0]
    gd = w // SGU_HEADS
    row = pl.program_id(0) // tiles_per_seq
    x = x_ref[...]
    hx = _rms_mod(x, g_ref[...], _mod_chunk(mod_ref, row, 1, d), _mod_chunk(mod_ref, row, 0, d))
    z = jax.nn.gelu(jnp.dot(hx.astype(BF16), win_ref[...], preferred_element_type=F32))
    u = z[:, :w]
    v = z[:, w:]
    mu = jnp.mean(v, axis=-1, keepdims=True)
    vc = v - mu
    v = vc * lax.rsqrt(jnp.mean(vc * vc, axis=-1, keepdims=True) + NORM_EPS) * lng_ref[...] + lnb_ref[...]
    vb = v.astype(BF16)
    for ch in range(t_rows // CHUNK):
        rs = slice(ch * CHUNK, (ch + 1) * CHUNK)
        for g in range(SGU_HEADS):
            cs = slice(g * gd, (g + 1) * gd)
            sv = jnp.dot(ws_ref[g], vb[rs, cs], preferred_element_type=F32) + bst_ref[:, g:g + 1]
            m_scr[rs, cs] = (u[rs, cs] * sv).astype(BF16)
    out = jnp.dot(m_scr[...], wout_ref[...], preferred_element_type=F32)
    x1 = x + _mod_chunk(mod_ref, row, 2, d) * out
    x1_ref[...] = x1
    _ffn_pre(x1, mod_ref, row, g2_ref, rwt_ref, rb_ref, hx2_ref, eidx_ref, gw_ref, mask_ref, cnt_ref)


def _sgu(x, mod, g, w_in, ln_g, ln_b, w_s, b_st, w_out, g2, rwt, rb, tiles_per_seq):
    n, d = x.shape
    w = w_out.shape[0]
    const = lambda i: (0, 0)
    out_specs, out_shape = _ffn_pre_specs(n, d)
    return pl.pallas_call(
        functools.partial(_sgu_kernel, tiles_per_seq=tiles_per_seq),
        grid=(n // MIX_TILE,),
        in_specs=[pl.BlockSpec((MIX_TILE, d), lambda i: (i, 0)),
                  pl.BlockSpec(mod.shape, const),
                  pl.BlockSpec((1, d), const),
                  pl.BlockSpec(w_in.shape, const, pipeline_mode=pl.Buffered(1)),
                  pl.BlockSpec((1, w), const),
                  pl.BlockSpec((1, w), const),
                  pl.BlockSpec(w_s.shape, lambda i: (0, 0, 0)),
                  pl.BlockSpec(b_st.shape, const),
                  pl.BlockSpec(w_out.shape, const, pipeline_mode=pl.Buffered(1)),
                  pl.BlockSpec((1, d), const),
                  pl.BlockSpec(rwt.shape, const),
                  pl.BlockSpec(rb.shape, const)],
        out_specs=out_specs,
        out_shape=out_shape,
        scratch_shapes=[pltpu.VMEM((MIX_TILE, w), BF16)],
        compiler_params=_params(("arbitrary",)),
        name="sgu",
    )(x, mod, g, w_in, ln_g, ln_b, w_s, b_st, w_out, g2, rwt, rb)


def _rank_kernel(mask_ref, eidx_ref, gw_ref, pstart_ref, hi_ref, lo_ref, gate_ref, tab_ref, tabt_ref, carry):
    e, t = mask_ref.shape

    @pl.when(pl.program_id(0) == 0)
    def _():
        carry[...] = jnp.zeros_like(carry)

    m = mask_ref[...]
    r = lax.broadcasted_iota(jnp.int32, (t, t), 0)
    c = lax.broadcasted_iota(jnp.int32, (t, t), 1)
    upper = jnp.where(r <= c, 1.0, 0.0).astype(BF16)
    incl = jnp.dot(m.astype(BF16), upper, preferred_element_type=F32)
    run = _round_up_rows(incl[:, t - 1:t])
    lanes = tab_ref.shape[2]
    re = lax.broadcasted_iota(jnp.int32, (e, e), 0)
    ce = lax.broadcasted_iota(jnp.int32, (e, e), 1)
    lower = jnp.where(ce < re, 1.0, 0.0).astype(BF16)
    tiles = jnp.broadcast_to(run * (1.0 / ROW_ALIGN), (e, lanes)).astype(BF16)
    lstart = jnp.dot(lower, tiles, preferred_element_type=F32)[:, 0:1] * float(ROW_ALIGN)
    pos = jnp.where(m > 0.0, lstart + incl - m, float(POS_RADIX * POS_RADIX - 1))
    hi = jnp.floor(pos * (1.0 / POS_RADIX))
    hi_ref[...] = hi.astype(BF16)
    lo_ref[...] = (pos - hi * float(POS_RADIX)).astype(BF16)
    iota_e = lax.broadcasted_iota(jnp.int32, (e, t), 0)
    eidx = eidx_ref[...]
    gw = gw_ref[...]
    gate = jnp.zeros((e, t), F32)
    for k in range(TOP_K):
        gate = jnp.where(iota_e == eidx[k:k + 1, :], gw[k:k + 1, :], gate)
    gate_ref[...] = gate.astype(BF16)
    lane = lax.broadcasted_iota(jnp.int32, (e, lanes), 1)
    tab_ref[0] = jnp.where(lane == 0, run, jnp.where(lane == 1, pstart_ref[...] + carry[...],
                                                     jnp.where(lane == 2, lstart, 0.0)))
    diag = lax.broadcasted_iota(jnp.int32, (e, lanes), 0) == lane
    row_of = lambda col: jnp.sum(jnp.where(diag, jnp.broadcast_to(col, (e, lanes)), 0.0), axis=0, keepdims=True)
    sub = lax.broadcasted_iota(jnp.int32, (SUBLANES, lanes), 0)
    tabt_ref[0] = jnp.where(sub == 0, row_of(lstart), jnp.where(sub == 1, row_of(lstart + run), 0.0))
    carry[...] = carry[...] + run


def _rank(mask_t, eidx_t, gw_t, pstart):
    e, n = mask_t.shape
    n_tiles = n // TOKEN_TILE
    col = lambda i: (0, i)
    return pl.pallas_call(
        _rank_kernel,
        grid=(n_tiles,),
        in_specs=[pl.BlockSpec((e, TOKEN_TILE), col),
                  pl.BlockSpec((TOP_K, TOKEN_TILE), col),
                  pl.BlockSpec((TOP_K, TOKEN_TILE), col),
                  pl.BlockSpec((e, 1), lambda i: (0, 0))],
        out_specs=[pl.BlockSpec((e, TOKEN_TILE), col),
                   pl.BlockSpec((e, TOKEN_TILE), col),
                   pl.BlockSpec((e, TOKEN_TILE), col),
                   pl.BlockSpec((1, e, LANES), lambda i: (i, 0, 0)),
                   pl.BlockSpec((1, SUBLANES, LANES), lambda i: (i, 0, 0))],
        out_shape=[jax.ShapeDtypeStruct((e, n), BF16),
                   jax.ShapeDtypeStruct((e, n), BF16),
                   jax.ShapeDtypeStruct((e, n), BF16),
                   jax.ShapeDtypeStruct((n_tiles, e, LANES), F32),
                   jax.ShapeDtypeStruct((n_tiles, SUBLANES, LANES), F32)],
        scratch_shapes=[pltpu.VMEM((e, 1), F32)],
        compiler_params=_params(("arbitrary",)),
        name="moe_rank",
    )(mask_t, eidx_t, gw_t, pstart)


def _aligned(v):
    return pl.multiple_of(v, ROW_ALIGN)


def _run_copies(tab_ref, make_copy, unroll):
    def body(pair, carry):
        for priority in range(2):
            e = 2 * pair + priority
            make_copy(_aligned(tab_ref[TAB_LOCAL + e]), _aligned(tab_ref[N_EXPERTS + e]),
                      _aligned(tab_ref[e])).start(priority=priority)
        return carry

    lax.fori_loop(0, N_EXPERTS // 2, body, 0, unroll=unroll)


def _dispatch_kernel(tab_ref, ztab_ref, tabt_ref, hi_ref, lo_ref, x_ref, xs_hbm, sbuf, zbuf, sems, zsem):
    i = pl.program_id(0)
    slot = i % 2
    t = x_ref.shape[0]
    rows = sbuf.shape[1]

    def zero_copy(e):
        n = _aligned(ztab_ref[e])
        return pltpu.make_async_copy(zbuf.at[pl.ds(0, n)], xs_hbm.at[pl.ds(_aligned(ztab_ref[N_EXPERTS + e]), n)], zsem)

    def for_zero_runs(fn):
        def body(e, c):
            @pl.when(ztab_ref[e] > 0)
            def _():
                fn(zero_copy(e))
            return c
        lax.fori_loop(0, N_EXPERTS, body, 0)

        def tail(b, c):
            fn(pltpu.make_async_copy(zbuf, xs_hbm.at[pl.ds(pl.multiple_of(b * EXPERT_BLOCK, EXPERT_BLOCK),
                                                           EXPERT_BLOCK)], zsem))
            return c
        lax.fori_loop(ztab_ref[2 * N_EXPERTS], xs_hbm.shape[0] // EXPERT_BLOCK, tail, 0)

    @pl.when(i == 0)
    def _():
        zbuf[...] = jnp.zeros_like(zbuf)
        for_zero_runs(lambda cp: cp.start())

    xb = x_ref[...].astype(BF16)
    total = tab_ref[2 * N_EXPERTS]
    run_lo = tabt_ref[0, 0:1, 0:N_EXPERTS]
    run_hi = tabt_ref[0, 1:2, 0:N_EXPERTS]

    def permute(r0):
        j_e = (lax.broadcasted_iota(jnp.int32, (PERM_CHUNK, N_EXPERTS), 0) + r0).astype(F32)
        owner = jnp.where((j_e >= run_lo) & (j_e < run_hi), 1.0, 0.0).astype(BF16)
        pos = (jnp.dot(owner, hi_ref[...], preferred_element_type=F32) * float(POS_RADIX)
               + jnp.dot(owner, lo_ref[...], preferred_element_type=F32))
        j_t = (lax.broadcasted_iota(jnp.int32, (PERM_CHUNK, t), 0) + r0).astype(F32)
        p = jnp.where(pos == j_t, 1.0, 0.0).astype(BF16)
        sbuf[slot, r0:r0 + PERM_CHUNK, :] = jnp.dot(p, xb, preferred_element_type=F32).astype(BF16)

    for r0 in range(0, rows, PERM_CHUNK):
        if r0 < t * TOP_K + PERM_CHUNK:
            permute(r0)
        else:
            pl.when(total > r0)(functools.partial(permute, r0))

    _run_copies(tab_ref, lambda loc, glob, n: pltpu.make_async_copy(
        sbuf.at[slot, pl.ds(loc, n)], xs_hbm.at[pl.ds(glob, n)], sems.at[slot]), unroll=True)

    def wait_rows(s, n):
        pltpu.make_async_copy(sbuf.at[s, pl.ds(0, n)], xs_hbm.at[pl.ds(0, n)], sems.at[s]).wait()

    @pl.when(i > 0)
    def _():
        wait_rows(1 - slot, _aligned(tab_ref[2 * N_EXPERTS + 1]))

    @pl.when(i == pl.num_programs(0) - 1)
    def _():
        wait_rows(slot, _aligned(tab_ref[2 * N_EXPERTS]))

    @pl.when(i == 0)
    def _():
        for_zero_runs(lambda cp: cp.wait())


def _dispatch(tab, ztab, tab_t, pos_hi, pos_lo, hx2, xs_rows):
    n, d = hx2.shape
    col = lambda i: (0, i)
    return pl.pallas_call(
        _dispatch_kernel,
        grid=(n // TOKEN_TILE,),
        in_specs=[pl.BlockSpec((TAB_WIDTH,), lambda i: (i,), memory_space=pltpu.SMEM),
                  pl.BlockSpec(memory_space=pltpu.SMEM),
                  pl.BlockSpec((1, SUBLANES, LANES), lambda i: (i, 0, 0)),
                  pl.BlockSpec((N_EXPERTS, TOKEN_TILE), col),
                  pl.BlockSpec((N_EXPERTS, TOKEN_TILE), col),
                  pl.BlockSpec((TOKEN_TILE, d), lambda i: (i, 0))],
        out_specs=pl.BlockSpec(memory_space=pl.ANY),
        out_shape=jax.ShapeDtypeStruct((xs_rows, d), BF16),
        scratch_shapes=[pltpu.VMEM((2, SORTED_ROWS, d), BF16),
                        pltpu.VMEM((EXPERT_BLOCK, d), BF16),
                        pltpu.SemaphoreType.DMA((2,)),
                        pltpu.SemaphoreType.DMA],
        compiler_params=_params(("arbitrary",)),
        name="moe_dispatch",
    )(tab, ztab, tab_t, pos_hi, pos_lo, hx2)


def _experts_kernel(be_ref, nb_ref, xs_ref, wgu_ref, wd_ref, ys_ref, wgu_b, wd_b):
    i = pl.program_id(0)
    ff = wd_b.shape[0]
    used = i < nb_ref[0]
    new_expert = (i == 0) | (be_ref[i] != be_ref[jnp.maximum(i - 1, 0)])

    @pl.when(used & new_expert)
    def _():
        wgu_b[...] = wgu_ref[0, 0].astype(BF16)
        wd_b[...] = wd_ref[0, 0].astype(BF16)

    @pl.when(used)
    def _():
        starts = range(0, xs_ref.shape[0], MIX_SUB)
        hs = [jnp.dot(xs_ref[r0:r0 + MIX_SUB, :], wgu_b[...], preferred_element_type=F32) for r0 in starts]
        for r0, h in zip(starts, hs):
            a = _silu(h[:, :ff]) * h[:, ff:]
            ys_ref[r0:r0 + MIX_SUB, :] = jnp.dot(a.astype(BF16), wd_b[...],
                                                 preferred_element_type=F32).astype(BF16)

    @pl.when(jnp.logical_not(used))
    def _():
        ys_ref[...] = jnp.zeros_like(ys_ref)


def _experts(block_e, n_used, xs, w_gu, w_down, layer):
    rows, d = xs.shape
    nb = rows // EXPERT_BLOCK
    blk = lambda i, be, nu: (jnp.maximum(jnp.minimum(i, nu[0] - 1), 0), 0)
    wmap = lambda i, be, nu: (layer, be[i], 0, 0)
    return pl.pallas_call(
        _experts_kernel,
        grid_spec=pltpu.PrefetchScalarGridSpec(
            num_scalar_prefetch=2,
            grid=(nb,),
            in_specs=[pl.BlockSpec((EXPERT_BLOCK, d), blk),
                      pl.BlockSpec((1, 1) + w_gu.shape[2:], wmap),
                      pl.BlockSpec((1, 1) + w_down.shape[2:], wmap)],
            out_specs=pl.BlockSpec((EXPERT_BLOCK, d), lambda i, be, nu: (i, 0)),
            scratch_shapes=[pltpu.VMEM(w_gu.shape[2:], BF16), pltpu.VMEM(w_down.shape[2:], BF16)]),
        out_shape=jax.ShapeDtypeStruct((rows, d), BF16),
        compiler_params=_params(("arbitrary",)),
        name="moe_experts",
    )(block_e, n_used, xs, w_gu, w_down)


def _combine_kernel(tab_ref, tabn_ref, tabv_ref, hi_ref, lo_ref, gate_ref, ys_hbm, hx2_ref, x1_ref, mod_ref,
                    wsgu_ref, wsd_ref, fg_ref, out_ref, ybuf, sems, *, tiles_per_seq, final_norm):
    t_rows, d = x1_ref.shape
    ff = wsd_ref.shape[0]
    rows = ybuf.shape[1]
    i = pl.program_id(0)
    last = pl.num_programs(0) - 1
    slot = i % 2
    row = i // tiles_per_seq

    def gather(table, s, unroll):
        _run_copies(table, lambda loc, glob, n: pltpu.make_async_copy(
            ys_hbm.at[pl.ds(glob, n)], ybuf.at[s, pl.ds(loc, n)], sems.at[s]), unroll=unroll)

    def wait_rows(table, s):
        n = _aligned(table[2 * N_EXPERTS])
        pltpu.make_async_copy(ys_hbm.at[pl.ds(0, n)], ybuf.at[s, pl.ds(0, n)], sems.at[s]).wait()

    @pl.when(i == 0)
    def _():
        ybuf[...] = jnp.zeros_like(ybuf)
        gather(tab_ref, 0, False)

    gather(tabn_ref, 1 - slot, True)

    hs = jnp.dot(hx2_ref[...].astype(BF16), wsgu_ref[...], preferred_element_type=F32)
    shared = jnp.dot((_silu(hs[:, :ff]) * hs[:, ff:]).astype(BF16), wsd_ref[...], preferred_element_type=F32)

    run_lo = tabv_ref[0, :, 2:3]
    run_hi = run_lo + tabv_ref[0, :, 0:1]
    j_e = lax.broadcasted_iota(jnp.int32, (N_EXPERTS, rows), 1).astype(F32)
    owner = jnp.where((j_e >= run_lo) & (j_e < run_hi), 1.0, 0.0).astype(BF16)
    pos = (jnp.dot(hi_ref[...], owner, preferred_element_type=F32) * float(POS_RADIX)
           + jnp.dot(lo_ref[...], owner, preferred_element_type=F32))
    gates = jnp.dot(gate_ref[...], owner, preferred_element_type=F32)
    j_t = lax.broadcasted_iota(jnp.int32, (t_rows, rows), 1).astype(F32)
    gb = jnp.where(pos == j_t, gates, 0.0).astype(BF16)

    wait_rows(tab_ref, slot)
    routed = jnp.dot(gb, ybuf[slot], preferred_element_type=F32)
    x2 = x1_ref[...] + _mod_chunk(mod_ref, row, 5, d) * (routed + shared)
    if final_norm:
        x2 = x2 * lax.rsqrt(jnp.mean(x2 * x2, axis=-1, keepdims=True) + NORM_EPS) * fg_ref[...]
    out_ref[...] = x2

    @pl.when(i == last)
    def _():
        wait_rows(tabn_ref, 1 - slot)


def _combine(tab, tab_v, pos_hi, pos_lo, gate, ys, hx2, x1, mod, ws_gu, ws_down, fg, tiles_per_seq, final_norm):
    n, d = x1.shape
    row = lambda i: (i, 0)
    const = lambda i: (0, 0)
    return pl.pallas_call(
        functools.partial(_combine_kernel, tiles_per_seq=tiles_per_seq, final_norm=final_norm),
        grid=(n // TOKEN_TILE,),
        in_specs=[pl.BlockSpec((TAB_WIDTH,), lambda i: (i,), memory_space=pltpu.SMEM),
                  pl.BlockSpec((TAB_WIDTH,), lambda i: (jnp.minimum(i + 1, n // TOKEN_TILE - 1),),
                               memory_space=pltpu.SMEM),
                  pl.BlockSpec((1, N_EXPERTS, LANES), lambda i: (i, 0, 0)),
                  pl.BlockSpec((TOKEN_TILE, N_EXPERTS), row),
                  pl.BlockSpec((TOKEN_TILE, N_EXPERTS), row),
                  pl.BlockSpec((TOKEN_TILE, N_EXPERTS), row),
                  pl.BlockSpec(memory_space=pl.ANY),
                  pl.BlockSpec((TOKEN_TILE, d), row),
                  pl.BlockSpec((TOKEN_TILE, d), row),
                  pl.BlockSpec(mod.shape, const),
                  pl.BlockSpec(ws_gu.shape, const),
                  pl.BlockSpec(ws_down.shape, const),
                  pl.BlockSpec((1, d), const)],
        out_specs=pl.BlockSpec((TOKEN_TILE, d), row),
        out_shape=jax.ShapeDtypeStruct((n, d), F32),
        scratch_shapes=[pltpu.VMEM((2, SORTED_ROWS, d), BF16), pltpu.SemaphoreType.DMA((2,))],
        compiler_params=_params(("arbitrary",)),
        name="moe_combine",
    )(tab, tab, tab_v, pos_hi, pos_lo, gate, ys, hx2, x1, mod, ws_gu, ws_down, fg)


def _moe(x1, hx2, eidx_t, gw_t, mask_t, cnt, mod, w_gu, w_down, layer, ws_gu, ws_down, fg, tiles_per_seq,
         final_norm):
    n, d = x1.shape
    n_tiles = n // TOKEN_TILE
    counts = cnt[:, 0].astype(jnp.int32)
    padded = (counts + EXPERT_BLOCK - 1) // EXPERT_BLOCK * EXPERT_BLOCK
    pad_end = jnp.cumsum(padded)
    pad_start = pad_end - padded
    max_rows = n * TOP_K + n_tiles * N_EXPERTS * ROW_ALIGN + N_EXPERTS * (EXPERT_BLOCK - ROW_ALIGN)
    n_blocks = (max_rows + EXPERT_BLOCK - 1) // EXPERT_BLOCK
    n_used = (pad_end[-1:] // EXPERT_BLOCK).astype(jnp.int32)
    block_start = jnp.arange(n_blocks, dtype=jnp.int32) * EXPERT_BLOCK
    block_e = jnp.minimum(jnp.sum((pad_end[None, :] <= block_start[:, None]).astype(jnp.int32), axis=1),
                          N_EXPERTS - 1)

    pos_hi, pos_lo, gate, tab_f, tab_t = _rank(mask_t, eidx_t, gw_t, pad_start.astype(F32).reshape(N_EXPERTS, 1))
    run_len = tab_f[:, :, 0].astype(jnp.int32)
    run_start = tab_f[:, :, 1].astype(jnp.int32)
    total = jnp.sum(run_len, axis=1, keepdims=True)
    prev_total = jnp.concatenate([jnp.zeros((1, 1), jnp.int32), total[:-1]], axis=0)
    run_local = tab_f[:, :, 2].astype(jnp.int32)
    fill = jnp.zeros((n_tiles, TAB_WIDTH - 3 * N_EXPERTS - 2), jnp.int32)
    tab = jnp.concatenate([run_len, run_start, total, prev_total, run_local, fill], axis=1).reshape(-1)
    ztab = jnp.concatenate([padded - counts, pad_start + counts, n_used])

    xs = _dispatch(tab, ztab, tab_t, pos_hi, pos_lo, hx2, n_blocks * EXPERT_BLOCK)
    ys = _experts(block_e, n_used, xs, w_gu, w_down, layer)
    return _combine(tab, tab_f, pos_hi.T, pos_lo.T, gate.T, ys, hx2, x1, mod, ws_gu, ws_down, fg,
                    tiles_per_seq, final_norm)


def _sincos_2d(rows, d):
    quarter = d // 4
    omega = 1.0 / (POS_BASE ** (jnp.arange(quarter, dtype=F32) / quarter))

    def emb(n):
        p = jnp.arange(n, dtype=F32)[:, None] * omega[None, :]
        return jnp.concatenate([jnp.sin(p), jnp.cos(p)], axis=-1)

    er, ec = emb(rows), emb(GRID_W)
    pe = jnp.concatenate([jnp.broadcast_to(er[:, None, :], (rows, GRID_W, d // 2)),
                          jnp.broadcast_to(ec[None, :, :], (rows, GRID_W, d // 2))], axis=-1)
    return pe.reshape(rows * GRID_W, d)


def kernel(x, c, ctx, c_ctx, ada_w, ada_b, mix_norm_g, ffn_norm_g, a_w_in, a_conv_w, a_conv_b, a_gate_r_w, a_gate_r_b, a_gate_i_w, a_gate_i_b, a_lambda, a_w_out, b_w_in, b_ln_g, b_ln_b, b_w_s, b_b_s, b_w_out, router_w, router_b, moe_w_gu, moe_w_down, shared_w_gu, shared_w_down, final_norm_g):
    bsz, s, d = x.shape
    ctx_len = ctx.shape[1]
    depth = ada_w.shape[0]
    assert depth == 2 and bsz < MOD_ROWS and s % MIX_TILE == 0 and MIX_TILE % TOKEN_TILE == 0
    assert ctx_len % TOKEN_TILE == 0
    n = bsz * s
    tps = s // TOKEN_TILE
    ctx_row = bsz

    cc = jnp.zeros((MOD_ROWS, d), F32).at[:bsz].set(c).at[ctx_row].set(c_ctx)
    mod = _modulation(cc, ada_w, ada_b)
    pe = _sincos_2d(s // GRID_W, d)
    rc = a_w_in.shape[2] // 2

    w_in0 = a_w_in[0].astype(BF16)
    g_mix0 = mix_norm_g[0].reshape(1, d)
    x0, gate, ux = _rglru_in(x.reshape(n, d), pe, mod[0], g_mix0, w_in0)
    uc = _ctx_in(ctx.reshape(bsz * ctx_len, d), mod[0], g_mix0, w_in0[:, rc:], ctx_row)
    w_ri = jnp.concatenate([a_gate_r_w[0], a_gate_i_w[0]], axis=-1).astype(BF16)
    conv = (a_conv_w[0], a_conv_b[0])
    gates = [(w_ri[k], a_gate_r_b[0, k], a_gate_i_b[0, k], a_lambda[0, k]) for k in range(2)]
    h_zero = jnp.zeros((bsz, rc), F32)
    h_fwd, uc = _lru_scan(uc, conv, *gates[0], h_zero, reverse=False, reset_first=True, emit_y=False)
    h_rev = _lru_scan(uc, None, *gates[1], h_zero, reverse=True, reset_first=True, emit_y=False)
    y_fwd, ux = _lru_scan(ux, conv, *gates[0], h_fwd, reverse=False, reset_first=False, emit_y=True)
    y_rev = _lru_scan(ux, None, *gates[1], h_rev, reverse=True, reset_first=False, emit_y=True)
    pre = _rglru_out(y_fwd, y_rev, gate, x0, mod[0], a_w_out[0].astype(BF16), ffn_norm_g[0].reshape(1, d),
                     router_w[0].T, router_b[0].reshape(N_EXPERTS, 1), s // MIX_TILE)
    x1 = _moe(*pre, mod[0], moe_w_gu, moe_w_down, 0, shared_w_gu[0].astype(BF16),
              shared_w_down[0].astype(BF16), final_norm_g.reshape(1, d), tps, False)

    pre = _sgu(x1, mod[1], mix_norm_g[1].reshape(1, d), b_w_in[0].astype(BF16),
               b_ln_g[0].reshape(1, -1), b_ln_b[0].reshape(1, -1), b_w_s[0].astype(BF16), b_b_s[0].T,
               b_w_out[0].astype(BF16), ffn_norm_g[1].reshape(1, d),
               router_w[1].T, router_b[1].reshape(N_EXPERTS, 1), s // MIX_TILE)
    out = _moe(*pre, mod[1], moe_w_gu, moe_w_down, 1, shared_w_gu[1].astype(BF16),
               shared_w_down[1].astype(BF16), final_norm_g.reshape(1, d), tps, True)
    return out.reshape(bsz, s, d)
```

```python
import functools

import jax
import jax.numpy as jnp
from jax import lax
from jax.experimental import pallas as pl
from jax.experimental.pallas import tpu as pltpu

F32 = jnp.float32
BF16 = jnp.bfloat16
HIGHEST = lax.Precision.HIGHEST

GRID_W = 64
N_MOD = 6
NORM_EPS = 1e-6
POS_BASE = 10000.0
RNN_HEADS = 5
CONV_WIDTH = 4
CONV_PAD_LEFT = 2
LRU_C = 8.0
SGU_HEADS = 8
CHUNK = 128
N_EXPERTS = 64
TOP_K = 8
N_GROUPS = 8
TOPK_GROUPS = 4
EXPERTS_PER_GROUP = N_EXPERTS // N_GROUPS
ROUTED_SCALE = 2.5

SUBLANES = 8
ROW_ALIGN = 16
LANES = 128
MOD_ROWS = 8
TOKEN_TILE = 256
MIX_TILE = 512
MIX_SUB = 256
EXPERT_BLOCK = 512
SORTED_ROWS = TOKEN_TILE * TOP_K + N_EXPERTS * ROW_ALIGN
PERM_CHUNK = 512
TAB_WIDTH = 256
TAB_LOCAL = 2 * N_EXPERTS + 2
POS_RADIX = 64
VMEM_LIMIT = 56 * 1024 * 1024


def _params(semantics, vmem=VMEM_LIMIT):
    return pltpu.CompilerParams(dimension_semantics=semantics, vmem_limit_bytes=vmem)


def _silu(x):
    return x * jax.nn.sigmoid(x)


def _rms_mod(x, g, sc, sh):
    y = x * lax.rsqrt(jnp.mean(x * x, axis=-1, keepdims=True) + NORM_EPS)
    return (y * g) * (1.0 + sc) + sh


def _mod_chunk(mod_ref, row, k, d):
    return mod_ref[pl.ds(row, 1), k * d:(k + 1) * d]


def _round_up_rows(count):
    return jnp.maximum(jnp.ceil(count * (1.0 / ROW_ALIGN)), 1.0) * float(ROW_ALIGN)


def _mod_kernel(cc_ref, w_ref, b_ref, o_ref):
    s = _silu(cc_ref[...])
    o_ref[0] = jnp.dot(s, w_ref[0], preferred_element_type=F32, precision=HIGHEST) + b_ref[0]


def _modulation(cc, ada_w, ada_b):
    depth, d, nd = ada_w.shape
    return pl.pallas_call(
        _mod_kernel,
        grid=(depth, nd // d),
        in_specs=[pl.BlockSpec((MOD_ROWS, d), lambda l, j: (0, 0)),
                  pl.BlockSpec((1, d, d), lambda l, j: (l, 0, j)),
                  pl.BlockSpec((1, 1, d), lambda l, j: (l, 0, j))],
        out_specs=pl.BlockSpec((1, MOD_ROWS, d), lambda l, j: (l, 0, j)),
        out_shape=jax.ShapeDtypeStruct((depth, MOD_ROWS, nd), F32),
        compiler_params=_params(("arbitrary", "arbitrary")),
        name="modulation",
    )(cc, ada_w, ada_b.reshape(depth, 1, nd))


def _rglru_in_kernel(x_ref, pe_ref, mod_ref, g_ref, w_ref, x0_ref, gate_ref, u_ref, *, tiles_per_seq):
    d = x_ref.shape[1]
    c = u_ref.shape[1]
    row = pl.program_id(0) // tiles_per_seq
    x = x_ref[...] + pe_ref[...]
    hx = _rms_mod(x, g_ref[...], _mod_chunk(mod_ref, row, 1, d), _mod_chunk(mod_ref, row, 0, d))
    z = jnp.dot(hx.astype(BF16), w_ref[...], preferred_element_type=F32)
    x0_ref[...] = x
    gate_ref[...] = jax.nn.gelu(z[:, :c]).astype(BF16)
    u_ref[...] = z[:, c:]


def _rglru_in(x2, pe, mod, g, w_in):
    n, d = x2.shape
    c = w_in.shape[1] // 2
    s = pe.shape[0]
    tps = s // MIX_TILE
    row = lambda i: (i, 0)
    return pl.pallas_call(
        functools.partial(_rglru_in_kernel, tiles_per_seq=tps),
        grid=(n // MIX_TILE,),
        in_specs=[pl.BlockSpec((MIX_TILE, d), row),
                  pl.BlockSpec((MIX_TILE, d), lambda i: (i % tps, 0)),
                  pl.BlockSpec(mod.shape, lambda i: (0, 0)),
                  pl.BlockSpec((1, d), lambda i: (0, 0)),
                  pl.BlockSpec(w_in.shape, lambda i: (0, 0))],
        out_specs=[pl.BlockSpec((MIX_TILE, d), row),
                   pl.BlockSpec((MIX_TILE, c), row),
                   pl.BlockSpec((MIX_TILE, c), row)],
        out_shape=[jax.ShapeDtypeStruct((n, d), F32),
                   jax.ShapeDtypeStruct((n, c), BF16),
                   jax.ShapeDtypeStruct((n, c), F32)],
        compiler_params=_params(("arbitrary",)),
        name="rglru_in",
    )(x2, pe, mod, g, w_in)


def _ctx_in_kernel(x_ref, mod_ref, g_ref, w_ref, u_ref, *, ctx_row):
    d = x_ref.shape[1]
    hx = _rms_mod(x_ref[...], g_ref[...], _mod_chunk(mod_ref, ctx_row, 1, d),
                  _mod_chunk(mod_ref, ctx_row, 0, d))
    u_ref[...] = jnp.dot(hx.astype(BF16), w_ref[...], preferred_element_type=F32)


def _ctx_in(c2, mod, g, w_u, ctx_row):
    n, d = c2.shape
    c = w_u.shape[1]
    return pl.pallas_call(
        functools.partial(_ctx_in_kernel, ctx_row=ctx_row),
        grid=(n // TOKEN_TILE,),
        in_specs=[pl.BlockSpec((TOKEN_TILE, d), lambda i: (i, 0)),
                  pl.BlockSpec(mod.shape, lambda i: (0, 0)),
                  pl.BlockSpec((1, d), lambda i: (0, 0)),
                  pl.BlockSpec(w_u.shape, lambda i: (0, 0))],
        out_specs=pl.BlockSpec((TOKEN_TILE, c), lambda i: (i, 0)),
        out_shape=jax.ShapeDtypeStruct((n, c), F32),
        compiler_params=_params(("arbitrary",)),
        name="ctx_in",
    )(c2, mod, g, w_u)


def _log_sigmoid(x):
    return jnp.minimum(x, 0.0) - jnp.log1p(jnp.exp(-jnp.abs(x)))


def _lru_scan_kernel(*refs, n_tiles, reverse, conv_done, reset_first, emit_y):
    if conv_done:
        u_ref, wri_ref, rb_ref, ib_ref, lam_ref, h0_ref, out_ref, a_scr, b_scr, h_scr = refs
    else:
        (u_ref, up_ref, un_ref, cw_ref, cb_ref, wri_ref, rb_ref, ib_ref, lam_ref, h0_ref,
         out_ref, uc_ref, ubuf, a_scr, b_scr, h_scr) = refs
    t_rows, c = u_ref.shape
    hb = c // RNN_HEADS
    b = pl.program_id(0)
    j = pl.program_id(1)
    jj = n_tiles - 1 - j if reverse else j

    if conv_done:
        u = u_ref[...]
    else:
        ubuf[SUBLANES:SUBLANES + t_rows, :] = u_ref[...]
        ubuf[0:SUBLANES, :] = jnp.where(jj == 0, 0.0, up_ref[...])
        ubuf[SUBLANES + t_rows:, :] = jnp.where(jj == n_tiles - 1, 0.0, un_ref[...])
        u = cb_ref[...]
        for k in range(CONV_WIDTH):
            u = u + cw_ref[k:k + 1, :] * ubuf[pl.ds(SUBLANES - CONV_PAD_LEFT + k, t_rows), :]
        uc_ref[...] = u

    log_lam = LRU_C * _log_sigmoid(lam_ref[...])
    rows = lax.broadcasted_iota(jnp.int32, (t_rows, 1), 0)
    first_row = jnp.where(j == 0, t_rows - 1 if reverse else 0, -1)
    for h in range(RNN_HEADS):
        sl = slice(h * hb, (h + 1) * hb)
        uh = u[:, sl]
        z = jnp.dot(uh.astype(BF16), wri_ref[h], preferred_element_type=F32)
        r = jax.nn.sigmoid(z[:, :hb] + rb_ref[:, sl])
        ig = jax.nn.sigmoid(z[:, hb:] + ib_ref[:, sl])
        log_a = r * log_lam[:, sl]
        a = jnp.exp(log_a)
        mult = jnp.sqrt((1.0 - a) * (1.0 + a))
        if reset_first:
            mult = jnp.where(rows == first_row, 1.0, mult)
        a_scr[:, sl] = a
        b_scr[:, sl] = mult * ig * uh

    @pl.when(j == 0)
    def _():
        h_scr[...] = h0_ref[pl.ds(b, 1), :]

    n_groups = t_rows // SUBLANES

    def group(g, h):
        base = pl.multiple_of((n_groups - 1 - g if reverse else g) * SUBLANES, SUBLANES)
        for s in range(SUBLANES):
            r = base + (SUBLANES - 1 - s if reverse else s)
            h = a_scr[pl.ds(r, 1), :] * h + b_scr[pl.ds(r, 1), :]
            if emit_y:
                out_ref[pl.ds(r, 1), :] = h
        return h

    h = lax.fori_loop(0, n_groups, group, h_scr[...])
    h_scr[...] = h
    if not emit_y:
        @pl.when(j == n_tiles - 1)
        def _():
            out_ref[pl.ds(b, 1), :] = h


def _lru_scan(u, conv, w_ri, r_b, i_b, lam, h0, *, reverse, reset_first, emit_y):
    n, c = u.shape
    n_batch = h0.shape[0]
    n_tiles = n // n_batch // TOKEN_TILE
    sub = TOKEN_TILE // SUBLANES
    n_sub = n // SUBLANES
    const = lambda b, j: (0, 0)

    def tile(b, j):
        return b * n_tiles + (n_tiles - 1 - j if reverse else j)

    tile_spec = pl.BlockSpec((TOKEN_TILE, c), lambda b, j: (tile(b, j), 0))
    in_specs = [tile_spec]
    args = [u]
    scratch = []
    if conv is not None:
        conv_w, conv_b = conv
        in_specs += [pl.BlockSpec((SUBLANES, c), lambda b, j: (jnp.maximum(tile(b, j) * sub - 1, 0), 0)),
                     pl.BlockSpec((SUBLANES, c), lambda b, j: (jnp.minimum((tile(b, j) + 1) * sub, n_sub - 1), 0)),
                     pl.BlockSpec(conv_w.shape, const),
                     pl.BlockSpec((1, c), const)]
        args += [u, u, conv_w, conv_b.reshape(1, c)]
        scratch = [pltpu.VMEM((TOKEN_TILE + 2 * SUBLANES, c), F32)]
    in_specs += [pl.BlockSpec(w_ri.shape, lambda b, j: (0, 0, 0)),
                 pl.BlockSpec((1, c), const), pl.BlockSpec((1, c), const), pl.BlockSpec((1, c), const),
                 pl.BlockSpec(h0.shape, const)]
    args += [w_ri, r_b.reshape(1, c), i_b.reshape(1, c), lam.reshape(1, c), h0]
    if emit_y:
        out_specs = [tile_spec]
        out_shape = [jax.ShapeDtypeStruct((n, c), F32)]
    else:
        out_specs = [pl.BlockSpec(h0.shape, const)]
        out_shape = [jax.ShapeDtypeStruct(h0.shape, F32)]
    if conv is not None:
        out_specs.append(tile_spec)
        out_shape.append(jax.ShapeDtypeStruct((n, c), F32))
    name = ("lru_scan" if emit_y else "lru_ctx") + ("_rev" if reverse else "_fwd")
    outs = pl.pallas_call(
        functools.partial(_lru_scan_kernel, n_tiles=n_tiles, reverse=reverse, conv_done=conv is None,
                          reset_first=reset_first, emit_y=emit_y),
        grid=(n_batch, n_tiles),
        in_specs=in_specs,
        out_specs=out_specs,
        out_shape=out_shape,
        scratch_shapes=scratch + [pltpu.VMEM((TOKEN_TILE, c), F32),
                                  pltpu.VMEM((TOKEN_TILE, c), F32),
                                  pltpu.VMEM((1, c), F32)],
        compiler_params=_params(("arbitrary", "arbitrary")),
        name=name,
    )(*args)
    return outs if conv is not None else outs[0]


def _route(logits, rb):
    e, t = logits.shape
    neg = -jnp.inf
    scores = jax.nn.sigmoid(logits)
    sel = scores + rb
    iota_g = lax.broadcasted_iota(jnp.int32, (N_GROUPS, t), 0).astype(F32)
    iota_e = lax.broadcasted_iota(jnp.int32, (e, t), 0).astype(F32)

    gs = jnp.full((N_GROUPS, t), neg, F32)
    for g in range(N_GROUPS):
        sg = sel[g * EXPERTS_PER_GROUP:(g + 1) * EXPERTS_PER_GROUP, :]
        m1 = jnp.max(sg, axis=0, keepdims=True)
        i1 = jnp.min(jnp.where(sg == m1, iota_g, float(EXPERTS_PER_GROUP)), axis=0, keepdims=True)
        m2 = jnp.max(jnp.where(iota_g == i1, neg, sg), axis=0, keepdims=True)
        gs = jnp.where(iota_g == float(g), m1 + m2, gs)

    keep = jnp.zeros((N_GROUPS, t), F32)
    for _ in range(TOPK_GROUPS):
        m = jnp.max(gs, axis=0, keepdims=True)
        idx = jnp.min(jnp.where(gs == m, iota_g, float(N_GROUPS)), axis=0, keepdims=True)
        hit = iota_g == idx
        keep = jnp.where(hit, 1.0, keep)
        gs = jnp.where(hit, neg, gs)

    masked = jnp.concatenate(
        [jnp.where(keep[g:g + 1, :] > 0.0, sel[g * EXPERTS_PER_GROUP:(g + 1) * EXPERTS_PER_GROUP, :], neg)
         for g in range(N_GROUPS)], axis=0)

    iota_k = lax.broadcasted_iota(jnp.int32, (TOP_K, t), 0)
    selmask = jnp.zeros((e, t), F32)
    eidx = jnp.zeros((TOP_K, t), F32)
    gw = jnp.zeros((TOP_K, t), F32)
    for k in range(TOP_K):
        m = jnp.max(masked, axis=0, keepdims=True)
        idx = jnp.min(jnp.where(masked == m, iota_e, float(e)), axis=0, keepdims=True)
        hit = iota_e == idx
        gk = jnp.sum(jnp.where(hit, scores, 0.0), axis=0, keepdims=True)
        masked = jnp.where(hit, neg, masked)
        selmask = jnp.where(hit, 1.0, selmask)
        eidx = jnp.where(iota_k == k, idx, eidx)
        gw = jnp.where(iota_k == k, gk, gw)
    gw = gw / jnp.sum(gw, axis=0, keepdims=True) * ROUTED_SCALE
    return eidx, gw, selmask


def _ffn_pre(x1, mod_ref, row, g2_ref, rwt_ref, rb_ref, hx2_ref, hi_ref, lo_ref, gate_ref, tab_ref, tabt_ref):
    d = x1.shape[1]
    hx2 = _rms_mod(x1, g2_ref[...], _mod_chunk(mod_ref, row, 4, d), _mod_chunk(mod_ref, row, 3, d))
    hx2_ref[...] = hx2
    logits = lax.dot_general(rwt_ref[...], hx2, (((1,), (1,)), ((), ())),
                             preferred_element_type=F32, precision=HIGHEST)
    eidx, gw, selmask = _route(logits, rb_ref[...])
    for s in range(selmask.shape[1] // TOKEN_TILE):
        ts = slice(s * TOKEN_TILE, (s + 1) * TOKEN_TILE)
        hi, lo, gate, tab, tabt = _sorted_layout(selmask[:, ts], eidx[:, ts], gw[:, ts])
        hi_ref[:, ts] = hi
        lo_ref[:, ts] = lo
        gate_ref[:, ts] = gate
        tab_ref[s] = tab
        tabt_ref[s] = tabt


def _ffn_pre_specs(n, d):
    row = lambda i: (i, 0)
    col = lambda i: (0, i)
    sub = MIX_TILE // TOKEN_TILE
    lead = lambda i: (i, 0, 0)
    out_specs = [pl.BlockSpec((MIX_TILE, d), row),
                 pl.BlockSpec((MIX_TILE, d), row),
                 pl.BlockSpec((N_EXPERTS, MIX_TILE), col),
                 pl.BlockSpec((N_EXPERTS, MIX_TILE), col),
                 pl.BlockSpec((N_EXPERTS, MIX_TILE), col),
                 pl.BlockSpec((sub, N_EXPERTS, LANES), lead),
                 pl.BlockSpec((sub, SUBLANES, LANES), lead)]
    out_shape = [jax.ShapeDtypeStruct((n, d), F32),
                 jax.ShapeDtypeStruct((n, d), F32),
                 jax.ShapeDtypeStruct((N_EXPERTS, n), BF16),
                 jax.ShapeDtypeStruct((N_EXPERTS, n), BF16),
                 jax.ShapeDtypeStruct((N_EXPERTS, n), BF16),
                 jax.ShapeDtypeStruct((n // TOKEN_TILE, N_EXPERTS, LANES), F32),
                 jax.ShapeDtypeStruct((n // TOKEN_TILE, SUBLANES, LANES), F32)]
    return out_specs, out_shape


def _rglru_out_kernel(yf_ref, yr_ref, gate_ref, x0_ref, mod_ref, wout_ref, g2_ref, rwt_ref, rb_ref,
                      x1_ref, hx2_ref, hi_ref, lo_ref, gate_out_ref, tab_ref, tabt_ref, *, tiles_per_seq):
    d = x0_ref.shape[1]
    row = pl.program_id(0) // tiles_per_seq
    yx = yf_ref[...] + yr_ref[...]
    v = gate_ref[...].astype(F32) * yx
    out = jnp.dot(v.astype(BF16), wout_ref[...], preferred_element_type=F32)
    x1 = x0_ref[...] + _mod_chunk(mod_ref, row, 2, d) * out
    x1_ref[...] = x1
    _ffn_pre(x1, mod_ref, row, g2_ref, rwt_ref, rb_ref, hx2_ref, hi_ref, lo_ref, gate_out_ref, tab_ref, tabt_ref)


def _rglru_out(y_fwd, y_rev, gate, x0, mod, w_out, g2, rwt, rb, tiles_per_seq):
    n, d = x0.shape
    c = gate.shape[1]
    row = lambda i: (i, 0)
    const = lambda i: (0, 0)
    out_specs, out_shape = _ffn_pre_specs(n, d)
    return pl.pallas_call(
        functools.partial(_rglru_out_kernel, tiles_per_seq=tiles_per_seq),
        grid=(n // MIX_TILE,),
        in_specs=[pl.BlockSpec((MIX_TILE, c), row),
                  pl.BlockSpec((MIX_TILE, c), row),
                  pl.BlockSpec((MIX_TILE, c), row),
                  pl.BlockSpec((MIX_TILE, d), row),
                  pl.BlockSpec(mod.shape, const),
                  pl.BlockSpec(w_out.shape, const),
                  pl.BlockSpec((1, d), const),
                  pl.BlockSpec(rwt.shape, const),
                  pl.BlockSpec(rb.shape, const)],
        out_specs=out_specs,
        out_shape=out_shape,
        compiler_params=_params(("arbitrary",)),
        name="rglru_out",
    )(y_fwd, y_rev, gate, x0, mod, w_out, g2, rwt, rb)


def _sgu_kernel(x_ref, mod_ref, g_ref, win_ref, lng_ref, lnb_ref, ws_ref, bst_ref, wout_ref,
                g2_ref, rwt_ref, rb_ref,
                x1_ref, hx2_ref, hi_ref, lo_ref, gate_out_ref, tab_ref, tabt_ref, m_scr, *, tiles_per_seq):
    t_rows, d = x_ref.shape
    w = wout_ref.shape[0]
    gd = w // SGU_HEADS
    row = pl.program_id(0) // tiles_per_seq
    x = x_ref[...]
    hx = _rms_mod(x, g_ref[...], _mod_chunk(mod_ref, row, 1, d), _mod_chunk(mod_ref, row, 0, d))
    z = jax.nn.gelu(jnp.dot(hx.astype(BF16), win_ref[...], preferred_element_type=F32))
    u = z[:, :w]
    v = z[:, w:]
    mu = jnp.mean(v, axis=-1, keepdims=True)
    vc = v - mu
    v = vc * lax.rsqrt(jnp.mean(vc * vc, axis=-1, keepdims=True) + NORM_EPS) * lng_ref[...] + lnb_ref[...]
    vb = v.astype(BF16)
    for ch in range(t_rows // CHUNK):
        rs = slice(ch * CHUNK, (ch + 1) * CHUNK)
        for g in range(SGU_HEADS):
            cs = slice(g * gd, (g + 1) * gd)
            sv = jnp.dot(ws_ref[g], vb[rs, cs], preferred_element_type=F32) + bst_ref[:, g:g + 1]
            m_scr[rs, cs] = (u[rs, cs] * sv).astype(BF16)
    out = jnp.dot(m_scr[...], wout_ref[...], preferred_element_type=F32)
    x1 = x + _mod_chunk(mod_ref, row, 2, d) * out
    x1_ref[...] = x1
    _ffn_pre(x1, mod_ref, row, g2_ref, rwt_ref, rb_ref, hx2_ref, hi_ref, lo_ref, gate_out_ref, tab_ref, tabt_ref)


def _sgu(x, mod, g, w_in, ln_g, ln_b, w_s, b_st, w_out, g2, rwt, rb, tiles_per_seq):
    n, d = x.shape
    w = w_out.shape[0]
    const = lambda i: (0, 0)
    out_specs, out_shape = _ffn_pre_specs(n, d)
    return pl.pallas_call(
        functools.partial(_sgu_kernel, tiles_per_seq=tiles_per_seq),
        grid=(n // MIX_TILE,),
        in_specs=[pl.BlockSpec((MIX_TILE, d), lambda i: (i, 0)),
                  pl.BlockSpec(mod.shape, const),
                  pl.BlockSpec((1, d), const),
                  pl.BlockSpec(w_in.shape, const, pipeline_mode=pl.Buffered(1)),
                  pl.BlockSpec((1, w), const),
                  pl.BlockSpec((1, w), const),
                  pl.BlockSpec(w_s.shape, lambda i: (0, 0, 0)),
                  pl.BlockSpec(b_st.shape, const),
                  pl.BlockSpec(w_out.shape, const, pipeline_mode=pl.Buffered(1)),
                  pl.BlockSpec((1, d), const),
                  pl.BlockSpec(rwt.shape, const),
                  pl.BlockSpec(rb.shape, const)],
        out_specs=out_specs,
        out_shape=out_shape,
        scratch_shapes=[pltpu.VMEM((MIX_TILE, w), BF16)],
        compiler_params=_params(("arbitrary",)),
        name="sgu",
    )(x, mod, g, w_in, ln_g, ln_b, w_s, b_st, w_out, g2, rwt, rb)


def _sorted_layout(m, eidx, gw):
    e, t = m.shape
    lanes = LANES
    r = lax.broadcasted_iota(jnp.int32, (t, t), 0)
    c = lax.broadcasted_iota(jnp.int32, (t, t), 1)
    upper = jnp.where(r <= c, 1.0, 0.0).astype(BF16)
    incl = jnp.dot(m.astype(BF16), upper, preferred_element_type=F32)
    run = _round_up_rows(incl[:, t - 1:t])
    re = lax.broadcasted_iota(jnp.int32, (e, e), 0)
    ce = lax.broadcasted_iota(jnp.int32, (e, e), 1)
    lower = jnp.where(ce < re, 1.0, 0.0).astype(BF16)
    tiles = jnp.broadcast_to(run * (1.0 / ROW_ALIGN), (e, lanes)).astype(BF16)
    lstart = jnp.dot(lower, tiles, preferred_element_type=F32)[:, 0:1] * float(ROW_ALIGN)
    pos = jnp.where(m > 0.0, lstart + incl - m, float(POS_RADIX * POS_RADIX - 1))
    hi = jnp.floor(pos * (1.0 / POS_RADIX))
    lo = pos - hi * float(POS_RADIX)
    iota_e = lax.broadcasted_iota(jnp.int32, (e, t), 0).astype(F32)
    gate = jnp.zeros((e, t), F32)
    for k in range(TOP_K):
        gate = jnp.where(iota_e == eidx[k:k + 1, :], gw[k:k + 1, :], gate)
    lane = lax.broadcasted_iota(jnp.int32, (e, lanes), 1)
    tab = jnp.where(lane == 0, run, jnp.where(lane == 2, lstart, 0.0))
    diag = lax.broadcasted_iota(jnp.int32, (e, lanes), 0) == lane
    row_of = lambda col: jnp.sum(jnp.where(diag, jnp.broadcast_to(col, (e, lanes)), 0.0), axis=0, keepdims=True)
    sub = lax.broadcasted_iota(jnp.int32, (SUBLANES, lanes), 0)
    tabt = jnp.where(sub == 0, row_of(lstart), jnp.where(sub == 1, row_of(lstart + run), 0.0))
    return hi.astype(BF16), lo.astype(BF16), gate.astype(BF16), tab, tabt


def _aligned(v):
    return pl.multiple_of(v, ROW_ALIGN)


def _run_copies(tab_ref, make_copy, unroll):
    def body(pair, carry):
        for priority in range(2):
            e = 2 * pair + priority
            make_copy(_aligned(tab_ref[TAB_LOCAL + e]), _aligned(tab_ref[N_EXPERTS + e]),
                      _aligned(tab_ref[e])).start(priority=priority)
        return carry

    lax.fori_loop(0, N_EXPERTS // 2, body, 0, unroll=unroll)


def _dispatch_kernel(tab_ref, ztab_ref, tabt_ref, hi_ref, lo_ref, x_ref, xs_hbm, sbuf, zbuf, sems, zsem):
    i = pl.program_id(0)
    slot = i % 2
    t = x_ref.shape[0]
    rows = sbuf.shape[1]

    def zero_copy(e):
        n = _aligned(ztab_ref[e])
        return pltpu.make_async_copy(zbuf.at[pl.ds(0, n)], xs_hbm.at[pl.ds(_aligned(ztab_ref[N_EXPERTS + e]), n)], zsem)

    def for_zero_runs(fn):
        def body(e, c):
            @pl.when(ztab_ref[e] > 0)
            def _():
                fn(zero_copy(e))
            return c
        lax.fori_loop(0, N_EXPERTS, body, 0)

        def tail(b, c):
            fn(pltpu.make_async_copy(zbuf, xs_hbm.at[pl.ds(pl.multiple_of(b * EXPERT_BLOCK, EXPERT_BLOCK),
                                                           EXPERT_BLOCK)], zsem))
            return c
        lax.fori_loop(ztab_ref[2 * N_EXPERTS], xs_hbm.shape[0] // EXPERT_BLOCK, tail, 0)

    @pl.when(i == 0)
    def _():
        zbuf[...] = jnp.zeros_like(zbuf)
        for_zero_runs(lambda cp: cp.start())

    xb = x_ref[...].astype(BF16)
    total = tab_ref[2 * N_EXPERTS]
    run_lo = tabt_ref[0, 0:1, 0:N_EXPERTS]
    run_hi = tabt_ref[0, 1:2, 0:N_EXPERTS]

    def permute(r0):
        j_e = (lax.broadcasted_iota(jnp.int32, (PERM_CHUNK, N_EXPERTS), 0) + r0).astype(F32)
        owner = jnp.where((j_e >= run_lo) & (j_e < run_hi), 1.0, 0.0).astype(BF16)
        pos = (jnp.dot(owner, hi_ref[...], preferred_element_type=F32) * float(POS_RADIX)
               + jnp.dot(owner, lo_ref[...], preferred_element_type=F32))
        j_t = (lax.broadcasted_iota(jnp.int32, (PERM_CHUNK, t), 0) + r0).astype(F32)
        p = jnp.where(pos == j_t, 1.0, 0.0).astype(BF16)
        sbuf[slot, r0:r0 + PERM_CHUNK, :] = jnp.dot(p, xb, preferred_element_type=F32).astype(BF16)

    for r0 in range(0, rows, PERM_CHUNK):
        if r0 < t * TOP_K + PERM_CHUNK:
            permute(r0)
        else:
            pl.when(total > r0)(functools.partial(permute, r0))

    _run_copies(tab_ref, lambda loc, glob, n: pltpu.make_async_copy(
        sbuf.at[slot, pl.ds(loc, n)], xs_hbm.at[pl.ds(glob, n)], sems.at[slot]), unroll=True)

    def wait_rows(s, n):
        pltpu.make_async_copy(sbuf.at[s, pl.ds(0, n)], xs_hbm.at[pl.ds(0, n)], sems.at[s]).wait()

    @pl.when(i > 0)
    def _():
        wait_rows(1 - slot, _aligned(tab_ref[2 * N_EXPERTS + 1]))

    @pl.when(i == pl.num_programs(0) - 1)
    def _():
        wait_rows(slot, _aligned(tab_ref[2 * N_EXPERTS]))

    @pl.when(i == 0)
    def _():
        for_zero_runs(lambda cp: cp.wait())


def _dispatch(tab, ztab, tab_t, pos_hi, pos_lo, hx2, xs_rows):
    n, d = hx2.shape
    col = lambda i: (0, i)
    return pl.pallas_call(
        _dispatch_kernel,
        grid=(n // TOKEN_TILE,),
        in_specs=[pl.BlockSpec((TAB_WIDTH,), lambda i: (i,), memory_space=pltpu.SMEM),
                  pl.BlockSpec(memory_space=pltpu.SMEM),
                  pl.BlockSpec((1, SUBLANES, LANES), lambda i: (i, 0, 0)),
                  pl.BlockSpec((N_EXPERTS, TOKEN_TILE), col),
                  pl.BlockSpec((N_EXPERTS, TOKEN_TILE), col),
                  pl.BlockSpec((TOKEN_TILE, d), lambda i: (i, 0))],
        out_specs=pl.BlockSpec(memory_space=pl.ANY),
        out_shape=jax.ShapeDtypeStruct((xs_rows, d), BF16),
        scratch_shapes=[pltpu.VMEM((2, SORTED_ROWS, d), BF16),
                        pltpu.VMEM((EXPERT_BLOCK, d), BF16),
                        pltpu.SemaphoreType.DMA((2,)),
                        pltpu.SemaphoreType.DMA],
        compiler_params=_params(("arbitrary",)),
        name="moe_dispatch",
    )(tab, ztab, tab_t, pos_hi, pos_lo, hx2)


def _experts_kernel(be_ref, nb_ref, xs_ref, wgu_ref, wd_ref, ys_ref, wgu_b, wd_b):
    i = pl.program_id(0)
    ff = wd_b.shape[0]
    used = i < nb_ref[0]
    new_expert = (i == 0) | (be_ref[i] != be_ref[jnp.maximum(i - 1, 0)])

    @pl.when(used & new_expert)
    def _():
        wgu_b[...] = wgu_ref[0, 0].astype(BF16)
        wd_b[...] = wd_ref[0, 0].astype(BF16)

    @pl.when(used)
    def _():
        starts = range(0, xs_ref.shape[0], MIX_SUB)
        hs = [jnp.dot(xs_ref[r0:r0 + MIX_SUB, :], wgu_b[...], preferred_element_type=F32) for r0 in starts]
        for r0, h in zip(starts, hs):
            a = _silu(h[:, :ff]) * h[:, ff:]
            ys_ref[r0:r0 + MIX_SUB, :] = jnp.dot(a.astype(BF16), wd_b[...],
                                                 preferred_element_type=F32).astype(BF16)

    @pl.when(jnp.logical_not(used))
    def _():
        ys_ref[...] = jnp.zeros_like(ys_ref)


def _experts(block_e, n_used, xs, w_gu, w_down, layer):
    rows, d = xs.shape
    nb = rows // EXPERT_BLOCK
    blk = lambda i, be, nu: (jnp.maximum(jnp.minimum(i, nu[0] - 1), 0), 0)
    wmap = lambda i, be, nu: (layer, be[i], 0, 0)
    return pl.pallas_call(
        _experts_kernel,
        grid_spec=pltpu.PrefetchScalarGridSpec(
            num_scalar_prefetch=2,
            grid=(nb,),
            in_specs=[pl.BlockSpec((EXPERT_BLOCK, d), blk),
                      pl.BlockSpec((1, 1) + w_gu.shape[2:], wmap),
                      pl.BlockSpec((1, 1) + w_down.shape[2:], wmap)],
            out_specs=pl.BlockSpec((EXPERT_BLOCK, d), lambda i, be, nu: (i, 0)),
            scratch_shapes=[pltpu.VMEM(w_gu.shape[2:], BF16), pltpu.VMEM(w_down.shape[2:], BF16)]),
        out_shape=jax.ShapeDtypeStruct((rows, d), BF16),
        compiler_params=_params(("arbitrary",)),
        name="moe_experts",
    )(block_e, n_used, xs, w_gu, w_down)


def _combine_kernel(tab_ref, tabn_ref, tabv_ref, hi_ref, lo_ref, gate_ref, ys_hbm, hx2_ref, x1_ref, mod_ref,
                    wsgu_ref, wsd_ref, fg_ref, out_ref, ybuf, sems, *, tiles_per_seq, final_norm):
    t_rows, d = x1_ref.shape
    ff = wsd_ref.shape[0]
    rows = ybuf.shape[1]
    i = pl.program_id(0)
    last = pl.num_programs(0) - 1
    slot = i % 2
    row = i // tiles_per_seq

    def gather(table, s, unroll):
        _run_copies(table, lambda loc, glob, n: pltpu.make_async_copy(
            ys_hbm.at[pl.ds(glob, n)], ybuf.at[s, pl.ds(loc, n)], sems.at[s]), unroll=unroll)

    def wait_rows(table, s):
        n = _aligned(table[2 * N_EXPERTS])
        pltpu.make_async_copy(ys_hbm.at[pl.ds(0, n)], ybuf.at[s, pl.ds(0, n)], sems.at[s]).wait()

    @pl.when(i == 0)
    def _():
        ybuf[...] = jnp.zeros_like(ybuf)
        gather(tab_ref, 0, False)

    gather(tabn_ref, 1 - slot, True)

    hs = jnp.dot(hx2_ref[...].astype(BF16), wsgu_ref[...], preferred_element_type=F32)
    shared = jnp.dot((_silu(hs[:, :ff]) * hs[:, ff:]).astype(BF16), wsd_ref[...], preferred_element_type=F32)

    run_lo = tabv_ref[0, :, 2:3]
    run_hi = run_lo + tabv_ref[0, :, 0:1]
    j_e = lax.broadcasted_iota(jnp.int32, (N_EXPERTS, rows), 1).astype(F32)
    owner = jnp.where((j_e >= run_lo) & (j_e < run_hi), 1.0, 0.0).astype(BF16)
    pos = (jnp.dot(hi_ref[...], owner, preferred_element_type=F32) * float(POS_RADIX)
           + jnp.dot(lo_ref[...], owner, preferred_element_type=F32))
    gates = jnp.dot(gate_ref[...], owner, preferred_element_type=F32)
    j_t = lax.broadcasted_iota(jnp.int32, (t_rows, rows), 1).astype(F32)
    gb = jnp.where(pos == j_t, gates, 0.0).astype(BF16)

    wait_rows(tab_ref, slot)
    routed = jnp.dot(gb, ybuf[slot], preferred_element_type=F32)
    x2 = x1_ref[...] + _mod_chunk(mod_ref, row, 5, d) * (routed + shared)
    if final_norm:
        x2 = x2 * lax.rsqrt(jnp.mean(x2 * x2, axis=-1, keepdims=True) + NORM_EPS) * fg_ref[...]
    out_ref[...] = x2

    @pl.when(i == last)
    def _():
        wait_rows(tabn_ref, 1 - slot)


def _combine(tab, tab_v, pos_hi, pos_lo, gate, ys, hx2, x1, mod, ws_gu, ws_down, fg, tiles_per_seq, final_norm):
    n, d = x1.shape
    row = lambda i: (i, 0)
    const = lambda i: (0, 0)
    return pl.pallas_call(
        functools.partial(_combine_kernel, tiles_per_seq=tiles_per_seq, final_norm=final_norm),
        grid=(n // TOKEN_TILE,),
        in_specs=[pl.BlockSpec((TAB_WIDTH,), lambda i: (i,), memory_space=pltpu.SMEM),
                  pl.BlockSpec((TAB_WIDTH,), lambda i: (jnp.minimum(i + 1, n // TOKEN_TILE - 1),),
                               memory_space=pltpu.SMEM),
                  pl.BlockSpec((1, N_EXPERTS, LANES), lambda i: (i, 0, 0)),
                  pl.BlockSpec((TOKEN_TILE, N_EXPERTS), row),
                  pl.BlockSpec((TOKEN_TILE, N_EXPERTS), row),
                  pl.BlockSpec((TOKEN_TILE, N_EXPERTS), row),
                  pl.BlockSpec(memory_space=pl.ANY),
                  pl.BlockSpec((TOKEN_TILE, d), row),
                  pl.BlockSpec((TOKEN_TILE, d), row),
                  pl.BlockSpec(mod.shape, const),
                  pl.BlockSpec(ws_gu.shape, const),
                  pl.BlockSpec(ws_down.shape, const),
                  pl.BlockSpec((1, d), const)],
        out_specs=pl.BlockSpec((TOKEN_TILE, d), row),
        out_shape=jax.ShapeDtypeStruct((n, d), F32),
        scratch_shapes=[pltpu.VMEM((2, SORTED_ROWS, d), BF16), pltpu.SemaphoreType.DMA((2,))],
        compiler_params=_params(("arbitrary",)),
        name="moe_combine",
    )(tab, tab, tab_v, pos_hi, pos_lo, gate, ys, hx2, x1, mod, ws_gu, ws_down, fg)


def _moe(x1, hx2, pos_hi, pos_lo, gate, tab_f, tab_t, mod, w_gu, w_down, layer, ws_gu, ws_down, fg,
         tiles_per_seq, final_norm):
    n, d = x1.shape
    n_tiles = n // TOKEN_TILE
    run_len = tab_f[:, :, 0].astype(jnp.int32)
    counts = jnp.sum(run_len, axis=0)
    padded = (counts + EXPERT_BLOCK - 1) // EXPERT_BLOCK * EXPERT_BLOCK
    pad_end = jnp.cumsum(padded)
    pad_start = pad_end - padded
    max_rows = n * TOP_K + n_tiles * N_EXPERTS * ROW_ALIGN + N_EXPERTS * (EXPERT_BLOCK - ROW_ALIGN)
    n_blocks = (max_rows + EXPERT_BLOCK - 1) // EXPERT_BLOCK
    n_used = (pad_end[-1:] // EXPERT_BLOCK).astype(jnp.int32)
    block_start = jnp.arange(n_blocks, dtype=jnp.int32) * EXPERT_BLOCK
    block_e = jnp.minimum(jnp.sum((pad_end[None, :] <= block_start[:, None]).astype(jnp.int32), axis=1),
                          N_EXPERTS - 1)

    run_start = pad_start[None, :] + jnp.cumsum(run_len, axis=0) - run_len
    total = jnp.sum(run_len, axis=1, keepdims=True)
    prev_total = jnp.concatenate([jnp.zeros((1, 1), jnp.int32), total[:-1]], axis=0)
    run_local = tab_f[:, :, 2].astype(jnp.int32)
    fill = jnp.zeros((n_tiles, TAB_WIDTH - 3 * N_EXPERTS - 2), jnp.int32)
    tab = jnp.concatenate([run_len, run_start, total, prev_total, run_local, fill], axis=1).reshape(-1)
    ztab = jnp.concatenate([padded - counts, pad_start + counts, n_used])

    xs = _dispatch(tab, ztab, tab_t, pos_hi, pos_lo, hx2, n_blocks * EXPERT_BLOCK)
    ys = _experts(block_e, n_used, xs, w_gu, w_down, layer)
    return _combine(tab, tab_f, pos_hi.T, pos_lo.T, gate.T, ys, hx2, x1, mod, ws_gu, ws_down, fg,
                    tiles_per_seq, final_norm)


def _sincos_2d(rows, d):
    quarter = d // 4
    omega = 1.0 / (POS_BASE ** (jnp.arange(quarter, dtype=F32) / quarter))

    def emb(n):
        p = jnp.arange(n, dtype=F32)[:, None] * omega[None, :]
        return jnp.concatenate([jnp.sin(p), jnp.cos(p)], axis=-1)

    er, ec = emb(rows), emb(GRID_W)
    pe = jnp.concatenate([jnp.broadcast_to(er[:, None, :], (rows, GRID_W, d // 2)),
                          jnp.broadcast_to(ec[None, :, :], (rows, GRID_W, d // 2))], axis=-1)
    return pe.reshape(rows * GRID_W, d)


def kernel(x, c, ctx, c_ctx, ada_w, ada_b, mix_norm_g, ffn_norm_g, a_w_in, a_conv_w, a_conv_b, a_gate_r_w, a_gate_r_b, a_gate_i_w, a_gate_i_b, a_lambda, a_w_out, b_w_in, b_ln_g, b_ln_b, b_w_s, b_b_s, b_w_out, router_w, router_b, moe_w_gu, moe_w_down, shared_w_gu, shared_w_down, final_norm_g):
    bsz, s, d = x.shape
    ctx_len = ctx.shape[1]
    depth = ada_w.shape[0]
    assert depth == 2 and bsz < MOD_ROWS and s % MIX_TILE == 0 and MIX_TILE % TOKEN_TILE == 0
    assert ctx_len % TOKEN_TILE == 0
    n = bsz * s
    tps = s // TOKEN_TILE
    ctx_row = bsz

    cc = jnp.zeros((MOD_ROWS, d), F32).at[:bsz].set(c).at[ctx_row].set(c_ctx)
    mod = _modulation(cc, ada_w, ada_b)
    pe = _sincos_2d(s // GRID_W, d)
    rc = a_w_in.shape[2] // 2

    w_in0 = a_w_in[0].astype(BF16)
    g_mix0 = mix_norm_g[0].reshape(1, d)
    x0, gate, ux = _rglru_in(x.reshape(n, d), pe, mod[0], g_mix0, w_in0)
    uc = _ctx_in(ctx.reshape(bsz * ctx_len, d), mod[0], g_mix0, w_in0[:, rc:], ctx_row)
    w_ri = jnp.concatenate([a_gate_r_w[0], a_gate_i_w[0]], axis=-1).astype(BF16)
    conv = (a_conv_w[0], a_conv_b[0])
    gates = [(w_ri[k], a_gate_r_b[0, k], a_gate_i_b[0, k], a_lambda[0, k]) for k in range(2)]
    h_zero = jnp.zeros((bsz, rc), F32)
    h_fwd, uc = _lru_scan(uc, conv, *gates[0], h_zero, reverse=False, reset_first=True, emit_y=False)
    h_rev = _lru_scan(uc, None, *gates[1], h_zero, reverse=True, reset_first=True, emit_y=False)
    y_fwd, ux = _lru_scan(ux, conv, *gates[0], h_fwd, reverse=False, reset_first=False, emit_y=True)
    y_rev = _lru_scan(ux, None, *gates[1], h_rev, reverse=True, reset_first=False, emit_y=True)
    pre = _rglru_out(y_fwd, y_rev, gate, x0, mod[0], a_w_out[0].astype(BF16), ffn_norm_g[0].reshape(1, d),
                     router_w[0].T, router_b[0].reshape(N_EXPERTS, 1), s // MIX_TILE)
    x1 = _moe(*pre, mod[0], moe_w_gu, moe_w_down, 0, shared_w_gu[0].astype(BF16),
              shared_w_down[0].astype(BF16), final_norm_g.reshape(1, d), tps, False)

    pre = _sgu(x1, mod[1], mix_norm_g[1].reshape(1, d), b_w_in[0].astype(BF16),
               b_ln_g[0].reshape(1, -1), b_ln_b[0].reshape(1, -1), b_w_s[0].astype(BF16), b_b_s[0].T,
               b_w_out[0].astype(BF16), ffn_norm_g[1].reshape(1, d),
               router_w[1].T, router_b[1].reshape(N_EXPERTS, 1), s // MIX_TILE)
    out = _moe(*pre, mod[1], moe_w_gu, moe_w_down, 1, shared_w_gu[1].astype(BF16),
               shared_w_down[1].astype(BF16), final_norm_g.reshape(1, d), tps, True)
    return out.reshape(bsz, s, d)
```

```python
import functools

import jax
import jax.numpy as jnp
from jax import lax
from jax.experimental import pallas as pl
from jax.experimental.pallas import tpu as pltpu

F32 = jnp.float32
BF16 = jnp.bfloat16
HIGHEST = lax.Precision.HIGHEST

GRID_W = 64
N_MOD = 6
NORM_EPS = 1e-6
POS_BASE = 10000.0
RNN_HEADS = 5
CONV_WIDTH = 4
CONV_PAD_LEFT = 2
LRU_C = 8.0
SGU_HEADS = 8
CHUNK = 128
N_EXPERTS = 64
TOP_K = 8
N_GROUPS = 8
TOPK_GROUPS = 4
EXPERTS_PER_GROUP = N_EXPERTS // N_GROUPS
ROUTED_SCALE = 2.5

SUBLANES = 8
ROW_ALIGN = 16
LANES = 128
MOD_ROWS = 8
TOKEN_TILE = 256
MIX_TILE = 512
MIX_SUB = 512
EXPERT_BLOCK = 1024
SORTED_ROWS = TOKEN_TILE * TOP_K + N_EXPERTS * ROW_ALIGN
PERM_CHUNK = 512
TAB_WIDTH = 256
TAB_LOCAL = 2 * N_EXPERTS + 2
POS_RADIX = 64
VMEM_LIMIT = 56 * 1024 * 1024


def _params(semantics, vmem=VMEM_LIMIT):
    return pltpu.CompilerParams(dimension_semantics=semantics, vmem_limit_bytes=vmem)


def _silu(x):
    return x * jax.nn.sigmoid(x)


def _rms_mod(x, g, sc, sh):
    y = x * lax.rsqrt(jnp.mean(x * x, axis=-1, keepdims=True) + NORM_EPS)
    return (y * g) * (1.0 + sc) + sh


def _mod_chunk(mod_ref, row, k, d):
    return mod_ref[pl.ds(row, 1), k * d:(k + 1) * d]


def _round_up_rows(count):
    return jnp.maximum(jnp.ceil(count * (1.0 / ROW_ALIGN)), 1.0) * float(ROW_ALIGN)


def _mod_kernel(cc_ref, w_ref, b_ref, o_ref):
    s = _silu(cc_ref[...])
    o_ref[0] = jnp.dot(s, w_ref[0], preferred_element_type=F32, precision=HIGHEST) + b_ref[0]


def _modulation(cc, ada_w, ada_b):
    depth, d, nd = ada_w.shape
    return pl.pallas_call(
        _mod_kernel,
        grid=(depth, nd // d),
        in_specs=[pl.BlockSpec((MOD_ROWS, d), lambda l, j: (0, 0)),
                  pl.BlockSpec((1, d, d), lambda l, j: (l, 0, j)),
                  pl.BlockSpec((1, 1, d), lambda l, j: (l, 0, j))],
        out_specs=pl.BlockSpec((1, MOD_ROWS, d), lambda l, j: (l, 0, j)),
        out_shape=jax.ShapeDtypeStruct((depth, MOD_ROWS, nd), F32),
        compiler_params=_params(("arbitrary", "arbitrary")),
        name="modulation",
    )(cc, ada_w, ada_b.reshape(depth, 1, nd))


def _rglru_in_kernel(x_ref, pe_ref, mod_ref, g_ref, w_ref, x0_ref, gate_ref, u_ref, *, tiles_per_seq):
    d = x_ref.shape[1]
    c = u_ref.shape[1]
    row = pl.program_id(0) // tiles_per_seq
    x = x_ref[...] + pe_ref[...]
    hx = _rms_mod(x, g_ref[...], _mod_chunk(mod_ref, row, 1, d), _mod_chunk(mod_ref, row, 0, d))
    z = jnp.dot(hx.astype(BF16), w_ref[...], preferred_element_type=F32)
    x0_ref[...] = x
    gate_ref[...] = jax.nn.gelu(z[:, :c]).astype(BF16)
    u_ref[...] = z[:, c:]


def _rglru_in(x2, pe, mod, g, w_in):
    n, d = x2.shape
    c = w_in.shape[1] // 2
    s = pe.shape[0]
    tps = s // MIX_TILE
    row = lambda i: (i, 0)
    return pl.pallas_call(
        functools.partial(_rglru_in_kernel, tiles_per_seq=tps),
        grid=(n // MIX_TILE,),
        in_specs=[pl.BlockSpec((MIX_TILE, d), row),
                  pl.BlockSpec((MIX_TILE, d), lambda i: (i % tps, 0)),
                  pl.BlockSpec(mod.shape, lambda i: (0, 0)),
                  pl.BlockSpec((1, d), lambda i: (0, 0)),
                  pl.BlockSpec(w_in.shape, lambda i: (0, 0))],
        out_specs=[pl.BlockSpec((MIX_TILE, d), row),
                   pl.BlockSpec((MIX_TILE, c), row),
                   pl.BlockSpec((MIX_TILE, c), row)],
        out_shape=[jax.ShapeDtypeStruct((n, d), F32),
                   jax.ShapeDtypeStruct((n, c), BF16),
                   jax.ShapeDtypeStruct((n, c), F32)],
        compiler_params=_params(("arbitrary",)),
        name="rglru_in",
    )(x2, pe, mod, g, w_in)


def _ctx_in_kernel(x_ref, mod_ref, g_ref, w_ref, u_ref, *, ctx_row):
    d = x_ref.shape[1]
    hx = _rms_mod(x_ref[...], g_ref[...], _mod_chunk(mod_ref, ctx_row, 1, d),
                  _mod_chunk(mod_ref, ctx_row, 0, d))
    u_ref[...] = jnp.dot(hx.astype(BF16), w_ref[...], preferred_element_type=F32)


def _ctx_in(c2, mod, g, w_u, ctx_row):
    n, d = c2.shape
    c = w_u.shape[1]
    return pl.pallas_call(
        functools.partial(_ctx_in_kernel, ctx_row=ctx_row),
        grid=(n // TOKEN_TILE,),
        in_specs=[pl.BlockSpec((TOKEN_TILE, d), lambda i: (i, 0)),
                  pl.BlockSpec(mod.shape, lambda i: (0, 0)),
                  pl.BlockSpec((1, d), lambda i: (0, 0)),
                  pl.BlockSpec(w_u.shape, lambda i: (0, 0))],
        out_specs=pl.BlockSpec((TOKEN_TILE, c), lambda i: (i, 0)),
        out_shape=jax.ShapeDtypeStruct((n, c), F32),
        compiler_params=_params(("arbitrary",)),
        name="ctx_in",
    )(c2, mod, g, w_u)


def _log_sigmoid(x):
    return jnp.minimum(x, 0.0) - jnp.log1p(jnp.exp(-jnp.abs(x)))


def _lru_scan_kernel(*refs, n_tiles, reverse, conv_done, reset_first, emit_y):
    if conv_done:
        u_ref, wri_ref, rb_ref, ib_ref, lam_ref, h0_ref, out_ref, a_scr, b_scr, h_scr = refs
    else:
        (u_ref, up_ref, un_ref, cw_ref, cb_ref, wri_ref, rb_ref, ib_ref, lam_ref, h0_ref,
         out_ref, uc_ref, ubuf, a_scr, b_scr, h_scr) = refs
    t_rows, c = u_ref.shape
    hb = c // RNN_HEADS
    b = pl.program_id(0)
    j = pl.program_id(1)
    jj = n_tiles - 1 - j if reverse else j

    if conv_done:
        u = u_ref[...]
    else:
        ubuf[SUBLANES:SUBLANES + t_rows, :] = u_ref[...]
        ubuf[0:SUBLANES, :] = jnp.where(jj == 0, 0.0, up_ref[...])
        ubuf[SUBLANES + t_rows:, :] = jnp.where(jj == n_tiles - 1, 0.0, un_ref[...])
        u = cb_ref[...]
        for k in range(CONV_WIDTH):
            u = u + cw_ref[k:k + 1, :] * ubuf[pl.ds(SUBLANES - CONV_PAD_LEFT + k, t_rows), :]
        uc_ref[...] = u

    log_lam = LRU_C * _log_sigmoid(lam_ref[...])
    rows = lax.broadcasted_iota(jnp.int32, (t_rows, 1), 0)
    first_row = jnp.where(j == 0, t_rows - 1 if reverse else 0, -1)
    for h in range(RNN_HEADS):
        sl = slice(h * hb, (h + 1) * hb)
        uh = u[:, sl]
        z = jnp.dot(uh.astype(BF16), wri_ref[h], preferred_element_type=F32)
        r = jax.nn.sigmoid(z[:, :hb] + rb_ref[:, sl])
        ig = jax.nn.sigmoid(z[:, hb:] + ib_ref[:, sl])
        log_a = r * log_lam[:, sl]
        a = jnp.exp(log_a)
        mult = jnp.sqrt((1.0 - a) * (1.0 + a))
        if reset_first:
            mult = jnp.where(rows == first_row, 1.0, mult)
        a_scr[:, sl] = a
        b_scr[:, sl] = mult * ig * uh

    @pl.when(j == 0)
    def _():
        h_scr[...] = h0_ref[pl.ds(b, 1), :]

    n_groups = t_rows // SUBLANES

    def group(g, h):
        base = pl.multiple_of((n_groups - 1 - g if reverse else g) * SUBLANES, SUBLANES)
        for s in range(SUBLANES):
            r = base + (SUBLANES - 1 - s if reverse else s)
            h = a_scr[pl.ds(r, 1), :] * h + b_scr[pl.ds(r, 1), :]
            if emit_y:
                out_ref[pl.ds(r, 1), :] = h
        return h

    h = lax.fori_loop(0, n_groups, group, h_scr[...])
    h_scr[...] = h
    if not emit_y:
        @pl.when(j == n_tiles - 1)
        def _():
            out_ref[pl.ds(b, 1), :] = h


def _lru_scan(u, conv, w_ri, r_b, i_b, lam, h0, *, reverse, reset_first, emit_y):
    n, c = u.shape
    n_batch = h0.shape[0]
    n_tiles = n // n_batch // TOKEN_TILE
    sub = TOKEN_TILE // SUBLANES
    n_sub = n // SUBLANES
    const = lambda b, j: (0, 0)

    def tile(b, j):
        return b * n_tiles + (n_tiles - 1 - j if reverse else j)

    tile_spec = pl.BlockSpec((TOKEN_TILE, c), lambda b, j: (tile(b, j), 0))
    in_specs = [tile_spec]
    args = [u]
    scratch = []
    if conv is not None:
        conv_w, conv_b = conv
        in_specs += [pl.BlockSpec((SUBLANES, c), lambda b, j: (jnp.maximum(tile(b, j) * sub - 1, 0), 0)),
                     pl.BlockSpec((SUBLANES, c), lambda b, j: (jnp.minimum((tile(b, j) + 1) * sub, n_sub - 1), 0)),
                     pl.BlockSpec(conv_w.shape, const),
                     pl.BlockSpec((1, c), const)]
        args += [u, u, conv_w, conv_b.reshape(1, c)]
        scratch = [pltpu.VMEM((TOKEN_TILE + 2 * SUBLANES, c), F32)]
    in_specs += [pl.BlockSpec(w_ri.shape, lambda b, j: (0, 0, 0)),
                 pl.BlockSpec((1, c), const), pl.BlockSpec((1, c), const), pl.BlockSpec((1, c), const),
                 pl.BlockSpec(h0.shape, const)]
    args += [w_ri, r_b.reshape(1, c), i_b.reshape(1, c), lam.reshape(1, c), h0]
    if emit_y:
        out_specs = [tile_spec]
        out_shape = [jax.ShapeDtypeStruct((n, c), F32)]
    else:
        out_specs = [pl.BlockSpec(h0.shape, const)]
        out_shape = [jax.ShapeDtypeStruct(h0.shape, F32)]
    if conv is not None:
        out_specs.append(tile_spec)
        out_shape.append(jax.ShapeDtypeStruct((n, c), F32))
    name = ("lru_scan" if emit_y else "lru_ctx") + ("_rev" if reverse else "_fwd")
    outs = pl.pallas_call(
        functools.partial(_lru_scan_kernel, n_tiles=n_tiles, reverse=reverse, conv_done=conv is None,
                          reset_first=reset_first, emit_y=emit_y),
        grid=(n_batch, n_tiles),
        in_specs=in_specs,
        out_specs=out_specs,
        out_shape=out_shape,
        scratch_shapes=scratch + [pltpu.VMEM((TOKEN_TILE, c), F32),
                                  pltpu.VMEM((TOKEN_TILE, c), F32),
                                  pltpu.VMEM((1, c), F32)],
        compiler_params=_params(("arbitrary", "arbitrary")),
        name=name,
    )(*args)
    return outs if conv is not None else outs[0]


def _route(logits, rb):
    e, t = logits.shape
    neg = -jnp.inf
    scores = jax.nn.sigmoid(logits)
    sel = scores + rb
    iota_g = lax.broadcasted_iota(jnp.int32, (N_GROUPS, t), 0).astype(F32)
    iota_e = lax.broadcasted_iota(jnp.int32, (e, t), 0).astype(F32)

    gs = jnp.full((N_GROUPS, t), neg, F32)
    for g in range(N_GROUPS):
        sg = sel[g * EXPERTS_PER_GROUP:(g + 1) * EXPERTS_PER_GROUP, :]
        m1 = jnp.max(sg, axis=0, keepdims=True)
        i1 = jnp.min(jnp.where(sg == m1, iota_g, float(EXPERTS_PER_GROUP)), axis=0, keepdims=True)
        m2 = jnp.max(jnp.where(iota_g == i1, neg, sg), axis=0, keepdims=True)
        gs = jnp.where(iota_g == float(g), m1 + m2, gs)

    keep = jnp.zeros((N_GROUPS, t), F32)
    for _ in range(TOPK_GROUPS):
        m = jnp.max(gs, axis=0, keepdims=True)
        idx = jnp.min(jnp.where(gs == m, iota_g, float(N_GROUPS)), axis=0, keepdims=True)
        hit = iota_g == idx
        keep = jnp.where(hit, 1.0, keep)
        gs = jnp.where(hit, neg, gs)

    masked = jnp.concatenate(
        [jnp.where(keep[g:g + 1, :] > 0.0, sel[g * EXPERTS_PER_GROUP:(g + 1) * EXPERTS_PER_GROUP, :], neg)
         for g in range(N_GROUPS)], axis=0)

    iota_k = lax.broadcasted_iota(jnp.int32, (TOP_K, t), 0)
    selmask = jnp.zeros((e, t), F32)
    eidx = jnp.zeros((TOP_K, t), F32)
    gw = jnp.zeros((TOP_K, t), F32)
    for k in range(TOP_K):
        m = jnp.max(masked, axis=0, keepdims=True)
        idx = jnp.min(jnp.where(masked == m, iota_e, float(e)), axis=0, keepdims=True)
        hit = iota_e == idx
        gk = jnp.sum(jnp.where(hit, scores, 0.0), axis=0, keepdims=True)
        masked = jnp.where(hit, neg, masked)
        selmask = jnp.where(hit, 1.0, selmask)
        eidx = jnp.where(iota_k == k, idx, eidx)
        gw = jnp.where(iota_k == k, gk, gw)
    gw = gw / jnp.sum(gw, axis=0, keepdims=True) * ROUTED_SCALE
    return eidx, gw, selmask


def _ffn_pre(x1, mod_ref, row, g2_ref, rwt_ref, rb_ref, hx2_ref, hi_ref, lo_ref, gate_ref, tab_ref, tabt_ref):
    d = x1.shape[1]
    hx2 = _rms_mod(x1, g2_ref[...], _mod_chunk(mod_ref, row, 4, d), _mod_chunk(mod_ref, row, 3, d))
    hx2_ref[...] = hx2
    logits = lax.dot_general(rwt_ref[...], hx2, (((1,), (1,)), ((), ())),
                             preferred_element_type=F32, precision=HIGHEST)
    eidx, gw, selmask = _route(logits, rb_ref[...])
    for s in range(selmask.shape[1] // TOKEN_TILE):
        ts = slice(s * TOKEN_TILE, (s + 1) * TOKEN_TILE)
        hi, lo, gate, tab, tabt = _sorted_layout(selmask[:, ts], eidx[:, ts], gw[:, ts])
        hi_ref[:, ts] = hi
        lo_ref[:, ts] = lo
        gate_ref[:, ts] = gate
        tab_ref[s] = tab
        tabt_ref[s] = tabt


def _ffn_pre_specs(n, d):
    row = lambda i: (i, 0)
    col = lambda i: (0, i)
    sub = MIX_TILE // TOKEN_TILE
    lead = lambda i: (i, 0, 0)
    out_specs = [pl.BlockSpec((MIX_TILE, d), row),
                 pl.BlockSpec((MIX_TILE, d), row),
                 pl.BlockSpec((N_EXPERTS, MIX_TILE), col),
                 pl.BlockSpec((N_EXPERTS, MIX_TILE), col),
                 pl.BlockSpec((N_EXPERTS, MIX_TILE), col),
                 pl.BlockSpec((sub, N_EXPERTS, LANES), lead),
                 pl.BlockSpec((sub, SUBLANES, LANES), lead)]
    out_shape = [jax.ShapeDtypeStruct((n, d), F32),
                 jax.ShapeDtypeStruct((n, d), F32),
                 jax.ShapeDtypeStruct((N_EXPERTS, n), BF16),
                 jax.ShapeDtypeStruct((N_EXPERTS, n), BF16),
                 jax.ShapeDtypeStruct((N_EXPERTS, n), BF16),
                 jax.ShapeDtypeStruct((n // TOKEN_TILE, N_EXPERTS, LANES), F32),
                 jax.ShapeDtypeStruct((n // TOKEN_TILE, SUBLANES, LANES), F32)]
    return out_specs, out_shape


def _rglru_out_kernel(yf_ref, yr_ref, gate_ref, x0_ref, mod_ref, wout_ref, g2_ref, rwt_ref, rb_ref,
                      x1_ref, hx2_ref, hi_ref, lo_ref, gate_out_ref, tab_ref, tabt_ref, *, tiles_per_seq):
    d = x0_ref.shape[1]
    row = pl.program_id(0) // tiles_per_seq
    yx = yf_ref[...] + yr_ref[...]
    v = gate_ref[...].astype(F32) * yx
    out = jnp.dot(v.astype(BF16), wout_ref[...], preferred_element_type=F32)
    x1 = x0_ref[...] + _mod_chunk(mod_ref, row, 2, d) * out
    x1_ref[...] = x1
    _ffn_pre(x1, mod_ref, row, g2_ref, rwt_ref, rb_ref, hx2_ref, hi_ref, lo_ref, gate_out_ref, tab_ref, tabt_ref)


def _rglru_out(y_fwd, y_rev, gate, x0, mod, w_out, g2, rwt, rb, tiles_per_seq):
    n, d = x0.shape
    c = gate.shape[1]
    row = lambda i: (i, 0)
    const = lambda i: (0, 0)
    out_specs, out_shape = _ffn_pre_specs(n, d)
    return pl.pallas_call(
        functools.partial(_rglru_out_kernel, tiles_per_seq=tiles_per_seq),
        grid=(n // MIX_TILE,),
        in_specs=[pl.BlockSpec((MIX_TILE, c), row),
                  pl.BlockSpec((MIX_TILE, c), row),
                  pl.BlockSpec((MIX_TILE, c), row),
                  pl.BlockSpec((MIX_TILE, d), row),
                  pl.BlockSpec(mod.shape, const),
                  pl.BlockSpec(w_out.shape, const),
                  pl.BlockSpec((1, d), const),
                  pl.BlockSpec(rwt.shape, const),
                  pl.BlockSpec(rb.shape, const)],
        out_specs=out_specs,
        out_shape=out_shape,
        compiler_params=_params(("arbitrary",)),
        name="rglru_out",
    )(y_fwd, y_rev, gate, x0, mod, w_out, g2, rwt, rb)


def _sgu_kernel(x_ref, mod_ref, g_ref, win_ref, lng_ref, lnb_ref, ws_ref, bst_ref, wout_ref,
                g2_ref, rwt_ref, rb_ref,
                x1_ref, hx2_ref, hi_ref, lo_ref, gate_out_ref, tab_ref, tabt_ref, m_scr, *, tiles_per_seq):
    t_rows, d = x_ref.shape
    w = wout_ref.shape[0]
    gd = w // SGU_HEADS
    row = pl.program_id(0) // tiles_per_seq
    x = x_ref[...]
    hx = _rms_mod(x, g_ref[...], _mod_chunk(mod_ref, row, 1, d), _mod_chunk(mod_ref, row, 0, d))
    z = jax.nn.gelu(jnp.dot(hx.astype(BF16), win_ref[...], preferred_element_type=F32))
    u = z[:, :w]
    v = z[:, w:]
    mu = jnp.mean(v, axis=-1, keepdims=True)
    vc = v - mu
    v = vc * lax.rsqrt(jnp.mean(vc * vc, axis=-1, keepdims=True) + NORM_EPS) * lng_ref[...] + lnb_ref[...]
    vb = v.astype(BF16)
    for ch in range(t_rows // CHUNK):
        rs = slice(ch * CHUNK, (ch + 1) * CHUNK)
        for g in range(SGU_HEADS):
            cs = slice(g * gd, (g + 1) * gd)
            sv = jnp.dot(ws_ref[g], vb[rs, cs], preferred_element_type=F32) + bst_ref[:, g:g + 1]
            m_scr[rs, cs] = (u[rs, cs] * sv).astype(BF16)
    out = jnp.dot(m_scr[...], wout_ref[...], preferred_element_type=F32)
    x1 = x + _mod_chunk(mod_ref, row, 2, d) * out
    x1_ref[...] = x1
    _ffn_pre(x1, mod_ref, row, g2_ref, rwt_ref, rb_ref, hx2_ref, hi_ref, lo_ref, gate_out_ref, tab_ref, tabt_ref)


def _sgu(x, mod, g, w_in, ln_g, ln_b, w_s, b_st, w_out, g2, rwt, rb, tiles_per_seq):
    n, d = x.shape
    w = w_out.shape[0]
    const = lambda i: (0, 0)
    out_specs, out_shape = _ffn_pre_specs(n, d)
    return pl.pallas_call(
        functools.partial(_sgu_kernel, tiles_per_seq=tiles_per_seq),
        grid=(n // MIX_TILE,),
        in_specs=[pl.BlockSpec((MIX_TILE, d), lambda i: (i, 0)),
                  pl.BlockSpec(mod.shape, const),
                  pl.BlockSpec((1, d), const),
                  pl.BlockSpec(w_in.shape, const, pipeline_mode=pl.Buffered(1)),
                  pl.BlockSpec((1, w), const),
                  pl.BlockSpec((1, w), const),
                  pl.BlockSpec(w_s.shape, lambda i: (0, 0, 0)),
                  pl.BlockSpec(b_st.shape, const),
                  pl.BlockSpec(w_out.shape, const, pipeline_mode=pl.Buffered(1)),
                  pl.BlockSpec((1, d), const),
                  pl.BlockSpec(rwt.shape, const),
                  pl.BlockSpec(rb.shape, const)],
        out_specs=out_specs,
        out_shape=out_shape,
        scratch_shapes=[pltpu.VMEM((MIX_TILE, w), BF16)],
        compiler_params=_params(("arbitrary",)),
        name="sgu",
    )(x, mod, g, w_in, ln_g, ln_b, w_s, b_st, w_out, g2, rwt, rb)


def _sorted_layout(m, eidx, gw):
    e, t = m.shape
    lanes = LANES
    r = lax.broadcasted_iota(jnp.int32, (t, t), 0)
    c = lax.broadcasted_iota(jnp.int32, (t, t), 1)
    upper = jnp.where(r <= c, 1.0, 0.0).astype(BF16)
    incl = jnp.dot(m.astype(BF16), upper, preferred_element_type=F32)
    run = _round_up_rows(incl[:, t - 1:t])
    re = lax.broadcasted_iota(jnp.int32, (e, e), 0)
    ce = lax.broadcasted_iota(jnp.int32, (e, e), 1)
    lower = jnp.where(ce < re, 1.0, 0.0).astype(BF16)
    tiles = jnp.broadcast_to(run * (1.0 / ROW_ALIGN), (e, lanes)).astype(BF16)
    lstart = jnp.dot(lower, tiles, preferred_element_type=F32)[:, 0:1] * float(ROW_ALIGN)
    pos = jnp.where(m > 0.0, lstart + incl - m, float(POS_RADIX * POS_RADIX - 1))
    hi = jnp.floor(pos * (1.0 / POS_RADIX))
    lo = pos - hi * float(POS_RADIX)
    iota_e = lax.broadcasted_iota(jnp.int32, (e, t), 0).astype(F32)
    gate = jnp.zeros((e, t), F32)
    for k in range(TOP_K):
        gate = jnp.where(iota_e == eidx[k:k + 1, :], gw[k:k + 1, :], gate)
    lane = lax.broadcasted_iota(jnp.int32, (e, lanes), 1)
    tab = jnp.where(lane == 0, run, jnp.where(lane == 2, lstart, 0.0))
    diag = lax.broadcasted_iota(jnp.int32, (e, lanes), 0) == lane
    row_of = lambda col: jnp.sum(jnp.where(diag, jnp.broadcast_to(col, (e, lanes)), 0.0), axis=0, keepdims=True)
    sub = lax.broadcasted_iota(jnp.int32, (SUBLANES, lanes), 0)
    tabt = jnp.where(sub == 0, row_of(lstart), jnp.where(sub == 1, row_of(lstart + run), 0.0))
    return hi.astype(BF16), lo.astype(BF16), gate.astype(BF16), tab, tabt


def _aligned(v):
    return pl.multiple_of(v, ROW_ALIGN)


def _run_copies(tab_ref, make_copy, unroll):
    def body(pair, carry):
        for priority in range(2):
            e = 2 * pair + priority
            make_copy(_aligned(tab_ref[TAB_LOCAL + e]), _aligned(tab_ref[N_EXPERTS + e]),
                      _aligned(tab_ref[e])).start(priority=priority)
        return carry

    lax.fori_loop(0, N_EXPERTS // 2, body, 0, unroll=unroll)


def _dispatch_kernel(tab_ref, ztab_ref, tabt_ref, hi_ref, lo_ref, x_ref, xs_hbm, sbuf, zbuf, sems, zsem):
    i = pl.program_id(0)
    slot = i % 2
    t = x_ref.shape[0]
    rows = sbuf.shape[1]

    def zero_copy(e):
        n = _aligned(ztab_ref[e])
        return pltpu.make_async_copy(zbuf.at[pl.ds(0, n)], xs_hbm.at[pl.ds(_aligned(ztab_ref[N_EXPERTS + e]), n)], zsem)

    def for_zero_runs(fn):
        def body(e, c):
            @pl.when(ztab_ref[e] > 0)
            def _():
                fn(zero_copy(e))
            return c
        lax.fori_loop(0, N_EXPERTS, body, 0)

        def tail(b, c):
            fn(pltpu.make_async_copy(zbuf, xs_hbm.at[pl.ds(pl.multiple_of(b * EXPERT_BLOCK, EXPERT_BLOCK),
                                                           EXPERT_BLOCK)], zsem))
            return c
        lax.fori_loop(ztab_ref[2 * N_EXPERTS], xs_hbm.shape[0] // EXPERT_BLOCK, tail, 0)

    @pl.when(i == 0)
    def _():
        zbuf[...] = jnp.zeros_like(zbuf)
        for_zero_runs(lambda cp: cp.start())

    xb = x_ref[...].astype(BF16)
    total = tab_ref[2 * N_EXPERTS]
    run_lo = tabt_ref[0, 0:1, 0:N_EXPERTS]
    run_hi = tabt_ref[0, 1:2, 0:N_EXPERTS]

    def permute(r0):
        j_e = (lax.broadcasted_iota(jnp.int32, (PERM_CHUNK, N_EXPERTS), 0) + r0).astype(F32)
        owner = jnp.where((j_e >= run_lo) & (j_e < run_hi), 1.0, 0.0).astype(BF16)
        pos = (jnp.dot(owner, hi_ref[...], preferred_element_type=F32) * float(POS_RADIX)
               + jnp.dot(owner, lo_ref[...], preferred_element_type=F32))
        j_t = (lax.broadcasted_iota(jnp.int32, (PERM_CHUNK, t), 0) + r0).astype(F32)
        p = jnp.where(pos == j_t, 1.0, 0.0).astype(BF16)
        sbuf[slot, r0:r0 + PERM_CHUNK, :] = jnp.dot(p, xb, preferred_element_type=F32).astype(BF16)

    for r0 in range(0, rows, PERM_CHUNK):
        if r0 < t * TOP_K + PERM_CHUNK:
            permute(r0)
        else:
            pl.when(total > r0)(functools.partial(permute, r0))

    _run_copies(tab_ref, lambda loc, glob, n: pltpu.make_async_copy(
        sbuf.at[slot, pl.ds(loc, n)], xs_hbm.at[pl.ds(glob, n)], sems.at[slot]), unroll=True)

    def wait_rows(s, n):
        pltpu.make_async_copy(sbuf.at[s, pl.ds(0, n)], xs_hbm.at[pl.ds(0, n)], sems.at[s]).wait()

    @pl.when(i > 0)
    def _():
        wait_rows(1 - slot, _aligned(tab_ref[2 * N_EXPERTS + 1]))

    @pl.when(i == pl.num_programs(0) - 1)
    def _():
        wait_rows(slot, _aligned(tab_ref[2 * N_EXPERTS]))

    @pl.when(i == 0)
    def _():
        for_zero_runs(lambda cp: cp.wait())


def _dispatch(tab, ztab, tab_t, pos_hi, pos_lo, hx2, xs_rows):
    n, d = hx2.shape
    col = lambda i: (0, i)
    return pl.pallas_call(
        _dispatch_kernel,
        grid=(n // TOKEN_TILE,),
        in_specs=[pl.BlockSpec((TAB_WIDTH,), lambda i: (i,), memory_space=pltpu.SMEM),
                  pl.BlockSpec(memory_space=pltpu.SMEM),
                  pl.BlockSpec((1, SUBLANES, LANES), lambda i: (i, 0, 0)),
                  pl.BlockSpec((N_EXPERTS, TOKEN_TILE), col),
                  pl.BlockSpec((N_EXPERTS, TOKEN_TILE), col),
                  pl.BlockSpec((TOKEN_TILE, d), lambda i: (i, 0))],
        out_specs=pl.BlockSpec(memory_space=pl.ANY),
        out_shape=jax.ShapeDtypeStruct((xs_rows, d), BF16),
        scratch_shapes=[pltpu.VMEM((2, SORTED_ROWS, d), BF16),
                        pltpu.VMEM((EXPERT_BLOCK, d), BF16),
                        pltpu.SemaphoreType.DMA((2,)),
                        pltpu.SemaphoreType.DMA],
        compiler_params=_params(("arbitrary",)),
        name="moe_dispatch",
    )(tab, ztab, tab_t, pos_hi, pos_lo, hx2)


def _experts_kernel(be_ref, nb_ref, xs_ref, wgu_ref, wd_ref, ys_ref, wgu_b, wd_b):
    i = pl.program_id(0)
    ff = wd_b.shape[0]
    used = i < nb_ref[0]
    new_expert = (i == 0) | (be_ref[i] != be_ref[jnp.maximum(i - 1, 0)])

    @pl.when(used & new_expert)
    def _():
        wgu_b[...] = wgu_ref[0, 0].astype(BF16)
        wd_b[...] = wd_ref[0, 0].astype(BF16)

    @pl.when(used)
    def _():
        starts = range(0, xs_ref.shape[0], MIX_SUB)
        hs = [jnp.dot(xs_ref[r0:r0 + MIX_SUB, :], wgu_b[...], preferred_element_type=F32) for r0 in starts]
        for r0, h in zip(starts, hs):
            a = _silu(h[:, :ff]) * h[:, ff:]
            ys_ref[r0:r0 + MIX_SUB, :] = jnp.dot(a.astype(BF16), wd_b[...],
                                                 preferred_element_type=F32).astype(BF16)

    @pl.when(jnp.logical_not(used))
    def _():
        ys_ref[...] = jnp.zeros_like(ys_ref)


def _experts(block_e, n_used, xs, w_gu, w_down, layer):
    rows, d = xs.shape
    nb = rows // EXPERT_BLOCK
    blk = lambda i, be, nu: (jnp.maximum(jnp.minimum(i, nu[0] - 1), 0), 0)
    wmap = lambda i, be, nu: (layer, be[i], 0, 0)
    return pl.pallas_call(
        _experts_kernel,
        grid_spec=pltpu.PrefetchScalarGridSpec(
            num_scalar_prefetch=2,
            grid=(nb,),
            in_specs=[pl.BlockSpec((EXPERT_BLOCK, d), blk),
                      pl.BlockSpec((1, 1) + w_gu.shape[2:], wmap),
                      pl.BlockSpec((1, 1) + w_down.shape[2:], wmap)],
            out_specs=pl.BlockSpec((EXPERT_BLOCK, d), lambda i, be, nu: (i, 0)),
            scratch_shapes=[pltpu.VMEM(w_gu.shape[2:], BF16), pltpu.VMEM(w_down.shape[2:], BF16)]),
        out_shape=jax.ShapeDtypeStruct((rows, d), BF16),
        compiler_params=_params(("arbitrary",)),
        name="moe_experts",
    )(block_e, n_used, xs, w_gu, w_down)


def _combine_kernel(tab_ref, tabn_ref, tabv_ref, hi_ref, lo_ref, gate_ref, ys_hbm, hx2_ref, x1_ref, mod_ref,
                    wsgu_ref, wsd_ref, fg_ref, out_ref, ybuf, sems, *, tiles_per_seq, final_norm):
    t_rows, d = x1_ref.shape
    ff = wsd_ref.shape[0]
    rows = ybuf.shape[1]
    i = pl.program_id(0)
    last = pl.num_programs(0) - 1
    slot = i % 2
    row = i // tiles_per_seq

    def gather(table, s, unroll):
        _run_copies(table, lambda loc, glob, n: pltpu.make_async_copy(
            ys_hbm.at[pl.ds(glob, n)], ybuf.at[s, pl.ds(loc, n)], sems.at[s]), unroll=unroll)

    def wait_rows(table, s):
        n = _aligned(table[2 * N_EXPERTS])
        pltpu.make_async_copy(ys_hbm.at[pl.ds(0, n)], ybuf.at[s, pl.ds(0, n)], sems.at[s]).wait()

    @pl.when(i == 0)
    def _():
        ybuf[...] = jnp.zeros_like(ybuf)
        gather(tab_ref, 0, False)

    gather(tabn_ref, 1 - slot, True)

    hs = jnp.dot(hx2_ref[...].astype(BF16), wsgu_ref[...], preferred_element_type=F32)
    shared = jnp.dot((_silu(hs[:, :ff]) * hs[:, ff:]).astype(BF16), wsd_ref[...], preferred_element_type=F32)

    run_lo = tabv_ref[0, :, 2:3]
    run_hi = run_lo + tabv_ref[0, :, 0:1]
    j_e = lax.broadcasted_iota(jnp.int32, (N_EXPERTS, rows), 1).astype(F32)
    owner = jnp.where((j_e >= run_lo) & (j_e < run_hi), 1.0, 0.0).astype(BF16)
    pos = (jnp.dot(hi_ref[...], owner, preferred_element_type=F32) * float(POS_RADIX)
           + jnp.dot(lo_ref[...], owner, preferred_element_type=F32))
    gates = jnp.dot(gate_ref[...], owner, preferred_element_type=F32)
    j_t = lax.broadcasted_iota(jnp.int32, (t_rows, rows), 1).astype(F32)
    gb = jnp.where(pos == j_t, gates, 0.0).astype(BF16)

    wait_rows(tab_ref, slot)
    routed = jnp.dot(gb, ybuf[slot], preferred_element_type=F32)
    x2 = x1_ref[...] + _mod_chunk(mod_ref, row, 5, d) * (routed + shared)
    if final_norm:
        x2 = x2 * lax.rsqrt(jnp.mean(x2 * x2, axis=-1, keepdims=True) + NORM_EPS) * fg_ref[...]
    out_ref[...] = x2

    @pl.when(i == last)
    def _():
        wait_rows(tabn_ref, 1 - slot)


def _combine(tab, tab_v, pos_hi, pos_lo, gate, ys, hx2, x1, mod, ws_gu, ws_down, fg, tiles_per_seq, final_norm):
    n, d = x1.shape
    row = lambda i: (i, 0)
    const = lambda i: (0, 0)
    return pl.pallas_call(
        functools.partial(_combine_kernel, tiles_per_seq=tiles_per_seq, final_norm=final_norm),
        grid=(n // TOKEN_TILE,),
        in_specs=[pl.BlockSpec((TAB_WIDTH,), lambda i: (i,), memory_space=pltpu.SMEM),
                  pl.BlockSpec((TAB_WIDTH,), lambda i: (jnp.minimum(i + 1, n // TOKEN_TILE - 1),),
                               memory_space=pltpu.SMEM),
                  pl.BlockSpec((1, N_EXPERTS, LANES), lambda i: (i, 0, 0)),
                  pl.BlockSpec((TOKEN_TILE, N_EXPERTS), row),
                  pl.BlockSpec((TOKEN_TILE, N_EXPERTS), row),
                  pl.BlockSpec((TOKEN_TILE, N_EXPERTS), row),
                  pl.BlockSpec(memory_space=pl.ANY),
                  pl.BlockSpec((TOKEN_TILE, d), row),
                  pl.BlockSpec((TOKEN_TILE, d), row),
                  pl.BlockSpec(mod.shape, const),
                  pl.BlockSpec(ws_gu.shape, const),
                  pl.BlockSpec(ws_down.shape, const),
                  pl.BlockSpec((1, d), const)],
        out_specs=pl.BlockSpec((TOKEN_TILE, d), row),
        out_shape=jax.ShapeDtypeStruct((n, d), F32),
        scratch_shapes=[pltpu.VMEM((2, SORTED_ROWS, d), BF16), pltpu.SemaphoreType.DMA((2,))],
        compiler_params=_params(("arbitrary",)),
        name="moe_combine",
    )(tab, tab, tab_v, pos_hi, pos_lo, gate, ys, hx2, x1, mod, ws_gu, ws_down, fg)


def _moe(x1, hx2, pos_hi, pos_lo, gate, tab_f, tab_t, mod, w_gu, w_down, layer, ws_gu, ws_down, fg,
         tiles_per_seq, final_norm):
    n, d = x1.shape
    n_tiles = n // TOKEN_TILE
    run_len = tab_f[:, :, 0].astype(jnp.int32)
    counts = jnp.sum(run_len, axis=0)
    padded = (counts + EXPERT_BLOCK - 1) // EXPERT_BLOCK * EXPERT_BLOCK
    pad_end = jnp.cumsum(padded)
    pad_start = pad_end - padded
    max_rows = n * TOP_K + n_tiles * N_EXPERTS * ROW_ALIGN + N_EXPERTS * (EXPERT_BLOCK - ROW_ALIGN)
    n_blocks = (max_rows + EXPERT_BLOCK - 1) // EXPERT_BLOCK
    n_used = (pad_end[-1:] // EXPERT_BLOCK).astype(jnp.int32)
    block_start = jnp.arange(n_blocks, dtype=jnp.int32) * EXPERT_BLOCK
    block_e = jnp.minimum(jnp.sum((pad_end[None, :] <= block_start[:, None]).astype(jnp.int32), axis=1),
                          N_EXPERTS - 1)

    run_start = pad_start[None, :] + jnp.cumsum(run_len, axis=0) - run_len
    total = jnp.sum(run_len, axis=1, keepdims=True)
    prev_total = jnp.concatenate([jnp.zeros((1, 1), jnp.int32), total[:-1]], axis=0)
    run_local = tab_f[:, :, 2].astype(jnp.int32)
    fill = jnp.zeros((n_tiles, TAB_WIDTH - 3 * N_EXPERTS - 2), jnp.int32)
    tab = jnp.concatenate([run_len, run_start, total, prev_total, run_local, fill], axis=1).reshape(-1)
    ztab = jnp.concatenate([padded - counts, pad_start + counts, n_used])

    xs = _dispatch(tab, ztab, tab_t, pos_hi, pos_lo, hx2, n_blocks * EXPERT_BLOCK)
    ys = _experts(block_e, n_used, xs, w_gu, w_down, layer)
    return _combine(tab, tab_f, pos_hi.T, pos_lo.T, gate.T, ys, hx2, x1, mod, ws_gu, ws_down, fg,
                    tiles_per_seq, final_norm)


def _sincos_2d(rows, d):
    quarter = d // 4
    omega = 1.0 / (POS_BASE ** (jnp.arange(quarter, dtype=F32) / quarter))

    def emb(n):
        p = jnp.arange(n, dtype=F32)[:, None] * omega[None, :]
        return jnp.concatenate([jnp.sin(p), jnp.cos(p)], axis=-1)

    er, ec = emb(rows), emb(GRID_W)
    pe = jnp.concatenate([jnp.broadcast_to(er[:, None, :], (rows, GRID_W, d // 2)),
                          jnp.broadcast_to(ec[None, :, :], (rows, GRID_W, d // 2))], axis=-1)
    return pe.reshape(rows * GRID_W, d)


def kernel(x, c, ctx, c_ctx, ada_w, ada_b, mix_norm_g, ffn_norm_g, a_w_in, a_conv_w, a_conv_b, a_gate_r_w, a_gate_r_b, a_gate_i_w, a_gate_i_b, a_lambda, a_w_out, b_w_in, b_ln_g, b_ln_b, b_w_s, b_b_s, b_w_out, router_w, router_b, moe_w_gu, moe_w_down, shared_w_gu, shared_w_down, final_norm_g):
    bsz, s, d = x.shape
    ctx_len = ctx.shape[1]
    depth = ada_w.shape[0]
    assert depth == 2 and bsz < MOD_ROWS and s % MIX_TILE == 0 and MIX_TILE % TOKEN_TILE == 0
    assert ctx_len % TOKEN_TILE == 0
    n = bsz * s
    tps = s // TOKEN_TILE
    ctx_row = bsz

    cc = jnp.zeros((MOD_ROWS, d), F32).at[:bsz].set(c).at[ctx_row].set(c_ctx)
    mod = _modulation(cc, ada_w, ada_b)
    pe = _sincos_2d(s // GRID_W, d)
    rc = a_w_in.shape[2] // 2

    w_in0 = a_w_in[0].astype(BF16)
    g_mix0 = mix_norm_g[0].reshape(1, d)
    x0, gate, ux = _rglru_in(x.reshape(n, d), pe, mod[0], g_mix0, w_in0)
    uc = _ctx_in(ctx.reshape(bsz * ctx_len, d), mod[0], g_mix0, w_in0[:, rc:], ctx_row)
    w_ri = jnp.concatenate([a_gate_r_w[0], a_gate_i_w[0]], axis=-1).astype(BF16)
    conv = (a_conv_w[0], a_conv_b[0])
    gates = [(w_ri[k], a_gate_r_b[0, k], a_gate_i_b[0, k], a_lambda[0, k]) for k in range(2)]
    h_zero = jnp.zeros((bsz, rc), F32)
    h_fwd, uc = _lru_scan(uc, conv, *gates[0], h_zero, reverse=False, reset_first=True, emit_y=False)
    h_rev = _lru_scan(uc, None, *gates[1], h_zero, reverse=True, reset_first=True, emit_y=False)
    y_fwd, ux = _lru_scan(ux, conv, *gates[0], h_fwd, reverse=False, reset_first=False, emit_y=True)
    y_rev = _lru_scan(ux, None, *gates[1], h_rev, reverse=True, reset_first=False, emit_y=True)
    pre = _rglru_out(y_fwd, y_rev, gate, x0, mod[0], a_w_out[0].astype(BF16), ffn_norm_g[0].reshape(1, d),
                     router_w[0].T, router_b[0].reshape(N_EXPERTS, 1), s // MIX_TILE)
    x1 = _moe(*pre, mod[0], moe_w_gu, moe_w_down, 0, shared_w_gu[0].astype(BF16),
              shared_w_down[0].astype(BF16), final_norm_g.reshape(1, d), tps, False)

    pre = _sgu(x1, mod[1], mix_norm_g[1].reshape(1, d), b_w_in[0].astype(BF16),
               b_ln_g[0].reshape(1, -1), b_ln_b[0].reshape(1, -1), b_w_s[0].astype(BF16), b_b_s[0].T,
               b_w_out[0].astype(BF16), ffn_norm_g[1].reshape(1, d),
               router_w[1].T, router_b[1].reshape(N_EXPERTS, 1), s // MIX_TILE)
    out = _moe(*pre, mod[1], moe_w_gu, moe_w_down, 1, shared_w_gu[1].astype(BF16),
               shared_w_down[1].astype(BF16), final_norm_g.reshape(1, d), tps, True)
    return out.reshape(bsz, s, d)
```

```python
import functools

import jax
import jax.numpy as jnp
from jax import lax
from jax.experimental import pallas as pl
from jax.experimental.pallas import tpu as pltpu

F32 = jnp.float32
BF16 = jnp.bfloat16
HIGHEST = lax.Precision.HIGHEST

GRID_W = 64
N_MOD = 6
NORM_EPS = 1e-6
POS_BASE = 10000.0
RNN_HEADS = 5
CONV_WIDTH = 4
CONV_PAD_LEFT = 2
LRU_C = 8.0
SGU_HEADS = 8
CHUNK = 128
N_EXPERTS = 64
TOP_K = 8
N_GROUPS = 8
TOPK_GROUPS = 4
EXPERTS_PER_GROUP = N_EXPERTS // N_GROUPS
ROUTED_SCALE = 2.5

SUBLANES = 8
ROW_ALIGN = 16
LANES = 128
MOD_ROWS = 8
TOKEN_TILE = 256
MIX_TILE = 512
SCAN_TILE = 512
MIX_SUB = 512
EXPERT_BLOCK = 1024
SORTED_ROWS = TOKEN_TILE * TOP_K + N_EXPERTS * ROW_ALIGN
PERM_CHUNK = 512
TAB_WIDTH = 256
TAB_LOCAL = 2 * N_EXPERTS + 2
POS_RADIX = 64
VMEM_LIMIT = 56 * 1024 * 1024


def _params(semantics, vmem=VMEM_LIMIT):
    return pltpu.CompilerParams(dimension_semantics=semantics, vmem_limit_bytes=vmem)


def _silu(x):
    return x * jax.nn.sigmoid(x)


def _rms_mod(x, g, sc, sh):
    y = x * lax.rsqrt(jnp.mean(x * x, axis=-1, keepdims=True) + NORM_EPS)
    return (y * g) * (1.0 + sc) + sh


def _mod_chunk(mod_ref, row, k, d):
    return mod_ref[pl.ds(row, 1), k * d:(k + 1) * d]


def _round_up_rows(count):
    return jnp.maximum(jnp.ceil(count * (1.0 / ROW_ALIGN)), 1.0) * float(ROW_ALIGN)


def _mod_kernel(cc_ref, w_ref, b_ref, o_ref):
    s = _silu(cc_ref[...])
    o_ref[0] = jnp.dot(s, w_ref[0], preferred_element_type=F32, precision=HIGHEST) + b_ref[0]


def _modulation(cc, ada_w, ada_b):
    depth, d, nd = ada_w.shape
    return pl.pallas_call(
        _mod_kernel,
        grid=(depth, nd // d),
        in_specs=[pl.BlockSpec((MOD_ROWS, d), lambda l, j: (0, 0)),
                  pl.BlockSpec((1, d, d), lambda l, j: (l, 0, j)),
                  pl.BlockSpec((1, 1, d), lambda l, j: (l, 0, j))],
        out_specs=pl.BlockSpec((1, MOD_ROWS, d), lambda l, j: (l, 0, j)),
        out_shape=jax.ShapeDtypeStruct((depth, MOD_ROWS, nd), F32),
        compiler_params=_params(("arbitrary", "arbitrary")),
        name="modulation",
    )(cc, ada_w, ada_b.reshape(depth, 1, nd))


def _rglru_in_kernel(x_ref, pe_ref, mod_ref, g_ref, w_ref, x0_ref, gate_ref, u_ref, *, tiles_per_seq):
    d = x_ref.shape[1]
    c = u_ref.shape[1]
    row = pl.program_id(0) // tiles_per_seq
    x = x_ref[...] + pe_ref[...]
    hx = _rms_mod(x, g_ref[...], _mod_chunk(mod_ref, row, 1, d), _mod_chunk(mod_ref, row, 0, d))
    z = jnp.dot(hx.astype(BF16), w_ref[...], preferred_element_type=F32)
    x0_ref[...] = x
    gate_ref[...] = jax.nn.gelu(z[:, :c]).astype(BF16)
    u_ref[...] = z[:, c:]


def _rglru_in(x2, pe, mod, g, w_in):
    n, d = x2.shape
    c = w_in.shape[1] // 2
    s = pe.shape[0]
    tps = s // MIX_TILE
    row = lambda i: (i, 0)
    return pl.pallas_call(
        functools.partial(_rglru_in_kernel, tiles_per_seq=tps),
        grid=(n // MIX_TILE,),
        in_specs=[pl.BlockSpec((MIX_TILE, d), row),
                  pl.BlockSpec((MIX_TILE, d), lambda i: (i % tps, 0)),
                  pl.BlockSpec(mod.shape, lambda i: (0, 0)),
                  pl.BlockSpec((1, d), lambda i: (0, 0)),
                  pl.BlockSpec(w_in.shape, lambda i: (0, 0))],
        out_specs=[pl.BlockSpec((MIX_TILE, d), row),
                   pl.BlockSpec((MIX_TILE, c), row),
                   pl.BlockSpec((MIX_TILE, c), row)],
        out_shape=[jax.ShapeDtypeStruct((n, d), F32),
                   jax.ShapeDtypeStruct((n, c), BF16),
                   jax.ShapeDtypeStruct((n, c), F32)],
        compiler_params=_params(("arbitrary",)),
        name="rglru_in",
    )(x2, pe, mod, g, w_in)


def _ctx_in_kernel(x_ref, mod_ref, g_ref, w_ref, u_ref, *, ctx_row):
    d = x_ref.shape[1]
    hx = _rms_mod(x_ref[...], g_ref[...], _mod_chunk(mod_ref, ctx_row, 1, d),
                  _mod_chunk(mod_ref, ctx_row, 0, d))
    u_ref[...] = jnp.dot(hx.astype(BF16), w_ref[...], preferred_element_type=F32)


def _ctx_in(c2, mod, g, w_u, ctx_row):
    n, d = c2.shape
    c = w_u.shape[1]
    return pl.pallas_call(
        functools.partial(_ctx_in_kernel, ctx_row=ctx_row),
        grid=(n // TOKEN_TILE,),
        in_specs=[pl.BlockSpec((TOKEN_TILE, d), lambda i: (i, 0)),
                  pl.BlockSpec(mod.shape, lambda i: (0, 0)),
                  pl.BlockSpec((1, d), lambda i: (0, 0)),
                  pl.BlockSpec(w_u.shape, lambda i: (0, 0))],
        out_specs=pl.BlockSpec((TOKEN_TILE, c), lambda i: (i, 0)),
        out_shape=jax.ShapeDtypeStruct((n, c), F32),
        compiler_params=_params(("arbitrary",)),
        name="ctx_in",
    )(c2, mod, g, w_u)


def _log_sigmoid(x):
    return jnp.minimum(x, 0.0) - jnp.log1p(jnp.exp(-jnp.abs(x)))


def _lru_scan_kernel(*refs, n_tiles, reverse, conv_done, reset_first, emit_y):
    if conv_done:
        u_ref, wri_ref, rb_ref, ib_ref, lam_ref, h0_ref, out_ref, a_scr, b_scr, h_scr = refs
    else:
        (u_ref, up_ref, un_ref, cw_ref, cb_ref, wri_ref, rb_ref, ib_ref, lam_ref, h0_ref,
         out_ref, uc_ref, ubuf, a_scr, b_scr, h_scr) = refs
    t_rows, c = u_ref.shape
    hb = c // RNN_HEADS
    b = pl.program_id(0)
    j = pl.program_id(1)
    jj = n_tiles - 1 - j if reverse else j

    if conv_done:
        u = u_ref[...]
    else:
        ubuf[SUBLANES:SUBLANES + t_rows, :] = u_ref[...]
        ubuf[0:SUBLANES, :] = jnp.where(jj == 0, 0.0, up_ref[...])
        ubuf[SUBLANES + t_rows:, :] = jnp.where(jj == n_tiles - 1, 0.0, un_ref[...])
        u = cb_ref[...]
        for k in range(CONV_WIDTH):
            u = u + cw_ref[k:k + 1, :] * ubuf[pl.ds(SUBLANES - CONV_PAD_LEFT + k, t_rows), :]
        uc_ref[...] = u

    log_lam = LRU_C * _log_sigmoid(lam_ref[...])
    rows = lax.broadcasted_iota(jnp.int32, (t_rows, 1), 0)
    first_row = jnp.where(j == 0, t_rows - 1 if reverse else 0, -1)
    for h in range(RNN_HEADS):
        sl = slice(h * hb, (h + 1) * hb)
        uh = u[:, sl]
        z = jnp.dot(uh.astype(BF16), wri_ref[h], preferred_element_type=F32)
        r = jax.nn.sigmoid(z[:, :hb] + rb_ref[:, sl])
        ig = jax.nn.sigmoid(z[:, hb:] + ib_ref[:, sl])
        log_a = r * log_lam[:, sl]
        a = jnp.exp(log_a)
        mult = jnp.sqrt((1.0 - a) * (1.0 + a))
        if reset_first:
            mult = jnp.where(rows == first_row, 1.0, mult)
        a_scr[:, sl] = a
        b_scr[:, sl] = mult * ig * uh

    @pl.when(j == 0)
    def _():
        h_scr[...] = h0_ref[pl.ds(b, 1), :]

    n_groups = t_rows // SUBLANES

    def group(g, h):
        base = pl.multiple_of((n_groups - 1 - g if reverse else g) * SUBLANES, SUBLANES)
        for s in range(SUBLANES):
            r = base + (SUBLANES - 1 - s if reverse else s)
            h = a_scr[pl.ds(r, 1), :] * h + b_scr[pl.ds(r, 1), :]
            if emit_y:
                out_ref[pl.ds(r, 1), :] = h
        return h

    h = lax.fori_loop(0, n_groups, group, h_scr[...])
    h_scr[...] = h
    if not emit_y:
        @pl.when(j == n_tiles - 1)
        def _():
            out_ref[pl.ds(b, 1), :] = h


def _lru_scan(u, conv, w_ri, r_b, i_b, lam, h0, *, reverse, reset_first, emit_y):
    n, c = u.shape
    n_batch = h0.shape[0]
    rows = min(SCAN_TILE, n // n_batch)
    n_tiles = n // n_batch // rows
    sub = rows // SUBLANES
    n_sub = n // SUBLANES
    const = lambda b, j: (0, 0)

    def tile(b, j):
        return b * n_tiles + (n_tiles - 1 - j if reverse else j)

    tile_spec = pl.BlockSpec((rows, c), lambda b, j: (tile(b, j), 0))
    in_specs = [tile_spec]
    args = [u]
    scratch = []
    if conv is not None:
        conv_w, conv_b = conv
        in_specs += [pl.BlockSpec((SUBLANES, c), lambda b, j: (jnp.maximum(tile(b, j) * sub - 1, 0), 0)),
                     pl.BlockSpec((SUBLANES, c), lambda b, j: (jnp.minimum((tile(b, j) + 1) * sub, n_sub - 1), 0)),
                     pl.BlockSpec(conv_w.shape, const),
                     pl.BlockSpec((1, c), const)]
        args += [u, u, conv_w, conv_b.reshape(1, c)]
        scratch = [pltpu.VMEM((rows + 2 * SUBLANES, c), F32)]
    in_specs += [pl.BlockSpec(w_ri.shape, lambda b, j: (0, 0, 0)),
                 pl.BlockSpec((1, c), const), pl.BlockSpec((1, c), const), pl.BlockSpec((1, c), const),
                 pl.BlockSpec(h0.shape, const)]
    args += [w_ri, r_b.reshape(1, c), i_b.reshape(1, c), lam.reshape(1, c), h0]
    if emit_y:
        out_specs = [tile_spec]
        out_shape = [jax.ShapeDtypeStruct((n, c), F32)]
    else:
        out_specs = [pl.BlockSpec(h0.shape, const)]
        out_shape = [jax.ShapeDtypeStruct(h0.shape, F32)]
    if conv is not None:
        out_specs.append(tile_spec)
        out_shape.append(jax.ShapeDtypeStruct((n, c), F32))
    name = ("lru_scan" if emit_y else "lru_ctx") + ("_rev" if reverse else "_fwd")
    outs = pl.pallas_call(
        functools.partial(_lru_scan_kernel, n_tiles=n_tiles, reverse=reverse, conv_done=conv is None,
                          reset_first=reset_first, emit_y=emit_y),
        grid=(n_batch, n_tiles),
        in_specs=in_specs,
        out_specs=out_specs,
        out_shape=out_shape,
        scratch_shapes=scratch + [pltpu.VMEM((rows, c), F32),
                                  pltpu.VMEM((rows, c), F32),
                                  pltpu.VMEM((1, c), F32)],
        compiler_params=_params(("arbitrary", "arbitrary")),
        name=name,
    )(*args)
    return outs if conv is not None else outs[0]


def _route(logits, rb):
    e, t = logits.shape
    neg = -jnp.inf
    scores = jax.nn.sigmoid(logits)
    sel = scores + rb
    iota_g = lax.broadcasted_iota(jnp.int32, (N_GROUPS, t), 0).astype(F32)
    iota_e = lax.broadcasted_iota(jnp.int32, (e, t), 0).astype(F32)

    gs = jnp.full((N_GROUPS, t), neg, F32)
    for g in range(N_GROUPS):
        sg = sel[g * EXPERTS_PER_GROUP:(g + 1) * EXPERTS_PER_GROUP, :]
        m1 = jnp.max(sg, axis=0, keepdims=True)
        i1 = jnp.min(jnp.where(sg == m1, iota_g, float(EXPERTS_PER_GROUP)), axis=0, keepdims=True)
        m2 = jnp.max(jnp.where(iota_g == i1, neg, sg), axis=0, keepdims=True)
        gs = jnp.where(iota_g == float(g), m1 + m2, gs)

    keep = jnp.zeros((N_GROUPS, t), F32)
    for _ in range(TOPK_GROUPS):
        m = jnp.max(gs, axis=0, keepdims=True)
        idx = jnp.min(jnp.where(gs == m, iota_g, float(N_GROUPS)), axis=0, keepdims=True)
        hit = iota_g == idx
        keep = jnp.where(hit, 1.0, keep)
        gs = jnp.where(hit, neg, gs)

    masked = jnp.concatenate(
        [jnp.where(keep[g:g + 1, :] > 0.0, sel[g * EXPERTS_PER_GROUP:(g + 1) * EXPERTS_PER_GROUP, :], neg)
         for g in range(N_GROUPS)], axis=0)

    iota_k = lax.broadcasted_iota(jnp.int32, (TOP_K, t), 0)
    selmask = jnp.zeros((e, t), F32)
    eidx = jnp.zeros((TOP_K, t), F32)
    gw = jnp.zeros((TOP_K, t), F32)
    for k in range(TOP_K):
        m = jnp.max(masked, axis=0, keepdims=True)
        idx = jnp.min(jnp.where(masked == m, iota_e, float(e)), axis=0, keepdims=True)
        hit = iota_e == idx
        gk = jnp.sum(jnp.where(hit, scores, 0.0), axis=0, keepdims=True)
        masked = jnp.where(hit, neg, masked)
        selmask = jnp.where(hit, 1.0, selmask)
        eidx = jnp.where(iota_k == k, idx, eidx)
        gw = jnp.where(iota_k == k, gk, gw)
    gw = gw / jnp.sum(gw, axis=0, keepdims=True) * ROUTED_SCALE
    return eidx, gw, selmask


def _ffn_pre(x1, mod_ref, row, g2_ref, rwt_ref, rb_ref, hx2_ref, hi_ref, lo_ref, gate_ref, tab_ref, tabt_ref):
    d = x1.shape[1]
    hx2 = _rms_mod(x1, g2_ref[...], _mod_chunk(mod_ref, row, 4, d), _mod_chunk(mod_ref, row, 3, d))
    hx2_ref[...] = hx2
    logits = lax.dot_general(rwt_ref[...], hx2, (((1,), (1,)), ((), ())),
                             preferred_element_type=F32, precision=HIGHEST)
    eidx, gw, selmask = _route(logits, rb_ref[...])
    for s in range(selmask.shape[1] // TOKEN_TILE):
        ts = slice(s * TOKEN_TILE, (s + 1) * TOKEN_TILE)
        hi, lo, gate, tab, tabt = _sorted_layout(selmask[:, ts], eidx[:, ts], gw[:, ts])
        hi_ref[:, ts] = hi
        lo_ref[:, ts] = lo
        gate_ref[:, ts] = gate
        tab_ref[s] = tab
        tabt_ref[s] = tabt


def _ffn_pre_specs(n, d):
    row = lambda i: (i, 0)
    col = lambda i: (0, i)
    sub = MIX_TILE // TOKEN_TILE
    lead = lambda i: (i, 0, 0)
    out_specs = [pl.BlockSpec((MIX_TILE, d), row),
                 pl.BlockSpec((MIX_TILE, d), row),
                 pl.BlockSpec((N_EXPERTS, MIX_TILE), col),
                 pl.BlockSpec((N_EXPERTS, MIX_TILE), col),
                 pl.BlockSpec((N_EXPERTS, MIX_TILE), col),
                 pl.BlockSpec((sub, N_EXPERTS, LANES), lead),
                 pl.BlockSpec((sub, SUBLANES, LANES), lead)]
    out_shape = [jax.ShapeDtypeStruct((n, d), F32),
                 jax.ShapeDtypeStruct((n, d), F32),
                 jax.ShapeDtypeStruct((N_EXPERTS, n), BF16),
                 jax.ShapeDtypeStruct((N_EXPERTS, n), BF16),
                 jax.ShapeDtypeStruct((N_EXPERTS, n), BF16),
                 jax.ShapeDtypeStruct((n // TOKEN_TILE, N_EXPERTS, LANES), F32),
                 jax.ShapeDtypeStruct((n // TOKEN_TILE, SUBLANES, LANES), F32)]
    return out_specs, out_shape


def _rglru_out_kernel(yf_ref, yr_ref, gate_ref, x0_ref, mod_ref, wout_ref, g2_ref, rwt_ref, rb_ref,
                      x1_ref, hx2_ref, hi_ref, lo_ref, gate_out_ref, tab_ref, tabt_ref, *, tiles_per_seq):
    d = x0_ref.shape[1]
    row = pl.program_id(0) // tiles_per_seq
    yx = yf_ref[...] + yr_ref[...]
    v = gate_ref[...].astype(F32) * yx
    out = jnp.dot(v.astype(BF16), wout_ref[...], preferred_element_type=F32)
    x1 = x0_ref[...] + _mod_chunk(mod_ref, row, 2, d) * out
    x1_ref[...] = x1
    _ffn_pre(x1, mod_ref, row, g2_ref, rwt_ref, rb_ref, hx2_ref, hi_ref, lo_ref, gate_out_ref, tab_ref, tabt_ref)


def _rglru_out(y_fwd, y_rev, gate, x0, mod, w_out, g2, rwt, rb, tiles_per_seq):
    n, d = x0.shape
    c = gate.shape[1]
    row = lambda i: (i, 0)
    const = lambda i: (0, 0)
    out_specs, out_shape = _ffn_pre_specs(n, d)
    return pl.pallas_call(
        functools.partial(_rglru_out_kernel, tiles_per_seq=tiles_per_seq),
        grid=(n // MIX_TILE,),
        in_specs=[pl.BlockSpec((MIX_TILE, c), row),
                  pl.BlockSpec((MIX_TILE, c), row),
                  pl.BlockSpec((MIX_TILE, c), row),
                  pl.BlockSpec((MIX_TILE, d), row),
                  pl.BlockSpec(mod.shape, const),
                  pl.BlockSpec(w_out.shape, const),
                  pl.BlockSpec((1, d), const),
                  pl.BlockSpec(rwt.shape, const),
                  pl.BlockSpec(rb.shape, const)],
        out_specs=out_specs,
        out_shape=out_shape,
        compiler_params=_params(("arbitrary",)),
        name="rglru_out",
    )(y_fwd, y_rev, gate, x0, mod, w_out, g2, rwt, rb)


def _sgu_kernel(x_ref, mod_ref, g_ref, win_ref, lng_ref, lnb_ref, ws_ref, bst_ref, wout_ref,
                g2_ref, rwt_ref, rb_ref,
                x1_ref, hx2_ref, hi_ref, lo_ref, gate_out_ref, tab_ref, tabt_ref, m_scr, *, tiles_per_seq):
    t_rows, d = x_ref.shape
    w = wout_ref.shape[0]
    gd = w // SGU_HEADS
    row = pl.program_id(0) // tiles_per_seq
    x = x_ref[...]
    hx = _rms_mod(x, g_ref[...], _mod_chunk(mod_ref, row, 1, d), _mod_chunk(mod_ref, row, 0, d))
    z = jax.nn.gelu(jnp.dot(hx.astype(BF16), win_ref[...], preferred_element_type=F32))
    u = z[:, :w]
    v = z[:, w:]
    mu = jnp.mean(v, axis=-1, keepdims=True)
    vc = v - mu
    v = vc * lax.rsqrt(jnp.mean(vc * vc, axis=-1, keepdims=True) + NORM_EPS) * lng_ref[...] + lnb_ref[...]
    vb = v.astype(BF16)
    for ch in range(t_rows // CHUNK):
        rs = slice(ch * CHUNK, (ch + 1) * CHUNK)
        for g in range(SGU_HEADS):
            cs = slice(g * gd, (g + 1) * gd)
            sv = jnp.dot(ws_ref[g], vb[rs, cs], preferred_element_type=F32) + bst_ref[:, g:g + 1]
            m_scr[rs, cs] = (u[rs, cs] * sv).astype(BF16)
    out = jnp.dot(m_scr[...], wout_ref[...], preferred_element_type=F32)
    x1 = x + _mod_chunk(mod_ref, row, 2, d) * out
    x1_ref[...] = x1
    _ffn_pre(x1, mod_ref, row, g2_ref, rwt_ref, rb_ref, hx2_ref, hi_ref, lo_ref, gate_out_ref, tab_ref, tabt_ref)


def _sgu(x, mod, g, w_in, ln_g, ln_b, w_s, b_st, w_out, g2, rwt, rb, tiles_per_seq):
    n, d = x.shape
    w = w_out.shape[0]
    const = lambda i: (0, 0)
    out_specs, out_shape = _ffn_pre_specs(n, d)
    return pl.pallas_call(
        functools.partial(_sgu_kernel, tiles_per_seq=tiles_per_seq),
        grid=(n // MIX_TILE,),
        in_specs=[pl.BlockSpec((MIX_TILE, d), lambda i: (i, 0)),
                  pl.BlockSpec(mod.shape, const),
                  pl.BlockSpec((1, d), const),
                  pl.BlockSpec(w_in.shape, const, pipeline_mode=pl.Buffered(1)),
                  pl.BlockSpec((1, w), const),
                  pl.BlockSpec((1, w), const),
                  pl.BlockSpec(w_s.shape, lambda i: (0, 0, 0)),
                  pl.BlockSpec(b_st.shape, const),
                  pl.BlockSpec(w_out.shape, const, pipeline_mode=pl.Buffered(1)),
                  pl.BlockSpec((1, d), const),
                  pl.BlockSpec(rwt.shape, const),
                  pl.BlockSpec(rb.shape, const)],
        out_specs=out_specs,
        out_shape=out_shape,
        scratch_shapes=[pltpu.VMEM((MIX_TILE, w), BF16)],
        compiler_params=_params(("arbitrary",)),
        name="sgu",
    )(x, mod, g, w_in, ln_g, ln_b, w_s, b_st, w_out, g2, rwt, rb)


def _sorted_layout(m, eidx, gw):
    e, t = m.shape
    lanes = LANES
    r = lax.broadcasted_iota(jnp.int32, (t, t), 0)
    c = lax.broadcasted_iota(jnp.int32, (t, t), 1)
    upper = jnp.where(r <= c, 1.0, 0.0).astype(BF16)
    incl = jnp.dot(m.astype(BF16), upper, preferred_element_type=F32)
    run = _round_up_rows(incl[:, t - 1:t])
    re = lax.broadcasted_iota(jnp.int32, (e, e), 0)
    ce = lax.broadcasted_iota(jnp.int32, (e, e), 1)
    lower = jnp.where(ce < re, 1.0, 0.0).astype(BF16)
    tiles = jnp.broadcast_to(run * (1.0 / ROW_ALIGN), (e, lanes)).astype(BF16)
    lstart = jnp.dot(lower, tiles, preferred_element_type=F32)[:, 0:1] * float(ROW_ALIGN)
    pos = jnp.where(m > 0.0, lstart + incl - m, float(POS_RADIX * POS_RADIX - 1))
    hi = jnp.floor(pos * (1.0 / POS_RADIX))
    lo = pos - hi * float(POS_RADIX)
    iota_e = lax.broadcasted_iota(jnp.int32, (e, t), 0).astype(F32)
    gate = jnp.zeros((e, t), F32)
    for k in range(TOP_K):
        gate = jnp.where(iota_e == eidx[k:k + 1, :], gw[k:k + 1, :], gate)
    lane = lax.broadcasted_iota(jnp.int32, (e, lanes), 1)
    tab = jnp.where(lane == 0, run, jnp.where(lane == 2, lstart, 0.0))
    diag = lax.broadcasted_iota(jnp.int32, (e, lanes), 0) == lane
    row_of = lambda col: jnp.sum(jnp.where(diag, jnp.broadcast_to(col, (e, lanes)), 0.0), axis=0, keepdims=True)
    sub = lax.broadcasted_iota(jnp.int32, (SUBLANES, lanes), 0)
    tabt = jnp.where(sub == 0, row_of(lstart), jnp.where(sub == 1, row_of(lstart + run), 0.0))
    return hi.astype(BF16), lo.astype(BF16), gate.astype(BF16), tab, tabt


def _aligned(v):
    return pl.multiple_of(v, ROW_ALIGN)


def _run_copies(tab_ref, make_copy, unroll):
    def body(pair, carry):
        for priority in range(2):
            e = 2 * pair + priority
            make_copy(_aligned(tab_ref[TAB_LOCAL + e]), _aligned(tab_ref[N_EXPERTS + e]),
                      _aligned(tab_ref[e])).start(priority=priority)
        return carry

    lax.fori_loop(0, N_EXPERTS // 2, body, 0, unroll=unroll)


def _dispatch_kernel(tab_ref, ztab_ref, tabt_ref, hi_ref, lo_ref, x_ref, xs_hbm, sbuf, zbuf, sems, zsem):
    i = pl.program_id(0)
    slot = i % 2
    t = x_ref.shape[0]
    rows = sbuf.shape[1]

    def zero_copy(e):
        n = _aligned(ztab_ref[e])
        return pltpu.make_async_copy(zbuf.at[pl.ds(0, n)], xs_hbm.at[pl.ds(_aligned(ztab_ref[N_EXPERTS + e]), n)], zsem)

    def for_zero_runs(fn):
        def body(e, c):
            @pl.when(ztab_ref[e] > 0)
            def _():
                fn(zero_copy(e))
            return c
        lax.fori_loop(0, N_EXPERTS, body, 0)

        def tail(b, c):
            fn(pltpu.make_async_copy(zbuf, xs_hbm.at[pl.ds(pl.multiple_of(b * EXPERT_BLOCK, EXPERT_BLOCK),
                                                           EXPERT_BLOCK)], zsem))
            return c
        lax.fori_loop(ztab_ref[2 * N_EXPERTS], xs_hbm.shape[0] // EXPERT_BLOCK, tail, 0)

    @pl.when(i == 0)
    def _():
        zbuf[...] = jnp.zeros_like(zbuf)
        for_zero_runs(lambda cp: cp.start())

    xb = x_ref[...].astype(BF16)
    total = tab_ref[2 * N_EXPERTS]
    run_lo = tabt_ref[0, 0:1, 0:N_EXPERTS]
    run_hi = tabt_ref[0, 1:2, 0:N_EXPERTS]

    def permute(r0):
        j_e = (lax.broadcasted_iota(jnp.int32, (PERM_CHUNK, N_EXPERTS), 0) + r0).astype(F32)
        owner = jnp.where((j_e >= run_lo) & (j_e < run_hi), 1.0, 0.0).astype(BF16)
        pos = (jnp.dot(owner, hi_ref[...], preferred_element_type=F32) * float(POS_RADIX)
               + jnp.dot(owner, lo_ref[...], preferred_element_type=F32))
        j_t = (lax.broadcasted_iota(jnp.int32, (PERM_CHUNK, t), 0) + r0).astype(F32)
        p = jnp.where(pos == j_t, 1.0, 0.0).astype(BF16)
        sbuf[slot, r0:r0 + PERM_CHUNK, :] = jnp.dot(p, xb, preferred_element_type=F32).astype(BF16)

    for r0 in range(0, rows, PERM_CHUNK):
        if r0 < t * TOP_K + PERM_CHUNK:
            permute(r0)
        else:
            pl.when(total > r0)(functools.partial(permute, r0))

    _run_copies(tab_ref, lambda loc, glob, n: pltpu.make_async_copy(
        sbuf.at[slot, pl.ds(loc, n)], xs_hbm.at[pl.ds(glob, n)], sems.at[slot]), unroll=True)

    def wait_rows(s, n):
        pltpu.make_async_copy(sbuf.at[s, pl.ds(0, n)], xs_hbm.at[pl.ds(0, n)], sems.at[s]).wait()

    @pl.when(i > 0)
    def _():
        wait_rows(1 - slot, _aligned(tab_ref[2 * N_EXPERTS + 1]))

    @pl.when(i == pl.num_programs(0) - 1)
    def _():
        wait_rows(slot, _aligned(tab_ref[2 * N_EXPERTS]))

    @pl.when(i == 0)
    def _():
        for_zero_runs(lambda cp: cp.wait())


def _dispatch(tab, ztab, tab_t, pos_hi, pos_lo, hx2, xs_rows):
    n, d = hx2.shape
    col = lambda i: (0, i)
    return pl.pallas_call(
        _dispatch_kernel,
        grid=(n // TOKEN_TILE,),
        in_specs=[pl.BlockSpec((TAB_WIDTH,), lambda i: (i,), memory_space=pltpu.SMEM),
                  pl.BlockSpec(memory_space=pltpu.SMEM),
                  pl.BlockSpec((1, SUBLANES, LANES), lambda i: (i, 0, 0)),
                  pl.BlockSpec((N_EXPERTS, TOKEN_TILE), col),
                  pl.BlockSpec((N_EXPERTS, TOKEN_TILE), col),
                  pl.BlockSpec((TOKEN_TILE, d), lambda i: (i, 0))],
        out_specs=pl.BlockSpec(memory_space=pl.ANY),
        out_shape=jax.ShapeDtypeStruct((xs_rows, d), BF16),
        scratch_shapes=[pltpu.VMEM((2, SORTED_ROWS, d), BF16),
                        pltpu.VMEM((EXPERT_BLOCK, d), BF16),
                        pltpu.SemaphoreType.DMA((2,)),
                        pltpu.SemaphoreType.DMA],
        compiler_params=_params(("arbitrary",)),
        name="moe_dispatch",
    )(tab, ztab, tab_t, pos_hi, pos_lo, hx2)


def _experts_kernel(be_ref, nb_ref, xs_ref, wgu_ref, wd_ref, ys_ref, wgu_b, wd_b):
    i = pl.program_id(0)
    ff = wd_b.shape[0]
    used = i < nb_ref[0]
    new_expert = (i == 0) | (be_ref[i] != be_ref[jnp.maximum(i - 1, 0)])

    @pl.when(used & new_expert)
    def _():
        wgu_b[...] = wgu_ref[0, 0].astype(BF16)
        wd_b[...] = wd_ref[0, 0].astype(BF16)

    @pl.when(used)
    def _():
        starts = range(0, xs_ref.shape[0], MIX_SUB)
        hs = [jnp.dot(xs_ref[r0:r0 + MIX_SUB, :], wgu_b[...], preferred_element_type=F32) for r0 in starts]
        for r0, h in zip(starts, hs):
            a = _silu(h[:, :ff]) * h[:, ff:]
            ys_ref[r0:r0 + MIX_SUB, :] = jnp.dot(a.astype(BF16), wd_b[...],
                                                 preferred_element_type=F32).astype(BF16)

    @pl.when(jnp.logical_not(used))
    def _():
        ys_ref[...] = jnp.zeros_like(ys_ref)


def _experts(block_e, n_used, xs, w_gu, w_down, layer):
    rows, d = xs.shape
    nb = rows // EXPERT_BLOCK
    blk = lambda i, be, nu: (jnp.maximum(jnp.minimum(i, nu[0] - 1), 0), 0)
    wmap = lambda i, be, nu: (layer, be[i], 0, 0)
    return pl.pallas_call(
        _experts_kernel,
        grid_spec=pltpu.PrefetchScalarGridSpec(
            num_scalar_prefetch=2,
            grid=(nb,),
            in_specs=[pl.BlockSpec((EXPERT_BLOCK, d), blk),
                      pl.BlockSpec((1, 1) + w_gu.shape[2:], wmap),
                      pl.BlockSpec((1, 1) + w_down.shape[2:], wmap)],
            out_specs=pl.BlockSpec((EXPERT_BLOCK, d), lambda i, be, nu: (i, 0)),
            scratch_shapes=[pltpu.VMEM(w_gu.shape[2:], BF16), pltpu.VMEM(w_down.shape[2:], BF16)]),
        out_shape=jax.ShapeDtypeStruct((rows, d), BF16),
        compiler_params=_params(("arbitrary",)),
        name="moe_experts",
    )(block_e, n_used, xs, w_gu, w_down)


def _combine_kernel(tab_ref, tabn_ref, tabv_ref, hi_ref, lo_ref, gate_ref, ys_hbm, hx2_ref, x1_ref, mod_ref,
                    wsgu_ref, wsd_ref, fg_ref, out_ref, ybuf, sems, *, tiles_per_seq, final_norm):
    t_rows, d = x1_ref.shape
    ff = wsd_ref.shape[0]
    rows = ybuf.shape[1]
    i = pl.program_id(0)
    last = pl.num_programs(0) - 1
    slot = i % 2
    row = i // tiles_per_seq

    def gather(table, s, unroll):
        _run_copies(table, lambda loc, glob, n: pltpu.make_async_copy(
            ys_hbm.at[pl.ds(glob, n)], ybuf.at[s, pl.ds(loc, n)], sems.at[s]), unroll=unroll)

    def wait_rows(table, s):
        n = _aligned(table[2 * N_EXPERTS])
        pltpu.make_async_copy(ys_hbm.at[pl.ds(0, n)], ybuf.at[s, pl.ds(0, n)], sems.at[s]).wait()

    @pl.when(i == 0)
    def _():
        ybuf[...] = jnp.zeros_like(ybuf)
        gather(tab_ref, 0, False)

    gather(tabn_ref, 1 - slot, True)

    hs = jnp.dot(hx2_ref[...].astype(BF16), wsgu_ref[...], preferred_element_type=F32)
    shared = jnp.dot((_silu(hs[:, :ff]) * hs[:, ff:]).astype(BF16), wsd_ref[...], preferred_element_type=F32)

    run_lo = tabv_ref[0, :, 2:3]
    run_hi = run_lo + tabv_ref[0, :, 0:1]
    j_e = lax.broadcasted_iota(jnp.int32, (N_EXPERTS, rows), 1).astype(F32)
    owner = jnp.where((j_e >= run_lo) & (j_e < run_hi), 1.0, 0.0).astype(BF16)
    pos = (jnp.dot(hi_ref[...], owner, preferred_element_type=F32) * float(POS_RADIX)
           + jnp.dot(lo_ref[...], owner, preferred_element_type=F32))
    gates = jnp.dot(gate_ref[...], owner, preferred_element_type=F32)
    j_t = lax.broadcasted_iota(jnp.int32, (t_rows, rows), 1).astype(F32)
    gb = jnp.where(pos == j_t, gates, 0.0).astype(BF16)

    wait_rows(tab_ref, slot)
    routed = jnp.dot(gb, ybuf[slot], preferred_element_type=F32)
    x2 = x1_ref[...] + _mod_chunk(mod_ref, row, 5, d) * (routed + shared)
    if final_norm:
        x2 = x2 * lax.rsqrt(jnp.mean(x2 * x2, axis=-1, keepdims=True) + NORM_EPS) * fg_ref[...]
    out_ref[...] = x2

    @pl.when(i == last)
    def _():
        wait_rows(tabn_ref, 1 - slot)


def _combine(tab, tab_v, pos_hi, pos_lo, gate, ys, hx2, x1, mod, ws_gu, ws_down, fg, tiles_per_seq, final_norm):
    n, d = x1.shape
    row = lambda i: (i, 0)
    const = lambda i: (0, 0)
    return pl.pallas_call(
        functools.partial(_combine_kernel, tiles_per_seq=tiles_per_seq, final_norm=final_norm),
        grid=(n // TOKEN_TILE,),
        in_specs=[pl.BlockSpec((TAB_WIDTH,), lambda i: (i,), memory_space=pltpu.SMEM),
                  pl.BlockSpec((TAB_WIDTH,), lambda i: (jnp.minimum(i + 1, n // TOKEN_TILE - 1),),
                               memory_space=pltpu.SMEM),
                  pl.BlockSpec((1, N_EXPERTS, LANES), lambda i: (i, 0, 0)),
                  pl.BlockSpec((TOKEN_TILE, N_EXPERTS), row),
                  pl.BlockSpec((TOKEN_TILE, N_EXPERTS), row),
                  pl.BlockSpec((TOKEN_TILE, N_EXPERTS), row),
                  pl.BlockSpec(memory_space=pl.ANY),
                  pl.BlockSpec((TOKEN_TILE, d), row),
                  pl.BlockSpec((TOKEN_TILE, d), row),
                  pl.BlockSpec(mod.shape, const),
                  pl.BlockSpec(ws_gu.shape, const),
                  pl.BlockSpec(ws_down.shape, const),
                  pl.BlockSpec((1, d), const)],
        out_specs=pl.BlockSpec((TOKEN_TILE, d), row),
        out_shape=jax.ShapeDtypeStruct((n, d), F32),
        scratch_shapes=[pltpu.VMEM((2, SORTED_ROWS, d), BF16), pltpu.SemaphoreType.DMA((2,))],
        compiler_params=_params(("arbitrary",)),
        name="moe_combine",
    )(tab, tab, tab_v, pos_hi, pos_lo, gate, ys, hx2, x1, mod, ws_gu, ws_down, fg)


def _moe(x1, hx2, pos_hi, pos_lo, gate, tab_f, tab_t, mod, w_gu, w_down, layer, ws_gu, ws_down, fg,
         tiles_per_seq, final_norm):
    n, d = x1.shape
    n_tiles = n // TOKEN_TILE
    run_len = tab_f[:, :, 0].astype(jnp.int32)
    counts = jnp.sum(run_len, axis=0)
    padded = (counts + EXPERT_BLOCK - 1) // EXPERT_BLOCK * EXPERT_BLOCK
    pad_end = jnp.cumsum(padded)
    pad_start = pad_end - padded
    max_rows = n * TOP_K + n_tiles * N_EXPERTS * ROW_ALIGN + N_EXPERTS * (EXPERT_BLOCK - ROW_ALIGN)
    n_blocks = (max_rows + EXPERT_BLOCK - 1) // EXPERT_BLOCK
    n_used = (pad_end[-1:] // EXPERT_BLOCK).astype(jnp.int32)
    block_start = jnp.arange(n_blocks, dtype=jnp.int32) * EXPERT_BLOCK
    block_e = jnp.minimum(jnp.sum((pad_end[None, :] <= block_start[:, None]).astype(jnp.int32), axis=1),
                          N_EXPERTS - 1)

    run_start = pad_start[None, :] + jnp.cumsum(run_len, axis=0) - run_len
    total = jnp.sum(run_len, axis=1, keepdims=True)
    prev_total = jnp.concatenate([jnp.zeros((1, 1), jnp.int32), total[:-1]], axis=0)
    run_local = tab_f[:, :, 2].astype(jnp.int32)
    fill = jnp.zeros((n_tiles, TAB_WIDTH - 3 * N_EXPERTS - 2), jnp.int32)
    tab = jnp.concatenate([run_len, run_start, total, prev_total, run_local, fill], axis=1).reshape(-1)
    ztab = jnp.concatenate([padded - counts, pad_start + counts, n_used])

    xs = _dispatch(tab, ztab, tab_t, pos_hi, pos_lo, hx2, n_blocks * EXPERT_BLOCK)
    ys = _experts(block_e, n_used, xs, w_gu, w_down, layer)
    return _combine(tab, tab_f, pos_hi.T, pos_lo.T, gate.T, ys, hx2, x1, mod, ws_gu, ws_down, fg,
                    tiles_per_seq, final_norm)


def _sincos_2d(rows, d):
    quarter = d // 4
    omega = 1.0 / (POS_BASE ** (jnp.arange(quarter, dtype=F32) / quarter))

    def emb(n):
        p = jnp.arange(n, dtype=F32)[:, None] * omega[None, :]
        return jnp.concatenate([jnp.sin(p), jnp.cos(p)], axis=-1)

    er, ec = emb(rows), emb(GRID_W)
    pe = jnp.concatenate([jnp.broadcast_to(er[:, None, :], (rows, GRID_W, d // 2)),
                          jnp.broadcast_to(ec[None, :, :], (rows, GRID_W, d // 2))], axis=-1)
    return pe.reshape(rows * GRID_W, d)


def kernel(x, c, ctx, c_ctx, ada_w, ada_b, mix_norm_g, ffn_norm_g, a_w_in, a_conv_w, a_conv_b, a_gate_r_w, a_gate_r_b, a_gate_i_w, a_gate_i_b, a_lambda, a_w_out, b_w_in, b_ln_g, b_ln_b, b_w_s, b_b_s, b_w_out, router_w, router_b, moe_w_gu, moe_w_down, shared_w_gu, shared_w_down, final_norm_g):
    bsz, s, d = x.shape
    ctx_len = ctx.shape[1]
    depth = ada_w.shape[0]
    assert depth == 2 and bsz < MOD_ROWS and s % MIX_TILE == 0 and MIX_TILE % TOKEN_TILE == 0
    assert s % SCAN_TILE == 0
    assert ctx_len % TOKEN_TILE == 0
    n = bsz * s
    tps = s // TOKEN_TILE
    ctx_row = bsz

    cc = jnp.zeros((MOD_ROWS, d), F32).at[:bsz].set(c).at[ctx_row].set(c_ctx)
    mod = _modulation(cc, ada_w, ada_b)
    pe = _sincos_2d(s // GRID_W, d)
    rc = a_w_in.shape[2] // 2

    w_in0 = a_w_in[0].astype(BF16)
    g_mix0 = mix_norm_g[0].reshape(1, d)
    x0, gate, ux = _rglru_in(x.reshape(n, d), pe, mod[0], g_mix0, w_in0)
    uc = _ctx_in(ctx.reshape(bsz * ctx_len, d), mod[0], g_mix0, w_in0[:, rc:], ctx_row)
    w_ri = jnp.concatenate([a_gate_r_w[0], a_gate_i_w[0]], axis=-1).astype(BF16)
    conv = (a_conv_w[0], a_conv_b[0])
    gates = [(w_ri[k], a_gate_r_b[0, k], a_gate_i_b[0, k], a_lambda[0, k]) for k in range(2)]
    h_zero = jnp.zeros((bsz, rc), F32)
    h_fwd, uc = _lru_scan(uc, conv, *gates[0], h_zero, reverse=False, reset_first=True, emit_y=False)
    h_rev = _lru_scan(uc, None, *gates[1], h_zero, reverse=True, reset_first=True, emit_y=False)
    y_fwd, ux = _lru_scan(ux, conv, *gates[0], h_fwd, reverse=False, reset_first=False, emit_y=True)
    y_rev = _lru_scan(ux, None, *gates[1], h_rev, reverse=True, reset_first=False, emit_y=True)
    pre = _rglru_out(y_fwd, y_rev, gate, x0, mod[0], a_w_out[0].astype(BF16), ffn_norm_g[0].reshape(1, d),
                     router_w[0].T, router_b[0].reshape(N_EXPERTS, 1), s // MIX_TILE)
    x1 = _moe(*pre, mod[0], moe_w_gu, moe_w_down, 0, shared_w_gu[0].astype(BF16),
              shared_w_down[0].astype(BF16), final_norm_g.reshape(1, d), tps, False)

    pre = _sgu(x1, mod[1], mix_norm_g[1].reshape(1, d), b_w_in[0].astype(BF16),
               b_ln_g[0].reshape(1, -1), b_ln_b[0].reshape(1, -1), b_w_s[0].astype(BF16), b_b_s[0].T,
               b_w_out[0].astype(BF16), ffn_norm_g[1].reshape(1, d),
               router_w[1].T, router_b[1].reshape(N_EXPERTS, 1), s // MIX_TILE)
    out = _moe(*pre, mod[1], moe_w_gu, moe_w_down, 1, shared_w_gu[1].astype(BF16),
               shared_w_down[1].astype(BF16), final_norm_g.reshape(1, d), tps, True)
    return out.reshape(bsz, s, d)
```

```python
import functools

import jax
import jax.numpy as jnp
from jax import lax
from jax.experimental import pallas as pl
from jax.experimental.pallas import tpu as pltpu

F32 = jnp.float32
BF16 = jnp.bfloat16
HIGHEST = lax.Precision.HIGHEST

GRID_W = 64
N_MOD = 6
NORM_EPS = 1e-6
POS_BASE = 10000.0
RNN_HEADS = 5
CONV_WIDTH = 4
CONV_PAD_LEFT = 2
LRU_C = 8.0
SGU_HEADS = 8
CHUNK = 128
N_EXPERTS = 64
TOP_K = 8
N_GROUPS = 8
TOPK_GROUPS = 4
EXPERTS_PER_GROUP = N_EXPERTS // N_GROUPS
ROUTED_SCALE = 2.5

SUBLANES = 8
ROW_ALIGN = 16
LANES = 128
MOD_ROWS = 8
TOKEN_TILE = 256
MIX_TILE = 512
SCAN_TILE = 512
EXPERT_BLOCK = 1024
SORTED_ROWS = TOKEN_TILE * TOP_K + N_EXPERTS * ROW_ALIGN
PERM_CHUNK = 512
TAB_WIDTH = 256
TAB_LOCAL = 2 * N_EXPERTS + 2
POS_RADIX = 64
VMEM_LIMIT = 56 * 1024 * 1024


def _params(semantics, vmem=VMEM_LIMIT):
    return pltpu.CompilerParams(dimension_semantics=semantics, vmem_limit_bytes=vmem)


def _silu(x):
    return x * jax.nn.sigmoid(x)


def _rms_mod(x, g, sc, sh):
    y = x * lax.rsqrt(jnp.mean(x * x, axis=-1, keepdims=True) + NORM_EPS)
    return (y * g) * (1.0 + sc) + sh


def _mod_chunk(mod_ref, row, k, d):
    return mod_ref[pl.ds(row, 1), k * d:(k + 1) * d]


def _round_up_rows(count):
    return jnp.maximum(jnp.ceil(count * (1.0 / ROW_ALIGN)), 1.0) * float(ROW_ALIGN)


def _mod_kernel(cc_ref, w_ref, b_ref, o_ref):
    s = _silu(cc_ref[...])
    o_ref[0] = jnp.dot(s, w_ref[0], preferred_element_type=F32, precision=HIGHEST) + b_ref[0]


def _modulation(cc, ada_w, ada_b):
    depth, d, nd = ada_w.shape
    return pl.pallas_call(
        _mod_kernel,
        grid=(depth, nd // d),
        in_specs=[pl.BlockSpec((MOD_ROWS, d), lambda l, j: (0, 0)),
                  pl.BlockSpec((1, d, d), lambda l, j: (l, 0, j)),
                  pl.BlockSpec((1, 1, d), lambda l, j: (l, 0, j))],
        out_specs=pl.BlockSpec((1, MOD_ROWS, d), lambda l, j: (l, 0, j)),
        out_shape=jax.ShapeDtypeStruct((depth, MOD_ROWS, nd), F32),
        compiler_params=_params(("arbitrary", "arbitrary")),
        name="modulation",
    )(cc, ada_w, ada_b.reshape(depth, 1, nd))


def _rglru_in_kernel(x_ref, pe_ref, mod_ref, g_ref, w_ref, x0_ref, gate_ref, u_ref, *, tiles_per_seq):
    d = x_ref.shape[1]
    c = u_ref.shape[1]
    row = pl.program_id(0) // tiles_per_seq
    x = x_ref[...] + pe_ref[...]
    hx = _rms_mod(x, g_ref[...], _mod_chunk(mod_ref, row, 1, d), _mod_chunk(mod_ref, row, 0, d))
    z = jnp.dot(hx.astype(BF16), w_ref[...], preferred_element_type=F32)
    x0_ref[...] = x
    gate_ref[...] = jax.nn.gelu(z[:, :c]).astype(BF16)
    u_ref[...] = z[:, c:]


def _rglru_in(x2, pe, mod, g, w_in):
    n, d = x2.shape
    c = w_in.shape[1] // 2
    s = pe.shape[0]
    tps = s // MIX_TILE
    row = lambda i: (i, 0)
    return pl.pallas_call(
        functools.partial(_rglru_in_kernel, tiles_per_seq=tps),
        grid=(n // MIX_TILE,),
        in_specs=[pl.BlockSpec((MIX_TILE, d), row),
                  pl.BlockSpec((MIX_TILE, d), lambda i: (i % tps, 0)),
                  pl.BlockSpec(mod.shape, lambda i: (0, 0)),
                  pl.BlockSpec((1, d), lambda i: (0, 0)),
                  pl.BlockSpec(w_in.shape, lambda i: (0, 0))],
        out_specs=[pl.BlockSpec((MIX_TILE, d), row),
                   pl.BlockSpec((MIX_TILE, c), row),
                   pl.BlockSpec((MIX_TILE, c), row)],
        out_shape=[jax.ShapeDtypeStruct((n, d), F32),
                   jax.ShapeDtypeStruct((n, c), BF16),
                   jax.ShapeDtypeStruct((n, c), F32)],
        compiler_params=_params(("arbitrary",)),
        name="rglru_in",
    )(x2, pe, mod, g, w_in)


def _ctx_in_kernel(x_ref, mod_ref, g_ref, w_ref, u_ref, *, ctx_row):
    d = x_ref.shape[1]
    hx = _rms_mod(x_ref[...], g_ref[...], _mod_chunk(mod_ref, ctx_row, 1, d),
                  _mod_chunk(mod_ref, ctx_row, 0, d))
    u_ref[...] = jnp.dot(hx.astype(BF16), w_ref[...], preferred_element_type=F32)


def _ctx_in(c2, mod, g, w_u, ctx_row):
    n, d = c2.shape
    c = w_u.shape[1]
    return pl.pallas_call(
        functools.partial(_ctx_in_kernel, ctx_row=ctx_row),
        grid=(n // TOKEN_TILE,),
        in_specs=[pl.BlockSpec((TOKEN_TILE, d), lambda i: (i, 0)),
                  pl.BlockSpec(mod.shape, lambda i: (0, 0)),
                  pl.BlockSpec((1, d), lambda i: (0, 0)),
                  pl.BlockSpec(w_u.shape, lambda i: (0, 0))],
        out_specs=pl.BlockSpec((TOKEN_TILE, c), lambda i: (i, 0)),
        out_shape=jax.ShapeDtypeStruct((n, c), F32),
        compiler_params=_params(("arbitrary",)),
        name="ctx_in",
    )(c2, mod, g, w_u)


def _log_sigmoid(x):
    return jnp.minimum(x, 0.0) - jnp.log1p(jnp.exp(-jnp.abs(x)))


def _lru_scan_kernel(*refs, n_tiles, reverse, conv_done, reset_first, emit_y):
    if conv_done:
        u_ref, wri_ref, rb_ref, ib_ref, lam_ref, h0_ref, out_ref, a_scr, b_scr, h_scr = refs
    else:
        (u_ref, up_ref, un_ref, cw_ref, cb_ref, wri_ref, rb_ref, ib_ref, lam_ref, h0_ref,
         out_ref, uc_ref, ubuf, a_scr, b_scr, h_scr) = refs
    t_rows, c = u_ref.shape
    hb = c // RNN_HEADS
    b = pl.program_id(0)
    j = pl.program_id(1)
    jj = n_tiles - 1 - j if reverse else j

    if conv_done:
        u = u_ref[...]
    else:
        ubuf[SUBLANES:SUBLANES + t_rows, :] = u_ref[...]
        ubuf[0:SUBLANES, :] = jnp.where(jj == 0, 0.0, up_ref[...])
        ubuf[SUBLANES + t_rows:, :] = jnp.where(jj == n_tiles - 1, 0.0, un_ref[...])
        u = cb_ref[...]
        for k in range(CONV_WIDTH):
            u = u + cw_ref[k:k + 1, :] * ubuf[pl.ds(SUBLANES - CONV_PAD_LEFT + k, t_rows), :]
        uc_ref[...] = u

    log_lam = LRU_C * _log_sigmoid(lam_ref[...])
    rows = lax.broadcasted_iota(jnp.int32, (t_rows, 1), 0)
    first_row = jnp.where(j == 0, t_rows - 1 if reverse else 0, -1)
    for h in range(RNN_HEADS):
        sl = slice(h * hb, (h + 1) * hb)
        uh = u[:, sl]
        z = jnp.dot(uh.astype(BF16), wri_ref[h], preferred_element_type=F32)
        r = jax.nn.sigmoid(z[:, :hb] + rb_ref[:, sl])
        ig = jax.nn.sigmoid(z[:, hb:] + ib_ref[:, sl])
        log_a = r * log_lam[:, sl]
        a = jnp.exp(log_a)
        mult = jnp.sqrt((1.0 - a) * (1.0 + a))
        if reset_first:
            mult = jnp.where(rows == first_row, 1.0, mult)
        a_scr[:, sl] = a
        b_scr[:, sl] = mult * ig * uh

    @pl.when(j == 0)
    def _():
        h_scr[...] = h0_ref[pl.ds(b, 1), :]

    n_groups = t_rows // SUBLANES

    def group(g, h):
        base = pl.multiple_of((n_groups - 1 - g if reverse else g) * SUBLANES, SUBLANES)
        for s in range(SUBLANES):
            r = base + (SUBLANES - 1 - s if reverse else s)
            h = a_scr[pl.ds(r, 1), :] * h + b_scr[pl.ds(r, 1), :]
            if emit_y:
                out_ref[pl.ds(r, 1), :] = h
        return h

    h = lax.fori_loop(0, n_groups, group, h_scr[...])
    h_scr[...] = h
    if not emit_y:
        @pl.when(j == n_tiles - 1)
        def _():
            out_ref[pl.ds(b, 1), :] = h


def _lru_scan(u, conv, w_ri, r_b, i_b, lam, h0, *, reverse, reset_first, emit_y):
    n, c = u.shape
    n_batch = h0.shape[0]
    rows = min(SCAN_TILE, n // n_batch)
    n_tiles = n // n_batch // rows
    sub = rows // SUBLANES
    n_sub = n // SUBLANES
    const = lambda b, j: (0, 0)

    def tile(b, j):
        return b * n_tiles + (n_tiles - 1 - j if reverse else j)

    tile_spec = pl.BlockSpec((rows, c), lambda b, j: (tile(b, j), 0))
    in_specs = [tile_spec]
    args = [u]
    scratch = []
    if conv is not None:
        conv_w, conv_b = conv
        in_specs += [pl.BlockSpec((SUBLANES, c), lambda b, j: (jnp.maximum(tile(b, j) * sub - 1, 0), 0)),
                     pl.BlockSpec((SUBLANES, c), lambda b, j: (jnp.minimum((tile(b, j) + 1) * sub, n_sub - 1), 0)),
                     pl.BlockSpec(conv_w.shape, const),
                     pl.BlockSpec((1, c), const)]
        args += [u, u, conv_w, conv_b.reshape(1, c)]
        scratch = [pltpu.VMEM((rows + 2 * SUBLANES, c), F32)]
    in_specs += [pl.BlockSpec(w_ri.shape, lambda b, j: (0, 0, 0)),
                 pl.BlockSpec((1, c), const), pl.BlockSpec((1, c), const), pl.BlockSpec((1, c), const),
                 pl.BlockSpec(h0.shape, const)]
    args += [w_ri, r_b.reshape(1, c), i_b.reshape(1, c), lam.reshape(1, c), h0]
    if emit_y:
        out_specs = [tile_spec]
        out_shape = [jax.ShapeDtypeStruct((n, c), F32)]
    else:
        out_specs = [pl.BlockSpec(h0.shape, const)]
        out_shape = [jax.ShapeDtypeStruct(h0.shape, F32)]
    if conv is not None:
        out_specs.append(tile_spec)
        out_shape.append(jax.ShapeDtypeStruct((n, c), F32))
    name = ("lru_scan" if emit_y else "lru_ctx") + ("_rev" if reverse else "_fwd")
    outs = pl.pallas_call(
        functools.partial(_lru_scan_kernel, n_tiles=n_tiles, reverse=reverse, conv_done=conv is None,
                          reset_first=reset_first, emit_y=emit_y),
        grid=(n_batch, n_tiles),
        in_specs=in_specs,
        out_specs=out_specs,
        out_shape=out_shape,
        scratch_shapes=scratch + [pltpu.VMEM((rows, c), F32),
                                  pltpu.VMEM((rows, c), F32),
                                  pltpu.VMEM((1, c), F32)],
        compiler_params=_params(("arbitrary", "arbitrary")),
        name=name,
    )(*args)
    return outs if conv is not None else outs[0]


def _route(logits, rb):
    e, t = logits.shape
    neg = -jnp.inf
    scores = jax.nn.sigmoid(logits)
    sel = scores + rb
    iota_g = lax.broadcasted_iota(jnp.int32, (N_GROUPS, t), 0).astype(F32)
    iota_e = lax.broadcasted_iota(jnp.int32, (e, t), 0).astype(F32)

    gs = jnp.full((N_GROUPS, t), neg, F32)
    for g in range(N_GROUPS):
        sg = sel[g * EXPERTS_PER_GROUP:(g + 1) * EXPERTS_PER_GROUP, :]
        m1 = jnp.max(sg, axis=0, keepdims=True)
        i1 = jnp.min(jnp.where(sg == m1, iota_g, float(EXPERTS_PER_GROUP)), axis=0, keepdims=True)
        m2 = jnp.max(jnp.where(iota_g == i1, neg, sg), axis=0, keepdims=True)
        gs = jnp.where(iota_g == float(g), m1 + m2, gs)

    keep = jnp.zeros((N_GROUPS, t), F32)
    for _ in range(TOPK_GROUPS):
        m = jnp.max(gs, axis=0, keepdims=True)
        idx = jnp.min(jnp.where(gs == m, iota_g, float(N_GROUPS)), axis=0, keepdims=True)
        hit = iota_g == idx
        keep = jnp.where(hit, 1.0, keep)
        gs = jnp.where(hit, neg, gs)

    masked = jnp.concatenate(
        [jnp.where(keep[g:g + 1, :] > 0.0, sel[g * EXPERTS_PER_GROUP:(g + 1) * EXPERTS_PER_GROUP, :], neg)
         for g in range(N_GROUPS)], axis=0)

    iota_k = lax.broadcasted_iota(jnp.int32, (TOP_K, t), 0)
    selmask = jnp.zeros((e, t), F32)
    eidx = jnp.zeros((TOP_K, t), F32)
    gw = jnp.zeros((TOP_K, t), F32)
    for k in range(TOP_K):
        m = jnp.max(masked, axis=0, keepdims=True)
        idx = jnp.min(jnp.where(masked == m, iota_e, float(e)), axis=0, keepdims=True)
        hit = iota_e == idx
        gk = jnp.sum(jnp.where(hit, scores, 0.0), axis=0, keepdims=True)
        masked = jnp.where(hit, neg, masked)
        selmask = jnp.where(hit, 1.0, selmask)
        eidx = jnp.where(iota_k == k, idx, eidx)
        gw = jnp.where(iota_k == k, gk, gw)
    gw = gw / jnp.sum(gw, axis=0, keepdims=True) * ROUTED_SCALE
    return eidx, gw, selmask


def _ffn_pre(x1, mod_ref, row, g2_ref, rwt_ref, rb_ref, hx2_ref, hi_ref, lo_ref, gate_ref, tab_ref, tabt_ref):
    d = x1.shape[1]
    hx2 = _rms_mod(x1, g2_ref[...], _mod_chunk(mod_ref, row, 4, d), _mod_chunk(mod_ref, row, 3, d))
    hx2_ref[...] = hx2
    logits = lax.dot_general(rwt_ref[...], hx2, (((1,), (1,)), ((), ())),
                             preferred_element_type=F32, precision=HIGHEST)
    eidx, gw, selmask = _route(logits, rb_ref[...])
    for s in range(selmask.shape[1] // TOKEN_TILE):
        ts = slice(s * TOKEN_TILE, (s + 1) * TOKEN_TILE)
        hi, lo, gate, tab, tabt = _sorted_layout(selmask[:, ts], eidx[:, ts], gw[:, ts])
        hi_ref[:, ts] = hi
        lo_ref[:, ts] = lo
        gate_ref[:, ts] = gate
        tab_ref[s] = tab
        tabt_ref[s] = tabt


def _ffn_pre_specs(n, d):
    row = lambda i: (i, 0)
    col = lambda i: (0, i)
    sub = MIX_TILE // TOKEN_TILE
    lead = lambda i: (i, 0, 0)
    out_specs = [pl.BlockSpec((MIX_TILE, d), row),
                 pl.BlockSpec((MIX_TILE, d), row),
                 pl.BlockSpec((N_EXPERTS, MIX_TILE), col),
                 pl.BlockSpec((N_EXPERTS, MIX_TILE), col),
                 pl.BlockSpec((N_EXPERTS, MIX_TILE), col),
                 pl.BlockSpec((sub, N_EXPERTS, LANES), lead),
                 pl.BlockSpec((sub, SUBLANES, LANES), lead)]
    out_shape = [jax.ShapeDtypeStruct((n, d), F32),
                 jax.ShapeDtypeStruct((n, d), F32),
                 jax.ShapeDtypeStruct((N_EXPERTS, n), BF16),
                 jax.ShapeDtypeStruct((N_EXPERTS, n), BF16),
                 jax.ShapeDtypeStruct((N_EXPERTS, n), BF16),
                 jax.ShapeDtypeStruct((n // TOKEN_TILE, N_EXPERTS, LANES), F32),
                 jax.ShapeDtypeStruct((n // TOKEN_TILE, SUBLANES, LANES), F32)]
    return out_specs, out_shape


def _rglru_out_kernel(yf_ref, yr_ref, gate_ref, x0_ref, mod_ref, wout_ref, g2_ref, rwt_ref, rb_ref,
                      x1_ref, hx2_ref, hi_ref, lo_ref, gate_out_ref, tab_ref, tabt_ref, *, tiles_per_seq):
    d = x0_ref.shape[1]
    row = pl.program_id(0) // tiles_per_seq
    yx = yf_ref[...] + yr_ref[...]
    v = gate_ref[...].astype(F32) * yx
    out = jnp.dot(v.astype(BF16), wout_ref[...], preferred_element_type=F32)
    x1 = x0_ref[...] + _mod_chunk(mod_ref, row, 2, d) * out
    x1_ref[...] = x1
    _ffn_pre(x1, mod_ref, row, g2_ref, rwt_ref, rb_ref, hx2_ref, hi_ref, lo_ref, gate_out_ref, tab_ref, tabt_ref)


def _rglru_out(y_fwd, y_rev, gate, x0, mod, w_out, g2, rwt, rb, tiles_per_seq):
    n, d = x0.shape
    c = gate.shape[1]
    row = lambda i: (i, 0)
    const = lambda i: (0, 0)
    out_specs, out_shape = _ffn_pre_specs(n, d)
    return pl.pallas_call(
        functools.partial(_rglru_out_kernel, tiles_per_seq=tiles_per_seq),
        grid=(n // MIX_TILE,),
        in_specs=[pl.BlockSpec((MIX_TILE, c), row),
                  pl.BlockSpec((MIX_TILE, c), row),
                  pl.BlockSpec((MIX_TILE, c), row),
                  pl.BlockSpec((MIX_TILE, d), row),
                  pl.BlockSpec(mod.shape, const),
                  pl.BlockSpec(w_out.shape, const),
                  pl.BlockSpec((1, d), const),
                  pl.BlockSpec(rwt.shape, const),
                  pl.BlockSpec(rb.shape, const)],
        out_specs=out_specs,
        out_shape=out_shape,
        compiler_params=_params(("arbitrary",)),
        name="rglru_out",
    )(y_fwd, y_rev, gate, x0, mod, w_out, g2, rwt, rb)


def _sgu_kernel(x_ref, mod_ref, g_ref, win_ref, lng_ref, lnb_ref, ws_ref, bst_ref, wout_ref,
                g2_ref, rwt_ref, rb_ref,
                x1_ref, hx2_ref, hi_ref, lo_ref, gate_out_ref, tab_ref, tabt_ref, m_scr, *, tiles_per_seq):
    t_rows, d = x_ref.shape
    w = wout_ref.shape[0]
    gd = w // SGU_HEADS
    row = pl.program_id(0) // tiles_per_seq
    x = x_ref[...]
    hx = _rms_mod(x, g_ref[...], _mod_chunk(mod_ref, row, 1, d), _mod_chunk(mod_ref, row, 0, d))
    z = jax.nn.gelu(jnp.dot(hx.astype(BF16), win_ref[...], preferred_element_type=F32))
    u = z[:, :w]
    v = z[:, w:]
    mu = jnp.mean(v, axis=-1, keepdims=True)
    vc = v - mu
    v = vc * lax.rsqrt(jnp.mean(vc * vc, axis=-1, keepdims=True) + NORM_EPS) * lng_ref[...] + lnb_ref[...]
    vb = v.astype(BF16)
    for ch in range(t_rows // CHUNK):
        rs = slice(ch * CHUNK, (ch + 1) * CHUNK)
        for g in range(SGU_HEADS):
            cs = slice(g * gd, (g + 1) * gd)
            sv = jnp.dot(ws_ref[g], vb[rs, cs], preferred_element_type=F32) + bst_ref[:, g:g + 1]
            m_scr[rs, cs] = (u[rs, cs] * sv).astype(BF16)
    out = jnp.dot(m_scr[...], wout_ref[...], preferred_element_type=F32)
    x1 = x + _mod_chunk(mod_ref, row, 2, d) * out
    x1_ref[...] = x1
    _ffn_pre(x1, mod_ref, row, g2_ref, rwt_ref, rb_ref, hx2_ref, hi_ref, lo_ref, gate_out_ref, tab_ref, tabt_ref)


def _sgu(x, mod, g, w_in, ln_g, ln_b, w_s, b_st, w_out, g2, rwt, rb, tiles_per_seq):
    n, d = x.shape
    w = w_out.shape[0]
    const = lambda i: (0, 0)
    out_specs, out_shape = _ffn_pre_specs(n, d)
    return pl.pallas_call(
        functools.partial(_sgu_kernel, tiles_per_seq=tiles_per_seq),
        grid=(n // MIX_TILE,),
        in_specs=[pl.BlockSpec((MIX_TILE, d), lambda i: (i, 0)),
                  pl.BlockSpec(mod.shape, const),
                  pl.BlockSpec((1, d), const),
                  pl.BlockSpec(w_in.shape, const, pipeline_mode=pl.Buffered(1)),
                  pl.BlockSpec((1, w), const),
                  pl.BlockSpec((1, w), const),
                  pl.BlockSpec(w_s.shape, lambda i: (0, 0, 0)),
                  pl.BlockSpec(b_st.shape, const),
                  pl.BlockSpec(w_out.shape, const, pipeline_mode=pl.Buffered(1)),
                  pl.BlockSpec((1, d), const),
                  pl.BlockSpec(rwt.shape, const),
                  pl.BlockSpec(rb.shape, const)],
        out_specs=out_specs,
        out_shape=out_shape,
        scratch_shapes=[pltpu.VMEM((MIX_TILE, w), BF16)],
        compiler_params=_params(("arbitrary",)),
        name="sgu",
    )(x, mod, g, w_in, ln_g, ln_b, w_s, b_st, w_out, g2, rwt, rb)


def _sorted_layout(m, eidx, gw):
    e, t = m.shape
    lanes = LANES
    r = lax.broadcasted_iota(jnp.int32, (t, t), 0)
    c = lax.broadcasted_iota(jnp.int32, (t, t), 1)
    upper = jnp.where(r <= c, 1.0, 0.0).astype(BF16)
    incl = jnp.dot(m.astype(BF16), upper, preferred_element_type=F32)
    run = _round_up_rows(incl[:, t - 1:t])
    re = lax.broadcasted_iota(jnp.int32, (e, e), 0)
    ce = lax.broadcasted_iota(jnp.int32, (e, e), 1)
    lower = jnp.where(ce < re, 1.0, 0.0).astype(BF16)
    tiles = jnp.broadcast_to(run * (1.0 / ROW_ALIGN), (e, lanes)).astype(BF16)
    lstart = jnp.dot(lower, tiles, preferred_element_type=F32)[:, 0:1] * float(ROW_ALIGN)
    pos = jnp.where(m > 0.0, lstart + incl - m, float(POS_RADIX * POS_RADIX - 1))
    hi = jnp.floor(pos * (1.0 / POS_RADIX))
    lo = pos - hi * float(POS_RADIX)
    iota_e = lax.broadcasted_iota(jnp.int32, (e, t), 0).astype(F32)
    gate = jnp.zeros((e, t), F32)
    for k in range(TOP_K):
        gate = jnp.where(iota_e == eidx[k:k + 1, :], gw[k:k + 1, :], gate)
    lane = lax.broadcasted_iota(jnp.int32, (e, lanes), 1)
    tab = jnp.where(lane == 0, run, jnp.where(lane == 2, lstart, 0.0))
    diag = lax.broadcasted_iota(jnp.int32, (e, lanes), 0) == lane
    row_of = lambda col: jnp.sum(jnp.where(diag, jnp.broadcast_to(col, (e, lanes)), 0.0), axis=0, keepdims=True)
    sub = lax.broadcasted_iota(jnp.int32, (SUBLANES, lanes), 0)
    tabt = jnp.where(sub == 0, row_of(lstart), jnp.where(sub == 1, row_of(lstart + run), 0.0))
    return hi.astype(BF16), lo.astype(BF16), gate.astype(BF16), tab, tabt


def _aligned(v):
    return pl.multiple_of(v, ROW_ALIGN)


def _run_copies(tab_ref, make_copy, unroll):
    def body(e, carry):
        make_copy(_aligned(tab_ref[TAB_LOCAL + e]), _aligned(tab_ref[N_EXPERTS + e]),
                  _aligned(tab_ref[e])).start()
        return carry

    lax.fori_loop(0, N_EXPERTS, body, 0, unroll=unroll)


def _dispatch_kernel(tab_ref, ztab_ref, tabt_ref, hi_ref, lo_ref, x_ref, xs_hbm, sbuf, zbuf, sems, zsem):
    i = pl.program_id(0)
    slot = i % 2
    t = x_ref.shape[0]
    rows = sbuf.shape[1]

    def zero_copy(e):
        n = _aligned(ztab_ref[e])
        return pltpu.make_async_copy(zbuf.at[pl.ds(0, n)], xs_hbm.at[pl.ds(_aligned(ztab_ref[N_EXPERTS + e]), n)], zsem)

    def for_zero_runs(fn):
        def body(e, c):
            @pl.when(ztab_ref[e] > 0)
            def _():
                fn(zero_copy(e))
            return c
        lax.fori_loop(0, N_EXPERTS, body, 0)

        def tail(b, c):
            fn(pltpu.make_async_copy(zbuf, xs_hbm.at[pl.ds(pl.multiple_of(b * EXPERT_BLOCK, EXPERT_BLOCK),
                                                           EXPERT_BLOCK)], zsem))
            return c
        lax.fori_loop(ztab_ref[2 * N_EXPERTS], xs_hbm.shape[0] // EXPERT_BLOCK, tail, 0)

    @pl.when(i == 0)
    def _():
        zbuf[...] = jnp.zeros_like(zbuf)
        for_zero_runs(lambda cp: cp.start())

    xb = x_ref[...].astype(BF16)
    total = tab_ref[2 * N_EXPERTS]
    run_lo = tabt_ref[0, 0:1, 0:N_EXPERTS]
    run_hi = tabt_ref[0, 1:2, 0:N_EXPERTS]

    def permute(r0):
        j_e = (lax.broadcasted_iota(jnp.int32, (PERM_CHUNK, N_EXPERTS), 0) + r0).astype(F32)
        owner = jnp.where((j_e >= run_lo) & (j_e < run_hi), 1.0, 0.0).astype(BF16)
        pos = (jnp.dot(owner, hi_ref[...], preferred_element_type=F32) * float(POS_RADIX)
               + jnp.dot(owner, lo_ref[...], preferred_element_type=F32))
        j_t = (lax.broadcasted_iota(jnp.int32, (PERM_CHUNK, t), 0) + r0).astype(F32)
        p = jnp.where(pos == j_t, 1.0, 0.0).astype(BF16)
        sbuf[slot, r0:r0 + PERM_CHUNK, :] = jnp.dot(p, xb, preferred_element_type=F32).astype(BF16)

    for r0 in range(0, rows, PERM_CHUNK):
        if r0 < t * TOP_K + PERM_CHUNK:
            permute(r0)
        else:
            pl.when(total > r0)(functools.partial(permute, r0))

    _run_copies(tab_ref, lambda loc, glob, n: pltpu.make_async_copy(
        sbuf.at[slot, pl.ds(loc, n)], xs_hbm.at[pl.ds(glob, n)], sems.at[slot]), unroll=True)

    def wait_rows(s, n):
        pltpu.make_async_copy(sbuf.at[s, pl.ds(0, n)], xs_hbm.at[pl.ds(0, n)], sems.at[s]).wait()

    @pl.when(i > 0)
    def _():
        wait_rows(1 - slot, _aligned(tab_ref[2 * N_EXPERTS + 1]))

    @pl.when(i == pl.num_programs(0) - 1)
    def _():
        wait_rows(slot, _aligned(tab_ref[2 * N_EXPERTS]))

    @pl.when(i == 0)
    def _():
        for_zero_runs(lambda cp: cp.wait())


def _dispatch(tab, ztab, tab_t, pos_hi, pos_lo, hx2, xs_rows):
    n, d = hx2.shape
    col = lambda i: (0, i)
    return pl.pallas_call(
        _dispatch_kernel,
        grid=(n // TOKEN_TILE,),
        in_specs=[pl.BlockSpec((TAB_WIDTH,), lambda i: (i,), memory_space=pltpu.SMEM),
                  pl.BlockSpec(memory_space=pltpu.SMEM),
                  pl.BlockSpec((1, SUBLANES, LANES), lambda i: (i, 0, 0)),
                  pl.BlockSpec((N_EXPERTS, TOKEN_TILE), col),
                  pl.BlockSpec((N_EXPERTS, TOKEN_TILE), col),
                  pl.BlockSpec((TOKEN_TILE, d), lambda i: (i, 0))],
        out_specs=pl.BlockSpec(memory_space=pl.ANY),
        out_shape=jax.ShapeDtypeStruct((xs_rows, d), BF16),
        scratch_shapes=[pltpu.VMEM((2, SORTED_ROWS, d), BF16),
                        pltpu.VMEM((EXPERT_BLOCK, d), BF16),
                        pltpu.SemaphoreType.DMA((2,)),
                        pltpu.SemaphoreType.DMA],
        compiler_params=_params(("arbitrary",)),
        name="moe_dispatch",
    )(tab, ztab, tab_t, pos_hi, pos_lo, hx2)


def _experts_kernel(be_ref, nb_ref, xs_ref, wgu_ref, wd_ref, ys_ref, wgu_b, wd_b):
    i = pl.program_id(0)
    ff = wd_b.shape[0]
    used = i < nb_ref[0]
    new_expert = (i == 0) | (be_ref[i] != be_ref[jnp.maximum(i - 1, 0)])

    @pl.when(used & new_expert)
    def _():
        wgu_b[...] = wgu_ref[0, 0].astype(BF16)
        wd_b[...] = wd_ref[0, 0].astype(BF16)

    @pl.when(used)
    def _():
        h = jnp.dot(xs_ref[...], wgu_b[...], preferred_element_type=F32)
        a = _silu(h[:, :ff]) * h[:, ff:]
        ys_ref[...] = jnp.dot(a.astype(BF16), wd_b[...], preferred_element_type=F32).astype(BF16)

    @pl.when(jnp.logical_not(used))
    def _():
        ys_ref[...] = jnp.zeros_like(ys_ref)


def _experts(block_e, n_used, xs, w_gu, w_down, layer):
    rows, d = xs.shape
    nb = rows // EXPERT_BLOCK
    blk = lambda i, be, nu: (jnp.maximum(jnp.minimum(i, nu[0] - 1), 0), 0)
    wmap = lambda i, be, nu: (layer, be[i], 0, 0)
    return pl.pallas_call(
        _experts_kernel,
        grid_spec=pltpu.PrefetchScalarGridSpec(
            num_scalar_prefetch=2,
            grid=(nb,),
            in_specs=[pl.BlockSpec((EXPERT_BLOCK, d), blk),
                      pl.BlockSpec((1, 1) + w_gu.shape[2:], wmap),
                      pl.BlockSpec((1, 1) + w_down.shape[2:], wmap)],
            out_specs=pl.BlockSpec((EXPERT_BLOCK, d), lambda i, be, nu: (i, 0)),
            scratch_shapes=[pltpu.VMEM(w_gu.shape[2:], BF16), pltpu.VMEM(w_down.shape[2:], BF16)]),
        out_shape=jax.ShapeDtypeStruct((rows, d), BF16),
        compiler_params=_params(("arbitrary",)),
        name="moe_experts",
    )(block_e, n_used, xs, w_gu, w_down)


def _combine_kernel(tab_ref, tabn_ref, tabv_ref, hi_ref, lo_ref, gate_ref, ys_hbm, hx2_ref, x1_ref, mod_ref,
                    wsgu_ref, wsd_ref, fg_ref, out_ref, ybuf, sems, *, tiles_per_seq, final_norm):
    t_rows, d = x1_ref.shape
    ff = wsd_ref.shape[0]
    rows = ybuf.shape[1]
    i = pl.program_id(0)
    last = pl.num_programs(0) - 1
    slot = i % 2
    row = i // tiles_per_seq

    def gather(table, s, unroll):
        _run_copies(table, lambda loc, glob, n: pltpu.make_async_copy(
            ys_hbm.at[pl.ds(glob, n)], ybuf.at[s, pl.ds(loc, n)], sems.at[s]), unroll=unroll)

    def wait_rows(table, s):
        n = _aligned(table[2 * N_EXPERTS])
        pltpu.make_async_copy(ys_hbm.at[pl.ds(0, n)], ybuf.at[s, pl.ds(0, n)], sems.at[s]).wait()

    @pl.when(i == 0)
    def _():
        ybuf[...] = jnp.zeros_like(ybuf)
        gather(tab_ref, 0, False)

    gather(tabn_ref, 1 - slot, True)

    hs = jnp.dot(hx2_ref[...].astype(BF16), wsgu_ref[...], preferred_element_type=F32)
    shared = jnp.dot((_silu(hs[:, :ff]) * hs[:, ff:]).astype(BF16), wsd_ref[...], preferred_element_type=F32)

    run_lo = tabv_ref[0, :, 2:3]
    run_hi = run_lo + tabv_ref[0, :, 0:1]
    j_e = lax.broadcasted_iota(jnp.int32, (N_EXPERTS, rows), 1).astype(F32)
    owner = jnp.where((j_e >= run_lo) & (j_e < run_hi), 1.0, 0.0).astype(BF16)
    pos = (jnp.dot(hi_ref[...], owner, preferred_element_type=F32) * float(POS_RADIX)
           + jnp.dot(lo_ref[...], owner, preferred_element_type=F32))
    gates = jnp.dot(gate_ref[...], owner, preferred_element_type=F32)
    j_t = lax.broadcasted_iota(jnp.int32, (t_rows, rows), 1).astype(F32)
    gb = jnp.where(pos == j_t, gates, 0.0).astype(BF16)

    wait_rows(tab_ref, slot)
    routed = jnp.dot(gb, ybuf[slot], preferred_element_type=F32)
    x2 = x1_ref[...] + _mod_chunk(mod_ref, row, 5, d) * (routed + shared)
    if final_norm:
        x2 = x2 * lax.rsqrt(jnp.mean(x2 * x2, axis=-1, keepdims=True) + NORM_EPS) * fg_ref[...]
    out_ref[...] = x2

    @pl.when(i == last)
    def _():
        wait_rows(tabn_ref, 1 - slot)


def _combine(tab, tab_v, pos_hi, pos_lo, gate, ys, hx2, x1, mod, ws_gu, ws_down, fg, tiles_per_seq, final_norm):
    n, d = x1.shape
    row = lambda i: (i, 0)
    const = lambda i: (0, 0)
    return pl.pallas_call(
        functools.partial(_combine_kernel, tiles_per_seq=tiles_per_seq, final_norm=final_norm),
        grid=(n // TOKEN_TILE,),
        in_specs=[pl.BlockSpec((TAB_WIDTH,), lambda i: (i,), memory_space=pltpu.SMEM),
                  pl.BlockSpec((TAB_WIDTH,), lambda i: (jnp.minimum(i + 1, n // TOKEN_TILE - 1),),
                               memory_space=pltpu.SMEM),
                  pl.BlockSpec((1, N_EXPERTS, LANES), lambda i: (i, 0, 0)),
                  pl.BlockSpec((TOKEN_TILE, N_EXPERTS), row),
                  pl.BlockSpec((TOKEN_TILE, N_EXPERTS), row),
                  pl.BlockSpec((TOKEN_TILE, N_EXPERTS), row),
                  pl.BlockSpec(memory_space=pl.ANY),
                  pl.BlockSpec((TOKEN_TILE, d), row),
                  pl.BlockSpec((TOKEN_TILE, d), row),
                  pl.BlockSpec(mod.shape, const),
                  pl.BlockSpec(ws_gu.shape, const),
                  pl.BlockSpec(ws_down.shape, const),
                  pl.BlockSpec((1, d), const)],
        out_specs=pl.BlockSpec((TOKEN_TILE, d), row),
        out_shape=jax.ShapeDtypeStruct((n, d), F32),
        scratch_shapes=[pltpu.VMEM((2, SORTED_ROWS, d), BF16), pltpu.SemaphoreType.DMA((2,))],
        compiler_params=_params(("arbitrary",)),
        name="moe_combine",
    )(tab, tab, tab_v, pos_hi, pos_lo, gate, ys, hx2, x1, mod, ws_gu, ws_down, fg)


def _moe(x1, hx2, pos_hi, pos_lo, gate, tab_f, tab_t, mod, w_gu, w_down, layer, ws_gu, ws_down, fg,
         tiles_per_seq, final_norm):
    n, d = x1.shape
    n_tiles = n // TOKEN_TILE
    run_len = tab_f[:, :, 0].astype(jnp.int32)
    counts = jnp.sum(run_len, axis=0)
    padded = (counts + EXPERT_BLOCK - 1) // EXPERT_BLOCK * EXPERT_BLOCK
    pad_end = jnp.cumsum(padded)
    pad_start = pad_end - padded
    max_rows = n * TOP_K + n_tiles * N_EXPERTS * ROW_ALIGN + N_EXPERTS * (EXPERT_BLOCK - ROW_ALIGN)
    n_blocks = (max_rows + EXPERT_BLOCK - 1) // EXPERT_BLOCK
    n_used = (pad_end[-1:] // EXPERT_BLOCK).astype(jnp.int32)
    block_start = jnp.arange(n_blocks, dtype=jnp.int32) * EXPERT_BLOCK
    block_e = jnp.minimum(jnp.sum((pad_end[None, :] <= block_start[:, None]).astype(jnp.int32), axis=1),
                          N_EXPERTS - 1)

    run_start = pad_start[None, :] + jnp.cumsum(run_len, axis=0) - run_len
    total = jnp.sum(run_len, axis=1, keepdims=True)
    prev_total = jnp.concatenate([jnp.zeros((1, 1), jnp.int32), total[:-1]], axis=0)
    run_local = tab_f[:, :, 2].astype(jnp.int32)
    fill = jnp.zeros((n_tiles, TAB_WIDTH - 3 * N_EXPERTS - 2), jnp.int32)
    tab = jnp.concatenate([run_len, run_start, total, prev_total, run_local, fill], axis=1).reshape(-1)
    ztab = jnp.concatenate([padded - counts, pad_start + counts, n_used])

    xs = _dispatch(tab, ztab, tab_t, pos_hi, pos_lo, hx2, n_blocks * EXPERT_BLOCK)
    ys = _experts(block_e, n_used, xs, w_gu, w_down, layer)
    return _combine(tab, tab_f, pos_hi.T, pos_lo.T, gate.T, ys, hx2, x1, mod, ws_gu, ws_down, fg,
                    tiles_per_seq, final_norm)


def _sincos_2d(rows, d):
    quarter = d // 4
    omega = 1.0 / (POS_BASE ** (jnp.arange(quarter, dtype=F32) / quarter))

    def emb(n):
        p = jnp.arange(n, dtype=F32)[:, None] * omega[None, :]
        return jnp.concatenate([jnp.sin(p), jnp.cos(p)], axis=-1)

    er, ec = emb(rows), emb(GRID_W)
    pe = jnp.concatenate([jnp.broadcast_to(er[:, None, :], (rows, GRID_W, d // 2)),
                          jnp.broadcast_to(ec[None, :, :], (rows, GRID_W, d // 2))], axis=-1)
    return pe.reshape(rows * GRID_W, d)


def kernel(x, c, ctx, c_ctx, ada_w, ada_b, mix_norm_g, ffn_norm_g, a_w_in, a_conv_w, a_conv_b, a_gate_r_w, a_gate_r_b, a_gate_i_w, a_gate_i_b, a_lambda, a_w_out, b_w_in, b_ln_g, b_ln_b, b_w_s, b_b_s, b_w_out, router_w, router_b, moe_w_gu, moe_w_down, shared_w_gu, shared_w_down, final_norm_g):
    bsz, s, d = x.shape
    ctx_len = ctx.shape[1]
    depth = ada_w.shape[0]
    assert depth == 2 and bsz < MOD_ROWS and s % MIX_TILE == 0 and MIX_TILE % TOKEN_TILE == 0
    assert s % SCAN_TILE == 0
    assert ctx_len % TOKEN_TILE == 0
    n = bsz * s
    tps = s // TOKEN_TILE
    ctx_row = bsz

    cc = jnp.zeros((MOD_ROWS, d), F32).at[:bsz].set(c).at[ctx_row].set(c_ctx)
    mod = _modulation(cc, ada_w, ada_b)
    pe = _sincos_2d(s // GRID_W, d)
    rc = a_w_in.shape[2] // 2

    w_in0 = a_w_in[0].astype(BF16)
    g_mix0 = mix_norm_g[0].reshape(1, d)
    x0, gate, ux = _rglru_in(x.reshape(n, d), pe, mod[0], g_mix0, w_in0)
    uc = _ctx_in(ctx.reshape(bsz * ctx_len, d), mod[0], g_mix0, w_in0[:, rc:], ctx_row)
    w_ri = jnp.concatenate([a_gate_r_w[0], a_gate_i_w[0]], axis=-1).astype(BF16)
    conv = (a_conv_w[0], a_conv_b[0])
    gates = [(w_ri[k], a_gate_r_b[0, k], a_gate_i_b[0, k], a_lambda[0, k]) for k in range(2)]
    h_zero = jnp.zeros((bsz, rc), F32)
    h_fwd, uc = _lru_scan(uc, conv, *gates[0], h_zero, reverse=False, reset_first=True, emit_y=False)
    h_rev = _lru_scan(uc, None, *gates[1], h_zero, reverse=True, reset_first=True, emit_y=False)
    y_fwd, ux = _lru_scan(ux, conv, *gates[0], h_fwd, reverse=False, reset_first=False, emit_y=True)
    y_rev = _lru_scan(ux, None, *gates[1], h_rev, reverse=True, reset_first=False, emit_y=True)
    pre = _rglru_out(y_fwd, y_rev, gate, x0, mod[0], a_w_out[0].astype(BF16), ffn_norm_g[0].reshape(1, d),
                     router_w[0].T, router_b[0].reshape(N_EXPERTS, 1), s // MIX_TILE)
    x1 = _moe(*pre, mod[0], moe_w_gu, moe_w_down, 0, shared_w_gu[0].astype(BF16),
              shared_w_down[0].astype(BF16), final_norm_g.reshape(1, d), tps, False)

    pre = _sgu(x1, mod[1], mix_norm_g[1].reshape(1, d), b_w_in[0].astype(BF16),
               b_ln_g[0].reshape(1, -1), b_ln_b[0].reshape(1, -1), b_w_s[0].astype(BF16), b_b_s[0].T,
               b_w_out[0].astype(BF16), ffn_norm_g[1].reshape(1, d),
               router_w[1].T, router_b[1].reshape(N_EXPERTS, 1), s // MIX_TILE)
    out = _moe(*pre, mod[1], moe_w_gu, moe_w_down, 1, shared_w_gu[1].astype(BF16),
               shared_w_down[1].astype(BF16), final_norm_g.reshape(1, d), tps, True)
    return out.reshape(bsz, s, d)
```

```python
import functools

import jax
import jax.numpy as jnp
from jax import lax
from jax.experimental import pallas as pl
from jax.experimental.pallas import tpu as pltpu

F32 = jnp.float32
BF16 = jnp.bfloat16
HIGHEST = lax.Precision.HIGHEST

GRID_W = 64
N_MOD = 6
NORM_EPS = 1e-6
POS_BASE = 10000.0
RNN_HEADS = 5
CONV_WIDTH = 4
CONV_PAD_LEFT = 2
LRU_C = 8.0
SGU_HEADS = 8
CHUNK = 128
N_EXPERTS = 64
TOP_K = 8
N_GROUPS = 8
TOPK_GROUPS = 4
EXPERTS_PER_GROUP = N_EXPERTS // N_GROUPS
ROUTED_SCALE = 2.5

SUBLANES = 8
ROW_ALIGN = 16
LANES = 128
MOD_ROWS = 8
TOKEN_TILE = 256
MIX_TILE = 512
SCAN_TILE = 512
EXPERT_BLOCK = 1024
EXPERT_HALF = EXPERT_BLOCK // 2
SORTED_ROWS = TOKEN_TILE * TOP_K + N_EXPERTS * ROW_ALIGN
PERM_CHUNK = 512
TAB_WIDTH = 256
TAB_LOCAL = 2 * N_EXPERTS + 2
POS_RADIX = 64
VMEM_LIMIT = 56 * 1024 * 1024


def _params(semantics, vmem=VMEM_LIMIT):
    return pltpu.CompilerParams(dimension_semantics=semantics, vmem_limit_bytes=vmem)


def _silu(x):
    return x * jax.nn.sigmoid(x)


def _rms_mod(x, g, sc, sh):
    y = x * lax.rsqrt(jnp.mean(x * x, axis=-1, keepdims=True) + NORM_EPS)
    return (y * g) * (1.0 + sc) + sh


def _mod_chunk(mod_ref, row, k, d):
    return mod_ref[pl.ds(row, 1), k * d:(k + 1) * d]


def _round_up_rows(count):
    return jnp.maximum(jnp.ceil(count * (1.0 / ROW_ALIGN)), 1.0) * float(ROW_ALIGN)


def _mod_kernel(cc_ref, w_ref, b_ref, o_ref):
    s = _silu(cc_ref[...])
    o_ref[0] = jnp.dot(s, w_ref[0], preferred_element_type=F32, precision=HIGHEST) + b_ref[0]


def _modulation(cc, ada_w, ada_b):
    depth, d, nd = ada_w.shape
    return pl.pallas_call(
        _mod_kernel,
        grid=(depth, nd // d),
        in_specs=[pl.BlockSpec((MOD_ROWS, d), lambda l, j: (0, 0)),
                  pl.BlockSpec((1, d, d), lambda l, j: (l, 0, j)),
                  pl.BlockSpec((1, 1, d), lambda l, j: (l, 0, j))],
        out_specs=pl.BlockSpec((1, MOD_ROWS, d), lambda l, j: (l, 0, j)),
        out_shape=jax.ShapeDtypeStruct((depth, MOD_ROWS, nd), F32),
        compiler_params=_params(("arbitrary", "arbitrary")),
        name="modulation",
    )(cc, ada_w, ada_b.reshape(depth, 1, nd))


def _rglru_in_kernel(x_ref, pe_ref, mod_ref, g_ref, w_ref, x0_ref, gate_ref, u_ref, *, tiles_per_seq):
    d = x_ref.shape[1]
    c = u_ref.shape[1]
    row = pl.program_id(0) // tiles_per_seq
    x = x_ref[...] + pe_ref[...]
    hx = _rms_mod(x, g_ref[...], _mod_chunk(mod_ref, row, 1, d), _mod_chunk(mod_ref, row, 0, d))
    z = jnp.dot(hx.astype(BF16), w_ref[...], preferred_element_type=F32)
    x0_ref[...] = x
    gate_ref[...] = jax.nn.gelu(z[:, :c]).astype(BF16)
    u_ref[...] = z[:, c:]


def _rglru_in(x2, pe, mod, g, w_in):
    n, d = x2.shape
    c = w_in.shape[1] // 2
    s = pe.shape[0]
    tps = s // MIX_TILE
    row = lambda i: (i, 0)
    return pl.pallas_call(
        functools.partial(_rglru_in_kernel, tiles_per_seq=tps),
        grid=(n // MIX_TILE,),
        in_specs=[pl.BlockSpec((MIX_TILE, d), row),
                  pl.BlockSpec((MIX_TILE, d), lambda i: (i % tps, 0)),
                  pl.BlockSpec(mod.shape, lambda i: (0, 0)),
                  pl.BlockSpec((1, d), lambda i: (0, 0)),
                  pl.BlockSpec(w_in.shape, lambda i: (0, 0))],
        out_specs=[pl.BlockSpec((MIX_TILE, d), row),
                   pl.BlockSpec((MIX_TILE, c), row),
                   pl.BlockSpec((MIX_TILE, c), row)],
        out_shape=[jax.ShapeDtypeStruct((n, d), F32),
                   jax.ShapeDtypeStruct((n, c), BF16),
                   jax.ShapeDtypeStruct((n, c), F32)],
        compiler_params=_params(("arbitrary",)),
        name="rglru_in",
    )(x2, pe, mod, g, w_in)


def _ctx_in_kernel(x_ref, mod_ref, g_ref, w_ref, u_ref, *, ctx_row):
    d = x_ref.shape[1]
    hx = _rms_mod(x_ref[...], g_ref[...], _mod_chunk(mod_ref, ctx_row, 1, d),
                  _mod_chunk(mod_ref, ctx_row, 0, d))
    u_ref[...] = jnp.dot(hx.astype(BF16), w_ref[...], preferred_element_type=F32)


def _ctx_in(c2, mod, g, w_u, ctx_row):
    n, d = c2.shape
    c = w_u.shape[1]
    return pl.pallas_call(
        functools.partial(_ctx_in_kernel, ctx_row=ctx_row),
        grid=(n // TOKEN_TILE,),
        in_specs=[pl.BlockSpec((TOKEN_TILE, d), lambda i: (i, 0)),
                  pl.BlockSpec(mod.shape, lambda i: (0, 0)),
                  pl.BlockSpec((1, d), lambda i: (0, 0)),
                  pl.BlockSpec(w_u.shape, lambda i: (0, 0))],
        out_specs=pl.BlockSpec((TOKEN_TILE, c), lambda i: (i, 0)),
        out_shape=jax.ShapeDtypeStruct((n, c), F32),
        compiler_params=_params(("arbitrary",)),
        name="ctx_in",
    )(c2, mod, g, w_u)


def _log_sigmoid(x):
    return jnp.minimum(x, 0.0) - jnp.log1p(jnp.exp(-jnp.abs(x)))


def _lru_scan_kernel(*refs, n_tiles, reverse, conv_done, reset_first, emit_y):
    if conv_done:
        u_ref, wri_ref, rb_ref, ib_ref, lam_ref, h0_ref, out_ref, a_scr, b_scr, h_scr = refs
    else:
        (u_ref, up_ref, un_ref, cw_ref, cb_ref, wri_ref, rb_ref, ib_ref, lam_ref, h0_ref,
         out_ref, uc_ref, ubuf, a_scr, b_scr, h_scr) = refs
    t_rows, c = u_ref.shape
    hb = c // RNN_HEADS
    b = pl.program_id(0)
    j = pl.program_id(1)
    jj = n_tiles - 1 - j if reverse else j

    if conv_done:
        u = u_ref[...]
    else:
        ubuf[SUBLANES:SUBLANES + t_rows, :] = u_ref[...]
        ubuf[0:SUBLANES, :] = jnp.where(jj == 0, 0.0, up_ref[...])
        ubuf[SUBLANES + t_rows:, :] = jnp.where(jj == n_tiles - 1, 0.0, un_ref[...])
        u = cb_ref[...]
        for k in range(CONV_WIDTH):
            u = u + cw_ref[k:k + 1, :] * ubuf[pl.ds(SUBLANES - CONV_PAD_LEFT + k, t_rows), :]
        uc_ref[...] = u

    log_lam = LRU_C * _log_sigmoid(lam_ref[...])
    rows = lax.broadcasted_iota(jnp.int32, (t_rows, 1), 0)
    first_row = jnp.where(j == 0, t_rows - 1 if reverse else 0, -1)
    for h in range(RNN_HEADS):
        sl = slice(h * hb, (h + 1) * hb)
        uh = u[:, sl]
        z = jnp.dot(uh.astype(BF16), wri_ref[h], preferred_element_type=F32)
        r = jax.nn.sigmoid(z[:, :hb] + rb_ref[:, sl])
        ig = jax.nn.sigmoid(z[:, hb:] + ib_ref[:, sl])
        log_a = r * log_lam[:, sl]
        a = jnp.exp(log_a)
        mult = jnp.sqrt((1.0 - a) * (1.0 + a))
        if reset_first:
            mult = jnp.where(rows == first_row, 1.0, mult)
        a_scr[:, sl] = a
        b_scr[:, sl] = mult * ig * uh

    @pl.when(j == 0)
    def _():
        h_scr[...] = h0_ref[pl.ds(b, 1), :]

    n_groups = t_rows // SUBLANES

    def group(g, h):
        base = pl.multiple_of((n_groups - 1 - g if reverse else g) * SUBLANES, SUBLANES)
        for s in range(SUBLANES):
            r = base + (SUBLANES - 1 - s if reverse else s)
            h = a_scr[pl.ds(r, 1), :] * h + b_scr[pl.ds(r, 1), :]
            if emit_y:
                out_ref[pl.ds(r, 1), :] = h
        return h

    h = lax.fori_loop(0, n_groups, group, h_scr[...])
    h_scr[...] = h
    if not emit_y:
        @pl.when(j == n_tiles - 1)
        def _():
            out_ref[pl.ds(b, 1), :] = h


def _lru_scan(u, conv, w_ri, r_b, i_b, lam, h0, *, reverse, reset_first, emit_y):
    n, c = u.shape
    n_batch = h0.shape[0]
    rows = min(SCAN_TILE, n // n_batch)
    n_tiles = n // n_batch // rows
    sub = rows // SUBLANES
    n_sub = n // SUBLANES
    const = lambda b, j: (0, 0)

    def tile(b, j):
        return b * n_tiles + (n_tiles - 1 - j if reverse else j)

    tile_spec = pl.BlockSpec((rows, c), lambda b, j: (tile(b, j), 0))
    in_specs = [tile_spec]
    args = [u]
    scratch = []
    if conv is not None:
        conv_w, conv_b = conv
        in_specs += [pl.BlockSpec((SUBLANES, c), lambda b, j: (jnp.maximum(tile(b, j) * sub - 1, 0), 0)),
                     pl.BlockSpec((SUBLANES, c), lambda b, j: (jnp.minimum((tile(b, j) + 1) * sub, n_sub - 1), 0)),
                     pl.BlockSpec(conv_w.shape, const),
                     pl.BlockSpec((1, c), const)]
        args += [u, u, conv_w, conv_b.reshape(1, c)]
        scratch = [pltpu.VMEM((rows + 2 * SUBLANES, c), F32)]
    in_specs += [pl.BlockSpec(w_ri.shape, lambda b, j: (0, 0, 0)),
                 pl.BlockSpec((1, c), const), pl.BlockSpec((1, c), const), pl.BlockSpec((1, c), const),
                 pl.BlockSpec(h0.shape, const)]
    args += [w_ri, r_b.reshape(1, c), i_b.reshape(1, c), lam.reshape(1, c), h0]
    if emit_y:
        out_specs = [tile_spec]
        out_shape = [jax.ShapeDtypeStruct((n, c), F32)]
    else:
        out_specs = [pl.BlockSpec(h0.shape, const)]
        out_shape = [jax.ShapeDtypeStruct(h0.shape, F32)]
    if conv is not None:
        out_specs.append(tile_spec)
        out_shape.append(jax.ShapeDtypeStruct((n, c), F32))
    name = ("lru_scan" if emit_y else "lru_ctx") + ("_rev" if reverse else "_fwd")
    outs = pl.pallas_call(
        functools.partial(_lru_scan_kernel, n_tiles=n_tiles, reverse=reverse, conv_done=conv is None,
                          reset_first=reset_first, emit_y=emit_y),
        grid=(n_batch, n_tiles),
        in_specs=in_specs,
        out_specs=out_specs,
        out_shape=out_shape,
        scratch_shapes=scratch + [pltpu.VMEM((rows, c), F32),
                                  pltpu.VMEM((rows, c), F32),
                                  pltpu.VMEM((1, c), F32)],
        compiler_params=_params(("arbitrary", "arbitrary")),
        name=name,
    )(*args)
    return outs if conv is not None else outs[0]


def _route(logits, rb):
    e, t = logits.shape
    neg = -jnp.inf
    scores = jax.nn.sigmoid(logits)
    sel = scores + rb
    iota_g = lax.broadcasted_iota(jnp.int32, (N_GROUPS, t), 0).astype(F32)
    iota_e = lax.broadcasted_iota(jnp.int32, (e, t), 0).astype(F32)

    gs = jnp.full((N_GROUPS, t), neg, F32)
    for g in range(N_GROUPS):
        sg = sel[g * EXPERTS_PER_GROUP:(g + 1) * EXPERTS_PER_GROUP, :]
        m1 = jnp.max(sg, axis=0, keepdims=True)
        i1 = jnp.min(jnp.where(sg == m1, iota_g, float(EXPERTS_PER_GROUP)), axis=0, keepdims=True)
        m2 = jnp.max(jnp.where(iota_g == i1, neg, sg), axis=0, keepdims=True)
        gs = jnp.where(iota_g == float(g), m1 + m2, gs)

    keep = jnp.zeros((N_GROUPS, t), F32)
    for _ in range(TOPK_GROUPS):
        m = jnp.max(gs, axis=0, keepdims=True)
        idx = jnp.min(jnp.where(gs == m, iota_g, float(N_GROUPS)), axis=0, keepdims=True)
        hit = iota_g == idx
        keep = jnp.where(hit, 1.0, keep)
        gs = jnp.where(hit, neg, gs)

    masked = jnp.concatenate(
        [jnp.where(keep[g:g + 1, :] > 0.0, sel[g * EXPERTS_PER_GROUP:(g + 1) * EXPERTS_PER_GROUP, :], neg)
         for g in range(N_GROUPS)], axis=0)

    iota_k = lax.broadcasted_iota(jnp.int32, (TOP_K, t), 0)
    selmask = jnp.zeros((e, t), F32)
    eidx = jnp.zeros((TOP_K, t), F32)
    gw = jnp.zeros((TOP_K, t), F32)
    for k in range(TOP_K):
        m = jnp.max(masked, axis=0, keepdims=True)
        idx = jnp.min(jnp.where(masked == m, iota_e, float(e)), axis=0, keepdims=True)
        hit = iota_e == idx
        gk = jnp.sum(jnp.where(hit, scores, 0.0), axis=0, keepdims=True)
        masked = jnp.where(hit, neg, masked)
        selmask = jnp.where(hit, 1.0, selmask)
        eidx = jnp.where(iota_k == k, idx, eidx)
        gw = jnp.where(iota_k == k, gk, gw)
    gw = gw / jnp.sum(gw, axis=0, keepdims=True) * ROUTED_SCALE
    return eidx, gw, selmask


def _ffn_pre(x1, mod_ref, row, g2_ref, rwt_ref, rb_ref, hx2_ref, hi_ref, lo_ref, gate_ref, tab_ref, tabt_ref):
    d = x1.shape[1]
    hx2 = _rms_mod(x1, g2_ref[...], _mod_chunk(mod_ref, row, 4, d), _mod_chunk(mod_ref, row, 3, d))
    hx2_ref[...] = hx2
    logits = lax.dot_general(rwt_ref[...], hx2, (((1,), (1,)), ((), ())),
                             preferred_element_type=F32, precision=HIGHEST)
    eidx, gw, selmask = _route(logits, rb_ref[...])
    for s in range(selmask.shape[1] // TOKEN_TILE):
        ts = slice(s * TOKEN_TILE, (s + 1) * TOKEN_TILE)
        hi, lo, gate, tab, tabt = _sorted_layout(selmask[:, ts], eidx[:, ts], gw[:, ts])
        hi_ref[:, ts] = hi
        lo_ref[:, ts] = lo
        gate_ref[:, ts] = gate
        tab_ref[s] = tab
        tabt_ref[s] = tabt


def _ffn_pre_specs(n, d):
    row = lambda i: (i, 0)
    col = lambda i: (0, i)
    sub = MIX_TILE // TOKEN_TILE
    lead = lambda i: (i, 0, 0)
    out_specs = [pl.BlockSpec((MIX_TILE, d), row),
                 pl.BlockSpec((MIX_TILE, d), row),
                 pl.BlockSpec((N_EXPERTS, MIX_TILE), col),
                 pl.BlockSpec((N_EXPERTS, MIX_TILE), col),
                 pl.BlockSpec((N_EXPERTS, MIX_TILE), col),
                 pl.BlockSpec((sub, N_EXPERTS, LANES), lead),
                 pl.BlockSpec((sub, SUBLANES, LANES), lead)]
    out_shape = [jax.ShapeDtypeStruct((n, d), F32),
                 jax.ShapeDtypeStruct((n, d), F32),
                 jax.ShapeDtypeStruct((N_EXPERTS, n), BF16),
                 jax.ShapeDtypeStruct((N_EXPERTS, n), BF16),
                 jax.ShapeDtypeStruct((N_EXPERTS, n), BF16),
                 jax.ShapeDtypeStruct((n // TOKEN_TILE, N_EXPERTS, LANES), F32),
                 jax.ShapeDtypeStruct((n // TOKEN_TILE, SUBLANES, LANES), F32)]
    return out_specs, out_shape


def _rglru_out_kernel(yf_ref, yr_ref, gate_ref, x0_ref, mod_ref, wout_ref, g2_ref, rwt_ref, rb_ref,
                      x1_ref, hx2_ref, hi_ref, lo_ref, gate_out_ref, tab_ref, tabt_ref, *, tiles_per_seq):
    d = x0_ref.shape[1]
    row = pl.program_id(0) // tiles_per_seq
    yx = yf_ref[...] + yr_ref[...]
    v = gate_ref[...].astype(F32) * yx
    out = jnp.dot(v.astype(BF16), wout_ref[...], preferred_element_type=F32)
    x1 = x0_ref[...] + _mod_chunk(mod_ref, row, 2, d) * out
    x1_ref[...] = x1
    _ffn_pre(x1, mod_ref, row, g2_ref, rwt_ref, rb_ref, hx2_ref, hi_ref, lo_ref, gate_out_ref, tab_ref, tabt_ref)


def _rglru_out(y_fwd, y_rev, gate, x0, mod, w_out, g2, rwt, rb, tiles_per_seq):
    n, d = x0.shape
    c = gate.shape[1]
    row = lambda i: (i, 0)
    const = lambda i: (0, 0)
    out_specs, out_shape = _ffn_pre_specs(n, d)
    return pl.pallas_call(
        functools.partial(_rglru_out_kernel, tiles_per_seq=tiles_per_seq),
        grid=(n // MIX_TILE,),
        in_specs=[pl.BlockSpec((MIX_TILE, c), row),
                  pl.BlockSpec((MIX_TILE, c), row),
                  pl.BlockSpec((MIX_TILE, c), row),
                  pl.BlockSpec((MIX_TILE, d), row),
                  pl.BlockSpec(mod.shape, const),
                  pl.BlockSpec(w_out.shape, const),
                  pl.BlockSpec((1, d), const),
                  pl.BlockSpec(rwt.shape, const),
                  pl.BlockSpec(rb.shape, const)],
        out_specs=out_specs,
        out_shape=out_shape,
        compiler_params=_params(("arbitrary",)),
        name="rglru_out",
    )(y_fwd, y_rev, gate, x0, mod, w_out, g2, rwt, rb)


def _sgu_kernel(x_ref, mod_ref, g_ref, win_ref, lng_ref, lnb_ref, ws_ref, bst_ref, wout_ref,
                g2_ref, rwt_ref, rb_ref,
                x1_ref, hx2_ref, hi_ref, lo_ref, gate_out_ref, tab_ref, tabt_ref, m_scr, *, tiles_per_seq):
    t_rows, d = x_ref.shape
    w = wout_ref.shape[0]
    gd = w // SGU_HEADS
    row = pl.program_id(0) // tiles_per_seq
    x = x_ref[...]
    hx = _rms_mod(x, g_ref[...], _mod_chunk(mod_ref, row, 1, d), _mod_chunk(mod_ref, row, 0, d))
    z = jax.nn.gelu(jnp.dot(hx.astype(BF16), win_ref[...], preferred_element_type=F32))
    u = z[:, :w]
    v = z[:, w:]
    mu = jnp.mean(v, axis=-1, keepdims=True)
    vc = v - mu
    v = vc * lax.rsqrt(jnp.mean(vc * vc, axis=-1, keepdims=True) + NORM_EPS) * lng_ref[...] + lnb_ref[...]
    vb = v.astype(BF16)
    for ch in range(t_rows // CHUNK):
        rs = slice(ch * CHUNK, (ch + 1) * CHUNK)
        for g in range(SGU_HEADS):
            cs = slice(g * gd, (g + 1) * gd)
            sv = jnp.dot(ws_ref[g], vb[rs, cs], preferred_element_type=F32) + bst_ref[:, g:g + 1]
            m_scr[rs, cs] = (u[rs, cs] * sv).astype(BF16)
    out = jnp.dot(m_scr[...], wout_ref[...], preferred_element_type=F32)
    x1 = x + _mod_chunk(mod_ref, row, 2, d) * out
    x1_ref[...] = x1
    _ffn_pre(x1, mod_ref, row, g2_ref, rwt_ref, rb_ref, hx2_ref, hi_ref, lo_ref, gate_out_ref, tab_ref, tabt_ref)


def _sgu(x, mod, g, w_in, ln_g, ln_b, w_s, b_st, w_out, g2, rwt, rb, tiles_per_seq):
    n, d = x.shape
    w = w_out.shape[0]
    const = lambda i: (0, 0)
    out_specs, out_shape = _ffn_pre_specs(n, d)
    return pl.pallas_call(
        functools.partial(_sgu_kernel, tiles_per_seq=tiles_per_seq),
        grid=(n // MIX_TILE,),
        in_specs=[pl.BlockSpec((MIX_TILE, d), lambda i: (i, 0)),
                  pl.BlockSpec(mod.shape, const),
                  pl.BlockSpec((1, d), const),
                  pl.BlockSpec(w_in.shape, const, pipeline_mode=pl.Buffered(1)),
                  pl.BlockSpec((1, w), const),
                  pl.BlockSpec((1, w), const),
                  pl.BlockSpec(w_s.shape, lambda i: (0, 0, 0)),
                  pl.BlockSpec(b_st.shape, const),
                  pl.BlockSpec(w_out.shape, const, pipeline_mode=pl.Buffered(1)),
                  pl.BlockSpec((1, d), const),
                  pl.BlockSpec(rwt.shape, const),
                  pl.BlockSpec(rb.shape, const)],
        out_specs=out_specs,
        out_shape=out_shape,
        scratch_shapes=[pltpu.VMEM((MIX_TILE, w), BF16)],
        compiler_params=_params(("arbitrary",)),
        name="sgu",
    )(x, mod, g, w_in, ln_g, ln_b, w_s, b_st, w_out, g2, rwt, rb)


def _sorted_layout(m, eidx, gw):
    e, t = m.shape
    lanes = LANES
    r = lax.broadcasted_iota(jnp.int32, (t, t), 0)
    c = lax.broadcasted_iota(jnp.int32, (t, t), 1)
    upper = jnp.where(r <= c, 1.0, 0.0).astype(BF16)
    incl = jnp.dot(m.astype(BF16), upper, preferred_element_type=F32)
    run = _round_up_rows(incl[:, t - 1:t])
    re = lax.broadcasted_iota(jnp.int32, (e, e), 0)
    ce = lax.broadcasted_iota(jnp.int32, (e, e), 1)
    lower = jnp.where(ce < re, 1.0, 0.0).astype(BF16)
    tiles = jnp.broadcast_to(run * (1.0 / ROW_ALIGN), (e, lanes)).astype(BF16)
    lstart = jnp.dot(lower, tiles, preferred_element_type=F32)[:, 0:1] * float(ROW_ALIGN)
    pos = jnp.where(m > 0.0, lstart + incl - m, float(POS_RADIX * POS_RADIX - 1))
    hi = jnp.floor(pos * (1.0 / POS_RADIX))
    lo = pos - hi * float(POS_RADIX)
    iota_e = lax.broadcasted_iota(jnp.int32, (e, t), 0).astype(F32)
    gate = jnp.zeros((e, t), F32)
    for k in range(TOP_K):
        gate = jnp.where(iota_e == eidx[k:k + 1, :], gw[k:k + 1, :], gate)
    lane = lax.broadcasted_iota(jnp.int32, (e, lanes), 1)
    tab = jnp.where(lane == 0, run, jnp.where(lane == 2, lstart, 0.0))
    diag = lax.broadcasted_iota(jnp.int32, (e, lanes), 0) == lane
    row_of = lambda col: jnp.sum(jnp.where(diag, jnp.broadcast_to(col, (e, lanes)), 0.0), axis=0, keepdims=True)
    sub = lax.broadcasted_iota(jnp.int32, (SUBLANES, lanes), 0)
    tabt = jnp.where(sub == 0, row_of(lstart), jnp.where(sub == 1, row_of(lstart + run), 0.0))
    return hi.astype(BF16), lo.astype(BF16), gate.astype(BF16), tab, tabt


def _aligned(v):
    return pl.multiple_of(v, ROW_ALIGN)


def _run_copies(tab_ref, make_copy, unroll):
    def body(e, carry):
        make_copy(_aligned(tab_ref[TAB_LOCAL + e]), _aligned(tab_ref[N_EXPERTS + e]),
                  _aligned(tab_ref[e])).start()
        return carry

    lax.fori_loop(0, N_EXPERTS, body, 0, unroll=unroll)


def _dispatch_kernel(tab_ref, ztab_ref, tabt_ref, hi_ref, lo_ref, x_ref, xs_hbm, sbuf, zbuf, sems, zsem):
    i = pl.program_id(0)
    slot = i % 2
    t = x_ref.shape[0]
    rows = sbuf.shape[1]

    def zero_copy(e):
        n = _aligned(ztab_ref[e])
        return pltpu.make_async_copy(zbuf.at[pl.ds(0, n)], xs_hbm.at[pl.ds(_aligned(ztab_ref[N_EXPERTS + e]), n)], zsem)

    def for_zero_runs(fn):
        def body(e, c):
            @pl.when(ztab_ref[e] > 0)
            def _():
                fn(zero_copy(e))
            return c
        lax.fori_loop(0, N_EXPERTS, body, 0)

        def tail(b, c):
            fn(pltpu.make_async_copy(zbuf, xs_hbm.at[pl.ds(pl.multiple_of(b * EXPERT_BLOCK, EXPERT_BLOCK),
                                                           EXPERT_BLOCK)], zsem))
            return c
        lax.fori_loop(ztab_ref[2 * N_EXPERTS], xs_hbm.shape[0] // EXPERT_BLOCK, tail, 0)

    @pl.when(i == 0)
    def _():
        zbuf[...] = jnp.zeros_like(zbuf)
        for_zero_runs(lambda cp: cp.start())

    xb = x_ref[...].astype(BF16)
    total = tab_ref[2 * N_EXPERTS]
    run_lo = tabt_ref[0, 0:1, 0:N_EXPERTS]
    run_hi = tabt_ref[0, 1:2, 0:N_EXPERTS]

    def permute(r0):
        j_e = (lax.broadcasted_iota(jnp.int32, (PERM_CHUNK, N_EXPERTS), 0) + r0).astype(F32)
        owner = jnp.where((j_e >= run_lo) & (j_e < run_hi), 1.0, 0.0).astype(BF16)
        pos = (jnp.dot(owner, hi_ref[...], preferred_element_type=F32) * float(POS_RADIX)
               + jnp.dot(owner, lo_ref[...], preferred_element_type=F32))
        j_t = (lax.broadcasted_iota(jnp.int32, (PERM_CHUNK, t), 0) + r0).astype(F32)
        p = jnp.where(pos == j_t, 1.0, 0.0).astype(BF16)
        sbuf[slot, r0:r0 + PERM_CHUNK, :] = jnp.dot(p, xb, preferred_element_type=F32).astype(BF16)

    for r0 in range(0, rows, PERM_CHUNK):
        if r0 < t * TOP_K + PERM_CHUNK:
            permute(r0)
        else:
            pl.when(total > r0)(functools.partial(permute, r0))

    _run_copies(tab_ref, lambda loc, glob, n: pltpu.make_async_copy(
        sbuf.at[slot, pl.ds(loc, n)], xs_hbm.at[pl.ds(glob, n)], sems.at[slot]), unroll=True)

    def wait_rows(s, n):
        pltpu.make_async_copy(sbuf.at[s, pl.ds(0, n)], xs_hbm.at[pl.ds(0, n)], sems.at[s]).wait()

    @pl.when(i > 0)
    def _():
        wait_rows(1 - slot, _aligned(tab_ref[2 * N_EXPERTS + 1]))

    @pl.when(i == pl.num_programs(0) - 1)
    def _():
        wait_rows(slot, _aligned(tab_ref[2 * N_EXPERTS]))

    @pl.when(i == 0)
    def _():
        for_zero_runs(lambda cp: cp.wait())


def _dispatch(tab, ztab, tab_t, pos_hi, pos_lo, hx2, xs_rows):
    n, d = hx2.shape
    col = lambda i: (0, i)
    return pl.pallas_call(
        _dispatch_kernel,
        grid=(n // TOKEN_TILE,),
        in_specs=[pl.BlockSpec((TAB_WIDTH,), lambda i: (i,), memory_space=pltpu.SMEM),
                  pl.BlockSpec(memory_space=pltpu.SMEM),
                  pl.BlockSpec((1, SUBLANES, LANES), lambda i: (i, 0, 0)),
                  pl.BlockSpec((N_EXPERTS, TOKEN_TILE), col),
                  pl.BlockSpec((N_EXPERTS, TOKEN_TILE), col),
                  pl.BlockSpec((TOKEN_TILE, d), lambda i: (i, 0))],
        out_specs=pl.BlockSpec(memory_space=pl.ANY),
        out_shape=jax.ShapeDtypeStruct((xs_rows, d), BF16),
        scratch_shapes=[pltpu.VMEM((2, SORTED_ROWS, d), BF16),
                        pltpu.VMEM((EXPERT_BLOCK, d), BF16),
                        pltpu.SemaphoreType.DMA((2,)),
                        pltpu.SemaphoreType.DMA],
        compiler_params=_params(("arbitrary",)),
        name="moe_dispatch",
    )(tab, ztab, tab_t, pos_hi, pos_lo, hx2)


def _experts_kernel(be_ref, nb_ref, rows_ref, xs_ref, wgu_ref, wd_ref, ys_ref, wgu_b, wd_b):
    del nb_ref
    i = pl.program_id(0)
    ff = wd_b.shape[0]
    n_rows = rows_ref[i]
    new_expert = (i == 0) | (be_ref[i] != be_ref[jnp.maximum(i - 1, 0)])

    @pl.when((n_rows > 0) & new_expert)
    def _():
        wgu_b[...] = wgu_ref[0, 0].astype(BF16)
        wd_b[...] = wd_ref[0, 0].astype(BF16)

    for r0 in range(0, xs_ref.shape[0], EXPERT_HALF):
        @pl.when(n_rows > r0)
        def _():
            h = jnp.dot(xs_ref[r0:r0 + EXPERT_HALF, :], wgu_b[...], preferred_element_type=F32)
            a = _silu(h[:, :ff]) * h[:, ff:]
            ys_ref[r0:r0 + EXPERT_HALF, :] = jnp.dot(a.astype(BF16), wd_b[...],
                                                     preferred_element_type=F32).astype(BF16)

        @pl.when(n_rows <= r0)
        def _():
            ys_ref[r0:r0 + EXPERT_HALF, :] = jnp.zeros((EXPERT_HALF, ys_ref.shape[1]), ys_ref.dtype)


def _experts(block_e, n_used, block_rows, xs, w_gu, w_down, layer):
    rows, d = xs.shape
    nb = rows // EXPERT_BLOCK
    blk = lambda i, be, nu, br: (jnp.maximum(jnp.minimum(i, nu[0] - 1), 0), 0)
    wmap = lambda i, be, nu, br: (layer, be[i], 0, 0)
    return pl.pallas_call(
        _experts_kernel,
        grid_spec=pltpu.PrefetchScalarGridSpec(
            num_scalar_prefetch=3,
            grid=(nb,),
            in_specs=[pl.BlockSpec((EXPERT_BLOCK, d), blk),
                      pl.BlockSpec((1, 1) + w_gu.shape[2:], wmap),
                      pl.BlockSpec((1, 1) + w_down.shape[2:], wmap)],
            out_specs=pl.BlockSpec((EXPERT_BLOCK, d), lambda i, be, nu, br: (i, 0)),
            scratch_shapes=[pltpu.VMEM(w_gu.shape[2:], BF16), pltpu.VMEM(w_down.shape[2:], BF16)]),
        out_shape=jax.ShapeDtypeStruct((rows, d), BF16),
        compiler_params=_params(("arbitrary",)),
        name="moe_experts",
    )(block_e, n_used, block_rows, xs, w_gu, w_down)


def _combine_kernel(tab_ref, tabn_ref, tabv_ref, hi_ref, lo_ref, gate_ref, ys_hbm, hx2_ref, x1_ref, mod_ref,
                    wsgu_ref, wsd_ref, fg_ref, out_ref, ybuf, sems, *, tiles_per_seq, final_norm):
    t_rows, d = x1_ref.shape
    ff = wsd_ref.shape[0]
    rows = ybuf.shape[1]
    i = pl.program_id(0)
    last = pl.num_programs(0) - 1
    slot = i % 2
    row = i // tiles_per_seq

    def gather(table, s, unroll):
        _run_copies(table, lambda loc, glob, n: pltpu.make_async_copy(
            ys_hbm.at[pl.ds(glob, n)], ybuf.at[s, pl.ds(loc, n)], sems.at[s]), unroll=unroll)

    def wait_rows(table, s):
        n = _aligned(table[2 * N_EXPERTS])
        pltpu.make_async_copy(ys_hbm.at[pl.ds(0, n)], ybuf.at[s, pl.ds(0, n)], sems.at[s]).wait()

    @pl.when(i == 0)
    def _():
        ybuf[...] = jnp.zeros_like(ybuf)
        gather(tab_ref, 0, False)

    gather(tabn_ref, 1 - slot, True)

    hs = jnp.dot(hx2_ref[...].astype(BF16), wsgu_ref[...], preferred_element_type=F32)
    shared = jnp.dot((_silu(hs[:, :ff]) * hs[:, ff:]).astype(BF16), wsd_ref[...], preferred_element_type=F32)

    run_lo = tabv_ref[0, :, 2:3]
    run_hi = run_lo + tabv_ref[0, :, 0:1]
    j_e = lax.broadcasted_iota(jnp.int32, (N_EXPERTS, rows), 1).astype(F32)
    owner = jnp.where((j_e >= run_lo) & (j_e < run_hi), 1.0, 0.0).astype(BF16)
    pos = (jnp.dot(hi_ref[...], owner, preferred_element_type=F32) * float(POS_RADIX)
           + jnp.dot(lo_ref[...], owner, preferred_element_type=F32))
    gates = jnp.dot(gate_ref[...], owner, preferred_element_type=F32)
    j_t = lax.broadcasted_iota(jnp.int32, (t_rows, rows), 1).astype(F32)
    gb = jnp.where(pos == j_t, gates, 0.0).astype(BF16)

    wait_rows(tab_ref, slot)
    routed = jnp.dot(gb, ybuf[slot], preferred_element_type=F32)
    x2 = x1_ref[...] + _mod_chunk(mod_ref, row, 5, d) * (routed + shared)
    if final_norm:
        x2 = x2 * lax.rsqrt(jnp.mean(x2 * x2, axis=-1, keepdims=True) + NORM_EPS) * fg_ref[...]
    out_ref[...] = x2

    @pl.when(i == last)
    def _():
        wait_rows(tabn_ref, 1 - slot)


def _combine(tab, tab_v, pos_hi, pos_lo, gate, ys, hx2, x1, mod, ws_gu, ws_down, fg, tiles_per_seq, final_norm):
    n, d = x1.shape
    row = lambda i: (i, 0)
    const = lambda i: (0, 0)
    return pl.pallas_call(
        functools.partial(_combine_kernel, tiles_per_seq=tiles_per_seq, final_norm=final_norm),
        grid=(n // TOKEN_TILE,),
        in_specs=[pl.BlockSpec((TAB_WIDTH,), lambda i: (i,), memory_space=pltpu.SMEM),
                  pl.BlockSpec((TAB_WIDTH,), lambda i: (jnp.minimum(i + 1, n // TOKEN_TILE - 1),),
                               memory_space=pltpu.SMEM),
                  pl.BlockSpec((1, N_EXPERTS, LANES), lambda i: (i, 0, 0)),
                  pl.BlockSpec((TOKEN_TILE, N_EXPERTS), row),
                  pl.BlockSpec((TOKEN_TILE, N_EXPERTS), row),
                  pl.BlockSpec((TOKEN_TILE, N_EXPERTS), row),
                  pl.BlockSpec(memory_space=pl.ANY),
                  pl.BlockSpec((TOKEN_TILE, d), row),
                  pl.BlockSpec((TOKEN_TILE, d), row),
                  pl.BlockSpec(mod.shape, const),
                  pl.BlockSpec(ws_gu.shape, const),
                  pl.BlockSpec(ws_down.shape, const),
                  pl.BlockSpec((1, d), const)],
        out_specs=pl.BlockSpec((TOKEN_TILE, d), row),
        out_shape=jax.ShapeDtypeStruct((n, d), F32),
        scratch_shapes=[pltpu.VMEM((2, SORTED_ROWS, d), BF16), pltpu.SemaphoreType.DMA((2,))],
        compiler_params=_params(("arbitrary",)),
        name="moe_combine",
    )(tab, tab, tab_v, pos_hi, pos_lo, gate, ys, hx2, x1, mod, ws_gu, ws_down, fg)


def _moe(x1, hx2, pos_hi, pos_lo, gate, tab_f, tab_t, mod, w_gu, w_down, layer, ws_gu, ws_down, fg,
         tiles_per_seq, final_norm):
    n, d = x1.shape
    n_tiles = n // TOKEN_TILE
    run_len = tab_f[:, :, 0].astype(jnp.int32)
    counts = jnp.sum(run_len, axis=0)
    padded = (counts + EXPERT_BLOCK - 1) // EXPERT_BLOCK * EXPERT_BLOCK
    pad_end = jnp.cumsum(padded)
    pad_start = pad_end - padded
    max_rows = n * TOP_K + n_tiles * N_EXPERTS * ROW_ALIGN + N_EXPERTS * (EXPERT_BLOCK - ROW_ALIGN)
    n_blocks = (max_rows + EXPERT_BLOCK - 1) // EXPERT_BLOCK
    n_used = (pad_end[-1:] // EXPERT_BLOCK).astype(jnp.int32)
    block_start = jnp.arange(n_blocks, dtype=jnp.int32) * EXPERT_BLOCK
    block_e = jnp.minimum(jnp.sum((pad_end[None, :] <= block_start[:, None]).astype(jnp.int32), axis=1),
                          N_EXPERTS - 1)
    block_rows = jnp.where(block_start < pad_end[-1],
                           jnp.clip((pad_start + counts)[block_e] - block_start, 0, EXPERT_BLOCK), 0)

    run_start = pad_start[None, :] + jnp.cumsum(run_len, axis=0) - run_len
    total = jnp.sum(run_len, axis=1, keepdims=True)
    prev_total = jnp.concatenate([jnp.zeros((1, 1), jnp.int32), total[:-1]], axis=0)
    run_local = tab_f[:, :, 2].astype(jnp.int32)
    fill = jnp.zeros((n_tiles, TAB_WIDTH - 3 * N_EXPERTS - 2), jnp.int32)
    tab = jnp.concatenate([run_len, run_start, total, prev_total, run_local, fill], axis=1).reshape(-1)
    ztab = jnp.concatenate([padded - counts, pad_start + counts, n_used])

    xs = _dispatch(tab, ztab, tab_t, pos_hi, pos_lo, hx2, n_blocks * EXPERT_BLOCK)
    ys = _experts(block_e, n_used, block_rows.astype(jnp.int32), xs, w_gu, w_down, layer)
    return _combine(tab, tab_f, pos_hi.T, pos_lo.T, gate.T, ys, hx2, x1, mod, ws_gu, ws_down, fg,
                    tiles_per_seq, final_norm)


def _sincos_2d(rows, d):
    quarter = d // 4
    omega = 1.0 / (POS_BASE ** (jnp.arange(quarter, dtype=F32) / quarter))

    def emb(n):
        p = jnp.arange(n, dtype=F32)[:, None] * omega[None, :]
        return jnp.concatenate([jnp.sin(p), jnp.cos(p)], axis=-1)

    er, ec = emb(rows), emb(GRID_W)
    pe = jnp.concatenate([jnp.broadcast_to(er[:, None, :], (rows, GRID_W, d // 2)),
                          jnp.broadcast_to(ec[None, :, :], (rows, GRID_W, d // 2))], axis=-1)
    return pe.reshape(rows * GRID_W, d)


def kernel(x, c, ctx, c_ctx, ada_w, ada_b, mix_norm_g, ffn_norm_g, a_w_in, a_conv_w, a_conv_b, a_gate_r_w, a_gate_r_b, a_gate_i_w, a_gate_i_b, a_lambda, a_w_out, b_w_in, b_ln_g, b_ln_b, b_w_s, b_b_s, b_w_out, router_w, router_b, moe_w_gu, moe_w_down, shared_w_gu, shared_w_down, final_norm_g):
    bsz, s, d = x.shape
    ctx_len = ctx.shape[1]
    depth = ada_w.shape[0]
    assert depth == 2 and bsz < MOD_ROWS and s % MIX_TILE == 0 and MIX_TILE % TOKEN_TILE == 0
    assert s % SCAN_TILE == 0
    assert ctx_len % TOKEN_TILE == 0
    n = bsz * s
    tps = s // TOKEN_TILE
    ctx_row = bsz

    cc = jnp.zeros((MOD_ROWS, d), F32).at[:bsz].set(c).at[ctx_row].set(c_ctx)
    mod = _modulation(cc, ada_w, ada_b)
    pe = _sincos_2d(s // GRID_W, d)
    rc = a_w_in.shape[2] // 2

    w_in0 = a_w_in[0].astype(BF16)
    g_mix0 = mix_norm_g[0].reshape(1, d)
    x0, gate, ux = _rglru_in(x.reshape(n, d), pe, mod[0], g_mix0, w_in0)
    uc = _ctx_in(ctx.reshape(bsz * ctx_len, d), mod[0], g_mix0, w_in0[:, rc:], ctx_row)
    w_ri = jnp.concatenate([a_gate_r_w[0], a_gate_i_w[0]], axis=-1).astype(BF16)
    conv = (a_conv_w[0], a_conv_b[0])
    gates = [(w_ri[k], a_gate_r_b[0, k], a_gate_i_b[0, k], a_lambda[0, k]) for k in range(2)]
    h_zero = jnp.zeros((bsz, rc), F32)
    h_fwd, uc = _lru_scan(uc, conv, *gates[0], h_zero, reverse=False, reset_first=True, emit_y=False)
    h_rev = _lru_scan(uc, None, *gates[1], h_zero, reverse=True, reset_first=True, emit_y=False)
    y_fwd, ux = _lru_scan(ux, conv, *gates[0], h_fwd, reverse=False, reset_first=False, emit_y=True)
    y_rev = _lru_scan(ux, None, *gates[1], h_rev, reverse=True, reset_first=False, emit_y=True)
    pre = _rglru_out(y_fwd, y_rev, gate, x0, mod[0], a_w_out[0].astype(BF16), ffn_norm_g[0].reshape(1, d),
                     router_w[0].T, router_b[0].reshape(N_EXPERTS, 1), s // MIX_TILE)
    x1 = _moe(*pre, mod[0], moe_w_gu, moe_w_down, 0, shared_w_gu[0].astype(BF16),
              shared_w_down[0].astype(BF16), final_norm_g.reshape(1, d), tps, False)

    pre = _sgu(x1, mod[1], mix_norm_g[1].reshape(1, d), b_w_in[0].astype(BF16),
               b_ln_g[0].reshape(1, -1), b_ln_b[0].reshape(1, -1), b_w_s[0].astype(BF16), b_b_s[0].T,
               b_w_out[0].astype(BF16), ffn_norm_g[1].reshape(1, d),
               router_w[1].T, router_b[1].reshape(N_EXPERTS, 1), s // MIX_TILE)
    out = _moe(*pre, mod[1], moe_w_gu, moe_w_down, 1, shared_w_gu[1].astype(BF16),
               shared_w_down[1].astype(BF16), final_norm_g.reshape(1, d), tps, True)
    return out.reshape(bsz, s, d)
```

```python
import functools

import jax
import jax.numpy as jnp
from jax import lax
from jax.experimental import pallas as pl
from jax.experimental.pallas import tpu as pltpu

F32 = jnp.float32
BF16 = jnp.bfloat16
HIGHEST = lax.Precision.HIGHEST

GRID_W = 64
N_MOD = 6
NORM_EPS = 1e-6
POS_BASE = 10000.0
RNN_HEADS = 5
CONV_WIDTH = 4
CONV_PAD_LEFT = 2
LRU_C = 8.0
SGU_HEADS = 8
CHUNK = 128
N_EXPERTS = 64
TOP_K = 8
N_GROUPS = 8
TOPK_GROUPS = 4
EXPERTS_PER_GROUP = N_EXPERTS // N_GROUPS
ROUTED_SCALE = 2.5

SUBLANES = 8
ROW_ALIGN = 16
LANES = 128
MOD_ROWS = 8
TOKEN_TILE = 256
MIX_TILE = 512
SCAN_TILE = 512
EXPERT_BLOCK = 1024
SORTED_ROWS = TOKEN_TILE * TOP_K + N_EXPERTS * ROW_ALIGN
PERM_CHUNK = 512
TAB_WIDTH = 256
TAB_LOCAL = 2 * N_EXPERTS + 2
POS_RADIX = 64
VMEM_LIMIT = 56 * 1024 * 1024


def _params(semantics, vmem=VMEM_LIMIT):
    return pltpu.CompilerParams(dimension_semantics=semantics, vmem_limit_bytes=vmem)


def _silu(x):
    return x * jax.nn.sigmoid(x)


def _rms_mod(x, g, sc, sh):
    y = x * lax.rsqrt(jnp.mean(x * x, axis=-1, keepdims=True) + NORM_EPS)
    return (y * g) * (1.0 + sc) + sh


def _mod_chunk(mod_ref, row, k, d):
    return mod_ref[pl.ds(row, 1), k * d:(k + 1) * d]


def _round_up_rows(count):
    return jnp.maximum(jnp.ceil(count * (1.0 / ROW_ALIGN)), 1.0) * float(ROW_ALIGN)


def _mod_kernel(cc_ref, w_ref, b_ref, o_ref):
    s = _silu(cc_ref[...])
    o_ref[0] = jnp.dot(s, w_ref[0], preferred_element_type=F32, precision=HIGHEST) + b_ref[0]


def _modulation(cc, ada_w, ada_b):
    depth, d, nd = ada_w.shape
    return pl.pallas_call(
        _mod_kernel,
        grid=(depth, nd // d),
        in_specs=[pl.BlockSpec((MOD_ROWS, d), lambda l, j: (0, 0)),
                  pl.BlockSpec((1, d, d), lambda l, j: (l, 0, j)),
                  pl.BlockSpec((1, 1, d), lambda l, j: (l, 0, j))],
        out_specs=pl.BlockSpec((1, MOD_ROWS, d), lambda l, j: (l, 0, j)),
        out_shape=jax.ShapeDtypeStruct((depth, MOD_ROWS, nd), F32),
        compiler_params=_params(("arbitrary", "arbitrary")),
        name="modulation",
    )(cc, ada_w, ada_b.reshape(depth, 1, nd))


def _rglru_in_kernel(x_ref, pe_ref, mod_ref, g_ref, w_ref, x0_ref, gate_ref, u_ref, *, tiles_per_seq):
    d = x_ref.shape[1]
    c = u_ref.shape[1]
    row = pl.program_id(0) // tiles_per_seq
    x = x_ref[...] + pe_ref[...]
    hx = _rms_mod(x, g_ref[...], _mod_chunk(mod_ref, row, 1, d), _mod_chunk(mod_ref, row, 0, d))
    z = jnp.dot(hx.astype(BF16), w_ref[...], preferred_element_type=F32)
    x0_ref[...] = x
    gate_ref[...] = jax.nn.gelu(z[:, :c]).astype(BF16)
    u_ref[...] = z[:, c:]


def _rglru_in(x2, pe, mod, g, w_in):
    n, d = x2.shape
    c = w_in.shape[1] // 2
    s = pe.shape[0]
    tps = s // MIX_TILE
    row = lambda i: (i, 0)
    return pl.pallas_call(
        functools.partial(_rglru_in_kernel, tiles_per_seq=tps),
        grid=(n // MIX_TILE,),
        in_specs=[pl.BlockSpec((MIX_TILE, d), row),
                  pl.BlockSpec((MIX_TILE, d), lambda i: (i % tps, 0)),
                  pl.BlockSpec(mod.shape, lambda i: (0, 0)),
                  pl.BlockSpec((1, d), lambda i: (0, 0)),
                  pl.BlockSpec(w_in.shape, lambda i: (0, 0))],
        out_specs=[pl.BlockSpec((MIX_TILE, d), row),
                   pl.BlockSpec((MIX_TILE, c), row),
                   pl.BlockSpec((MIX_TILE, c), row)],
        out_shape=[jax.ShapeDtypeStruct((n, d), F32),
                   jax.ShapeDtypeStruct((n, c), BF16),
                   jax.ShapeDtypeStruct((n, c), F32)],
        compiler_params=_params(("arbitrary",)),
        name="rglru_in",
    )(x2, pe, mod, g, w_in)


def _ctx_in_kernel(x_ref, mod_ref, g_ref, w_ref, u_ref, *, ctx_row):
    d = x_ref.shape[1]
    hx = _rms_mod(x_ref[...], g_ref[...], _mod_chunk(mod_ref, ctx_row, 1, d),
                  _mod_chunk(mod_ref, ctx_row, 0, d))
    u_ref[...] = jnp.dot(hx.astype(BF16), w_ref[...], preferred_element_type=F32)


def _ctx_in(c2, mod, g, w_u, ctx_row):
    n, d = c2.shape
    c = w_u.shape[1]
    return pl.pallas_call(
        functools.partial(_ctx_in_kernel, ctx_row=ctx_row),
        grid=(n // TOKEN_TILE,),
        in_specs=[pl.BlockSpec((TOKEN_TILE, d), lambda i: (i, 0)),
                  pl.BlockSpec(mod.shape, lambda i: (0, 0)),
                  pl.BlockSpec((1, d), lambda i: (0, 0)),
                  pl.BlockSpec(w_u.shape, lambda i: (0, 0))],
        out_specs=pl.BlockSpec((TOKEN_TILE, c), lambda i: (i, 0)),
        out_shape=jax.ShapeDtypeStruct((n, c), F32),
        compiler_params=_params(("arbitrary",)),
        name="ctx_in",
    )(c2, mod, g, w_u)


def _log_sigmoid(x):
    return jnp.minimum(x, 0.0) - jnp.log1p(jnp.exp(-jnp.abs(x)))


def _lru_scan_kernel(*refs, n_tiles, reverse, conv_done, reset_first, emit_y):
    if conv_done:
        u_ref, wri_ref, rb_ref, ib_ref, lam_ref, h0_ref, out_ref, a_scr, b_scr, h_scr = refs
    else:
        (u_ref, up_ref, un_ref, cw_ref, cb_ref, wri_ref, rb_ref, ib_ref, lam_ref, h0_ref,
         out_ref, uc_ref, ubuf, a_scr, b_scr, h_scr) = refs
    t_rows, c = u_ref.shape
    hb = c // RNN_HEADS
    b = pl.program_id(0)
    j = pl.program_id(1)
    jj = n_tiles - 1 - j if reverse else j

    if conv_done:
        u = u_ref[...]
    else:
        ubuf[SUBLANES:SUBLANES + t_rows, :] = u_ref[...]
        ubuf[0:SUBLANES, :] = jnp.where(jj == 0, 0.0, up_ref[...])
        ubuf[SUBLANES + t_rows:, :] = jnp.where(jj == n_tiles - 1, 0.0, un_ref[...])
        u = cb_ref[...]
        for k in range(CONV_WIDTH):
            u = u + cw_ref[k:k + 1, :] * ubuf[pl.ds(SUBLANES - CONV_PAD_LEFT + k, t_rows), :]
        uc_ref[...] = u

    log_lam = LRU_C * _log_sigmoid(lam_ref[...])
    rows = lax.broadcasted_iota(jnp.int32, (t_rows, 1), 0)
    first_row = jnp.where(j == 0, t_rows - 1 if reverse else 0, -1)
    for h in range(RNN_HEADS):
        sl = slice(h * hb, (h + 1) * hb)
        uh = u[:, sl]
        z = jnp.dot(uh.astype(BF16), wri_ref[h], preferred_element_type=F32)
        r = jax.nn.sigmoid(z[:, :hb] + rb_ref[:, sl])
        ig = jax.nn.sigmoid(z[:, hb:] + ib_ref[:, sl])
        log_a = r * log_lam[:, sl]
        a = jnp.exp(log_a)
        mult = jnp.sqrt((1.0 - a) * (1.0 + a))
        if reset_first:
            mult = jnp.where(rows == first_row, 1.0, mult)
        a_scr[:, sl] = a
        b_scr[:, sl] = mult * ig * uh

    @pl.when(j == 0)
    def _():
        h_scr[...] = h0_ref[pl.ds(b, 1), :]

    n_groups = t_rows // SUBLANES

    def group(g, h):
        base = pl.multiple_of((n_groups - 1 - g if reverse else g) * SUBLANES, SUBLANES)
        for s in range(SUBLANES):
            r = base + (SUBLANES - 1 - s if reverse else s)
            h = a_scr[pl.ds(r, 1), :] * h + b_scr[pl.ds(r, 1), :]
            if emit_y:
                out_ref[pl.ds(r, 1), :] = h
        return h

    h = lax.fori_loop(0, n_groups, group, h_scr[...])
    h_scr[...] = h
    if not emit_y:
        @pl.when(j == n_tiles - 1)
        def _():
            out_ref[pl.ds(b, 1), :] = h


def _lru_scan(u, conv, w_ri, r_b, i_b, lam, h0, *, reverse, reset_first, emit_y):
    n, c = u.shape
    n_batch = h0.shape[0]
    rows = min(SCAN_TILE, n // n_batch)
    n_tiles = n // n_batch // rows
    sub = rows // SUBLANES
    n_sub = n // SUBLANES
    const = lambda b, j: (0, 0)

    def tile(b, j):
        return b * n_tiles + (n_tiles - 1 - j if reverse else j)

    tile_spec = pl.BlockSpec((rows, c), lambda b, j: (tile(b, j), 0))
    in_specs = [tile_spec]
    args = [u]
    scratch = []
    if conv is not None:
        conv_w, conv_b = conv
        in_specs += [pl.BlockSpec((SUBLANES, c), lambda b, j: (jnp.maximum(tile(b, j) * sub - 1, 0), 0)),
                     pl.BlockSpec((SUBLANES, c), lambda b, j: (jnp.minimum((tile(b, j) + 1) * sub, n_sub - 1), 0)),
                     pl.BlockSpec(conv_w.shape, const),
                     pl.BlockSpec((1, c), const)]
        args += [u, u, conv_w, conv_b.reshape(1, c)]
        scratch = [pltpu.VMEM((rows + 2 * SUBLANES, c), F32)]
    in_specs += [pl.BlockSpec(w_ri.shape, lambda b, j: (0, 0, 0)),
                 pl.BlockSpec((1, c), const), pl.BlockSpec((1, c), const), pl.BlockSpec((1, c), const),
                 pl.BlockSpec(h0.shape, const)]
    args += [w_ri, r_b.reshape(1, c), i_b.reshape(1, c), lam.reshape(1, c), h0]
    if emit_y:
        out_specs = [tile_spec]
        out_shape = [jax.ShapeDtypeStruct((n, c), F32)]
    else:
        out_specs = [pl.BlockSpec(h0.shape, const)]
        out_shape = [jax.ShapeDtypeStruct(h0.shape, F32)]
    if conv is not None:
        out_specs.append(tile_spec)
        out_shape.append(jax.ShapeDtypeStruct((n, c), F32))
    name = ("lru_scan" if emit_y else "lru_ctx") + ("_rev" if reverse else "_fwd")
    outs = pl.pallas_call(
        functools.partial(_lru_scan_kernel, n_tiles=n_tiles, reverse=reverse, conv_done=conv is None,
                          reset_first=reset_first, emit_y=emit_y),
        grid=(n_batch, n_tiles),
        in_specs=in_specs,
        out_specs=out_specs,
        out_shape=out_shape,
        scratch_shapes=scratch + [pltpu.VMEM((rows, c), F32),
                                  pltpu.VMEM((rows, c), F32),
                                  pltpu.VMEM((1, c), F32)],
        compiler_params=_params(("arbitrary", "arbitrary")),
        name=name,
    )(*args)
    return outs if conv is not None else outs[0]


def _route(logits, rb):
    e, t = logits.shape
    neg = -jnp.inf
    scores = jax.nn.sigmoid(logits)
    sel = scores + rb
    iota_g = lax.broadcasted_iota(jnp.int32, (N_GROUPS, t), 0).astype(F32)
    iota_e = lax.broadcasted_iota(jnp.int32, (e, t), 0).astype(F32)

    gs = jnp.full((N_GROUPS, t), neg, F32)
    for g in range(N_GROUPS):
        sg = sel[g * EXPERTS_PER_GROUP:(g + 1) * EXPERTS_PER_GROUP, :]
        m1 = jnp.max(sg, axis=0, keepdims=True)
        i1 = jnp.min(jnp.where(sg == m1, iota_g, float(EXPERTS_PER_GROUP)), axis=0, keepdims=True)
        m2 = jnp.max(jnp.where(iota_g == i1, neg, sg), axis=0, keepdims=True)
        gs = jnp.where(iota_g == float(g), m1 + m2, gs)

    keep = jnp.zeros((N_GROUPS, t), F32)
    for _ in range(TOPK_GROUPS):
        m = jnp.max(gs, axis=0, keepdims=True)
        idx = jnp.min(jnp.where(gs == m, iota_g, float(N_GROUPS)), axis=0, keepdims=True)
        hit = iota_g == idx
        keep = jnp.where(hit, 1.0, keep)
        gs = jnp.where(hit, neg, gs)

    masked = jnp.concatenate(
        [jnp.where(keep[g:g + 1, :] > 0.0, sel[g * EXPERTS_PER_GROUP:(g + 1) * EXPERTS_PER_GROUP, :], neg)
         for g in range(N_GROUPS)], axis=0)

    iota_k = lax.broadcasted_iota(jnp.int32, (TOP_K, t), 0)
    selmask = jnp.zeros((e, t), F32)
    eidx = jnp.zeros((TOP_K, t), F32)
    gw = jnp.zeros((TOP_K, t), F32)
    for k in range(TOP_K):
        m = jnp.max(masked, axis=0, keepdims=True)
        idx = jnp.min(jnp.where(masked == m, iota_e, float(e)), axis=0, keepdims=True)
        hit = iota_e == idx
        gk = jnp.sum(jnp.where(hit, scores, 0.0), axis=0, keepdims=True)
        masked = jnp.where(hit, neg, masked)
        selmask = jnp.where(hit, 1.0, selmask)
        eidx = jnp.where(iota_k == k, idx, eidx)
        gw = jnp.where(iota_k == k, gk, gw)
    gw = gw / jnp.sum(gw, axis=0, keepdims=True) * ROUTED_SCALE
    return eidx, gw, selmask


def _ffn_pre(x1, mod_ref, row, g2_ref, rwt_ref, rb_ref, hx2_ref, hi_ref, lo_ref, hit_ref, lot_ref, gatet_ref,
             tab_ref, tabt_ref):
    d = x1.shape[1]
    hx2 = _rms_mod(x1, g2_ref[...], _mod_chunk(mod_ref, row, 4, d), _mod_chunk(mod_ref, row, 3, d))
    hx2_ref[...] = hx2
    logits = lax.dot_general(rwt_ref[...], hx2, (((1,), (1,)), ((), ())),
                             preferred_element_type=F32, precision=HIGHEST)
    eidx, gw, selmask = _route(logits, rb_ref[...])
    for s in range(selmask.shape[1] // TOKEN_TILE):
        ts = slice(s * TOKEN_TILE, (s + 1) * TOKEN_TILE)
        hi, lo, gate, tab, tabt = _sorted_layout(selmask[:, ts], eidx[:, ts], gw[:, ts])
        hi_ref[:, ts] = hi.astype(BF16)
        lo_ref[:, ts] = lo.astype(BF16)
        hit_ref[ts, :] = hi.T.astype(BF16)
        lot_ref[ts, :] = lo.T.astype(BF16)
        gatet_ref[ts, :] = gate.T.astype(BF16)
        tab_ref[s] = tab
        tabt_ref[s] = tabt


def _ffn_pre_specs(n, d):
    row = lambda i: (i, 0)
    col = lambda i: (0, i)
    sub = MIX_TILE // TOKEN_TILE
    lead = lambda i: (i, 0, 0)
    out_specs = [pl.BlockSpec((MIX_TILE, d), row),
                 pl.BlockSpec((MIX_TILE, d), row),
                 pl.BlockSpec((N_EXPERTS, MIX_TILE), col),
                 pl.BlockSpec((N_EXPERTS, MIX_TILE), col),
                 pl.BlockSpec((MIX_TILE, N_EXPERTS), row),
                 pl.BlockSpec((MIX_TILE, N_EXPERTS), row),
                 pl.BlockSpec((MIX_TILE, N_EXPERTS), row),
                 pl.BlockSpec((sub, N_EXPERTS, LANES), lead),
                 pl.BlockSpec((sub, SUBLANES, LANES), lead)]
    out_shape = [jax.ShapeDtypeStruct((n, d), F32),
                 jax.ShapeDtypeStruct((n, d), F32),
                 jax.ShapeDtypeStruct((N_EXPERTS, n), BF16),
                 jax.ShapeDtypeStruct((N_EXPERTS, n), BF16),
                 jax.ShapeDtypeStruct((n, N_EXPERTS), BF16),
                 jax.ShapeDtypeStruct((n, N_EXPERTS), BF16),
                 jax.ShapeDtypeStruct((n, N_EXPERTS), BF16),
                 jax.ShapeDtypeStruct((n // TOKEN_TILE, N_EXPERTS, LANES), F32),
                 jax.ShapeDtypeStruct((n // TOKEN_TILE, SUBLANES, LANES), F32)]
    return out_specs, out_shape


def _rglru_out_kernel(yf_ref, yr_ref, gate_ref, x0_ref, mod_ref, wout_ref, g2_ref, rwt_ref, rb_ref,
                      x1_ref, hx2_ref, hi_ref, lo_ref, hit_ref, lot_ref, gatet_ref, tab_ref, tabt_ref, *,
                      tiles_per_seq):
    d = x0_ref.shape[1]
    row = pl.program_id(0) // tiles_per_seq
    yx = yf_ref[...] + yr_ref[...]
    v = gate_ref[...].astype(F32) * yx
    out = jnp.dot(v.astype(BF16), wout_ref[...], preferred_element_type=F32)
    x1 = x0_ref[...] + _mod_chunk(mod_ref, row, 2, d) * out
    x1_ref[...] = x1
    _ffn_pre(x1, mod_ref, row, g2_ref, rwt_ref, rb_ref, hx2_ref, hi_ref, lo_ref, hit_ref, lot_ref, gatet_ref,
             tab_ref, tabt_ref)


def _rglru_out(y_fwd, y_rev, gate, x0, mod, w_out, g2, rwt, rb, tiles_per_seq):
    n, d = x0.shape
    c = gate.shape[1]
    row = lambda i: (i, 0)
    const = lambda i: (0, 0)
    out_specs, out_shape = _ffn_pre_specs(n, d)
    return pl.pallas_call(
        functools.partial(_rglru_out_kernel, tiles_per_seq=tiles_per_seq),
        grid=(n // MIX_TILE,),
        in_specs=[pl.BlockSpec((MIX_TILE, c), row),
                  pl.BlockSpec((MIX_TILE, c), row),
                  pl.BlockSpec((MIX_TILE, c), row),
                  pl.BlockSpec((MIX_TILE, d), row),
                  pl.BlockSpec(mod.shape, const),
                  pl.BlockSpec(w_out.shape, const),
                  pl.BlockSpec((1, d), const),
                  pl.BlockSpec(rwt.shape, const),
                  pl.BlockSpec(rb.shape, const)],
        out_specs=out_specs,
        out_shape=out_shape,
        compiler_params=_params(("arbitrary",)),
        name="rglru_out",
    )(y_fwd, y_rev, gate, x0, mod, w_out, g2, rwt, rb)


def _sgu_kernel(x_ref, mod_ref, g_ref, win_ref, lng_ref, lnb_ref, ws_ref, bst_ref, wout_ref,
                g2_ref, rwt_ref, rb_ref,
                x1_ref, hx2_ref, hi_ref, lo_ref, hit_ref, lot_ref, gatet_ref, tab_ref, tabt_ref, m_scr, *,
                tiles_per_seq):
    t_rows, d = x_ref.shape
    w = wout_ref.shape[0]
    gd = w // SGU_HEADS
    row = pl.program_id(0) // tiles_per_seq
    x = x_ref[...]
    hx = _rms_mod(x, g_ref[...], _mod_chunk(mod_ref, row, 1, d), _mod_chunk(mod_ref, row, 0, d))
    z = jax.nn.gelu(jnp.dot(hx.astype(BF16), win_ref[...], preferred_element_type=F32))
    u = z[:, :w]
    v = z[:, w:]
    mu = jnp.mean(v, axis=-1, keepdims=True)
    vc = v - mu
    v = vc * lax.rsqrt(jnp.mean(vc * vc, axis=-1, keepdims=True) + NORM_EPS) * lng_ref[...] + lnb_ref[...]
    vb = v.astype(BF16)
    for ch in range(t_rows // CHUNK):
        rs = slice(ch * CHUNK, (ch + 1) * CHUNK)
        for g in range(SGU_HEADS):
            cs = slice(g * gd, (g + 1) * gd)
            sv = jnp.dot(ws_ref[g], vb[rs, cs], preferred_element_type=F32) + bst_ref[:, g:g + 1]
            m_scr[rs, cs] = (u[rs, cs] * sv).astype(BF16)
    out = jnp.dot(m_scr[...], wout_ref[...], preferred_element_type=F32)
    x1 = x + _mod_chunk(mod_ref, row, 2, d) * out
    x1_ref[...] = x1
    _ffn_pre(x1, mod_ref, row, g2_ref, rwt_ref, rb_ref, hx2_ref, hi_ref, lo_ref, hit_ref, lot_ref, gatet_ref,
             tab_ref, tabt_ref)


def _sgu(x, mod, g, w_in, ln_g, ln_b, w_s, b_st, w_out, g2, rwt, rb, tiles_per_seq):
    n, d = x.shape
    w = w_out.shape[0]
    const = lambda i: (0, 0)
    out_specs, out_shape = _ffn_pre_specs(n, d)
    return pl.pallas_call(
        functools.partial(_sgu_kernel, tiles_per_seq=tiles_per_seq),
        grid=(n // MIX_TILE,),
        in_specs=[pl.BlockSpec((MIX_TILE, d), lambda i: (i, 0)),
                  pl.BlockSpec(mod.shape, const),
                  pl.BlockSpec((1, d), const),
                  pl.BlockSpec(w_in.shape, const, pipeline_mode=pl.Buffered(1)),
                  pl.BlockSpec((1, w), const),
                  pl.BlockSpec((1, w), const),
                  pl.BlockSpec(w_s.shape, lambda i: (0, 0, 0)),
                  pl.BlockSpec(b_st.shape, const),
                  pl.BlockSpec(w_out.shape, const, pipeline_mode=pl.Buffered(1)),
                  pl.BlockSpec((1, d), const),
                  pl.BlockSpec(rwt.shape, const),
                  pl.BlockSpec(rb.shape, const)],
        out_specs=out_specs,
        out_shape=out_shape,
        scratch_shapes=[pltpu.VMEM((MIX_TILE, w), BF16)],
        compiler_params=_params(("arbitrary",)),
        name="sgu",
    )(x, mod, g, w_in, ln_g, ln_b, w_s, b_st, w_out, g2, rwt, rb)


def _sorted_layout(m, eidx, gw):
    e, t = m.shape
    lanes = LANES
    r = lax.broadcasted_iota(jnp.int32, (t, t), 0)
    c = lax.broadcasted_iota(jnp.int32, (t, t), 1)
    upper = jnp.where(r <= c, 1.0, 0.0).astype(BF16)
    incl = jnp.dot(m.astype(BF16), upper, preferred_element_type=F32)
    run = _round_up_rows(incl[:, t - 1:t])
    re = lax.broadcasted_iota(jnp.int32, (e, e), 0)
    ce = lax.broadcasted_iota(jnp.int32, (e, e), 1)
    lower = jnp.where(ce < re, 1.0, 0.0).astype(BF16)
    tiles = jnp.broadcast_to(run * (1.0 / ROW_ALIGN), (e, lanes)).astype(BF16)
    lstart = jnp.dot(lower, tiles, preferred_element_type=F32)[:, 0:1] * float(ROW_ALIGN)
    pos = jnp.where(m > 0.0, lstart + incl - m, float(POS_RADIX * POS_RADIX - 1))
    hi = jnp.floor(pos * (1.0 / POS_RADIX))
    lo = pos - hi * float(POS_RADIX)
    iota_e = lax.broadcasted_iota(jnp.int32, (e, t), 0).astype(F32)
    gate = jnp.zeros((e, t), F32)
    for k in range(TOP_K):
        gate = jnp.where(iota_e == eidx[k:k + 1, :], gw[k:k + 1, :], gate)
    lane = lax.broadcasted_iota(jnp.int32, (e, lanes), 1)
    tab = jnp.where(lane == 0, run, jnp.where(lane == 2, lstart, 0.0))
    diag = lax.broadcasted_iota(jnp.int32, (e, lanes), 0) == lane
    row_of = lambda col: jnp.sum(jnp.where(diag, jnp.broadcast_to(col, (e, lanes)), 0.0), axis=0, keepdims=True)
    sub = lax.broadcasted_iota(jnp.int32, (SUBLANES, lanes), 0)
    tabt = jnp.where(sub == 0, row_of(lstart), jnp.where(sub == 1, row_of(lstart + run), 0.0))
    return hi, lo, gate, tab, tabt


def _aligned(v):
    return pl.multiple_of(v, ROW_ALIGN)


def _run_copies(tab_ref, make_copy, unroll):
    def body(e, carry):
        make_copy(_aligned(tab_ref[TAB_LOCAL + e]), _aligned(tab_ref[N_EXPERTS + e]),
                  _aligned(tab_ref[e])).start()
        return carry

    lax.fori_loop(0, N_EXPERTS, body, 0, unroll=unroll)


def _dispatch_kernel(tab_ref, ztab_ref, tabt_ref, hi_ref, lo_ref, x_ref, xs_hbm, sbuf, zbuf, sems, zsem):
    i = pl.program_id(0)
    slot = i % 2
    t = x_ref.shape[0]
    rows = sbuf.shape[1]

    def zero_copy(e):
        n = _aligned(ztab_ref[e])
        return pltpu.make_async_copy(zbuf.at[pl.ds(0, n)], xs_hbm.at[pl.ds(_aligned(ztab_ref[N_EXPERTS + e]), n)], zsem)

    def for_zero_runs(fn):
        def body(e, c):
            @pl.when(ztab_ref[e] > 0)
            def _():
                fn(zero_copy(e))
            return c
        lax.fori_loop(0, N_EXPERTS, body, 0)

        def tail(b, c):
            fn(pltpu.make_async_copy(zbuf, xs_hbm.at[pl.ds(pl.multiple_of(b * EXPERT_BLOCK, EXPERT_BLOCK),
                                                           EXPERT_BLOCK)], zsem))
            return c
        lax.fori_loop(ztab_ref[2 * N_EXPERTS], xs_hbm.shape[0] // EXPERT_BLOCK, tail, 0)

    @pl.when(i == 0)
    def _():
        zbuf[...] = jnp.zeros_like(zbuf)
        for_zero_runs(lambda cp: cp.start())

    xb = x_ref[...].astype(BF16)
    total = tab_ref[2 * N_EXPERTS]
    run_lo = tabt_ref[0, 0:1, 0:N_EXPERTS]
    run_hi = tabt_ref[0, 1:2, 0:N_EXPERTS]

    def permute(r0):
        j_e = (lax.broadcasted_iota(jnp.int32, (PERM_CHUNK, N_EXPERTS), 0) + r0).astype(F32)
        owner = jnp.where((j_e >= run_lo) & (j_e < run_hi), 1.0, 0.0).astype(BF16)
        pos = (jnp.dot(owner, hi_ref[...], preferred_element_type=F32) * float(POS_RADIX)
               + jnp.dot(owner, lo_ref[...], preferred_element_type=F32))
        j_t = (lax.broadcasted_iota(jnp.int32, (PERM_CHUNK, t), 0) + r0).astype(F32)
        p = jnp.where(pos == j_t, 1.0, 0.0).astype(BF16)
        sbuf[slot, r0:r0 + PERM_CHUNK, :] = jnp.dot(p, xb, preferred_element_type=F32).astype(BF16)

    for r0 in range(0, rows, PERM_CHUNK):
        if r0 < t * TOP_K + PERM_CHUNK:
            permute(r0)
        else:
            pl.when(total > r0)(functools.partial(permute, r0))

    _run_copies(tab_ref, lambda loc, glob, n: pltpu.make_async_copy(
        sbuf.at[slot, pl.ds(loc, n)], xs_hbm.at[pl.ds(glob, n)], sems.at[slot]), unroll=True)

    def wait_rows(s, n):
        pltpu.make_async_copy(sbuf.at[s, pl.ds(0, n)], xs_hbm.at[pl.ds(0, n)], sems.at[s]).wait()

    @pl.when(i > 0)
    def _():
        wait_rows(1 - slot, _aligned(tab_ref[2 * N_EXPERTS + 1]))

    @pl.when(i == pl.num_programs(0) - 1)
    def _():
        wait_rows(slot, _aligned(tab_ref[2 * N_EXPERTS]))

    @pl.when(i == 0)
    def _():
        for_zero_runs(lambda cp: cp.wait())


def _dispatch(tab, ztab, tab_t, pos_hi, pos_lo, hx2, xs_rows):
    n, d = hx2.shape
    col = lambda i: (0, i)
    return pl.pallas_call(
        _dispatch_kernel,
        grid=(n // TOKEN_TILE,),
        in_specs=[pl.BlockSpec((TAB_WIDTH,), lambda i: (i,), memory_space=pltpu.SMEM),
                  pl.BlockSpec(memory_space=pltpu.SMEM),
                  pl.BlockSpec((1, SUBLANES, LANES), lambda i: (i, 0, 0)),
                  pl.BlockSpec((N_EXPERTS, TOKEN_TILE), col),
                  pl.BlockSpec((N_EXPERTS, TOKEN_TILE), col),
                  pl.BlockSpec((TOKEN_TILE, d), lambda i: (i, 0))],
        out_specs=pl.BlockSpec(memory_space=pl.ANY),
        out_shape=jax.ShapeDtypeStruct((xs_rows, d), BF16),
        scratch_shapes=[pltpu.VMEM((2, SORTED_ROWS, d), BF16),
                        pltpu.VMEM((EXPERT_BLOCK, d), BF16),
                        pltpu.SemaphoreType.DMA((2,)),
                        pltpu.SemaphoreType.DMA],
        compiler_params=_params(("arbitrary",)),
        name="moe_dispatch",
    )(tab, ztab, tab_t, pos_hi, pos_lo, hx2)


def _experts_kernel(be_ref, nb_ref, xs_ref, wgu_ref, wd_ref, ys_ref, wgu_b, wd_b):
    i = pl.program_id(0)
    ff = wd_b.shape[0]
    used = i < nb_ref[0]
    new_expert = (i == 0) | (be_ref[i] != be_ref[jnp.maximum(i - 1, 0)])

    @pl.when(used & new_expert)
    def _():
        wgu_b[...] = wgu_ref[0, 0].astype(BF16)
        wd_b[...] = wd_ref[0, 0].astype(BF16)

    @pl.when(used)
    def _():
        h = jnp.dot(xs_ref[...], wgu_b[...], preferred_element_type=F32)
        a = _silu(h[:, :ff]) * h[:, ff:]
        ys_ref[...] = jnp.dot(a.astype(BF16), wd_b[...], preferred_element_type=F32).astype(BF16)

    @pl.when(jnp.logical_not(used))
    def _():
        ys_ref[...] = jnp.zeros_like(ys_ref)


def _experts(block_e, n_used, xs, w_gu, w_down, layer):
    rows, d = xs.shape
    nb = rows // EXPERT_BLOCK
    blk = lambda i, be, nu: (jnp.maximum(jnp.minimum(i, nu[0] - 1), 0), 0)
    wmap = lambda i, be, nu: (layer, be[i], 0, 0)
    return pl.pallas_call(
        _experts_kernel,
        grid_spec=pltpu.PrefetchScalarGridSpec(
            num_scalar_prefetch=2,
            grid=(nb,),
            in_specs=[pl.BlockSpec((EXPERT_BLOCK, d), blk),
                      pl.BlockSpec((1, 1) + w_gu.shape[2:], wmap),
                      pl.BlockSpec((1, 1) + w_down.shape[2:], wmap)],
            out_specs=pl.BlockSpec((EXPERT_BLOCK, d), lambda i, be, nu: (i, 0)),
            scratch_shapes=[pltpu.VMEM(w_gu.shape[2:], BF16), pltpu.VMEM(w_down.shape[2:], BF16)]),
        out_shape=jax.ShapeDtypeStruct((rows, d), BF16),
        compiler_params=_params(("arbitrary",)),
        name="moe_experts",
    )(block_e, n_used, xs, w_gu, w_down)


def _combine_kernel(tab_ref, tabn_ref, tabv_ref, hi_ref, lo_ref, gate_ref, ys_hbm, hx2_ref, x1_ref, mod_ref,
                    wsgu_ref, wsd_ref, fg_ref, out_ref, ybuf, sems, *, tiles_per_seq, final_norm):
    t_rows, d = x1_ref.shape
    ff = wsd_ref.shape[0]
    rows = ybuf.shape[1]
    i = pl.program_id(0)
    last = pl.num_programs(0) - 1
    slot = i % 2
    row = i // tiles_per_seq

    def gather(table, s, unroll):
        _run_copies(table, lambda loc, glob, n: pltpu.make_async_copy(
            ys_hbm.at[pl.ds(glob, n)], ybuf.at[s, pl.ds(loc, n)], sems.at[s]), unroll=unroll)

    def wait_rows(table, s):
        n = _aligned(table[2 * N_EXPERTS])
        pltpu.make_async_copy(ys_hbm.at[pl.ds(0, n)], ybuf.at[s, pl.ds(0, n)], sems.at[s]).wait()

    @pl.when(i == 0)
    def _():
        ybuf[...] = jnp.zeros_like(ybuf)
        gather(tab_ref, 0, False)

    gather(tabn_ref, 1 - slot, True)

    hs = jnp.dot(hx2_ref[...].astype(BF16), wsgu_ref[...], preferred_element_type=F32)
    shared = jnp.dot((_silu(hs[:, :ff]) * hs[:, ff:]).astype(BF16), wsd_ref[...], preferred_element_type=F32)

    run_lo = tabv_ref[0, :, 2:3]
    run_hi = run_lo + tabv_ref[0, :, 0:1]
    j_e = lax.broadcasted_iota(jnp.int32, (N_EXPERTS, rows), 1).astype(F32)
    owner = jnp.where((j_e >= run_lo) & (j_e < run_hi), 1.0, 0.0).astype(BF16)
    pos = (jnp.dot(hi_ref[...], owner, preferred_element_type=F32) * float(POS_RADIX)
           + jnp.dot(lo_ref[...], owner, preferred_element_type=F32))
    gates = jnp.dot(gate_ref[...], owner, preferred_element_type=F32)
    j_t = lax.broadcasted_iota(jnp.int32, (t_rows, rows), 1).astype(F32)
    gb = jnp.where(pos == j_t, gates, 0.0).astype(BF16)

    wait_rows(tab_ref, slot)
    routed = jnp.dot(gb, ybuf[slot], preferred_element_type=F32)
    x2 = x1_ref[...] + _mod_chunk(mod_ref, row, 5, d) * (routed + shared)
    if final_norm:
        x2 = x2 * lax.rsqrt(jnp.mean(x2 * x2, axis=-1, keepdims=True) + NORM_EPS) * fg_ref[...]
    out_ref[...] = x2

    @pl.when(i == last)
    def _():
        wait_rows(tabn_ref, 1 - slot)


def _combine(tab, tab_v, pos_hi, pos_lo, gate, ys, hx2, x1, mod, ws_gu, ws_down, fg, tiles_per_seq, final_norm):
    n, d = x1.shape
    row = lambda i: (i, 0)
    const = lambda i: (0, 0)
    return pl.pallas_call(
        functools.partial(_combine_kernel, tiles_per_seq=tiles_per_seq, final_norm=final_norm),
        grid=(n // TOKEN_TILE,),
        in_specs=[pl.BlockSpec((TAB_WIDTH,), lambda i: (i,), memory_space=pltpu.SMEM),
                  pl.BlockSpec((TAB_WIDTH,), lambda i: (jnp.minimum(i + 1, n // TOKEN_TILE - 1),),
                               memory_space=pltpu.SMEM),
                  pl.BlockSpec((1, N_EXPERTS, LANES), lambda i: (i, 0, 0)),
                  pl.BlockSpec((TOKEN_TILE, N_EXPERTS), row),
                  pl.BlockSpec((TOKEN_TILE, N_EXPERTS), row),
                  pl.BlockSpec((TOKEN_TILE, N_EXPERTS), row),
                  pl.BlockSpec(memory_space=pl.ANY),
                  pl.BlockSpec((TOKEN_TILE, d), row),
                  pl.BlockSpec((TOKEN_TILE, d), row),
                  pl.BlockSpec(mod.shape, const),
                  pl.BlockSpec(ws_gu.shape, const),
                  pl.BlockSpec(ws_down.shape, const),
                  pl.BlockSpec((1, d), const)],
        out_specs=pl.BlockSpec((TOKEN_TILE, d), row),
        out_shape=jax.ShapeDtypeStruct((n, d), F32),
        scratch_shapes=[pltpu.VMEM((2, SORTED_ROWS, d), BF16), pltpu.SemaphoreType.DMA((2,))],
        compiler_params=_params(("arbitrary",)),
        name="moe_combine",
    )(tab, tab, tab_v, pos_hi, pos_lo, gate, ys, hx2, x1, mod, ws_gu, ws_down, fg)


def _moe(x1, hx2, pos_hi, pos_lo, pos_hi_t, pos_lo_t, gate_t, tab_f, tab_t, mod, w_gu, w_down, layer, ws_gu,
         ws_down, fg, tiles_per_seq, final_norm):
    n, d = x1.shape
    n_tiles = n // TOKEN_TILE
    run_len = tab_f[:, :, 0].astype(jnp.int32)
    counts = jnp.sum(run_len, axis=0)
    padded = (counts + EXPERT_BLOCK - 1) // EXPERT_BLOCK * EXPERT_BLOCK
    pad_end = jnp.cumsum(padded)
    pad_start = pad_end - padded
    max_rows = n * TOP_K + n_tiles * N_EXPERTS * ROW_ALIGN + N_EXPERTS * (EXPERT_BLOCK - ROW_ALIGN)
    n_blocks = (max_rows + EXPERT_BLOCK - 1) // EXPERT_BLOCK
    n_used = (pad_end[-1:] // EXPERT_BLOCK).astype(jnp.int32)
    block_start = jnp.arange(n_blocks, dtype=jnp.int32) * EXPERT_BLOCK
    block_e = jnp.minimum(jnp.sum((pad_end[None, :] <= block_start[:, None]).astype(jnp.int32), axis=1),
                          N_EXPERTS - 1)

    run_start = pad_start[None, :] + jnp.cumsum(run_len, axis=0) - run_len
    total = jnp.sum(run_len, axis=1, keepdims=True)
    prev_total = jnp.concatenate([jnp.zeros((1, 1), jnp.int32), total[:-1]], axis=0)
    run_local = tab_f[:, :, 2].astype(jnp.int32)
    fill = jnp.zeros((n_tiles, TAB_WIDTH - 3 * N_EXPERTS - 2), jnp.int32)
    tab = jnp.concatenate([run_len, run_start, total, prev_total, run_local, fill], axis=1).reshape(-1)
    ztab = jnp.concatenate([padded - counts, pad_start + counts, n_used])

    xs = _dispatch(tab, ztab, tab_t, pos_hi, pos_lo, hx2, n_blocks * EXPERT_BLOCK)
    ys = _experts(block_e, n_used, xs, w_gu, w_down, layer)
    return _combine(tab, tab_f, pos_hi_t, pos_lo_t, gate_t, ys, hx2, x1, mod, ws_gu, ws_down, fg,
                    tiles_per_seq, final_norm)


def _sincos_2d(rows, d):
    quarter = d // 4
    omega = 1.0 / (POS_BASE ** (jnp.arange(quarter, dtype=F32) / quarter))

    def emb(n):
        p = jnp.arange(n, dtype=F32)[:, None] * omega[None, :]
        return jnp.concatenate([jnp.sin(p), jnp.cos(p)], axis=-1)

    er, ec = emb(rows), emb(GRID_W)
    pe = jnp.concatenate([jnp.broadcast_to(er[:, None, :], (rows, GRID_W, d // 2)),
                          jnp.broadcast_to(ec[None, :, :], (rows, GRID_W, d // 2))], axis=-1)
    return pe.reshape(rows * GRID_W, d)


def kernel(x, c, ctx, c_ctx, ada_w, ada_b, mix_norm_g, ffn_norm_g, a_w_in, a_conv_w, a_conv_b, a_gate_r_w, a_gate_r_b, a_gate_i_w, a_gate_i_b, a_lambda, a_w_out, b_w_in, b_ln_g, b_ln_b, b_w_s, b_b_s, b_w_out, router_w, router_b, moe_w_gu, moe_w_down, shared_w_gu, shared_w_down, final_norm_g):
    bsz, s, d = x.shape
    ctx_len = ctx.shape[1]
    depth = ada_w.shape[0]
    assert depth == 2 and bsz < MOD_ROWS and s % MIX_TILE == 0 and MIX_TILE % TOKEN_TILE == 0
    assert s % SCAN_TILE == 0
    assert ctx_len % TOKEN_TILE == 0
    n = bsz * s
    tps = s // TOKEN_TILE
    ctx_row = bsz

    cc = jnp.zeros((MOD_ROWS, d), F32).at[:bsz].set(c).at[ctx_row].set(c_ctx)
    mod = _modulation(cc, ada_w, ada_b)
    pe = _sincos_2d(s // GRID_W, d)
    rc = a_w_in.shape[2] // 2

    w_in0 = a_w_in[0].astype(BF16)
    g_mix0 = mix_norm_g[0].reshape(1, d)
    x0, gate, ux = _rglru_in(x.reshape(n, d), pe, mod[0], g_mix0, w_in0)
    uc = _ctx_in(ctx.reshape(bsz * ctx_len, d), mod[0], g_mix0, w_in0[:, rc:], ctx_row)
    w_ri = jnp.concatenate([a_gate_r_w[0], a_gate_i_w[0]], axis=-1).astype(BF16)
    conv = (a_conv_w[0], a_conv_b[0])
    gates = [(w_ri[k], a_gate_r_b[0, k], a_gate_i_b[0, k], a_lambda[0, k]) for k in range(2)]
    h_zero = jnp.zeros((bsz, rc), F32)
    h_fwd, uc = _lru_scan(uc, conv, *gates[0], h_zero, reverse=False, reset_first=True, emit_y=False)
    h_rev = _lru_scan(uc, None, *gates[1], h_zero, reverse=True, reset_first=True, emit_y=False)
    y_fwd, ux = _lru_scan(ux, conv, *gates[0], h_fwd, reverse=False, reset_first=False, emit_y=True)
    y_rev = _lru_scan(ux, None, *gates[1], h_rev, reverse=True, reset_first=False, emit_y=True)
    pre = _rglru_out(y_fwd, y_rev, gate, x0, mod[0], a_w_out[0].astype(BF16), ffn_norm_g[0].reshape(1, d),
                     router_w[0].T, router_b[0].reshape(N_EXPERTS, 1), s // MIX_TILE)
    x1 = _moe(*pre, mod[0], moe_w_gu, moe_w_down, 0, shared_w_gu[0].astype(BF16),
              shared_w_down[0].astype(BF16), final_norm_g.reshape(1, d), tps, False)

    pre = _sgu(x1, mod[1], mix_norm_g[1].reshape(1, d), b_w_in[0].astype(BF16),
               b_ln_g[0].reshape(1, -1), b_ln_b[0].reshape(1, -1), b_w_s[0].astype(BF16), b_b_s[0].T,
               b_w_out[0].astype(BF16), ffn_norm_g[1].reshape(1, d),
               router_w[1].T, router_b[1].reshape(N_EXPERTS, 1), s // MIX_TILE)
    out = _moe(*pre, mod[1], moe_w_gu, moe_w_down, 1, shared_w_gu[1].astype(BF16),
               shared_w_down[1].astype(BF16), final_norm_g.reshape(1, d), tps, True)
    return out.reshape(bsz, s, d)
```

```python
import functools

import jax
import jax.numpy as jnp
from jax import lax
from jax.experimental import pallas as pl
from jax.experimental.pallas import tpu as pltpu

F32 = jnp.float32
BF16 = jnp.bfloat16
HIGHEST = lax.Precision.HIGHEST

GRID_W = 64
NORM_EPS = 1e-6
POS_BASE = 10000.0
RNN_HEADS = 5
CONV_WIDTH = 4
CONV_PAD_LEFT = 2
LRU_C = 8.0
SGU_HEADS = 8
CHUNK = 128
N_EXPERTS = 64
TOP_K = 8
N_GROUPS = 8
TOPK_GROUPS = 4
EXPERTS_PER_GROUP = N_EXPERTS // N_GROUPS
ROUTED_SCALE = 2.5

SUBLANES = 8
ROW_ALIGN = 16
LANES = 128
MOD_ROWS = 8
TOKEN_TILE = 256
MIX_TILE = 512
SCAN_TILE = 512
EXPERT_BLOCK = 1024
SORTED_ROWS = TOKEN_TILE * TOP_K + N_EXPERTS * ROW_ALIGN
PERM_CHUNK = 512
TAB_WIDTH = 256
TAB_LOCAL = 2 * N_EXPERTS + 2
POS_RADIX = 64
VMEM_LIMIT = 56 * 1024 * 1024


def _params(semantics, vmem=VMEM_LIMIT):
    return pltpu.CompilerParams(dimension_semantics=semantics, vmem_limit_bytes=vmem)


def _silu(x):
    return x * jax.nn.sigmoid(x)


def _rms_mod(x, g, sc, sh):
    y = x * lax.rsqrt(jnp.mean(x * x, axis=-1, keepdims=True) + NORM_EPS)
    return (y * g) * (1.0 + sc) + sh


def _mod_chunk(mod_ref, row, k, d):
    return mod_ref[pl.ds(row, 1), k * d:(k + 1) * d]


def _round_up_rows(count):
    return jnp.maximum(jnp.ceil(count * (1.0 / ROW_ALIGN)), 1.0) * float(ROW_ALIGN)


def _mod_kernel(cc_ref, w_ref, b_ref, o_ref):
    s = _silu(cc_ref[...])
    o_ref[0] = jnp.dot(s, w_ref[0], preferred_element_type=F32, precision=HIGHEST) + b_ref[0]


def _modulation(cc, ada_w, ada_b):
    depth, d, nd = ada_w.shape
    return pl.pallas_call(
        _mod_kernel,
        grid=(depth, nd // d),
        in_specs=[pl.BlockSpec((MOD_ROWS, d), lambda l, j: (0, 0)),
                  pl.BlockSpec((1, d, d), lambda l, j: (l, 0, j)),
                  pl.BlockSpec((1, 1, d), lambda l, j: (l, 0, j))],
        out_specs=pl.BlockSpec((1, MOD_ROWS, d), lambda l, j: (l, 0, j)),
        out_shape=jax.ShapeDtypeStruct((depth, MOD_ROWS, nd), F32),
        compiler_params=_params(("arbitrary", "arbitrary")),
        name="modulation",
    )(cc, ada_w, ada_b.reshape(depth, 1, nd))


def _rglru_in_kernel(x_ref, er_ref, ec_ref, mod_ref, g_ref, w_ref, x0_ref, gate_ref, u_ref, *, tiles_per_seq):
    t_rows, d = x_ref.shape
    c = u_ref.shape[1]
    grid_w, half = ec_ref.shape
    rows = t_rows // grid_w
    row = pl.program_id(0) // tiles_per_seq
    r0 = pl.multiple_of((pl.program_id(0) % tiles_per_seq) * rows, rows)
    er = er_ref[pl.ds(r0, rows), :]
    pe = jnp.concatenate([jnp.broadcast_to(er[:, None, :], (rows, grid_w, half)).reshape(t_rows, half),
                          jnp.concatenate([ec_ref[...]] * rows, axis=0)], axis=1)
    x = x_ref[...] + pe
    hx = _rms_mod(x, g_ref[...], _mod_chunk(mod_ref, row, 1, d), _mod_chunk(mod_ref, row, 0, d))
    z = jnp.dot(hx.astype(BF16), w_ref[...], preferred_element_type=F32)
    x0_ref[...] = x
    gate_ref[...] = jax.nn.gelu(z[:, :c]).astype(BF16)
    u_ref[...] = z[:, c:]


def _rglru_in(x2, er, ec, mod, g, w_in):
    n, d = x2.shape
    c = w_in.shape[1] // 2
    tps = er.shape[0] * ec.shape[0] // MIX_TILE
    row = lambda i: (i, 0)
    return pl.pallas_call(
        functools.partial(_rglru_in_kernel, tiles_per_seq=tps),
        grid=(n // MIX_TILE,),
        in_specs=[pl.BlockSpec((MIX_TILE, d), row),
                  pl.BlockSpec(er.shape, lambda i: (0, 0)),
                  pl.BlockSpec(ec.shape, lambda i: (0, 0)),
                  pl.BlockSpec(mod.shape, lambda i: (0, 0)),
                  pl.BlockSpec((1, d), lambda i: (0, 0)),
                  pl.BlockSpec(w_in.shape, lambda i: (0, 0))],
        out_specs=[pl.BlockSpec((MIX_TILE, d), row),
                   pl.BlockSpec((MIX_TILE, c), row),
                   pl.BlockSpec((MIX_TILE, c), row)],
        out_shape=[jax.ShapeDtypeStruct((n, d), F32),
                   jax.ShapeDtypeStruct((n, c), BF16),
                   jax.ShapeDtypeStruct((n, c), F32)],
        compiler_params=_params(("arbitrary",)),
        name="rglru_in",
    )(x2, er, ec, mod, g, w_in)


def _ctx_in_kernel(x_ref, mod_ref, g_ref, w_ref, u_ref, *, ctx_row):
    d = x_ref.shape[1]
    hx = _rms_mod(x_ref[...], g_ref[...], _mod_chunk(mod_ref, ctx_row, 1, d),
                  _mod_chunk(mod_ref, ctx_row, 0, d))
    u_ref[...] = jnp.dot(hx.astype(BF16), w_ref[...], preferred_element_type=F32)


def _ctx_in(c2, mod, g, w_u, ctx_row):
    n, d = c2.shape
    c = w_u.shape[1]
    return pl.pallas_call(
        functools.partial(_ctx_in_kernel, ctx_row=ctx_row),
        grid=(n // TOKEN_TILE,),
        in_specs=[pl.BlockSpec((TOKEN_TILE, d), lambda i: (i, 0)),
                  pl.BlockSpec(mod.shape, lambda i: (0, 0)),
                  pl.BlockSpec((1, d), lambda i: (0, 0)),
                  pl.BlockSpec(w_u.shape, lambda i: (0, 0))],
        out_specs=pl.BlockSpec((TOKEN_TILE, c), lambda i: (i, 0)),
        out_shape=jax.ShapeDtypeStruct((n, c), F32),
        compiler_params=_params(("arbitrary",)),
        name="ctx_in",
    )(c2, mod, g, w_u)


def _log_sigmoid(x):
    return jnp.minimum(x, 0.0) - jnp.log1p(jnp.exp(-jnp.abs(x)))


def _lru_scan_kernel(*refs, n_tiles, reverse, conv_done, reset_first, emit_y):
    if conv_done:
        u_ref, wri_ref, rb_ref, ib_ref, lam_ref, h0_ref, out_ref, a_scr, b_scr, h_scr = refs
    else:
        (u_ref, up_ref, un_ref, cw_ref, cb_ref, wri_ref, rb_ref, ib_ref, lam_ref, h0_ref,
         out_ref, uc_ref, ubuf, a_scr, b_scr, h_scr) = refs
    t_rows, c = u_ref.shape
    hb = c // RNN_HEADS
    b = pl.program_id(0)
    j = pl.program_id(1)
    jj = n_tiles - 1 - j if reverse else j

    if conv_done:
        u = u_ref[...]
    else:
        ubuf[SUBLANES:SUBLANES + t_rows, :] = u_ref[...]
        ubuf[0:SUBLANES, :] = jnp.where(jj == 0, 0.0, up_ref[...])
        ubuf[SUBLANES + t_rows:, :] = jnp.where(jj == n_tiles - 1, 0.0, un_ref[...])
        u = cb_ref[...]
        for k in range(CONV_WIDTH):
            u = u + cw_ref[k:k + 1, :] * ubuf[pl.ds(SUBLANES - CONV_PAD_LEFT + k, t_rows), :]
        uc_ref[...] = u

    log_lam = LRU_C * _log_sigmoid(lam_ref[...])
    rows = lax.broadcasted_iota(jnp.int32, (t_rows, 1), 0)
    first_row = jnp.where(j == 0, t_rows - 1 if reverse else 0, -1)
    for h in range(RNN_HEADS):
        sl = slice(h * hb, (h + 1) * hb)
        uh = u[:, sl]
        z = jnp.dot(uh.astype(BF16), wri_ref[h], preferred_element_type=F32)
        r = jax.nn.sigmoid(z[:, :hb] + rb_ref[:, sl])
        ig = jax.nn.sigmoid(z[:, hb:] + ib_ref[:, sl])
        log_a = r * log_lam[:, sl]
        a = jnp.exp(log_a)
        mult = jnp.sqrt((1.0 - a) * (1.0 + a))
        if reset_first:
            mult = jnp.where(rows == first_row, 1.0, mult)
        a_scr[:, sl] = a
        b_scr[:, sl] = mult * ig * uh

    @pl.when(j == 0)
    def _():
        h_scr[...] = h0_ref[pl.ds(b, 1), :]

    n_groups = t_rows // SUBLANES

    def group(g, h):
        base = pl.multiple_of((n_groups - 1 - g if reverse else g) * SUBLANES, SUBLANES)
        for s in range(SUBLANES):
            r = base + (SUBLANES - 1 - s if reverse else s)
            h = a_scr[pl.ds(r, 1), :] * h + b_scr[pl.ds(r, 1), :]
            if emit_y:
                out_ref[pl.ds(r, 1), :] = h
        return h

    h = lax.fori_loop(0, n_groups, group, h_scr[...])
    h_scr[...] = h
    if not emit_y:
        @pl.when(j == n_tiles - 1)
        def _():
            out_ref[pl.ds(b, 1), :] = h


def _lru_scan(u, conv, w_ri, r_b, i_b, lam, h0, *, reverse, reset_first, emit_y):
    n, c = u.shape
    n_batch = h0.shape[0]
    rows = min(SCAN_TILE, n // n_batch)
    n_tiles = n // n_batch // rows
    sub = rows // SUBLANES
    n_sub = n // SUBLANES
    const = lambda b, j: (0, 0)

    def tile(b, j):
        return b * n_tiles + (n_tiles - 1 - j if reverse else j)

    tile_spec = pl.BlockSpec((rows, c), lambda b, j: (tile(b, j), 0))
    in_specs = [tile_spec]
    args = [u]
    scratch = []
    if conv is not None:
        conv_w, conv_b = conv
        in_specs += [pl.BlockSpec((SUBLANES, c), lambda b, j: (jnp.maximum(tile(b, j) * sub - 1, 0), 0)),
                     pl.BlockSpec((SUBLANES, c), lambda b, j: (jnp.minimum((tile(b, j) + 1) * sub, n_sub - 1), 0)),
                     pl.BlockSpec(conv_w.shape, const),
                     pl.BlockSpec((1, c), const)]
        args += [u, u, conv_w, conv_b.reshape(1, c)]
        scratch = [pltpu.VMEM((rows + 2 * SUBLANES, c), F32)]
    in_specs += [pl.BlockSpec(w_ri.shape, lambda b, j: (0, 0, 0)),
                 pl.BlockSpec((1, c), const), pl.BlockSpec((1, c), const), pl.BlockSpec((1, c), const),
                 pl.BlockSpec(h0.shape, const)]
    args += [w_ri, r_b.reshape(1, c), i_b.reshape(1, c), lam.reshape(1, c), h0]
    if emit_y:
        out_specs = [tile_spec]
        out_shape = [jax.ShapeDtypeStruct((n, c), F32)]
    else:
        out_specs = [pl.BlockSpec(h0.shape, const)]
        out_shape = [jax.ShapeDtypeStruct(h0.shape, F32)]
    if conv is not None:
        out_specs.append(tile_spec)
        out_shape.append(jax.ShapeDtypeStruct((n, c), F32))
    name = ("lru_scan" if emit_y else "lru_ctx") + ("_rev" if reverse else "_fwd")
    outs = pl.pallas_call(
        functools.partial(_lru_scan_kernel, n_tiles=n_tiles, reverse=reverse, conv_done=conv is None,
                          reset_first=reset_first, emit_y=emit_y),
        grid=(n_batch, n_tiles),
        in_specs=in_specs,
        out_specs=out_specs,
        out_shape=out_shape,
        scratch_shapes=scratch + [pltpu.VMEM((rows, c), F32),
                                  pltpu.VMEM((rows, c), F32),
                                  pltpu.VMEM((1, c), F32)],
        compiler_params=_params(("arbitrary", "arbitrary")),
        name=name,
    )(*args)
    return outs if conv is not None else outs[0]


def _route(logits, rb):
    e, t = logits.shape
    neg = -jnp.inf
    scores = jax.nn.sigmoid(logits)
    sel = scores + rb
    iota_g = lax.broadcasted_iota(jnp.int32, (N_GROUPS, t), 0).astype(F32)
    iota_e = lax.broadcasted_iota(jnp.int32, (e, t), 0).astype(F32)

    gs = jnp.full((N_GROUPS, t), neg, F32)
    for g in range(N_GROUPS):
        sg = sel[g * EXPERTS_PER_GROUP:(g + 1) * EXPERTS_PER_GROUP, :]
        m1 = jnp.max(sg, axis=0, keepdims=True)
        i1 = jnp.min(jnp.where(sg == m1, iota_g, float(EXPERTS_PER_GROUP)), axis=0, keepdims=True)
        m2 = jnp.max(jnp.where(iota_g == i1, neg, sg), axis=0, keepdims=True)
        gs = jnp.where(iota_g == float(g), m1 + m2, gs)

    keep = jnp.zeros((N_GROUPS, t), F32)
    for _ in range(TOPK_GROUPS):
        m = jnp.max(gs, axis=0, keepdims=True)
        idx = jnp.min(jnp.where(gs == m, iota_g, float(N_GROUPS)), axis=0, keepdims=True)
        hit = iota_g == idx
        keep = jnp.where(hit, 1.0, keep)
        gs = jnp.where(hit, neg, gs)

    masked = jnp.concatenate(
        [jnp.where(keep[g:g + 1, :] > 0.0, sel[g * EXPERTS_PER_GROUP:(g + 1) * EXPERTS_PER_GROUP, :], neg)
         for g in range(N_GROUPS)], axis=0)

    iota_k = lax.broadcasted_iota(jnp.int32, (TOP_K, t), 0)
    selmask = jnp.zeros((e, t), F32)
    eidx = jnp.zeros((TOP_K, t), F32)
    gw = jnp.zeros((TOP_K, t), F32)
    for k in range(TOP_K):
        m = jnp.max(masked, axis=0, keepdims=True)
        idx = jnp.min(jnp.where(masked == m, iota_e, float(e)), axis=0, keepdims=True)
        hit = iota_e == idx
        gk = jnp.sum(jnp.where(hit, scores, 0.0), axis=0, keepdims=True)
        masked = jnp.where(hit, neg, masked)
        selmask = jnp.where(hit, 1.0, selmask)
        eidx = jnp.where(iota_k == k, idx, eidx)
        gw = jnp.where(iota_k == k, gk, gw)
    gw = gw / jnp.sum(gw, axis=0, keepdims=True) * ROUTED_SCALE
    return eidx, gw, selmask


def _ffn_pre(x1, mod_ref, row, g2_ref, rwt_ref, rb_ref, hx2_ref, hi_ref, lo_ref, hit_ref, lot_ref, gatet_ref,
             tab_ref, tabt_ref):
    d = x1.shape[1]
    hx2 = _rms_mod(x1, g2_ref[...], _mod_chunk(mod_ref, row, 4, d), _mod_chunk(mod_ref, row, 3, d))
    hx2_ref[...] = hx2
    logits = lax.dot_general(rwt_ref[...], hx2, (((1,), (1,)), ((), ())),
                             preferred_element_type=F32, precision=HIGHEST)
    eidx, gw, selmask = _route(logits, rb_ref[...])
    for s in range(selmask.shape[1] // TOKEN_TILE):
        ts = slice(s * TOKEN_TILE, (s + 1) * TOKEN_TILE)
        hi, lo, gate, tab, tabt = _sorted_layout(selmask[:, ts], eidx[:, ts], gw[:, ts])
        hi_ref[:, ts] = hi.astype(BF16)
        lo_ref[:, ts] = lo.astype(BF16)
        hit_ref[ts, :] = hi.T.astype(BF16)
        lot_ref[ts, :] = lo.T.astype(BF16)
        gatet_ref[ts, :] = gate.T.astype(BF16)
        tab_ref[s] = tab
        tabt_ref[s] = tabt


def _ffn_pre_specs(n, d):
    row = lambda i: (i, 0)
    col = lambda i: (0, i)
    sub = MIX_TILE // TOKEN_TILE
    lead = lambda i: (i, 0, 0)
    out_specs = [pl.BlockSpec((MIX_TILE, d), row),
                 pl.BlockSpec((MIX_TILE, d), row),
                 pl.BlockSpec((N_EXPERTS, MIX_TILE), col),
                 pl.BlockSpec((N_EXPERTS, MIX_TILE), col),
                 pl.BlockSpec((MIX_TILE, N_EXPERTS), row),
                 pl.BlockSpec((MIX_TILE, N_EXPERTS), row),
                 pl.BlockSpec((MIX_TILE, N_EXPERTS), row),
                 pl.BlockSpec((sub, N_EXPERTS, LANES), lead),
                 pl.BlockSpec((sub, SUBLANES, LANES), lead)]
    out_shape = [jax.ShapeDtypeStruct((n, d), F32),
                 jax.ShapeDtypeStruct((n, d), F32),
                 jax.ShapeDtypeStruct((N_EXPERTS, n), BF16),
                 jax.ShapeDtypeStruct((N_EXPERTS, n), BF16),
                 jax.ShapeDtypeStruct((n, N_EXPERTS), BF16),
                 jax.ShapeDtypeStruct((n, N_EXPERTS), BF16),
                 jax.ShapeDtypeStruct((n, N_EXPERTS), BF16),
                 jax.ShapeDtypeStruct((n // TOKEN_TILE, N_EXPERTS, LANES), F32),
                 jax.ShapeDtypeStruct((n // TOKEN_TILE, SUBLANES, LANES), F32)]
    return out_specs, out_shape


def _rglru_out_kernel(yf_ref, yr_ref, gate_ref, x0_ref, mod_ref, wout_ref, g2_ref, rwt_ref, rb_ref,
                      x1_ref, hx2_ref, hi_ref, lo_ref, hit_ref, lot_ref, gatet_ref, tab_ref, tabt_ref, *,
                      tiles_per_seq):
    d = x0_ref.shape[1]
    row = pl.program_id(0) // tiles_per_seq
    yx = yf_ref[...] + yr_ref[...]
    v = gate_ref[...].astype(F32) * yx
    out = jnp.dot(v.astype(BF16), wout_ref[...], preferred_element_type=F32)
    x1 = x0_ref[...] + _mod_chunk(mod_ref, row, 2, d) * out
    x1_ref[...] = x1
    _ffn_pre(x1, mod_ref, row, g2_ref, rwt_ref, rb_ref, hx2_ref, hi_ref, lo_ref, hit_ref, lot_ref, gatet_ref,
             tab_ref, tabt_ref)


def _rglru_out(y_fwd, y_rev, gate, x0, mod, w_out, g2, rwt, rb, tiles_per_seq):
    n, d = x0.shape
    c = gate.shape[1]
    row = lambda i: (i, 0)
    const = lambda i: (0, 0)
    out_specs, out_shape = _ffn_pre_specs(n, d)
    return pl.pallas_call(
        functools.partial(_rglru_out_kernel, tiles_per_seq=tiles_per_seq),
        grid=(n // MIX_TILE,),
        in_specs=[pl.BlockSpec((MIX_TILE, c), row),
                  pl.BlockSpec((MIX_TILE, c), row),
                  pl.BlockSpec((MIX_TILE, c), row),
                  pl.BlockSpec((MIX_TILE, d), row),
                  pl.BlockSpec(mod.shape, const),
                  pl.BlockSpec(w_out.shape, const),
                  pl.BlockSpec((1, d), const),
                  pl.BlockSpec(rwt.shape, const),
                  pl.BlockSpec(rb.shape, const)],
        out_specs=out_specs,
        out_shape=out_shape,
        compiler_params=_params(("arbitrary",)),
        name="rglru_out",
    )(y_fwd, y_rev, gate, x0, mod, w_out, g2, rwt, rb)


def _sgu_kernel(x_ref, mod_ref, g_ref, win_ref, lng_ref, lnb_ref, ws_ref, bst_ref, wout_ref,
                g2_ref, rwt_ref, rb_ref,
                x1_ref, hx2_ref, hi_ref, lo_ref, hit_ref, lot_ref, gatet_ref, tab_ref, tabt_ref, m_scr, *,
                tiles_per_seq):
    t_rows, d = x_ref.shape
    w = wout_ref.shape[0]
    gd = w // SGU_HEADS
    row = pl.program_id(0) // tiles_per_seq
    x = x_ref[...]
    hx = _rms_mod(x, g_ref[...], _mod_chunk(mod_ref, row, 1, d), _mod_chunk(mod_ref, row, 0, d))
    z = jax.nn.gelu(jnp.dot(hx.astype(BF16), win_ref[...], preferred_element_type=F32))
    u = z[:, :w]
    v = z[:, w:]
    mu = jnp.mean(v, axis=-1, keepdims=True)
    vc = v - mu
    v = vc * lax.rsqrt(jnp.mean(vc * vc, axis=-1, keepdims=True) + NORM_EPS) * lng_ref[...] + lnb_ref[...]
    vb = v.astype(BF16)
    for ch in range(t_rows // CHUNK):
        rs = slice(ch * CHUNK, (ch + 1) * CHUNK)
        for g in range(SGU_HEADS):
            cs = slice(g * gd, (g + 1) * gd)
            sv = jnp.dot(ws_ref[g], vb[rs, cs], preferred_element_type=F32) + bst_ref[:, g:g + 1]
            m_scr[rs, cs] = (u[rs, cs] * sv).astype(BF16)
    out = jnp.dot(m_scr[...], wout_ref[...], preferred_element_type=F32)
    x1 = x + _mod_chunk(mod_ref, row, 2, d) * out
    x1_ref[...] = x1
    _ffn_pre(x1, mod_ref, row, g2_ref, rwt_ref, rb_ref, hx2_ref, hi_ref, lo_ref, hit_ref, lot_ref, gatet_ref,
             tab_ref, tabt_ref)


def _sgu(x, mod, g, w_in, ln_g, ln_b, w_s, b_st, w_out, g2, rwt, rb, tiles_per_seq):
    n, d = x.shape
    w = w_out.shape[0]
    const = lambda i: (0, 0)
    out_specs, out_shape = _ffn_pre_specs(n, d)
    return pl.pallas_call(
        functools.partial(_sgu_kernel, tiles_per_seq=tiles_per_seq),
        grid=(n // MIX_TILE,),
        in_specs=[pl.BlockSpec((MIX_TILE, d), lambda i: (i, 0)),
                  pl.BlockSpec(mod.shape, const),
                  pl.BlockSpec((1, d), const),
                  pl.BlockSpec(w_in.shape, const, pipeline_mode=pl.Buffered(1)),
                  pl.BlockSpec((1, w), const),
                  pl.BlockSpec((1, w), const),
                  pl.BlockSpec(w_s.shape, lambda i: (0, 0, 0)),
                  pl.BlockSpec(b_st.shape, const),
                  pl.BlockSpec(w_out.shape, const, pipeline_mode=pl.Buffered(1)),
                  pl.BlockSpec((1, d), const),
                  pl.BlockSpec(rwt.shape, const),
                  pl.BlockSpec(rb.shape, const)],
        out_specs=out_specs,
        out_shape=out_shape,
        scratch_shapes=[pltpu.VMEM((MIX_TILE, w), BF16)],
        compiler_params=_params(("arbitrary",)),
        name="sgu",
    )(x, mod, g, w_in, ln_g, ln_b, w_s, b_st, w_out, g2, rwt, rb)


def _sorted_layout(m, eidx, gw):
    e, t = m.shape
    lanes = LANES
    r = lax.broadcasted_iota(jnp.int32, (t, t), 0)
    c = lax.broadcasted_iota(jnp.int32, (t, t), 1)
    upper = jnp.where(r <= c, 1.0, 0.0).astype(BF16)
    incl = jnp.dot(m.astype(BF16), upper, preferred_element_type=F32)
    run = _round_up_rows(incl[:, t - 1:t])
    re = lax.broadcasted_iota(jnp.int32, (e, e), 0)
    ce = lax.broadcasted_iota(jnp.int32, (e, e), 1)
    lower = jnp.where(ce < re, 1.0, 0.0).astype(BF16)
    tiles = jnp.broadcast_to(run * (1.0 / ROW_ALIGN), (e, lanes)).astype(BF16)
    lstart = jnp.dot(lower, tiles, preferred_element_type=F32)[:, 0:1] * float(ROW_ALIGN)
    pos = jnp.where(m > 0.0, lstart + incl - m, float(POS_RADIX * POS_RADIX - 1))
    hi = jnp.floor(pos * (1.0 / POS_RADIX))
    lo = pos - hi * float(POS_RADIX)
    iota_e = lax.broadcasted_iota(jnp.int32, (e, t), 0).astype(F32)
    gate = jnp.zeros((e, t), F32)
    for k in range(TOP_K):
        gate = jnp.where(iota_e == eidx[k:k + 1, :], gw[k:k + 1, :], gate)
    lane = lax.broadcasted_iota(jnp.int32, (e, lanes), 1)
    tab = jnp.where(lane == 0, run, jnp.where(lane == 2, lstart, 0.0))
    diag = lax.broadcasted_iota(jnp.int32, (e, lanes), 0) == lane
    row_of = lambda col: jnp.sum(jnp.where(diag, jnp.broadcast_to(col, (e, lanes)), 0.0), axis=0, keepdims=True)
    sub = lax.broadcasted_iota(jnp.int32, (SUBLANES, lanes), 0)
    tabt = jnp.where(sub == 0, row_of(lstart), jnp.where(sub == 1, row_of(lstart + run), 0.0))
    return hi, lo, gate, tab, tabt


def _aligned(v):
    return pl.multiple_of(v, ROW_ALIGN)


def _run_copies(tab_ref, make_copy, unroll):
    def body(e, carry):
        make_copy(_aligned(tab_ref[TAB_LOCAL + e]), _aligned(tab_ref[N_EXPERTS + e]),
                  _aligned(tab_ref[e])).start()
        return carry

    lax.fori_loop(0, N_EXPERTS, body, 0, unroll=unroll)


def _dispatch_kernel(tab_ref, ztab_ref, tabt_ref, hi_ref, lo_ref, x_ref, xs_hbm, sbuf, zbuf, sems, zsem):
    i = pl.program_id(0)
    slot = i % 2
    t = x_ref.shape[0]
    rows = sbuf.shape[1]

    def zero_copy(e):
        n = _aligned(ztab_ref[e])
        return pltpu.make_async_copy(zbuf.at[pl.ds(0, n)], xs_hbm.at[pl.ds(_aligned(ztab_ref[N_EXPERTS + e]), n)], zsem)

    def for_zero_runs(fn):
        def body(e, c):
            @pl.when(ztab_ref[e] > 0)
            def _():
                fn(zero_copy(e))
            return c
        lax.fori_loop(0, N_EXPERTS, body, 0)

        def tail(b, c):
            fn(pltpu.make_async_copy(zbuf, xs_hbm.at[pl.ds(pl.multiple_of(b * EXPERT_BLOCK, EXPERT_BLOCK),
                                                           EXPERT_BLOCK)], zsem))
            return c
        lax.fori_loop(ztab_ref[2 * N_EXPERTS], xs_hbm.shape[0] // EXPERT_BLOCK, tail, 0)

    @pl.when(i == 0)
    def _():
        zbuf[...] = jnp.zeros_like(zbuf)
        for_zero_runs(lambda cp: cp.start())

    xb = x_ref[...].astype(BF16)
    total = tab_ref[2 * N_EXPERTS]
    run_lo = tabt_ref[0, 0:1, 0:N_EXPERTS]
    run_hi = tabt_ref[0, 1:2, 0:N_EXPERTS]

    def permute(r0):
        j_e = (lax.broadcasted_iota(jnp.int32, (PERM_CHUNK, N_EXPERTS), 0) + r0).astype(F32)
        owner = jnp.where((j_e >= run_lo) & (j_e < run_hi), 1.0, 0.0).astype(BF16)
        pos = (jnp.dot(owner, hi_ref[...], preferred_element_type=F32) * float(POS_RADIX)
               + jnp.dot(owner, lo_ref[...], preferred_element_type=F32))
        j_t = (lax.broadcasted_iota(jnp.int32, (PERM_CHUNK, t), 0) + r0).astype(F32)
        p = jnp.where(pos == j_t, 1.0, 0.0).astype(BF16)
        sbuf[slot, r0:r0 + PERM_CHUNK, :] = jnp.dot(p, xb, preferred_element_type=F32).astype(BF16)

    for r0 in range(0, rows, PERM_CHUNK):
        if r0 < t * TOP_K + PERM_CHUNK:
            permute(r0)
        else:
            pl.when(total > r0)(functools.partial(permute, r0))

    _run_copies(tab_ref, lambda loc, glob, n: pltpu.make_async_copy(
        sbuf.at[slot, pl.ds(loc, n)], xs_hbm.at[pl.ds(glob, n)], sems.at[slot]), unroll=True)

    def wait_rows(s, n):
        pltpu.make_async_copy(sbuf.at[s, pl.ds(0, n)], xs_hbm.at[pl.ds(0, n)], sems.at[s]).wait()

    @pl.when(i > 0)
    def _():
        wait_rows(1 - slot, _aligned(tab_ref[2 * N_EXPERTS + 1]))

    @pl.when(i == pl.num_programs(0) - 1)
    def _():
        wait_rows(slot, _aligned(tab_ref[2 * N_EXPERTS]))

    @pl.when(i == 0)
    def _():
        for_zero_runs(lambda cp: cp.wait())


def _dispatch(tab, ztab, tab_t, pos_hi, pos_lo, hx2, xs_rows):
    n, d = hx2.shape
    col = lambda i: (0, i)
    return pl.pallas_call(
        _dispatch_kernel,
        grid=(n // TOKEN_TILE,),
        in_specs=[pl.BlockSpec((TAB_WIDTH,), lambda i: (i,), memory_space=pltpu.SMEM),
                  pl.BlockSpec(memory_space=pltpu.SMEM),
                  pl.BlockSpec((1, SUBLANES, LANES), lambda i: (i, 0, 0)),
                  pl.BlockSpec((N_EXPERTS, TOKEN_TILE), col),
                  pl.BlockSpec((N_EXPERTS, TOKEN_TILE), col),
                  pl.BlockSpec((TOKEN_TILE, d), lambda i: (i, 0))],
        out_specs=pl.BlockSpec(memory_space=pl.ANY),
        out_shape=jax.ShapeDtypeStruct((xs_rows, d), BF16),
        scratch_shapes=[pltpu.VMEM((2, SORTED_ROWS, d), BF16),
                        pltpu.VMEM((EXPERT_BLOCK, d), BF16),
                        pltpu.SemaphoreType.DMA((2,)),
                        pltpu.SemaphoreType.DMA],
        compiler_params=_params(("arbitrary",)),
        name="moe_dispatch",
    )(tab, ztab, tab_t, pos_hi, pos_lo, hx2)


def _experts_kernel(be_ref, nb_ref, xs_ref, wgu_ref, wd_ref, ys_ref, wgu_b, wd_b):
    i = pl.program_id(0)
    ff = wd_b.shape[0]
    used = i < nb_ref[0]
    new_expert = (i == 0) | (be_ref[i] != be_ref[jnp.maximum(i - 1, 0)])

    @pl.when(used & new_expert)
    def _():
        wgu_b[...] = wgu_ref[0, 0].astype(BF16)
        wd_b[...] = wd_ref[0, 0].astype(BF16)

    @pl.when(used)
    def _():
        h = jnp.dot(xs_ref[...], wgu_b[...], preferred_element_type=F32)
        a = _silu(h[:, :ff]) * h[:, ff:]
        ys_ref[...] = jnp.dot(a.astype(BF16), wd_b[...], preferred_element_type=F32).astype(BF16)

    @pl.when(jnp.logical_not(used))
    def _():
        ys_ref[...] = jnp.zeros_like(ys_ref)


def _experts(block_e, n_used, xs, w_gu, w_down, layer):
    rows, d = xs.shape
    nb = rows // EXPERT_BLOCK
    blk = lambda i, be, nu: (jnp.maximum(jnp.minimum(i, nu[0] - 1), 0), 0)
    wmap = lambda i, be, nu: (layer, be[i], 0, 0)
    return pl.pallas_call(
        _experts_kernel,
        grid_spec=pltpu.PrefetchScalarGridSpec(
            num_scalar_prefetch=2,
            grid=(nb,),
            in_specs=[pl.BlockSpec((EXPERT_BLOCK, d), blk),
                      pl.BlockSpec((1, 1) + w_gu.shape[2:], wmap),
                      pl.BlockSpec((1, 1) + w_down.shape[2:], wmap)],
            out_specs=pl.BlockSpec((EXPERT_BLOCK, d), lambda i, be, nu: (i, 0)),
            scratch_shapes=[pltpu.VMEM(w_gu.shape[2:], BF16), pltpu.VMEM(w_down.shape[2:], BF16)]),
        out_shape=jax.ShapeDtypeStruct((rows, d), BF16),
        compiler_params=_params(("arbitrary",)),
        name="moe_experts",
    )(block_e, n_used, xs, w_gu, w_down)


def _combine_kernel(tab_ref, tabn_ref, tabv_ref, hi_ref, lo_ref, gate_ref, ys_hbm, hx2_ref, x1_ref, mod_ref,
                    wsgu_ref, wsd_ref, fg_ref, out_ref, ybuf, sems, *, tiles_per_seq, final_norm):
    t_rows, d = x1_ref.shape
    ff = wsd_ref.shape[0]
    rows = ybuf.shape[1]
    i = pl.program_id(0)
    last = pl.num_programs(0) - 1
    slot = i % 2
    row = i // tiles_per_seq

    def gather(table, s, unroll):
        _run_copies(table, lambda loc, glob, n: pltpu.make_async_copy(
            ys_hbm.at[pl.ds(glob, n)], ybuf.at[s, pl.ds(loc, n)], sems.at[s]), unroll=unroll)

    def wait_rows(table, s):
        n = _aligned(table[2 * N_EXPERTS])
        pltpu.make_async_copy(ys_hbm.at[pl.ds(0, n)], ybuf.at[s, pl.ds(0, n)], sems.at[s]).wait()

    @pl.when(i == 0)
    def _():
        ybuf[...] = jnp.zeros_like(ybuf)
        gather(tab_ref, 0, False)

    gather(tabn_ref, 1 - slot, True)

    hs = jnp.dot(hx2_ref[...].astype(BF16), wsgu_ref[...], preferred_element_type=F32)
    shared = jnp.dot((_silu(hs[:, :ff]) * hs[:, ff:]).astype(BF16), wsd_ref[...], preferred_element_type=F32)

    run_lo = tabv_ref[0, :, 2:3]
    run_hi = run_lo + tabv_ref[0, :, 0:1]
    j_e = lax.broadcasted_iota(jnp.int32, (N_EXPERTS, rows), 1).astype(F32)
    owner = jnp.where((j_e >= run_lo) & (j_e < run_hi), 1.0, 0.0).astype(BF16)
    pos = (jnp.dot(hi_ref[...], owner, preferred_element_type=F32) * float(POS_RADIX)
           + jnp.dot(lo_ref[...], owner, preferred_element_type=F32))
    gates = jnp.dot(gate_ref[...], owner, preferred_element_type=F32)
    j_t = lax.broadcasted_iota(jnp.int32, (t_rows, rows), 1).astype(F32)
    gb = jnp.where(pos == j_t, gates, 0.0).astype(BF16)

    wait_rows(tab_ref, slot)
    routed = jnp.dot(gb, ybuf[slot], preferred_element_type=F32)
    x2 = x1_ref[...] + _mod_chunk(mod_ref, row, 5, d) * (routed + shared)
    if final_norm:
        x2 = x2 * lax.rsqrt(jnp.mean(x2 * x2, axis=-1, keepdims=True) + NORM_EPS) * fg_ref[...]
    out_ref[...] = x2

    @pl.when(i == last)
    def _():
        wait_rows(tabn_ref, 1 - slot)


def _combine(tab, tab_v, pos_hi, pos_lo, gate, ys, hx2, x1, mod, ws_gu, ws_down, fg, tiles_per_seq, final_norm):
    n, d = x1.shape
    row = lambda i: (i, 0)
    const = lambda i: (0, 0)
    return pl.pallas_call(
        functools.partial(_combine_kernel, tiles_per_seq=tiles_per_seq, final_norm=final_norm),
        grid=(n // TOKEN_TILE,),
        in_specs=[pl.BlockSpec((TAB_WIDTH,), lambda i: (i,), memory_space=pltpu.SMEM),
                  pl.BlockSpec((TAB_WIDTH,), lambda i: (jnp.minimum(i + 1, n // TOKEN_TILE - 1),),
                               memory_space=pltpu.SMEM),
                  pl.BlockSpec((1, N_EXPERTS, LANES), lambda i: (i, 0, 0)),
                  pl.BlockSpec((TOKEN_TILE, N_EXPERTS), row),
                  pl.BlockSpec((TOKEN_TILE, N_EXPERTS), row),
                  pl.BlockSpec((TOKEN_TILE, N_EXPERTS), row),
                  pl.BlockSpec(memory_space=pl.ANY),
                  pl.BlockSpec((TOKEN_TILE, d), row),
                  pl.BlockSpec((TOKEN_TILE, d), row),
                  pl.BlockSpec(mod.shape, const),
                  pl.BlockSpec(ws_gu.shape, const),
                  pl.BlockSpec(ws_down.shape, const),
                  pl.BlockSpec((1, d), const)],
        out_specs=pl.BlockSpec((TOKEN_TILE, d), row),
        out_shape=jax.ShapeDtypeStruct((n, d), F32),
        scratch_shapes=[pltpu.VMEM((2, SORTED_ROWS, d), BF16), pltpu.SemaphoreType.DMA((2,))],
        compiler_params=_params(("arbitrary",)),
        name="moe_combine",
    )(tab, tab, tab_v, pos_hi, pos_lo, gate, ys, hx2, x1, mod, ws_gu, ws_down, fg)


def _moe(x1, hx2, pos_hi, pos_lo, pos_hi_t, pos_lo_t, gate_t, tab_f, tab_t, mod, w_gu, w_down, layer, ws_gu,
         ws_down, fg, tiles_per_seq, final_norm):
    n, d = x1.shape
    n_tiles = n // TOKEN_TILE
    run_len = tab_f[:, :, 0].astype(jnp.int32)
    counts = jnp.sum(run_len, axis=0)
    padded = (counts + EXPERT_BLOCK - 1) // EXPERT_BLOCK * EXPERT_BLOCK
    pad_end = jnp.cumsum(padded)
    pad_start = pad_end - padded
    max_rows = n * TOP_K + n_tiles * N_EXPERTS * ROW_ALIGN + N_EXPERTS * (EXPERT_BLOCK - ROW_ALIGN)
    n_blocks = (max_rows + EXPERT_BLOCK - 1) // EXPERT_BLOCK
    n_used = (pad_end[-1:] // EXPERT_BLOCK).astype(jnp.int32)
    block_start = jnp.arange(n_blocks, dtype=jnp.int32) * EXPERT_BLOCK
    block_e = jnp.minimum(jnp.sum((pad_end[None, :] <= block_start[:, None]).astype(jnp.int32), axis=1),
                          N_EXPERTS - 1)

    run_start = pad_start[None, :] + jnp.cumsum(run_len, axis=0) - run_len
    total = jnp.sum(run_len, axis=1, keepdims=True)
    prev_total = jnp.concatenate([jnp.zeros((1, 1), jnp.int32), total[:-1]], axis=0)
    run_local = tab_f[:, :, 2].astype(jnp.int32)
    fill = jnp.zeros((n_tiles, TAB_WIDTH - 3 * N_EXPERTS - 2), jnp.int32)
    tab = jnp.concatenate([run_len, run_start, total, prev_total, run_local, fill], axis=1).reshape(-1)
    ztab = jnp.concatenate([padded - counts, pad_start + counts, n_used])

    xs = _dispatch(tab, ztab, tab_t, pos_hi, pos_lo, hx2, n_blocks * EXPERT_BLOCK)
    ys = _experts(block_e, n_used, xs, w_gu, w_down, layer)
    return _combine(tab, tab_f, pos_hi_t, pos_lo_t, gate_t, ys, hx2, x1, mod, ws_gu, ws_down, fg,
                    tiles_per_seq, final_norm)


def _sincos_tables(rows, d):
    quarter = d // 4
    omega = 1.0 / (POS_BASE ** (jnp.arange(quarter, dtype=F32) / quarter))

    def emb(n):
        p = jnp.arange(n, dtype=F32)[:, None] * omega[None, :]
        return jnp.concatenate([jnp.sin(p), jnp.cos(p)], axis=-1)

    return emb(rows), emb(GRID_W)


def kernel(x, c, ctx, c_ctx, ada_w, ada_b, mix_norm_g, ffn_norm_g, a_w_in, a_conv_w, a_conv_b, a_gate_r_w, a_gate_r_b, a_gate_i_w, a_gate_i_b, a_lambda, a_w_out, b_w_in, b_ln_g, b_ln_b, b_w_s, b_b_s, b_w_out, router_w, router_b, moe_w_gu, moe_w_down, shared_w_gu, shared_w_down, final_norm_g):
    bsz, s, d = x.shape
    ctx_len = ctx.shape[1]
    depth = ada_w.shape[0]
    assert depth == 2 and bsz < MOD_ROWS and s % MIX_TILE == 0 and MIX_TILE % TOKEN_TILE == 0
    assert s % SCAN_TILE == 0 and MIX_TILE % GRID_W == 0
    assert ctx_len % TOKEN_TILE == 0
    n = bsz * s
    tps = s // TOKEN_TILE
    ctx_row = bsz

    cc = jnp.zeros((MOD_ROWS, d), F32).at[:bsz].set(c).at[ctx_row].set(c_ctx)
    mod = _modulation(cc, ada_w, ada_b)
    er, ec = _sincos_tables(s // GRID_W, d)
    rc = a_w_in.shape[2] // 2

    w_in0 = a_w_in[0].astype(BF16)
    g_mix0 = mix_norm_g[0].reshape(1, d)
    x0, gate, ux = _rglru_in(x.reshape(n, d), er, ec, mod[0], g_mix0, w_in0)
    uc = _ctx_in(ctx.reshape(bsz * ctx_len, d), mod[0], g_mix0, w_in0[:, rc:], ctx_row)
    w_ri = jnp.concatenate([a_gate_r_w[0], a_gate_i_w[0]], axis=-1).astype(BF16)
    conv = (a_conv_w[0], a_conv_b[0])
    gates = [(w_ri[k], a_gate_r_b[0, k], a_gate_i_b[0, k], a_lambda[0, k]) for k in range(2)]
    h_zero = jnp.zeros((bsz, rc), F32)
    h_fwd, uc = _lru_scan(uc, conv, *gates[0], h_zero, reverse=False, reset_first=True, emit_y=False)
    h_rev = _lru_scan(uc, None, *gates[1], h_zero, reverse=True, reset_first=True, emit_y=False)
    y_fwd, ux = _lru_scan(ux, conv, *gates[0], h_fwd, reverse=False, reset_first=False, emit_y=True)
    y_rev = _lru_scan(ux, None, *gates[1], h_rev, reverse=True, reset_first=False, emit_y=True)
    pre = _rglru_out(y_fwd, y_rev, gate, x0, mod[0], a_w_out[0].astype(BF16), ffn_norm_g[0].reshape(1, d),
                     router_w[0].T, router_b[0].reshape(N_EXPERTS, 1), s // MIX_TILE)
    x1 = _moe(*pre, mod[0], moe_w_gu, moe_w_down, 0, shared_w_gu[0].astype(BF16),
              shared_w_down[0].astype(BF16), final_norm_g.reshape(1, d), tps, False)

    pre = _sgu(x1, mod[1], mix_norm_g[1].reshape(1, d), b_w_in[0].astype(BF16),
               b_ln_g[0].reshape(1, -1), b_ln_b[0].reshape(1, -1), b_w_s[0].astype(BF16), b_b_s[0].T,
               b_w_out[0].astype(BF16), ffn_norm_g[1].reshape(1, d),
               router_w[1].T, router_b[1].reshape(N_EXPERTS, 1), s // MIX_TILE)
    out = _moe(*pre, mod[1], moe_w_gu, moe_w_down, 1, shared_w_gu[1].astype(BF16),
               shared_w_down[1].astype(BF16), final_norm_g.reshape(1, d), tps, True)
    return out.reshape(bsz, s, d)
```

```python
import functools

import jax
import jax.numpy as jnp
from jax import lax
from jax.experimental import pallas as pl
from jax.experimental.pallas import tpu as pltpu

F32 = jnp.float32
BF16 = jnp.bfloat16
HIGHEST = lax.Precision.HIGHEST

GRID_W = 64
NORM_EPS = 1e-6
POS_BASE = 10000.0
RNN_HEADS = 5
CONV_WIDTH = 4
CONV_PAD_LEFT = 2
LRU_C = 8.0
SGU_HEADS = 8
CHUNK = 128
N_EXPERTS = 64
TOP_K = 8
N_GROUPS = 8
TOPK_GROUPS = 4
EXPERTS_PER_GROUP = N_EXPERTS // N_GROUPS
ROUTED_SCALE = 2.5

SUBLANES = 8
ROW_ALIGN = 16
LANES = 128
MOD_ROWS = 8
TOKEN_TILE = 256
MIX_TILE = 512
SCAN_TILE = 512
EXPERT_BLOCK = 1024
SORTED_ROWS = TOKEN_TILE * TOP_K + N_EXPERTS * ROW_ALIGN
PERM_CHUNK = 512
TAB_WIDTH = 256
TAB_LOCAL = 2 * N_EXPERTS + 2
POS_RADIX = 64
VMEM_LIMIT = 56 * 1024 * 1024


def _params(semantics, vmem=VMEM_LIMIT):
    return pltpu.CompilerParams(dimension_semantics=semantics, vmem_limit_bytes=vmem)


def _silu(x):
    return x * jax.nn.sigmoid(x)


def _rms_mod(x, g, sc, sh):
    y = x * lax.rsqrt(jnp.mean(x * x, axis=-1, keepdims=True) + NORM_EPS)
    return (y * g) * (1.0 + sc) + sh


def _mod_chunk(mod_ref, row, k, d):
    return mod_ref[pl.ds(row, 1), k * d:(k + 1) * d]


def _round_up_rows(count):
    return jnp.maximum(jnp.ceil(count * (1.0 / ROW_ALIGN)), 1.0) * float(ROW_ALIGN)


def _mod_kernel(cc_ref, w_ref, b_ref, o_ref):
    s = _silu(cc_ref[...])
    o_ref[0] = jnp.dot(s, w_ref[0], preferred_element_type=F32, precision=HIGHEST) + b_ref[0]


def _modulation(cc, ada_w, ada_b):
    depth, d, nd = ada_w.shape
    return pl.pallas_call(
        _mod_kernel,
        grid=(depth, nd // d),
        in_specs=[pl.BlockSpec((MOD_ROWS, d), lambda l, j: (0, 0)),
                  pl.BlockSpec((1, d, d), lambda l, j: (l, 0, j)),
                  pl.BlockSpec((1, 1, d), lambda l, j: (l, 0, j))],
        out_specs=pl.BlockSpec((1, MOD_ROWS, d), lambda l, j: (l, 0, j)),
        out_shape=jax.ShapeDtypeStruct((depth, MOD_ROWS, nd), F32),
        compiler_params=_params(("arbitrary", "arbitrary")),
        name="modulation",
    )(cc, ada_w, ada_b.reshape(depth, 1, nd))


def _with_pos_code(x_ref, er_ref, ec_ref, tiles_per_seq):
    t_rows = x_ref.shape[0]
    grid_w, half = ec_ref.shape
    rows = t_rows // grid_w
    r0 = pl.multiple_of((pl.program_id(0) % tiles_per_seq) * rows, rows)
    er = er_ref[pl.ds(r0, rows), :]
    pe = jnp.concatenate([jnp.broadcast_to(er[:, None, :], (rows, grid_w, half)).reshape(t_rows, half),
                          jnp.concatenate([ec_ref[...]] * rows, axis=0)], axis=1)
    return x_ref[...] + pe


def _rglru_in_kernel(x_ref, er_ref, ec_ref, mod_ref, g_ref, w_ref, gate_ref, u_ref, *, tiles_per_seq):
    d = x_ref.shape[1]
    c = u_ref.shape[1]
    row = pl.program_id(0) // tiles_per_seq
    x = _with_pos_code(x_ref, er_ref, ec_ref, tiles_per_seq)
    hx = _rms_mod(x, g_ref[...], _mod_chunk(mod_ref, row, 1, d), _mod_chunk(mod_ref, row, 0, d))
    z = jnp.dot(hx.astype(BF16), w_ref[...], preferred_element_type=F32)
    gate_ref[...] = jax.nn.gelu(z[:, :c]).astype(BF16)
    u_ref[...] = z[:, c:]


def _rglru_in(x2, er, ec, mod, g, w_in):
    n, d = x2.shape
    c = w_in.shape[1] // 2
    tps = er.shape[0] * ec.shape[0] // MIX_TILE
    row = lambda i: (i, 0)
    return pl.pallas_call(
        functools.partial(_rglru_in_kernel, tiles_per_seq=tps),
        grid=(n // MIX_TILE,),
        in_specs=[pl.BlockSpec((MIX_TILE, d), row),
                  pl.BlockSpec(er.shape, lambda i: (0, 0)),
                  pl.BlockSpec(ec.shape, lambda i: (0, 0)),
                  pl.BlockSpec(mod.shape, lambda i: (0, 0)),
                  pl.BlockSpec((1, d), lambda i: (0, 0)),
                  pl.BlockSpec(w_in.shape, lambda i: (0, 0))],
        out_specs=[pl.BlockSpec((MIX_TILE, c), row),
                   pl.BlockSpec((MIX_TILE, c), row)],
        out_shape=[jax.ShapeDtypeStruct((n, c), BF16),
                   jax.ShapeDtypeStruct((n, c), F32)],
        compiler_params=_params(("arbitrary",)),
        name="rglru_in",
    )(x2, er, ec, mod, g, w_in)


def _ctx_in_kernel(x_ref, mod_ref, g_ref, w_ref, u_ref, *, ctx_row):
    d = x_ref.shape[1]
    hx = _rms_mod(x_ref[...], g_ref[...], _mod_chunk(mod_ref, ctx_row, 1, d),
                  _mod_chunk(mod_ref, ctx_row, 0, d))
    u_ref[...] = jnp.dot(hx.astype(BF16), w_ref[...], preferred_element_type=F32)


def _ctx_in(c2, mod, g, w_u, ctx_row):
    n, d = c2.shape
    c = w_u.shape[1]
    return pl.pallas_call(
        functools.partial(_ctx_in_kernel, ctx_row=ctx_row),
        grid=(n // TOKEN_TILE,),
        in_specs=[pl.BlockSpec((TOKEN_TILE, d), lambda i: (i, 0)),
                  pl.BlockSpec(mod.shape, lambda i: (0, 0)),
                  pl.BlockSpec((1, d), lambda i: (0, 0)),
                  pl.BlockSpec(w_u.shape, lambda i: (0, 0))],
        out_specs=pl.BlockSpec((TOKEN_TILE, c), lambda i: (i, 0)),
        out_shape=jax.ShapeDtypeStruct((n, c), F32),
        compiler_params=_params(("arbitrary",)),
        name="ctx_in",
    )(c2, mod, g, w_u)


def _log_sigmoid(x):
    return jnp.minimum(x, 0.0) - jnp.log1p(jnp.exp(-jnp.abs(x)))


def _lru_scan_kernel(*refs, n_tiles, reverse, conv_done, reset_first, emit_y):
    if conv_done:
        u_ref, wri_ref, rb_ref, ib_ref, lam_ref, h0_ref, out_ref, a_scr, b_scr, h_scr = refs
    else:
        (u_ref, up_ref, un_ref, cw_ref, cb_ref, wri_ref, rb_ref, ib_ref, lam_ref, h0_ref,
         out_ref, uc_ref, ubuf, a_scr, b_scr, h_scr) = refs
    t_rows, c = u_ref.shape
    hb = c // RNN_HEADS
    b = pl.program_id(0)
    j = pl.program_id(1)
    jj = n_tiles - 1 - j if reverse else j

    if conv_done:
        u = u_ref[...]
    else:
        ubuf[SUBLANES:SUBLANES + t_rows, :] = u_ref[...]
        ubuf[0:SUBLANES, :] = jnp.where(jj == 0, 0.0, up_ref[...])
        ubuf[SUBLANES + t_rows:, :] = jnp.where(jj == n_tiles - 1, 0.0, un_ref[...])
        u = cb_ref[...]
        for k in range(CONV_WIDTH):
            u = u + cw_ref[k:k + 1, :] * ubuf[pl.ds(SUBLANES - CONV_PAD_LEFT + k, t_rows), :]
        uc_ref[...] = u

    log_lam = LRU_C * _log_sigmoid(lam_ref[...])
    rows = lax.broadcasted_iota(jnp.int32, (t_rows, 1), 0)
    first_row = jnp.where(j == 0, t_rows - 1 if reverse else 0, -1)
    for h in range(RNN_HEADS):
        sl = slice(h * hb, (h + 1) * hb)
        uh = u[:, sl]
        z = jnp.dot(uh.astype(BF16), wri_ref[h], preferred_element_type=F32)
        r = jax.nn.sigmoid(z[:, :hb] + rb_ref[:, sl])
        ig = jax.nn.sigmoid(z[:, hb:] + ib_ref[:, sl])
        log_a = r * log_lam[:, sl]
        a = jnp.exp(log_a)
        mult = jnp.sqrt((1.0 - a) * (1.0 + a))
        if reset_first:
            mult = jnp.where(rows == first_row, 1.0, mult)
        a_scr[:, sl] = a
        b_scr[:, sl] = mult * ig * uh

    @pl.when(j == 0)
    def _():
        h_scr[...] = h0_ref[pl.ds(b, 1), :]

    n_groups = t_rows // SUBLANES

    def group(g, h):
        base = pl.multiple_of((n_groups - 1 - g if reverse else g) * SUBLANES, SUBLANES)
        for s in range(SUBLANES):
            r = base + (SUBLANES - 1 - s if reverse else s)
            h = a_scr[pl.ds(r, 1), :] * h + b_scr[pl.ds(r, 1), :]
            if emit_y:
                out_ref[pl.ds(r, 1), :] = h
        return h

    h = lax.fori_loop(0, n_groups, group, h_scr[...])
    h_scr[...] = h
    if not emit_y:
        @pl.when(j == n_tiles - 1)
        def _():
            out_ref[pl.ds(b, 1), :] = h


def _lru_scan(u, conv, w_ri, r_b, i_b, lam, h0, *, reverse, reset_first, emit_y):
    n, c = u.shape
    n_batch = h0.shape[0]
    rows = min(SCAN_TILE, n // n_batch)
    n_tiles = n // n_batch // rows
    sub = rows // SUBLANES
    n_sub = n // SUBLANES
    const = lambda b, j: (0, 0)

    def tile(b, j):
        return b * n_tiles + (n_tiles - 1 - j if reverse else j)

    tile_spec = pl.BlockSpec((rows, c), lambda b, j: (tile(b, j), 0))
    in_specs = [tile_spec]
    args = [u]
    scratch = []
    if conv is not None:
        conv_w, conv_b = conv
        in_specs += [pl.BlockSpec((SUBLANES, c), lambda b, j: (jnp.maximum(tile(b, j) * sub - 1, 0), 0)),
                     pl.BlockSpec((SUBLANES, c), lambda b, j: (jnp.minimum((tile(b, j) + 1) * sub, n_sub - 1), 0)),
                     pl.BlockSpec(conv_w.shape, const),
                     pl.BlockSpec((1, c), const)]
        args += [u, u, conv_w, conv_b.reshape(1, c)]
        scratch = [pltpu.VMEM((rows + 2 * SUBLANES, c), F32)]
    in_specs += [pl.BlockSpec(w_ri.shape, lambda b, j: (0, 0, 0)),
                 pl.BlockSpec((1, c), const), pl.BlockSpec((1, c), const), pl.BlockSpec((1, c), const),
                 pl.BlockSpec(h0.shape, const)]
    args += [w_ri, r_b.reshape(1, c), i_b.reshape(1, c), lam.reshape(1, c), h0]
    if emit_y:
        out_specs = [tile_spec]
        out_shape = [jax.ShapeDtypeStruct((n, c), F32)]
    else:
        out_specs = [pl.BlockSpec(h0.shape, const)]
        out_shape = [jax.ShapeDtypeStruct(h0.shape, F32)]
    if conv is not None:
        out_specs.append(tile_spec)
        out_shape.append(jax.ShapeDtypeStruct((n, c), F32))
    name = ("lru_scan" if emit_y else "lru_ctx") + ("_rev" if reverse else "_fwd")
    outs = pl.pallas_call(
        functools.partial(_lru_scan_kernel, n_tiles=n_tiles, reverse=reverse, conv_done=conv is None,
                          reset_first=reset_first, emit_y=emit_y),
        grid=(n_batch, n_tiles),
        in_specs=in_specs,
        out_specs=out_specs,
        out_shape=out_shape,
        scratch_shapes=scratch + [pltpu.VMEM((rows, c), F32),
                                  pltpu.VMEM((rows, c), F32),
                                  pltpu.VMEM((1, c), F32)],
        compiler_params=_params(("arbitrary", "arbitrary")),
        name=name,
    )(*args)
    return outs if conv is not None else outs[0]


def _route(logits, rb):
    e, t = logits.shape
    neg = -jnp.inf
    scores = jax.nn.sigmoid(logits)
    sel = scores + rb
    iota_g = lax.broadcasted_iota(jnp.int32, (N_GROUPS, t), 0).astype(F32)
    iota_e = lax.broadcasted_iota(jnp.int32, (e, t), 0).astype(F32)

    gs = jnp.full((N_GROUPS, t), neg, F32)
    for g in range(N_GROUPS):
        sg = sel[g * EXPERTS_PER_GROUP:(g + 1) * EXPERTS_PER_GROUP, :]
        m1 = jnp.max(sg, axis=0, keepdims=True)
        i1 = jnp.min(jnp.where(sg == m1, iota_g, float(EXPERTS_PER_GROUP)), axis=0, keepdims=True)
        m2 = jnp.max(jnp.where(iota_g == i1, neg, sg), axis=0, keepdims=True)
        gs = jnp.where(iota_g == float(g), m1 + m2, gs)

    keep = jnp.zeros((N_GROUPS, t), F32)
    for _ in range(TOPK_GROUPS):
        m = jnp.max(gs, axis=0, keepdims=True)
        idx = jnp.min(jnp.where(gs == m, iota_g, float(N_GROUPS)), axis=0, keepdims=True)
        hit = iota_g == idx
        keep = jnp.where(hit, 1.0, keep)
        gs = jnp.where(hit, neg, gs)

    masked = jnp.concatenate(
        [jnp.where(keep[g:g + 1, :] > 0.0, sel[g * EXPERTS_PER_GROUP:(g + 1) * EXPERTS_PER_GROUP, :], neg)
         for g in range(N_GROUPS)], axis=0)

    iota_k = lax.broadcasted_iota(jnp.int32, (TOP_K, t), 0)
    selmask = jnp.zeros((e, t), F32)
    eidx = jnp.zeros((TOP_K, t), F32)
    gw = jnp.zeros((TOP_K, t), F32)
    for k in range(TOP_K):
        m = jnp.max(masked, axis=0, keepdims=True)
        idx = jnp.min(jnp.where(masked == m, iota_e, float(e)), axis=0, keepdims=True)
        hit = iota_e == idx
        gk = jnp.sum(jnp.where(hit, scores, 0.0), axis=0, keepdims=True)
        masked = jnp.where(hit, neg, masked)
        selmask = jnp.where(hit, 1.0, selmask)
        eidx = jnp.where(iota_k == k, idx, eidx)
        gw = jnp.where(iota_k == k, gk, gw)
    gw = gw / jnp.sum(gw, axis=0, keepdims=True) * ROUTED_SCALE
    return eidx, gw, selmask


def _ffn_pre(x1, mod_ref, row, g2_ref, rwt_ref, rb_ref, hx2_ref, hi_ref, lo_ref, hit_ref, lot_ref, gatet_ref,
             tab_ref, tabt_ref):
    d = x1.shape[1]
    hx2 = _rms_mod(x1, g2_ref[...], _mod_chunk(mod_ref, row, 4, d), _mod_chunk(mod_ref, row, 3, d))
    hx2_ref[...] = hx2.astype(BF16)
    logits = lax.dot_general(rwt_ref[...], hx2, (((1,), (1,)), ((), ())),
                             preferred_element_type=F32, precision=HIGHEST)
    eidx, gw, selmask = _route(logits, rb_ref[...])
    for s in range(selmask.shape[1] // TOKEN_TILE):
        ts = slice(s * TOKEN_TILE, (s + 1) * TOKEN_TILE)
        hi, lo, gate, tab, tabt = _sorted_layout(selmask[:, ts], eidx[:, ts], gw[:, ts])
        hi_ref[:, ts] = hi.astype(BF16)
        lo_ref[:, ts] = lo.astype(BF16)
        hit_ref[ts, :] = hi.T.astype(BF16)
        lot_ref[ts, :] = lo.T.astype(BF16)
        gatet_ref[ts, :] = gate.T.astype(BF16)
        tab_ref[s] = tab
        tabt_ref[s] = tabt


def _ffn_pre_specs(n, d):
    row = lambda i: (i, 0)
    col = lambda i: (0, i)
    sub = MIX_TILE // TOKEN_TILE
    lead = lambda i: (i, 0, 0)
    out_specs = [pl.BlockSpec((MIX_TILE, d), row),
                 pl.BlockSpec((MIX_TILE, d), row),
                 pl.BlockSpec((N_EXPERTS, MIX_TILE), col),
                 pl.BlockSpec((N_EXPERTS, MIX_TILE), col),
                 pl.BlockSpec((MIX_TILE, N_EXPERTS), row),
                 pl.BlockSpec((MIX_TILE, N_EXPERTS), row),
                 pl.BlockSpec((MIX_TILE, N_EXPERTS), row),
                 pl.BlockSpec((sub, N_EXPERTS, LANES), lead),
                 pl.BlockSpec((sub, SUBLANES, LANES), lead)]
    out_shape = [jax.ShapeDtypeStruct((n, d), F32),
                 jax.ShapeDtypeStruct((n, d), BF16),
                 jax.ShapeDtypeStruct((N_EXPERTS, n), BF16),
                 jax.ShapeDtypeStruct((N_EXPERTS, n), BF16),
                 jax.ShapeDtypeStruct((n, N_EXPERTS), BF16),
                 jax.ShapeDtypeStruct((n, N_EXPERTS), BF16),
                 jax.ShapeDtypeStruct((n, N_EXPERTS), BF16),
                 jax.ShapeDtypeStruct((n // TOKEN_TILE, N_EXPERTS, LANES), F32),
                 jax.ShapeDtypeStruct((n // TOKEN_TILE, SUBLANES, LANES), F32)]
    return out_specs, out_shape


def _rglru_out_kernel(yf_ref, yr_ref, gate_ref, x_ref, er_ref, ec_ref, mod_ref, wout_ref, g2_ref, rwt_ref,
                      rb_ref, x1_ref, hx2_ref, hi_ref, lo_ref, hit_ref, lot_ref, gatet_ref, tab_ref, tabt_ref,
                      *, tiles_per_seq):
    d = x_ref.shape[1]
    row = pl.program_id(0) // tiles_per_seq
    yx = yf_ref[...] + yr_ref[...]
    v = gate_ref[...].astype(F32) * yx
    out = jnp.dot(v.astype(BF16), wout_ref[...], preferred_element_type=F32)
    x1 = _with_pos_code(x_ref, er_ref, ec_ref, tiles_per_seq) + _mod_chunk(mod_ref, row, 2, d) * out
    x1_ref[...] = x1
    _ffn_pre(x1, mod_ref, row, g2_ref, rwt_ref, rb_ref, hx2_ref, hi_ref, lo_ref, hit_ref, lot_ref, gatet_ref,
             tab_ref, tabt_ref)


def _rglru_out(y_fwd, y_rev, gate, x, er, ec, mod, w_out, g2, rwt, rb, tiles_per_seq):
    n, d = x.shape
    c = gate.shape[1]
    row = lambda i: (i, 0)
    const = lambda i: (0, 0)
    out_specs, out_shape = _ffn_pre_specs(n, d)
    return pl.pallas_call(
        functools.partial(_rglru_out_kernel, tiles_per_seq=tiles_per_seq),
        grid=(n // MIX_TILE,),
        in_specs=[pl.BlockSpec((MIX_TILE, c), row),
                  pl.BlockSpec((MIX_TILE, c), row),
                  pl.BlockSpec((MIX_TILE, c), row),
                  pl.BlockSpec((MIX_TILE, d), row),
                  pl.BlockSpec(er.shape, const),
                  pl.BlockSpec(ec.shape, const),
                  pl.BlockSpec(mod.shape, const),
                  pl.BlockSpec(w_out.shape, const),
                  pl.BlockSpec((1, d), const),
                  pl.BlockSpec(rwt.shape, const),
                  pl.BlockSpec(rb.shape, const)],
        out_specs=out_specs,
        out_shape=out_shape,
        compiler_params=_params(("arbitrary",)),
        name="rglru_out",
    )(y_fwd, y_rev, gate, x, er, ec, mod, w_out, g2, rwt, rb)


def _sgu_kernel(x_ref, mod_ref, g_ref, win_ref, lng_ref, lnb_ref, ws_ref, bst_ref, wout_ref,
                g2_ref, rwt_ref, rb_ref,
                x1_ref, hx2_ref, hi_ref, lo_ref, hit_ref, lot_ref, gatet_ref, tab_ref, tabt_ref, m_scr, *,
                tiles_per_seq):
    t_rows, d = x_ref.shape
    w = wout_ref.shape[0]
    gd = w // SGU_HEADS
    row = pl.program_id(0) // tiles_per_seq
    x = x_ref[...]
    hx = _rms_mod(x, g_ref[...], _mod_chunk(mod_ref, row, 1, d), _mod_chunk(mod_ref, row, 0, d))
    z = jax.nn.gelu(jnp.dot(hx.astype(BF16), win_ref[...], preferred_element_type=F32))
    u = z[:, :w]
    v = z[:, w:]
    mu = jnp.mean(v, axis=-1, keepdims=True)
    vc = v - mu
    v = vc * lax.rsqrt(jnp.mean(vc * vc, axis=-1, keepdims=True) + NORM_EPS) * lng_ref[...] + lnb_ref[...]
    vb = v.astype(BF16)
    for ch in range(t_rows // CHUNK):
        rs = slice(ch * CHUNK, (ch + 1) * CHUNK)
        for g in range(SGU_HEADS):
            cs = slice(g * gd, (g + 1) * gd)
            sv = jnp.dot(ws_ref[g], vb[rs, cs], preferred_element_type=F32) + bst_ref[:, g:g + 1]
            m_scr[rs, cs] = (u[rs, cs] * sv).astype(BF16)
    out = jnp.dot(m_scr[...], wout_ref[...], preferred_element_type=F32)
    x1 = x + _mod_chunk(mod_ref, row, 2, d) * out
    x1_ref[...] = x1
    _ffn_pre(x1, mod_ref, row, g2_ref, rwt_ref, rb_ref, hx2_ref, hi_ref, lo_ref, hit_ref, lot_ref, gatet_ref,
             tab_ref, tabt_ref)


def _sgu(x, mod, g, w_in, ln_g, ln_b, w_s, b_st, w_out, g2, rwt, rb, tiles_per_seq):
    n, d = x.shape
    w = w_out.shape[0]
    const = lambda i: (0, 0)
    out_specs, out_shape = _ffn_pre_specs(n, d)
    return pl.pallas_call(
        functools.partial(_sgu_kernel, tiles_per_seq=tiles_per_seq),
        grid=(n // MIX_TILE,),
        in_specs=[pl.BlockSpec((MIX_TILE, d), lambda i: (i, 0)),
                  pl.BlockSpec(mod.shape, const),
                  pl.BlockSpec((1, d), const),
                  pl.BlockSpec(w_in.shape, const, pipeline_mode=pl.Buffered(1)),
                  pl.BlockSpec((1, w), const),
                  pl.BlockSpec((1, w), const),
                  pl.BlockSpec(w_s.shape, lambda i: (0, 0, 0)),
                  pl.BlockSpec(b_st.shape, const),
                  pl.BlockSpec(w_out.shape, const, pipeline_mode=pl.Buffered(1)),
                  pl.BlockSpec((1, d), const),
                  pl.BlockSpec(rwt.shape, const),
                  pl.BlockSpec(rb.shape, const)],
        out_specs=out_specs,
        out_shape=out_shape,
        scratch_shapes=[pltpu.VMEM((MIX_TILE, w), BF16)],
        compiler_params=_params(("arbitrary",)),
        name="sgu",
    )(x, mod, g, w_in, ln_g, ln_b, w_s, b_st, w_out, g2, rwt, rb)


def _sorted_layout(m, eidx, gw):
    e, t = m.shape
    lanes = LANES
    r = lax.broadcasted_iota(jnp.int32, (t, t), 0)
    c = lax.broadcasted_iota(jnp.int32, (t, t), 1)
    upper = jnp.where(r <= c, 1.0, 0.0).astype(BF16)
    incl = jnp.dot(m.astype(BF16), upper, preferred_element_type=F32)
    run = _round_up_rows(incl[:, t - 1:t])
    re = lax.broadcasted_iota(jnp.int32, (e, e), 0)
    ce = lax.broadcasted_iota(jnp.int32, (e, e), 1)
    lower = jnp.where(ce < re, 1.0, 0.0).astype(BF16)
    tiles = jnp.broadcast_to(run * (1.0 / ROW_ALIGN), (e, lanes)).astype(BF16)
    lstart = jnp.dot(lower, tiles, preferred_element_type=F32)[:, 0:1] * float(ROW_ALIGN)
    pos = jnp.where(m > 0.0, lstart + incl - m, float(POS_RADIX * POS_RADIX - 1))
    hi = jnp.floor(pos * (1.0 / POS_RADIX))
    lo = pos - hi * float(POS_RADIX)
    iota_e = lax.broadcasted_iota(jnp.int32, (e, t), 0).astype(F32)
    gate = jnp.zeros((e, t), F32)
    for k in range(TOP_K):
        gate = jnp.where(iota_e == eidx[k:k + 1, :], gw[k:k + 1, :], gate)
    lane = lax.broadcasted_iota(jnp.int32, (e, lanes), 1)
    tab = jnp.where(lane == 0, run, jnp.where(lane == 2, lstart, 0.0))
    diag = lax.broadcasted_iota(jnp.int32, (e, lanes), 0) == lane
    row_of = lambda col: jnp.sum(jnp.where(diag, jnp.broadcast_to(col, (e, lanes)), 0.0), axis=0, keepdims=True)
    sub = lax.broadcasted_iota(jnp.int32, (SUBLANES, lanes), 0)
    tabt = jnp.where(sub == 0, row_of(lstart), jnp.where(sub == 1, row_of(lstart + run), 0.0))
    return hi, lo, gate, tab, tabt


def _aligned(v):
    return pl.multiple_of(v, ROW_ALIGN)


def _run_copies(tab_ref, make_copy, unroll):
    def body(e, carry):
        make_copy(_aligned(tab_ref[TAB_LOCAL + e]), _aligned(tab_ref[N_EXPERTS + e]),
                  _aligned(tab_ref[e])).start()
        return carry

    lax.fori_loop(0, N_EXPERTS, body, 0, unroll=unroll)


def _dispatch_kernel(tab_ref, ztab_ref, tabt_ref, hi_ref, lo_ref, x_ref, xs_hbm, sbuf, zbuf, sems, zsem):
    i = pl.program_id(0)
    slot = i % 2
    t = x_ref.shape[0]
    rows = sbuf.shape[1]

    def zero_copy(e):
        n = _aligned(ztab_ref[e])
        return pltpu.make_async_copy(zbuf.at[pl.ds(0, n)], xs_hbm.at[pl.ds(_aligned(ztab_ref[N_EXPERTS + e]), n)], zsem)

    def for_zero_runs(fn):
        def body(e, c):
            @pl.when(ztab_ref[e] > 0)
            def _():
                fn(zero_copy(e))
            return c
        lax.fori_loop(0, N_EXPERTS, body, 0)

        def tail(b, c):
            fn(pltpu.make_async_copy(zbuf, xs_hbm.at[pl.ds(pl.multiple_of(b * EXPERT_BLOCK, EXPERT_BLOCK),
                                                           EXPERT_BLOCK)], zsem))
            return c
        lax.fori_loop(ztab_ref[2 * N_EXPERTS], xs_hbm.shape[0] // EXPERT_BLOCK, tail, 0)

    @pl.when(i == 0)
    def _():
        zbuf[...] = jnp.zeros_like(zbuf)
        for_zero_runs(lambda cp: cp.start())

    xb = x_ref[...]
    total = tab_ref[2 * N_EXPERTS]
    run_lo = tabt_ref[0, 0:1, 0:N_EXPERTS]
    run_hi = tabt_ref[0, 1:2, 0:N_EXPERTS]

    def permute(r0):
        j_e = (lax.broadcasted_iota(jnp.int32, (PERM_CHUNK, N_EXPERTS), 0) + r0).astype(F32)
        owner = jnp.where((j_e >= run_lo) & (j_e < run_hi), 1.0, 0.0).astype(BF16)
        pos = (jnp.dot(owner, hi_ref[...], preferred_element_type=F32) * float(POS_RADIX)
               + jnp.dot(owner, lo_ref[...], preferred_element_type=F32))
        j_t = (lax.broadcasted_iota(jnp.int32, (PERM_CHUNK, t), 0) + r0).astype(F32)
        p = jnp.where(pos == j_t, 1.0, 0.0).astype(BF16)
        sbuf[slot, r0:r0 + PERM_CHUNK, :] = jnp.dot(p, xb, preferred_element_type=F32).astype(BF16)

    for r0 in range(0, rows, PERM_CHUNK):
        if r0 < t * TOP_K + PERM_CHUNK:
            permute(r0)
        else:
            pl.when(total > r0)(functools.partial(permute, r0))

    _run_copies(tab_ref, lambda loc, glob, n: pltpu.make_async_copy(
        sbuf.at[slot, pl.ds(loc, n)], xs_hbm.at[pl.ds(glob, n)], sems.at[slot]), unroll=True)

    def wait_rows(s, n):
        pltpu.make_async_copy(sbuf.at[s, pl.ds(0, n)], xs_hbm.at[pl.ds(0, n)], sems.at[s]).wait()

    @pl.when(i > 0)
    def _():
        wait_rows(1 - slot, _aligned(tab_ref[2 * N_EXPERTS + 1]))

    @pl.when(i == pl.num_programs(0) - 1)
    def _():
        wait_rows(slot, _aligned(tab_ref[2 * N_EXPERTS]))

    @pl.when(i == 0)
    def _():
        for_zero_runs(lambda cp: cp.wait())


def _dispatch(tab, ztab, tab_t, pos_hi, pos_lo, hx2, xs_rows):
    n, d = hx2.shape
    col = lambda i: (0, i)
    return pl.pallas_call(
        _dispatch_kernel,
        grid=(n // TOKEN_TILE,),
        in_specs=[pl.BlockSpec((TAB_WIDTH,), lambda i: (i,), memory_space=pltpu.SMEM),
                  pl.BlockSpec(memory_space=pltpu.SMEM),
                  pl.BlockSpec((1, SUBLANES, LANES), lambda i: (i, 0, 0)),
                  pl.BlockSpec((N_EXPERTS, TOKEN_TILE), col),
                  pl.BlockSpec((N_EXPERTS, TOKEN_TILE), col),
                  pl.BlockSpec((TOKEN_TILE, d), lambda i: (i, 0))],
        out_specs=pl.BlockSpec(memory_space=pl.ANY),
        out_shape=jax.ShapeDtypeStruct((xs_rows, d), BF16),
        scratch_shapes=[pltpu.VMEM((2, SORTED_ROWS, d), BF16),
                        pltpu.VMEM((EXPERT_BLOCK, d), BF16),
                        pltpu.SemaphoreType.DMA((2,)),
                        pltpu.SemaphoreType.DMA],
        compiler_params=_params(("arbitrary",)),
        name="moe_dispatch",
    )(tab, ztab, tab_t, pos_hi, pos_lo, hx2)


def _experts_kernel(be_ref, nb_ref, xs_ref, wgu_ref, wd_ref, ys_ref, wgu_b, wd_b):
    i = pl.program_id(0)
    ff = wd_b.shape[0]
    used = i < nb_ref[0]
    new_expert = (i == 0) | (be_ref[i] != be_ref[jnp.maximum(i - 1, 0)])

    @pl.when(used & new_expert)
    def _():
        wgu_b[...] = wgu_ref[0, 0].astype(BF16)
        wd_b[...] = wd_ref[0, 0].astype(BF16)

    @pl.when(used)
    def _():
        h = jnp.dot(xs_ref[...], wgu_b[...], preferred_element_type=F32)
        a = _silu(h[:, :ff]) * h[:, ff:]
        ys_ref[...] = jnp.dot(a.astype(BF16), wd_b[...], preferred_element_type=F32).astype(BF16)

    @pl.when(jnp.logical_not(used))
    def _():
        ys_ref[...] = jnp.zeros_like(ys_ref)


def _experts(block_e, n_used, xs, w_gu, w_down, layer):
    rows, d = xs.shape
    nb = rows // EXPERT_BLOCK
    blk = lambda i, be, nu: (jnp.maximum(jnp.minimum(i, nu[0] - 1), 0), 0)
    wmap = lambda i, be, nu: (layer, be[i], 0, 0)
    return pl.pallas_call(
        _experts_kernel,
        grid_spec=pltpu.PrefetchScalarGridSpec(
            num_scalar_prefetch=2,
            grid=(nb,),
            in_specs=[pl.BlockSpec((EXPERT_BLOCK, d), blk),
                      pl.BlockSpec((1, 1) + w_gu.shape[2:], wmap),
                      pl.BlockSpec((1, 1) + w_down.shape[2:], wmap)],
            out_specs=pl.BlockSpec((EXPERT_BLOCK, d), lambda i, be, nu: (i, 0)),
            scratch_shapes=[pltpu.VMEM(w_gu.shape[2:], BF16), pltpu.VMEM(w_down.shape[2:], BF16)]),
        out_shape=jax.ShapeDtypeStruct((rows, d), BF16),
        compiler_params=_params(("arbitrary",)),
        name="moe_experts",
    )(block_e, n_used, xs, w_gu, w_down)


def _combine_kernel(tab_ref, tabn_ref, tabv_ref, hi_ref, lo_ref, gate_ref, ys_hbm, hx2_ref, x1_ref, mod_ref,
                    wsgu_ref, wsd_ref, fg_ref, out_ref, ybuf, sems, *, tiles_per_seq, final_norm):
    t_rows, d = x1_ref.shape
    ff = wsd_ref.shape[0]
    rows = ybuf.shape[1]
    i = pl.program_id(0)
    last = pl.num_programs(0) - 1
    slot = i % 2
    row = i // tiles_per_seq

    def gather(table, s, unroll):
        _run_copies(table, lambda loc, glob, n: pltpu.make_async_copy(
            ys_hbm.at[pl.ds(glob, n)], ybuf.at[s, pl.ds(loc, n)], sems.at[s]), unroll=unroll)

    def wait_rows(table, s):
        n = _aligned(table[2 * N_EXPERTS])
        pltpu.make_async_copy(ys_hbm.at[pl.ds(0, n)], ybuf.at[s, pl.ds(0, n)], sems.at[s]).wait()

    @pl.when(i == 0)
    def _():
        ybuf[...] = jnp.zeros_like(ybuf)
        gather(tab_ref, 0, False)

    gather(tabn_ref, 1 - slot, True)

    hs = jnp.dot(hx2_ref[...], wsgu_ref[...], preferred_element_type=F32)
    shared = jnp.dot((_silu(hs[:, :ff]) * hs[:, ff:]).astype(BF16), wsd_ref[...], preferred_element_type=F32)

    run_lo = tabv_ref[0, :, 2:3]
    run_hi = run_lo + tabv_ref[0, :, 0:1]
    j_e = lax.broadcasted_iota(jnp.int32, (N_EXPERTS, rows), 1).astype(F32)
    owner = jnp.where((j_e >= run_lo) & (j_e < run_hi), 1.0, 0.0).astype(BF16)
    pos = (jnp.dot(hi_ref[...], owner, preferred_element_type=F32) * float(POS_RADIX)
           + jnp.dot(lo_ref[...], owner, preferred_element_type=F32))
    gates = jnp.dot(gate_ref[...], owner, preferred_element_type=F32)
    j_t = lax.broadcasted_iota(jnp.int32, (t_rows, rows), 1).astype(F32)
    gb = jnp.where(pos == j_t, gates, 0.0).astype(BF16)

    wait_rows(tab_ref, slot)
    routed = jnp.dot(gb, ybuf[slot], preferred_element_type=F32)
    x2 = x1_ref[...] + _mod_chunk(mod_ref, row, 5, d) * (routed + shared)
    if final_norm:
        x2 = x2 * lax.rsqrt(jnp.mean(x2 * x2, axis=-1, keepdims=True) + NORM_EPS) * fg_ref[...]
    out_ref[...] = x2

    @pl.when(i == last)
    def _():
        wait_rows(tabn_ref, 1 - slot)


def _combine(tab, tab_v, pos_hi, pos_lo, gate, ys, hx2, x1, mod, ws_gu, ws_down, fg, tiles_per_seq, final_norm):
    n, d = x1.shape
    row = lambda i: (i, 0)
    const = lambda i: (0, 0)
    return pl.pallas_call(
        functools.partial(_combine_kernel, tiles_per_seq=tiles_per_seq, final_norm=final_norm),
        grid=(n // TOKEN_TILE,),
        in_specs=[pl.BlockSpec((TAB_WIDTH,), lambda i: (i,), memory_space=pltpu.SMEM),
                  pl.BlockSpec((TAB_WIDTH,), lambda i: (jnp.minimum(i + 1, n // TOKEN_TILE - 1),),
                               memory_space=pltpu.SMEM),
                  pl.BlockSpec((1, N_EXPERTS, LANES), lambda i: (i, 0, 0)),
                  pl.BlockSpec((TOKEN_TILE, N_EXPERTS), row),
                  pl.BlockSpec((TOKEN_TILE, N_EXPERTS), row),
                  pl.BlockSpec((TOKEN_TILE, N_EXPERTS), row),
                  pl.BlockSpec(memory_space=pl.ANY),
                  pl.BlockSpec((TOKEN_TILE, d), row),
                  pl.BlockSpec((TOKEN_TILE, d), row),
                  pl.BlockSpec(mod.shape, const),
                  pl.BlockSpec(ws_gu.shape, const),
                  pl.BlockSpec(ws_down.shape, const),
                  pl.BlockSpec((1, d), const)],
        out_specs=pl.BlockSpec((TOKEN_TILE, d), row),
        out_shape=jax.ShapeDtypeStruct((n, d), F32),
        scratch_shapes=[pltpu.VMEM((2, SORTED_ROWS, d), BF16), pltpu.SemaphoreType.DMA((2,))],
        compiler_params=_params(("arbitrary",)),
        name="moe_combine",
    )(tab, tab, tab_v, pos_hi, pos_lo, gate, ys, hx2, x1, mod, ws_gu, ws_down, fg)


def _moe(x1, hx2, pos_hi, pos_lo, pos_hi_t, pos_lo_t, gate_t, tab_f, tab_t, mod, w_gu, w_down, layer, ws_gu,
         ws_down, fg, tiles_per_seq, final_norm):
    n, d = x1.shape
    n_tiles = n // TOKEN_TILE
    run_len = tab_f[:, :, 0].astype(jnp.int32)
    counts = jnp.sum(run_len, axis=0)
    padded = (counts + EXPERT_BLOCK - 1) // EXPERT_BLOCK * EXPERT_BLOCK
    pad_end = jnp.cumsum(padded)
    pad_start = pad_end - padded
    max_rows = n * TOP_K + n_tiles * N_EXPERTS * ROW_ALIGN + N_EXPERTS * (EXPERT_BLOCK - ROW_ALIGN)
    n_blocks = (max_rows + EXPERT_BLOCK - 1) // EXPERT_BLOCK
    n_used = (pad_end[-1:] // EXPERT_BLOCK).astype(jnp.int32)
    block_start = jnp.arange(n_blocks, dtype=jnp.int32) * EXPERT_BLOCK
    block_e = jnp.minimum(jnp.sum((pad_end[None, :] <= block_start[:, None]).astype(jnp.int32), axis=1),
                          N_EXPERTS - 1)

    run_start = pad_start[None, :] + jnp.cumsum(run_len, axis=0) - run_len
    total = jnp.sum(run_len, axis=1, keepdims=True)
    prev_total = jnp.concatenate([jnp.zeros((1, 1), jnp.int32), total[:-1]], axis=0)
    run_local = tab_f[:, :, 2].astype(jnp.int32)
    fill = jnp.zeros((n_tiles, TAB_WIDTH - 3 * N_EXPERTS - 2), jnp.int32)
    tab = jnp.concatenate([run_len, run_start, total, prev_total, run_local, fill], axis=1).reshape(-1)
    ztab = jnp.concatenate([padded - counts, pad_start + counts, n_used])

    xs = _dispatch(tab, ztab, tab_t, pos_hi, pos_lo, hx2, n_blocks * EXPERT_BLOCK)
    ys = _experts(block_e, n_used, xs, w_gu, w_down, layer)
    return _combine(tab, tab_f, pos_hi_t, pos_lo_t, gate_t, ys, hx2, x1, mod, ws_gu, ws_down, fg,
                    tiles_per_seq, final_norm)


def _sincos_tables(rows, d):
    quarter = d // 4
    omega = 1.0 / (POS_BASE ** (jnp.arange(quarter, dtype=F32) / quarter))

    def emb(n):
        p = jnp.arange(n, dtype=F32)[:, None] * omega[None, :]
        return jnp.concatenate([jnp.sin(p), jnp.cos(p)], axis=-1)

    return emb(rows), emb(GRID_W)


def kernel(x, c, ctx, c_ctx, ada_w, ada_b, mix_norm_g, ffn_norm_g, a_w_in, a_conv_w, a_conv_b, a_gate_r_w, a_gate_r_b, a_gate_i_w, a_gate_i_b, a_lambda, a_w_out, b_w_in, b_ln_g, b_ln_b, b_w_s, b_b_s, b_w_out, router_w, router_b, moe_w_gu, moe_w_down, shared_w_gu, shared_w_down, final_norm_g):
    bsz, s, d = x.shape
    ctx_len = ctx.shape[1]
    depth = ada_w.shape[0]
    assert depth == 2 and bsz < MOD_ROWS and s % MIX_TILE == 0 and MIX_TILE % TOKEN_TILE == 0
    assert s % SCAN_TILE == 0 and MIX_TILE % GRID_W == 0
    assert ctx_len % TOKEN_TILE == 0
    n = bsz * s
    tps = s // TOKEN_TILE
    ctx_row = bsz

    cc = jnp.zeros((MOD_ROWS, d), F32).at[:bsz].set(c).at[ctx_row].set(c_ctx)
    mod = _modulation(cc, ada_w, ada_b)
    er, ec = _sincos_tables(s // GRID_W, d)
    rc = a_w_in.shape[2] // 2

    w_in0 = a_w_in[0].astype(BF16)
    g_mix0 = mix_norm_g[0].reshape(1, d)
    x2 = x.reshape(n, d)
    gate, ux = _rglru_in(x2, er, ec, mod[0], g_mix0, w_in0)
    uc = _ctx_in(ctx.reshape(bsz * ctx_len, d), mod[0], g_mix0, w_in0[:, rc:], ctx_row)
    w_ri = jnp.concatenate([a_gate_r_w[0], a_gate_i_w[0]], axis=-1).astype(BF16)
    conv = (a_conv_w[0], a_conv_b[0])
    gates = [(w_ri[k], a_gate_r_b[0, k], a_gate_i_b[0, k], a_lambda[0, k]) for k in range(2)]
    h_zero = jnp.zeros((bsz, rc), F32)
    h_fwd, uc = _lru_scan(uc, conv, *gates[0], h_zero, reverse=False, reset_first=True, emit_y=False)
    h_rev = _lru_scan(uc, None, *gates[1], h_zero, reverse=True, reset_first=True, emit_y=False)
    y_fwd, ux = _lru_scan(ux, conv, *gates[0], h_fwd, reverse=False, reset_first=False, emit_y=True)
    y_rev = _lru_scan(ux, None, *gates[1], h_rev, reverse=True, reset_first=False, emit_y=True)
    pre = _rglru_out(y_fwd, y_rev, gate, x2, er, ec, mod[0], a_w_out[0].astype(BF16), ffn_norm_g[0].reshape(1, d),
                     router_w[0].T, router_b[0].reshape(N_EXPERTS, 1), s // MIX_TILE)
    x1 = _moe(*pre, mod[0], moe_w_gu, moe_w_down, 0, shared_w_gu[0].astype(BF16),
              shared_w_down[0].astype(BF16), final_norm_g.reshape(1, d), tps, False)

    pre = _sgu(x1, mod[1], mix_norm_g[1].reshape(1, d), b_w_in[0].astype(BF16),
               b_ln_g[0].reshape(1, -1), b_ln_b[0].reshape(1, -1), b_w_s[0].astype(BF16), b_b_s[0].T,
               b_w_out[0].astype(BF16), ffn_norm_g[1].reshape(1, d),
               router_w[1].T, router_b[1].reshape(N_EXPERTS, 1), s // MIX_TILE)
    out = _moe(*pre, mod[1], moe_w_gu, moe_w_down, 1, shared_w_gu[1].astype(BF16),
               shared_w_down[1].astype(BF16), final_norm_g.reshape(1, d), tps, True)
    return out.reshape(bsz, s, d)
```

```python
import functools

import jax
import jax.numpy as jnp
from jax import lax
from jax.experimental import pallas as pl
from jax.experimental.pallas import tpu as pltpu

F32 = jnp.float32
BF16 = jnp.bfloat16
HIGHEST = lax.Precision.HIGHEST

GRID_W = 64
NORM_EPS = 1e-6
POS_BASE = 10000.0
RNN_HEADS = 5
CONV_WIDTH = 4
CONV_PAD_LEFT = 2
LRU_C = 8.0
SGU_HEADS = 8
CHUNK = 128
N_EXPERTS = 64
TOP_K = 8
N_GROUPS = 8
TOPK_GROUPS = 4
EXPERTS_PER_GROUP = N_EXPERTS // N_GROUPS
ROUTED_SCALE = 2.5

SUBLANES = 8
ROW_ALIGN = 16
LANES = 128
MOD_ROWS = 8
TOKEN_TILE = 256
MIX_TILE = 512
SCAN_TILE = 512
EXPERT_BLOCK = 1024
SORTED_ROWS = TOKEN_TILE * TOP_K + N_EXPERTS * ROW_ALIGN
PERM_CHUNK = 512
DISPATCH_TILES = 2
TAB_WIDTH = 256
TAB_LOCAL = 2 * N_EXPERTS + 2
POS_RADIX = 64
VMEM_LIMIT = 56 * 1024 * 1024


def _params(semantics, vmem=VMEM_LIMIT):
    return pltpu.CompilerParams(dimension_semantics=semantics, vmem_limit_bytes=vmem)


def _silu(x):
    return x * jax.nn.sigmoid(x)


def _rms_mod(x, g, sc, sh):
    y = x * lax.rsqrt(jnp.mean(x * x, axis=-1, keepdims=True) + NORM_EPS)
    return (y * g) * (1.0 + sc) + sh


def _mod_chunk(mod_ref, row, k, d):
    return mod_ref[pl.ds(row, 1), k * d:(k + 1) * d]


def _dot_nt_3pass(a, b):
    dims = (((1,), (1,)), ((), ()))
    a_hi, b_hi = a.astype(BF16), b.astype(BF16)
    a_lo = (a - a_hi.astype(F32)).astype(BF16)
    b_lo = (b - b_hi.astype(F32)).astype(BF16)
    dot = functools.partial(lax.dot_general, dimension_numbers=dims, preferred_element_type=F32)
    return dot(a_hi, b_hi) + dot(a_hi, b_lo) + dot(a_lo, b_hi)


def _round_up_rows(count):
    return jnp.maximum(jnp.ceil(count * (1.0 / ROW_ALIGN)), 1.0) * float(ROW_ALIGN)


def _mod_kernel(cc_ref, w_ref, b_ref, o_ref):
    s = _silu(cc_ref[...])
    o_ref[0] = jnp.dot(s, w_ref[0], preferred_element_type=F32, precision=HIGHEST) + b_ref[0]


def _modulation(cc, ada_w, ada_b):
    depth, d, nd = ada_w.shape
    return pl.pallas_call(
        _mod_kernel,
        grid=(depth, nd // d),
        in_specs=[pl.BlockSpec((MOD_ROWS, d), lambda l, j: (0, 0)),
                  pl.BlockSpec((1, d, d), lambda l, j: (l, 0, j)),
                  pl.BlockSpec((1, 1, d), lambda l, j: (l, 0, j))],
        out_specs=pl.BlockSpec((1, MOD_ROWS, d), lambda l, j: (l, 0, j)),
        out_shape=jax.ShapeDtypeStruct((depth, MOD_ROWS, nd), F32),
        compiler_params=_params(("arbitrary", "arbitrary")),
        name="modulation",
    )(cc, ada_w, ada_b.reshape(depth, 1, nd))


def _with_pos_code(x_ref, er_ref, ec_ref, tiles_per_seq):
    t_rows = x_ref.shape[0]
    grid_w, half = ec_ref.shape
    rows = t_rows // grid_w
    r0 = pl.multiple_of((pl.program_id(0) % tiles_per_seq) * rows, rows)
    er = er_ref[pl.ds(r0, rows), :]
    pe = jnp.concatenate([jnp.broadcast_to(er[:, None, :], (rows, grid_w, half)).reshape(t_rows, half),
                          jnp.concatenate([ec_ref[...]] * rows, axis=0)], axis=1)
    return x_ref[...] + pe


def _rglru_in_kernel(x_ref, er_ref, ec_ref, mod_ref, g_ref, w_ref, gate_ref, u_ref, *, tiles_per_seq):
    d = x_ref.shape[1]
    c = u_ref.shape[1]
    row = pl.program_id(0) // tiles_per_seq
    x = _with_pos_code(x_ref, er_ref, ec_ref, tiles_per_seq)
    hx = _rms_mod(x, g_ref[...], _mod_chunk(mod_ref, row, 1, d), _mod_chunk(mod_ref, row, 0, d))
    z = jnp.dot(hx.astype(BF16), w_ref[...], preferred_element_type=F32)
    gate_ref[...] = jax.nn.gelu(z[:, :c]).astype(BF16)
    u_ref[...] = z[:, c:]


def _rglru_in(x2, er, ec, mod, g, w_in):
    n, d = x2.shape
    c = w_in.shape[1] // 2
    tps = er.shape[0] * ec.shape[0] // MIX_TILE
    row = lambda i: (i, 0)
    return pl.pallas_call(
        functools.partial(_rglru_in_kernel, tiles_per_seq=tps),
        grid=(n // MIX_TILE,),
        in_specs=[pl.BlockSpec((MIX_TILE, d), row),
                  pl.BlockSpec(er.shape, lambda i: (0, 0)),
                  pl.BlockSpec(ec.shape, lambda i: (0, 0)),
                  pl.BlockSpec(mod.shape, lambda i: (0, 0)),
                  pl.BlockSpec((1, d), lambda i: (0, 0)),
                  pl.BlockSpec(w_in.shape, lambda i: (0, 0))],
        out_specs=[pl.BlockSpec((MIX_TILE, c), row),
                   pl.BlockSpec((MIX_TILE, c), row)],
        out_shape=[jax.ShapeDtypeStruct((n, c), BF16),
                   jax.ShapeDtypeStruct((n, c), F32)],
        compiler_params=_params(("arbitrary",)),
        name="rglru_in",
    )(x2, er, ec, mod, g, w_in)


def _ctx_in_kernel(x_ref, mod_ref, g_ref, w_ref, u_ref, *, ctx_row):
    d = x_ref.shape[1]
    hx = _rms_mod(x_ref[...], g_ref[...], _mod_chunk(mod_ref, ctx_row, 1, d),
                  _mod_chunk(mod_ref, ctx_row, 0, d))
    u_ref[...] = jnp.dot(hx.astype(BF16), w_ref[...], preferred_element_type=F32)


def _ctx_in(c2, mod, g, w_u, ctx_row):
    n, d = c2.shape
    c = w_u.shape[1]
    return pl.pallas_call(
        functools.partial(_ctx_in_kernel, ctx_row=ctx_row),
        grid=(n // TOKEN_TILE,),
        in_specs=[pl.BlockSpec((TOKEN_TILE, d), lambda i: (i, 0)),
                  pl.BlockSpec(mod.shape, lambda i: (0, 0)),
                  pl.BlockSpec((1, d), lambda i: (0, 0)),
                  pl.BlockSpec(w_u.shape, lambda i: (0, 0))],
        out_specs=pl.BlockSpec((TOKEN_TILE, c), lambda i: (i, 0)),
        out_shape=jax.ShapeDtypeStruct((n, c), F32),
        compiler_params=_params(("arbitrary",)),
        name="ctx_in",
    )(c2, mod, g, w_u)


def _log_sigmoid(x):
    return jnp.minimum(x, 0.0) - jnp.log1p(jnp.exp(-jnp.abs(x)))


def _lru_scan_kernel(*refs, n_tiles, reverse, conv_done, reset_first, emit_y):
    if conv_done:
        u_ref, wri_ref, rb_ref, ib_ref, lam_ref, h0_ref, out_ref, a_scr, b_scr, h_scr = refs
    else:
        (u_ref, up_ref, un_ref, cw_ref, cb_ref, wri_ref, rb_ref, ib_ref, lam_ref, h0_ref,
         out_ref, uc_ref, ubuf, a_scr, b_scr, h_scr) = refs
    t_rows, c = u_ref.shape
    hb = c // RNN_HEADS
    b = pl.program_id(0)
    j = pl.program_id(1)
    jj = n_tiles - 1 - j if reverse else j

    if conv_done:
        u = u_ref[...]
    else:
        ubuf[SUBLANES:SUBLANES + t_rows, :] = u_ref[...]
        ubuf[0:SUBLANES, :] = jnp.where(jj == 0, 0.0, up_ref[...])
        ubuf[SUBLANES + t_rows:, :] = jnp.where(jj == n_tiles - 1, 0.0, un_ref[...])
        u = cb_ref[...]
        full = ubuf[...]
        for k in range(CONV_WIDTH):
            shift = (CONV_PAD_LEFT - k) % full.shape[0]
            tap = full if shift == 0 else pltpu.roll(full, shift, 0)
            u = u + cw_ref[k:k + 1, :] * tap[SUBLANES:SUBLANES + t_rows, :]
        uc_ref[...] = u

    log_lam = LRU_C * _log_sigmoid(lam_ref[...])
    rows = lax.broadcasted_iota(jnp.int32, (t_rows, 1), 0)
    first_row = jnp.where(j == 0, t_rows - 1 if reverse else 0, -1)
    for h in range(RNN_HEADS):
        sl = slice(h * hb, (h + 1) * hb)
        uh = u[:, sl]
        z = jnp.dot(uh.astype(BF16), wri_ref[h], preferred_element_type=F32)
        r = jax.nn.sigmoid(z[:, :hb] + rb_ref[:, sl])
        ig = jax.nn.sigmoid(z[:, hb:] + ib_ref[:, sl])
        log_a = r * log_lam[:, sl]
        a = jnp.exp(log_a)
        mult = jnp.sqrt((1.0 - a) * (1.0 + a))
        if reset_first:
            mult = jnp.where(rows == first_row, 1.0, mult)
        a_scr[:, sl] = a
        b_scr[:, sl] = mult * ig * uh

    @pl.when(j == 0)
    def _():
        h_scr[...] = h0_ref[pl.ds(b, 1), :]

    n_groups = t_rows // SUBLANES

    def group(g, h):
        base = pl.multiple_of((n_groups - 1 - g if reverse else g) * SUBLANES, SUBLANES)
        for s in range(SUBLANES):
            r = base + (SUBLANES - 1 - s if reverse else s)
            h = a_scr[pl.ds(r, 1), :] * h + b_scr[pl.ds(r, 1), :]
            if emit_y:
                out_ref[pl.ds(r, 1), :] = h
        return h

    h = lax.fori_loop(0, n_groups, group, h_scr[...])
    h_scr[...] = h
    if not emit_y:
        @pl.when(j == n_tiles - 1)
        def _():
            out_ref[pl.ds(b, 1), :] = h


def _lru_scan(u, conv, w_ri, r_b, i_b, lam, h0, *, reverse, reset_first, emit_y):
    n, c = u.shape
    n_batch = h0.shape[0]
    rows = min(SCAN_TILE, n // n_batch)
    n_tiles = n // n_batch // rows
    sub = rows // SUBLANES
    n_sub = n // SUBLANES
    const = lambda b, j: (0, 0)

    def tile(b, j):
        return b * n_tiles + (n_tiles - 1 - j if reverse else j)

    tile_spec = pl.BlockSpec((rows, c), lambda b, j: (tile(b, j), 0))
    in_specs = [tile_spec]
    args = [u]
    scratch = []
    if conv is not None:
        conv_w, conv_b = conv
        in_specs += [pl.BlockSpec((SUBLANES, c), lambda b, j: (jnp.maximum(tile(b, j) * sub - 1, 0), 0)),
                     pl.BlockSpec((SUBLANES, c), lambda b, j: (jnp.minimum((tile(b, j) + 1) * sub, n_sub - 1), 0)),
                     pl.BlockSpec(conv_w.shape, const),
                     pl.BlockSpec((1, c), const)]
        args += [u, u, conv_w, conv_b.reshape(1, c)]
        scratch = [pltpu.VMEM((rows + 2 * SUBLANES, c), F32)]
    in_specs += [pl.BlockSpec(w_ri.shape, lambda b, j: (0, 0, 0)),
                 pl.BlockSpec((1, c), const), pl.BlockSpec((1, c), const), pl.BlockSpec((1, c), const),
                 pl.BlockSpec(h0.shape, const)]
    args += [w_ri, r_b.reshape(1, c), i_b.reshape(1, c), lam.reshape(1, c), h0]
    if emit_y:
        out_specs = [tile_spec]
        out_shape = [jax.ShapeDtypeStruct((n, c), F32)]
    else:
        out_specs = [pl.BlockSpec(h0.shape, const)]
        out_shape = [jax.ShapeDtypeStruct(h0.shape, F32)]
    if conv is not None:
        out_specs.append(tile_spec)
        out_shape.append(jax.ShapeDtypeStruct((n, c), F32))
    name = ("lru_scan" if emit_y else "lru_ctx") + ("_rev" if reverse else "_fwd")
    outs = pl.pallas_call(
        functools.partial(_lru_scan_kernel, n_tiles=n_tiles, reverse=reverse, conv_done=conv is None,
                          reset_first=reset_first, emit_y=emit_y),
        grid=(n_batch, n_tiles),
        in_specs=in_specs,
        out_specs=out_specs,
        out_shape=out_shape,
        scratch_shapes=scratch + [pltpu.VMEM((rows, c), F32),
                                  pltpu.VMEM((rows, c), F32),
                                  pltpu.VMEM((1, c), F32)],
        compiler_params=_params(("arbitrary", "arbitrary")),
        name=name,
    )(*args)
    return outs if conv is not None else outs[0]


def _route(logits, rb):
    e, t = logits.shape
    neg = -jnp.inf
    scores = jax.nn.sigmoid(logits)
    sel = scores + rb
    iota_g = lax.broadcasted_iota(jnp.int32, (N_GROUPS, t), 0).astype(F32)
    iota_e = lax.broadcasted_iota(jnp.int32, (e, t), 0).astype(F32)

    gs = jnp.full((N_GROUPS, t), neg, F32)
    for g in range(N_GROUPS):
        sg = sel[g * EXPERTS_PER_GROUP:(g + 1) * EXPERTS_PER_GROUP, :]
        m1 = jnp.max(sg, axis=0, keepdims=True)
        i1 = jnp.min(jnp.where(sg == m1, iota_g, float(EXPERTS_PER_GROUP)), axis=0, keepdims=True)
        m2 = jnp.max(jnp.where(iota_g == i1, neg, sg), axis=0, keepdims=True)
        gs = jnp.where(iota_g == float(g), m1 + m2, gs)

    keep = jnp.zeros((N_GROUPS, t), F32)
    for _ in range(TOPK_GROUPS):
        m = jnp.max(gs, axis=0, keepdims=True)
        idx = jnp.min(jnp.where(gs == m, iota_g, float(N_GROUPS)), axis=0, keepdims=True)
        hit = iota_g == idx
        keep = jnp.where(hit, 1.0, keep)
        gs = jnp.where(hit, neg, gs)

    masked = jnp.concatenate(
        [jnp.where(keep[g:g + 1, :] > 0.0, sel[g * EXPERTS_PER_GROUP:(g + 1) * EXPERTS_PER_GROUP, :], neg)
         for g in range(N_GROUPS)], axis=0)

    iota_k = lax.broadcasted_iota(jnp.int32, (TOP_K, t), 0)
    selmask = jnp.zeros((e, t), F32)
    eidx = jnp.zeros((TOP_K, t), F32)
    gw = jnp.zeros((TOP_K, t), F32)
    for k in range(TOP_K):
        m = jnp.max(masked, axis=0, keepdims=True)
        idx = jnp.min(jnp.where(masked == m, iota_e, float(e)), axis=0, keepdims=True)
        hit = iota_e == idx
        gk = jnp.sum(jnp.where(hit, scores, 0.0), axis=0, keepdims=True)
        masked = jnp.where(hit, neg, masked)
        selmask = jnp.where(hit, 1.0, selmask)
        eidx = jnp.where(iota_k == k, idx, eidx)
        gw = jnp.where(iota_k == k, gk, gw)
    gw = gw / jnp.sum(gw, axis=0, keepdims=True) * ROUTED_SCALE
    return eidx, gw, selmask


def _ffn_pre(x1, mod_ref, row, g2_ref, rwt_ref, rb_ref, hx2_ref, hi_ref, lo_ref, hit_ref, lot_ref, gatet_ref,
             tab_ref, tabt_ref):
    d = x1.shape[1]
    hx2 = _rms_mod(x1, g2_ref[...], _mod_chunk(mod_ref, row, 4, d), _mod_chunk(mod_ref, row, 3, d))
    hx2_ref[...] = hx2.astype(BF16)
    logits = _dot_nt_3pass(rwt_ref[...], hx2)
    eidx, gw, selmask = _route(logits, rb_ref[...])
    for s in range(selmask.shape[1] // TOKEN_TILE):
        ts = slice(s * TOKEN_TILE, (s + 1) * TOKEN_TILE)
        hi, lo, gate, tab, tabt = _sorted_layout(selmask[:, ts], eidx[:, ts], gw[:, ts])
        hi_ref[:, ts] = hi.astype(BF16)
        lo_ref[:, ts] = lo.astype(BF16)
        hit_ref[ts, :] = hi.T.astype(BF16)
        lot_ref[ts, :] = lo.T.astype(BF16)
        gatet_ref[ts, :] = gate.T.astype(BF16)
        tab_ref[s] = tab
        tabt_ref[s] = tabt


def _ffn_pre_specs(n, d):
    row = lambda i: (i, 0)
    col = lambda i: (0, i)
    sub = MIX_TILE // TOKEN_TILE
    lead = lambda i: (i, 0, 0)
    out_specs = [pl.BlockSpec((MIX_TILE, d), row),
                 pl.BlockSpec((MIX_TILE, d), row),
                 pl.BlockSpec((N_EXPERTS, MIX_TILE), col),
                 pl.BlockSpec((N_EXPERTS, MIX_TILE), col),
                 pl.BlockSpec((MIX_TILE, N_EXPERTS), row),
                 pl.BlockSpec((MIX_TILE, N_EXPERTS), row),
                 pl.BlockSpec((MIX_TILE, N_EXPERTS), row),
                 pl.BlockSpec((sub, N_EXPERTS, LANES), lead),
                 pl.BlockSpec((sub, SUBLANES, LANES), lead)]
    out_shape = [jax.ShapeDtypeStruct((n, d), F32),
                 jax.ShapeDtypeStruct((n, d), BF16),
                 jax.ShapeDtypeStruct((N_EXPERTS, n), BF16),
                 jax.ShapeDtypeStruct((N_EXPERTS, n), BF16),
                 jax.ShapeDtypeStruct((n, N_EXPERTS), BF16),
                 jax.ShapeDtypeStruct((n, N_EXPERTS), BF16),
                 jax.ShapeDtypeStruct((n, N_EXPERTS), BF16),
                 jax.ShapeDtypeStruct((n // TOKEN_TILE, N_EXPERTS, LANES), F32),
                 jax.ShapeDtypeStruct((n // TOKEN_TILE, SUBLANES, LANES), F32)]
    return out_specs, out_shape


def _rglru_out_kernel(yf_ref, yr_ref, gate_ref, x_ref, er_ref, ec_ref, mod_ref, wout_ref, g2_ref, rwt_ref,
                      rb_ref, x1_ref, hx2_ref, hi_ref, lo_ref, hit_ref, lot_ref, gatet_ref, tab_ref, tabt_ref,
                      *, tiles_per_seq):
    d = x_ref.shape[1]
    row = pl.program_id(0) // tiles_per_seq
    yx = yf_ref[...] + yr_ref[...]
    v = gate_ref[...].astype(F32) * yx
    out = jnp.dot(v.astype(BF16), wout_ref[...], preferred_element_type=F32)
    x1 = _with_pos_code(x_ref, er_ref, ec_ref, tiles_per_seq) + _mod_chunk(mod_ref, row, 2, d) * out
    x1_ref[...] = x1
    _ffn_pre(x1, mod_ref, row, g2_ref, rwt_ref, rb_ref, hx2_ref, hi_ref, lo_ref, hit_ref, lot_ref, gatet_ref,
             tab_ref, tabt_ref)


def _rglru_out(y_fwd, y_rev, gate, x, er, ec, mod, w_out, g2, rwt, rb, tiles_per_seq):
    n, d = x.shape
    c = gate.shape[1]
    row = lambda i: (i, 0)
    const = lambda i: (0, 0)
    out_specs, out_shape = _ffn_pre_specs(n, d)
    return pl.pallas_call(
        functools.partial(_rglru_out_kernel, tiles_per_seq=tiles_per_seq),
        grid=(n // MIX_TILE,),
        in_specs=[pl.BlockSpec((MIX_TILE, c), row),
                  pl.BlockSpec((MIX_TILE, c), row),
                  pl.BlockSpec((MIX_TILE, c), row),
                  pl.BlockSpec((MIX_TILE, d), row),
                  pl.BlockSpec(er.shape, const),
                  pl.BlockSpec(ec.shape, const),
                  pl.BlockSpec(mod.shape, const),
                  pl.BlockSpec(w_out.shape, const),
                  pl.BlockSpec((1, d), const),
                  pl.BlockSpec(rwt.shape, const),
                  pl.BlockSpec(rb.shape, const)],
        out_specs=out_specs,
        out_shape=out_shape,
        compiler_params=_params(("arbitrary",)),
        name="rglru_out",
    )(y_fwd, y_rev, gate, x, er, ec, mod, w_out, g2, rwt, rb)


def _sgu_kernel(x_ref, mod_ref, g_ref, win_ref, lng_ref, lnb_ref, ws_ref, bst_ref, wout_ref,
                g2_ref, rwt_ref, rb_ref,
                x1_ref, hx2_ref, hi_ref, lo_ref, hit_ref, lot_ref, gatet_ref, tab_ref, tabt_ref, m_scr, *,
                tiles_per_seq):
    t_rows, d = x_ref.shape
    w = wout_ref.shape[0]
    gd = w // SGU_HEADS
    row = pl.program_id(0) // tiles_per_seq
    x = x_ref[...]
    hx = _rms_mod(x, g_ref[...], _mod_chunk(mod_ref, row, 1, d), _mod_chunk(mod_ref, row, 0, d))
    z = jax.nn.gelu(jnp.dot(hx.astype(BF16), win_ref[...], preferred_element_type=F32))
    u = z[:, :w]
    v = z[:, w:]
    mu = jnp.mean(v, axis=-1, keepdims=True)
    vc = v - mu
    v = vc * lax.rsqrt(jnp.mean(vc * vc, axis=-1, keepdims=True) + NORM_EPS) * lng_ref[...] + lnb_ref[...]
    vb = v.astype(BF16)
    for ch in range(t_rows // CHUNK):
        rs = slice(ch * CHUNK, (ch + 1) * CHUNK)
        for g in range(SGU_HEADS):
            cs = slice(g * gd, (g + 1) * gd)
            sv = jnp.dot(ws_ref[g], vb[rs, cs], preferred_element_type=F32) + bst_ref[:, g:g + 1]
            m_scr[rs, cs] = (u[rs, cs] * sv).astype(BF16)
    out = jnp.dot(m_scr[...], wout_ref[...], preferred_element_type=F32)
    x1 = x + _mod_chunk(mod_ref, row, 2, d) * out
    x1_ref[...] = x1
    _ffn_pre(x1, mod_ref, row, g2_ref, rwt_ref, rb_ref, hx2_ref, hi_ref, lo_ref, hit_ref, lot_ref, gatet_ref,
             tab_ref, tabt_ref)


def _sgu(x, mod, g, w_in, ln_g, ln_b, w_s, b_st, w_out, g2, rwt, rb, tiles_per_seq):
    n, d = x.shape
    w = w_out.shape[0]
    const = lambda i: (0, 0)
    out_specs, out_shape = _ffn_pre_specs(n, d)
    return pl.pallas_call(
        functools.partial(_sgu_kernel, tiles_per_seq=tiles_per_seq),
        grid=(n // MIX_TILE,),
        in_specs=[pl.BlockSpec((MIX_TILE, d), lambda i: (i, 0)),
                  pl.BlockSpec(mod.shape, const),
                  pl.BlockSpec((1, d), const),
                  pl.BlockSpec(w_in.shape, const, pipeline_mode=pl.Buffered(1)),
                  pl.BlockSpec((1, w), const),
                  pl.BlockSpec((1, w), const),
                  pl.BlockSpec(w_s.shape, lambda i: (0, 0, 0)),
                  pl.BlockSpec(b_st.shape, const),
                  pl.BlockSpec(w_out.shape, const, pipeline_mode=pl.Buffered(1)),
                  pl.BlockSpec((1, d), const),
                  pl.BlockSpec(rwt.shape, const),
                  pl.BlockSpec(rb.shape, const)],
        out_specs=out_specs,
        out_shape=out_shape,
        scratch_shapes=[pltpu.VMEM((MIX_TILE, w), BF16)],
        compiler_params=_params(("arbitrary",)),
        name="sgu",
    )(x, mod, g, w_in, ln_g, ln_b, w_s, b_st, w_out, g2, rwt, rb)


def _sorted_layout(m, eidx, gw):
    e, t = m.shape
    lanes = LANES
    r = lax.broadcasted_iota(jnp.int32, (t, t), 0)
    c = lax.broadcasted_iota(jnp.int32, (t, t), 1)
    upper = jnp.where(r <= c, 1.0, 0.0).astype(BF16)
    incl = jnp.dot(m.astype(BF16), upper, preferred_element_type=F32)
    run = _round_up_rows(incl[:, t - 1:t])
    re = lax.broadcasted_iota(jnp.int32, (e, e), 0)
    ce = lax.broadcasted_iota(jnp.int32, (e, e), 1)
    lower = jnp.where(ce < re, 1.0, 0.0).astype(BF16)
    tiles = jnp.broadcast_to(run * (1.0 / ROW_ALIGN), (e, lanes)).astype(BF16)
    lstart = jnp.dot(lower, tiles, preferred_element_type=F32)[:, 0:1] * float(ROW_ALIGN)
    pos = jnp.where(m > 0.0, lstart + incl - m, float(POS_RADIX * POS_RADIX - 1))
    hi = jnp.floor(pos * (1.0 / POS_RADIX))
    lo = pos - hi * float(POS_RADIX)
    iota_e = lax.broadcasted_iota(jnp.int32, (e, t), 0).astype(F32)
    gate = jnp.zeros((e, t), F32)
    for k in range(TOP_K):
        gate = jnp.where(iota_e == eidx[k:k + 1, :], gw[k:k + 1, :], gate)
    lane = lax.broadcasted_iota(jnp.int32, (e, lanes), 1)
    tab = jnp.where(lane == 0, run, jnp.where(lane == 2, lstart, 0.0))
    diag = lax.broadcasted_iota(jnp.int32, (e, lanes), 0) == lane
    row_of = lambda col: jnp.sum(jnp.where(diag, jnp.broadcast_to(col, (e, lanes)), 0.0), axis=0, keepdims=True)
    sub = lax.broadcasted_iota(jnp.int32, (SUBLANES, lanes), 0)
    tabt = jnp.where(sub == 0, row_of(lstart), jnp.where(sub == 1, row_of(lstart + run), 0.0))
    return hi, lo, gate, tab, tabt


def _aligned(v):
    return pl.multiple_of(v, ROW_ALIGN)


def _run_copies(tab_ref, make_copy, unroll, base=0):
    def body(e, carry):
        make_copy(_aligned(tab_ref[base + TAB_LOCAL + e]), _aligned(tab_ref[base + N_EXPERTS + e]),
                  _aligned(tab_ref[base + e])).start()
        return carry

    lax.fori_loop(0, N_EXPERTS, body, 0, unroll=unroll)


def _dispatch_kernel(tab_ref, ztab_ref, tabt_ref, hi_ref, lo_ref, x_ref, xs_hbm, sbuf, zbuf, sems, zsem):
    i = pl.program_id(0)
    tiles = sbuf.shape[0]
    t = x_ref.shape[0] // tiles
    rows = sbuf.shape[1]

    def zero_copy(e):
        n = _aligned(ztab_ref[e])
        return pltpu.make_async_copy(zbuf.at[pl.ds(0, n)], xs_hbm.at[pl.ds(_aligned(ztab_ref[N_EXPERTS + e]), n)], zsem)

    def for_zero_runs(fn):
        def body(e, c):
            @pl.when(ztab_ref[e] > 0)
            def _():
                fn(zero_copy(e))
            return c
        lax.fori_loop(0, N_EXPERTS, body, 0)

        def tail(b, c):
            fn(pltpu.make_async_copy(zbuf, xs_hbm.at[pl.ds(pl.multiple_of(b * EXPERT_BLOCK, EXPERT_BLOCK),
                                                           EXPERT_BLOCK)], zsem))
            return c
        lax.fori_loop(ztab_ref[2 * N_EXPERTS], xs_hbm.shape[0] // EXPERT_BLOCK, tail, 0)

    @pl.when(i == 0)
    def _():
        zbuf[...] = jnp.zeros_like(zbuf)
        for_zero_runs(lambda cp: cp.start())

    def wait_rows(s, n):
        pltpu.make_async_copy(sbuf.at[s, pl.ds(0, n)], xs_hbm.at[pl.ds(0, n)], sems.at[s]).wait()

    def permute(s, r0):
        ts = slice(s * t, (s + 1) * t)
        j_e = (lax.broadcasted_iota(jnp.int32, (PERM_CHUNK, N_EXPERTS), 0) + r0).astype(F32)
        owner = jnp.where((j_e >= tabt_ref[s, 0:1, 0:N_EXPERTS]) & (j_e < tabt_ref[s, 1:2, 0:N_EXPERTS]),
                          1.0, 0.0).astype(BF16)
        pos = (jnp.dot(owner, hi_ref[:, ts], preferred_element_type=F32) * float(POS_RADIX)
               + jnp.dot(owner, lo_ref[:, ts], preferred_element_type=F32))
        j_t = (lax.broadcasted_iota(jnp.int32, (PERM_CHUNK, t), 0) + r0).astype(F32)
        p = jnp.where(pos == j_t, 1.0, 0.0).astype(BF16)
        sbuf[s, r0:r0 + PERM_CHUNK, :] = jnp.dot(p, x_ref[ts, :], preferred_element_type=F32).astype(BF16)

    for s in range(tiles):
        base = s * TAB_WIDTH

        @pl.when(i > 0)
        def _():
            wait_rows(s, _aligned(tab_ref[base + 2 * N_EXPERTS + 1]))

        for r0 in range(0, rows, PERM_CHUNK):
            if r0 < t * TOP_K + PERM_CHUNK:
                permute(s, r0)
            else:
                pl.when(tab_ref[base + 2 * N_EXPERTS] > r0)(functools.partial(permute, s, r0))

        _run_copies(tab_ref, lambda loc, glob, n: pltpu.make_async_copy(
            sbuf.at[s, pl.ds(loc, n)], xs_hbm.at[pl.ds(glob, n)], sems.at[s]), unroll=True, base=base)

    @pl.when(i == pl.num_programs(0) - 1)
    def _():
        for s in range(tiles):
            wait_rows(s, _aligned(tab_ref[s * TAB_WIDTH + 2 * N_EXPERTS]))

    @pl.when(i == 0)
    def _():
        for_zero_runs(lambda cp: cp.wait())


def _dispatch(tab, ztab, tab_t, pos_hi, pos_lo, hx2, xs_rows):
    n, d = hx2.shape
    col = lambda i: (0, i)
    return pl.pallas_call(
        _dispatch_kernel,
        grid=(n // (DISPATCH_TILES * TOKEN_TILE),),
        in_specs=[pl.BlockSpec((DISPATCH_TILES * TAB_WIDTH,), lambda i: (i,), memory_space=pltpu.SMEM),
                  pl.BlockSpec(memory_space=pltpu.SMEM),
                  pl.BlockSpec((DISPATCH_TILES, SUBLANES, LANES), lambda i: (i, 0, 0)),
                  pl.BlockSpec((N_EXPERTS, DISPATCH_TILES * TOKEN_TILE), col),
                  pl.BlockSpec((N_EXPERTS, DISPATCH_TILES * TOKEN_TILE), col),
                  pl.BlockSpec((DISPATCH_TILES * TOKEN_TILE, d), lambda i: (i, 0))],
        out_specs=pl.BlockSpec(memory_space=pl.ANY),
        out_shape=jax.ShapeDtypeStruct((xs_rows, d), BF16),
        scratch_shapes=[pltpu.VMEM((DISPATCH_TILES, SORTED_ROWS, d), BF16),
                        pltpu.VMEM((EXPERT_BLOCK, d), BF16),
                        pltpu.SemaphoreType.DMA((DISPATCH_TILES,)),
                        pltpu.SemaphoreType.DMA],
        compiler_params=_params(("arbitrary",)),
        name="moe_dispatch",
    )(tab, ztab, tab_t, pos_hi, pos_lo, hx2)


def _experts_kernel(be_ref, nb_ref, xs_ref, wgu_ref, wd_ref, ys_ref, wgu_b, wd_b):
    i = pl.program_id(0)
    ff = wd_b.shape[0]
    used = i < nb_ref[0]
    new_expert = (i == 0) | (be_ref[i] != be_ref[jnp.maximum(i - 1, 0)])

    @pl.when(used & new_expert)
    def _():
        wgu_b[...] = wgu_ref[0, 0].astype(BF16)
        wd_b[...] = wd_ref[0, 0].astype(BF16)

    @pl.when(used)
    def _():
        h = jnp.dot(xs_ref[...], wgu_b[...], preferred_element_type=F32)
        a = _silu(h[:, :ff]) * h[:, ff:]
        ys_ref[...] = jnp.dot(a.astype(BF16), wd_b[...], preferred_element_type=F32).astype(BF16)

    @pl.when(jnp.logical_not(used))
    def _():
        ys_ref[...] = jnp.zeros_like(ys_ref)


def _experts(block_e, n_used, xs, w_gu, w_down, layer):
    rows, d = xs.shape
    nb = rows // EXPERT_BLOCK
    blk = lambda i, be, nu: (jnp.maximum(jnp.minimum(i, nu[0] - 1), 0), 0)
    wmap = lambda i, be, nu: (layer, be[i], 0, 0)
    return pl.pallas_call(
        _experts_kernel,
        grid_spec=pltpu.PrefetchScalarGridSpec(
            num_scalar_prefetch=2,
            grid=(nb,),
            in_specs=[pl.BlockSpec((EXPERT_BLOCK, d), blk),
                      pl.BlockSpec((1, 1) + w_gu.shape[2:], wmap),
                      pl.BlockSpec((1, 1) + w_down.shape[2:], wmap)],
            out_specs=pl.BlockSpec((EXPERT_BLOCK, d), lambda i, be, nu: (i, 0)),
            scratch_shapes=[pltpu.VMEM(w_gu.shape[2:], BF16), pltpu.VMEM(w_down.shape[2:], BF16)]),
        out_shape=jax.ShapeDtypeStruct((rows, d), BF16),
        compiler_params=_params(("arbitrary",)),
        name="moe_experts",
    )(block_e, n_used, xs, w_gu, w_down)


def _combine_kernel(tab_ref, tabn_ref, tabv_ref, hi_ref, lo_ref, gate_ref, ys_hbm, hx2_ref, x1_ref, mod_ref,
                    wsgu_ref, wsd_ref, fg_ref, out_ref, ybuf, sems, *, tiles_per_seq, final_norm):
    t_rows, d = x1_ref.shape
    ff = wsd_ref.shape[0]
    rows = ybuf.shape[1]
    i = pl.program_id(0)
    last = pl.num_programs(0) - 1
    slot = i % 2
    row = i // tiles_per_seq

    def gather(table, s, unroll):
        _run_copies(table, lambda loc, glob, n: pltpu.make_async_copy(
            ys_hbm.at[pl.ds(glob, n)], ybuf.at[s, pl.ds(loc, n)], sems.at[s]), unroll=unroll)

    def wait_rows(table, s):
        n = _aligned(table[2 * N_EXPERTS])
        pltpu.make_async_copy(ys_hbm.at[pl.ds(0, n)], ybuf.at[s, pl.ds(0, n)], sems.at[s]).wait()

    @pl.when(i == 0)
    def _():
        ybuf[...] = jnp.zeros_like(ybuf)
        gather(tab_ref, 0, False)

    gather(tabn_ref, 1 - slot, True)

    hs = jnp.dot(hx2_ref[...], wsgu_ref[...], preferred_element_type=F32)
    shared = jnp.dot((_silu(hs[:, :ff]) * hs[:, ff:]).astype(BF16), wsd_ref[...], preferred_element_type=F32)

    run_lo = tabv_ref[0, :, 2:3]
    run_hi = run_lo + tabv_ref[0, :, 0:1]
    j_e = lax.broadcasted_iota(jnp.int32, (N_EXPERTS, rows), 1).astype(F32)
    owner = jnp.where((j_e >= run_lo) & (j_e < run_hi), 1.0, 0.0).astype(BF16)
    pos = (jnp.dot(hi_ref[...], owner, preferred_element_type=F32) * float(POS_RADIX)
           + jnp.dot(lo_ref[...], owner, preferred_element_type=F32))
    gates = jnp.dot(gate_ref[...], owner, preferred_element_type=F32)
    j_t = lax.broadcasted_iota(jnp.int32, (t_rows, rows), 1).astype(F32)
    gb = jnp.where(pos == j_t, gates, 0.0).astype(BF16)

    wait_rows(tab_ref, slot)
    routed = jnp.dot(gb, ybuf[slot], preferred_element_type=F32)
    x2 = x1_ref[...] + _mod_chunk(mod_ref, row, 5, d) * (routed + shared)
    if final_norm:
        x2 = x2 * lax.rsqrt(jnp.mean(x2 * x2, axis=-1, keepdims=True) + NORM_EPS) * fg_ref[...]
    out_ref[...] = x2

    @pl.when(i == last)
    def _():
        wait_rows(tabn_ref, 1 - slot)


def _combine(tab, tab_v, pos_hi, pos_lo, gate, ys, hx2, x1, mod, ws_gu, ws_down, fg, tiles_per_seq, final_norm):
    n, d = x1.shape
    row = lambda i: (i, 0)
    const = lambda i: (0, 0)
    return pl.pallas_call(
        functools.partial(_combine_kernel, tiles_per_seq=tiles_per_seq, final_norm=final_norm),
        grid=(n // TOKEN_TILE,),
        in_specs=[pl.BlockSpec((TAB_WIDTH,), lambda i: (i,), memory_space=pltpu.SMEM),
                  pl.BlockSpec((TAB_WIDTH,), lambda i: (jnp.minimum(i + 1, n // TOKEN_TILE - 1),),
                               memory_space=pltpu.SMEM),
                  pl.BlockSpec((1, N_EXPERTS, LANES), lambda i: (i, 0, 0)),
                  pl.BlockSpec((TOKEN_TILE, N_EXPERTS), row),
                  pl.BlockSpec((TOKEN_TILE, N_EXPERTS), row),
                  pl.BlockSpec((TOKEN_TILE, N_EXPERTS), row),
                  pl.BlockSpec(memory_space=pl.ANY),
                  pl.BlockSpec((TOKEN_TILE, d), row),
                  pl.BlockSpec((TOKEN_TILE, d), row),
                  pl.BlockSpec(mod.shape, const),
                  pl.BlockSpec(ws_gu.shape, const),
                  pl.BlockSpec(ws_down.shape, const),
                  pl.BlockSpec((1, d), const)],
        out_specs=pl.BlockSpec((TOKEN_TILE, d), row),
        out_shape=jax.ShapeDtypeStruct((n, d), F32),
        scratch_shapes=[pltpu.VMEM((2, SORTED_ROWS, d), BF16), pltpu.SemaphoreType.DMA((2,))],
        compiler_params=_params(("arbitrary",)),
        name="moe_combine",
    )(tab, tab, tab_v, pos_hi, pos_lo, gate, ys, hx2, x1, mod, ws_gu, ws_down, fg)


def _moe(x1, hx2, pos_hi, pos_lo, pos_hi_t, pos_lo_t, gate_t, tab_f, tab_t, mod, w_gu, w_down, layer, ws_gu,
         ws_down, fg, tiles_per_seq, final_norm):
    n, d = x1.shape
    n_tiles = n // TOKEN_TILE
    run_len = tab_f[:, :, 0].astype(jnp.int32)
    counts = jnp.sum(run_len, axis=0)
    padded = (counts + EXPERT_BLOCK - 1) // EXPERT_BLOCK * EXPERT_BLOCK
    pad_end = jnp.cumsum(padded)
    pad_start = pad_end - padded
    max_rows = n * TOP_K + n_tiles * N_EXPERTS * ROW_ALIGN + N_EXPERTS * (EXPERT_BLOCK - ROW_ALIGN)
    n_blocks = (max_rows + EXPERT_BLOCK - 1) // EXPERT_BLOCK
    n_used = (pad_end[-1:] // EXPERT_BLOCK).astype(jnp.int32)
    block_start = jnp.arange(n_blocks, dtype=jnp.int32) * EXPERT_BLOCK
    block_e = jnp.minimum(jnp.sum((pad_end[None, :] <= block_start[:, None]).astype(jnp.int32), axis=1),
                          N_EXPERTS - 1)

    run_start = pad_start[None, :] + jnp.cumsum(run_len, axis=0) - run_len
    total = jnp.sum(run_len, axis=1, keepdims=True)
    prev_total = jnp.concatenate([jnp.zeros((DISPATCH_TILES, 1), jnp.int32), total[:-DISPATCH_TILES]], axis=0)
    run_local = tab_f[:, :, 2].astype(jnp.int32)
    fill = jnp.zeros((n_tiles, TAB_WIDTH - 3 * N_EXPERTS - 2), jnp.int32)
    tab = jnp.concatenate([run_len, run_start, total, prev_total, run_local, fill], axis=1).reshape(-1)
    ztab = jnp.concatenate([padded - counts, pad_start + counts, n_used])

    xs = _dispatch(tab, ztab, tab_t, pos_hi, pos_lo, hx2, n_blocks * EXPERT_BLOCK)
    ys = _experts(block_e, n_used, xs, w_gu, w_down, layer)
    return _combine(tab, tab_f, pos_hi_t, pos_lo_t, gate_t, ys, hx2, x1, mod, ws_gu, ws_down, fg,
                    tiles_per_seq, final_norm)


def _sincos_tables(rows, d):
    quarter = d // 4
    omega = 1.0 / (POS_BASE ** (jnp.arange(quarter, dtype=F32) / quarter))

    def emb(n):
        p = jnp.arange(n, dtype=F32)[:, None] * omega[None, :]
        return jnp.concatenate([jnp.sin(p), jnp.cos(p)], axis=-1)

    return emb(rows), emb(GRID_W)


def kernel(x, c, ctx, c_ctx, ada_w, ada_b, mix_norm_g, ffn_norm_g, a_w_in, a_conv_w, a_conv_b, a_gate_r_w, a_gate_r_b, a_gate_i_w, a_gate_i_b, a_lambda, a_w_out, b_w_in, b_ln_g, b_ln_b, b_w_s, b_b_s, b_w_out, router_w, router_b, moe_w_gu, moe_w_down, shared_w_gu, shared_w_down, final_norm_g):
    bsz, s, d = x.shape
    ctx_len = ctx.shape[1]
    depth = ada_w.shape[0]
    assert depth == 2 and bsz < MOD_ROWS and s % MIX_TILE == 0 and MIX_TILE % TOKEN_TILE == 0
    assert s % SCAN_TILE == 0 and MIX_TILE % GRID_W == 0
    assert ctx_len % TOKEN_TILE == 0
    n = bsz * s
    tps = s // TOKEN_TILE
    ctx_row = bsz

    cc = jnp.zeros((MOD_ROWS, d), F32).at[:bsz].set(c).at[ctx_row].set(c_ctx)
    mod = _modulation(cc, ada_w, ada_b)
    er, ec = _sincos_tables(s // GRID_W, d)
    rc = a_w_in.shape[2] // 2

    w_in0 = a_w_in[0].astype(BF16)
    g_mix0 = mix_norm_g[0].reshape(1, d)
    x2 = x.reshape(n, d)
    gate, ux = _rglru_in(x2, er, ec, mod[0], g_mix0, w_in0)
    uc = _ctx_in(ctx.reshape(bsz * ctx_len, d), mod[0], g_mix0, w_in0[:, rc:], ctx_row)
    w_ri = jnp.concatenate([a_gate_r_w[0], a_gate_i_w[0]], axis=-1).astype(BF16)
    conv = (a_conv_w[0], a_conv_b[0])
    gates = [(w_ri[k], a_gate_r_b[0, k], a_gate_i_b[0, k], a_lambda[0, k]) for k in range(2)]
    h_zero = jnp.zeros((bsz, rc), F32)
    h_fwd, uc = _lru_scan(uc, conv, *gates[0], h_zero, reverse=False, reset_first=True, emit_y=False)
    h_rev = _lru_scan(uc, None, *gates[1], h_zero, reverse=True, reset_first=True, emit_y=False)
    y_fwd, ux = _lru_scan(ux, conv, *gates[0], h_fwd, reverse=False, reset_first=False, emit_y=True)
    y_rev = _lru_scan(ux, None, *gates[1], h_rev, reverse=True, reset_first=False, emit_y=True)
    pre = _rglru_out(y_fwd, y_rev, gate, x2, er, ec, mod[0], a_w_out[0].astype(BF16), ffn_norm_g[0].reshape(1, d),
                     router_w[0].T, router_b[0].reshape(N_EXPERTS, 1), s // MIX_TILE)
    x1 = _moe(*pre, mod[0], moe_w_gu, moe_w_down, 0, shared_w_gu[0].astype(BF16),
              shared_w_down[0].astype(BF16), final_norm_g.reshape(1, d), tps, False)

    pre = _sgu(x1, mod[1], mix_norm_g[1].reshape(1, d), b_w_in[0].astype(BF16),
               b_ln_g[0].reshape(1, -1), b_ln_b[0].reshape(1, -1), b_w_s[0].astype(BF16), b_b_s[0].T,
               b_w_out[0].astype(BF16), ffn_norm_g[1].reshape(1, d),
               router_w[1].T, router_b[1].reshape(N_EXPERTS, 1), s // MIX_TILE)
    out = _moe(*pre, mod[1], moe_w_gu, moe_w_down, 1, shared_w_gu[1].astype(BF16),
               shared_w_down[1].astype(BF16), final_norm_g.reshape(1, d), tps, True)
    return out.reshape(bsz, s, d)
```

```python
import functools

import jax
import jax.numpy as jnp
from jax import lax
from jax.experimental import pallas as pl
from jax.experimental.pallas import tpu as pltpu

F32 = jnp.float32
BF16 = jnp.bfloat16
HIGHEST = lax.Precision.HIGHEST

GRID_W = 64
NORM_EPS = 1e-6
POS_BASE = 10000.0
RNN_HEADS = 5
CONV_WIDTH = 4
CONV_PAD_LEFT = 2
LRU_C = 8.0
SGU_HEADS = 8
CHUNK = 128
N_EXPERTS = 64
TOP_K = 8
N_GROUPS = 8
TOPK_GROUPS = 4
EXPERTS_PER_GROUP = N_EXPERTS // N_GROUPS
ROUTED_SCALE = 2.5

SUBLANES = 8
ROW_ALIGN = 16
LANES = 128
MOD_ROWS = 8
TOKEN_TILE = 256
MIX_TILE = 512
SCAN_TILE = 512
EXPERT_BLOCK = 1024
SORTED_ROWS = TOKEN_TILE * TOP_K + N_EXPERTS * ROW_ALIGN
PERM_CHUNK = 512
DISPATCH_TILES = 2
TAB_WIDTH = 256
TAB_LOCAL = 2 * N_EXPERTS + 2
POS_RADIX = 64
VMEM_LIMIT = 56 * 1024 * 1024


def _params(semantics, vmem=VMEM_LIMIT):
    return pltpu.CompilerParams(dimension_semantics=semantics, vmem_limit_bytes=vmem)


def _silu(x):
    return x * jax.nn.sigmoid(x)


def _rms_mod(x, g, sc, sh):
    y = x * lax.rsqrt(jnp.mean(x * x, axis=-1, keepdims=True) + NORM_EPS)
    return (y * g) * (1.0 + sc) + sh


def _mod_chunk(mod_ref, row, k, d):
    return mod_ref[pl.ds(row, 1), k * d:(k + 1) * d]


def _dot_nt_3pass(a, b):
    dims = (((1,), (1,)), ((), ()))
    a_hi, b_hi = a.astype(BF16), b.astype(BF16)
    a_lo = (a - a_hi.astype(F32)).astype(BF16)
    b_lo = (b - b_hi.astype(F32)).astype(BF16)
    dot = functools.partial(lax.dot_general, dimension_numbers=dims, preferred_element_type=F32)
    return dot(a_hi, b_hi) + dot(a_hi, b_lo) + dot(a_lo, b_hi)


def _round_up_rows(count):
    return jnp.maximum(jnp.ceil(count * (1.0 / ROW_ALIGN)), 1.0) * float(ROW_ALIGN)


def _mod_kernel(cc_ref, w_ref, b_ref, o_ref):
    s = _silu(cc_ref[...])
    o_ref[0] = jnp.dot(s, w_ref[0], preferred_element_type=F32, precision=HIGHEST) + b_ref[0]


def _modulation(cc, ada_w, ada_b):
    depth, d, nd = ada_w.shape
    return pl.pallas_call(
        _mod_kernel,
        grid=(depth, nd // d),
        in_specs=[pl.BlockSpec((MOD_ROWS, d), lambda l, j: (0, 0)),
                  pl.BlockSpec((1, d, d), lambda l, j: (l, 0, j)),
                  pl.BlockSpec((1, 1, d), lambda l, j: (l, 0, j))],
        out_specs=pl.BlockSpec((1, MOD_ROWS, d), lambda l, j: (l, 0, j)),
        out_shape=jax.ShapeDtypeStruct((depth, MOD_ROWS, nd), F32),
        compiler_params=_params(("arbitrary", "arbitrary")),
        name="modulation",
    )(cc, ada_w, ada_b.reshape(depth, 1, nd))


def _with_pos_code(x_ref, er_ref, ec_ref, tiles_per_seq):
    t_rows = x_ref.shape[0]
    grid_w, half = ec_ref.shape
    rows = t_rows // grid_w
    r0 = pl.multiple_of((pl.program_id(0) % tiles_per_seq) * rows, rows)
    er = er_ref[pl.ds(r0, rows), :]
    pe = jnp.concatenate([jnp.broadcast_to(er[:, None, :], (rows, grid_w, half)).reshape(t_rows, half),
                          jnp.concatenate([ec_ref[...]] * rows, axis=0)], axis=1)
    return x_ref[...] + pe


def _rglru_in_kernel(x_ref, er_ref, ec_ref, mod_ref, g_ref, w_ref, gate_ref, u_ref, *, tiles_per_seq):
    d = x_ref.shape[1]
    c = u_ref.shape[1]
    row = pl.program_id(0) // tiles_per_seq
    x = _with_pos_code(x_ref, er_ref, ec_ref, tiles_per_seq)
    hx = _rms_mod(x, g_ref[...], _mod_chunk(mod_ref, row, 1, d), _mod_chunk(mod_ref, row, 0, d))
    z = jnp.dot(hx.astype(BF16), w_ref[...], preferred_element_type=F32)
    gate_ref[...] = jax.nn.gelu(z[:, :c]).astype(BF16)
    u_ref[...] = z[:, c:]


def _rglru_in(x2, er, ec, mod, g, w_in):
    n, d = x2.shape
    c = w_in.shape[1] // 2
    tps = er.shape[0] * ec.shape[0] // MIX_TILE
    row = lambda i: (i, 0)
    return pl.pallas_call(
        functools.partial(_rglru_in_kernel, tiles_per_seq=tps),
        grid=(n // MIX_TILE,),
        in_specs=[pl.BlockSpec((MIX_TILE, d), row),
                  pl.BlockSpec(er.shape, lambda i: (0, 0)),
                  pl.BlockSpec(ec.shape, lambda i: (0, 0)),
                  pl.BlockSpec(mod.shape, lambda i: (0, 0)),
                  pl.BlockSpec((1, d), lambda i: (0, 0)),
                  pl.BlockSpec(w_in.shape, lambda i: (0, 0))],
        out_specs=[pl.BlockSpec((MIX_TILE, c), row),
                   pl.BlockSpec((MIX_TILE, c), row)],
        out_shape=[jax.ShapeDtypeStruct((n, c), BF16),
                   jax.ShapeDtypeStruct((n, c), F32)],
        compiler_params=_params(("arbitrary",)),
        name="rglru_in",
    )(x2, er, ec, mod, g, w_in)


def _ctx_in_kernel(x_ref, mod_ref, g_ref, w_ref, u_ref, *, ctx_row):
    d = x_ref.shape[1]
    hx = _rms_mod(x_ref[...], g_ref[...], _mod_chunk(mod_ref, ctx_row, 1, d),
                  _mod_chunk(mod_ref, ctx_row, 0, d))
    u_ref[...] = jnp.dot(hx.astype(BF16), w_ref[...], preferred_element_type=F32)


def _ctx_in(c2, mod, g, w_u, ctx_row):
    n, d = c2.shape
    c = w_u.shape[1]
    return pl.pallas_call(
        functools.partial(_ctx_in_kernel, ctx_row=ctx_row),
        grid=(n // TOKEN_TILE,),
        in_specs=[pl.BlockSpec((TOKEN_TILE, d), lambda i: (i, 0)),
                  pl.BlockSpec(mod.shape, lambda i: (0, 0)),
                  pl.BlockSpec((1, d), lambda i: (0, 0)),
                  pl.BlockSpec(w_u.shape, lambda i: (0, 0))],
        out_specs=pl.BlockSpec((TOKEN_TILE, c), lambda i: (i, 0)),
        out_shape=jax.ShapeDtypeStruct((n, c), F32),
        compiler_params=_params(("arbitrary",)),
        name="ctx_in",
    )(c2, mod, g, w_u)


def _log_sigmoid(x):
    return jnp.minimum(x, 0.0) - jnp.log1p(jnp.exp(-jnp.abs(x)))


def _lru_scan_kernel(*refs, n_tiles, reverse, conv_done, reset_first, emit_y):
    if conv_done:
        u_ref, wri_ref, rb_ref, ib_ref, lam_ref, h0_ref, out_ref, a_scr, b_scr, h_scr = refs
    else:
        (u_ref, up_ref, un_ref, cw_ref, cb_ref, wri_ref, rb_ref, ib_ref, lam_ref, h0_ref,
         out_ref, uc_ref, ubuf, a_scr, b_scr, h_scr) = refs
    t_rows, c = u_ref.shape
    hb = c // RNN_HEADS
    b = pl.program_id(0)
    j = pl.program_id(1)
    jj = n_tiles - 1 - j if reverse else j

    if conv_done:
        u = u_ref[...]
    else:
        ubuf[SUBLANES:SUBLANES + t_rows, :] = u_ref[...]
        ubuf[0:SUBLANES, :] = jnp.where(jj == 0, 0.0, up_ref[...])
        ubuf[SUBLANES + t_rows:, :] = jnp.where(jj == n_tiles - 1, 0.0, un_ref[...])
        u = cb_ref[...]
        full = ubuf[...]
        for k in range(CONV_WIDTH):
            shift = (CONV_PAD_LEFT - k) % full.shape[0]
            tap = full if shift == 0 else pltpu.roll(full, shift, 0)
            u = u + cw_ref[k:k + 1, :] * tap[SUBLANES:SUBLANES + t_rows, :]
        uc_ref[...] = u

    log_lam = LRU_C * _log_sigmoid(lam_ref[...])
    rows = lax.broadcasted_iota(jnp.int32, (t_rows, 1), 0)
    first_row = jnp.where(j == 0, t_rows - 1 if reverse else 0, -1)
    for h in range(RNN_HEADS):
        sl = slice(h * hb, (h + 1) * hb)
        uh = u[:, sl]
        z = jnp.dot(uh.astype(BF16), wri_ref[h], preferred_element_type=F32)
        r = jax.nn.sigmoid(z[:, :hb] + rb_ref[:, sl])
        ig = jax.nn.sigmoid(z[:, hb:] + ib_ref[:, sl])
        log_a = r * log_lam[:, sl]
        a = jnp.exp(log_a)
        mult = jnp.sqrt((1.0 - a) * (1.0 + a))
        if reset_first:
            mult = jnp.where(rows == first_row, 1.0, mult)
        a_scr[:, sl] = a
        b_scr[:, sl] = mult * ig * uh

    @pl.when(j == 0)
    def _():
        h_scr[...] = h0_ref[pl.ds(b, 1), :]

    n_groups = t_rows // SUBLANES

    def group(g, h):
        base = pl.multiple_of((n_groups - 1 - g if reverse else g) * SUBLANES, SUBLANES)
        for s in range(SUBLANES):
            r = base + (SUBLANES - 1 - s if reverse else s)
            h = a_scr[pl.ds(r, 1), :] * h + b_scr[pl.ds(r, 1), :]
            if emit_y:
                out_ref[pl.ds(r, 1), :] = h
        return h

    h = lax.fori_loop(0, n_groups, group, h_scr[...])
    h_scr[...] = h
    if not emit_y:
        @pl.when(j == n_tiles - 1)
        def _():
            out_ref[pl.ds(b, 1), :] = h


def _lru_scan(u, conv, w_ri, r_b, i_b, lam, h0, *, reverse, reset_first, emit_y):
    n, c = u.shape
    n_batch = h0.shape[0]
    rows = min(SCAN_TILE, n // n_batch)
    n_tiles = n // n_batch // rows
    sub = rows // SUBLANES
    n_sub = n // SUBLANES
    const = lambda b, j: (0, 0)

    def tile(b, j):
        return b * n_tiles + (n_tiles - 1 - j if reverse else j)

    tile_spec = pl.BlockSpec((rows, c), lambda b, j: (tile(b, j), 0))
    in_specs = [tile_spec]
    args = [u]
    scratch = []
    if conv is not None:
        conv_w, conv_b = conv
        in_specs += [pl.BlockSpec((SUBLANES, c), lambda b, j: (jnp.maximum(tile(b, j) * sub - 1, 0), 0)),
                     pl.BlockSpec((SUBLANES, c), lambda b, j: (jnp.minimum((tile(b, j) + 1) * sub, n_sub - 1), 0)),
                     pl.BlockSpec(conv_w.shape, const),
                     pl.BlockSpec((1, c), const)]
        args += [u, u, conv_w, conv_b.reshape(1, c)]
        scratch = [pltpu.VMEM((rows + 2 * SUBLANES, c), F32)]
    in_specs += [pl.BlockSpec(w_ri.shape, lambda b, j: (0, 0, 0)),
                 pl.BlockSpec((1, c), const), pl.BlockSpec((1, c), const), pl.BlockSpec((1, c), const),
                 pl.BlockSpec(h0.shape, const)]
    args += [w_ri, r_b.reshape(1, c), i_b.reshape(1, c), lam.reshape(1, c), h0]
    if emit_y:
        out_specs = [tile_spec]
        out_shape = [jax.ShapeDtypeStruct((n, c), F32)]
    else:
        out_specs = [pl.BlockSpec(h0.shape, const)]
        out_shape = [jax.ShapeDtypeStruct(h0.shape, F32)]
    if conv is not None:
        out_specs.append(tile_spec)
        out_shape.append(jax.ShapeDtypeStruct((n, c), F32))
    name = ("lru_scan" if emit_y else "lru_ctx") + ("_rev" if reverse else "_fwd")
    outs = pl.pallas_call(
        functools.partial(_lru_scan_kernel, n_tiles=n_tiles, reverse=reverse, conv_done=conv is None,
                          reset_first=reset_first, emit_y=emit_y),
        grid=(n_batch, n_tiles),
        in_specs=in_specs,
        out_specs=out_specs,
        out_shape=out_shape,
        scratch_shapes=scratch + [pltpu.VMEM((rows, c), F32),
                                  pltpu.VMEM((rows, c), F32),
                                  pltpu.VMEM((1, c), F32)],
        compiler_params=_params(("arbitrary", "arbitrary")),
        name=name,
    )(*args)
    return outs if conv is not None else outs[0]


def _route(logits, rb):
    e, t = logits.shape
    neg = -jnp.inf
    scores = jax.nn.sigmoid(logits)
    sel = scores + rb
    iota_g = lax.broadcasted_iota(jnp.int32, (N_GROUPS, t), 0).astype(F32)
    iota_e = lax.broadcasted_iota(jnp.int32, (e, t), 0).astype(F32)

    gs = jnp.full((N_GROUPS, t), neg, F32)
    for g in range(N_GROUPS):
        sg = sel[g * EXPERTS_PER_GROUP:(g + 1) * EXPERTS_PER_GROUP, :]
        m1 = jnp.max(sg, axis=0, keepdims=True)
        i1 = jnp.min(jnp.where(sg == m1, iota_g, float(EXPERTS_PER_GROUP)), axis=0, keepdims=True)
        m2 = jnp.max(jnp.where(iota_g == i1, neg, sg), axis=0, keepdims=True)
        gs = jnp.where(iota_g == float(g), m1 + m2, gs)

    keep = jnp.zeros((N_GROUPS, t), F32)
    for _ in range(TOPK_GROUPS):
        m = jnp.max(gs, axis=0, keepdims=True)
        idx = jnp.min(jnp.where(gs == m, iota_g, float(N_GROUPS)), axis=0, keepdims=True)
        hit = iota_g == idx
        keep = jnp.where(hit, 1.0, keep)
        gs = jnp.where(hit, neg, gs)

    masked = jnp.concatenate(
        [jnp.where(keep[g:g + 1, :] > 0.0, sel[g * EXPERTS_PER_GROUP:(g + 1) * EXPERTS_PER_GROUP, :], neg)
         for g in range(N_GROUPS)], axis=0)

    iota_k = lax.broadcasted_iota(jnp.int32, (TOP_K, t), 0)
    selmask = jnp.zeros((e, t), F32)
    eidx = jnp.zeros((TOP_K, t), F32)
    gw = jnp.zeros((TOP_K, t), F32)
    for k in range(TOP_K):
        m = jnp.max(masked, axis=0, keepdims=True)
        idx = jnp.min(jnp.where(masked == m, iota_e, float(e)), axis=0, keepdims=True)
        hit = iota_e == idx
        gk = jnp.sum(jnp.where(hit, scores, 0.0), axis=0, keepdims=True)
        masked = jnp.where(hit, neg, masked)
        selmask = jnp.where(hit, 1.0, selmask)
        eidx = jnp.where(iota_k == k, idx, eidx)
        gw = jnp.where(iota_k == k, gk, gw)
    gw = gw / jnp.sum(gw, axis=0, keepdims=True) * ROUTED_SCALE
    return eidx, gw, selmask


def _ffn_pre(x1, mod_ref, row, g2_ref, rwt_ref, rb_ref, hx2_ref, hi_ref, lo_ref, hit_ref, lot_ref, gatet_ref,
             tab_ref, tabt_ref):
    d = x1.shape[1]
    hx2 = _rms_mod(x1, g2_ref[...], _mod_chunk(mod_ref, row, 4, d), _mod_chunk(mod_ref, row, 3, d))
    hx2_ref[...] = hx2.astype(BF16)
    logits = _dot_nt_3pass(rwt_ref[...], hx2)
    eidx, gw, selmask = _route(logits, rb_ref[...])
    for s in range(selmask.shape[1] // TOKEN_TILE):
        ts = slice(s * TOKEN_TILE, (s + 1) * TOKEN_TILE)
        hi, lo, gate, tab, tabt = _sorted_layout(selmask[:, ts], eidx[:, ts], gw[:, ts])
        hi_ref[:, ts] = hi.astype(BF16)
        lo_ref[:, ts] = lo.astype(BF16)
        hit_ref[ts, :] = hi.T.astype(BF16)
        lot_ref[ts, :] = lo.T.astype(BF16)
        gatet_ref[ts, :] = gate.T.astype(BF16)
        tab_ref[s] = tab
        tabt_ref[s] = tabt


def _ffn_pre_specs(n, d):
    row = lambda i: (i, 0)
    col = lambda i: (0, i)
    sub = MIX_TILE // TOKEN_TILE
    lead = lambda i: (i, 0, 0)
    out_specs = [pl.BlockSpec((MIX_TILE, d), row),
                 pl.BlockSpec((MIX_TILE, d), row),
                 pl.BlockSpec((N_EXPERTS, MIX_TILE), col),
                 pl.BlockSpec((N_EXPERTS, MIX_TILE), col),
                 pl.BlockSpec((MIX_TILE, N_EXPERTS), row),
                 pl.BlockSpec((MIX_TILE, N_EXPERTS), row),
                 pl.BlockSpec((MIX_TILE, N_EXPERTS), row),
                 pl.BlockSpec((sub, N_EXPERTS, LANES), lead),
                 pl.BlockSpec((sub, SUBLANES, LANES), lead)]
    out_shape = [jax.ShapeDtypeStruct((n, d), F32),
                 jax.ShapeDtypeStruct((n, d), BF16),
                 jax.ShapeDtypeStruct((N_EXPERTS, n), BF16),
                 jax.ShapeDtypeStruct((N_EXPERTS, n), BF16),
                 jax.ShapeDtypeStruct((n, N_EXPERTS), BF16),
                 jax.ShapeDtypeStruct((n, N_EXPERTS), BF16),
                 jax.ShapeDtypeStruct((n, N_EXPERTS), BF16),
                 jax.ShapeDtypeStruct((n // TOKEN_TILE, N_EXPERTS, LANES), F32),
                 jax.ShapeDtypeStruct((n // TOKEN_TILE, SUBLANES, LANES), F32)]
    return out_specs, out_shape


def _rglru_out_kernel(yf_ref, yr_ref, gate_ref, x_ref, er_ref, ec_ref, mod_ref, wout_ref, g2_ref, rwt_ref,
                      rb_ref, x1_ref, hx2_ref, hi_ref, lo_ref, hit_ref, lot_ref, gatet_ref, tab_ref, tabt_ref,
                      *, tiles_per_seq):
    d = x_ref.shape[1]
    row = pl.program_id(0) // tiles_per_seq
    yx = yf_ref[...] + yr_ref[...]
    v = gate_ref[...].astype(F32) * yx
    out = jnp.dot(v.astype(BF16), wout_ref[...], preferred_element_type=F32)
    x1 = _with_pos_code(x_ref, er_ref, ec_ref, tiles_per_seq) + _mod_chunk(mod_ref, row, 2, d) * out
    x1_ref[...] = x1
    _ffn_pre(x1, mod_ref, row, g2_ref, rwt_ref, rb_ref, hx2_ref, hi_ref, lo_ref, hit_ref, lot_ref, gatet_ref,
             tab_ref, tabt_ref)


def _rglru_out(y_fwd, y_rev, gate, x, er, ec, mod, w_out, g2, rwt, rb, tiles_per_seq):
    n, d = x.shape
    c = gate.shape[1]
    row = lambda i: (i, 0)
    const = lambda i: (0, 0)
    out_specs, out_shape = _ffn_pre_specs(n, d)
    return pl.pallas_call(
        functools.partial(_rglru_out_kernel, tiles_per_seq=tiles_per_seq),
        grid=(n // MIX_TILE,),
        in_specs=[pl.BlockSpec((MIX_TILE, c), row),
                  pl.BlockSpec((MIX_TILE, c), row),
                  pl.BlockSpec((MIX_TILE, c), row),
                  pl.BlockSpec((MIX_TILE, d), row),
                  pl.BlockSpec(er.shape, const),
                  pl.BlockSpec(ec.shape, const),
                  pl.BlockSpec(mod.shape, const),
                  pl.BlockSpec(w_out.shape, const),
                  pl.BlockSpec((1, d), const),
                  pl.BlockSpec(rwt.shape, const),
                  pl.BlockSpec(rb.shape, const)],
        out_specs=out_specs,
        out_shape=out_shape,
        compiler_params=_params(("arbitrary",)),
        name="rglru_out",
    )(y_fwd, y_rev, gate, x, er, ec, mod, w_out, g2, rwt, rb)


def _sgu_kernel(x_ref, mod_ref, g_ref, win_ref, lng_ref, lnb_ref, ws_ref, bst_ref, wout_ref,
                g2_ref, rwt_ref, rb_ref,
                x1_ref, hx2_ref, hi_ref, lo_ref, hit_ref, lot_ref, gatet_ref, tab_ref, tabt_ref, m_scr, *,
                tiles_per_seq):
    t_rows, d = x_ref.shape
    w = wout_ref.shape[0]
    gd = w // SGU_HEADS
    row = pl.program_id(0) // tiles_per_seq
    x = x_ref[...]
    hx = _rms_mod(x, g_ref[...], _mod_chunk(mod_ref, row, 1, d), _mod_chunk(mod_ref, row, 0, d))
    z = jax.nn.gelu(jnp.dot(hx.astype(BF16), win_ref[...], preferred_element_type=F32))
    u = z[:, :w]
    v = z[:, w:]
    mu = jnp.mean(v, axis=-1, keepdims=True)
    vc = v - mu
    v = vc * lax.rsqrt(jnp.mean(vc * vc, axis=-1, keepdims=True) + NORM_EPS) * lng_ref[...] + lnb_ref[...]
    vb = v.astype(BF16)
    for ch in range(t_rows // CHUNK):
        rs = slice(ch * CHUNK, (ch + 1) * CHUNK)
        for g in range(SGU_HEADS):
            cs = slice(g * gd, (g + 1) * gd)
            sv = jnp.dot(ws_ref[g], vb[rs, cs], preferred_element_type=F32) + bst_ref[:, g:g + 1]
            m_scr[rs, cs] = (u[rs, cs] * sv).astype(BF16)
    out = jnp.dot(m_scr[...], wout_ref[...], preferred_element_type=F32)
    x1 = x + _mod_chunk(mod_ref, row, 2, d) * out
    x1_ref[...] = x1
    _ffn_pre(x1, mod_ref, row, g2_ref, rwt_ref, rb_ref, hx2_ref, hi_ref, lo_ref, hit_ref, lot_ref, gatet_ref,
             tab_ref, tabt_ref)


def _sgu(x, mod, g, w_in, ln_g, ln_b, w_s, b_st, w_out, g2, rwt, rb, tiles_per_seq):
    n, d = x.shape
    w = w_out.shape[0]
    const = lambda i: (0, 0)
    out_specs, out_shape = _ffn_pre_specs(n, d)
    return pl.pallas_call(
        functools.partial(_sgu_kernel, tiles_per_seq=tiles_per_seq),
        grid=(n // MIX_TILE,),
        in_specs=[pl.BlockSpec((MIX_TILE, d), lambda i: (i, 0)),
                  pl.BlockSpec(mod.shape, const),
                  pl.BlockSpec((1, d), const),
                  pl.BlockSpec(w_in.shape, const, pipeline_mode=pl.Buffered(1)),
                  pl.BlockSpec((1, w), const),
                  pl.BlockSpec((1, w), const),
                  pl.BlockSpec(w_s.shape, lambda i: (0, 0, 0)),
                  pl.BlockSpec(b_st.shape, const),
                  pl.BlockSpec(w_out.shape, const, pipeline_mode=pl.Buffered(1)),
                  pl.BlockSpec((1, d), const),
                  pl.BlockSpec(rwt.shape, const),
                  pl.BlockSpec(rb.shape, const)],
        out_specs=out_specs,
        out_shape=out_shape,
        scratch_shapes=[pltpu.VMEM((MIX_TILE, w), BF16)],
        compiler_params=_params(("arbitrary",)),
        name="sgu",
    )(x, mod, g, w_in, ln_g, ln_b, w_s, b_st, w_out, g2, rwt, rb)


def _sorted_layout(m, eidx, gw):
    e, t = m.shape
    lanes = LANES
    r = lax.broadcasted_iota(jnp.int32, (t, t), 0)
    c = lax.broadcasted_iota(jnp.int32, (t, t), 1)
    upper = jnp.where(r <= c, 1.0, 0.0).astype(BF16)
    incl = jnp.dot(m.astype(BF16), upper, preferred_element_type=F32)
    run = _round_up_rows(incl[:, t - 1:t])
    re = lax.broadcasted_iota(jnp.int32, (e, e), 0)
    ce = lax.broadcasted_iota(jnp.int32, (e, e), 1)
    lower = jnp.where(ce < re, 1.0, 0.0).astype(BF16)
    tiles = jnp.broadcast_to(run * (1.0 / ROW_ALIGN), (e, lanes)).astype(BF16)
    lstart = jnp.dot(lower, tiles, preferred_element_type=F32)[:, 0:1] * float(ROW_ALIGN)
    pos = jnp.where(m > 0.0, lstart + incl - m, float(POS_RADIX * POS_RADIX - 1))
    hi = jnp.floor(pos * (1.0 / POS_RADIX))
    lo = pos - hi * float(POS_RADIX)
    iota_e = lax.broadcasted_iota(jnp.int32, (e, t), 0).astype(F32)
    gate = jnp.zeros((e, t), F32)
    for k in range(TOP_K):
        gate = jnp.where(iota_e == eidx[k:k + 1, :], gw[k:k + 1, :], gate)
    lane = lax.broadcasted_iota(jnp.int32, (e, lanes), 1)
    tab = jnp.where(lane == 0, run, jnp.where(lane == 2, lstart, 0.0))
    diag = lax.broadcasted_iota(jnp.int32, (e, lanes), 0) == lane
    row_of = lambda col: jnp.sum(jnp.where(diag, jnp.broadcast_to(col, (e, lanes)), 0.0), axis=0, keepdims=True)
    sub = lax.broadcasted_iota(jnp.int32, (SUBLANES, lanes), 0)
    tabt = jnp.where(sub == 0, row_of(lstart), jnp.where(sub == 1, row_of(lstart + run), 0.0))
    return hi, lo, gate, tab, tabt


def _aligned(v):
    return pl.multiple_of(v, ROW_ALIGN)


def _run_copies(tab_ref, make_copy, unroll, base=0):
    def body(e, carry):
        make_copy(_aligned(tab_ref[base + TAB_LOCAL + e]), _aligned(tab_ref[base + N_EXPERTS + e]),
                  _aligned(tab_ref[base + e])).start()
        return carry

    lax.fori_loop(0, N_EXPERTS, body, 0, unroll=unroll)


def _dispatch_kernel(tab_ref, ztab_ref, tabt_ref, hi_ref, lo_ref, x_ref, xs_hbm, sbuf, zbuf, sems, zsem):
    i = pl.program_id(0)
    tiles = sbuf.shape[0]
    t = x_ref.shape[0] // tiles
    rows = sbuf.shape[1]

    def zero_copy(e):
        n = _aligned(ztab_ref[e])
        return pltpu.make_async_copy(zbuf.at[pl.ds(0, n)], xs_hbm.at[pl.ds(_aligned(ztab_ref[N_EXPERTS + e]), n)], zsem)

    def for_zero_runs(fn):
        def body(e, c):
            @pl.when(ztab_ref[e] > 0)
            def _():
                fn(zero_copy(e))
            return c
        lax.fori_loop(0, N_EXPERTS, body, 0)

        def tail(b, c):
            fn(pltpu.make_async_copy(zbuf, xs_hbm.at[pl.ds(pl.multiple_of(b * EXPERT_BLOCK, EXPERT_BLOCK),
                                                           EXPERT_BLOCK)], zsem))
            return c
        lax.fori_loop(ztab_ref[2 * N_EXPERTS], xs_hbm.shape[0] // EXPERT_BLOCK, tail, 0)

    @pl.when(i == 0)
    def _():
        zbuf[...] = jnp.zeros_like(zbuf)
        for_zero_runs(lambda cp: cp.start())

    def wait_rows(s, n):
        pltpu.make_async_copy(sbuf.at[s, pl.ds(0, n)], xs_hbm.at[pl.ds(0, n)], sems.at[s]).wait()

    def permute(s, r0):
        ts = slice(s * t, (s + 1) * t)
        j_e = (lax.broadcasted_iota(jnp.int32, (PERM_CHUNK, N_EXPERTS), 0) + r0).astype(F32)
        owner = jnp.where((j_e >= tabt_ref[s, 0:1, 0:N_EXPERTS]) & (j_e < tabt_ref[s, 1:2, 0:N_EXPERTS]),
                          1.0, 0.0).astype(BF16)
        pos = (jnp.dot(owner, hi_ref[:, ts], preferred_element_type=F32) * float(POS_RADIX)
               + jnp.dot(owner, lo_ref[:, ts], preferred_element_type=F32))
        j_t = (lax.broadcasted_iota(jnp.int32, (PERM_CHUNK, t), 0) + r0).astype(F32)
        p = jnp.where(pos == j_t, 1.0, 0.0).astype(BF16)
        sbuf[s, r0:r0 + PERM_CHUNK, :] = jnp.dot(p, x_ref[ts, :], preferred_element_type=F32).astype(BF16)

    for s in range(tiles):
        base = s * TAB_WIDTH

        @pl.when(i > 0)
        def _():
            wait_rows(s, _aligned(tab_ref[base + 2 * N_EXPERTS + 1]))

        for r0 in range(0, rows, PERM_CHUNK):
            if r0 < t * TOP_K + PERM_CHUNK:
                permute(s, r0)
            else:
                pl.when(tab_ref[base + 2 * N_EXPERTS] > r0)(functools.partial(permute, s, r0))

        _run_copies(tab_ref, lambda loc, glob, n: pltpu.make_async_copy(
            sbuf.at[s, pl.ds(loc, n)], xs_hbm.at[pl.ds(glob, n)], sems.at[s]), unroll=True, base=base)

    @pl.when(i == pl.num_programs(0) - 1)
    def _():
        for s in range(tiles):
            wait_rows(s, _aligned(tab_ref[s * TAB_WIDTH + 2 * N_EXPERTS]))

    @pl.when(i == 0)
    def _():
        for_zero_runs(lambda cp: cp.wait())


def _dispatch(tab, ztab, tab_t, pos_hi, pos_lo, hx2, xs_rows):
    n, d = hx2.shape
    col = lambda i: (0, i)
    return pl.pallas_call(
        _dispatch_kernel,
        grid=(n // (DISPATCH_TILES * TOKEN_TILE),),
        in_specs=[pl.BlockSpec((DISPATCH_TILES * TAB_WIDTH,), lambda i: (i,), memory_space=pltpu.SMEM),
                  pl.BlockSpec(memory_space=pltpu.SMEM),
                  pl.BlockSpec((DISPATCH_TILES, SUBLANES, LANES), lambda i: (i, 0, 0)),
                  pl.BlockSpec((N_EXPERTS, DISPATCH_TILES * TOKEN_TILE), col),
                  pl.BlockSpec((N_EXPERTS, DISPATCH_TILES * TOKEN_TILE), col),
                  pl.BlockSpec((DISPATCH_TILES * TOKEN_TILE, d), lambda i: (i, 0))],
        out_specs=pl.BlockSpec(memory_space=pl.ANY),
        out_shape=jax.ShapeDtypeStruct((xs_rows, d), BF16),
        scratch_shapes=[pltpu.VMEM((DISPATCH_TILES, SORTED_ROWS, d), BF16),
                        pltpu.VMEM((EXPERT_BLOCK, d), BF16),
                        pltpu.SemaphoreType.DMA((DISPATCH_TILES,)),
                        pltpu.SemaphoreType.DMA],
        compiler_params=_params(("arbitrary",)),
        name="moe_dispatch",
    )(tab, ztab, tab_t, pos_hi, pos_lo, hx2)


def _experts_kernel(be_ref, nb_ref, xs_ref, wgu_ref, wd_ref, ys_ref, wgu_b, wd_b):
    i = pl.program_id(0)
    ff = wd_b.shape[0]
    used = i < nb_ref[0]
    new_expert = (i == 0) | (be_ref[i] != be_ref[jnp.maximum(i - 1, 0)])

    @pl.when(used & new_expert)
    def _():
        wgu_b[...] = wgu_ref[0, 0].astype(BF16)
        wd_b[...] = wd_ref[0, 0].astype(BF16)

    @pl.when(used)
    def _():
        h = jnp.dot(xs_ref[...], wgu_b[...], preferred_element_type=F32)
        a = _silu(h[:, :ff]) * h[:, ff:]
        ys_ref[...] = jnp.dot(a.astype(BF16), wd_b[...], preferred_element_type=F32).astype(BF16)


def _experts(block_e, n_used, xs, w_gu, w_down, layer):
    rows, d = xs.shape
    nb = rows // EXPERT_BLOCK
    blk = lambda i, be, nu: (jnp.maximum(jnp.minimum(i, nu[0] - 1), 0), 0)
    wmap = lambda i, be, nu: (layer, be[i], 0, 0)
    return pl.pallas_call(
        _experts_kernel,
        grid_spec=pltpu.PrefetchScalarGridSpec(
            num_scalar_prefetch=2,
            grid=(nb,),
            in_specs=[pl.BlockSpec((EXPERT_BLOCK, d), blk),
                      pl.BlockSpec((1, 1) + w_gu.shape[2:], wmap),
                      pl.BlockSpec((1, 1) + w_down.shape[2:], wmap)],
            out_specs=pl.BlockSpec((EXPERT_BLOCK, d), blk),
            scratch_shapes=[pltpu.VMEM(w_gu.shape[2:], BF16), pltpu.VMEM(w_down.shape[2:], BF16)]),
        out_shape=jax.ShapeDtypeStruct((rows, d), BF16),
        input_output_aliases={2: 0},
        compiler_params=_params(("arbitrary",)),
        name="moe_experts",
    )(block_e, n_used, xs, w_gu, w_down)


def _combine_kernel(tab_ref, tabn_ref, tabv_ref, hi_ref, lo_ref, gate_ref, ys_hbm, hx2_ref, x1_ref, mod_ref,
                    wsgu_ref, wsd_ref, fg_ref, out_ref, ybuf, sems, *, tiles_per_seq, final_norm):
    t_rows, d = x1_ref.shape
    ff = wsd_ref.shape[0]
    rows = ybuf.shape[1]
    i = pl.program_id(0)
    last = pl.num_programs(0) - 1
    slot = i % 2
    row = i // tiles_per_seq

    def gather(table, s, unroll):
        _run_copies(table, lambda loc, glob, n: pltpu.make_async_copy(
            ys_hbm.at[pl.ds(glob, n)], ybuf.at[s, pl.ds(loc, n)], sems.at[s]), unroll=unroll)

    def wait_rows(table, s):
        n = _aligned(table[2 * N_EXPERTS])
        pltpu.make_async_copy(ys_hbm.at[pl.ds(0, n)], ybuf.at[s, pl.ds(0, n)], sems.at[s]).wait()

    @pl.when(i == 0)
    def _():
        ybuf[...] = jnp.zeros_like(ybuf)
        gather(tab_ref, 0, False)

    gather(tabn_ref, 1 - slot, True)

    hs = jnp.dot(hx2_ref[...], wsgu_ref[...], preferred_element_type=F32)
    shared = jnp.dot((_silu(hs[:, :ff]) * hs[:, ff:]).astype(BF16), wsd_ref[...], preferred_element_type=F32)

    run_lo = tabv_ref[0, :, 2:3]
    run_hi = run_lo + tabv_ref[0, :, 0:1]
    j_e = lax.broadcasted_iota(jnp.int32, (N_EXPERTS, rows), 1).astype(F32)
    owner = jnp.where((j_e >= run_lo) & (j_e < run_hi), 1.0, 0.0).astype(BF16)
    pos = (jnp.dot(hi_ref[...], owner, preferred_element_type=F32) * float(POS_RADIX)
           + jnp.dot(lo_ref[...], owner, preferred_element_type=F32))
    gates = jnp.dot(gate_ref[...], owner, preferred_element_type=F32)
    j_t = lax.broadcasted_iota(jnp.int32, (t_rows, rows), 1).astype(F32)
    gb = jnp.where(pos == j_t, gates, 0.0).astype(BF16)

    wait_rows(tab_ref, slot)
    routed = jnp.dot(gb, ybuf[slot], preferred_element_type=F32)
    x2 = x1_ref[...] + _mod_chunk(mod_ref, row, 5, d) * (routed + shared)
    if final_norm:
        x2 = x2 * lax.rsqrt(jnp.mean(x2 * x2, axis=-1, keepdims=True) + NORM_EPS) * fg_ref[...]
    out_ref[...] = x2

    @pl.when(i == last)
    def _():
        wait_rows(tabn_ref, 1 - slot)


def _combine(tab, tab_v, pos_hi, pos_lo, gate, ys, hx2, x1, mod, ws_gu, ws_down, fg, tiles_per_seq, final_norm):
    n, d = x1.shape
    row = lambda i: (i, 0)
    const = lambda i: (0, 0)
    return pl.pallas_call(
        functools.partial(_combine_kernel, tiles_per_seq=tiles_per_seq, final_norm=final_norm),
        grid=(n // TOKEN_TILE,),
        in_specs=[pl.BlockSpec((TAB_WIDTH,), lambda i: (i,), memory_space=pltpu.SMEM),
                  pl.BlockSpec((TAB_WIDTH,), lambda i: (jnp.minimum(i + 1, n // TOKEN_TILE - 1),),
                               memory_space=pltpu.SMEM),
                  pl.BlockSpec((1, N_EXPERTS, LANES), lambda i: (i, 0, 0)),
                  pl.BlockSpec((TOKEN_TILE, N_EXPERTS), row),
                  pl.BlockSpec((TOKEN_TILE, N_EXPERTS), row),
                  pl.BlockSpec((TOKEN_TILE, N_EXPERTS), row),
                  pl.BlockSpec(memory_space=pl.ANY),
                  pl.BlockSpec((TOKEN_TILE, d), row),
                  pl.BlockSpec((TOKEN_TILE, d), row),
                  pl.BlockSpec(mod.shape, const),
                  pl.BlockSpec(ws_gu.shape, const),
                  pl.BlockSpec(ws_down.shape, const),
                  pl.BlockSpec((1, d), const)],
        out_specs=pl.BlockSpec((TOKEN_TILE, d), row),
        out_shape=jax.ShapeDtypeStruct((n, d), F32),
        scratch_shapes=[pltpu.VMEM((2, SORTED_ROWS, d), BF16), pltpu.SemaphoreType.DMA((2,))],
        compiler_params=_params(("arbitrary",)),
        name="moe_combine",
    )(tab, tab, tab_v, pos_hi, pos_lo, gate, ys, hx2, x1, mod, ws_gu, ws_down, fg)


def _moe(x1, hx2, pos_hi, pos_lo, pos_hi_t, pos_lo_t, gate_t, tab_f, tab_t, mod, w_gu, w_down, layer, ws_gu,
         ws_down, fg, tiles_per_seq, final_norm):
    n, d = x1.shape
    n_tiles = n // TOKEN_TILE
    run_len = tab_f[:, :, 0].astype(jnp.int32)
    counts = jnp.sum(run_len, axis=0)
    padded = (counts + EXPERT_BLOCK - 1) // EXPERT_BLOCK * EXPERT_BLOCK
    pad_end = jnp.cumsum(padded)
    pad_start = pad_end - padded
    max_rows = n * TOP_K + n_tiles * N_EXPERTS * ROW_ALIGN + N_EXPERTS * (EXPERT_BLOCK - ROW_ALIGN)
    n_blocks = (max_rows + EXPERT_BLOCK - 1) // EXPERT_BLOCK
    n_used = (pad_end[-1:] // EXPERT_BLOCK).astype(jnp.int32)
    block_start = jnp.arange(n_blocks, dtype=jnp.int32) * EXPERT_BLOCK
    block_e = jnp.minimum(jnp.sum((pad_end[None, :] <= block_start[:, None]).astype(jnp.int32), axis=1),
                          N_EXPERTS - 1)

    run_start = pad_start[None, :] + jnp.cumsum(run_len, axis=0) - run_len
    total = jnp.sum(run_len, axis=1, keepdims=True)
    prev_total = jnp.concatenate([jnp.zeros((DISPATCH_TILES, 1), jnp.int32), total[:-DISPATCH_TILES]], axis=0)
    run_local = tab_f[:, :, 2].astype(jnp.int32)
    fill = jnp.zeros((n_tiles, TAB_WIDTH - 3 * N_EXPERTS - 2), jnp.int32)
    tab = jnp.concatenate([run_len, run_start, total, prev_total, run_local, fill], axis=1).reshape(-1)
    ztab = jnp.concatenate([padded - counts, pad_start + counts, n_used])

    xs = _dispatch(tab, ztab, tab_t, pos_hi, pos_lo, hx2, n_blocks * EXPERT_BLOCK)
    ys = _experts(block_e, n_used, xs, w_gu, w_down, layer)
    return _combine(tab, tab_f, pos_hi_t, pos_lo_t, gate_t, ys, hx2, x1, mod, ws_gu, ws_down, fg,
                    tiles_per_seq, final_norm)


def _sincos_tables(rows, d):
    quarter = d // 4
    omega = 1.0 / (POS_BASE ** (jnp.arange(quarter, dtype=F32) / quarter))

    def emb(n):
        p = jnp.arange(n, dtype=F32)[:, None] * omega[None, :]
        return jnp.concatenate([jnp.sin(p), jnp.cos(p)], axis=-1)

    return emb(rows), emb(GRID_W)


def kernel(x, c, ctx, c_ctx, ada_w, ada_b, mix_norm_g, ffn_norm_g, a_w_in, a_conv_w, a_conv_b, a_gate_r_w, a_gate_r_b, a_gate_i_w, a_gate_i_b, a_lambda, a_w_out, b_w_in, b_ln_g, b_ln_b, b_w_s, b_b_s, b_w_out, router_w, router_b, moe_w_gu, moe_w_down, shared_w_gu, shared_w_down, final_norm_g):
    bsz, s, d = x.shape
    ctx_len = ctx.shape[1]
    depth = ada_w.shape[0]
    assert depth == 2 and bsz < MOD_ROWS and s % MIX_TILE == 0 and MIX_TILE % TOKEN_TILE == 0
    assert s % SCAN_TILE == 0 and MIX_TILE % GRID_W == 0
    assert ctx_len % TOKEN_TILE == 0
    n = bsz * s
    tps = s // TOKEN_TILE
    ctx_row = bsz

    cc = jnp.zeros((MOD_ROWS, d), F32).at[:bsz].set(c).at[ctx_row].set(c_ctx)
    mod = _modulation(cc, ada_w, ada_b)
    er, ec = _sincos_tables(s // GRID_W, d)
    rc = a_w_in.shape[2] // 2

    w_in0 = a_w_in[0].astype(BF16)
    g_mix0 = mix_norm_g[0].reshape(1, d)
    x2 = x.reshape(n, d)
    gate, ux = _rglru_in(x2, er, ec, mod[0], g_mix0, w_in0)
    uc = _ctx_in(ctx.reshape(bsz * ctx_len, d), mod[0], g_mix0, w_in0[:, rc:], ctx_row)
    w_ri = jnp.concatenate([a_gate_r_w[0], a_gate_i_w[0]], axis=-1).astype(BF16)
    conv = (a_conv_w[0], a_conv_b[0])
    gates = [(w_ri[k], a_gate_r_b[0, k], a_gate_i_b[0, k], a_lambda[0, k]) for k in range(2)]
    h_zero = jnp.zeros((bsz, rc), F32)
    h_fwd, uc = _lru_scan(uc, conv, *gates[0], h_zero, reverse=False, reset_first=True, emit_y=False)
    h_rev = _lru_scan(uc, None, *gates[1], h_zero, reverse=True, reset_first=True, emit_y=False)
    y_fwd, ux = _lru_scan(ux, conv, *gates[0], h_fwd, reverse=False, reset_first=False, emit_y=True)
    y_rev = _lru_scan(ux, None, *gates[1], h_rev, reverse=True, reset_first=False, emit_y=True)
    pre = _rglru_out(y_fwd, y_rev, gate, x2, er, ec, mod[0], a_w_out[0].astype(BF16), ffn_norm_g[0].reshape(1, d),
                     router_w[0].T, router_b[0].reshape(N_EXPERTS, 1), s // MIX_TILE)
    x1 = _moe(*pre, mod[0], moe_w_gu, moe_w_down, 0, shared_w_gu[0].astype(BF16),
              shared_w_down[0].astype(BF16), final_norm_g.reshape(1, d), tps, False)

    pre = _sgu(x1, mod[1], mix_norm_g[1].reshape(1, d), b_w_in[0].astype(BF16),
               b_ln_g[0].reshape(1, -1), b_ln_b[0].reshape(1, -1), b_w_s[0].astype(BF16), b_b_s[0].T,
               b_w_out[0].astype(BF16), ffn_norm_g[1].reshape(1, d),
               router_w[1].T, router_b[1].reshape(N_EXPERTS, 1), s // MIX_TILE)
    out = _moe(*pre, mod[1], moe_w_gu, moe_w_down, 1, shared_w_gu[1].astype(BF16),
               shared_w_down[1].astype(BF16), final_norm_g.reshape(1, d), tps, True)
    return out.reshape(bsz, s, d)
```

```python
import functools

import jax
import jax.numpy as jnp
from jax import lax
from jax.experimental import pallas as pl
from jax.experimental.pallas import tpu as pltpu

F32 = jnp.float32
BF16 = jnp.bfloat16
HIGHEST = lax.Precision.HIGHEST

GRID_W = 64
NORM_EPS = 1e-6
POS_BASE = 10000.0
RNN_HEADS = 5
CONV_WIDTH = 4
CONV_PAD_LEFT = 2
LRU_C = 8.0
SGU_HEADS = 8
CHUNK = 128
N_EXPERTS = 64
TOP_K = 8
N_GROUPS = 8
TOPK_GROUPS = 4
EXPERTS_PER_GROUP = N_EXPERTS // N_GROUPS
ROUTED_SCALE = 2.5

SUBLANES = 8
ROW_ALIGN = 16
LANES = 128
MOD_ROWS = 8
TOKEN_TILE = 256
MIX_TILE = 512
SCAN_TILE = 512
EXPERT_BLOCK = 1024
SORTED_ROWS = TOKEN_TILE * TOP_K + N_EXPERTS * ROW_ALIGN
PERM_CHUNK = 512
DISPATCH_TILES = 2
TAB_WIDTH = 256
TAB_LOCAL = 2 * N_EXPERTS + 2
POS_RADIX = 64
VMEM_LIMIT = 56 * 1024 * 1024


def _params(semantics, vmem=VMEM_LIMIT):
    return pltpu.CompilerParams(dimension_semantics=semantics, vmem_limit_bytes=vmem)


def _silu(x):
    return x * jax.nn.sigmoid(x)


def _rms_mod(x, g, sc, sh):
    y = x * lax.rsqrt(jnp.mean(x * x, axis=-1, keepdims=True) + NORM_EPS)
    return (y * g) * (1.0 + sc) + sh


def _mod_chunk(mod_ref, row, k, d):
    return mod_ref[pl.ds(row, 1), k * d:(k + 1) * d]


def _dot_nt_3pass(a, b):
    dims = (((1,), (1,)), ((), ()))
    a_hi, b_hi = a.astype(BF16), b.astype(BF16)
    a_lo = (a - a_hi.astype(F32)).astype(BF16)
    b_lo = (b - b_hi.astype(F32)).astype(BF16)
    dot = functools.partial(lax.dot_general, dimension_numbers=dims, preferred_element_type=F32)
    return dot(a_hi, b_hi) + dot(a_hi, b_lo) + dot(a_lo, b_hi)


def _round_up_rows(count):
    return jnp.maximum(jnp.ceil(count * (1.0 / ROW_ALIGN)), 1.0) * float(ROW_ALIGN)


def _mod_kernel(cc_ref, w_ref, b_ref, o_ref):
    s = _silu(cc_ref[...])
    o_ref[0] = jnp.dot(s, w_ref[0], preferred_element_type=F32, precision=HIGHEST) + b_ref[0]


def _modulation(cc, ada_w, ada_b):
    depth, d, nd = ada_w.shape
    return pl.pallas_call(
        _mod_kernel,
        grid=(depth, nd // d),
        in_specs=[pl.BlockSpec((MOD_ROWS, d), lambda l, j: (0, 0)),
                  pl.BlockSpec((1, d, d), lambda l, j: (l, 0, j)),
                  pl.BlockSpec((1, 1, d), lambda l, j: (l, 0, j))],
        out_specs=pl.BlockSpec((1, MOD_ROWS, d), lambda l, j: (l, 0, j)),
        out_shape=jax.ShapeDtypeStruct((depth, MOD_ROWS, nd), F32),
        compiler_params=_params(("arbitrary", "arbitrary")),
        name="modulation",
    )(cc, ada_w, ada_b.reshape(depth, 1, nd))


def _with_pos_code(x_ref, er_ref, ec_ref, tiles_per_seq):
    t_rows = x_ref.shape[0]
    grid_w, half = ec_ref.shape
    rows = t_rows // grid_w
    r0 = pl.multiple_of((pl.program_id(0) % tiles_per_seq) * rows, rows)
    er = er_ref[pl.ds(r0, rows), :]
    pe = jnp.concatenate([jnp.broadcast_to(er[:, None, :], (rows, grid_w, half)).reshape(t_rows, half),
                          jnp.concatenate([ec_ref[...]] * rows, axis=0)], axis=1)
    return x_ref[...] + pe


def _rglru_in_kernel(x_ref, er_ref, ec_ref, mod_ref, g_ref, w_ref, gate_ref, u_ref, *, tiles_per_seq):
    d = x_ref.shape[1]
    c = u_ref.shape[1]
    row = pl.program_id(0) // tiles_per_seq
    x = _with_pos_code(x_ref, er_ref, ec_ref, tiles_per_seq)
    hx = _rms_mod(x, g_ref[...], _mod_chunk(mod_ref, row, 1, d), _mod_chunk(mod_ref, row, 0, d))
    z = jnp.dot(hx.astype(BF16), w_ref[...], preferred_element_type=F32)
    gate_ref[...] = jax.nn.gelu(z[:, :c]).astype(BF16)
    u_ref[...] = z[:, c:]


def _rglru_in(x2, er, ec, mod, g, w_in):
    n, d = x2.shape
    c = w_in.shape[1] // 2
    tps = er.shape[0] * ec.shape[0] // MIX_TILE
    row = lambda i: (i, 0)
    return pl.pallas_call(
        functools.partial(_rglru_in_kernel, tiles_per_seq=tps),
        grid=(n // MIX_TILE,),
        in_specs=[pl.BlockSpec((MIX_TILE, d), row),
                  pl.BlockSpec(er.shape, lambda i: (0, 0)),
                  pl.BlockSpec(ec.shape, lambda i: (0, 0)),
                  pl.BlockSpec(mod.shape, lambda i: (0, 0)),
                  pl.BlockSpec((1, d), lambda i: (0, 0)),
                  pl.BlockSpec(w_in.shape, lambda i: (0, 0))],
        out_specs=[pl.BlockSpec((MIX_TILE, c), row),
                   pl.BlockSpec((MIX_TILE, c), row)],
        out_shape=[jax.ShapeDtypeStruct((n, c), BF16),
                   jax.ShapeDtypeStruct((n, c), F32)],
        compiler_params=_params(("arbitrary",)),
        name="rglru_in",
    )(x2, er, ec, mod, g, w_in)


def _ctx_in_kernel(x_ref, mod_ref, g_ref, w_ref, u_ref, *, ctx_row):
    d = x_ref.shape[1]
    hx = _rms_mod(x_ref[...], g_ref[...], _mod_chunk(mod_ref, ctx_row, 1, d),
                  _mod_chunk(mod_ref, ctx_row, 0, d))
    u_ref[...] = jnp.dot(hx.astype(BF16), w_ref[...], preferred_element_type=F32)


def _ctx_in(c2, mod, g, w_u, ctx_row):
    n, d = c2.shape
    c = w_u.shape[1]
    return pl.pallas_call(
        functools.partial(_ctx_in_kernel, ctx_row=ctx_row),
        grid=(n // TOKEN_TILE,),
        in_specs=[pl.BlockSpec((TOKEN_TILE, d), lambda i: (i, 0)),
                  pl.BlockSpec(mod.shape, lambda i: (0, 0)),
                  pl.BlockSpec((1, d), lambda i: (0, 0)),
                  pl.BlockSpec(w_u.shape, lambda i: (0, 0))],
        out_specs=pl.BlockSpec((TOKEN_TILE, c), lambda i: (i, 0)),
        out_shape=jax.ShapeDtypeStruct((n, c), F32),
        compiler_params=_params(("arbitrary",)),
        name="ctx_in",
    )(c2, mod, g, w_u)


def _log_sigmoid(x):
    return jnp.minimum(x, 0.0) - jnp.log1p(jnp.exp(-jnp.abs(x)))


def _lru_scan_kernel(*refs, n_tiles, reverse, conv_done, reset_first, emit_y):
    if conv_done:
        u_ref, wri_ref, rb_ref, ib_ref, lam_ref, h0_ref, out_ref, a_scr, b_scr, h_scr = refs
    else:
        (u_ref, up_ref, un_ref, cw_ref, cb_ref, wri_ref, rb_ref, ib_ref, lam_ref, h0_ref,
         out_ref, uc_ref, ubuf, a_scr, b_scr, h_scr) = refs
    t_rows, c = u_ref.shape
    hb = c // RNN_HEADS
    b = pl.program_id(0)
    j = pl.program_id(1)
    jj = n_tiles - 1 - j if reverse else j

    if conv_done:
        u = u_ref[...]
    else:
        ubuf[SUBLANES:SUBLANES + t_rows, :] = u_ref[...]
        ubuf[0:SUBLANES, :] = jnp.where(jj == 0, 0.0, up_ref[...])
        ubuf[SUBLANES + t_rows:, :] = jnp.where(jj == n_tiles - 1, 0.0, un_ref[...])
        u = cb_ref[...]
        full = ubuf[...]
        for k in range(CONV_WIDTH):
            shift = (CONV_PAD_LEFT - k) % full.shape[0]
            tap = full if shift == 0 else pltpu.roll(full, shift, 0)
            u = u + cw_ref[k:k + 1, :] * tap[SUBLANES:SUBLANES + t_rows, :]
        uc_ref[...] = u

    log_lam = LRU_C * _log_sigmoid(lam_ref[...])
    rows = lax.broadcasted_iota(jnp.int32, (t_rows, 1), 0)
    first_row = jnp.where(j == 0, t_rows - 1 if reverse else 0, -1)
    for h in range(RNN_HEADS):
        sl = slice(h * hb, (h + 1) * hb)
        uh = u[:, sl]
        z = jnp.dot(uh.astype(BF16), wri_ref[h], preferred_element_type=F32)
        r = jax.nn.sigmoid(z[:, :hb] + rb_ref[:, sl])
        ig = jax.nn.sigmoid(z[:, hb:] + ib_ref[:, sl])
        log_a = r * log_lam[:, sl]
        a = jnp.exp(log_a)
        mult = jnp.sqrt((1.0 - a) * (1.0 + a))
        if reset_first:
            mult = jnp.where(rows == first_row, 1.0, mult)
        a_scr[:, sl] = a
        b_scr[:, sl] = mult * ig * uh

    @pl.when(j == 0)
    def _():
        h_scr[...] = h0_ref[pl.ds(b, 1), :]

    n_groups = t_rows // SUBLANES

    def group(g, h):
        base = pl.multiple_of((n_groups - 1 - g if reverse else g) * SUBLANES, SUBLANES)
        for s in range(SUBLANES):
            r = base + (SUBLANES - 1 - s if reverse else s)
            h = a_scr[pl.ds(r, 1), :] * h + b_scr[pl.ds(r, 1), :]
            if emit_y:
                out_ref[pl.ds(r, 1), :] = h
        return h

    h = lax.fori_loop(0, n_groups, group, h_scr[...])
    h_scr[...] = h
    if not emit_y:
        @pl.when(j == n_tiles - 1)
        def _():
            out_ref[pl.ds(b, 1), :] = h


def _lru_scan(u, conv, w_ri, r_b, i_b, lam, h0, *, reverse, reset_first, emit_y):
    n, c = u.shape
    n_batch = h0.shape[0]
    rows = min(SCAN_TILE, n // n_batch)
    n_tiles = n // n_batch // rows
    sub = rows // SUBLANES
    n_sub = n // SUBLANES
    const = lambda b, j: (0, 0)

    def tile(b, j):
        return b * n_tiles + (n_tiles - 1 - j if reverse else j)

    tile_spec = pl.BlockSpec((rows, c), lambda b, j: (tile(b, j), 0))
    in_specs = [tile_spec]
    args = [u]
    scratch = []
    if conv is not None:
        conv_w, conv_b = conv
        in_specs += [pl.BlockSpec((SUBLANES, c), lambda b, j: (jnp.maximum(tile(b, j) * sub - 1, 0), 0)),
                     pl.BlockSpec((SUBLANES, c), lambda b, j: (jnp.minimum((tile(b, j) + 1) * sub, n_sub - 1), 0)),
                     pl.BlockSpec(conv_w.shape, const),
                     pl.BlockSpec((1, c), const)]
        args += [u, u, conv_w, conv_b.reshape(1, c)]
        scratch = [pltpu.VMEM((rows + 2 * SUBLANES, c), F32)]
    in_specs += [pl.BlockSpec(w_ri.shape, lambda b, j: (0, 0, 0)),
                 pl.BlockSpec((1, c), const), pl.BlockSpec((1, c), const), pl.BlockSpec((1, c), const),
                 pl.BlockSpec(h0.shape, const)]
    args += [w_ri, r_b.reshape(1, c), i_b.reshape(1, c), lam.reshape(1, c), h0]
    if emit_y:
        out_specs = [tile_spec]
        out_shape = [jax.ShapeDtypeStruct((n, c), F32)]
    else:
        out_specs = [pl.BlockSpec(h0.shape, const)]
        out_shape = [jax.ShapeDtypeStruct(h0.shape, F32)]
    if conv is not None:
        out_specs.append(tile_spec)
        out_shape.append(jax.ShapeDtypeStruct((n, c), F32))
    name = ("lru_scan" if emit_y else "lru_ctx") + ("_rev" if reverse else "_fwd")
    outs = pl.pallas_call(
        functools.partial(_lru_scan_kernel, n_tiles=n_tiles, reverse=reverse, conv_done=conv is None,
                          reset_first=reset_first, emit_y=emit_y),
        grid=(n_batch, n_tiles),
        in_specs=in_specs,
        out_specs=out_specs,
        out_shape=out_shape,
        scratch_shapes=scratch + [pltpu.VMEM((rows, c), F32),
                                  pltpu.VMEM((rows, c), F32),
                                  pltpu.VMEM((1, c), F32)],
        compiler_params=_params(("arbitrary", "arbitrary")),
        name=name,
    )(*args)
    return outs if conv is not None else outs[0]


def _route(logits, rb):
    e, t = logits.shape
    neg = -jnp.inf
    scores = jax.nn.sigmoid(logits)
    sel = scores + rb
    iota_g = lax.broadcasted_iota(jnp.int32, (N_GROUPS, t), 0).astype(F32)
    iota_e = lax.broadcasted_iota(jnp.int32, (e, t), 0).astype(F32)

    gs = jnp.full((N_GROUPS, t), neg, F32)
    for g in range(N_GROUPS):
        sg = sel[g * EXPERTS_PER_GROUP:(g + 1) * EXPERTS_PER_GROUP, :]
        m1 = jnp.max(sg, axis=0, keepdims=True)
        i1 = jnp.min(jnp.where(sg == m1, iota_g, float(EXPERTS_PER_GROUP)), axis=0, keepdims=True)
        m2 = jnp.max(jnp.where(iota_g == i1, neg, sg), axis=0, keepdims=True)
        gs = jnp.where(iota_g == float(g), m1 + m2, gs)

    keep = jnp.zeros((N_GROUPS, t), F32)
    for _ in range(TOPK_GROUPS):
        m = jnp.max(gs, axis=0, keepdims=True)
        idx = jnp.min(jnp.where(gs == m, iota_g, float(N_GROUPS)), axis=0, keepdims=True)
        hit = iota_g == idx
        keep = jnp.where(hit, 1.0, keep)
        gs = jnp.where(hit, neg, gs)

    masked = jnp.concatenate(
        [jnp.where(keep[g:g + 1, :] > 0.0, sel[g * EXPERTS_PER_GROUP:(g + 1) * EXPERTS_PER_GROUP, :], neg)
         for g in range(N_GROUPS)], axis=0)

    iota_k = lax.broadcasted_iota(jnp.int32, (TOP_K, t), 0)
    selmask = jnp.zeros((e, t), F32)
    eidx = jnp.zeros((TOP_K, t), F32)
    gw = jnp.zeros((TOP_K, t), F32)
    for k in range(TOP_K):
        m = jnp.max(masked, axis=0, keepdims=True)
        idx = jnp.min(jnp.where(masked == m, iota_e, float(e)), axis=0, keepdims=True)
        hit = iota_e == idx
        gk = jnp.sum(jnp.where(hit, scores, 0.0), axis=0, keepdims=True)
        masked = jnp.where(hit, neg, masked)
        selmask = jnp.where(hit, 1.0, selmask)
        eidx = jnp.where(iota_k == k, idx, eidx)
        gw = jnp.where(iota_k == k, gk, gw)
    gw = gw / jnp.sum(gw, axis=0, keepdims=True) * ROUTED_SCALE
    return eidx, gw, selmask


def _ffn_pre(x1, mod_ref, row, g2_ref, rwt_ref, rb_ref, hx2_ref, hi_ref, lo_ref, hit_ref, lot_ref, gatet_ref,
             tab_ref, tabt_ref):
    d = x1.shape[1]
    hx2 = _rms_mod(x1, g2_ref[...], _mod_chunk(mod_ref, row, 4, d), _mod_chunk(mod_ref, row, 3, d))
    hx2_ref[...] = hx2.astype(BF16)
    logits = _dot_nt_3pass(rwt_ref[...], hx2)
    eidx, gw, selmask = _route(logits, rb_ref[...])
    for s in range(selmask.shape[1] // TOKEN_TILE):
        ts = slice(s * TOKEN_TILE, (s + 1) * TOKEN_TILE)
        hi, lo, gate, tab, tabt = _sorted_layout(selmask[:, ts], eidx[:, ts], gw[:, ts])
        hi_ref[:, ts] = hi.astype(BF16)
        lo_ref[:, ts] = lo.astype(BF16)
        hit_ref[ts, :] = hi.T.astype(BF16)
        lot_ref[ts, :] = lo.T.astype(BF16)
        gatet_ref[ts, :] = gate.T.astype(BF16)
        tab_ref[s] = tab
        tabt_ref[s] = tabt


def _ffn_pre_specs(n, d):
    row = lambda i: (i, 0)
    col = lambda i: (0, i)
    sub = MIX_TILE // TOKEN_TILE
    lead = lambda i: (i, 0, 0)
    out_specs = [pl.BlockSpec((MIX_TILE, d), row),
                 pl.BlockSpec((MIX_TILE, d), row),
                 pl.BlockSpec((N_EXPERTS, MIX_TILE), col),
                 pl.BlockSpec((N_EXPERTS, MIX_TILE), col),
                 pl.BlockSpec((MIX_TILE, N_EXPERTS), row),
                 pl.BlockSpec((MIX_TILE, N_EXPERTS), row),
                 pl.BlockSpec((MIX_TILE, N_EXPERTS), row),
                 pl.BlockSpec((sub, N_EXPERTS, LANES), lead),
                 pl.BlockSpec((sub, SUBLANES, LANES), lead)]
    out_shape = [jax.ShapeDtypeStruct((n, d), F32),
                 jax.ShapeDtypeStruct((n, d), BF16),
                 jax.ShapeDtypeStruct((N_EXPERTS, n), BF16),
                 jax.ShapeDtypeStruct((N_EXPERTS, n), BF16),
                 jax.ShapeDtypeStruct((n, N_EXPERTS), BF16),
                 jax.ShapeDtypeStruct((n, N_EXPERTS), BF16),
                 jax.ShapeDtypeStruct((n, N_EXPERTS), BF16),
                 jax.ShapeDtypeStruct((n // TOKEN_TILE, N_EXPERTS, LANES), F32),
                 jax.ShapeDtypeStruct((n // TOKEN_TILE, SUBLANES, LANES), F32)]
    return out_specs, out_shape


def _rglru_out_kernel(yf_ref, yr_ref, gate_ref, x_ref, er_ref, ec_ref, mod_ref, wout_ref, g2_ref, rwt_ref,
                      rb_ref, x1_ref, hx2_ref, hi_ref, lo_ref, hit_ref, lot_ref, gatet_ref, tab_ref, tabt_ref,
                      *, tiles_per_seq):
    d = x_ref.shape[1]
    row = pl.program_id(0) // tiles_per_seq
    yx = yf_ref[...] + yr_ref[...]
    v = gate_ref[...].astype(F32) * yx
    out = jnp.dot(v.astype(BF16), wout_ref[...], preferred_element_type=F32)
    x1 = _with_pos_code(x_ref, er_ref, ec_ref, tiles_per_seq) + _mod_chunk(mod_ref, row, 2, d) * out
    x1_ref[...] = x1
    _ffn_pre(x1, mod_ref, row, g2_ref, rwt_ref, rb_ref, hx2_ref, hi_ref, lo_ref, hit_ref, lot_ref, gatet_ref,
             tab_ref, tabt_ref)


def _rglru_out(y_fwd, y_rev, gate, x, er, ec, mod, w_out, g2, rwt, rb, tiles_per_seq):
    n, d = x.shape
    c = gate.shape[1]
    row = lambda i: (i, 0)
    const = lambda i: (0, 0)
    out_specs, out_shape = _ffn_pre_specs(n, d)
    return pl.pallas_call(
        functools.partial(_rglru_out_kernel, tiles_per_seq=tiles_per_seq),
        grid=(n // MIX_TILE,),
        in_specs=[pl.BlockSpec((MIX_TILE, c), row),
                  pl.BlockSpec((MIX_TILE, c), row),
                  pl.BlockSpec((MIX_TILE, c), row),
                  pl.BlockSpec((MIX_TILE, d), row),
                  pl.BlockSpec(er.shape, const),
                  pl.BlockSpec(ec.shape, const),
                  pl.BlockSpec(mod.shape, const),
                  pl.BlockSpec(w_out.shape, const),
                  pl.BlockSpec((1, d), const),
                  pl.BlockSpec(rwt.shape, const),
                  pl.BlockSpec(rb.shape, const)],
        out_specs=out_specs,
        out_shape=out_shape,
        compiler_params=_params(("arbitrary",)),
        name="rglru_out",
    )(y_fwd, y_rev, gate, x, er, ec, mod, w_out, g2, rwt, rb)


def _sgu_kernel(x_ref, mod_ref, g_ref, win_ref, lng_ref, lnb_ref, ws_ref, bst_ref, wout_ref,
                g2_ref, rwt_ref, rb_ref,
                x1_ref, hx2_ref, hi_ref, lo_ref, hit_ref, lot_ref, gatet_ref, tab_ref, tabt_ref, m_scr, *,
                tiles_per_seq):
    t_rows, d = x_ref.shape
    w = wout_ref.shape[0]
    gd = w // SGU_HEADS
    row = pl.program_id(0) // tiles_per_seq
    x = x_ref[...]
    hx = _rms_mod(x, g_ref[...], _mod_chunk(mod_ref, row, 1, d), _mod_chunk(mod_ref, row, 0, d))
    z = jax.nn.gelu(jnp.dot(hx.astype(BF16), win_ref[...], preferred_element_type=F32))
    u = z[:, :w]
    v = z[:, w:]
    mu = jnp.mean(v, axis=-1, keepdims=True)
    vc = v - mu
    v = vc * lax.rsqrt(jnp.mean(vc * vc, axis=-1, keepdims=True) + NORM_EPS) * lng_ref[...] + lnb_ref[...]
    vb = v.astype(BF16)
    for ch in range(t_rows // CHUNK):
        rs = slice(ch * CHUNK, (ch + 1) * CHUNK)
        for g in range(SGU_HEADS):
            cs = slice(g * gd, (g + 1) * gd)
            sv = jnp.dot(ws_ref[g], vb[rs, cs], preferred_element_type=F32) + bst_ref[:, g:g + 1]
            m_scr[rs, cs] = (u[rs, cs] * sv).astype(BF16)
    out = jnp.dot(m_scr[...], wout_ref[...], preferred_element_type=F32)
    x1 = x + _mod_chunk(mod_ref, row, 2, d) * out
    x1_ref[...] = x1
    _ffn_pre(x1, mod_ref, row, g2_ref, rwt_ref, rb_ref, hx2_ref, hi_ref, lo_ref, hit_ref, lot_ref, gatet_ref,
             tab_ref, tabt_ref)


def _sgu(x, mod, g, w_in, ln_g, ln_b, w_s, b_st, w_out, g2, rwt, rb, tiles_per_seq):
    n, d = x.shape
    w = w_out.shape[0]
    const = lambda i: (0, 0)
    out_specs, out_shape = _ffn_pre_specs(n, d)
    return pl.pallas_call(
        functools.partial(_sgu_kernel, tiles_per_seq=tiles_per_seq),
        grid=(n // MIX_TILE,),
        in_specs=[pl.BlockSpec((MIX_TILE, d), lambda i: (i, 0)),
                  pl.BlockSpec(mod.shape, const),
                  pl.BlockSpec((1, d), const),
                  pl.BlockSpec(w_in.shape, const, pipeline_mode=pl.Buffered(1)),
                  pl.BlockSpec((1, w), const),
                  pl.BlockSpec((1, w), const),
                  pl.BlockSpec(w_s.shape, lambda i: (0, 0, 0)),
                  pl.BlockSpec(b_st.shape, const),
                  pl.BlockSpec(w_out.shape, const, pipeline_mode=pl.Buffered(1)),
                  pl.BlockSpec((1, d), const),
                  pl.BlockSpec(rwt.shape, const),
                  pl.BlockSpec(rb.shape, const)],
        out_specs=out_specs,
        out_shape=out_shape,
        scratch_shapes=[pltpu.VMEM((MIX_TILE, w), BF16)],
        compiler_params=_params(("arbitrary",)),
        name="sgu",
    )(x, mod, g, w_in, ln_g, ln_b, w_s, b_st, w_out, g2, rwt, rb)


def _sorted_layout(m, eidx, gw):
    e, t = m.shape
    lanes = LANES
    r = lax.broadcasted_iota(jnp.int32, (t, t), 0)
    c = lax.broadcasted_iota(jnp.int32, (t, t), 1)
    upper = jnp.where(r <= c, 1.0, 0.0).astype(BF16)
    incl = jnp.dot(m.astype(BF16), upper, preferred_element_type=F32)
    run = _round_up_rows(incl[:, t - 1:t])
    re = lax.broadcasted_iota(jnp.int32, (e, e), 0)
    ce = lax.broadcasted_iota(jnp.int32, (e, e), 1)
    lower = jnp.where(ce < re, 1.0, 0.0).astype(BF16)
    tiles = jnp.broadcast_to(run * (1.0 / ROW_ALIGN), (e, lanes)).astype(BF16)
    lstart = jnp.dot(lower, tiles, preferred_element_type=F32)[:, 0:1] * float(ROW_ALIGN)
    pos = jnp.where(m > 0.0, lstart + incl - m, float(POS_RADIX * POS_RADIX - 1))
    hi = jnp.floor(pos * (1.0 / POS_RADIX))
    lo = pos - hi * float(POS_RADIX)
    iota_e = lax.broadcasted_iota(jnp.int32, (e, t), 0).astype(F32)
    gate = jnp.zeros((e, t), F32)
    for k in range(TOP_K):
        gate = jnp.where(iota_e == eidx[k:k + 1, :], gw[k:k + 1, :], gate)
    lane = lax.broadcasted_iota(jnp.int32, (e, lanes), 1)
    tab = jnp.where(lane == 0, run, jnp.where(lane == 2, lstart, 0.0))
    diag = lax.broadcasted_iota(jnp.int32, (e, lanes), 0) == lane
    row_of = lambda col: jnp.sum(jnp.where(diag, jnp.broadcast_to(col, (e, lanes)), 0.0), axis=0, keepdims=True)
    sub = lax.broadcasted_iota(jnp.int32, (SUBLANES, lanes), 0)
    tabt = jnp.where(sub == 0, row_of(lstart), jnp.where(sub == 1, row_of(lstart + run), 0.0))
    return hi, lo, gate, tab, tabt


def _aligned(v):
    return pl.multiple_of(v, ROW_ALIGN)


def _run_copies(tab_ref, make_copy, unroll, base=0):
    def body(e, carry):
        make_copy(_aligned(tab_ref[base + TAB_LOCAL + e]), _aligned(tab_ref[base + N_EXPERTS + e]),
                  _aligned(tab_ref[base + e])).start()
        return carry

    lax.fori_loop(0, N_EXPERTS, body, 0, unroll=unroll)


def _dispatch_kernel(tab_ref, ztab_ref, tabt_ref, hi_ref, lo_ref, x_ref, xs_hbm, sbuf, zbuf, sems, zsem):
    i = pl.program_id(0)
    tiles = sbuf.shape[0]
    t = x_ref.shape[0] // tiles
    rows = sbuf.shape[1]

    def zero_copy(e):
        n = _aligned(ztab_ref[e])
        return pltpu.make_async_copy(zbuf.at[pl.ds(0, n)], xs_hbm.at[pl.ds(_aligned(ztab_ref[N_EXPERTS + e]), n)], zsem)

    def for_zero_runs(fn):
        def body(e, c):
            @pl.when(ztab_ref[e] > 0)
            def _():
                fn(zero_copy(e))
            return c
        lax.fori_loop(0, N_EXPERTS, body, 0)

        def tail(b, c):
            fn(pltpu.make_async_copy(zbuf, xs_hbm.at[pl.ds(pl.multiple_of(b * EXPERT_BLOCK, EXPERT_BLOCK),
                                                           EXPERT_BLOCK)], zsem))
            return c
        lax.fori_loop(ztab_ref[2 * N_EXPERTS], xs_hbm.shape[0] // EXPERT_BLOCK, tail, 0)

    @pl.when(i == 0)
    def _():
        zbuf[...] = jnp.zeros_like(zbuf)
        for_zero_runs(lambda cp: cp.start())

    def wait_rows(s, n):
        pltpu.make_async_copy(sbuf.at[s, pl.ds(0, n)], xs_hbm.at[pl.ds(0, n)], sems.at[s]).wait()

    def permute(s, r0):
        ts = slice(s * t, (s + 1) * t)
        j_e = (lax.broadcasted_iota(jnp.int32, (PERM_CHUNK, N_EXPERTS), 0) + r0).astype(F32)
        owner = jnp.where((j_e >= tabt_ref[s, 0:1, 0:N_EXPERTS]) & (j_e < tabt_ref[s, 1:2, 0:N_EXPERTS]),
                          1.0, 0.0).astype(BF16)
        pos = (jnp.dot(owner, hi_ref[:, ts], preferred_element_type=F32) * float(POS_RADIX)
               + jnp.dot(owner, lo_ref[:, ts], preferred_element_type=F32))
        j_t = (lax.broadcasted_iota(jnp.int32, (PERM_CHUNK, t), 0) + r0).astype(F32)
        p = jnp.where(pos == j_t, 1.0, 0.0).astype(BF16)
        sbuf[s, r0:r0 + PERM_CHUNK, :] = jnp.dot(p, x_ref[ts, :], preferred_element_type=F32).astype(BF16)

    for s in range(tiles):
        base = s * TAB_WIDTH

        @pl.when(i > 0)
        def _():
            wait_rows(s, _aligned(tab_ref[base + 2 * N_EXPERTS + 1]))

        for r0 in range(0, rows, PERM_CHUNK):
            if r0 < t * TOP_K + PERM_CHUNK:
                permute(s, r0)
            else:
                pl.when(tab_ref[base + 2 * N_EXPERTS] > r0)(functools.partial(permute, s, r0))

        _run_copies(tab_ref, lambda loc, glob, n: pltpu.make_async_copy(
            sbuf.at[s, pl.ds(loc, n)], xs_hbm.at[pl.ds(glob, n)], sems.at[s]), unroll=True, base=base)

    @pl.when(i == pl.num_programs(0) - 1)
    def _():
        for s in range(tiles):
            wait_rows(s, _aligned(tab_ref[s * TAB_WIDTH + 2 * N_EXPERTS]))
        for_zero_runs(lambda cp: cp.wait())


def _dispatch(tab, ztab, tab_t, pos_hi, pos_lo, hx2, xs_rows):
    n, d = hx2.shape
    col = lambda i: (0, i)
    return pl.pallas_call(
        _dispatch_kernel,
        grid=(n // (DISPATCH_TILES * TOKEN_TILE),),
        in_specs=[pl.BlockSpec((DISPATCH_TILES * TAB_WIDTH,), lambda i: (i,), memory_space=pltpu.SMEM),
                  pl.BlockSpec(memory_space=pltpu.SMEM),
                  pl.BlockSpec((DISPATCH_TILES, SUBLANES, LANES), lambda i: (i, 0, 0)),
                  pl.BlockSpec((N_EXPERTS, DISPATCH_TILES * TOKEN_TILE), col),
                  pl.BlockSpec((N_EXPERTS, DISPATCH_TILES * TOKEN_TILE), col),
                  pl.BlockSpec((DISPATCH_TILES * TOKEN_TILE, d), lambda i: (i, 0))],
        out_specs=pl.BlockSpec(memory_space=pl.ANY),
        out_shape=jax.ShapeDtypeStruct((xs_rows, d), BF16),
        scratch_shapes=[pltpu.VMEM((DISPATCH_TILES, SORTED_ROWS, d), BF16),
                        pltpu.VMEM((EXPERT_BLOCK, d), BF16),
                        pltpu.SemaphoreType.DMA((DISPATCH_TILES,)),
                        pltpu.SemaphoreType.DMA],
        compiler_params=_params(("arbitrary",)),
        name="moe_dispatch",
    )(tab, ztab, tab_t, pos_hi, pos_lo, hx2)


def _experts_kernel(be_ref, nb_ref, xs_ref, wgu_ref, wd_ref, ys_ref, wgu_b, wd_b):
    i = pl.program_id(0)
    ff = wd_b.shape[0]
    used = i < nb_ref[0]
    new_expert = (i == 0) | (be_ref[i] != be_ref[jnp.maximum(i - 1, 0)])

    @pl.when(used & new_expert)
    def _():
        wgu_b[...] = wgu_ref[0, 0].astype(BF16)
        wd_b[...] = wd_ref[0, 0].astype(BF16)

    @pl.when(used)
    def _():
        h = jnp.dot(xs_ref[...], wgu_b[...], preferred_element_type=F32)
        a = _silu(h[:, :ff]) * h[:, ff:]
        ys_ref[...] = jnp.dot(a.astype(BF16), wd_b[...], preferred_element_type=F32).astype(BF16)


def _experts(block_e, n_used, xs, w_gu, w_down, layer):
    rows, d = xs.shape
    nb = rows // EXPERT_BLOCK
    blk = lambda i, be, nu: (jnp.maximum(jnp.minimum(i, nu[0] - 1), 0), 0)
    wmap = lambda i, be, nu: (layer, be[i], 0, 0)
    return pl.pallas_call(
        _experts_kernel,
        grid_spec=pltpu.PrefetchScalarGridSpec(
            num_scalar_prefetch=2,
            grid=(nb,),
            in_specs=[pl.BlockSpec((EXPERT_BLOCK, d), blk),
                      pl.BlockSpec((1, 1) + w_gu.shape[2:], wmap),
                      pl.BlockSpec((1, 1) + w_down.shape[2:], wmap)],
            out_specs=pl.BlockSpec((EXPERT_BLOCK, d), blk),
            scratch_shapes=[pltpu.VMEM(w_gu.shape[2:], BF16), pltpu.VMEM(w_down.shape[2:], BF16)]),
        out_shape=jax.ShapeDtypeStruct((rows, d), BF16),
        input_output_aliases={2: 0},
        compiler_params=_params(("arbitrary",)),
        name="moe_experts",
    )(block_e, n_used, xs, w_gu, w_down)


def _combine_kernel(tab_ref, tabn_ref, tabv_ref, hi_ref, lo_ref, gate_ref, ys_hbm, hx2_ref, x1_ref, mod_ref,
                    wsgu_ref, wsd_ref, fg_ref, out_ref, ybuf, sems, *, tiles_per_seq, final_norm):
    t_rows, d = x1_ref.shape
    ff = wsd_ref.shape[0]
    rows = ybuf.shape[1]
    i = pl.program_id(0)
    last = pl.num_programs(0) - 1
    slot = i % 2
    row = i // tiles_per_seq

    def gather(table, s, unroll):
        _run_copies(table, lambda loc, glob, n: pltpu.make_async_copy(
            ys_hbm.at[pl.ds(glob, n)], ybuf.at[s, pl.ds(loc, n)], sems.at[s]), unroll=unroll)

    def wait_rows(table, s):
        n = _aligned(table[2 * N_EXPERTS])
        pltpu.make_async_copy(ys_hbm.at[pl.ds(0, n)], ybuf.at[s, pl.ds(0, n)], sems.at[s]).wait()

    @pl.when(i == 0)
    def _():
        ybuf[...] = jnp.zeros_like(ybuf)
        gather(tab_ref, 0, False)

    gather(tabn_ref, 1 - slot, True)

    hs = jnp.dot(hx2_ref[...], wsgu_ref[...], preferred_element_type=F32)
    shared = jnp.dot((_silu(hs[:, :ff]) * hs[:, ff:]).astype(BF16), wsd_ref[...], preferred_element_type=F32)

    run_lo = tabv_ref[0, :, 2:3]
    run_hi = run_lo + tabv_ref[0, :, 0:1]
    j_e = lax.broadcasted_iota(jnp.int32, (N_EXPERTS, rows), 1).astype(F32)
    owner = jnp.where((j_e >= run_lo) & (j_e < run_hi), 1.0, 0.0).astype(BF16)
    pos = (jnp.dot(hi_ref[...], owner, preferred_element_type=F32) * float(POS_RADIX)
           + jnp.dot(lo_ref[...], owner, preferred_element_type=F32))
    gates = jnp.dot(gate_ref[...], owner, preferred_element_type=F32)
    j_t = lax.broadcasted_iota(jnp.int32, (t_rows, rows), 1).astype(F32)
    gb = jnp.where(pos == j_t, gates, 0.0).astype(BF16)

    wait_rows(tab_ref, slot)
    routed = jnp.dot(gb, ybuf[slot], preferred_element_type=F32)
    x2 = x1_ref[...] + _mod_chunk(mod_ref, row, 5, d) * (routed + shared)
    if final_norm:
        x2 = x2 * lax.rsqrt(jnp.mean(x2 * x2, axis=-1, keepdims=True) + NORM_EPS) * fg_ref[...]
    out_ref[...] = x2

    @pl.when(i == last)
    def _():
        wait_rows(tabn_ref, 1 - slot)


def _combine(tab, tab_v, pos_hi, pos_lo, gate, ys, hx2, x1, mod, ws_gu, ws_down, fg, tiles_per_seq, final_norm):
    n, d = x1.shape
    row = lambda i: (i, 0)
    const = lambda i: (0, 0)
    return pl.pallas_call(
        functools.partial(_combine_kernel, tiles_per_seq=tiles_per_seq, final_norm=final_norm),
        grid=(n // TOKEN_TILE,),
        in_specs=[pl.BlockSpec((TAB_WIDTH,), lambda i: (i,), memory_space=pltpu.SMEM),
                  pl.BlockSpec((TAB_WIDTH,), lambda i: (jnp.minimum(i + 1, n // TOKEN_TILE - 1),),
                               memory_space=pltpu.SMEM),
                  pl.BlockSpec((1, N_EXPERTS, LANES), lambda i: (i, 0, 0)),
                  pl.BlockSpec((TOKEN_TILE, N_EXPERTS), row),
                  pl.BlockSpec((TOKEN_TILE, N_EXPERTS), row),
                  pl.BlockSpec((TOKEN_TILE, N_EXPERTS), row),
                  pl.BlockSpec(memory_space=pl.ANY),
                  pl.BlockSpec((TOKEN_TILE, d), row),
                  pl.BlockSpec((TOKEN_TILE, d), row),
                  pl.BlockSpec(mod.shape, const),
                  pl.BlockSpec(ws_gu.shape, const),
                  pl.BlockSpec(ws_down.shape, const),
                  pl.BlockSpec((1, d), const)],
        out_specs=pl.BlockSpec((TOKEN_TILE, d), row),
        out_shape=jax.ShapeDtypeStruct((n, d), F32),
        scratch_shapes=[pltpu.VMEM((2, SORTED_ROWS, d), BF16), pltpu.SemaphoreType.DMA((2,))],
        compiler_params=_params(("arbitrary",)),
        name="moe_combine",
    )(tab, tab, tab_v, pos_hi, pos_lo, gate, ys, hx2, x1, mod, ws_gu, ws_down, fg)


def _moe(x1, hx2, pos_hi, pos_lo, pos_hi_t, pos_lo_t, gate_t, tab_f, tab_t, mod, w_gu, w_down, layer, ws_gu,
         ws_down, fg, tiles_per_seq, final_norm):
    n, d = x1.shape
    n_tiles = n // TOKEN_TILE
    run_len = tab_f[:, :, 0].astype(jnp.int32)
    counts = jnp.sum(run_len, axis=0)
    padded = (counts + EXPERT_BLOCK - 1) // EXPERT_BLOCK * EXPERT_BLOCK
    pad_end = jnp.cumsum(padded)
    pad_start = pad_end - padded
    max_rows = n * TOP_K + n_tiles * N_EXPERTS * ROW_ALIGN + N_EXPERTS * (EXPERT_BLOCK - ROW_ALIGN)
    n_blocks = (max_rows + EXPERT_BLOCK - 1) // EXPERT_BLOCK
    n_used = (pad_end[-1:] // EXPERT_BLOCK).astype(jnp.int32)
    block_start = jnp.arange(n_blocks, dtype=jnp.int32) * EXPERT_BLOCK
    block_e = jnp.minimum(jnp.sum((pad_end[None, :] <= block_start[:, None]).astype(jnp.int32), axis=1),
                          N_EXPERTS - 1)

    run_start = pad_start[None, :] + jnp.cumsum(run_len, axis=0) - run_len
    total = jnp.sum(run_len, axis=1, keepdims=True)
    prev_total = jnp.concatenate([jnp.zeros((DISPATCH_TILES, 1), jnp.int32), total[:-DISPATCH_TILES]], axis=0)
    run_local = tab_f[:, :, 2].astype(jnp.int32)
    fill = jnp.zeros((n_tiles, TAB_WIDTH - 3 * N_EXPERTS - 2), jnp.int32)
    tab = jnp.concatenate([run_len, run_start, total, prev_total, run_local, fill], axis=1).reshape(-1)
    ztab = jnp.concatenate([padded - counts, pad_start + counts, n_used])

    xs = _dispatch(tab, ztab, tab_t, pos_hi, pos_lo, hx2, n_blocks * EXPERT_BLOCK)
    ys = _experts(block_e, n_used, xs, w_gu, w_down, layer)
    return _combine(tab, tab_f, pos_hi_t, pos_lo_t, gate_t, ys, hx2, x1, mod, ws_gu, ws_down, fg,
                    tiles_per_seq, final_norm)


def _sincos_tables(rows, d):
    quarter = d // 4
    omega = 1.0 / (POS_BASE ** (jnp.arange(quarter, dtype=F32) / quarter))

    def emb(n):
        p = jnp.arange(n, dtype=F32)[:, None] * omega[None, :]
        return jnp.concatenate([jnp.sin(p), jnp.cos(p)], axis=-1)

    return emb(rows), emb(GRID_W)


def kernel(x, c, ctx, c_ctx, ada_w, ada_b, mix_norm_g, ffn_norm_g, a_w_in, a_conv_w, a_conv_b, a_gate_r_w, a_gate_r_b, a_gate_i_w, a_gate_i_b, a_lambda, a_w_out, b_w_in, b_ln_g, b_ln_b, b_w_s, b_b_s, b_w_out, router_w, router_b, moe_w_gu, moe_w_down, shared_w_gu, shared_w_down, final_norm_g):
    bsz, s, d = x.shape
    ctx_len = ctx.shape[1]
    depth = ada_w.shape[0]
    assert depth == 2 and bsz < MOD_ROWS and s % MIX_TILE == 0 and MIX_TILE % TOKEN_TILE == 0
    assert s % SCAN_TILE == 0 and MIX_TILE % GRID_W == 0
    assert ctx_len % TOKEN_TILE == 0
    n = bsz * s
    tps = s // TOKEN_TILE
    ctx_row = bsz

    cc = jnp.zeros((MOD_ROWS, d), F32).at[:bsz].set(c).at[ctx_row].set(c_ctx)
    mod = _modulation(cc, ada_w, ada_b)
    er, ec = _sincos_tables(s // GRID_W, d)
    rc = a_w_in.shape[2] // 2

    w_in0 = a_w_in[0].astype(BF16)
    g_mix0 = mix_norm_g[0].reshape(1, d)
    x2 = x.reshape(n, d)
    gate, ux = _rglru_in(x2, er, ec, mod[0], g_mix0, w_in0)
    uc = _ctx_in(ctx.reshape(bsz * ctx_len, d), mod[0], g_mix0, w_in0[:, rc:], ctx_row)
    w_ri = jnp.concatenate([a_gate_r_w[0], a_gate_i_w[0]], axis=-1).astype(BF16)
    conv = (a_conv_w[0], a_conv_b[0])
    gates = [(w_ri[k], a_gate_r_b[0, k], a_gate_i_b[0, k], a_lambda[0, k]) for k in range(2)]
    h_zero = jnp.zeros((bsz, rc), F32)
    h_fwd, uc = _lru_scan(uc, conv, *gates[0], h_zero, reverse=False, reset_first=True, emit_y=False)
    h_rev = _lru_scan(uc, None, *gates[1], h_zero, reverse=True, reset_first=True, emit_y=False)
    y_fwd, ux = _lru_scan(ux, conv, *gates[0], h_fwd, reverse=False, reset_first=False, emit_y=True)
    y_rev = _lru_scan(ux, None, *gates[1], h_rev, reverse=True, reset_first=False, emit_y=True)
    pre = _rglru_out(y_fwd, y_rev, gate, x2, er, ec, mod[0], a_w_out[0].astype(BF16), ffn_norm_g[0].reshape(1, d),
                     router_w[0].T, router_b[0].reshape(N_EXPERTS, 1), s // MIX_TILE)
    x1 = _moe(*pre, mod[0], moe_w_gu, moe_w_down, 0, shared_w_gu[0].astype(BF16),
              shared_w_down[0].astype(BF16), final_norm_g.reshape(1, d), tps, False)

    pre = _sgu(x1, mod[1], mix_norm_g[1].reshape(1, d), b_w_in[0].astype(BF16),
               b_ln_g[0].reshape(1, -1), b_ln_b[0].reshape(1, -1), b_w_s[0].astype(BF16), b_b_s[0].T,
               b_w_out[0].astype(BF16), ffn_norm_g[1].reshape(1, d),
               router_w[1].T, router_b[1].reshape(N_EXPERTS, 1), s // MIX_TILE)
    out = _moe(*pre, mod[1], moe_w_gu, moe_w_down, 1, shared_w_gu[1].astype(BF16),
               shared_w_down[1].astype(BF16), final_norm_g.reshape(1, d), tps, True)
    return out.reshape(bsz, s, d)
```

```python
import functools

import jax
import jax.numpy as jnp
from jax import lax
from jax.experimental import pallas as pl
from jax.experimental.pallas import tpu as pltpu

F32 = jnp.float32
BF16 = jnp.bfloat16
HIGHEST = lax.Precision.HIGHEST

GRID_W = 64
NORM_EPS = 1e-6
POS_BASE = 10000.0
RNN_HEADS = 5
CONV_WIDTH = 4
CONV_PAD_LEFT = 2
LRU_C = 8.0
SGU_HEADS = 8
CHUNK = 128
N_EXPERTS = 64
TOP_K = 8
N_GROUPS = 8
TOPK_GROUPS = 4
EXPERTS_PER_GROUP = N_EXPERTS // N_GROUPS
ROUTED_SCALE = 2.5

SUBLANES = 8
ROW_ALIGN = 16
LANES = 128
MOD_ROWS = 8
TOKEN_TILE = 256
MIX_TILE = 512
SCAN_TILE = 512
EXPERT_BLOCK = 1024
SORTED_ROWS = TOKEN_TILE * TOP_K + N_EXPERTS * ROW_ALIGN
PERM_CHUNK = 512
DISPATCH_TILES = 2
TAB_WIDTH = 256
TAB_LOCAL = 2 * N_EXPERTS + 2
POS_RADIX = 64
VMEM_LIMIT = 56 * 1024 * 1024


def _params(semantics, vmem=VMEM_LIMIT):
    return pltpu.CompilerParams(dimension_semantics=semantics, vmem_limit_bytes=vmem)


def _silu(x):
    return x * jax.nn.sigmoid(x)


def _rms_mod(x, g, sc, sh):
    y = x * lax.rsqrt(jnp.mean(x * x, axis=-1, keepdims=True) + NORM_EPS)
    return (y * g) * (1.0 + sc) + sh


def _mod_chunk(mod_ref, row, k, d):
    return mod_ref[pl.ds(row, 1), k * d:(k + 1) * d]


def _dot_nt_3pass(a, b):
    dims = (((1,), (1,)), ((), ()))
    a_hi, b_hi = a.astype(BF16), b.astype(BF16)
    a_lo = (a - a_hi.astype(F32)).astype(BF16)
    b_lo = (b - b_hi.astype(F32)).astype(BF16)
    dot = functools.partial(lax.dot_general, dimension_numbers=dims, preferred_element_type=F32)
    return dot(a_hi, b_hi) + dot(a_hi, b_lo) + dot(a_lo, b_hi)


def _round_up_rows(count):
    return jnp.maximum(jnp.ceil(count * (1.0 / ROW_ALIGN)), 1.0) * float(ROW_ALIGN)


def _mod_kernel(cc_ref, w_ref, b_ref, o_ref):
    s = _silu(cc_ref[...])
    o_ref[0] = jnp.dot(s, w_ref[0], preferred_element_type=F32, precision=HIGHEST) + b_ref[0]


def _modulation(cc, ada_w, ada_b):
    depth, d, nd = ada_w.shape
    return pl.pallas_call(
        _mod_kernel,
        grid=(depth, nd // d),
        in_specs=[pl.BlockSpec((MOD_ROWS, d), lambda l, j: (0, 0)),
                  pl.BlockSpec((1, d, d), lambda l, j: (l, 0, j)),
                  pl.BlockSpec((1, 1, d), lambda l, j: (l, 0, j))],
        out_specs=pl.BlockSpec((1, MOD_ROWS, d), lambda l, j: (l, 0, j)),
        out_shape=jax.ShapeDtypeStruct((depth, MOD_ROWS, nd), F32),
        compiler_params=_params(("arbitrary", "arbitrary")),
        name="modulation",
    )(cc, ada_w, ada_b.reshape(depth, 1, nd))


def _with_pos_code(x_ref, er_ref, ec_ref, tiles_per_seq):
    t_rows = x_ref.shape[0]
    grid_w, half = ec_ref.shape
    rows = t_rows // grid_w
    r0 = pl.multiple_of((pl.program_id(0) % tiles_per_seq) * rows, rows)
    er = er_ref[pl.ds(r0, rows), :]
    pe = jnp.concatenate([jnp.broadcast_to(er[:, None, :], (rows, grid_w, half)).reshape(t_rows, half),
                          jnp.concatenate([ec_ref[...]] * rows, axis=0)], axis=1)
    return x_ref[...] + pe


def _rglru_in_kernel(x_ref, er_ref, ec_ref, mod_ref, g_ref, w_ref, gate_ref, u_ref, *, tiles_per_seq):
    d = x_ref.shape[1]
    c = u_ref.shape[1]
    row = pl.program_id(0) // tiles_per_seq
    x = _with_pos_code(x_ref, er_ref, ec_ref, tiles_per_seq)
    hx = _rms_mod(x, g_ref[...], _mod_chunk(mod_ref, row, 1, d), _mod_chunk(mod_ref, row, 0, d))
    z = jnp.dot(hx.astype(BF16), w_ref[...], preferred_element_type=F32)
    gate_ref[...] = jax.nn.gelu(z[:, :c]).astype(BF16)
    u_ref[...] = z[:, c:]


def _rglru_in(x2, er, ec, mod, g, w_in):
    n, d = x2.shape
    c = w_in.shape[1] // 2
    tps = er.shape[0] * ec.shape[0] // MIX_TILE
    row = lambda i: (i, 0)
    return pl.pallas_call(
        functools.partial(_rglru_in_kernel, tiles_per_seq=tps),
        grid=(n // MIX_TILE,),
        in_specs=[pl.BlockSpec((MIX_TILE, d), row),
                  pl.BlockSpec(er.shape, lambda i: (0, 0)),
                  pl.BlockSpec(ec.shape, lambda i: (0, 0)),
                  pl.BlockSpec(mod.shape, lambda i: (0, 0)),
                  pl.BlockSpec((1, d), lambda i: (0, 0)),
                  pl.BlockSpec(w_in.shape, lambda i: (0, 0))],
        out_specs=[pl.BlockSpec((MIX_TILE, c), row),
                   pl.BlockSpec((MIX_TILE, c), row)],
        out_shape=[jax.ShapeDtypeStruct((n, c), BF16),
                   jax.ShapeDtypeStruct((n, c), F32)],
        compiler_params=_params(("arbitrary",)),
        name="rglru_in",
    )(x2, er, ec, mod, g, w_in)


def _ctx_in_kernel(x_ref, mod_ref, g_ref, w_ref, u_ref, *, ctx_row):
    d = x_ref.shape[1]
    hx = _rms_mod(x_ref[...], g_ref[...], _mod_chunk(mod_ref, ctx_row, 1, d),
                  _mod_chunk(mod_ref, ctx_row, 0, d))
    u_ref[...] = jnp.dot(hx.astype(BF16), w_ref[...], preferred_element_type=F32)


def _ctx_in(c2, mod, g, w_u, ctx_row):
    n, d = c2.shape
    c = w_u.shape[1]
    return pl.pallas_call(
        functools.partial(_ctx_in_kernel, ctx_row=ctx_row),
        grid=(n // TOKEN_TILE,),
        in_specs=[pl.BlockSpec((TOKEN_TILE, d), lambda i: (i, 0)),
                  pl.BlockSpec(mod.shape, lambda i: (0, 0)),
                  pl.BlockSpec((1, d), lambda i: (0, 0)),
                  pl.BlockSpec(w_u.shape, lambda i: (0, 0))],
        out_specs=pl.BlockSpec((TOKEN_TILE, c), lambda i: (i, 0)),
        out_shape=jax.ShapeDtypeStruct((n, c), F32),
        compiler_params=_params(("arbitrary",)),
        name="ctx_in",
    )(c2, mod, g, w_u)


def _log_sigmoid(x):
    return jnp.minimum(x, 0.0) - jnp.log1p(jnp.exp(-jnp.abs(x)))


def _lru_scan_kernel(*refs, n_tiles, reverse, conv_done, reset_first, emit_y):
    if conv_done:
        u_ref, wri_ref, rb_ref, ib_ref, lam_ref, h0_ref, out_ref, a_scr, b_scr, h_scr = refs
    else:
        (u_ref, up_ref, un_ref, cw_ref, cb_ref, wri_ref, rb_ref, ib_ref, lam_ref, h0_ref,
         out_ref, uc_ref, ubuf, a_scr, b_scr, h_scr) = refs
    t_rows, c = u_ref.shape
    hb = c // RNN_HEADS
    b = pl.program_id(0)
    j = pl.program_id(1)
    jj = n_tiles - 1 - j if reverse else j

    if conv_done:
        u = u_ref[...]
    else:
        ubuf[SUBLANES:SUBLANES + t_rows, :] = u_ref[...]
        ubuf[0:SUBLANES, :] = jnp.where(jj == 0, 0.0, up_ref[...])
        ubuf[SUBLANES + t_rows:, :] = jnp.where(jj == n_tiles - 1, 0.0, un_ref[...])
        u = cb_ref[...]
        full = ubuf[...]
        for k in range(CONV_WIDTH):
            shift = (CONV_PAD_LEFT - k) % full.shape[0]
            tap = full if shift == 0 else pltpu.roll(full, shift, 0)
            u = u + cw_ref[k:k + 1, :] * tap[SUBLANES:SUBLANES + t_rows, :]
        uc_ref[...] = u

    log_lam = LRU_C * _log_sigmoid(lam_ref[...])
    rows = lax.broadcasted_iota(jnp.int32, (t_rows, 1), 0)
    first_row = jnp.where(j == 0, t_rows - 1 if reverse else 0, -1)
    for h in range(RNN_HEADS):
        sl = slice(h * hb, (h + 1) * hb)
        uh = u[:, sl]
        z = jnp.dot(uh.astype(BF16), wri_ref[h], preferred_element_type=F32)
        r = jax.nn.sigmoid(z[:, :hb] + rb_ref[:, sl])
        ig = jax.nn.sigmoid(z[:, hb:] + ib_ref[:, sl])
        log_a = r * log_lam[:, sl]
        a = jnp.exp(log_a)
        mult = jnp.sqrt((1.0 - a) * (1.0 + a))
        if reset_first:
            mult = jnp.where(rows == first_row, 1.0, mult)
        a_scr[:, sl] = a
        b_scr[:, sl] = mult * ig * uh

    @pl.when(j == 0)
    def _():
        h_scr[...] = h0_ref[pl.ds(b, 1), :]

    n_groups = t_rows // SUBLANES

    def group(g, h):
        base = pl.multiple_of((n_groups - 1 - g if reverse else g) * SUBLANES, SUBLANES)
        for s in range(SUBLANES):
            r = base + (SUBLANES - 1 - s if reverse else s)
            h = a_scr[pl.ds(r, 1), :] * h + b_scr[pl.ds(r, 1), :]
            if emit_y:
                out_ref[pl.ds(r, 1), :] = h
        return h

    h = lax.fori_loop(0, n_groups, group, h_scr[...])
    h_scr[...] = h
    if not emit_y:
        @pl.when(j == n_tiles - 1)
        def _():
            out_ref[pl.ds(b, 1), :] = h


def _lru_scan(u, conv, w_ri, r_b, i_b, lam, h0, *, reverse, reset_first, emit_y):
    n, c = u.shape
    n_batch = h0.shape[0]
    rows = min(SCAN_TILE, n // n_batch)
    n_tiles = n // n_batch // rows
    sub = rows // SUBLANES
    n_sub = n // SUBLANES
    const = lambda b, j: (0, 0)

    def tile(b, j):
        return b * n_tiles + (n_tiles - 1 - j if reverse else j)

    tile_spec = pl.BlockSpec((rows, c), lambda b, j: (tile(b, j), 0))
    in_specs = [tile_spec]
    args = [u]
    scratch = []
    if conv is not None:
        conv_w, conv_b = conv
        in_specs += [pl.BlockSpec((SUBLANES, c), lambda b, j: (jnp.maximum(tile(b, j) * sub - 1, 0), 0)),
                     pl.BlockSpec((SUBLANES, c), lambda b, j: (jnp.minimum((tile(b, j) + 1) * sub, n_sub - 1), 0)),
                     pl.BlockSpec(conv_w.shape, const),
                     pl.BlockSpec((1, c), const)]
        args += [u, u, conv_w, conv_b.reshape(1, c)]
        scratch = [pltpu.VMEM((rows + 2 * SUBLANES, c), F32)]
    in_specs += [pl.BlockSpec(w_ri.shape, lambda b, j: (0, 0, 0)),
                 pl.BlockSpec((1, c), const), pl.BlockSpec((1, c), const), pl.BlockSpec((1, c), const),
                 pl.BlockSpec(h0.shape, const)]
    args += [w_ri, r_b.reshape(1, c), i_b.reshape(1, c), lam.reshape(1, c), h0]
    if emit_y:
        out_specs = [tile_spec]
        out_shape = [jax.ShapeDtypeStruct((n, c), F32)]
    else:
        out_specs = [pl.BlockSpec(h0.shape, const)]
        out_shape = [jax.ShapeDtypeStruct(h0.shape, F32)]
    if conv is not None:
        out_specs.append(tile_spec)
        out_shape.append(jax.ShapeDtypeStruct((n, c), F32))
    name = ("lru_scan" if emit_y else "lru_ctx") + ("_rev" if reverse else "_fwd")
    outs = pl.pallas_call(
        functools.partial(_lru_scan_kernel, n_tiles=n_tiles, reverse=reverse, conv_done=conv is None,
                          reset_first=reset_first, emit_y=emit_y),
        grid=(n_batch, n_tiles),
        in_specs=in_specs,
        out_specs=out_specs,
        out_shape=out_shape,
        scratch_shapes=scratch + [pltpu.VMEM((rows, c), F32),
                                  pltpu.VMEM((rows, c), F32),
                                  pltpu.VMEM((1, c), F32)],
        compiler_params=_params(("arbitrary", "arbitrary")),
        name=name,
    )(*args)
    return outs if conv is not None else outs[0]


def _route(logits, rb):
    e, t = logits.shape
    neg = -jnp.inf
    scores = jax.nn.sigmoid(logits)
    sel = scores + rb
    iota_g = lax.broadcasted_iota(jnp.int32, (N_GROUPS, t), 0).astype(F32)
    iota_e = lax.broadcasted_iota(jnp.int32, (e, t), 0).astype(F32)

    gs = jnp.full((N_GROUPS, t), neg, F32)
    for g in range(N_GROUPS):
        sg = sel[g * EXPERTS_PER_GROUP:(g + 1) * EXPERTS_PER_GROUP, :]
        m1 = jnp.max(sg, axis=0, keepdims=True)
        i1 = jnp.min(jnp.where(sg == m1, iota_g, float(EXPERTS_PER_GROUP)), axis=0, keepdims=True)
        m2 = jnp.max(jnp.where(iota_g == i1, neg, sg), axis=0, keepdims=True)
        gs = jnp.where(iota_g == float(g), m1 + m2, gs)

    keep = jnp.zeros((N_GROUPS, t), F32)
    for _ in range(TOPK_GROUPS):
        m = jnp.max(gs, axis=0, keepdims=True)
        idx = jnp.min(jnp.where(gs == m, iota_g, float(N_GROUPS)), axis=0, keepdims=True)
        hit = iota_g == idx
        keep = jnp.where(hit, 1.0, keep)
        gs = jnp.where(hit, neg, gs)

    masked = jnp.concatenate(
        [jnp.where(keep[g:g + 1, :] > 0.0, sel[g * EXPERTS_PER_GROUP:(g + 1) * EXPERTS_PER_GROUP, :], neg)
         for g in range(N_GROUPS)], axis=0)

    iota_k = lax.broadcasted_iota(jnp.int32, (TOP_K, t), 0)
    selmask = jnp.zeros((e, t), F32)
    eidx = jnp.zeros((TOP_K, t), F32)
    gw = jnp.zeros((TOP_K, t), F32)
    for k in range(TOP_K):
        m = jnp.max(masked, axis=0, keepdims=True)
        idx = jnp.min(jnp.where(masked == m, iota_e, float(e)), axis=0, keepdims=True)
        hit = iota_e == idx
        gk = jnp.sum(jnp.where(hit, scores, 0.0), axis=0, keepdims=True)
        masked = jnp.where(hit, neg, masked)
        selmask = jnp.where(hit, 1.0, selmask)
        eidx = jnp.where(iota_k == k, idx, eidx)
        gw = jnp.where(iota_k == k, gk, gw)
    gw = gw / jnp.sum(gw, axis=0, keepdims=True) * ROUTED_SCALE
    return eidx, gw, selmask


def _ffn_pre(x1, mod_ref, row, g2_ref, rwt_ref, rb_ref, hx2_ref, hi_ref, lo_ref, hit_ref, lot_ref, gatet_ref,
             tab_ref, tabt_ref):
    d = x1.shape[1]
    hx2 = _rms_mod(x1, g2_ref[...], _mod_chunk(mod_ref, row, 4, d), _mod_chunk(mod_ref, row, 3, d))
    hx2_ref[...] = hx2.astype(BF16)
    logits = _dot_nt_3pass(rwt_ref[...], hx2)
    eidx, gw, selmask = _route(logits, rb_ref[...])
    for s in range(selmask.shape[1] // TOKEN_TILE):
        ts = slice(s * TOKEN_TILE, (s + 1) * TOKEN_TILE)
        hi, lo, gate, tab, tabt = _sorted_layout(selmask[:, ts], eidx[:, ts], gw[:, ts])
        hi_ref[:, ts] = hi.astype(BF16)
        lo_ref[:, ts] = lo.astype(BF16)
        hit_ref[ts, :] = hi.T.astype(BF16)
        lot_ref[ts, :] = lo.T.astype(BF16)
        gatet_ref[ts, :] = gate.T.astype(BF16)
        tab_ref[s] = tab
        tabt_ref[s] = tabt


def _ffn_pre_specs(n, d):
    row = lambda i: (i, 0)
    col = lambda i: (0, i)
    sub = MIX_TILE // TOKEN_TILE
    lead = lambda i: (i, 0, 0)
    out_specs = [pl.BlockSpec((MIX_TILE, d), row),
                 pl.BlockSpec((MIX_TILE, d), row),
                 pl.BlockSpec((N_EXPERTS, MIX_TILE), col),
                 pl.BlockSpec((N_EXPERTS, MIX_TILE), col),
                 pl.BlockSpec((MIX_TILE, N_EXPERTS), row),
                 pl.BlockSpec((MIX_TILE, N_EXPERTS), row),
                 pl.BlockSpec((MIX_TILE, N_EXPERTS), row),
                 pl.BlockSpec((sub, N_EXPERTS, LANES), lead),
                 pl.BlockSpec((sub, SUBLANES, LANES), lead)]
    out_shape = [jax.ShapeDtypeStruct((n, d), F32),
                 jax.ShapeDtypeStruct((n, d), BF16),
                 jax.ShapeDtypeStruct((N_EXPERTS, n), BF16),
                 jax.ShapeDtypeStruct((N_EXPERTS, n), BF16),
                 jax.ShapeDtypeStruct((n, N_EXPERTS), BF16),
                 jax.ShapeDtypeStruct((n, N_EXPERTS), BF16),
                 jax.ShapeDtypeStruct((n, N_EXPERTS), BF16),
                 jax.ShapeDtypeStruct((n // TOKEN_TILE, N_EXPERTS, LANES), F32),
                 jax.ShapeDtypeStruct((n // TOKEN_TILE, SUBLANES, LANES), F32)]
    return out_specs, out_shape


def _rglru_out_kernel(yf_ref, yr_ref, gate_ref, x_ref, er_ref, ec_ref, mod_ref, wout_ref, g2_ref, rwt_ref,
                      rb_ref, x1_ref, hx2_ref, hi_ref, lo_ref, hit_ref, lot_ref, gatet_ref, tab_ref, tabt_ref,
                      *, tiles_per_seq):
    d = x_ref.shape[1]
    row = pl.program_id(0) // tiles_per_seq
    yx = yf_ref[...] + yr_ref[...]
    v = gate_ref[...].astype(F32) * yx
    out = jnp.dot(v.astype(BF16), wout_ref[...], preferred_element_type=F32)
    x1 = _with_pos_code(x_ref, er_ref, ec_ref, tiles_per_seq) + _mod_chunk(mod_ref, row, 2, d) * out
    x1_ref[...] = x1
    _ffn_pre(x1, mod_ref, row, g2_ref, rwt_ref, rb_ref, hx2_ref, hi_ref, lo_ref, hit_ref, lot_ref, gatet_ref,
             tab_ref, tabt_ref)


def _rglru_out(y_fwd, y_rev, gate, x, er, ec, mod, w_out, g2, rwt, rb, tiles_per_seq):
    n, d = x.shape
    c = gate.shape[1]
    row = lambda i: (i, 0)
    const = lambda i: (0, 0)
    out_specs, out_shape = _ffn_pre_specs(n, d)
    return pl.pallas_call(
        functools.partial(_rglru_out_kernel, tiles_per_seq=tiles_per_seq),
        grid=(n // MIX_TILE,),
        in_specs=[pl.BlockSpec((MIX_TILE, c), row),
                  pl.BlockSpec((MIX_TILE, c), row),
                  pl.BlockSpec((MIX_TILE, c), row),
                  pl.BlockSpec((MIX_TILE, d), row),
                  pl.BlockSpec(er.shape, const),
                  pl.BlockSpec(ec.shape, const),
                  pl.BlockSpec(mod.shape, const),
                  pl.BlockSpec(w_out.shape, const),
                  pl.BlockSpec((1, d), const),
                  pl.BlockSpec(rwt.shape, const),
                  pl.BlockSpec(rb.shape, const)],
        out_specs=out_specs,
        out_shape=out_shape,
        compiler_params=_params(("arbitrary",)),
        name="rglru_out",
    )(y_fwd, y_rev, gate, x, er, ec, mod, w_out, g2, rwt, rb)


def _sgu_kernel(x_ref, mod_ref, g_ref, win_ref, lng_ref, lnb_ref, ws_ref, bst_ref, wout_ref,
                g2_ref, rwt_ref, rb_ref,
                x1_ref, hx2_ref, hi_ref, lo_ref, hit_ref, lot_ref, gatet_ref, tab_ref, tabt_ref, m_scr, *,
                tiles_per_seq):
    t_rows, d = x_ref.shape
    w = wout_ref.shape[0]
    gd = w // SGU_HEADS
    row = pl.program_id(0) // tiles_per_seq
    x = x_ref[...]
    hx = _rms_mod(x, g_ref[...], _mod_chunk(mod_ref, row, 1, d), _mod_chunk(mod_ref, row, 0, d))
    z = jax.nn.gelu(jnp.dot(hx.astype(BF16), win_ref[...], preferred_element_type=F32))
    u = z[:, :w]
    v = z[:, w:]
    mu = jnp.mean(v, axis=-1, keepdims=True)
    vc = v - mu
    v = vc * lax.rsqrt(jnp.mean(vc * vc, axis=-1, keepdims=True) + NORM_EPS) * lng_ref[...] + lnb_ref[...]
    vb = v.astype(BF16)
    for ch in range(t_rows // CHUNK):
        rs = slice(ch * CHUNK, (ch + 1) * CHUNK)
        for g in range(SGU_HEADS):
            cs = slice(g * gd, (g + 1) * gd)
            sv = jnp.dot(ws_ref[g], vb[rs, cs], preferred_element_type=F32) + bst_ref[:, g:g + 1]
            m_scr[rs, cs] = (u[rs, cs] * sv).astype(BF16)
    out = jnp.dot(m_scr[...], wout_ref[...], preferred_element_type=F32)
    x1 = x + _mod_chunk(mod_ref, row, 2, d) * out
    x1_ref[...] = x1
    _ffn_pre(x1, mod_ref, row, g2_ref, rwt_ref, rb_ref, hx2_ref, hi_ref, lo_ref, hit_ref, lot_ref, gatet_ref,
             tab_ref, tabt_ref)


def _sgu(x, mod, g, w_in, ln_g, ln_b, w_s, b_st, w_out, g2, rwt, rb, tiles_per_seq):
    n, d = x.shape
    w = w_out.shape[0]
    const = lambda i: (0, 0)
    out_specs, out_shape = _ffn_pre_specs(n, d)
    return pl.pallas_call(
        functools.partial(_sgu_kernel, tiles_per_seq=tiles_per_seq),
        grid=(n // MIX_TILE,),
        in_specs=[pl.BlockSpec((MIX_TILE, d), lambda i: (i, 0)),
                  pl.BlockSpec(mod.shape, const),
                  pl.BlockSpec((1, d), const),
                  pl.BlockSpec(w_in.shape, const, pipeline_mode=pl.Buffered(1)),
                  pl.BlockSpec((1, w), const),
                  pl.BlockSpec((1, w), const),
                  pl.BlockSpec(w_s.shape, lambda i: (0, 0, 0)),
                  pl.BlockSpec(b_st.shape, const),
                  pl.BlockSpec(w_out.shape, const, pipeline_mode=pl.Buffered(1)),
                  pl.BlockSpec((1, d), const),
                  pl.BlockSpec(rwt.shape, const),
                  pl.BlockSpec(rb.shape, const)],
        out_specs=out_specs,
        out_shape=out_shape,
        scratch_shapes=[pltpu.VMEM((MIX_TILE, w), BF16)],
        compiler_params=_params(("arbitrary",)),
        name="sgu",
    )(x, mod, g, w_in, ln_g, ln_b, w_s, b_st, w_out, g2, rwt, rb)


def _sorted_layout(m, eidx, gw):
    e, t = m.shape
    lanes = LANES
    r = lax.broadcasted_iota(jnp.int32, (t, t), 0)
    c = lax.broadcasted_iota(jnp.int32, (t, t), 1)
    upper = jnp.where(r <= c, 1.0, 0.0).astype(BF16)
    incl = jnp.dot(m.astype(BF16), upper, preferred_element_type=F32)
    run = _round_up_rows(incl[:, t - 1:t])
    re = lax.broadcasted_iota(jnp.int32, (e, e), 0)
    ce = lax.broadcasted_iota(jnp.int32, (e, e), 1)
    lower = jnp.where(ce < re, 1.0, 0.0).astype(BF16)
    tiles = jnp.broadcast_to(run * (1.0 / ROW_ALIGN), (e, lanes)).astype(BF16)
    lstart = jnp.dot(lower, tiles, preferred_element_type=F32)[:, 0:1] * float(ROW_ALIGN)
    pos = jnp.where(m > 0.0, lstart + incl - m, float(POS_RADIX * POS_RADIX - 1))
    hi = jnp.floor(pos * (1.0 / POS_RADIX))
    lo = pos - hi * float(POS_RADIX)
    iota_e = lax.broadcasted_iota(jnp.int32, (e, t), 0).astype(F32)
    gate = jnp.zeros((e, t), F32)
    for k in range(TOP_K):
        gate = jnp.where(iota_e == eidx[k:k + 1, :], gw[k:k + 1, :], gate)
    lane = lax.broadcasted_iota(jnp.int32, (e, lanes), 1)
    tab = jnp.where(lane == 0, run, jnp.where(lane == 2, lstart, 0.0))
    diag = lax.broadcasted_iota(jnp.int32, (e, lanes), 0) == lane
    row_of = lambda col: jnp.sum(jnp.where(diag, jnp.broadcast_to(col, (e, lanes)), 0.0), axis=0, keepdims=True)
    sub = lax.broadcasted_iota(jnp.int32, (SUBLANES, lanes), 0)
    tabt = jnp.where(sub == 0, row_of(lstart), jnp.where(sub == 1, row_of(lstart + run), 0.0))
    return hi, lo, gate, tab, tabt


def _aligned(v):
    return pl.multiple_of(v, ROW_ALIGN)


def _run_copies(tab_ref, make_copy, unroll, base=0):
    def body(e, carry):
        make_copy(_aligned(tab_ref[base + TAB_LOCAL + e]), _aligned(tab_ref[base + N_EXPERTS + e]),
                  _aligned(tab_ref[base + e])).start()
        return carry

    lax.fori_loop(0, N_EXPERTS, body, 0, unroll=unroll)


def _dispatch_kernel(tab_ref, ztab_ref, tabt_ref, hi_ref, lo_ref, x_ref, xs_hbm, sbuf, zbuf, sems, zsem):
    i = pl.program_id(0)
    tiles = sbuf.shape[0]
    t = x_ref.shape[0] // tiles
    rows = sbuf.shape[1]

    def zero_copy(e):
        n = _aligned(ztab_ref[e])
        return pltpu.make_async_copy(zbuf.at[pl.ds(0, n)], xs_hbm.at[pl.ds(_aligned(ztab_ref[N_EXPERTS + e]), n)], zsem)

    def for_zero_runs(fn):
        def body(e, c):
            @pl.when(ztab_ref[e] > 0)
            def _():
                fn(zero_copy(e))
            return c
        lax.fori_loop(0, N_EXPERTS, body, 0)

        def tail(b, c):
            fn(pltpu.make_async_copy(zbuf, xs_hbm.at[pl.ds(pl.multiple_of(b * EXPERT_BLOCK, EXPERT_BLOCK),
                                                           EXPERT_BLOCK)], zsem))
            return c
        lax.fori_loop(ztab_ref[2 * N_EXPERTS], xs_hbm.shape[0] // EXPERT_BLOCK, tail, 0)

    @pl.when(i == 0)
    def _():
        zbuf[...] = jnp.zeros_like(zbuf)
        for_zero_runs(lambda cp: cp.start())

    def wait_rows(s, n):
        pltpu.make_async_copy(sbuf.at[s, pl.ds(0, n)], xs_hbm.at[pl.ds(0, n)], sems.at[s]).wait()

    def permute(s, r0):
        ts = slice(s * t, (s + 1) * t)
        j_e = (lax.broadcasted_iota(jnp.int32, (PERM_CHUNK, N_EXPERTS), 0) + r0).astype(F32)
        owner = jnp.where((j_e >= tabt_ref[s, 0:1, 0:N_EXPERTS]) & (j_e < tabt_ref[s, 1:2, 0:N_EXPERTS]),
                          1.0, 0.0).astype(BF16)
        pos = (jnp.dot(owner, hi_ref[:, ts], preferred_element_type=F32) * float(POS_RADIX)
               + jnp.dot(owner, lo_ref[:, ts], preferred_element_type=F32))
        j_t = (lax.broadcasted_iota(jnp.int32, (PERM_CHUNK, t), 0) + r0).astype(F32)
        p = jnp.where(pos == j_t, 1.0, 0.0).astype(BF16)
        sbuf[s, r0:r0 + PERM_CHUNK, :] = jnp.dot(p, x_ref[ts, :], preferred_element_type=F32).astype(BF16)

    for s in range(tiles):
        base = s * TAB_WIDTH

        @pl.when(i > 0)
        def _():
            wait_rows(s, _aligned(tab_ref[base + 2 * N_EXPERTS + 1]))

        for r0 in range(0, rows, PERM_CHUNK):
            if r0 < t * TOP_K + PERM_CHUNK:
                permute(s, r0)
            else:
                pl.when(tab_ref[base + 2 * N_EXPERTS] > r0)(functools.partial(permute, s, r0))

        _run_copies(tab_ref, lambda loc, glob, n: pltpu.make_async_copy(
            sbuf.at[s, pl.ds(loc, n)], xs_hbm.at[pl.ds(glob, n)], sems.at[s]), unroll=True, base=base)

    @pl.when(i == pl.num_programs(0) - 1)
    def _():
        for s in range(tiles):
            wait_rows(s, _aligned(tab_ref[s * TAB_WIDTH + 2 * N_EXPERTS]))

    @pl.when(i == 0)
    def _():
        for_zero_runs(lambda cp: cp.wait())


def _dispatch(tab, ztab, tab_t, pos_hi, pos_lo, hx2, xs_rows):
    n, d = hx2.shape
    col = lambda i: (0, i)
    return pl.pallas_call(
        _dispatch_kernel,
        grid=(n // (DISPATCH_TILES * TOKEN_TILE),),
        in_specs=[pl.BlockSpec((DISPATCH_TILES * TAB_WIDTH,), lambda i: (i,), memory_space=pltpu.SMEM),
                  pl.BlockSpec(memory_space=pltpu.SMEM),
                  pl.BlockSpec((DISPATCH_TILES, SUBLANES, LANES), lambda i: (i, 0, 0)),
                  pl.BlockSpec((N_EXPERTS, DISPATCH_TILES * TOKEN_TILE), col),
                  pl.BlockSpec((N_EXPERTS, DISPATCH_TILES * TOKEN_TILE), col),
                  pl.BlockSpec((DISPATCH_TILES * TOKEN_TILE, d), lambda i: (i, 0))],
        out_specs=pl.BlockSpec(memory_space=pl.ANY),
        out_shape=jax.ShapeDtypeStruct((xs_rows, d), BF16),
        scratch_shapes=[pltpu.VMEM((DISPATCH_TILES, SORTED_ROWS, d), BF16),
                        pltpu.VMEM((EXPERT_BLOCK, d), BF16),
                        pltpu.SemaphoreType.DMA((DISPATCH_TILES,)),
                        pltpu.SemaphoreType.DMA],
        compiler_params=_params(("arbitrary",)),
        name="moe_dispatch",
    )(tab, ztab, tab_t, pos_hi, pos_lo, hx2)


def _experts_kernel(be_ref, nb_ref, xs_ref, wgu_ref, wd_ref, ys_ref, wgu_b, wd_b):
    i = pl.program_id(0)
    ff = wd_b.shape[0]
    used = i < nb_ref[0]
    new_expert = (i == 0) | (be_ref[i] != be_ref[jnp.maximum(i - 1, 0)])

    @pl.when(used & new_expert)
    def _():
        wgu_b[...] = wgu_ref[0, 0].astype(BF16)
        wd_b[...] = wd_ref[0, 0].astype(BF16)

    @pl.when(used)
    def _():
        h = jnp.dot(xs_ref[...], wgu_b[...], preferred_element_type=F32)
        a = _silu(h[:, :ff]) * h[:, ff:]
        ys_ref[...] = jnp.dot(a.astype(BF16), wd_b[...], preferred_element_type=F32).astype(BF16)


def _experts(block_e, n_used, xs, w_gu, w_down, layer):
    rows, d = xs.shape
    nb = rows // EXPERT_BLOCK
    blk = lambda i, be, nu: (jnp.maximum(jnp.minimum(i, nu[0] - 1), 0), 0)
    wmap = lambda i, be, nu: (layer, be[i], 0, 0)
    return pl.pallas_call(
        _experts_kernel,
        grid_spec=pltpu.PrefetchScalarGridSpec(
            num_scalar_prefetch=2,
            grid=(jnp.minimum(n_used[0], nb),),
            in_specs=[pl.BlockSpec((EXPERT_BLOCK, d), blk),
                      pl.BlockSpec((1, 1) + w_gu.shape[2:], wmap),
                      pl.BlockSpec((1, 1) + w_down.shape[2:], wmap)],
            out_specs=pl.BlockSpec((EXPERT_BLOCK, d), blk),
            scratch_shapes=[pltpu.VMEM(w_gu.shape[2:], BF16), pltpu.VMEM(w_down.shape[2:], BF16)]),
        out_shape=jax.ShapeDtypeStruct((rows, d), BF16),
        input_output_aliases={2: 0},
        compiler_params=_params(("arbitrary",)),
        name="moe_experts",
    )(block_e, n_used, xs, w_gu, w_down)


def _combine_kernel(tab_ref, tabn_ref, tabv_ref, hi_ref, lo_ref, gate_ref, ys_hbm, hx2_ref, x1_ref, mod_ref,
                    wsgu_ref, wsd_ref, fg_ref, out_ref, ybuf, sems, *, tiles_per_seq, final_norm):
    t_rows, d = x1_ref.shape
    ff = wsd_ref.shape[0]
    rows = ybuf.shape[1]
    i = pl.program_id(0)
    last = pl.num_programs(0) - 1
    slot = i % 2
    row = i // tiles_per_seq

    def gather(table, s, unroll):
        _run_copies(table, lambda loc, glob, n: pltpu.make_async_copy(
            ys_hbm.at[pl.ds(glob, n)], ybuf.at[s, pl.ds(loc, n)], sems.at[s]), unroll=unroll)

    def wait_rows(table, s):
        n = _aligned(table[2 * N_EXPERTS])
        pltpu.make_async_copy(ys_hbm.at[pl.ds(0, n)], ybuf.at[s, pl.ds(0, n)], sems.at[s]).wait()

    @pl.when(i == 0)
    def _():
        ybuf[...] = jnp.zeros_like(ybuf)
        gather(tab_ref, 0, False)

    gather(tabn_ref, 1 - slot, True)

    hs = jnp.dot(hx2_ref[...], wsgu_ref[...], preferred_element_type=F32)
    shared = jnp.dot((_silu(hs[:, :ff]) * hs[:, ff:]).astype(BF16), wsd_ref[...], preferred_element_type=F32)

    run_lo = tabv_ref[0, :, 2:3]
    run_hi = run_lo + tabv_ref[0, :, 0:1]
    j_e = lax.broadcasted_iota(jnp.int32, (N_EXPERTS, rows), 1).astype(F32)
    owner = jnp.where((j_e >= run_lo) & (j_e < run_hi), 1.0, 0.0).astype(BF16)
    pos = (jnp.dot(hi_ref[...], owner, preferred_element_type=F32) * float(POS_RADIX)
           + jnp.dot(lo_ref[...], owner, preferred_element_type=F32))
    gates = jnp.dot(gate_ref[...], owner, preferred_element_type=F32)
    j_t = lax.broadcasted_iota(jnp.int32, (t_rows, rows), 1).astype(F32)
    gb = jnp.where(pos == j_t, gates, 0.0).astype(BF16)

    wait_rows(tab_ref, slot)
    routed = jnp.dot(gb, ybuf[slot], preferred_element_type=F32)
    x2 = x1_ref[...] + _mod_chunk(mod_ref, row, 5, d) * (routed + shared)
    if final_norm:
        x2 = x2 * lax.rsqrt(jnp.mean(x2 * x2, axis=-1, keepdims=True) + NORM_EPS) * fg_ref[...]
    out_ref[...] = x2

    @pl.when(i == last)
    def _():
        wait_rows(tabn_ref, 1 - slot)


def _combine(tab, tab_v, pos_hi, pos_lo, gate, ys, hx2, x1, mod, ws_gu, ws_down, fg, tiles_per_seq, final_norm):
    n, d = x1.shape
    row = lambda i: (i, 0)
    const = lambda i: (0, 0)
    return pl.pallas_call(
        functools.partial(_combine_kernel, tiles_per_seq=tiles_per_seq, final_norm=final_norm),
        grid=(n // TOKEN_TILE,),
        in_specs=[pl.BlockSpec((TAB_WIDTH,), lambda i: (i,), memory_space=pltpu.SMEM),
                  pl.BlockSpec((TAB_WIDTH,), lambda i: (jnp.minimum(i + 1, n // TOKEN_TILE - 1),),
                               memory_space=pltpu.SMEM),
                  pl.BlockSpec((1, N_EXPERTS, LANES), lambda i: (i, 0, 0)),
                  pl.BlockSpec((TOKEN_TILE, N_EXPERTS), row),
                  pl.BlockSpec((TOKEN_TILE, N_EXPERTS), row),
                  pl.BlockSpec((TOKEN_TILE, N_EXPERTS), row),
                  pl.BlockSpec(memory_space=pl.ANY),
                  pl.BlockSpec((TOKEN_TILE, d), row),
                  pl.BlockSpec((TOKEN_TILE, d), row),
                  pl.BlockSpec(mod.shape, const),
                  pl.BlockSpec(ws_gu.shape, const),
                  pl.BlockSpec(ws_down.shape, const),
                  pl.BlockSpec((1, d), const)],
        out_specs=pl.BlockSpec((TOKEN_TILE, d), row),
        out_shape=jax.ShapeDtypeStruct((n, d), F32),
        scratch_shapes=[pltpu.VMEM((2, SORTED_ROWS, d), BF16), pltpu.SemaphoreType.DMA((2,))],
        compiler_params=_params(("arbitrary",)),
        name="moe_combine",
    )(tab, tab, tab_v, pos_hi, pos_lo, gate, ys, hx2, x1, mod, ws_gu, ws_down, fg)


def _moe(x1, hx2, pos_hi, pos_lo, pos_hi_t, pos_lo_t, gate_t, tab_f, tab_t, mod, w_gu, w_down, layer, ws_gu,
         ws_down, fg, tiles_per_seq, final_norm):
    n, d = x1.shape
    n_tiles = n // TOKEN_TILE
    run_len = tab_f[:, :, 0].astype(jnp.int32)
    counts = jnp.sum(run_len, axis=0)
    padded = (counts + EXPERT_BLOCK - 1) // EXPERT_BLOCK * EXPERT_BLOCK
    pad_end = jnp.cumsum(padded)
    pad_start = pad_end - padded
    max_rows = n * TOP_K + n_tiles * N_EXPERTS * ROW_ALIGN + N_EXPERTS * (EXPERT_BLOCK - ROW_ALIGN)
    n_blocks = (max_rows + EXPERT_BLOCK - 1) // EXPERT_BLOCK
    n_used = (pad_end[-1:] // EXPERT_BLOCK).astype(jnp.int32)
    block_start = jnp.arange(n_blocks, dtype=jnp.int32) * EXPERT_BLOCK
    block_e = jnp.minimum(jnp.sum((pad_end[None, :] <= block_start[:, None]).astype(jnp.int32), axis=1),
                          N_EXPERTS - 1)

    run_start = pad_start[None, :] + jnp.cumsum(run_len, axis=0) - run_len
    total = jnp.sum(run_len, axis=1, keepdims=True)
    prev_total = jnp.concatenate([jnp.zeros((DISPATCH_TILES, 1), jnp.int32), total[:-DISPATCH_TILES]], axis=0)
    run_local = tab_f[:, :, 2].astype(jnp.int32)
    fill = jnp.zeros((n_tiles, TAB_WIDTH - 3 * N_EXPERTS - 2), jnp.int32)
    tab = jnp.concatenate([run_len, run_start, total, prev_total, run_local, fill], axis=1).reshape(-1)
    ztab = jnp.concatenate([padded - counts, pad_start + counts, n_used])

    xs = _dispatch(tab, ztab, tab_t, pos_hi, pos_lo, hx2, n_blocks * EXPERT_BLOCK)
    ys = _experts(block_e, n_used, xs, w_gu, w_down, layer)
    return _combine(tab, tab_f, pos_hi_t, pos_lo_t, gate_t, ys, hx2, x1, mod, ws_gu, ws_down, fg,
                    tiles_per_seq, final_norm)


def _sincos_tables(rows, d):
    quarter = d // 4
    omega = 1.0 / (POS_BASE ** (jnp.arange(quarter, dtype=F32) / quarter))

    def emb(n):
        p = jnp.arange(n, dtype=F32)[:, None] * omega[None, :]
        return jnp.concatenate([jnp.sin(p), jnp.cos(p)], axis=-1)

    return emb(rows), emb(GRID_W)


def kernel(x, c, ctx, c_ctx, ada_w, ada_b, mix_norm_g, ffn_norm_g, a_w_in, a_conv_w, a_conv_b, a_gate_r_w, a_gate_r_b, a_gate_i_w, a_gate_i_b, a_lambda, a_w_out, b_w_in, b_ln_g, b_ln_b, b_w_s, b_b_s, b_w_out, router_w, router_b, moe_w_gu, moe_w_down, shared_w_gu, shared_w_down, final_norm_g):
    bsz, s, d = x.shape
    ctx_len = ctx.shape[1]
    depth = ada_w.shape[0]
    assert depth == 2 and bsz < MOD_ROWS and s % MIX_TILE == 0 and MIX_TILE % TOKEN_TILE == 0
    assert s % SCAN_TILE == 0 and MIX_TILE % GRID_W == 0
    assert ctx_len % TOKEN_TILE == 0
    n = bsz * s
    tps = s // TOKEN_TILE
    ctx_row = bsz

    cc = jnp.zeros((MOD_ROWS, d), F32).at[:bsz].set(c).at[ctx_row].set(c_ctx)
    mod = _modulation(cc, ada_w, ada_b)
    er, ec = _sincos_tables(s // GRID_W, d)
    rc = a_w_in.shape[2] // 2

    w_in0 = a_w_in[0].astype(BF16)
    g_mix0 = mix_norm_g[0].reshape(1, d)
    x2 = x.reshape(n, d)
    gate, ux = _rglru_in(x2, er, ec, mod[0], g_mix0, w_in0)
    uc = _ctx_in(ctx.reshape(bsz * ctx_len, d), mod[0], g_mix0, w_in0[:, rc:], ctx_row)
    w_ri = jnp.concatenate([a_gate_r_w[0], a_gate_i_w[0]], axis=-1).astype(BF16)
    conv = (a_conv_w[0], a_conv_b[0])
    gates = [(w_ri[k], a_gate_r_b[0, k], a_gate_i_b[0, k], a_lambda[0, k]) for k in range(2)]
    h_zero = jnp.zeros((bsz, rc), F32)
    h_fwd, uc = _lru_scan(uc, conv, *gates[0], h_zero, reverse=False, reset_first=True, emit_y=False)
    h_rev = _lru_scan(uc, None, *gates[1], h_zero, reverse=True, reset_first=True, emit_y=False)
    y_fwd, ux = _lru_scan(ux, conv, *gates[0], h_fwd, reverse=False, reset_first=False, emit_y=True)
    y_rev = _lru_scan(ux, None, *gates[1], h_rev, reverse=True, reset_first=False, emit_y=True)
    pre = _rglru_out(y_fwd, y_rev, gate, x2, er, ec, mod[0], a_w_out[0].astype(BF16), ffn_norm_g[0].reshape(1, d),
                     router_w[0].T, router_b[0].reshape(N_EXPERTS, 1), s // MIX_TILE)
    x1 = _moe(*pre, mod[0], moe_w_gu, moe_w_down, 0, shared_w_gu[0].astype(BF16),
              shared_w_down[0].astype(BF16), final_norm_g.reshape(1, d), tps, False)

    pre = _sgu(x1, mod[1], mix_norm_g[1].reshape(1, d), b_w_in[0].astype(BF16),
               b_ln_g[0].reshape(1, -1), b_ln_b[0].reshape(1, -1), b_w_s[0].astype(BF16), b_b_s[0].T,
               b_w_out[0].astype(BF16), ffn_norm_g[1].reshape(1, d),
               router_w[1].T, router_b[1].reshape(N_EXPERTS, 1), s // MIX_TILE)
    out = _moe(*pre, mod[1], moe_w_gu, moe_w_down, 1, shared_w_gu[1].astype(BF16),
               shared_w_down[1].astype(BF16), final_norm_g.reshape(1, d), tps, True)
    return out.reshape(bsz, s, d)
```

```python
import functools

import jax
import jax.numpy as jnp
from jax import lax
from jax.experimental import pallas as pl
from jax.experimental.pallas import tpu as pltpu

F32 = jnp.float32
BF16 = jnp.bfloat16
HIGHEST = lax.Precision.HIGHEST

GRID_W = 64
NORM_EPS = 1e-6
POS_BASE = 10000.0
RNN_HEADS = 5
CONV_WIDTH = 4
CONV_PAD_LEFT = 2
LRU_C = 8.0
SGU_HEADS = 8
CHUNK = 128
N_EXPERTS = 64
TOP_K = 8
N_GROUPS = 8
TOPK_GROUPS = 4
EXPERTS_PER_GROUP = N_EXPERTS // N_GROUPS
ROUTED_SCALE = 2.5

SUBLANES = 8
ROW_ALIGN = 16
LANES = 128
MOD_ROWS = 8
TOKEN_TILE = 256
MIX_TILE = 512
SCAN_TILE = 512
IN_TILE = 1024
EXPERT_BLOCK = 1024
SORTED_ROWS = TOKEN_TILE * TOP_K + N_EXPERTS * ROW_ALIGN
PERM_CHUNK = 512
DISPATCH_TILES = 2
TAB_WIDTH = 256
TAB_LOCAL = 2 * N_EXPERTS + 2
POS_RADIX = 64
VMEM_LIMIT = 56 * 1024 * 1024


def _params(semantics, vmem=VMEM_LIMIT):
    return pltpu.CompilerParams(dimension_semantics=semantics, vmem_limit_bytes=vmem)


def _silu(x):
    return x * jax.nn.sigmoid(x)


def _rms_mod(x, g, sc, sh):
    y = x * lax.rsqrt(jnp.mean(x * x, axis=-1, keepdims=True) + NORM_EPS)
    return (y * g) * (1.0 + sc) + sh


def _mod_chunk(mod_ref, row, k, d):
    return mod_ref[pl.ds(row, 1), k * d:(k + 1) * d]


def _dot_nt_3pass(a, b):
    dims = (((1,), (1,)), ((), ()))
    a_hi, b_hi = a.astype(BF16), b.astype(BF16)
    a_lo = (a - a_hi.astype(F32)).astype(BF16)
    b_lo = (b - b_hi.astype(F32)).astype(BF16)
    dot = functools.partial(lax.dot_general, dimension_numbers=dims, preferred_element_type=F32)
    return dot(a_hi, b_hi) + dot(a_hi, b_lo) + dot(a_lo, b_hi)


def _round_up_rows(count):
    return jnp.maximum(jnp.ceil(count * (1.0 / ROW_ALIGN)), 1.0) * float(ROW_ALIGN)


def _mod_kernel(cc_ref, w_ref, b_ref, o_ref):
    s = _silu(cc_ref[...])
    o_ref[0] = jnp.dot(s, w_ref[0], preferred_element_type=F32, precision=HIGHEST) + b_ref[0]


def _modulation(cc, ada_w, ada_b):
    depth, d, nd = ada_w.shape
    return pl.pallas_call(
        _mod_kernel,
        grid=(depth, nd // d),
        in_specs=[pl.BlockSpec((MOD_ROWS, d), lambda l, j: (0, 0)),
                  pl.BlockSpec((1, d, d), lambda l, j: (l, 0, j)),
                  pl.BlockSpec((1, 1, d), lambda l, j: (l, 0, j))],
        out_specs=pl.BlockSpec((1, MOD_ROWS, d), lambda l, j: (l, 0, j)),
        out_shape=jax.ShapeDtypeStruct((depth, MOD_ROWS, nd), F32),
        compiler_params=_params(("arbitrary", "arbitrary")),
        name="modulation",
    )(cc, ada_w, ada_b.reshape(depth, 1, nd))


def _with_pos_code(x_ref, er_ref, ec_ref, tiles_per_seq):
    t_rows = x_ref.shape[0]
    grid_w, half = ec_ref.shape
    rows = t_rows // grid_w
    r0 = pl.multiple_of((pl.program_id(0) % tiles_per_seq) * rows, rows)
    er = er_ref[pl.ds(r0, rows), :]
    pe = jnp.concatenate([jnp.broadcast_to(er[:, None, :], (rows, grid_w, half)).reshape(t_rows, half),
                          jnp.concatenate([ec_ref[...]] * rows, axis=0)], axis=1)
    return x_ref[...] + pe


def _rglru_in_kernel(x_ref, er_ref, ec_ref, mod_ref, g_ref, w_ref, gate_ref, u_ref, *, tiles_per_seq):
    d = x_ref.shape[1]
    c = u_ref.shape[1]
    row = pl.program_id(0) // tiles_per_seq
    x = _with_pos_code(x_ref, er_ref, ec_ref, tiles_per_seq)
    hx = _rms_mod(x, g_ref[...], _mod_chunk(mod_ref, row, 1, d), _mod_chunk(mod_ref, row, 0, d))
    z = jnp.dot(hx.astype(BF16), w_ref[...], preferred_element_type=F32)
    gate_ref[...] = jax.nn.gelu(z[:, :c]).astype(BF16)
    u_ref[...] = z[:, c:]


def _rglru_in(x2, er, ec, mod, g, w_in):
    n, d = x2.shape
    c = w_in.shape[1] // 2
    tps = er.shape[0] * ec.shape[0] // IN_TILE
    row = lambda i: (i, 0)
    return pl.pallas_call(
        functools.partial(_rglru_in_kernel, tiles_per_seq=tps),
        grid=(n // IN_TILE,),
        in_specs=[pl.BlockSpec((IN_TILE, d), row),
                  pl.BlockSpec(er.shape, lambda i: (0, 0)),
                  pl.BlockSpec(ec.shape, lambda i: (0, 0)),
                  pl.BlockSpec(mod.shape, lambda i: (0, 0)),
                  pl.BlockSpec((1, d), lambda i: (0, 0)),
                  pl.BlockSpec(w_in.shape, lambda i: (0, 0))],
        out_specs=[pl.BlockSpec((IN_TILE, c), row),
                   pl.BlockSpec((IN_TILE, c), row)],
        out_shape=[jax.ShapeDtypeStruct((n, c), BF16),
                   jax.ShapeDtypeStruct((n, c), F32)],
        compiler_params=_params(("arbitrary",)),
        name="rglru_in",
    )(x2, er, ec, mod, g, w_in)


def _ctx_in_kernel(x_ref, mod_ref, g_ref, w_ref, u_ref, *, ctx_row):
    d = x_ref.shape[1]
    hx = _rms_mod(x_ref[...], g_ref[...], _mod_chunk(mod_ref, ctx_row, 1, d),
                  _mod_chunk(mod_ref, ctx_row, 0, d))
    u_ref[...] = jnp.dot(hx.astype(BF16), w_ref[...], preferred_element_type=F32)


def _ctx_in(c2, mod, g, w_u, ctx_row):
    n, d = c2.shape
    c = w_u.shape[1]
    return pl.pallas_call(
        functools.partial(_ctx_in_kernel, ctx_row=ctx_row),
        grid=(n // TOKEN_TILE,),
        in_specs=[pl.BlockSpec((TOKEN_TILE, d), lambda i: (i, 0)),
                  pl.BlockSpec(mod.shape, lambda i: (0, 0)),
                  pl.BlockSpec((1, d), lambda i: (0, 0)),
                  pl.BlockSpec(w_u.shape, lambda i: (0, 0))],
        out_specs=pl.BlockSpec((TOKEN_TILE, c), lambda i: (i, 0)),
        out_shape=jax.ShapeDtypeStruct((n, c), F32),
        compiler_params=_params(("arbitrary",)),
        name="ctx_in",
    )(c2, mod, g, w_u)


def _log_sigmoid(x):
    return jnp.minimum(x, 0.0) - jnp.log1p(jnp.exp(-jnp.abs(x)))


def _lru_scan_kernel(*refs, n_tiles, reverse, conv_done, reset_first, emit_y):
    if conv_done:
        u_ref, wri_ref, rb_ref, ib_ref, lam_ref, h0_ref, out_ref, a_scr, b_scr, h_scr = refs
    else:
        (u_ref, up_ref, un_ref, cw_ref, cb_ref, wri_ref, rb_ref, ib_ref, lam_ref, h0_ref,
         out_ref, uc_ref, ubuf, a_scr, b_scr, h_scr) = refs
    t_rows, c = u_ref.shape
    hb = c // RNN_HEADS
    b = pl.program_id(0)
    j = pl.program_id(1)
    jj = n_tiles - 1 - j if reverse else j

    if conv_done:
        u = u_ref[...]
    else:
        ubuf[SUBLANES:SUBLANES + t_rows, :] = u_ref[...]
        ubuf[0:SUBLANES, :] = jnp.where(jj == 0, 0.0, up_ref[...])
        ubuf[SUBLANES + t_rows:, :] = jnp.where(jj == n_tiles - 1, 0.0, un_ref[...])
        u = cb_ref[...]
        full = ubuf[...]
        for k in range(CONV_WIDTH):
            shift = (CONV_PAD_LEFT - k) % full.shape[0]
            tap = full if shift == 0 else pltpu.roll(full, shift, 0)
            u = u + cw_ref[k:k + 1, :] * tap[SUBLANES:SUBLANES + t_rows, :]
        uc_ref[...] = u

    log_lam = LRU_C * _log_sigmoid(lam_ref[...])
    rows = lax.broadcasted_iota(jnp.int32, (t_rows, 1), 0)
    first_row = jnp.where(j == 0, t_rows - 1 if reverse else 0, -1)
    for h in range(RNN_HEADS):
        sl = slice(h * hb, (h + 1) * hb)
        uh = u[:, sl]
        z = jnp.dot(uh.astype(BF16), wri_ref[h], preferred_element_type=F32)
        r = jax.nn.sigmoid(z[:, :hb] + rb_ref[:, sl])
        ig = jax.nn.sigmoid(z[:, hb:] + ib_ref[:, sl])
        log_a = r * log_lam[:, sl]
        a = jnp.exp(log_a)
        mult = jnp.sqrt((1.0 - a) * (1.0 + a))
        if reset_first:
            mult = jnp.where(rows == first_row, 1.0, mult)
        a_scr[:, sl] = a
        b_scr[:, sl] = mult * ig * uh

    @pl.when(j == 0)
    def _():
        h_scr[...] = h0_ref[pl.ds(b, 1), :]

    n_groups = t_rows // SUBLANES

    def group(g, h):
        base = pl.multiple_of((n_groups - 1 - g if reverse else g) * SUBLANES, SUBLANES)
        for s in range(SUBLANES):
            r = base + (SUBLANES - 1 - s if reverse else s)
            h = a_scr[pl.ds(r, 1), :] * h + b_scr[pl.ds(r, 1), :]
            if emit_y:
                out_ref[pl.ds(r, 1), :] = h
        return h

    h = lax.fori_loop(0, n_groups, group, h_scr[...])
    h_scr[...] = h
    if not emit_y:
        @pl.when(j == n_tiles - 1)
        def _():
            out_ref[pl.ds(b, 1), :] = h


def _lru_scan(u, conv, w_ri, r_b, i_b, lam, h0, *, reverse, reset_first, emit_y):
    n, c = u.shape
    n_batch = h0.shape[0]
    rows = min(SCAN_TILE, n // n_batch)
    n_tiles = n // n_batch // rows
    sub = rows // SUBLANES
    n_sub = n // SUBLANES
    const = lambda b, j: (0, 0)

    def tile(b, j):
        return b * n_tiles + (n_tiles - 1 - j if reverse else j)

    tile_spec = pl.BlockSpec((rows, c), lambda b, j: (tile(b, j), 0))
    in_specs = [tile_spec]
    args = [u]
    scratch = []
    if conv is not None:
        conv_w, conv_b = conv
        in_specs += [pl.BlockSpec((SUBLANES, c), lambda b, j: (jnp.maximum(tile(b, j) * sub - 1, 0), 0)),
                     pl.BlockSpec((SUBLANES, c), lambda b, j: (jnp.minimum((tile(b, j) + 1) * sub, n_sub - 1), 0)),
                     pl.BlockSpec(conv_w.shape, const),
                     pl.BlockSpec((1, c), const)]
        args += [u, u, conv_w, conv_b.reshape(1, c)]
        scratch = [pltpu.VMEM((rows + 2 * SUBLANES, c), F32)]
    in_specs += [pl.BlockSpec(w_ri.shape, lambda b, j: (0, 0, 0)),
                 pl.BlockSpec((1, c), const), pl.BlockSpec((1, c), const), pl.BlockSpec((1, c), const),
                 pl.BlockSpec(h0.shape, const)]
    args += [w_ri, r_b.reshape(1, c), i_b.reshape(1, c), lam.reshape(1, c), h0]
    if emit_y:
        out_specs = [tile_spec]
        out_shape = [jax.ShapeDtypeStruct((n, c), F32)]
    else:
        out_specs = [pl.BlockSpec(h0.shape, const)]
        out_shape = [jax.ShapeDtypeStruct(h0.shape, F32)]
    if conv is not None:
        out_specs.append(tile_spec)
        out_shape.append(jax.ShapeDtypeStruct((n, c), F32))
    name = ("lru_scan" if emit_y else "lru_ctx") + ("_rev" if reverse else "_fwd")
    outs = pl.pallas_call(
        functools.partial(_lru_scan_kernel, n_tiles=n_tiles, reverse=reverse, conv_done=conv is None,
                          reset_first=reset_first, emit_y=emit_y),
        grid=(n_batch, n_tiles),
        in_specs=in_specs,
        out_specs=out_specs,
        out_shape=out_shape,
        scratch_shapes=scratch + [pltpu.VMEM((rows, c), F32),
                                  pltpu.VMEM((rows, c), F32),
                                  pltpu.VMEM((1, c), F32)],
        compiler_params=_params(("arbitrary", "arbitrary")),
        name=name,
    )(*args)
    return outs if conv is not None else outs[0]


def _route(logits, rb):
    e, t = logits.shape
    neg = -jnp.inf
    scores = jax.nn.sigmoid(logits)
    sel = scores + rb
    iota_g = lax.broadcasted_iota(jnp.int32, (N_GROUPS, t), 0).astype(F32)
    iota_e = lax.broadcasted_iota(jnp.int32, (e, t), 0).astype(F32)

    gs = jnp.full((N_GROUPS, t), neg, F32)
    for g in range(N_GROUPS):
        sg = sel[g * EXPERTS_PER_GROUP:(g + 1) * EXPERTS_PER_GROUP, :]
        m1 = jnp.max(sg, axis=0, keepdims=True)
        i1 = jnp.min(jnp.where(sg == m1, iota_g, float(EXPERTS_PER_GROUP)), axis=0, keepdims=True)
        m2 = jnp.max(jnp.where(iota_g == i1, neg, sg), axis=0, keepdims=True)
        gs = jnp.where(iota_g == float(g), m1 + m2, gs)

    keep = jnp.zeros((N_GROUPS, t), F32)
    for _ in range(TOPK_GROUPS):
        m = jnp.max(gs, axis=0, keepdims=True)
        idx = jnp.min(jnp.where(gs == m, iota_g, float(N_GROUPS)), axis=0, keepdims=True)
        hit = iota_g == idx
        keep = jnp.where(hit, 1.0, keep)
        gs = jnp.where(hit, neg, gs)

    masked = jnp.concatenate(
        [jnp.where(keep[g:g + 1, :] > 0.0, sel[g * EXPERTS_PER_GROUP:(g + 1) * EXPERTS_PER_GROUP, :], neg)
         for g in range(N_GROUPS)], axis=0)

    iota_k = lax.broadcasted_iota(jnp.int32, (TOP_K, t), 0)
    selmask = jnp.zeros((e, t), F32)
    eidx = jnp.zeros((TOP_K, t), F32)
    gw = jnp.zeros((TOP_K, t), F32)
    for k in range(TOP_K):
        m = jnp.max(masked, axis=0, keepdims=True)
        idx = jnp.min(jnp.where(masked == m, iota_e, float(e)), axis=0, keepdims=True)
        hit = iota_e == idx
        gk = jnp.sum(jnp.where(hit, scores, 0.0), axis=0, keepdims=True)
        masked = jnp.where(hit, neg, masked)
        selmask = jnp.where(hit, 1.0, selmask)
        eidx = jnp.where(iota_k == k, idx, eidx)
        gw = jnp.where(iota_k == k, gk, gw)
    gw = gw / jnp.sum(gw, axis=0, keepdims=True) * ROUTED_SCALE
    return eidx, gw, selmask


def _ffn_pre(x1, mod_ref, row, g2_ref, rwt_ref, rb_ref, hx2_ref, hi_ref, lo_ref, hit_ref, lot_ref, gatet_ref,
             tab_ref, tabt_ref):
    d = x1.shape[1]
    hx2 = _rms_mod(x1, g2_ref[...], _mod_chunk(mod_ref, row, 4, d), _mod_chunk(mod_ref, row, 3, d))
    hx2_ref[...] = hx2.astype(BF16)
    logits = _dot_nt_3pass(rwt_ref[...], hx2)
    eidx, gw, selmask = _route(logits, rb_ref[...])
    for s in range(selmask.shape[1] // TOKEN_TILE):
        ts = slice(s * TOKEN_TILE, (s + 1) * TOKEN_TILE)
        hi, lo, gate, tab, tabt = _sorted_layout(selmask[:, ts], eidx[:, ts], gw[:, ts])
        hi_ref[:, ts] = hi.astype(BF16)
        lo_ref[:, ts] = lo.astype(BF16)
        hit_ref[ts, :] = hi.T.astype(BF16)
        lot_ref[ts, :] = lo.T.astype(BF16)
        gatet_ref[ts, :] = gate.T.astype(BF16)
        tab_ref[s] = tab
        tabt_ref[s] = tabt


def _ffn_pre_specs(n, d):
    row = lambda i: (i, 0)
    col = lambda i: (0, i)
    sub = MIX_TILE // TOKEN_TILE
    lead = lambda i: (i, 0, 0)
    out_specs = [pl.BlockSpec((MIX_TILE, d), row),
                 pl.BlockSpec((MIX_TILE, d), row),
                 pl.BlockSpec((N_EXPERTS, MIX_TILE), col),
                 pl.BlockSpec((N_EXPERTS, MIX_TILE), col),
                 pl.BlockSpec((MIX_TILE, N_EXPERTS), row),
                 pl.BlockSpec((MIX_TILE, N_EXPERTS), row),
                 pl.BlockSpec((MIX_TILE, N_EXPERTS), row),
                 pl.BlockSpec((sub, N_EXPERTS, LANES), lead),
                 pl.BlockSpec((sub, SUBLANES, LANES), lead)]
    out_shape = [jax.ShapeDtypeStruct((n, d), F32),
                 jax.ShapeDtypeStruct((n, d), BF16),
                 jax.ShapeDtypeStruct((N_EXPERTS, n), BF16),
                 jax.ShapeDtypeStruct((N_EXPERTS, n), BF16),
                 jax.ShapeDtypeStruct((n, N_EXPERTS), BF16),
                 jax.ShapeDtypeStruct((n, N_EXPERTS), BF16),
                 jax.ShapeDtypeStruct((n, N_EXPERTS), BF16),
                 jax.ShapeDtypeStruct((n // TOKEN_TILE, N_EXPERTS, LANES), F32),
                 jax.ShapeDtypeStruct((n // TOKEN_TILE, SUBLANES, LANES), F32)]
    return out_specs, out_shape


def _rglru_out_kernel(yf_ref, yr_ref, gate_ref, x_ref, er_ref, ec_ref, mod_ref, wout_ref, g2_ref, rwt_ref,
                      rb_ref, x1_ref, hx2_ref, hi_ref, lo_ref, hit_ref, lot_ref, gatet_ref, tab_ref, tabt_ref,
                      *, tiles_per_seq):
    d = x_ref.shape[1]
    row = pl.program_id(0) // tiles_per_seq
    yx = yf_ref[...] + yr_ref[...]
    v = gate_ref[...].astype(F32) * yx
    out = jnp.dot(v.astype(BF16), wout_ref[...], preferred_element_type=F32)
    x1 = _with_pos_code(x_ref, er_ref, ec_ref, tiles_per_seq) + _mod_chunk(mod_ref, row, 2, d) * out
    x1_ref[...] = x1
    _ffn_pre(x1, mod_ref, row, g2_ref, rwt_ref, rb_ref, hx2_ref, hi_ref, lo_ref, hit_ref, lot_ref, gatet_ref,
             tab_ref, tabt_ref)


def _rglru_out(y_fwd, y_rev, gate, x, er, ec, mod, w_out, g2, rwt, rb, tiles_per_seq):
    n, d = x.shape
    c = gate.shape[1]
    row = lambda i: (i, 0)
    const = lambda i: (0, 0)
    out_specs, out_shape = _ffn_pre_specs(n, d)
    return pl.pallas_call(
        functools.partial(_rglru_out_kernel, tiles_per_seq=tiles_per_seq),
        grid=(n // MIX_TILE,),
        in_specs=[pl.BlockSpec((MIX_TILE, c), row),
                  pl.BlockSpec((MIX_TILE, c), row),
                  pl.BlockSpec((MIX_TILE, c), row),
                  pl.BlockSpec((MIX_TILE, d), row),
                  pl.BlockSpec(er.shape, const),
                  pl.BlockSpec(ec.shape, const),
                  pl.BlockSpec(mod.shape, const),
                  pl.BlockSpec(w_out.shape, const),
                  pl.BlockSpec((1, d), const),
                  pl.BlockSpec(rwt.shape, const),
                  pl.BlockSpec(rb.shape, const)],
        out_specs=out_specs,
        out_shape=out_shape,
        compiler_params=_params(("arbitrary",)),
        name="rglru_out",
    )(y_fwd, y_rev, gate, x, er, ec, mod, w_out, g2, rwt, rb)


def _sgu_kernel(x_ref, mod_ref, g_ref, win_ref, lng_ref, lnb_ref, ws_ref, bst_ref, wout_ref,
                g2_ref, rwt_ref, rb_ref,
                x1_ref, hx2_ref, hi_ref, lo_ref, hit_ref, lot_ref, gatet_ref, tab_ref, tabt_ref, m_scr, *,
                tiles_per_seq):
    t_rows, d = x_ref.shape
    w = wout_ref.shape[0]
    gd = w // SGU_HEADS
    row = pl.program_id(0) // tiles_per_seq
    x = x_ref[...]
    hx = _rms_mod(x, g_ref[...], _mod_chunk(mod_ref, row, 1, d), _mod_chunk(mod_ref, row, 0, d))
    z = jax.nn.gelu(jnp.dot(hx.astype(BF16), win_ref[...], preferred_element_type=F32))
    u = z[:, :w]
    v = z[:, w:]
    mu = jnp.mean(v, axis=-1, keepdims=True)
    vc = v - mu
    v = vc * lax.rsqrt(jnp.mean(vc * vc, axis=-1, keepdims=True) + NORM_EPS) * lng_ref[...] + lnb_ref[...]
    vb = v.astype(BF16)
    for ch in range(t_rows // CHUNK):
        rs = slice(ch * CHUNK, (ch + 1) * CHUNK)
        for g in range(SGU_HEADS):
            cs = slice(g * gd, (g + 1) * gd)
            sv = jnp.dot(ws_ref[g], vb[rs, cs], preferred_element_type=F32) + bst_ref[:, g:g + 1]
            m_scr[rs, cs] = (u[rs, cs] * sv).astype(BF16)
    out = jnp.dot(m_scr[...], wout_ref[...], preferred_element_type=F32)
    x1 = x + _mod_chunk(mod_ref, row, 2, d) * out
    x1_ref[...] = x1
    _ffn_pre(x1, mod_ref, row, g2_ref, rwt_ref, rb_ref, hx2_ref, hi_ref, lo_ref, hit_ref, lot_ref, gatet_ref,
             tab_ref, tabt_ref)


def _sgu(x, mod, g, w_in, ln_g, ln_b, w_s, b_st, w_out, g2, rwt, rb, tiles_per_seq):
    n, d = x.shape
    w = w_out.shape[0]
    const = lambda i: (0, 0)
    out_specs, out_shape = _ffn_pre_specs(n, d)
    return pl.pallas_call(
        functools.partial(_sgu_kernel, tiles_per_seq=tiles_per_seq),
        grid=(n // MIX_TILE,),
        in_specs=[pl.BlockSpec((MIX_TILE, d), lambda i: (i, 0)),
                  pl.BlockSpec(mod.shape, const),
                  pl.BlockSpec((1, d), const),
                  pl.BlockSpec(w_in.shape, const, pipeline_mode=pl.Buffered(1)),
                  pl.BlockSpec((1, w), const),
                  pl.BlockSpec((1, w), const),
                  pl.BlockSpec(w_s.shape, lambda i: (0, 0, 0)),
                  pl.BlockSpec(b_st.shape, const),
                  pl.BlockSpec(w_out.shape, const, pipeline_mode=pl.Buffered(1)),
                  pl.BlockSpec((1, d), const),
                  pl.BlockSpec(rwt.shape, const),
                  pl.BlockSpec(rb.shape, const)],
        out_specs=out_specs,
        out_shape=out_shape,
        scratch_shapes=[pltpu.VMEM((MIX_TILE, w), BF16)],
        compiler_params=_params(("arbitrary",)),
        name="sgu",
    )(x, mod, g, w_in, ln_g, ln_b, w_s, b_st, w_out, g2, rwt, rb)


def _sorted_layout(m, eidx, gw):
    e, t = m.shape
    lanes = LANES
    r = lax.broadcasted_iota(jnp.int32, (t, t), 0)
    c = lax.broadcasted_iota(jnp.int32, (t, t), 1)
    upper = jnp.where(r <= c, 1.0, 0.0).astype(BF16)
    incl = jnp.dot(m.astype(BF16), upper, preferred_element_type=F32)
    run = _round_up_rows(incl[:, t - 1:t])
    re = lax.broadcasted_iota(jnp.int32, (e, e), 0)
    ce = lax.broadcasted_iota(jnp.int32, (e, e), 1)
    lower = jnp.where(ce < re, 1.0, 0.0).astype(BF16)
    tiles = jnp.broadcast_to(run * (1.0 / ROW_ALIGN), (e, lanes)).astype(BF16)
    lstart = jnp.dot(lower, tiles, preferred_element_type=F32)[:, 0:1] * float(ROW_ALIGN)
    pos = jnp.where(m > 0.0, lstart + incl - m, float(POS_RADIX * POS_RADIX - 1))
    hi = jnp.floor(pos * (1.0 / POS_RADIX))
    lo = pos - hi * float(POS_RADIX)
    iota_e = lax.broadcasted_iota(jnp.int32, (e, t), 0).astype(F32)
    gate = jnp.zeros((e, t), F32)
    for k in range(TOP_K):
        gate = jnp.where(iota_e == eidx[k:k + 1, :], gw[k:k + 1, :], gate)
    lane = lax.broadcasted_iota(jnp.int32, (e, lanes), 1)
    tab = jnp.where(lane == 0, run, jnp.where(lane == 2, lstart, 0.0))
    diag = lax.broadcasted_iota(jnp.int32, (e, lanes), 0) == lane
    row_of = lambda col: jnp.sum(jnp.where(diag, jnp.broadcast_to(col, (e, lanes)), 0.0), axis=0, keepdims=True)
    sub = lax.broadcasted_iota(jnp.int32, (SUBLANES, lanes), 0)
    tabt = jnp.where(sub == 0, row_of(lstart), jnp.where(sub == 1, row_of(lstart + run), 0.0))
    return hi, lo, gate, tab, tabt


def _aligned(v):
    return pl.multiple_of(v, ROW_ALIGN)


def _run_copies(tab_ref, make_copy, unroll, base=0):
    def body(e, carry):
        make_copy(_aligned(tab_ref[base + TAB_LOCAL + e]), _aligned(tab_ref[base + N_EXPERTS + e]),
                  _aligned(tab_ref[base + e])).start()
        return carry

    lax.fori_loop(0, N_EXPERTS, body, 0, unroll=unroll)


def _dispatch_kernel(tab_ref, ztab_ref, tabt_ref, hi_ref, lo_ref, x_ref, xs_hbm, sbuf, zbuf, sems, zsem):
    i = pl.program_id(0)
    tiles = sbuf.shape[0]
    t = x_ref.shape[0] // tiles
    rows = sbuf.shape[1]

    def zero_copy(e):
        n = _aligned(ztab_ref[e])
        return pltpu.make_async_copy(zbuf.at[pl.ds(0, n)], xs_hbm.at[pl.ds(_aligned(ztab_ref[N_EXPERTS + e]), n)], zsem)

    def for_zero_runs(fn):
        def body(e, c):
            @pl.when(ztab_ref[e] > 0)
            def _():
                fn(zero_copy(e))
            return c
        lax.fori_loop(0, N_EXPERTS, body, 0)

        def tail(b, c):
            fn(pltpu.make_async_copy(zbuf, xs_hbm.at[pl.ds(pl.multiple_of(b * EXPERT_BLOCK, EXPERT_BLOCK),
                                                           EXPERT_BLOCK)], zsem))
            return c
        lax.fori_loop(ztab_ref[2 * N_EXPERTS], xs_hbm.shape[0] // EXPERT_BLOCK, tail, 0)

    @pl.when(i == 0)
    def _():
        zbuf[...] = jnp.zeros_like(zbuf)
        for_zero_runs(lambda cp: cp.start())

    def wait_rows(s, n):
        pltpu.make_async_copy(sbuf.at[s, pl.ds(0, n)], xs_hbm.at[pl.ds(0, n)], sems.at[s]).wait()

    def permute(s, r0):
        ts = slice(s * t, (s + 1) * t)
        j_e = (lax.broadcasted_iota(jnp.int32, (PERM_CHUNK, N_EXPERTS), 0) + r0).astype(F32)
        owner = jnp.where((j_e >= tabt_ref[s, 0:1, 0:N_EXPERTS]) & (j_e < tabt_ref[s, 1:2, 0:N_EXPERTS]),
                          1.0, 0.0).astype(BF16)
        pos = (jnp.dot(owner, hi_ref[:, ts], preferred_element_type=F32) * float(POS_RADIX)
               + jnp.dot(owner, lo_ref[:, ts], preferred_element_type=F32))
        j_t = (lax.broadcasted_iota(jnp.int32, (PERM_CHUNK, t), 0) + r0).astype(F32)
        p = jnp.where(pos == j_t, 1.0, 0.0).astype(BF16)
        sbuf[s, r0:r0 + PERM_CHUNK, :] = jnp.dot(p, x_ref[ts, :], preferred_element_type=F32).astype(BF16)

    for s in range(tiles):
        base = s * TAB_WIDTH

        @pl.when(i > 0)
        def _():
            wait_rows(s, _aligned(tab_ref[base + 2 * N_EXPERTS + 1]))

        for r0 in range(0, rows, PERM_CHUNK):
            if r0 < t * TOP_K + PERM_CHUNK:
                permute(s, r0)
            else:
                pl.when(tab_ref[base + 2 * N_EXPERTS] > r0)(functools.partial(permute, s, r0))

        _run_copies(tab_ref, lambda loc, glob, n: pltpu.make_async_copy(
            sbuf.at[s, pl.ds(loc, n)], xs_hbm.at[pl.ds(glob, n)], sems.at[s]), unroll=True, base=base)

    @pl.when(i == pl.num_programs(0) - 1)
    def _():
        for s in range(tiles):
            wait_rows(s, _aligned(tab_ref[s * TAB_WIDTH + 2 * N_EXPERTS]))

    @pl.when(i == 0)
    def _():
        for_zero_runs(lambda cp: cp.wait())


def _dispatch(tab, ztab, tab_t, pos_hi, pos_lo, hx2, xs_rows):
    n, d = hx2.shape
    col = lambda i: (0, i)
    return pl.pallas_call(
        _dispatch_kernel,
        grid=(n // (DISPATCH_TILES * TOKEN_TILE),),
        in_specs=[pl.BlockSpec((DISPATCH_TILES * TAB_WIDTH,), lambda i: (i,), memory_space=pltpu.SMEM),
                  pl.BlockSpec(memory_space=pltpu.SMEM),
                  pl.BlockSpec((DISPATCH_TILES, SUBLANES, LANES), lambda i: (i, 0, 0)),
                  pl.BlockSpec((N_EXPERTS, DISPATCH_TILES * TOKEN_TILE), col),
                  pl.BlockSpec((N_EXPERTS, DISPATCH_TILES * TOKEN_TILE), col),
                  pl.BlockSpec((DISPATCH_TILES * TOKEN_TILE, d), lambda i: (i, 0))],
        out_specs=pl.BlockSpec(memory_space=pl.ANY),
        out_shape=jax.ShapeDtypeStruct((xs_rows, d), BF16),
        scratch_shapes=[pltpu.VMEM((DISPATCH_TILES, SORTED_ROWS, d), BF16),
                        pltpu.VMEM((EXPERT_BLOCK, d), BF16),
                        pltpu.SemaphoreType.DMA((DISPATCH_TILES,)),
                        pltpu.SemaphoreType.DMA],
        compiler_params=_params(("arbitrary",)),
        name="moe_dispatch",
    )(tab, ztab, tab_t, pos_hi, pos_lo, hx2)


def _experts_kernel(be_ref, nb_ref, xs_ref, wgu_ref, wd_ref, ys_ref, wgu_b, wd_b):
    i = pl.program_id(0)
    ff = wd_b.shape[0]
    used = i < nb_ref[0]
    new_expert = (i == 0) | (be_ref[i] != be_ref[jnp.maximum(i - 1, 0)])

    @pl.when(used & new_expert)
    def _():
        wgu_b[...] = wgu_ref[0, 0].astype(BF16)
        wd_b[...] = wd_ref[0, 0].astype(BF16)

    @pl.when(used)
    def _():
        h = jnp.dot(xs_ref[...], wgu_b[...], preferred_element_type=F32)
        a = _silu(h[:, :ff]) * h[:, ff:]
        ys_ref[...] = jnp.dot(a.astype(BF16), wd_b[...], preferred_element_type=F32).astype(BF16)


def _experts(block_e, n_used, xs, w_gu, w_down, layer):
    rows, d = xs.shape
    nb = rows // EXPERT_BLOCK
    blk = lambda i, be, nu: (jnp.maximum(jnp.minimum(i, nu[0] - 1), 0), 0)
    wmap = lambda i, be, nu: (layer, be[i], 0, 0)
    return pl.pallas_call(
        _experts_kernel,
        grid_spec=pltpu.PrefetchScalarGridSpec(
            num_scalar_prefetch=2,
            grid=(nb,),
            in_specs=[pl.BlockSpec((EXPERT_BLOCK, d), blk),
                      pl.BlockSpec((1, 1) + w_gu.shape[2:], wmap),
                      pl.BlockSpec((1, 1) + w_down.shape[2:], wmap)],
            out_specs=pl.BlockSpec((EXPERT_BLOCK, d), blk),
            scratch_shapes=[pltpu.VMEM(w_gu.shape[2:], BF16), pltpu.VMEM(w_down.shape[2:], BF16)]),
        out_shape=jax.ShapeDtypeStruct((rows, d), BF16),
        input_output_aliases={2: 0},
        compiler_params=_params(("arbitrary",)),
        name="moe_experts",
    )(block_e, n_used, xs, w_gu, w_down)


def _combine_kernel(tab_ref, tabn_ref, tabv_ref, hi_ref, lo_ref, gate_ref, ys_hbm, hx2_ref, x1_ref, mod_ref,
                    wsgu_ref, wsd_ref, fg_ref, out_ref, ybuf, sems, *, tiles_per_seq, final_norm):
    t_rows, d = x1_ref.shape
    ff = wsd_ref.shape[0]
    rows = ybuf.shape[1]
    i = pl.program_id(0)
    last = pl.num_programs(0) - 1
    slot = i % 2
    row = i // tiles_per_seq

    def gather(table, s, unroll):
        _run_copies(table, lambda loc, glob, n: pltpu.make_async_copy(
            ys_hbm.at[pl.ds(glob, n)], ybuf.at[s, pl.ds(loc, n)], sems.at[s]), unroll=unroll)

    def wait_rows(table, s):
        n = _aligned(table[2 * N_EXPERTS])
        pltpu.make_async_copy(ys_hbm.at[pl.ds(0, n)], ybuf.at[s, pl.ds(0, n)], sems.at[s]).wait()

    @pl.when(i == 0)
    def _():
        ybuf[...] = jnp.zeros_like(ybuf)
        gather(tab_ref, 0, False)

    gather(tabn_ref, 1 - slot, True)

    hs = jnp.dot(hx2_ref[...], wsgu_ref[...], preferred_element_type=F32)
    shared = jnp.dot((_silu(hs[:, :ff]) * hs[:, ff:]).astype(BF16), wsd_ref[...], preferred_element_type=F32)

    run_lo = tabv_ref[0, :, 2:3]
    run_hi = run_lo + tabv_ref[0, :, 0:1]
    j_e = lax.broadcasted_iota(jnp.int32, (N_EXPERTS, rows), 1).astype(F32)
    owner = jnp.where((j_e >= run_lo) & (j_e < run_hi), 1.0, 0.0).astype(BF16)
    pos = (jnp.dot(hi_ref[...], owner, preferred_element_type=F32) * float(POS_RADIX)
           + jnp.dot(lo_ref[...], owner, preferred_element_type=F32))
    gates = jnp.dot(gate_ref[...], owner, preferred_element_type=F32)
    j_t = lax.broadcasted_iota(jnp.int32, (t_rows, rows), 1).astype(F32)
    gb = jnp.where(pos == j_t, gates, 0.0).astype(BF16)

    wait_rows(tab_ref, slot)
    routed = jnp.dot(gb, ybuf[slot], preferred_element_type=F32)
    x2 = x1_ref[...] + _mod_chunk(mod_ref, row, 5, d) * (routed + shared)
    if final_norm:
        x2 = x2 * lax.rsqrt(jnp.mean(x2 * x2, axis=-1, keepdims=True) + NORM_EPS) * fg_ref[...]
    out_ref[...] = x2

    @pl.when(i == last)
    def _():
        wait_rows(tabn_ref, 1 - slot)


def _combine(tab, tab_v, pos_hi, pos_lo, gate, ys, hx2, x1, mod, ws_gu, ws_down, fg, tiles_per_seq, final_norm):
    n, d = x1.shape
    row = lambda i: (i, 0)
    const = lambda i: (0, 0)
    return pl.pallas_call(
        functools.partial(_combine_kernel, tiles_per_seq=tiles_per_seq, final_norm=final_norm),
        grid=(n // TOKEN_TILE,),
        in_specs=[pl.BlockSpec((TAB_WIDTH,), lambda i: (i,), memory_space=pltpu.SMEM),
                  pl.BlockSpec((TAB_WIDTH,), lambda i: (jnp.minimum(i + 1, n // TOKEN_TILE - 1),),
                               memory_space=pltpu.SMEM),
                  pl.BlockSpec((1, N_EXPERTS, LANES), lambda i: (i, 0, 0)),
                  pl.BlockSpec((TOKEN_TILE, N_EXPERTS), row),
                  pl.BlockSpec((TOKEN_TILE, N_EXPERTS), row),
                  pl.BlockSpec((TOKEN_TILE, N_EXPERTS), row),
                  pl.BlockSpec(memory_space=pl.ANY),
                  pl.BlockSpec((TOKEN_TILE, d), row),
                  pl.BlockSpec((TOKEN_TILE, d), row),
                  pl.BlockSpec(mod.shape, const),
                  pl.BlockSpec(ws_gu.shape, const),
                  pl.BlockSpec(ws_down.shape, const),
                  pl.BlockSpec((1, d), const)],
        out_specs=pl.BlockSpec((TOKEN_TILE, d), row),
        out_shape=jax.ShapeDtypeStruct((n, d), F32),
        scratch_shapes=[pltpu.VMEM((2, SORTED_ROWS, d), BF16), pltpu.SemaphoreType.DMA((2,))],
        compiler_params=_params(("arbitrary",)),
        name="moe_combine",
    )(tab, tab, tab_v, pos_hi, pos_lo, gate, ys, hx2, x1, mod, ws_gu, ws_down, fg)


def _moe(x1, hx2, pos_hi, pos_lo, pos_hi_t, pos_lo_t, gate_t, tab_f, tab_t, mod, w_gu, w_down, layer, ws_gu,
         ws_down, fg, tiles_per_seq, final_norm):
    n, d = x1.shape
    n_tiles = n // TOKEN_TILE
    run_len = tab_f[:, :, 0].astype(jnp.int32)
    counts = jnp.sum(run_len, axis=0)
    padded = (counts + EXPERT_BLOCK - 1) // EXPERT_BLOCK * EXPERT_BLOCK
    pad_end = jnp.cumsum(padded)
    pad_start = pad_end - padded
    max_rows = n * TOP_K + n_tiles * N_EXPERTS * ROW_ALIGN + N_EXPERTS * (EXPERT_BLOCK - ROW_ALIGN)
    n_blocks = (max_rows + EXPERT_BLOCK - 1) // EXPERT_BLOCK
    n_used = (pad_end[-1:] // EXPERT_BLOCK).astype(jnp.int32)
    block_start = jnp.arange(n_blocks, dtype=jnp.int32) * EXPERT_BLOCK
    block_e = jnp.minimum(jnp.sum((pad_end[None, :] <= block_start[:, None]).astype(jnp.int32), axis=1),
                          N_EXPERTS - 1)

    run_start = pad_start[None, :] + jnp.cumsum(run_len, axis=0) - run_len
    total = jnp.sum(run_len, axis=1, keepdims=True)
    prev_total = jnp.concatenate([jnp.zeros((DISPATCH_TILES, 1), jnp.int32), total[:-DISPATCH_TILES]], axis=0)
    run_local = tab_f[:, :, 2].astype(jnp.int32)
    fill = jnp.zeros((n_tiles, TAB_WIDTH - 3 * N_EXPERTS - 2), jnp.int32)
    tab = jnp.concatenate([run_len, run_start, total, prev_total, run_local, fill], axis=1).reshape(-1)
    ztab = jnp.concatenate([padded - counts, pad_start + counts, n_used])

    xs = _dispatch(tab, ztab, tab_t, pos_hi, pos_lo, hx2, n_blocks * EXPERT_BLOCK)
    ys = _experts(block_e, n_used, xs, w_gu, w_down, layer)
    return _combine(tab, tab_f, pos_hi_t, pos_lo_t, gate_t, ys, hx2, x1, mod, ws_gu, ws_down, fg,
                    tiles_per_seq, final_norm)


def _sincos_tables(rows, d):
    quarter = d // 4
    omega = 1.0 / (POS_BASE ** (jnp.arange(quarter, dtype=F32) / quarter))

    def emb(n):
        p = jnp.arange(n, dtype=F32)[:, None] * omega[None, :]
        return jnp.concatenate([jnp.sin(p), jnp.cos(p)], axis=-1)

    return emb(rows), emb(GRID_W)


def kernel(x, c, ctx, c_ctx, ada_w, ada_b, mix_norm_g, ffn_norm_g, a_w_in, a_conv_w, a_conv_b, a_gate_r_w, a_gate_r_b, a_gate_i_w, a_gate_i_b, a_lambda, a_w_out, b_w_in, b_ln_g, b_ln_b, b_w_s, b_b_s, b_w_out, router_w, router_b, moe_w_gu, moe_w_down, shared_w_gu, shared_w_down, final_norm_g):
    bsz, s, d = x.shape
    ctx_len = ctx.shape[1]
    depth = ada_w.shape[0]
    assert depth == 2 and bsz < MOD_ROWS and s % MIX_TILE == 0 and MIX_TILE % TOKEN_TILE == 0
    assert s % SCAN_TILE == 0 and MIX_TILE % GRID_W == 0 and s % IN_TILE == 0 and IN_TILE % GRID_W == 0
    assert ctx_len % TOKEN_TILE == 0
    n = bsz * s
    tps = s // TOKEN_TILE
    ctx_row = bsz

    cc = jnp.zeros((MOD_ROWS, d), F32).at[:bsz].set(c).at[ctx_row].set(c_ctx)
    mod = _modulation(cc, ada_w, ada_b)
    er, ec = _sincos_tables(s // GRID_W, d)
    rc = a_w_in.shape[2] // 2

    w_in0 = a_w_in[0].astype(BF16)
    g_mix0 = mix_norm_g[0].reshape(1, d)
    x2 = x.reshape(n, d)
    gate, ux = _rglru_in(x2, er, ec, mod[0], g_mix0, w_in0)
    uc = _ctx_in(ctx.reshape(bsz * ctx_len, d), mod[0], g_mix0, w_in0[:, rc:], ctx_row)
    w_ri = jnp.concatenate([a_gate_r_w[0], a_gate_i_w[0]], axis=-1).astype(BF16)
    conv = (a_conv_w[0], a_conv_b[0])
    gates = [(w_ri[k], a_gate_r_b[0, k], a_gate_i_b[0, k], a_lambda[0, k]) for k in range(2)]
    h_zero = jnp.zeros((bsz, rc), F32)
    h_fwd, uc = _lru_scan(uc, conv, *gates[0], h_zero, reverse=False, reset_first=True, emit_y=False)
    h_rev = _lru_scan(uc, None, *gates[1], h_zero, reverse=True, reset_first=True, emit_y=False)
    y_fwd, ux = _lru_scan(ux, conv, *gates[0], h_fwd, reverse=False, reset_first=False, emit_y=True)
    y_rev = _lru_scan(ux, None, *gates[1], h_rev, reverse=True, reset_first=False, emit_y=True)
    pre = _rglru_out(y_fwd, y_rev, gate, x2, er, ec, mod[0], a_w_out[0].astype(BF16), ffn_norm_g[0].reshape(1, d),
                     router_w[0].T, router_b[0].reshape(N_EXPERTS, 1), s // MIX_TILE)
    x1 = _moe(*pre, mod[0], moe_w_gu, moe_w_down, 0, shared_w_gu[0].astype(BF16),
              shared_w_down[0].astype(BF16), final_norm_g.reshape(1, d), tps, False)

    pre = _sgu(x1, mod[1], mix_norm_g[1].reshape(1, d), b_w_in[0].astype(BF16),
               b_ln_g[0].reshape(1, -1), b_ln_b[0].reshape(1, -1), b_w_s[0].astype(BF16), b_b_s[0].T,
               b_w_out[0].astype(BF16), ffn_norm_g[1].reshape(1, d),
               router_w[1].T, router_b[1].reshape(N_EXPERTS, 1), s // MIX_TILE)
    out = _moe(*pre, mod[1], moe_w_gu, moe_w_down, 1, shared_w_gu[1].astype(BF16),
               shared_w_down[1].astype(BF16), final_norm_g.reshape(1, d), tps, True)
    return out.reshape(bsz, s, d)
```

```python
import functools

import jax
import jax.numpy as jnp
from jax import lax
from jax.experimental import pallas as pl
from jax.experimental.pallas import tpu as pltpu

F32 = jnp.float32
BF16 = jnp.bfloat16
HIGHEST = lax.Precision.HIGHEST

GRID_W = 64
NORM_EPS = 1e-6
POS_BASE = 10000.0
RNN_HEADS = 5
CONV_WIDTH = 4
CONV_PAD_LEFT = 2
LRU_C = 8.0
SGU_HEADS = 8
CHUNK = 128
N_EXPERTS = 64
TOP_K = 8
N_GROUPS = 8
TOPK_GROUPS = 4
EXPERTS_PER_GROUP = N_EXPERTS // N_GROUPS
ROUTED_SCALE = 2.5

SUBLANES = 8
ROW_ALIGN = 16
LANES = 128
MOD_ROWS = 8
TOKEN_TILE = 256
MIX_TILE = 512
SCAN_TILE = 512
EXPERT_BLOCK = 1024
SORTED_ROWS = TOKEN_TILE * TOP_K + N_EXPERTS * ROW_ALIGN
PERM_CHUNK = 512
DISPATCH_TILES = 2
TAB_WIDTH = 256
TAB_LOCAL = 2 * N_EXPERTS + 2
POS_RADIX = 64
VMEM_LIMIT = 56 * 1024 * 1024


def _params(semantics, vmem=VMEM_LIMIT):
    return pltpu.CompilerParams(dimension_semantics=semantics, vmem_limit_bytes=vmem)


def _silu(x):
    return x * jax.nn.sigmoid(x)


def _rms_mod(x, g, sc, sh):
    y = x * lax.rsqrt(jnp.mean(x * x, axis=-1, keepdims=True) + NORM_EPS)
    return (y * g) * (1.0 + sc) + sh


def _mod_chunk(mod_ref, row, k, d):
    return mod_ref[pl.ds(row, 1), k * d:(k + 1) * d]


def _dot_nt_3pass(a, b):
    dims = (((1,), (1,)), ((), ()))
    a_hi, b_hi = a.astype(BF16), b.astype(BF16)
    a_lo = (a - a_hi.astype(F32)).astype(BF16)
    b_lo = (b - b_hi.astype(F32)).astype(BF16)
    dot = functools.partial(lax.dot_general, dimension_numbers=dims, preferred_element_type=F32)
    return dot(a_hi, b_hi) + dot(a_hi, b_lo) + dot(a_lo, b_hi)


def _round_up_rows(count):
    return jnp.maximum(jnp.ceil(count * (1.0 / ROW_ALIGN)), 1.0) * float(ROW_ALIGN)


def _mod_kernel(cc_ref, w_ref, b_ref, o_ref):
    s = _silu(cc_ref[...])
    o_ref[0] = jnp.dot(s, w_ref[0], preferred_element_type=F32, precision=HIGHEST) + b_ref[0]


def _modulation(cc, ada_w, ada_b):
    depth, d, nd = ada_w.shape
    return pl.pallas_call(
        _mod_kernel,
        grid=(depth, nd // d),
        in_specs=[pl.BlockSpec((MOD_ROWS, d), lambda l, j: (0, 0)),
                  pl.BlockSpec((1, d, d), lambda l, j: (l, 0, j)),
                  pl.BlockSpec((1, 1, d), lambda l, j: (l, 0, j))],
        out_specs=pl.BlockSpec((1, MOD_ROWS, d), lambda l, j: (l, 0, j)),
        out_shape=jax.ShapeDtypeStruct((depth, MOD_ROWS, nd), F32),
        compiler_params=_params(("arbitrary", "arbitrary")),
        name="modulation",
    )(cc, ada_w, ada_b.reshape(depth, 1, nd))


def _with_pos_code(x_ref, er_ref, ec_ref, tile_in_seq):
    t_rows = x_ref.shape[0]
    grid_w, half = ec_ref.shape
    rows = t_rows // grid_w
    r0 = pl.multiple_of(tile_in_seq * rows, rows)
    er = er_ref[pl.ds(r0, rows), :]
    pe = jnp.concatenate([jnp.broadcast_to(er[:, None, :], (rows, grid_w, half)).reshape(t_rows, half),
                          jnp.concatenate([ec_ref[...]] * rows, axis=0)], axis=1)
    return x_ref[...] + pe


def _rglru_in_kernel(x_ref, er_ref, ec_ref, mod_ref, g_ref, w_ref, gate_ref, u_ref, *, tiles_per_seq):
    d = x_ref.shape[1]
    c = u_ref.shape[1]
    row = pl.program_id(0) // tiles_per_seq
    x = _with_pos_code(x_ref, er_ref, ec_ref, pl.program_id(0) % tiles_per_seq)
    hx = _rms_mod(x, g_ref[...], _mod_chunk(mod_ref, row, 1, d), _mod_chunk(mod_ref, row, 0, d))
    z = jnp.dot(hx.astype(BF16), w_ref[...], preferred_element_type=F32)
    gate_ref[...] = jax.nn.gelu(z[:, :c]).astype(BF16)
    u_ref[...] = z[:, c:]


def _rglru_in(x2, er, ec, mod, g, w_in):
    n, d = x2.shape
    c = w_in.shape[1] // 2
    tps = er.shape[0] * ec.shape[0] // MIX_TILE
    row = lambda i: (i, 0)
    return pl.pallas_call(
        functools.partial(_rglru_in_kernel, tiles_per_seq=tps),
        grid=(n // MIX_TILE,),
        in_specs=[pl.BlockSpec((MIX_TILE, d), row),
                  pl.BlockSpec(er.shape, lambda i: (0, 0)),
                  pl.BlockSpec(ec.shape, lambda i: (0, 0)),
                  pl.BlockSpec(mod.shape, lambda i: (0, 0)),
                  pl.BlockSpec((1, d), lambda i: (0, 0)),
                  pl.BlockSpec(w_in.shape, lambda i: (0, 0))],
        out_specs=[pl.BlockSpec((MIX_TILE, c), row),
                   pl.BlockSpec((MIX_TILE, c), row)],
        out_shape=[jax.ShapeDtypeStruct((n, c), BF16),
                   jax.ShapeDtypeStruct((n, c), F32)],
        compiler_params=_params(("arbitrary",)),
        name="rglru_in",
    )(x2, er, ec, mod, g, w_in)


def _ctx_in_kernel(x_ref, mod_ref, g_ref, w_ref, u_ref, *, ctx_row):
    d = x_ref.shape[1]
    hx = _rms_mod(x_ref[...], g_ref[...], _mod_chunk(mod_ref, ctx_row, 1, d),
                  _mod_chunk(mod_ref, ctx_row, 0, d))
    u_ref[...] = jnp.dot(hx.astype(BF16), w_ref[...], preferred_element_type=F32)


def _ctx_in(c2, mod, g, w_u, ctx_row):
    n, d = c2.shape
    c = w_u.shape[1]
    return pl.pallas_call(
        functools.partial(_ctx_in_kernel, ctx_row=ctx_row),
        grid=(n // TOKEN_TILE,),
        in_specs=[pl.BlockSpec((TOKEN_TILE, d), lambda i: (i, 0)),
                  pl.BlockSpec(mod.shape, lambda i: (0, 0)),
                  pl.BlockSpec((1, d), lambda i: (0, 0)),
                  pl.BlockSpec(w_u.shape, lambda i: (0, 0))],
        out_specs=pl.BlockSpec((TOKEN_TILE, c), lambda i: (i, 0)),
        out_shape=jax.ShapeDtypeStruct((n, c), F32),
        compiler_params=_params(("arbitrary",)),
        name="ctx_in",
    )(c2, mod, g, w_u)


def _log_sigmoid(x):
    return jnp.minimum(x, 0.0) - jnp.log1p(jnp.exp(-jnp.abs(x)))


def _lru_scan_kernel(*refs, n_tiles, reverse, conv_done, reset_first, emit_y):
    if conv_done:
        u_ref, wri_ref, rb_ref, ib_ref, lam_ref, h0_ref, out_ref, a_scr, b_scr, h_scr = refs
    else:
        (u_ref, up_ref, un_ref, cw_ref, cb_ref, wri_ref, rb_ref, ib_ref, lam_ref, h0_ref,
         out_ref, uc_ref, ubuf, a_scr, b_scr, h_scr) = refs
    t_rows, c = u_ref.shape
    hb = c // RNN_HEADS
    b = pl.program_id(0)
    j = pl.program_id(1)
    jj = n_tiles - 1 - j if reverse else j

    if conv_done:
        u = u_ref[...]
    else:
        ubuf[SUBLANES:SUBLANES + t_rows, :] = u_ref[...]
        ubuf[0:SUBLANES, :] = jnp.where(jj == 0, 0.0, up_ref[...])
        ubuf[SUBLANES + t_rows:, :] = jnp.where(jj == n_tiles - 1, 0.0, un_ref[...])
        u = cb_ref[...]
        full = ubuf[...]
        for k in range(CONV_WIDTH):
            shift = (CONV_PAD_LEFT - k) % full.shape[0]
            tap = full if shift == 0 else pltpu.roll(full, shift, 0)
            u = u + cw_ref[k:k + 1, :] * tap[SUBLANES:SUBLANES + t_rows, :]
        uc_ref[...] = u

    h = _scan_tile(u, wri_ref, rb_ref, ib_ref, lam_ref, h0_ref, a_scr, b_scr, h_scr, out_ref if emit_y else None,
                   b, j, reverse, reset_first)
    if not emit_y:
        @pl.when(j == n_tiles - 1)
        def _():
            out_ref[pl.ds(b, 1), :] = h


def _scan_tile(u, wri_ref, rb_ref, ib_ref, lam_ref, h0_ref, a_scr, b_scr, h_scr, y_ref, b, j, reverse, reset_first):
    t_rows, c = u.shape
    hb = c // RNN_HEADS
    log_lam = LRU_C * _log_sigmoid(lam_ref[...])
    rows = lax.broadcasted_iota(jnp.int32, (t_rows, 1), 0)
    first_row = jnp.where(j == 0, t_rows - 1 if reverse else 0, -1)
    for h in range(RNN_HEADS):
        sl = slice(h * hb, (h + 1) * hb)
        uh = u[:, sl]
        z = jnp.dot(uh.astype(BF16), wri_ref[h], preferred_element_type=F32)
        r = jax.nn.sigmoid(z[:, :hb] + rb_ref[:, sl])
        ig = jax.nn.sigmoid(z[:, hb:] + ib_ref[:, sl])
        log_a = r * log_lam[:, sl]
        a = jnp.exp(log_a)
        mult = jnp.sqrt((1.0 - a) * (1.0 + a))
        if reset_first:
            mult = jnp.where(rows == first_row, 1.0, mult)
        a_scr[:, sl] = a
        b_scr[:, sl] = mult * ig * uh

    @pl.when(j == 0)
    def _():
        h_scr[...] = h0_ref[pl.ds(b, 1), :]

    n_groups = t_rows // SUBLANES

    def group(g, h):
        base = pl.multiple_of((n_groups - 1 - g if reverse else g) * SUBLANES, SUBLANES)
        for s in range(SUBLANES):
            r = base + (SUBLANES - 1 - s if reverse else s)
            h = a_scr[pl.ds(r, 1), :] * h + b_scr[pl.ds(r, 1), :]
            if y_ref is not None:
                y_ref[pl.ds(r, 1), :] = h
        return h

    h = lax.fori_loop(0, n_groups, group, h_scr[...])
    h_scr[...] = h
    return h


def _lru_scan(u, conv, w_ri, r_b, i_b, lam, h0, *, reverse, reset_first, emit_y):
    n, c = u.shape
    n_batch = h0.shape[0]
    rows = min(SCAN_TILE, n // n_batch)
    n_tiles = n // n_batch // rows
    sub = rows // SUBLANES
    n_sub = n // SUBLANES
    const = lambda b, j: (0, 0)

    def tile(b, j):
        return b * n_tiles + (n_tiles - 1 - j if reverse else j)

    tile_spec = pl.BlockSpec((rows, c), lambda b, j: (tile(b, j), 0))
    in_specs = [tile_spec]
    args = [u]
    scratch = []
    if conv is not None:
        conv_w, conv_b = conv
        in_specs += [pl.BlockSpec((SUBLANES, c), lambda b, j: (jnp.maximum(tile(b, j) * sub - 1, 0), 0)),
                     pl.BlockSpec((SUBLANES, c), lambda b, j: (jnp.minimum((tile(b, j) + 1) * sub, n_sub - 1), 0)),
                     pl.BlockSpec(conv_w.shape, const),
                     pl.BlockSpec((1, c), const)]
        args += [u, u, conv_w, conv_b.reshape(1, c)]
        scratch = [pltpu.VMEM((rows + 2 * SUBLANES, c), F32)]
    in_specs += [pl.BlockSpec(w_ri.shape, lambda b, j: (0, 0, 0)),
                 pl.BlockSpec((1, c), const), pl.BlockSpec((1, c), const), pl.BlockSpec((1, c), const),
                 pl.BlockSpec(h0.shape, const)]
    args += [w_ri, r_b.reshape(1, c), i_b.reshape(1, c), lam.reshape(1, c), h0]
    if emit_y:
        out_specs = [tile_spec]
        out_shape = [jax.ShapeDtypeStruct((n, c), F32)]
    else:
        out_specs = [pl.BlockSpec(h0.shape, const)]
        out_shape = [jax.ShapeDtypeStruct(h0.shape, F32)]
    if conv is not None:
        out_specs.append(tile_spec)
        out_shape.append(jax.ShapeDtypeStruct((n, c), F32))
    name = ("lru_scan" if emit_y else "lru_ctx") + ("_rev" if reverse else "_fwd")
    outs = pl.pallas_call(
        functools.partial(_lru_scan_kernel, n_tiles=n_tiles, reverse=reverse, conv_done=conv is None,
                          reset_first=reset_first, emit_y=emit_y),
        grid=(n_batch, n_tiles),
        in_specs=in_specs,
        out_specs=out_specs,
        out_shape=out_shape,
        scratch_shapes=scratch + [pltpu.VMEM((rows, c), F32),
                                  pltpu.VMEM((rows, c), F32),
                                  pltpu.VMEM((1, c), F32)],
        compiler_params=_params(("arbitrary", "arbitrary")),
        name=name,
    )(*args)
    return outs if conv is not None else outs[0]


def _route(logits, rb):
    e, t = logits.shape
    neg = -jnp.inf
    scores = jax.nn.sigmoid(logits)
    sel = scores + rb
    iota_g = lax.broadcasted_iota(jnp.int32, (N_GROUPS, t), 0).astype(F32)
    iota_e = lax.broadcasted_iota(jnp.int32, (e, t), 0).astype(F32)

    gs = jnp.full((N_GROUPS, t), neg, F32)
    for g in range(N_GROUPS):
        sg = sel[g * EXPERTS_PER_GROUP:(g + 1) * EXPERTS_PER_GROUP, :]
        m1 = jnp.max(sg, axis=0, keepdims=True)
        i1 = jnp.min(jnp.where(sg == m1, iota_g, float(EXPERTS_PER_GROUP)), axis=0, keepdims=True)
        m2 = jnp.max(jnp.where(iota_g == i1, neg, sg), axis=0, keepdims=True)
        gs = jnp.where(iota_g == float(g), m1 + m2, gs)

    keep = jnp.zeros((N_GROUPS, t), F32)
    for _ in range(TOPK_GROUPS):
        m = jnp.max(gs, axis=0, keepdims=True)
        idx = jnp.min(jnp.where(gs == m, iota_g, float(N_GROUPS)), axis=0, keepdims=True)
        hit = iota_g == idx
        keep = jnp.where(hit, 1.0, keep)
        gs = jnp.where(hit, neg, gs)

    masked = jnp.concatenate(
        [jnp.where(keep[g:g + 1, :] > 0.0, sel[g * EXPERTS_PER_GROUP:(g + 1) * EXPERTS_PER_GROUP, :], neg)
         for g in range(N_GROUPS)], axis=0)

    iota_k = lax.broadcasted_iota(jnp.int32, (TOP_K, t), 0)
    selmask = jnp.zeros((e, t), F32)
    eidx = jnp.zeros((TOP_K, t), F32)
    gw = jnp.zeros((TOP_K, t), F32)
    for k in range(TOP_K):
        m = jnp.max(masked, axis=0, keepdims=True)
        idx = jnp.min(jnp.where(masked == m, iota_e, float(e)), axis=0, keepdims=True)
        hit = iota_e == idx
        gk = jnp.sum(jnp.where(hit, scores, 0.0), axis=0, keepdims=True)
        masked = jnp.where(hit, neg, masked)
        selmask = jnp.where(hit, 1.0, selmask)
        eidx = jnp.where(iota_k == k, idx, eidx)
        gw = jnp.where(iota_k == k, gk, gw)
    gw = gw / jnp.sum(gw, axis=0, keepdims=True) * ROUTED_SCALE
    return eidx, gw, selmask


def _ffn_pre(x1, mod_ref, row, g2_ref, rwt_ref, rb_ref, hx2_ref, hi_ref, lo_ref, hit_ref, lot_ref, gatet_ref,
             tab_ref, tabt_ref):
    d = x1.shape[1]
    hx2 = _rms_mod(x1, g2_ref[...], _mod_chunk(mod_ref, row, 4, d), _mod_chunk(mod_ref, row, 3, d))
    hx2_ref[...] = hx2.astype(BF16)
    logits = _dot_nt_3pass(rwt_ref[...], hx2)
    eidx, gw, selmask = _route(logits, rb_ref[...])
    for s in range(selmask.shape[1] // TOKEN_TILE):
        ts = slice(s * TOKEN_TILE, (s + 1) * TOKEN_TILE)
        hi, lo, gate, tab, tabt = _sorted_layout(selmask[:, ts], eidx[:, ts], gw[:, ts])
        hi_ref[:, ts] = hi.astype(BF16)
        lo_ref[:, ts] = lo.astype(BF16)
        hit_ref[ts, :] = hi.T.astype(BF16)
        lot_ref[ts, :] = lo.T.astype(BF16)
        gatet_ref[ts, :] = gate.T.astype(BF16)
        tab_ref[s] = tab
        tabt_ref[s] = tabt


def _ffn_pre_specs(n, d, tile=lambda i: i):
    row = lambda *g: (tile(*g), 0)
    col = lambda *g: (0, tile(*g))
    sub = MIX_TILE // TOKEN_TILE
    lead = lambda *g: (tile(*g), 0, 0)
    out_specs = [pl.BlockSpec((MIX_TILE, d), row),
                 pl.BlockSpec((MIX_TILE, d), row),
                 pl.BlockSpec((N_EXPERTS, MIX_TILE), col),
                 pl.BlockSpec((N_EXPERTS, MIX_TILE), col),
                 pl.BlockSpec((MIX_TILE, N_EXPERTS), row),
                 pl.BlockSpec((MIX_TILE, N_EXPERTS), row),
                 pl.BlockSpec((MIX_TILE, N_EXPERTS), row),
                 pl.BlockSpec((sub, N_EXPERTS, LANES), lead),
                 pl.BlockSpec((sub, SUBLANES, LANES), lead)]
    out_shape = [jax.ShapeDtypeStruct((n, d), F32),
                 jax.ShapeDtypeStruct((n, d), BF16),
                 jax.ShapeDtypeStruct((N_EXPERTS, n), BF16),
                 jax.ShapeDtypeStruct((N_EXPERTS, n), BF16),
                 jax.ShapeDtypeStruct((n, N_EXPERTS), BF16),
                 jax.ShapeDtypeStruct((n, N_EXPERTS), BF16),
                 jax.ShapeDtypeStruct((n, N_EXPERTS), BF16),
                 jax.ShapeDtypeStruct((n // TOKEN_TILE, N_EXPERTS, LANES), F32),
                 jax.ShapeDtypeStruct((n // TOKEN_TILE, SUBLANES, LANES), F32)]
    return out_specs, out_shape


def _rglru_out_kernel(u_ref, wri_ref, grb_ref, gib_ref, lam_ref, h0_ref, yf_ref, gate_ref, x_ref, er_ref, ec_ref,
                      mod_ref, wout_ref, g2_ref, rwt_ref, rb_ref,
                      x1_ref, hx2_ref, hi_ref, lo_ref, hit_ref, lot_ref, gatet_ref, tab_ref, tabt_ref,
                      a_scr, b_scr, h_scr, yr_scr, *, n_tiles):
    d = x_ref.shape[1]
    row = pl.program_id(0)
    j = pl.program_id(1)
    tile_in_seq = n_tiles - 1 - j
    _scan_tile(u_ref[...], wri_ref, grb_ref, gib_ref, lam_ref, h0_ref, a_scr, b_scr, h_scr, yr_scr,
               row, j, True, False)
    yx = yf_ref[...] + yr_scr[...]
    v = gate_ref[...].astype(F32) * yx
    out = jnp.dot(v.astype(BF16), wout_ref[...], preferred_element_type=F32)
    x1 = _with_pos_code(x_ref, er_ref, ec_ref, tile_in_seq) + _mod_chunk(mod_ref, row, 2, d) * out
    x1_ref[...] = x1
    _ffn_pre(x1, mod_ref, row, g2_ref, rwt_ref, rb_ref, hx2_ref, hi_ref, lo_ref, hit_ref, lot_ref, gatet_ref,
             tab_ref, tabt_ref)


def _rglru_out(u_conv, w_ri, r_b, i_b, lam, h0, y_fwd, gate, x, er, ec, mod, w_out, g2, rwt, rb):
    n, d = x.shape
    c = gate.shape[1]
    n_batch = h0.shape[0]
    n_tiles = n // n_batch // MIX_TILE
    tile = lambda b, j: b * n_tiles + n_tiles - 1 - j
    row = lambda b, j: (tile(b, j), 0)
    const = lambda b, j: (0, 0)
    out_specs, out_shape = _ffn_pre_specs(n, d, tile)
    return pl.pallas_call(
        functools.partial(_rglru_out_kernel, n_tiles=n_tiles),
        grid=(n_batch, n_tiles),
        in_specs=[pl.BlockSpec((MIX_TILE, c), row),
                  pl.BlockSpec(w_ri.shape, lambda b, j: (0, 0, 0)),
                  pl.BlockSpec((1, c), const), pl.BlockSpec((1, c), const), pl.BlockSpec((1, c), const),
                  pl.BlockSpec(h0.shape, const),
                  pl.BlockSpec((MIX_TILE, c), row),
                  pl.BlockSpec((MIX_TILE, c), row),
                  pl.BlockSpec((MIX_TILE, d), row),
                  pl.BlockSpec(er.shape, const),
                  pl.BlockSpec(ec.shape, const),
                  pl.BlockSpec(mod.shape, const),
                  pl.BlockSpec(w_out.shape, const),
                  pl.BlockSpec((1, d), const),
                  pl.BlockSpec(rwt.shape, const),
                  pl.BlockSpec(rb.shape, const)],
        out_specs=out_specs,
        out_shape=out_shape,
        scratch_shapes=[pltpu.VMEM((MIX_TILE, c), F32), pltpu.VMEM((MIX_TILE, c), F32),
                        pltpu.VMEM((1, c), F32), pltpu.VMEM((MIX_TILE, c), F32)],
        compiler_params=_params(("arbitrary", "arbitrary")),
        name="rglru_rev_out",
    )(u_conv, w_ri, r_b.reshape(1, c), i_b.reshape(1, c), lam.reshape(1, c), h0, y_fwd, gate, x, er, ec, mod,
      w_out, g2, rwt, rb)


def _sgu_kernel(x_ref, mod_ref, g_ref, win_ref, lng_ref, lnb_ref, ws_ref, bst_ref, wout_ref,
                g2_ref, rwt_ref, rb_ref,
                x1_ref, hx2_ref, hi_ref, lo_ref, hit_ref, lot_ref, gatet_ref, tab_ref, tabt_ref, m_scr, *,
                tiles_per_seq):
    t_rows, d = x_ref.shape
    w = wout_ref.shape[0]
    gd = w // SGU_HEADS
    row = pl.program_id(0) // tiles_per_seq
    x = x_ref[...]
    hx = _rms_mod(x, g_ref[...], _mod_chunk(mod_ref, row, 1, d), _mod_chunk(mod_ref, row, 0, d))
    z = jax.nn.gelu(jnp.dot(hx.astype(BF16), win_ref[...], preferred_element_type=F32))
    u = z[:, :w]
    v = z[:, w:]
    mu = jnp.mean(v, axis=-1, keepdims=True)
    vc = v - mu
    v = vc * lax.rsqrt(jnp.mean(vc * vc, axis=-1, keepdims=True) + NORM_EPS) * lng_ref[...] + lnb_ref[...]
    vb = v.astype(BF16)
    for ch in range(t_rows // CHUNK):
        rs = slice(ch * CHUNK, (ch + 1) * CHUNK)
        for g in range(SGU_HEADS):
            cs = slice(g * gd, (g + 1) * gd)
            sv = jnp.dot(ws_ref[g], vb[rs, cs], preferred_element_type=F32) + bst_ref[:, g:g + 1]
            m_scr[rs, cs] = (u[rs, cs] * sv).astype(BF16)
    out = jnp.dot(m_scr[...], wout_ref[...], preferred_element_type=F32)
    x1 = x + _mod_chunk(mod_ref, row, 2, d) * out
    x1_ref[...] = x1
    _ffn_pre(x1, mod_ref, row, g2_ref, rwt_ref, rb_ref, hx2_ref, hi_ref, lo_ref, hit_ref, lot_ref, gatet_ref,
             tab_ref, tabt_ref)


def _sgu(x, mod, g, w_in, ln_g, ln_b, w_s, b_st, w_out, g2, rwt, rb, tiles_per_seq):
    n, d = x.shape
    w = w_out.shape[0]
    const = lambda i: (0, 0)
    out_specs, out_shape = _ffn_pre_specs(n, d)
    return pl.pallas_call(
        functools.partial(_sgu_kernel, tiles_per_seq=tiles_per_seq),
        grid=(n // MIX_TILE,),
        in_specs=[pl.BlockSpec((MIX_TILE, d), lambda i: (i, 0)),
                  pl.BlockSpec(mod.shape, const),
                  pl.BlockSpec((1, d), const),
                  pl.BlockSpec(w_in.shape, const, pipeline_mode=pl.Buffered(1)),
                  pl.BlockSpec((1, w), const),
                  pl.BlockSpec((1, w), const),
                  pl.BlockSpec(w_s.shape, lambda i: (0, 0, 0)),
                  pl.BlockSpec(b_st.shape, const),
                  pl.BlockSpec(w_out.shape, const, pipeline_mode=pl.Buffered(1)),
                  pl.BlockSpec((1, d), const),
                  pl.BlockSpec(rwt.shape, const),
                  pl.BlockSpec(rb.shape, const)],
        out_specs=out_specs,
        out_shape=out_shape,
        scratch_shapes=[pltpu.VMEM((MIX_TILE, w), BF16)],
        compiler_params=_params(("arbitrary",)),
        name="sgu",
    )(x, mod, g, w_in, ln_g, ln_b, w_s, b_st, w_out, g2, rwt, rb)


def _sorted_layout(m, eidx, gw):
    e, t = m.shape
    lanes = LANES
    r = lax.broadcasted_iota(jnp.int32, (t, t), 0)
    c = lax.broadcasted_iota(jnp.int32, (t, t), 1)
    upper = jnp.where(r <= c, 1.0, 0.0).astype(BF16)
    incl = jnp.dot(m.astype(BF16), upper, preferred_element_type=F32)
    run = _round_up_rows(incl[:, t - 1:t])
    re = lax.broadcasted_iota(jnp.int32, (e, e), 0)
    ce = lax.broadcasted_iota(jnp.int32, (e, e), 1)
    lower = jnp.where(ce < re, 1.0, 0.0).astype(BF16)
    tiles = jnp.broadcast_to(run * (1.0 / ROW_ALIGN), (e, lanes)).astype(BF16)
    lstart = jnp.dot(lower, tiles, preferred_element_type=F32)[:, 0:1] * float(ROW_ALIGN)
    pos = jnp.where(m > 0.0, lstart + incl - m, float(POS_RADIX * POS_RADIX - 1))
    hi = jnp.floor(pos * (1.0 / POS_RADIX))
    lo = pos - hi * float(POS_RADIX)
    iota_e = lax.broadcasted_iota(jnp.int32, (e, t), 0).astype(F32)
    gate = jnp.zeros((e, t), F32)
    for k in range(TOP_K):
        gate = jnp.where(iota_e == eidx[k:k + 1, :], gw[k:k + 1, :], gate)
    lane = lax.broadcasted_iota(jnp.int32, (e, lanes), 1)
    tab = jnp.where(lane == 0, run, jnp.where(lane == 2, lstart, 0.0))
    diag = lax.broadcasted_iota(jnp.int32, (e, lanes), 0) == lane
    row_of = lambda col: jnp.sum(jnp.where(diag, jnp.broadcast_to(col, (e, lanes)), 0.0), axis=0, keepdims=True)
    sub = lax.broadcasted_iota(jnp.int32, (SUBLANES, lanes), 0)
    tabt = jnp.where(sub == 0, row_of(lstart), jnp.where(sub == 1, row_of(lstart + run), 0.0))
    return hi, lo, gate, tab, tabt


def _aligned(v):
    return pl.multiple_of(v, ROW_ALIGN)


def _run_copies(tab_ref, make_copy, unroll, base=0):
    def body(e, carry):
        make_copy(_aligned(tab_ref[base + TAB_LOCAL + e]), _aligned(tab_ref[base + N_EXPERTS + e]),
                  _aligned(tab_ref[base + e])).start()
        return carry

    lax.fori_loop(0, N_EXPERTS, body, 0, unroll=unroll)


def _dispatch_kernel(tab_ref, ztab_ref, tabt_ref, hi_ref, lo_ref, x_ref, xs_hbm, sbuf, zbuf, sems, zsem):
    i = pl.program_id(0)
    tiles = sbuf.shape[0]
    t = x_ref.shape[0] // tiles
    rows = sbuf.shape[1]

    def zero_copy(e):
        n = _aligned(ztab_ref[e])
        return pltpu.make_async_copy(zbuf.at[pl.ds(0, n)], xs_hbm.at[pl.ds(_aligned(ztab_ref[N_EXPERTS + e]), n)], zsem)

    def for_zero_runs(fn):
        def body(e, c):
            @pl.when(ztab_ref[e] > 0)
            def _():
                fn(zero_copy(e))
            return c
        lax.fori_loop(0, N_EXPERTS, body, 0)

        def tail(b, c):
            fn(pltpu.make_async_copy(zbuf, xs_hbm.at[pl.ds(pl.multiple_of(b * EXPERT_BLOCK, EXPERT_BLOCK),
                                                           EXPERT_BLOCK)], zsem))
            return c
        lax.fori_loop(ztab_ref[2 * N_EXPERTS], xs_hbm.shape[0] // EXPERT_BLOCK, tail, 0)

    @pl.when(i == 0)
    def _():
        zbuf[...] = jnp.zeros_like(zbuf)
        for_zero_runs(lambda cp: cp.start())

    def wait_rows(s, n):
        pltpu.make_async_copy(sbuf.at[s, pl.ds(0, n)], xs_hbm.at[pl.ds(0, n)], sems.at[s]).wait()

    def permute(s, r0):
        ts = slice(s * t, (s + 1) * t)
        j_e = (lax.broadcasted_iota(jnp.int32, (PERM_CHUNK, N_EXPERTS), 0) + r0).astype(F32)
        owner = jnp.where((j_e >= tabt_ref[s, 0:1, 0:N_EXPERTS]) & (j_e < tabt_ref[s, 1:2, 0:N_EXPERTS]),
                          1.0, 0.0).astype(BF16)
        pos = (jnp.dot(owner, hi_ref[:, ts], preferred_element_type=F32) * float(POS_RADIX)
               + jnp.dot(owner, lo_ref[:, ts], preferred_element_type=F32))
        j_t = (lax.broadcasted_iota(jnp.int32, (PERM_CHUNK, t), 0) + r0).astype(F32)
        p = jnp.where(pos == j_t, 1.0, 0.0).astype(BF16)
        sbuf[s, r0:r0 + PERM_CHUNK, :] = jnp.dot(p, x_ref[ts, :], preferred_element_type=F32).astype(BF16)

    for s in range(tiles):
        base = s * TAB_WIDTH

        @pl.when(i > 0)
        def _():
            wait_rows(s, _aligned(tab_ref[base + 2 * N_EXPERTS + 1]))

        for r0 in range(0, rows, PERM_CHUNK):
            if r0 < t * TOP_K + PERM_CHUNK:
                permute(s, r0)
            else:
                pl.when(tab_ref[base + 2 * N_EXPERTS] > r0)(functools.partial(permute, s, r0))

        _run_copies(tab_ref, lambda loc, glob, n: pltpu.make_async_copy(
            sbuf.at[s, pl.ds(loc, n)], xs_hbm.at[pl.ds(glob, n)], sems.at[s]), unroll=True, base=base)

    @pl.when(i == pl.num_programs(0) - 1)
    def _():
        for s in range(tiles):
            wait_rows(s, _aligned(tab_ref[s * TAB_WIDTH + 2 * N_EXPERTS]))

    @pl.when(i == 0)
    def _():
        for_zero_runs(lambda cp: cp.wait())


def _dispatch(tab, ztab, tab_t, pos_hi, pos_lo, hx2, xs_rows):
    n, d = hx2.shape
    col = lambda i: (0, i)
    return pl.pallas_call(
        _dispatch_kernel,
        grid=(n // (DISPATCH_TILES * TOKEN_TILE),),
        in_specs=[pl.BlockSpec((DISPATCH_TILES * TAB_WIDTH,), lambda i: (i,), memory_space=pltpu.SMEM),
                  pl.BlockSpec(memory_space=pltpu.SMEM),
                  pl.BlockSpec((DISPATCH_TILES, SUBLANES, LANES), lambda i: (i, 0, 0)),
                  pl.BlockSpec((N_EXPERTS, DISPATCH_TILES * TOKEN_TILE), col),
                  pl.BlockSpec((N_EXPERTS, DISPATCH_TILES * TOKEN_TILE), col),
                  pl.BlockSpec((DISPATCH_TILES * TOKEN_TILE, d), lambda i: (i, 0))],
        out_specs=pl.BlockSpec(memory_space=pl.ANY),
        out_shape=jax.ShapeDtypeStruct((xs_rows, d), BF16),
        scratch_shapes=[pltpu.VMEM((DISPATCH_TILES, SORTED_ROWS, d), BF16),
                        pltpu.VMEM((EXPERT_BLOCK, d), BF16),
                        pltpu.SemaphoreType.DMA((DISPATCH_TILES,)),
                        pltpu.SemaphoreType.DMA],
        compiler_params=_params(("arbitrary",)),
        name="moe_dispatch",
    )(tab, ztab, tab_t, pos_hi, pos_lo, hx2)


def _experts_kernel(be_ref, nb_ref, xs_ref, wgu_ref, wd_ref, ys_ref, wgu_b, wd_b):
    i = pl.program_id(0)
    ff = wd_b.shape[0]
    used = i < nb_ref[0]
    new_expert = (i == 0) | (be_ref[i] != be_ref[jnp.maximum(i - 1, 0)])

    @pl.when(used & new_expert)
    def _():
        wgu_b[...] = wgu_ref[0, 0].astype(BF16)
        wd_b[...] = wd_ref[0, 0].astype(BF16)

    @pl.when(used)
    def _():
        h = jnp.dot(xs_ref[...], wgu_b[...], preferred_element_type=F32)
        a = _silu(h[:, :ff]) * h[:, ff:]
        ys_ref[...] = jnp.dot(a.astype(BF16), wd_b[...], preferred_element_type=F32).astype(BF16)


def _experts(block_e, n_used, xs, w_gu, w_down, layer):
    rows, d = xs.shape
    nb = rows // EXPERT_BLOCK
    blk = lambda i, be, nu: (jnp.maximum(jnp.minimum(i, nu[0] - 1), 0), 0)
    wmap = lambda i, be, nu: (layer, be[i], 0, 0)
    return pl.pallas_call(
        _experts_kernel,
        grid_spec=pltpu.PrefetchScalarGridSpec(
            num_scalar_prefetch=2,
            grid=(nb,),
            in_specs=[pl.BlockSpec((EXPERT_BLOCK, d), blk),
                      pl.BlockSpec((1, 1) + w_gu.shape[2:], wmap),
                      pl.BlockSpec((1, 1) + w_down.shape[2:], wmap)],
            out_specs=pl.BlockSpec((EXPERT_BLOCK, d), blk),
            scratch_shapes=[pltpu.VMEM(w_gu.shape[2:], BF16), pltpu.VMEM(w_down.shape[2:], BF16)]),
        out_shape=jax.ShapeDtypeStruct((rows, d), BF16),
        input_output_aliases={2: 0},
        compiler_params=_params(("arbitrary",)),
        name="moe_experts",
    )(block_e, n_used, xs, w_gu, w_down)


def _combine_kernel(tab_ref, tabn_ref, tabv_ref, hi_ref, lo_ref, gate_ref, ys_hbm, hx2_ref, x1_ref, mod_ref,
                    wsgu_ref, wsd_ref, fg_ref, out_ref, ybuf, sems, *, tiles_per_seq, final_norm):
    t_rows, d = x1_ref.shape
    ff = wsd_ref.shape[0]
    rows = ybuf.shape[1]
    i = pl.program_id(0)
    last = pl.num_programs(0) - 1
    slot = i % 2
    row = i // tiles_per_seq

    def gather(table, s, unroll):
        _run_copies(table, lambda loc, glob, n: pltpu.make_async_copy(
            ys_hbm.at[pl.ds(glob, n)], ybuf.at[s, pl.ds(loc, n)], sems.at[s]), unroll=unroll)

    def wait_rows(table, s):
        n = _aligned(table[2 * N_EXPERTS])
        pltpu.make_async_copy(ys_hbm.at[pl.ds(0, n)], ybuf.at[s, pl.ds(0, n)], sems.at[s]).wait()

    @pl.when(i == 0)
    def _():
        ybuf[...] = jnp.zeros_like(ybuf)
        gather(tab_ref, 0, False)

    gather(tabn_ref, 1 - slot, True)

    hs = jnp.dot(hx2_ref[...], wsgu_ref[...], preferred_element_type=F32)
    shared = jnp.dot((_silu(hs[:, :ff]) * hs[:, ff:]).astype(BF16), wsd_ref[...], preferred_element_type=F32)

    run_lo = tabv_ref[0, :, 2:3]
    run_hi = run_lo + tabv_ref[0, :, 0:1]
    j_e = lax.broadcasted_iota(jnp.int32, (N_EXPERTS, rows), 1).astype(F32)
    owner = jnp.where((j_e >= run_lo) & (j_e < run_hi), 1.0, 0.0).astype(BF16)
    pos = (jnp.dot(hi_ref[...], owner, preferred_element_type=F32) * float(POS_RADIX)
           + jnp.dot(lo_ref[...], owner, preferred_element_type=F32))
    gates = jnp.dot(gate_ref[...], owner, preferred_element_type=F32)
    j_t = lax.broadcasted_iota(jnp.int32, (t_rows, rows), 1).astype(F32)
    gb = jnp.where(pos == j_t, gates, 0.0).astype(BF16)

    wait_rows(tab_ref, slot)
    routed = jnp.dot(gb, ybuf[slot], preferred_element_type=F32)
    x2 = x1_ref[...] + _mod_chunk(mod_ref, row, 5, d) * (routed + shared)
    if final_norm:
        x2 = x2 * lax.rsqrt(jnp.mean(x2 * x2, axis=-1, keepdims=True) + NORM_EPS) * fg_ref[...]
    out_ref[...] = x2

    @pl.when(i == last)
    def _():
        wait_rows(tabn_ref, 1 - slot)


def _combine(tab, tab_v, pos_hi, pos_lo, gate, ys, hx2, x1, mod, ws_gu, ws_down, fg, tiles_per_seq, final_norm):
    n, d = x1.shape
    row = lambda i: (i, 0)
    const = lambda i: (0, 0)
    return pl.pallas_call(
        functools.partial(_combine_kernel, tiles_per_seq=tiles_per_seq, final_norm=final_norm),
        grid=(n // TOKEN_TILE,),
        in_specs=[pl.BlockSpec((TAB_WIDTH,), lambda i: (i,), memory_space=pltpu.SMEM),
                  pl.BlockSpec((TAB_WIDTH,), lambda i: (jnp.minimum(i + 1, n // TOKEN_TILE - 1),),
                               memory_space=pltpu.SMEM),
                  pl.BlockSpec((1, N_EXPERTS, LANES), lambda i: (i, 0, 0)),
                  pl.BlockSpec((TOKEN_TILE, N_EXPERTS), row),
                  pl.BlockSpec((TOKEN_TILE, N_EXPERTS), row),
                  pl.BlockSpec((TOKEN_TILE, N_EXPERTS), row),
                  pl.BlockSpec(memory_space=pl.ANY),
                  pl.BlockSpec((TOKEN_TILE, d), row),
                  pl.BlockSpec((TOKEN_TILE, d), row),
                  pl.BlockSpec(mod.shape, const),
                  pl.BlockSpec(ws_gu.shape, const),
                  pl.BlockSpec(ws_down.shape, const),
                  pl.BlockSpec((1, d), const)],
        out_specs=pl.BlockSpec((TOKEN_TILE, d), row),
        out_shape=jax.ShapeDtypeStruct((n, d), F32),
        scratch_shapes=[pltpu.VMEM((2, SORTED_ROWS, d), BF16), pltpu.SemaphoreType.DMA((2,))],
        compiler_params=_params(("arbitrary",)),
        name="moe_combine",
    )(tab, tab, tab_v, pos_hi, pos_lo, gate, ys, hx2, x1, mod, ws_gu, ws_down, fg)


def _moe(x1, hx2, pos_hi, pos_lo, pos_hi_t, pos_lo_t, gate_t, tab_f, tab_t, mod, w_gu, w_down, layer, ws_gu,
         ws_down, fg, tiles_per_seq, final_norm):
    n, d = x1.shape
    n_tiles = n // TOKEN_TILE
    run_len = tab_f[:, :, 0].astype(jnp.int32)
    counts = jnp.sum(run_len, axis=0)
    padded = (counts + EXPERT_BLOCK - 1) // EXPERT_BLOCK * EXPERT_BLOCK
    pad_end = jnp.cumsum(padded)
    pad_start = pad_end - padded
    max_rows = n * TOP_K + n_tiles * N_EXPERTS * ROW_ALIGN + N_EXPERTS * (EXPERT_BLOCK - ROW_ALIGN)
    n_blocks = (max_rows + EXPERT_BLOCK - 1) // EXPERT_BLOCK
    n_used = (pad_end[-1:] // EXPERT_BLOCK).astype(jnp.int32)
    block_start = jnp.arange(n_blocks, dtype=jnp.int32) * EXPERT_BLOCK
    block_e = jnp.minimum(jnp.sum((pad_end[None, :] <= block_start[:, None]).astype(jnp.int32), axis=1),
                          N_EXPERTS - 1)

    run_start = pad_start[None, :] + jnp.cumsum(run_len, axis=0) - run_len
    total = jnp.sum(run_len, axis=1, keepdims=True)
    prev_total = jnp.concatenate([jnp.zeros((DISPATCH_TILES, 1), jnp.int32), total[:-DISPATCH_TILES]], axis=0)
    run_local = tab_f[:, :, 2].astype(jnp.int32)
    fill = jnp.zeros((n_tiles, TAB_WIDTH - 3 * N_EXPERTS - 2), jnp.int32)
    tab = jnp.concatenate([run_len, run_start, total, prev_total, run_local, fill], axis=1).reshape(-1)
    ztab = jnp.concatenate([padded - counts, pad_start + counts, n_used])

    xs = _dispatch(tab, ztab, tab_t, pos_hi, pos_lo, hx2, n_blocks * EXPERT_BLOCK)
    ys = _experts(block_e, n_used, xs, w_gu, w_down, layer)
    return _combine(tab, tab_f, pos_hi_t, pos_lo_t, gate_t, ys, hx2, x1, mod, ws_gu, ws_down, fg,
                    tiles_per_seq, final_norm)


def _sincos_tables(rows, d):
    quarter = d // 4
    omega = 1.0 / (POS_BASE ** (jnp.arange(quarter, dtype=F32) / quarter))

    def emb(n):
        p = jnp.arange(n, dtype=F32)[:, None] * omega[None, :]
        return jnp.concatenate([jnp.sin(p), jnp.cos(p)], axis=-1)

    return emb(rows), emb(GRID_W)


def kernel(x, c, ctx, c_ctx, ada_w, ada_b, mix_norm_g, ffn_norm_g, a_w_in, a_conv_w, a_conv_b, a_gate_r_w, a_gate_r_b, a_gate_i_w, a_gate_i_b, a_lambda, a_w_out, b_w_in, b_ln_g, b_ln_b, b_w_s, b_b_s, b_w_out, router_w, router_b, moe_w_gu, moe_w_down, shared_w_gu, shared_w_down, final_norm_g):
    bsz, s, d = x.shape
    ctx_len = ctx.shape[1]
    depth = ada_w.shape[0]
    assert depth == 2 and bsz < MOD_ROWS and s % MIX_TILE == 0 and MIX_TILE % TOKEN_TILE == 0
    assert s % SCAN_TILE == 0 and MIX_TILE % GRID_W == 0
    assert ctx_len % TOKEN_TILE == 0
    n = bsz * s
    tps = s // TOKEN_TILE
    ctx_row = bsz

    cc = jnp.zeros((MOD_ROWS, d), F32).at[:bsz].set(c).at[ctx_row].set(c_ctx)
    mod = _modulation(cc, ada_w, ada_b)
    er, ec = _sincos_tables(s // GRID_W, d)
    rc = a_w_in.shape[2] // 2

    w_in0 = a_w_in[0].astype(BF16)
    g_mix0 = mix_norm_g[0].reshape(1, d)
    x2 = x.reshape(n, d)
    gate, ux = _rglru_in(x2, er, ec, mod[0], g_mix0, w_in0)
    uc = _ctx_in(ctx.reshape(bsz * ctx_len, d), mod[0], g_mix0, w_in0[:, rc:], ctx_row)
    w_ri = jnp.concatenate([a_gate_r_w[0], a_gate_i_w[0]], axis=-1).astype(BF16)
    conv = (a_conv_w[0], a_conv_b[0])
    gates = [(w_ri[k], a_gate_r_b[0, k], a_gate_i_b[0, k], a_lambda[0, k]) for k in range(2)]
    h_zero = jnp.zeros((bsz, rc), F32)
    h_fwd, uc = _lru_scan(uc, conv, *gates[0], h_zero, reverse=False, reset_first=True, emit_y=False)
    h_rev = _lru_scan(uc, None, *gates[1], h_zero, reverse=True, reset_first=True, emit_y=False)
    y_fwd, ux = _lru_scan(ux, conv, *gates[0], h_fwd, reverse=False, reset_first=False, emit_y=True)
    pre = _rglru_out(ux, *gates[1], h_rev, y_fwd, gate, x2, er, ec, mod[0], a_w_out[0].astype(BF16),
                     ffn_norm_g[0].reshape(1, d), router_w[0].T, router_b[0].reshape(N_EXPERTS, 1))
    x1 = _moe(*pre, mod[0], moe_w_gu, moe_w_down, 0, shared_w_gu[0].astype(BF16),
              shared_w_down[0].astype(BF16), final_norm_g.reshape(1, d), tps, False)

    pre = _sgu(x1, mod[1], mix_norm_g[1].reshape(1, d), b_w_in[0].astype(BF16),
               b_ln_g[0].reshape(1, -1), b_ln_b[0].reshape(1, -1), b_w_s[0].astype(BF16), b_b_s[0].T,
               b_w_out[0].astype(BF16), ffn_norm_g[1].reshape(1, d),
               router_w[1].T, router_b[1].reshape(N_EXPERTS, 1), s // MIX_TILE)
    out = _moe(*pre, mod[1], moe_w_gu, moe_w_down, 1, shared_w_gu[1].astype(BF16),
               shared_w_down[1].astype(BF16), final_norm_g.reshape(1, d), tps, True)
    return out.reshape(bsz, s, d)
```

```python
import functools

import jax
import jax.numpy as jnp
from jax import lax
from jax.experimental import pallas as pl
from jax.experimental.pallas import tpu as pltpu

F32 = jnp.float32
BF16 = jnp.bfloat16
HIGHEST = lax.Precision.HIGHEST

GRID_W = 64
NORM_EPS = 1e-6
POS_BASE = 10000.0
RNN_HEADS = 5
CONV_WIDTH = 4
CONV_PAD_LEFT = 2
LRU_C = 8.0
SGU_HEADS = 8
CHUNK = 128
N_EXPERTS = 64
TOP_K = 8
N_GROUPS = 8
TOPK_GROUPS = 4
EXPERTS_PER_GROUP = N_EXPERTS // N_GROUPS
ROUTED_SCALE = 2.5

SUBLANES = 8
ROW_ALIGN = 16
LANES = 128
MOD_ROWS = 8
TOKEN_TILE = 256
MIX_TILE = 512
SCAN_TILE = 512
EXPERT_BLOCK = 1024
SORTED_ROWS = TOKEN_TILE * TOP_K + N_EXPERTS * ROW_ALIGN
PERM_CHUNK = 512
DISPATCH_TILES = 2
TAB_WIDTH = 256
TAB_LOCAL = 2 * N_EXPERTS + 2
POS_RADIX = 64
VMEM_LIMIT = 56 * 1024 * 1024


def _params(semantics, vmem=VMEM_LIMIT):
    return pltpu.CompilerParams(dimension_semantics=semantics, vmem_limit_bytes=vmem)


def _silu(x):
    return x * jax.nn.sigmoid(x)


def _rms_mod(x, g, sc, sh):
    y = x * lax.rsqrt(jnp.mean(x * x, axis=-1, keepdims=True) + NORM_EPS)
    return (y * g) * (1.0 + sc) + sh


def _mod_chunk(mod_ref, row, k, d):
    return mod_ref[pl.ds(row, 1), k * d:(k + 1) * d]


def _dot_nt_3pass(a, b):
    dims = (((1,), (1,)), ((), ()))
    a_hi, b_hi = a.astype(BF16), b.astype(BF16)
    a_lo = (a - a_hi.astype(F32)).astype(BF16)
    b_lo = (b - b_hi.astype(F32)).astype(BF16)
    dot = functools.partial(lax.dot_general, dimension_numbers=dims, preferred_element_type=F32)
    return dot(a_hi, b_hi) + dot(a_hi, b_lo) + dot(a_lo, b_hi)


def _round_up_rows(count):
    return jnp.maximum(jnp.ceil(count * (1.0 / ROW_ALIGN)), 1.0) * float(ROW_ALIGN)


def _mod_kernel(cc_ref, w_ref, b_ref, o_ref):
    s = _silu(cc_ref[...])
    o_ref[0] = jnp.dot(s, w_ref[0], preferred_element_type=F32, precision=HIGHEST) + b_ref[0]


def _modulation(cc, ada_w, ada_b):
    depth, d, nd = ada_w.shape
    return pl.pallas_call(
        _mod_kernel,
        grid=(depth, nd // d),
        in_specs=[pl.BlockSpec((MOD_ROWS, d), lambda l, j: (0, 0)),
                  pl.BlockSpec((1, d, d), lambda l, j: (l, 0, j)),
                  pl.BlockSpec((1, 1, d), lambda l, j: (l, 0, j))],
        out_specs=pl.BlockSpec((1, MOD_ROWS, d), lambda l, j: (l, 0, j)),
        out_shape=jax.ShapeDtypeStruct((depth, MOD_ROWS, nd), F32),
        compiler_params=_params(("arbitrary", "arbitrary")),
        name="modulation",
    )(cc, ada_w, ada_b.reshape(depth, 1, nd))


def _with_pos_code(x_ref, er_ref, ec_ref, tile_in_seq):
    t_rows = x_ref.shape[0]
    grid_w, half = ec_ref.shape
    rows = t_rows // grid_w
    r0 = pl.multiple_of(tile_in_seq * rows, rows)
    er = er_ref[pl.ds(r0, rows), :]
    pe = jnp.concatenate([jnp.broadcast_to(er[:, None, :], (rows, grid_w, half)).reshape(t_rows, half),
                          jnp.concatenate([ec_ref[...]] * rows, axis=0)], axis=1)
    return x_ref[...] + pe


def _rglru_in_kernel(x_ref, xn_ref, er_ref, ec_ref, mod_ref, g_ref, w_ref, cw_ref, cb_ref, wri_ref, grb_ref,
                     gib_ref, lam_ref, h0_ref, gate_ref, uc_ref, y_ref, ubuf, a_scr, b_scr, h_scr, tail_scr,
                     *, n_tiles):
    t_rows, d = x_ref.shape
    c = uc_ref.shape[1]
    halo = xn_ref.shape[0]
    grid_w, half = ec_ref.shape
    row = pl.program_id(0)
    j = pl.program_id(1)
    sc, sh = _mod_chunk(mod_ref, row, 1, d), _mod_chunk(mod_ref, row, 0, d)

    x = _with_pos_code(x_ref, er_ref, ec_ref, j)
    z = jnp.dot(_rms_mod(x, g_ref[...], sc, sh).astype(BF16), w_ref[...], preferred_element_type=F32)
    gate_ref[...] = jax.nn.gelu(z[:, :c]).astype(BF16)
    u = z[:, c:]

    r_next = jnp.minimum((j + 1) * (t_rows // grid_w), er_ref.shape[0] - 1)
    pe_n = jnp.concatenate([jnp.broadcast_to(er_ref[pl.ds(r_next, 1), :], (halo, half)), ec_ref[0:halo, :]], axis=1)
    hn = _rms_mod(xn_ref[...] + pe_n, g_ref[...], sc, sh)
    un = jnp.dot(hn.astype(BF16), w_ref[:, c:], preferred_element_type=F32)

    @pl.when(j == 0)
    def _():
        tail_scr[...] = jnp.zeros_like(tail_scr)

    ubuf[0:halo, :] = tail_scr[...]
    ubuf[halo:halo + t_rows, :] = u
    ubuf[halo + t_rows:, :] = jnp.where(j == n_tiles - 1, 0.0, un)
    tail_scr[...] = u[t_rows - halo:, :]
    uc = cb_ref[...]
    full = ubuf[...]
    for k in range(CONV_WIDTH):
        shift = (CONV_PAD_LEFT - k) % full.shape[0]
        tap = full if shift == 0 else pltpu.roll(full, shift, 0)
        uc = uc + cw_ref[k:k + 1, :] * tap[halo:halo + t_rows, :]
    uc_ref[...] = uc
    _scan_tile(uc, wri_ref, grb_ref, gib_ref, lam_ref, h0_ref, a_scr, b_scr, h_scr, y_ref, row, j, False, False)


def _rglru_in(x2, er, ec, mod, g, w_in, conv_w, conv_b, w_ri, r_b, i_b, lam, h0):
    n, d = x2.shape
    c = w_in.shape[1] // 2
    n_batch = h0.shape[0]
    n_tiles = n // n_batch // MIX_TILE
    sub = MIX_TILE // SUBLANES
    row = lambda b, j: (b * n_tiles + j, 0)
    const = lambda b, j: (0, 0)
    out_c = pl.BlockSpec((MIX_TILE, c), row)
    return pl.pallas_call(
        functools.partial(_rglru_in_kernel, n_tiles=n_tiles),
        grid=(n_batch, n_tiles),
        in_specs=[pl.BlockSpec((MIX_TILE, d), row),
                  pl.BlockSpec((SUBLANES, d), lambda b, j: (jnp.minimum((b * n_tiles + j + 1) * sub,
                                                                        n // SUBLANES - 1), 0)),
                  pl.BlockSpec(er.shape, const),
                  pl.BlockSpec(ec.shape, const),
                  pl.BlockSpec(mod.shape, const),
                  pl.BlockSpec((1, d), const),
                  pl.BlockSpec(w_in.shape, const),
                  pl.BlockSpec(conv_w.shape, const),
                  pl.BlockSpec((1, c), const),
                  pl.BlockSpec(w_ri.shape, lambda b, j: (0, 0, 0)),
                  pl.BlockSpec((1, c), const), pl.BlockSpec((1, c), const), pl.BlockSpec((1, c), const),
                  pl.BlockSpec(h0.shape, const)],
        out_specs=[out_c, out_c, out_c],
        out_shape=[jax.ShapeDtypeStruct((n, c), BF16),
                   jax.ShapeDtypeStruct((n, c), F32),
                   jax.ShapeDtypeStruct((n, c), F32)],
        scratch_shapes=[pltpu.VMEM((MIX_TILE + 2 * SUBLANES, c), F32),
                        pltpu.VMEM((MIX_TILE, c), F32),
                        pltpu.VMEM((MIX_TILE, c), F32),
                        pltpu.VMEM((1, c), F32),
                        pltpu.VMEM((SUBLANES, c), F32)],
        compiler_params=_params(("arbitrary", "arbitrary")),
        name="rglru_in_scan",
    )(x2, x2, er, ec, mod, g, w_in, conv_w, conv_b.reshape(1, c), w_ri, r_b.reshape(1, c), i_b.reshape(1, c),
      lam.reshape(1, c), h0)


def _ctx_in_kernel(x_ref, mod_ref, g_ref, w_ref, u_ref, *, ctx_row):
    d = x_ref.shape[1]
    hx = _rms_mod(x_ref[...], g_ref[...], _mod_chunk(mod_ref, ctx_row, 1, d),
                  _mod_chunk(mod_ref, ctx_row, 0, d))
    u_ref[...] = jnp.dot(hx.astype(BF16), w_ref[...], preferred_element_type=F32)


def _ctx_in(c2, mod, g, w_u, ctx_row):
    n, d = c2.shape
    c = w_u.shape[1]
    return pl.pallas_call(
        functools.partial(_ctx_in_kernel, ctx_row=ctx_row),
        grid=(n // TOKEN_TILE,),
        in_specs=[pl.BlockSpec((TOKEN_TILE, d), lambda i: (i, 0)),
                  pl.BlockSpec(mod.shape, lambda i: (0, 0)),
                  pl.BlockSpec((1, d), lambda i: (0, 0)),
                  pl.BlockSpec(w_u.shape, lambda i: (0, 0))],
        out_specs=pl.BlockSpec((TOKEN_TILE, c), lambda i: (i, 0)),
        out_shape=jax.ShapeDtypeStruct((n, c), F32),
        compiler_params=_params(("arbitrary",)),
        name="ctx_in",
    )(c2, mod, g, w_u)


def _log_sigmoid(x):
    return jnp.minimum(x, 0.0) - jnp.log1p(jnp.exp(-jnp.abs(x)))


def _lru_scan_kernel(*refs, n_tiles, reverse, conv_done, reset_first, emit_y):
    if conv_done:
        u_ref, wri_ref, rb_ref, ib_ref, lam_ref, h0_ref, out_ref, a_scr, b_scr, h_scr = refs
    else:
        (u_ref, up_ref, un_ref, cw_ref, cb_ref, wri_ref, rb_ref, ib_ref, lam_ref, h0_ref,
         out_ref, uc_ref, ubuf, a_scr, b_scr, h_scr) = refs
    t_rows, c = u_ref.shape
    hb = c // RNN_HEADS
    b = pl.program_id(0)
    j = pl.program_id(1)
    jj = n_tiles - 1 - j if reverse else j

    if conv_done:
        u = u_ref[...]
    else:
        ubuf[SUBLANES:SUBLANES + t_rows, :] = u_ref[...]
        ubuf[0:SUBLANES, :] = jnp.where(jj == 0, 0.0, up_ref[...])
        ubuf[SUBLANES + t_rows:, :] = jnp.where(jj == n_tiles - 1, 0.0, un_ref[...])
        u = cb_ref[...]
        full = ubuf[...]
        for k in range(CONV_WIDTH):
            shift = (CONV_PAD_LEFT - k) % full.shape[0]
            tap = full if shift == 0 else pltpu.roll(full, shift, 0)
            u = u + cw_ref[k:k + 1, :] * tap[SUBLANES:SUBLANES + t_rows, :]
        uc_ref[...] = u

    h = _scan_tile(u, wri_ref, rb_ref, ib_ref, lam_ref, h0_ref, a_scr, b_scr, h_scr, out_ref if emit_y else None,
                   b, j, reverse, reset_first)
    if not emit_y:
        @pl.when(j == n_tiles - 1)
        def _():
            out_ref[pl.ds(b, 1), :] = h


def _scan_tile(u, wri_ref, rb_ref, ib_ref, lam_ref, h0_ref, a_scr, b_scr, h_scr, y_ref, b, j, reverse, reset_first):
    t_rows, c = u.shape
    hb = c // RNN_HEADS
    log_lam = LRU_C * _log_sigmoid(lam_ref[...])
    rows = lax.broadcasted_iota(jnp.int32, (t_rows, 1), 0)
    first_row = jnp.where(j == 0, t_rows - 1 if reverse else 0, -1)
    for h in range(RNN_HEADS):
        sl = slice(h * hb, (h + 1) * hb)
        uh = u[:, sl]
        z = jnp.dot(uh.astype(BF16), wri_ref[h], preferred_element_type=F32)
        r = jax.nn.sigmoid(z[:, :hb] + rb_ref[:, sl])
        ig = jax.nn.sigmoid(z[:, hb:] + ib_ref[:, sl])
        log_a = r * log_lam[:, sl]
        a = jnp.exp(log_a)
        mult = jnp.sqrt((1.0 - a) * (1.0 + a))
        if reset_first:
            mult = jnp.where(rows == first_row, 1.0, mult)
        a_scr[:, sl] = a
        b_scr[:, sl] = mult * ig * uh

    @pl.when(j == 0)
    def _():
        h_scr[...] = h0_ref[pl.ds(b, 1), :]

    n_groups = t_rows // SUBLANES

    def group(g, h):
        base = pl.multiple_of((n_groups - 1 - g if reverse else g) * SUBLANES, SUBLANES)
        for s in range(SUBLANES):
            r = base + (SUBLANES - 1 - s if reverse else s)
            h = a_scr[pl.ds(r, 1), :] * h + b_scr[pl.ds(r, 1), :]
            if y_ref is not None:
                y_ref[pl.ds(r, 1), :] = h
        return h

    h = lax.fori_loop(0, n_groups, group, h_scr[...])
    h_scr[...] = h
    return h


def _lru_scan(u, conv, w_ri, r_b, i_b, lam, h0, *, reverse, reset_first, emit_y):
    n, c = u.shape
    n_batch = h0.shape[0]
    rows = min(SCAN_TILE, n // n_batch)
    n_tiles = n // n_batch // rows
    sub = rows // SUBLANES
    n_sub = n // SUBLANES
    const = lambda b, j: (0, 0)

    def tile(b, j):
        return b * n_tiles + (n_tiles - 1 - j if reverse else j)

    tile_spec = pl.BlockSpec((rows, c), lambda b, j: (tile(b, j), 0))
    in_specs = [tile_spec]
    args = [u]
    scratch = []
    if conv is not None:
        conv_w, conv_b = conv
        in_specs += [pl.BlockSpec((SUBLANES, c), lambda b, j: (jnp.maximum(tile(b, j) * sub - 1, 0), 0)),
                     pl.BlockSpec((SUBLANES, c), lambda b, j: (jnp.minimum((tile(b, j) + 1) * sub, n_sub - 1), 0)),
                     pl.BlockSpec(conv_w.shape, const),
                     pl.BlockSpec((1, c), const)]
        args += [u, u, conv_w, conv_b.reshape(1, c)]
        scratch = [pltpu.VMEM((rows + 2 * SUBLANES, c), F32)]
    in_specs += [pl.BlockSpec(w_ri.shape, lambda b, j: (0, 0, 0)),
                 pl.BlockSpec((1, c), const), pl.BlockSpec((1, c), const), pl.BlockSpec((1, c), const),
                 pl.BlockSpec(h0.shape, const)]
    args += [w_ri, r_b.reshape(1, c), i_b.reshape(1, c), lam.reshape(1, c), h0]
    if emit_y:
        out_specs = [tile_spec]
        out_shape = [jax.ShapeDtypeStruct((n, c), F32)]
    else:
        out_specs = [pl.BlockSpec(h0.shape, const)]
        out_shape = [jax.ShapeDtypeStruct(h0.shape, F32)]
    if conv is not None:
        out_specs.append(tile_spec)
        out_shape.append(jax.ShapeDtypeStruct((n, c), F32))
    name = ("lru_scan" if emit_y else "lru_ctx") + ("_rev" if reverse else "_fwd")
    outs = pl.pallas_call(
        functools.partial(_lru_scan_kernel, n_tiles=n_tiles, reverse=reverse, conv_done=conv is None,
                          reset_first=reset_first, emit_y=emit_y),
        grid=(n_batch, n_tiles),
        in_specs=in_specs,
        out_specs=out_specs,
        out_shape=out_shape,
        scratch_shapes=scratch + [pltpu.VMEM((rows, c), F32),
                                  pltpu.VMEM((rows, c), F32),
                                  pltpu.VMEM((1, c), F32)],
        compiler_params=_params(("arbitrary", "arbitrary")),
        name=name,
    )(*args)
    return outs if conv is not None else outs[0]


def _route(logits, rb):
    e, t = logits.shape
    neg = -jnp.inf
    scores = jax.nn.sigmoid(logits)
    sel = scores + rb
    iota_g = lax.broadcasted_iota(jnp.int32, (N_GROUPS, t), 0).astype(F32)
    iota_e = lax.broadcasted_iota(jnp.int32, (e, t), 0).astype(F32)

    gs = jnp.full((N_GROUPS, t), neg, F32)
    for g in range(N_GROUPS):
        sg = sel[g * EXPERTS_PER_GROUP:(g + 1) * EXPERTS_PER_GROUP, :]
        m1 = jnp.max(sg, axis=0, keepdims=True)
        i1 = jnp.min(jnp.where(sg == m1, iota_g, float(EXPERTS_PER_GROUP)), axis=0, keepdims=True)
        m2 = jnp.max(jnp.where(iota_g == i1, neg, sg), axis=0, keepdims=True)
        gs = jnp.where(iota_g == float(g), m1 + m2, gs)

    keep = jnp.zeros((N_GROUPS, t), F32)
    for _ in range(TOPK_GROUPS):
        m = jnp.max(gs, axis=0, keepdims=True)
        idx = jnp.min(jnp.where(gs == m, iota_g, float(N_GROUPS)), axis=0, keepdims=True)
        hit = iota_g == idx
        keep = jnp.where(hit, 1.0, keep)
        gs = jnp.where(hit, neg, gs)

    masked = jnp.concatenate(
        [jnp.where(keep[g:g + 1, :] > 0.0, sel[g * EXPERTS_PER_GROUP:(g + 1) * EXPERTS_PER_GROUP, :], neg)
         for g in range(N_GROUPS)], axis=0)

    iota_k = lax.broadcasted_iota(jnp.int32, (TOP_K, t), 0)
    selmask = jnp.zeros((e, t), F32)
    eidx = jnp.zeros((TOP_K, t), F32)
    gw = jnp.zeros((TOP_K, t), F32)
    for k in range(TOP_K):
        m = jnp.max(masked, axis=0, keepdims=True)
        idx = jnp.min(jnp.where(masked == m, iota_e, float(e)), axis=0, keepdims=True)
        hit = iota_e == idx
        gk = jnp.sum(jnp.where(hit, scores, 0.0), axis=0, keepdims=True)
        masked = jnp.where(hit, neg, masked)
        selmask = jnp.where(hit, 1.0, selmask)
        eidx = jnp.where(iota_k == k, idx, eidx)
        gw = jnp.where(iota_k == k, gk, gw)
    gw = gw / jnp.sum(gw, axis=0, keepdims=True) * ROUTED_SCALE
    return eidx, gw, selmask


def _ffn_pre(x1, mod_ref, row, g2_ref, rwt_ref, rb_ref, hx2_ref, hi_ref, lo_ref, hit_ref, lot_ref, gatet_ref,
             tab_ref, tabt_ref):
    d = x1.shape[1]
    hx2 = _rms_mod(x1, g2_ref[...], _mod_chunk(mod_ref, row, 4, d), _mod_chunk(mod_ref, row, 3, d))
    hx2_ref[...] = hx2.astype(BF16)
    logits = _dot_nt_3pass(rwt_ref[...], hx2)
    eidx, gw, selmask = _route(logits, rb_ref[...])
    for s in range(selmask.shape[1] // TOKEN_TILE):
        ts = slice(s * TOKEN_TILE, (s + 1) * TOKEN_TILE)
        hi, lo, gate, tab, tabt = _sorted_layout(selmask[:, ts], eidx[:, ts], gw[:, ts])
        hi_ref[:, ts] = hi.astype(BF16)
        lo_ref[:, ts] = lo.astype(BF16)
        hit_ref[ts, :] = hi.T.astype(BF16)
        lot_ref[ts, :] = lo.T.astype(BF16)
        gatet_ref[ts, :] = gate.T.astype(BF16)
        tab_ref[s] = tab
        tabt_ref[s] = tabt


def _ffn_pre_specs(n, d, tile=lambda i: i):
    row = lambda *g: (tile(*g), 0)
    col = lambda *g: (0, tile(*g))
    sub = MIX_TILE // TOKEN_TILE
    lead = lambda *g: (tile(*g), 0, 0)
    out_specs = [pl.BlockSpec((MIX_TILE, d), row),
                 pl.BlockSpec((MIX_TILE, d), row),
                 pl.BlockSpec((N_EXPERTS, MIX_TILE), col),
                 pl.BlockSpec((N_EXPERTS, MIX_TILE), col),
                 pl.BlockSpec((MIX_TILE, N_EXPERTS), row),
                 pl.BlockSpec((MIX_TILE, N_EXPERTS), row),
                 pl.BlockSpec((MIX_TILE, N_EXPERTS), row),
                 pl.BlockSpec((sub, N_EXPERTS, LANES), lead),
                 pl.BlockSpec((sub, SUBLANES, LANES), lead)]
    out_shape = [jax.ShapeDtypeStruct((n, d), F32),
                 jax.ShapeDtypeStruct((n, d), BF16),
                 jax.ShapeDtypeStruct((N_EXPERTS, n), BF16),
                 jax.ShapeDtypeStruct((N_EXPERTS, n), BF16),
                 jax.ShapeDtypeStruct((n, N_EXPERTS), BF16),
                 jax.ShapeDtypeStruct((n, N_EXPERTS), BF16),
                 jax.ShapeDtypeStruct((n, N_EXPERTS), BF16),
                 jax.ShapeDtypeStruct((n // TOKEN_TILE, N_EXPERTS, LANES), F32),
                 jax.ShapeDtypeStruct((n // TOKEN_TILE, SUBLANES, LANES), F32)]
    return out_specs, out_shape


def _rglru_out_kernel(u_ref, wri_ref, grb_ref, gib_ref, lam_ref, h0_ref, yf_ref, gate_ref, x_ref, er_ref, ec_ref,
                      mod_ref, wout_ref, g2_ref, rwt_ref, rb_ref,
                      x1_ref, hx2_ref, hi_ref, lo_ref, hit_ref, lot_ref, gatet_ref, tab_ref, tabt_ref,
                      a_scr, b_scr, h_scr, yr_scr, *, n_tiles):
    d = x_ref.shape[1]
    row = pl.program_id(0)
    j = pl.program_id(1)
    tile_in_seq = n_tiles - 1 - j
    _scan_tile(u_ref[...], wri_ref, grb_ref, gib_ref, lam_ref, h0_ref, a_scr, b_scr, h_scr, yr_scr,
               row, j, True, False)
    yx = yf_ref[...] + yr_scr[...]
    v = gate_ref[...].astype(F32) * yx
    out = jnp.dot(v.astype(BF16), wout_ref[...], preferred_element_type=F32)
    x1 = _with_pos_code(x_ref, er_ref, ec_ref, tile_in_seq) + _mod_chunk(mod_ref, row, 2, d) * out
    x1_ref[...] = x1
    _ffn_pre(x1, mod_ref, row, g2_ref, rwt_ref, rb_ref, hx2_ref, hi_ref, lo_ref, hit_ref, lot_ref, gatet_ref,
             tab_ref, tabt_ref)


def _rglru_out(u_conv, w_ri, r_b, i_b, lam, h0, y_fwd, gate, x, er, ec, mod, w_out, g2, rwt, rb):
    n, d = x.shape
    c = gate.shape[1]
    n_batch = h0.shape[0]
    n_tiles = n // n_batch // MIX_TILE
    tile = lambda b, j: b * n_tiles + n_tiles - 1 - j
    row = lambda b, j: (tile(b, j), 0)
    const = lambda b, j: (0, 0)
    out_specs, out_shape = _ffn_pre_specs(n, d, tile)
    return pl.pallas_call(
        functools.partial(_rglru_out_kernel, n_tiles=n_tiles),
        grid=(n_batch, n_tiles),
        in_specs=[pl.BlockSpec((MIX_TILE, c), row),
                  pl.BlockSpec(w_ri.shape, lambda b, j: (0, 0, 0)),
                  pl.BlockSpec((1, c), const), pl.BlockSpec((1, c), const), pl.BlockSpec((1, c), const),
                  pl.BlockSpec(h0.shape, const),
                  pl.BlockSpec((MIX_TILE, c), row),
                  pl.BlockSpec((MIX_TILE, c), row),
                  pl.BlockSpec((MIX_TILE, d), row),
                  pl.BlockSpec(er.shape, const),
                  pl.BlockSpec(ec.shape, const),
                  pl.BlockSpec(mod.shape, const),
                  pl.BlockSpec(w_out.shape, const),
                  pl.BlockSpec((1, d), const),
                  pl.BlockSpec(rwt.shape, const),
                  pl.BlockSpec(rb.shape, const)],
        out_specs=out_specs,
        out_shape=out_shape,
        scratch_shapes=[pltpu.VMEM((MIX_TILE, c), F32), pltpu.VMEM((MIX_TILE, c), F32),
                        pltpu.VMEM((1, c), F32), pltpu.VMEM((MIX_TILE, c), F32)],
        compiler_params=_params(("arbitrary", "arbitrary")),
        name="rglru_rev_out",
    )(u_conv, w_ri, r_b.reshape(1, c), i_b.reshape(1, c), lam.reshape(1, c), h0, y_fwd, gate, x, er, ec, mod,
      w_out, g2, rwt, rb)


def _sgu_kernel(x_ref, mod_ref, g_ref, win_ref, lng_ref, lnb_ref, ws_ref, bst_ref, wout_ref,
                g2_ref, rwt_ref, rb_ref,
                x1_ref, hx2_ref, hi_ref, lo_ref, hit_ref, lot_ref, gatet_ref, tab_ref, tabt_ref, m_scr, *,
                tiles_per_seq):
    t_rows, d = x_ref.shape
    w = wout_ref.shape[0]
    gd = w // SGU_HEADS
    row = pl.program_id(0) // tiles_per_seq
    x = x_ref[...]
    hx = _rms_mod(x, g_ref[...], _mod_chunk(mod_ref, row, 1, d), _mod_chunk(mod_ref, row, 0, d))
    z = jax.nn.gelu(jnp.dot(hx.astype(BF16), win_ref[...], preferred_element_type=F32))
    u = z[:, :w]
    v = z[:, w:]
    mu = jnp.mean(v, axis=-1, keepdims=True)
    vc = v - mu
    v = vc * lax.rsqrt(jnp.mean(vc * vc, axis=-1, keepdims=True) + NORM_EPS) * lng_ref[...] + lnb_ref[...]
    vb = v.astype(BF16)
    for ch in range(t_rows // CHUNK):
        rs = slice(ch * CHUNK, (ch + 1) * CHUNK)
        for g in range(SGU_HEADS):
            cs = slice(g * gd, (g + 1) * gd)
            sv = jnp.dot(ws_ref[g], vb[rs, cs], preferred_element_type=F32) + bst_ref[:, g:g + 1]
            m_scr[rs, cs] = (u[rs, cs] * sv).astype(BF16)
    out = jnp.dot(m_scr[...], wout_ref[...], preferred_element_type=F32)
    x1 = x + _mod_chunk(mod_ref, row, 2, d) * out
    x1_ref[...] = x1
    _ffn_pre(x1, mod_ref, row, g2_ref, rwt_ref, rb_ref, hx2_ref, hi_ref, lo_ref, hit_ref, lot_ref, gatet_ref,
             tab_ref, tabt_ref)


def _sgu(x, mod, g, w_in, ln_g, ln_b, w_s, b_st, w_out, g2, rwt, rb, tiles_per_seq):
    n, d = x.shape
    w = w_out.shape[0]
    const = lambda i: (0, 0)
    out_specs, out_shape = _ffn_pre_specs(n, d)
    return pl.pallas_call(
        functools.partial(_sgu_kernel, tiles_per_seq=tiles_per_seq),
        grid=(n // MIX_TILE,),
        in_specs=[pl.BlockSpec((MIX_TILE, d), lambda i: (i, 0)),
                  pl.BlockSpec(mod.shape, const),
                  pl.BlockSpec((1, d), const),
                  pl.BlockSpec(w_in.shape, const, pipeline_mode=pl.Buffered(1)),
                  pl.BlockSpec((1, w), const),
                  pl.BlockSpec((1, w), const),
                  pl.BlockSpec(w_s.shape, lambda i: (0, 0, 0)),
                  pl.BlockSpec(b_st.shape, const),
                  pl.BlockSpec(w_out.shape, const, pipeline_mode=pl.Buffered(1)),
                  pl.BlockSpec((1, d), const),
                  pl.BlockSpec(rwt.shape, const),
                  pl.BlockSpec(rb.shape, const)],
        out_specs=out_specs,
        out_shape=out_shape,
        scratch_shapes=[pltpu.VMEM((MIX_TILE, w), BF16)],
        compiler_params=_params(("arbitrary",)),
        name="sgu",
    )(x, mod, g, w_in, ln_g, ln_b, w_s, b_st, w_out, g2, rwt, rb)


def _sorted_layout(m, eidx, gw):
    e, t = m.shape
    lanes = LANES
    r = lax.broadcasted_iota(jnp.int32, (t, t), 0)
    c = lax.broadcasted_iota(jnp.int32, (t, t), 1)
    upper = jnp.where(r <= c, 1.0, 0.0).astype(BF16)
    incl = jnp.dot(m.astype(BF16), upper, preferred_element_type=F32)
    run = _round_up_rows(incl[:, t - 1:t])
    re = lax.broadcasted_iota(jnp.int32, (e, e), 0)
    ce = lax.broadcasted_iota(jnp.int32, (e, e), 1)
    lower = jnp.where(ce < re, 1.0, 0.0).astype(BF16)
    tiles = jnp.broadcast_to(run * (1.0 / ROW_ALIGN), (e, lanes)).astype(BF16)
    lstart = jnp.dot(lower, tiles, preferred_element_type=F32)[:, 0:1] * float(ROW_ALIGN)
    pos = jnp.where(m > 0.0, lstart + incl - m, float(POS_RADIX * POS_RADIX - 1))
    hi = jnp.floor(pos * (1.0 / POS_RADIX))
    lo = pos - hi * float(POS_RADIX)
    iota_e = lax.broadcasted_iota(jnp.int32, (e, t), 0).astype(F32)
    gate = jnp.zeros((e, t), F32)
    for k in range(TOP_K):
        gate = jnp.where(iota_e == eidx[k:k + 1, :], gw[k:k + 1, :], gate)
    lane = lax.broadcasted_iota(jnp.int32, (e, lanes), 1)
    tab = jnp.where(lane == 0, run, jnp.where(lane == 2, lstart, 0.0))
    diag = lax.broadcasted_iota(jnp.int32, (e, lanes), 0) == lane
    row_of = lambda col: jnp.sum(jnp.where(diag, jnp.broadcast_to(col, (e, lanes)), 0.0), axis=0, keepdims=True)
    sub = lax.broadcasted_iota(jnp.int32, (SUBLANES, lanes), 0)
    tabt = jnp.where(sub == 0, row_of(lstart), jnp.where(sub == 1, row_of(lstart + run), 0.0))
    return hi, lo, gate, tab, tabt


def _aligned(v):
    return pl.multiple_of(v, ROW_ALIGN)


def _run_copies(tab_ref, make_copy, unroll, base=0):
    def body(e, carry):
        make_copy(_aligned(tab_ref[base + TAB_LOCAL + e]), _aligned(tab_ref[base + N_EXPERTS + e]),
                  _aligned(tab_ref[base + e])).start()
        return carry

    lax.fori_loop(0, N_EXPERTS, body, 0, unroll=unroll)


def _dispatch_kernel(tab_ref, ztab_ref, tabt_ref, hi_ref, lo_ref, x_ref, xs_hbm, sbuf, zbuf, sems, zsem):
    i = pl.program_id(0)
    tiles = sbuf.shape[0]
    t = x_ref.shape[0] // tiles
    rows = sbuf.shape[1]

    def zero_copy(e):
        n = _aligned(ztab_ref[e])
        return pltpu.make_async_copy(zbuf.at[pl.ds(0, n)], xs_hbm.at[pl.ds(_aligned(ztab_ref[N_EXPERTS + e]), n)], zsem)

    def for_zero_runs(fn):
        def body(e, c):
            @pl.when(ztab_ref[e] > 0)
            def _():
                fn(zero_copy(e))
            return c
        lax.fori_loop(0, N_EXPERTS, body, 0)

        def tail(b, c):
            fn(pltpu.make_async_copy(zbuf, xs_hbm.at[pl.ds(pl.multiple_of(b * EXPERT_BLOCK, EXPERT_BLOCK),
                                                           EXPERT_BLOCK)], zsem))
            return c
        lax.fori_loop(ztab_ref[2 * N_EXPERTS], xs_hbm.shape[0] // EXPERT_BLOCK, tail, 0)

    @pl.when(i == 0)
    def _():
        zbuf[...] = jnp.zeros_like(zbuf)
        for_zero_runs(lambda cp: cp.start())

    def wait_rows(s, n):
        pltpu.make_async_copy(sbuf.at[s, pl.ds(0, n)], xs_hbm.at[pl.ds(0, n)], sems.at[s]).wait()

    def permute(s, r0):
        ts = slice(s * t, (s + 1) * t)
        j_e = (lax.broadcasted_iota(jnp.int32, (PERM_CHUNK, N_EXPERTS), 0) + r0).astype(F32)
        owner = jnp.where((j_e >= tabt_ref[s, 0:1, 0:N_EXPERTS]) & (j_e < tabt_ref[s, 1:2, 0:N_EXPERTS]),
                          1.0, 0.0).astype(BF16)
        pos = (jnp.dot(owner, hi_ref[:, ts], preferred_element_type=F32) * float(POS_RADIX)
               + jnp.dot(owner, lo_ref[:, ts], preferred_element_type=F32))
        j_t = (lax.broadcasted_iota(jnp.int32, (PERM_CHUNK, t), 0) + r0).astype(F32)
        p = jnp.where(pos == j_t, 1.0, 0.0).astype(BF16)
        sbuf[s, r0:r0 + PERM_CHUNK, :] = jnp.dot(p, x_ref[ts, :], preferred_element_type=F32).astype(BF16)

    for s in range(tiles):
        base = s * TAB_WIDTH

        @pl.when(i > 0)
        def _():
            wait_rows(s, _aligned(tab_ref[base + 2 * N_EXPERTS + 1]))

        for r0 in range(0, rows, PERM_CHUNK):
            if r0 < t * TOP_K + PERM_CHUNK:
                permute(s, r0)
            else:
                pl.when(tab_ref[base + 2 * N_EXPERTS] > r0)(functools.partial(permute, s, r0))

        _run_copies(tab_ref, lambda loc, glob, n: pltpu.make_async_copy(
            sbuf.at[s, pl.ds(loc, n)], xs_hbm.at[pl.ds(glob, n)], sems.at[s]), unroll=True, base=base)

    @pl.when(i == pl.num_programs(0) - 1)
    def _():
        for s in range(tiles):
            wait_rows(s, _aligned(tab_ref[s * TAB_WIDTH + 2 * N_EXPERTS]))

    @pl.when(i == 0)
    def _():
        for_zero_runs(lambda cp: cp.wait())


def _dispatch(tab, ztab, tab_t, pos_hi, pos_lo, hx2, xs_rows):
    n, d = hx2.shape
    col = lambda i: (0, i)
    return pl.pallas_call(
        _dispatch_kernel,
        grid=(n // (DISPATCH_TILES * TOKEN_TILE),),
        in_specs=[pl.BlockSpec((DISPATCH_TILES * TAB_WIDTH,), lambda i: (i,), memory_space=pltpu.SMEM),
                  pl.BlockSpec(memory_space=pltpu.SMEM),
                  pl.BlockSpec((DISPATCH_TILES, SUBLANES, LANES), lambda i: (i, 0, 0)),
                  pl.BlockSpec((N_EXPERTS, DISPATCH_TILES * TOKEN_TILE), col),
                  pl.BlockSpec((N_EXPERTS, DISPATCH_TILES * TOKEN_TILE), col),
                  pl.BlockSpec((DISPATCH_TILES * TOKEN_TILE, d), lambda i: (i, 0))],
        out_specs=pl.BlockSpec(memory_space=pl.ANY),
        out_shape=jax.ShapeDtypeStruct((xs_rows, d), BF16),
        scratch_shapes=[pltpu.VMEM((DISPATCH_TILES, SORTED_ROWS, d), BF16),
                        pltpu.VMEM((EXPERT_BLOCK, d), BF16),
                        pltpu.SemaphoreType.DMA((DISPATCH_TILES,)),
                        pltpu.SemaphoreType.DMA],
        compiler_params=_params(("arbitrary",)),
        name="moe_dispatch",
    )(tab, ztab, tab_t, pos_hi, pos_lo, hx2)


def _experts_kernel(be_ref, nb_ref, xs_ref, wgu_ref, wd_ref, ys_ref, wgu_b, wd_b):
    i = pl.program_id(0)
    ff = wd_b.shape[0]
    used = i < nb_ref[0]
    new_expert = (i == 0) | (be_ref[i] != be_ref[jnp.maximum(i - 1, 0)])

    @pl.when(used & new_expert)
    def _():
        wgu_b[...] = wgu_ref[0, 0].astype(BF16)
        wd_b[...] = wd_ref[0, 0].astype(BF16)

    @pl.when(used)
    def _():
        h = jnp.dot(xs_ref[...], wgu_b[...], preferred_element_type=F32)
        a = _silu(h[:, :ff]) * h[:, ff:]
        ys_ref[...] = jnp.dot(a.astype(BF16), wd_b[...], preferred_element_type=F32).astype(BF16)


def _experts(block_e, n_used, xs, w_gu, w_down, layer):
    rows, d = xs.shape
    nb = rows // EXPERT_BLOCK
    blk = lambda i, be, nu: (jnp.maximum(jnp.minimum(i, nu[0] - 1), 0), 0)
    wmap = lambda i, be, nu: (layer, be[i], 0, 0)
    return pl.pallas_call(
        _experts_kernel,
        grid_spec=pltpu.PrefetchScalarGridSpec(
            num_scalar_prefetch=2,
            grid=(nb,),
            in_specs=[pl.BlockSpec((EXPERT_BLOCK, d), blk),
                      pl.BlockSpec((1, 1) + w_gu.shape[2:], wmap),
                      pl.BlockSpec((1, 1) + w_down.shape[2:], wmap)],
            out_specs=pl.BlockSpec((EXPERT_BLOCK, d), blk),
            scratch_shapes=[pltpu.VMEM(w_gu.shape[2:], BF16), pltpu.VMEM(w_down.shape[2:], BF16)]),
        out_shape=jax.ShapeDtypeStruct((rows, d), BF16),
        input_output_aliases={2: 0},
        compiler_params=_params(("arbitrary",)),
        name="moe_experts",
    )(block_e, n_used, xs, w_gu, w_down)


def _combine_kernel(tab_ref, tabn_ref, tabv_ref, hi_ref, lo_ref, gate_ref, ys_hbm, hx2_ref, x1_ref, mod_ref,
                    wsgu_ref, wsd_ref, fg_ref, out_ref, ybuf, sems, *, tiles_per_seq, final_norm):
    t_rows, d = x1_ref.shape
    ff = wsd_ref.shape[0]
    rows = ybuf.shape[1]
    i = pl.program_id(0)
    last = pl.num_programs(0) - 1
    slot = i % 2
    row = i // tiles_per_seq

    def gather(table, s, unroll):
        _run_copies(table, lambda loc, glob, n: pltpu.make_async_copy(
            ys_hbm.at[pl.ds(glob, n)], ybuf.at[s, pl.ds(loc, n)], sems.at[s]), unroll=unroll)

    def wait_rows(table, s):
        n = _aligned(table[2 * N_EXPERTS])
        pltpu.make_async_copy(ys_hbm.at[pl.ds(0, n)], ybuf.at[s, pl.ds(0, n)], sems.at[s]).wait()

    @pl.when(i == 0)
    def _():
        ybuf[...] = jnp.zeros_like(ybuf)
        gather(tab_ref, 0, False)

    gather(tabn_ref, 1 - slot, True)

    hs = jnp.dot(hx2_ref[...], wsgu_ref[...], preferred_element_type=F32)
    shared = jnp.dot((_silu(hs[:, :ff]) * hs[:, ff:]).astype(BF16), wsd_ref[...], preferred_element_type=F32)

    run_lo = tabv_ref[0, :, 2:3]
    run_hi = run_lo + tabv_ref[0, :, 0:1]
    j_e = lax.broadcasted_iota(jnp.int32, (N_EXPERTS, rows), 1).astype(F32)
    owner = jnp.where((j_e >= run_lo) & (j_e < run_hi), 1.0, 0.0).astype(BF16)
    pos = (jnp.dot(hi_ref[...], owner, preferred_element_type=F32) * float(POS_RADIX)
           + jnp.dot(lo_ref[...], owner, preferred_element_type=F32))
    gates = jnp.dot(gate_ref[...], owner, preferred_element_type=F32)
    j_t = lax.broadcasted_iota(jnp.int32, (t_rows, rows), 1).astype(F32)
    gb = jnp.where(pos == j_t, gates, 0.0).astype(BF16)

    wait_rows(tab_ref, slot)
    routed = jnp.dot(gb, ybuf[slot], preferred_element_type=F32)
    x2 = x1_ref[...] + _mod_chunk(mod_ref, row, 5, d) * (routed + shared)
    if final_norm:
        x2 = x2 * lax.rsqrt(jnp.mean(x2 * x2, axis=-1, keepdims=True) + NORM_EPS) * fg_ref[...]
    out_ref[...] = x2

    @pl.when(i == last)
    def _():
        wait_rows(tabn_ref, 1 - slot)


def _combine(tab, tab_v, pos_hi, pos_lo, gate, ys, hx2, x1, mod, ws_gu, ws_down, fg, tiles_per_seq, final_norm):
    n, d = x1.shape
    row = lambda i: (i, 0)
    const = lambda i: (0, 0)
    return pl.pallas_call(
        functools.partial(_combine_kernel, tiles_per_seq=tiles_per_seq, final_norm=final_norm),
        grid=(n // TOKEN_TILE,),
        in_specs=[pl.BlockSpec((TAB_WIDTH,), lambda i: (i,), memory_space=pltpu.SMEM),
                  pl.BlockSpec((TAB_WIDTH,), lambda i: (jnp.minimum(i + 1, n // TOKEN_TILE - 1),),
                               memory_space=pltpu.SMEM),
                  pl.BlockSpec((1, N_EXPERTS, LANES), lambda i: (i, 0, 0)),
                  pl.BlockSpec((TOKEN_TILE, N_EXPERTS), row),
                  pl.BlockSpec((TOKEN_TILE, N_EXPERTS), row),
                  pl.BlockSpec((TOKEN_TILE, N_EXPERTS), row),
                  pl.BlockSpec(memory_space=pl.ANY),
                  pl.BlockSpec((TOKEN_TILE, d), row),
                  pl.BlockSpec((TOKEN_TILE, d), row),
                  pl.BlockSpec(mod.shape, const),
                  pl.BlockSpec(ws_gu.shape, const),
                  pl.BlockSpec(ws_down.shape, const),
                  pl.BlockSpec((1, d), const)],
        out_specs=pl.BlockSpec((TOKEN_TILE, d), row),
        out_shape=jax.ShapeDtypeStruct((n, d), F32),
        scratch_shapes=[pltpu.VMEM((2, SORTED_ROWS, d), BF16), pltpu.SemaphoreType.DMA((2,))],
        compiler_params=_params(("arbitrary",)),
        name="moe_combine",
    )(tab, tab, tab_v, pos_hi, pos_lo, gate, ys, hx2, x1, mod, ws_gu, ws_down, fg)


def _moe(x1, hx2, pos_hi, pos_lo, pos_hi_t, pos_lo_t, gate_t, tab_f, tab_t, mod, w_gu, w_down, layer, ws_gu,
         ws_down, fg, tiles_per_seq, final_norm):
    n, d = x1.shape
    n_tiles = n // TOKEN_TILE
    run_len = tab_f[:, :, 0].astype(jnp.int32)
    counts = jnp.sum(run_len, axis=0)
    padded = (counts + EXPERT_BLOCK - 1) // EXPERT_BLOCK * EXPERT_BLOCK
    pad_end = jnp.cumsum(padded)
    pad_start = pad_end - padded
    max_rows = n * TOP_K + n_tiles * N_EXPERTS * ROW_ALIGN + N_EXPERTS * (EXPERT_BLOCK - ROW_ALIGN)
    n_blocks = (max_rows + EXPERT_BLOCK - 1) // EXPERT_BLOCK
    n_used = (pad_end[-1:] // EXPERT_BLOCK).astype(jnp.int32)
    block_start = jnp.arange(n_blocks, dtype=jnp.int32) * EXPERT_BLOCK
    block_e = jnp.minimum(jnp.sum((pad_end[None, :] <= block_start[:, None]).astype(jnp.int32), axis=1),
                          N_EXPERTS - 1)

    run_start = pad_start[None, :] + jnp.cumsum(run_len, axis=0) - run_len
    total = jnp.sum(run_len, axis=1, keepdims=True)
    prev_total = jnp.concatenate([jnp.zeros((DISPATCH_TILES, 1), jnp.int32), total[:-DISPATCH_TILES]], axis=0)
    run_local = tab_f[:, :, 2].astype(jnp.int32)
    fill = jnp.zeros((n_tiles, TAB_WIDTH - 3 * N_EXPERTS - 2), jnp.int32)
    tab = jnp.concatenate([run_len, run_start, total, prev_total, run_local, fill], axis=1).reshape(-1)
    ztab = jnp.concatenate([padded - counts, pad_start + counts, n_used])

    xs = _dispatch(tab, ztab, tab_t, pos_hi, pos_lo, hx2, n_blocks * EXPERT_BLOCK)
    ys = _experts(block_e, n_used, xs, w_gu, w_down, layer)
    return _combine(tab, tab_f, pos_hi_t, pos_lo_t, gate_t, ys, hx2, x1, mod, ws_gu, ws_down, fg,
                    tiles_per_seq, final_norm)


def _sincos_tables(rows, d):
    quarter = d // 4
    omega = 1.0 / (POS_BASE ** (jnp.arange(quarter, dtype=F32) / quarter))

    def emb(n):
        p = jnp.arange(n, dtype=F32)[:, None] * omega[None, :]
        return jnp.concatenate([jnp.sin(p), jnp.cos(p)], axis=-1)

    return emb(rows), emb(GRID_W)


def kernel(x, c, ctx, c_ctx, ada_w, ada_b, mix_norm_g, ffn_norm_g, a_w_in, a_conv_w, a_conv_b, a_gate_r_w, a_gate_r_b, a_gate_i_w, a_gate_i_b, a_lambda, a_w_out, b_w_in, b_ln_g, b_ln_b, b_w_s, b_b_s, b_w_out, router_w, router_b, moe_w_gu, moe_w_down, shared_w_gu, shared_w_down, final_norm_g):
    bsz, s, d = x.shape
    ctx_len = ctx.shape[1]
    depth = ada_w.shape[0]
    assert depth == 2 and bsz < MOD_ROWS and s % MIX_TILE == 0 and MIX_TILE % TOKEN_TILE == 0
    assert s % SCAN_TILE == 0 and MIX_TILE % GRID_W == 0
    assert ctx_len % TOKEN_TILE == 0
    n = bsz * s
    tps = s // TOKEN_TILE
    ctx_row = bsz

    cc = jnp.zeros((MOD_ROWS, d), F32).at[:bsz].set(c).at[ctx_row].set(c_ctx)
    mod = _modulation(cc, ada_w, ada_b)
    er, ec = _sincos_tables(s // GRID_W, d)
    rc = a_w_in.shape[2] // 2

    w_in0 = a_w_in[0].astype(BF16)
    g_mix0 = mix_norm_g[0].reshape(1, d)
    x2 = x.reshape(n, d)
    uc = _ctx_in(ctx.reshape(bsz * ctx_len, d), mod[0], g_mix0, w_in0[:, rc:], ctx_row)
    w_ri = jnp.concatenate([a_gate_r_w[0], a_gate_i_w[0]], axis=-1).astype(BF16)
    conv = (a_conv_w[0], a_conv_b[0])
    gates = [(w_ri[k], a_gate_r_b[0, k], a_gate_i_b[0, k], a_lambda[0, k]) for k in range(2)]
    h_zero = jnp.zeros((bsz, rc), F32)
    h_fwd, uc = _lru_scan(uc, conv, *gates[0], h_zero, reverse=False, reset_first=True, emit_y=False)
    h_rev = _lru_scan(uc, None, *gates[1], h_zero, reverse=True, reset_first=True, emit_y=False)
    gate, ux, y_fwd = _rglru_in(x2, er, ec, mod[0], g_mix0, w_in0, *conv, *gates[0], h_fwd)
    pre = _rglru_out(ux, *gates[1], h_rev, y_fwd, gate, x2, er, ec, mod[0], a_w_out[0].astype(BF16),
                     ffn_norm_g[0].reshape(1, d), router_w[0].T, router_b[0].reshape(N_EXPERTS, 1))
    x1 = _moe(*pre, mod[0], moe_w_gu, moe_w_down, 0, shared_w_gu[0].astype(BF16),
              shared_w_down[0].astype(BF16), final_norm_g.reshape(1, d), tps, False)

    pre = _sgu(x1, mod[1], mix_norm_g[1].reshape(1, d), b_w_in[0].astype(BF16),
               b_ln_g[0].reshape(1, -1), b_ln_b[0].reshape(1, -1), b_w_s[0].astype(BF16), b_b_s[0].T,
               b_w_out[0].astype(BF16), ffn_norm_g[1].reshape(1, d),
               router_w[1].T, router_b[1].reshape(N_EXPERTS, 1), s // MIX_TILE)
    out = _moe(*pre, mod[1], moe_w_gu, moe_w_down, 1, shared_w_gu[1].astype(BF16),
               shared_w_down[1].astype(BF16), final_norm_g.reshape(1, d), tps, True)
    return out.reshape(bsz, s, d)
```

```python
import functools

import jax
import jax.numpy as jnp
from jax import lax
from jax.experimental import pallas as pl
from jax.experimental.pallas import tpu as pltpu

F32 = jnp.float32
BF16 = jnp.bfloat16
HIGHEST = lax.Precision.HIGHEST

GRID_W = 64
NORM_EPS = 1e-6
POS_BASE = 10000.0
RNN_HEADS = 5
CONV_WIDTH = 4
CONV_PAD_LEFT = 2
LRU_C = 8.0
SGU_HEADS = 8
CHUNK = 128
N_EXPERTS = 64
TOP_K = 8
N_GROUPS = 8
TOPK_GROUPS = 4
EXPERTS_PER_GROUP = N_EXPERTS // N_GROUPS
ROUTED_SCALE = 2.5

SUBLANES = 8
ROW_ALIGN = 16
LANES = 128
MOD_ROWS = 8
TOKEN_TILE = 256
MIX_TILE = 512
SCAN_TILE = 512
EXPERT_BLOCK = 1024
SORTED_ROWS = TOKEN_TILE * TOP_K + N_EXPERTS * ROW_ALIGN
PERM_CHUNK = 512
DISPATCH_TILES = 2
TAB_WIDTH = 256
TAB_LOCAL = 2 * N_EXPERTS + 2
POS_RADIX = 64
VMEM_LIMIT = 56 * 1024 * 1024


def _params(semantics, vmem=VMEM_LIMIT):
    return pltpu.CompilerParams(dimension_semantics=semantics, vmem_limit_bytes=vmem)


def _silu(x):
    return x * jax.nn.sigmoid(x)


def _rms_mod(x, g, sc, sh):
    y = x * lax.rsqrt(jnp.mean(x * x, axis=-1, keepdims=True) + NORM_EPS)
    return (y * g) * (1.0 + sc) + sh


def _mod_chunk(mod_ref, row, k, d):
    return mod_ref[pl.ds(row, 1), k * d:(k + 1) * d]


def _dot_nt_3pass(a, b):
    dims = (((1,), (1,)), ((), ()))
    a_hi, b_hi = a.astype(BF16), b.astype(BF16)
    a_lo = (a - a_hi.astype(F32)).astype(BF16)
    b_lo = (b - b_hi.astype(F32)).astype(BF16)
    dot = functools.partial(lax.dot_general, dimension_numbers=dims, preferred_element_type=F32)
    return dot(a_hi, b_hi) + dot(a_hi, b_lo) + dot(a_lo, b_hi)


def _round_up_rows(count):
    return jnp.maximum(jnp.ceil(count * (1.0 / ROW_ALIGN)), 1.0) * float(ROW_ALIGN)


def _mod_kernel(cc_ref, w_ref, b_ref, o_ref):
    s = _silu(cc_ref[...])
    o_ref[0] = jnp.dot(s, w_ref[0], preferred_element_type=F32, precision=HIGHEST) + b_ref[0]


def _modulation(cc, ada_w, ada_b):
    depth, d, nd = ada_w.shape
    return pl.pallas_call(
        _mod_kernel,
        grid=(depth, nd // d),
        in_specs=[pl.BlockSpec((MOD_ROWS, d), lambda l, j: (0, 0)),
                  pl.BlockSpec((1, d, d), lambda l, j: (l, 0, j)),
                  pl.BlockSpec((1, 1, d), lambda l, j: (l, 0, j))],
        out_specs=pl.BlockSpec((1, MOD_ROWS, d), lambda l, j: (l, 0, j)),
        out_shape=jax.ShapeDtypeStruct((depth, MOD_ROWS, nd), F32),
        compiler_params=_params(("arbitrary", "arbitrary")),
        name="modulation",
    )(cc, ada_w, ada_b.reshape(depth, 1, nd))


def _with_pos_code(x_ref, er_ref, ec_ref, tile_in_seq):
    t_rows = x_ref.shape[0]
    grid_w, half = ec_ref.shape
    rows = t_rows // grid_w
    r0 = pl.multiple_of(tile_in_seq * rows, rows)
    er = er_ref[pl.ds(r0, rows), :]
    pe = jnp.concatenate([jnp.broadcast_to(er[:, None, :], (rows, grid_w, half)).reshape(t_rows, half),
                          jnp.concatenate([ec_ref[...]] * rows, axis=0)], axis=1)
    return x_ref[...] + pe


def _rglru_in_kernel(x_ref, er_ref, ec_ref, mod_ref, g_ref, w_ref, gate_ref, u_ref, *, tiles_per_seq):
    d = x_ref.shape[1]
    c = u_ref.shape[1]
    row = pl.program_id(0) // tiles_per_seq
    x = _with_pos_code(x_ref, er_ref, ec_ref, pl.program_id(0) % tiles_per_seq)
    hx = _rms_mod(x, g_ref[...], _mod_chunk(mod_ref, row, 1, d), _mod_chunk(mod_ref, row, 0, d))
    z = jnp.dot(hx.astype(BF16), w_ref[...], preferred_element_type=F32)
    gate_ref[...] = jax.nn.gelu(z[:, :c]).astype(BF16)
    u_ref[...] = z[:, c:]


def _rglru_in(x2, er, ec, mod, g, w_in):
    n, d = x2.shape
    c = w_in.shape[1] // 2
    tps = er.shape[0] * ec.shape[0] // MIX_TILE
    row = lambda i: (i, 0)
    return pl.pallas_call(
        functools.partial(_rglru_in_kernel, tiles_per_seq=tps),
        grid=(n // MIX_TILE,),
        in_specs=[pl.BlockSpec((MIX_TILE, d), row),
                  pl.BlockSpec(er.shape, lambda i: (0, 0)),
                  pl.BlockSpec(ec.shape, lambda i: (0, 0)),
                  pl.BlockSpec(mod.shape, lambda i: (0, 0)),
                  pl.BlockSpec((1, d), lambda i: (0, 0)),
                  pl.BlockSpec(w_in.shape, lambda i: (0, 0))],
        out_specs=[pl.BlockSpec((MIX_TILE, c), row),
                   pl.BlockSpec((MIX_TILE, c), row)],
        out_shape=[jax.ShapeDtypeStruct((n, c), BF16),
                   jax.ShapeDtypeStruct((n, c), F32)],
        compiler_params=_params(("arbitrary",)),
        name="rglru_in",
    )(x2, er, ec, mod, g, w_in)


def _ctx_states_kernel(x_ref, mod_ref, g_ref, w_ref, cw_ref, cb_ref, wri_ref, grb_ref, gib_ref, lam_ref, h0_ref,
                       hf_ref, hr_ref, ubuf, a_scr, b_scr, h_scr, *, ctx_row):
    t_rows, d = x_ref.shape
    b = pl.program_id(0)
    hx = _rms_mod(x_ref[...], g_ref[...], _mod_chunk(mod_ref, ctx_row, 1, d),
                  _mod_chunk(mod_ref, ctx_row, 0, d))
    ubuf[...] = jnp.zeros_like(ubuf)
    ubuf[SUBLANES:SUBLANES + t_rows, :] = jnp.dot(hx.astype(BF16), w_ref[...], preferred_element_type=F32)
    uc = cb_ref[...]
    full = ubuf[...]
    for k in range(CONV_WIDTH):
        shift = (CONV_PAD_LEFT - k) % full.shape[0]
        tap = full if shift == 0 else pltpu.roll(full, shift, 0)
        uc = uc + cw_ref[k:k + 1, :] * tap[SUBLANES:SUBLANES + t_rows, :]
    first_tile = b - b
    for k, (reverse, out_ref) in enumerate(((False, hf_ref), (True, hr_ref))):
        one = pl.ds(k, 1)
        h = _scan_tile(uc, wri_ref.at[k], grb_ref.at[one], gib_ref.at[one], lam_ref.at[one], h0_ref,
                       a_scr, b_scr, h_scr, None, b, first_tile, reverse, True)
        out_ref[pl.ds(b, 1), :] = h


def _ctx_states(ctx3, mod, g, w_u, conv_w, conv_b, w_ri, r_b, i_b, lam, ctx_row):
    n_batch, t, d = ctx3.shape
    c = w_u.shape[1]
    const = lambda b: (0, 0)
    h0 = jnp.zeros((n_batch, c), F32)
    state = pl.BlockSpec((n_batch, c), const)
    return pl.pallas_call(
        functools.partial(_ctx_states_kernel, ctx_row=ctx_row),
        grid=(n_batch,),
        in_specs=[pl.BlockSpec((None, t, d), lambda b: (b, 0, 0)),
                  pl.BlockSpec(mod.shape, const),
                  pl.BlockSpec((1, d), const),
                  pl.BlockSpec(w_u.shape, const),
                  pl.BlockSpec(conv_w.shape, const),
                  pl.BlockSpec((1, c), const),
                  pl.BlockSpec(w_ri.shape, lambda b: (0, 0, 0, 0)),
                  pl.BlockSpec((2, c), const), pl.BlockSpec((2, c), const), pl.BlockSpec((2, c), const),
                  state],
        out_specs=[state, state],
        out_shape=[jax.ShapeDtypeStruct((n_batch, c), F32)] * 2,
        scratch_shapes=[pltpu.VMEM((t + 2 * SUBLANES, c), F32), pltpu.VMEM((t, c), F32),
                        pltpu.VMEM((t, c), F32), pltpu.VMEM((1, c), F32)],
        compiler_params=_params(("arbitrary",)),
        name="ctx_states",
    )(ctx3, mod, g, w_u, conv_w, conv_b.reshape(1, c), w_ri, r_b, i_b, lam, h0)


def _log_sigmoid(x):
    return jnp.minimum(x, 0.0) - jnp.log1p(jnp.exp(-jnp.abs(x)))


def _lru_scan_kernel(*refs, n_tiles, reverse, conv_done, reset_first, emit_y):
    if conv_done:
        u_ref, wri_ref, rb_ref, ib_ref, lam_ref, h0_ref, out_ref, a_scr, b_scr, h_scr = refs
    else:
        (u_ref, up_ref, un_ref, cw_ref, cb_ref, wri_ref, rb_ref, ib_ref, lam_ref, h0_ref,
         out_ref, uc_ref, ubuf, a_scr, b_scr, h_scr) = refs
    t_rows, c = u_ref.shape
    hb = c // RNN_HEADS
    b = pl.program_id(0)
    j = pl.program_id(1)
    jj = n_tiles - 1 - j if reverse else j

    if conv_done:
        u = u_ref[...]
    else:
        ubuf[SUBLANES:SUBLANES + t_rows, :] = u_ref[...]
        ubuf[0:SUBLANES, :] = jnp.where(jj == 0, 0.0, up_ref[...])
        ubuf[SUBLANES + t_rows:, :] = jnp.where(jj == n_tiles - 1, 0.0, un_ref[...])
        u = cb_ref[...]
        full = ubuf[...]
        for k in range(CONV_WIDTH):
            shift = (CONV_PAD_LEFT - k) % full.shape[0]
            tap = full if shift == 0 else pltpu.roll(full, shift, 0)
            u = u + cw_ref[k:k + 1, :] * tap[SUBLANES:SUBLANES + t_rows, :]
        uc_ref[...] = u

    h = _scan_tile(u, wri_ref, rb_ref, ib_ref, lam_ref, h0_ref, a_scr, b_scr, h_scr, out_ref if emit_y else None,
                   b, j, reverse, reset_first)
    if not emit_y:
        @pl.when(j == n_tiles - 1)
        def _():
            out_ref[pl.ds(b, 1), :] = h


def _scan_tile(u, wri_ref, rb_ref, ib_ref, lam_ref, h0_ref, a_scr, b_scr, h_scr, y_ref, b, j, reverse, reset_first):
    t_rows, c = u.shape
    hb = c // RNN_HEADS
    log_lam = LRU_C * _log_sigmoid(lam_ref[...])
    rows = lax.broadcasted_iota(jnp.int32, (t_rows, 1), 0)
    first_row = jnp.where(j == 0, t_rows - 1 if reverse else 0, -1)
    for h in range(RNN_HEADS):
        sl = slice(h * hb, (h + 1) * hb)
        uh = u[:, sl]
        z = jnp.dot(uh.astype(BF16), wri_ref[h], preferred_element_type=F32)
        r = jax.nn.sigmoid(z[:, :hb] + rb_ref[:, sl])
        ig = jax.nn.sigmoid(z[:, hb:] + ib_ref[:, sl])
        log_a = r * log_lam[:, sl]
        a = jnp.exp(log_a)
        mult = jnp.sqrt((1.0 - a) * (1.0 + a))
        if reset_first:
            mult = jnp.where(rows == first_row, 1.0, mult)
        a_scr[:, sl] = a
        b_scr[:, sl] = mult * ig * uh

    @pl.when(j == 0)
    def _():
        h_scr[...] = h0_ref[pl.ds(b, 1), :]

    n_groups = t_rows // SUBLANES

    def group(g, h):
        base = pl.multiple_of((n_groups - 1 - g if reverse else g) * SUBLANES, SUBLANES)
        for s in range(SUBLANES):
            r = base + (SUBLANES - 1 - s if reverse else s)
            h = a_scr[pl.ds(r, 1), :] * h + b_scr[pl.ds(r, 1), :]
            if y_ref is not None:
                y_ref[pl.ds(r, 1), :] = h
        return h

    h = lax.fori_loop(0, n_groups, group, h_scr[...])
    h_scr[...] = h
    return h


def _lru_scan(u, conv, w_ri, r_b, i_b, lam, h0, *, reverse, reset_first, emit_y):
    n, c = u.shape
    n_batch = h0.shape[0]
    rows = min(SCAN_TILE, n // n_batch)
    n_tiles = n // n_batch // rows
    sub = rows // SUBLANES
    n_sub = n // SUBLANES
    const = lambda b, j: (0, 0)

    def tile(b, j):
        return b * n_tiles + (n_tiles - 1 - j if reverse else j)

    tile_spec = pl.BlockSpec((rows, c), lambda b, j: (tile(b, j), 0))
    in_specs = [tile_spec]
    args = [u]
    scratch = []
    if conv is not None:
        conv_w, conv_b = conv
        in_specs += [pl.BlockSpec((SUBLANES, c), lambda b, j: (jnp.maximum(tile(b, j) * sub - 1, 0), 0)),
                     pl.BlockSpec((SUBLANES, c), lambda b, j: (jnp.minimum((tile(b, j) + 1) * sub, n_sub - 1), 0)),
                     pl.BlockSpec(conv_w.shape, const),
                     pl.BlockSpec((1, c), const)]
        args += [u, u, conv_w, conv_b.reshape(1, c)]
        scratch = [pltpu.VMEM((rows + 2 * SUBLANES, c), F32)]
    in_specs += [pl.BlockSpec(w_ri.shape, lambda b, j: (0, 0, 0)),
                 pl.BlockSpec((1, c), const), pl.BlockSpec((1, c), const), pl.BlockSpec((1, c), const),
                 pl.BlockSpec(h0.shape, const)]
    args += [w_ri, r_b.reshape(1, c), i_b.reshape(1, c), lam.reshape(1, c), h0]
    if emit_y:
        out_specs = [tile_spec]
        out_shape = [jax.ShapeDtypeStruct((n, c), F32)]
    else:
        out_specs = [pl.BlockSpec(h0.shape, const)]
        out_shape = [jax.ShapeDtypeStruct(h0.shape, F32)]
    if conv is not None:
        out_specs.append(tile_spec)
        out_shape.append(jax.ShapeDtypeStruct((n, c), F32))
    name = ("lru_scan" if emit_y else "lru_ctx") + ("_rev" if reverse else "_fwd")
    outs = pl.pallas_call(
        functools.partial(_lru_scan_kernel, n_tiles=n_tiles, reverse=reverse, conv_done=conv is None,
                          reset_first=reset_first, emit_y=emit_y),
        grid=(n_batch, n_tiles),
        in_specs=in_specs,
        out_specs=out_specs,
        out_shape=out_shape,
        scratch_shapes=scratch + [pltpu.VMEM((rows, c), F32),
                                  pltpu.VMEM((rows, c), F32),
                                  pltpu.VMEM((1, c), F32)],
        compiler_params=_params(("arbitrary", "arbitrary")),
        name=name,
    )(*args)
    return outs if conv is not None else outs[0]


def _route(logits, rb):
    e, t = logits.shape
    neg = -jnp.inf
    scores = jax.nn.sigmoid(logits)
    sel = scores + rb
    iota_g = lax.broadcasted_iota(jnp.int32, (N_GROUPS, t), 0).astype(F32)
    iota_e = lax.broadcasted_iota(jnp.int32, (e, t), 0).astype(F32)

    gs = jnp.full((N_GROUPS, t), neg, F32)
    for g in range(N_GROUPS):
        sg = sel[g * EXPERTS_PER_GROUP:(g + 1) * EXPERTS_PER_GROUP, :]
        m1 = jnp.max(sg, axis=0, keepdims=True)
        i1 = jnp.min(jnp.where(sg == m1, iota_g, float(EXPERTS_PER_GROUP)), axis=0, keepdims=True)
        m2 = jnp.max(jnp.where(iota_g == i1, neg, sg), axis=0, keepdims=True)
        gs = jnp.where(iota_g == float(g), m1 + m2, gs)

    keep = jnp.zeros((N_GROUPS, t), F32)
    for _ in range(TOPK_GROUPS):
        m = jnp.max(gs, axis=0, keepdims=True)
        idx = jnp.min(jnp.where(gs == m, iota_g, float(N_GROUPS)), axis=0, keepdims=True)
        hit = iota_g == idx
        keep = jnp.where(hit, 1.0, keep)
        gs = jnp.where(hit, neg, gs)

    masked = jnp.concatenate(
        [jnp.where(keep[g:g + 1, :] > 0.0, sel[g * EXPERTS_PER_GROUP:(g + 1) * EXPERTS_PER_GROUP, :], neg)
         for g in range(N_GROUPS)], axis=0)

    iota_k = lax.broadcasted_iota(jnp.int32, (TOP_K, t), 0)
    selmask = jnp.zeros((e, t), F32)
    eidx = jnp.zeros((TOP_K, t), F32)
    gw = jnp.zeros((TOP_K, t), F32)
    for k in range(TOP_K):
        m = jnp.max(masked, axis=0, keepdims=True)
        idx = jnp.min(jnp.where(masked == m, iota_e, float(e)), axis=0, keepdims=True)
        hit = iota_e == idx
        gk = jnp.sum(jnp.where(hit, scores, 0.0), axis=0, keepdims=True)
        masked = jnp.where(hit, neg, masked)
        selmask = jnp.where(hit, 1.0, selmask)
        eidx = jnp.where(iota_k == k, idx, eidx)
        gw = jnp.where(iota_k == k, gk, gw)
    gw = gw / jnp.sum(gw, axis=0, keepdims=True) * ROUTED_SCALE
    return eidx, gw, selmask


def _ffn_pre(x1, mod_ref, row, g2_ref, rwt_ref, rb_ref, hx2_ref, hi_ref, lo_ref, hit_ref, lot_ref, gatet_ref,
             tab_ref, tabt_ref):
    d = x1.shape[1]
    hx2 = _rms_mod(x1, g2_ref[...], _mod_chunk(mod_ref, row, 4, d), _mod_chunk(mod_ref, row, 3, d))
    hx2_ref[...] = hx2.astype(BF16)
    logits = _dot_nt_3pass(rwt_ref[...], hx2)
    eidx, gw, selmask = _route(logits, rb_ref[...])
    for s in range(selmask.shape[1] // TOKEN_TILE):
        ts = slice(s * TOKEN_TILE, (s + 1) * TOKEN_TILE)
        hi, lo, gate, tab, tabt = _sorted_layout(selmask[:, ts], eidx[:, ts], gw[:, ts])
        hi_ref[:, ts] = hi.astype(BF16)
        lo_ref[:, ts] = lo.astype(BF16)
        hit_ref[ts, :] = hi.T.astype(BF16)
        lot_ref[ts, :] = lo.T.astype(BF16)
        gatet_ref[ts, :] = gate.T.astype(BF16)
        tab_ref[s] = tab
        tabt_ref[s] = tabt


def _ffn_pre_specs(n, d, tile=lambda i: i):
    row = lambda *g: (tile(*g), 0)
    col = lambda *g: (0, tile(*g))
    sub = MIX_TILE // TOKEN_TILE
    lead = lambda *g: (tile(*g), 0, 0)
    out_specs = [pl.BlockSpec((MIX_TILE, d), row),
                 pl.BlockSpec((MIX_TILE, d), row),
                 pl.BlockSpec((N_EXPERTS, MIX_TILE), col),
                 pl.BlockSpec((N_EXPERTS, MIX_TILE), col),
                 pl.BlockSpec((MIX_TILE, N_EXPERTS), row),
                 pl.BlockSpec((MIX_TILE, N_EXPERTS), row),
                 pl.BlockSpec((MIX_TILE, N_EXPERTS), row),
                 pl.BlockSpec((sub, N_EXPERTS, LANES), lead),
                 pl.BlockSpec((sub, SUBLANES, LANES), lead)]
    out_shape = [jax.ShapeDtypeStruct((n, d), F32),
                 jax.ShapeDtypeStruct((n, d), BF16),
                 jax.ShapeDtypeStruct((N_EXPERTS, n), BF16),
                 jax.ShapeDtypeStruct((N_EXPERTS, n), BF16),
                 jax.ShapeDtypeStruct((n, N_EXPERTS), BF16),
                 jax.ShapeDtypeStruct((n, N_EXPERTS), BF16),
                 jax.ShapeDtypeStruct((n, N_EXPERTS), BF16),
                 jax.ShapeDtypeStruct((n // TOKEN_TILE, N_EXPERTS, LANES), F32),
                 jax.ShapeDtypeStruct((n // TOKEN_TILE, SUBLANES, LANES), F32)]
    return out_specs, out_shape


def _rglru_out_kernel(u_ref, wri_ref, grb_ref, gib_ref, lam_ref, h0_ref, yf_ref, gate_ref, x_ref, er_ref, ec_ref,
                      mod_ref, wout_ref, g2_ref, rwt_ref, rb_ref,
                      x1_ref, hx2_ref, hi_ref, lo_ref, hit_ref, lot_ref, gatet_ref, tab_ref, tabt_ref,
                      a_scr, b_scr, h_scr, yr_scr, *, n_tiles):
    d = x_ref.shape[1]
    row = pl.program_id(0)
    j = pl.program_id(1)
    tile_in_seq = n_tiles - 1 - j
    _scan_tile(u_ref[...], wri_ref, grb_ref, gib_ref, lam_ref, h0_ref, a_scr, b_scr, h_scr, yr_scr,
               row, j, True, False)
    yx = yf_ref[...] + yr_scr[...]
    v = gate_ref[...].astype(F32) * yx
    out = jnp.dot(v.astype(BF16), wout_ref[...], preferred_element_type=F32)
    x1 = _with_pos_code(x_ref, er_ref, ec_ref, tile_in_seq) + _mod_chunk(mod_ref, row, 2, d) * out
    x1_ref[...] = x1
    _ffn_pre(x1, mod_ref, row, g2_ref, rwt_ref, rb_ref, hx2_ref, hi_ref, lo_ref, hit_ref, lot_ref, gatet_ref,
             tab_ref, tabt_ref)


def _rglru_out(u_conv, w_ri, r_b, i_b, lam, h0, y_fwd, gate, x, er, ec, mod, w_out, g2, rwt, rb):
    n, d = x.shape
    c = gate.shape[1]
    n_batch = h0.shape[0]
    n_tiles = n // n_batch // MIX_TILE
    tile = lambda b, j: b * n_tiles + n_tiles - 1 - j
    row = lambda b, j: (tile(b, j), 0)
    const = lambda b, j: (0, 0)
    out_specs, out_shape = _ffn_pre_specs(n, d, tile)
    return pl.pallas_call(
        functools.partial(_rglru_out_kernel, n_tiles=n_tiles),
        grid=(n_batch, n_tiles),
        in_specs=[pl.BlockSpec((MIX_TILE, c), row),
                  pl.BlockSpec(w_ri.shape, lambda b, j: (0, 0, 0)),
                  pl.BlockSpec((1, c), const), pl.BlockSpec((1, c), const), pl.BlockSpec((1, c), const),
                  pl.BlockSpec(h0.shape, const),
                  pl.BlockSpec((MIX_TILE, c), row),
                  pl.BlockSpec((MIX_TILE, c), row),
                  pl.BlockSpec((MIX_TILE, d), row),
                  pl.BlockSpec(er.shape, const),
                  pl.BlockSpec(ec.shape, const),
                  pl.BlockSpec(mod.shape, const),
                  pl.BlockSpec(w_out.shape, const),
                  pl.BlockSpec((1, d), const),
                  pl.BlockSpec(rwt.shape, const),
                  pl.BlockSpec(rb.shape, const)],
        out_specs=out_specs,
        out_shape=out_shape,
        scratch_shapes=[pltpu.VMEM((MIX_TILE, c), F32), pltpu.VMEM((MIX_TILE, c), F32),
                        pltpu.VMEM((1, c), F32), pltpu.VMEM((MIX_TILE, c), F32)],
        compiler_params=_params(("arbitrary", "arbitrary")),
        name="rglru_rev_out",
    )(u_conv, w_ri, r_b.reshape(1, c), i_b.reshape(1, c), lam.reshape(1, c), h0, y_fwd, gate, x, er, ec, mod,
      w_out, g2, rwt, rb)


def _sgu_kernel(x_ref, mod_ref, g_ref, win_ref, lng_ref, lnb_ref, ws_ref, bst_ref, wout_ref,
                g2_ref, rwt_ref, rb_ref,
                x1_ref, hx2_ref, hi_ref, lo_ref, hit_ref, lot_ref, gatet_ref, tab_ref, tabt_ref, m_scr, *,
                tiles_per_seq):
    t_rows, d = x_ref.shape
    w = wout_ref.shape[0]
    gd = w // SGU_HEADS
    row = pl.program_id(0) // tiles_per_seq
    x = x_ref[...]
    hx = _rms_mod(x, g_ref[...], _mod_chunk(mod_ref, row, 1, d), _mod_chunk(mod_ref, row, 0, d))
    z = jax.nn.gelu(jnp.dot(hx.astype(BF16), win_ref[...], preferred_element_type=F32))
    u = z[:, :w]
    v = z[:, w:]
    mu = jnp.mean(v, axis=-1, keepdims=True)
    vc = v - mu
    v = vc * lax.rsqrt(jnp.mean(vc * vc, axis=-1, keepdims=True) + NORM_EPS) * lng_ref[...] + lnb_ref[...]
    vb = v.astype(BF16)
    for ch in range(t_rows // CHUNK):
        rs = slice(ch * CHUNK, (ch + 1) * CHUNK)
        for g in range(SGU_HEADS):
            cs = slice(g * gd, (g + 1) * gd)
            sv = jnp.dot(ws_ref[g], vb[rs, cs], preferred_element_type=F32) + bst_ref[:, g:g + 1]
            m_scr[rs, cs] = (u[rs, cs] * sv).astype(BF16)
    out = jnp.dot(m_scr[...], wout_ref[...], preferred_element_type=F32)
    x1 = x + _mod_chunk(mod_ref, row, 2, d) * out
    x1_ref[...] = x1
    _ffn_pre(x1, mod_ref, row, g2_ref, rwt_ref, rb_ref, hx2_ref, hi_ref, lo_ref, hit_ref, lot_ref, gatet_ref,
             tab_ref, tabt_ref)


def _sgu(x, mod, g, w_in, ln_g, ln_b, w_s, b_st, w_out, g2, rwt, rb, tiles_per_seq):
    n, d = x.shape
    w = w_out.shape[0]
    const = lambda i: (0, 0)
    out_specs, out_shape = _ffn_pre_specs(n, d)
    return pl.pallas_call(
        functools.partial(_sgu_kernel, tiles_per_seq=tiles_per_seq),
        grid=(n // MIX_TILE,),
        in_specs=[pl.BlockSpec((MIX_TILE, d), lambda i: (i, 0)),
                  pl.BlockSpec(mod.shape, const),
                  pl.BlockSpec((1, d), const),
                  pl.BlockSpec(w_in.shape, const, pipeline_mode=pl.Buffered(1)),
                  pl.BlockSpec((1, w), const),
                  pl.BlockSpec((1, w), const),
                  pl.BlockSpec(w_s.shape, lambda i: (0, 0, 0)),
                  pl.BlockSpec(b_st.shape, const),
                  pl.BlockSpec(w_out.shape, const, pipeline_mode=pl.Buffered(1)),
                  pl.BlockSpec((1, d), const),
                  pl.BlockSpec(rwt.shape, const),
                  pl.BlockSpec(rb.shape, const)],
        out_specs=out_specs,
        out_shape=out_shape,
        scratch_shapes=[pltpu.VMEM((MIX_TILE, w), BF16)],
        compiler_params=_params(("arbitrary",)),
        name="sgu",
    )(x, mod, g, w_in, ln_g, ln_b, w_s, b_st, w_out, g2, rwt, rb)


def _sorted_layout(m, eidx, gw):
    e, t = m.shape
    lanes = LANES
    r = lax.broadcasted_iota(jnp.int32, (t, t), 0)
    c = lax.broadcasted_iota(jnp.int32, (t, t), 1)
    upper = jnp.where(r <= c, 1.0, 0.0).astype(BF16)
    incl = jnp.dot(m.astype(BF16), upper, preferred_element_type=F32)
    run = _round_up_rows(incl[:, t - 1:t])
    re = lax.broadcasted_iota(jnp.int32, (e, e), 0)
    ce = lax.broadcasted_iota(jnp.int32, (e, e), 1)
    lower = jnp.where(ce < re, 1.0, 0.0).astype(BF16)
    tiles = jnp.broadcast_to(run * (1.0 / ROW_ALIGN), (e, lanes)).astype(BF16)
    lstart = jnp.dot(lower, tiles, preferred_element_type=F32)[:, 0:1] * float(ROW_ALIGN)
    pos = jnp.where(m > 0.0, lstart + incl - m, float(POS_RADIX * POS_RADIX - 1))
    hi = jnp.floor(pos * (1.0 / POS_RADIX))
    lo = pos - hi * float(POS_RADIX)
    iota_e = lax.broadcasted_iota(jnp.int32, (e, t), 0).astype(F32)
    gate = jnp.zeros((e, t), F32)
    for k in range(TOP_K):
        gate = jnp.where(iota_e == eidx[k:k + 1, :], gw[k:k + 1, :], gate)
    lane = lax.broadcasted_iota(jnp.int32, (e, lanes), 1)
    tab = jnp.where(lane == 0, run, jnp.where(lane == 2, lstart, 0.0))
    diag = lax.broadcasted_iota(jnp.int32, (e, lanes), 0) == lane
    row_of = lambda col: jnp.sum(jnp.where(diag, jnp.broadcast_to(col, (e, lanes)), 0.0), axis=0, keepdims=True)
    sub = lax.broadcasted_iota(jnp.int32, (SUBLANES, lanes), 0)
    tabt = jnp.where(sub == 0, row_of(lstart), jnp.where(sub == 1, row_of(lstart + run), 0.0))
    return hi, lo, gate, tab, tabt


def _aligned(v):
    return pl.multiple_of(v, ROW_ALIGN)


def _run_copies(tab_ref, make_copy, unroll, base=0):
    def body(e, carry):
        make_copy(_aligned(tab_ref[base + TAB_LOCAL + e]), _aligned(tab_ref[base + N_EXPERTS + e]),
                  _aligned(tab_ref[base + e])).start()
        return carry

    lax.fori_loop(0, N_EXPERTS, body, 0, unroll=unroll)


def _dispatch_kernel(tab_ref, ztab_ref, tabt_ref, hi_ref, lo_ref, x_ref, xs_hbm, sbuf, zbuf, sems, zsem):
    i = pl.program_id(0)
    tiles = sbuf.shape[0]
    t = x_ref.shape[0] // tiles
    rows = sbuf.shape[1]

    def zero_copy(e):
        n = _aligned(ztab_ref[e])
        return pltpu.make_async_copy(zbuf.at[pl.ds(0, n)], xs_hbm.at[pl.ds(_aligned(ztab_ref[N_EXPERTS + e]), n)], zsem)

    def for_zero_runs(fn):
        def body(e, c):
            @pl.when(ztab_ref[e] > 0)
            def _():
                fn(zero_copy(e))
            return c
        lax.fori_loop(0, N_EXPERTS, body, 0)

        def tail(b, c):
            fn(pltpu.make_async_copy(zbuf, xs_hbm.at[pl.ds(pl.multiple_of(b * EXPERT_BLOCK, EXPERT_BLOCK),
                                                           EXPERT_BLOCK)], zsem))
            return c
        lax.fori_loop(ztab_ref[2 * N_EXPERTS], xs_hbm.shape[0] // EXPERT_BLOCK, tail, 0)

    @pl.when(i == 0)
    def _():
        zbuf[...] = jnp.zeros_like(zbuf)
        for_zero_runs(lambda cp: cp.start())

    def wait_rows(s, n):
        pltpu.make_async_copy(sbuf.at[s, pl.ds(0, n)], xs_hbm.at[pl.ds(0, n)], sems.at[s]).wait()

    def permute(s, r0):
        ts = slice(s * t, (s + 1) * t)
        j_e = (lax.broadcasted_iota(jnp.int32, (PERM_CHUNK, N_EXPERTS), 0) + r0).astype(F32)
        owner = jnp.where((j_e >= tabt_ref[s, 0:1, 0:N_EXPERTS]) & (j_e < tabt_ref[s, 1:2, 0:N_EXPERTS]),
                          1.0, 0.0).astype(BF16)
        pos = (jnp.dot(owner, hi_ref[:, ts], preferred_element_type=F32) * float(POS_RADIX)
               + jnp.dot(owner, lo_ref[:, ts], preferred_element_type=F32))
        j_t = (lax.broadcasted_iota(jnp.int32, (PERM_CHUNK, t), 0) + r0).astype(F32)
        p = jnp.where(pos == j_t, 1.0, 0.0).astype(BF16)
        sbuf[s, r0:r0 + PERM_CHUNK, :] = jnp.dot(p, x_ref[ts, :], preferred_element_type=F32).astype(BF16)

    for s in range(tiles):
        base = s * TAB_WIDTH

        @pl.when(i > 0)
        def _():
            wait_rows(s, _aligned(tab_ref[base + 2 * N_EXPERTS + 1]))

        for r0 in range(0, rows, PERM_CHUNK):
            if r0 < t * TOP_K + PERM_CHUNK:
                permute(s, r0)
            else:
                pl.when(tab_ref[base + 2 * N_EXPERTS] > r0)(functools.partial(permute, s, r0))

        _run_copies(tab_ref, lambda loc, glob, n: pltpu.make_async_copy(
            sbuf.at[s, pl.ds(loc, n)], xs_hbm.at[pl.ds(glob, n)], sems.at[s]), unroll=True, base=base)

    @pl.when(i == pl.num_programs(0) - 1)
    def _():
        for s in range(tiles):
            wait_rows(s, _aligned(tab_ref[s * TAB_WIDTH + 2 * N_EXPERTS]))

    @pl.when(i == 0)
    def _():
        for_zero_runs(lambda cp: cp.wait())


def _dispatch(tab, ztab, tab_t, pos_hi, pos_lo, hx2, xs_rows):
    n, d = hx2.shape
    col = lambda i: (0, i)
    return pl.pallas_call(
        _dispatch_kernel,
        grid=(n // (DISPATCH_TILES * TOKEN_TILE),),
        in_specs=[pl.BlockSpec((DISPATCH_TILES * TAB_WIDTH,), lambda i: (i,), memory_space=pltpu.SMEM),
                  pl.BlockSpec(memory_space=pltpu.SMEM),
                  pl.BlockSpec((DISPATCH_TILES, SUBLANES, LANES), lambda i: (i, 0, 0)),
                  pl.BlockSpec((N_EXPERTS, DISPATCH_TILES * TOKEN_TILE), col),
                  pl.BlockSpec((N_EXPERTS, DISPATCH_TILES * TOKEN_TILE), col),
                  pl.BlockSpec((DISPATCH_TILES * TOKEN_TILE, d), lambda i: (i, 0))],
        out_specs=pl.BlockSpec(memory_space=pl.ANY),
        out_shape=jax.ShapeDtypeStruct((xs_rows, d), BF16),
        scratch_shapes=[pltpu.VMEM((DISPATCH_TILES, SORTED_ROWS, d), BF16),
                        pltpu.VMEM((EXPERT_BLOCK, d), BF16),
                        pltpu.SemaphoreType.DMA((DISPATCH_TILES,)),
                        pltpu.SemaphoreType.DMA],
        compiler_params=_params(("arbitrary",)),
        name="moe_dispatch",
    )(tab, ztab, tab_t, pos_hi, pos_lo, hx2)


def _experts_kernel(be_ref, nb_ref, xs_ref, wgu_ref, wd_ref, ys_ref, wgu_b, wd_b):
    i = pl.program_id(0)
    ff = wd_b.shape[0]
    used = i < nb_ref[0]
    new_expert = (i == 0) | (be_ref[i] != be_ref[jnp.maximum(i - 1, 0)])

    @pl.when(used & new_expert)
    def _():
        wgu_b[...] = wgu_ref[0, 0].astype(BF16)
        wd_b[...] = wd_ref[0, 0].astype(BF16)

    @pl.when(used)
    def _():
        h = jnp.dot(xs_ref[...], wgu_b[...], preferred_element_type=F32)
        a = _silu(h[:, :ff]) * h[:, ff:]
        ys_ref[...] = jnp.dot(a.astype(BF16), wd_b[...], preferred_element_type=F32).astype(BF16)


def _experts(block_e, n_used, xs, w_gu, w_down, layer):
    rows, d = xs.shape
    nb = rows // EXPERT_BLOCK
    blk = lambda i, be, nu: (jnp.maximum(jnp.minimum(i, nu[0] - 1), 0), 0)
    wmap = lambda i, be, nu: (layer, be[i], 0, 0)
    return pl.pallas_call(
        _experts_kernel,
        grid_spec=pltpu.PrefetchScalarGridSpec(
            num_scalar_prefetch=2,
            grid=(nb,),
            in_specs=[pl.BlockSpec((EXPERT_BLOCK, d), blk),
                      pl.BlockSpec((1, 1) + w_gu.shape[2:], wmap),
                      pl.BlockSpec((1, 1) + w_down.shape[2:], wmap)],
            out_specs=pl.BlockSpec((EXPERT_BLOCK, d), blk),
            scratch_shapes=[pltpu.VMEM(w_gu.shape[2:], BF16), pltpu.VMEM(w_down.shape[2:], BF16)]),
        out_shape=jax.ShapeDtypeStruct((rows, d), BF16),
        input_output_aliases={2: 0},
        compiler_params=_params(("arbitrary",)),
        name="moe_experts",
    )(block_e, n_used, xs, w_gu, w_down)


def _combine_kernel(tab_ref, tabn_ref, tabv_ref, hi_ref, lo_ref, gate_ref, ys_hbm, hx2_ref, x1_ref, mod_ref,
                    wsgu_ref, wsd_ref, fg_ref, out_ref, ybuf, sems, *, tiles_per_seq, final_norm):
    t_rows, d = x1_ref.shape
    ff = wsd_ref.shape[0]
    rows = ybuf.shape[1]
    i = pl.program_id(0)
    last = pl.num_programs(0) - 1
    slot = i % 2
    row = i // tiles_per_seq

    def gather(table, s, unroll):
        _run_copies(table, lambda loc, glob, n: pltpu.make_async_copy(
            ys_hbm.at[pl.ds(glob, n)], ybuf.at[s, pl.ds(loc, n)], sems.at[s]), unroll=unroll)

    def wait_rows(table, s):
        n = _aligned(table[2 * N_EXPERTS])
        pltpu.make_async_copy(ys_hbm.at[pl.ds(0, n)], ybuf.at[s, pl.ds(0, n)], sems.at[s]).wait()

    @pl.when(i == 0)
    def _():
        ybuf[...] = jnp.zeros_like(ybuf)
        gather(tab_ref, 0, False)

    gather(tabn_ref, 1 - slot, True)

    hs = jnp.dot(hx2_ref[...], wsgu_ref[...], preferred_element_type=F32)
    shared = jnp.dot((_silu(hs[:, :ff]) * hs[:, ff:]).astype(BF16), wsd_ref[...], preferred_element_type=F32)

    run_lo = tabv_ref[0, :, 2:3]
    run_hi = run_lo + tabv_ref[0, :, 0:1]
    j_e = lax.broadcasted_iota(jnp.int32, (N_EXPERTS, rows), 1).astype(F32)
    owner = jnp.where((j_e >= run_lo) & (j_e < run_hi), 1.0, 0.0).astype(BF16)
    pos = (jnp.dot(hi_ref[...], owner, preferred_element_type=F32) * float(POS_RADIX)
           + jnp.dot(lo_ref[...], owner, preferred_element_type=F32))
    gates = jnp.dot(gate_ref[...], owner, preferred_element_type=F32)
    j_t = lax.broadcasted_iota(jnp.int32, (t_rows, rows), 1).astype(F32)
    gb = jnp.where(pos == j_t, gates, 0.0).astype(BF16)

    wait_rows(tab_ref, slot)
    routed = jnp.dot(gb, ybuf[slot], preferred_element_type=F32)
    x2 = x1_ref[...] + _mod_chunk(mod_ref, row, 5, d) * (routed + shared)
    if final_norm:
        x2 = x2 * lax.rsqrt(jnp.mean(x2 * x2, axis=-1, keepdims=True) + NORM_EPS) * fg_ref[...]
    out_ref[...] = x2

    @pl.when(i == last)
    def _():
        wait_rows(tabn_ref, 1 - slot)


def _combine(tab, tab_v, pos_hi, pos_lo, gate, ys, hx2, x1, mod, ws_gu, ws_down, fg, tiles_per_seq, final_norm):
    n, d = x1.shape
    row = lambda i: (i, 0)
    const = lambda i: (0, 0)
    return pl.pallas_call(
        functools.partial(_combine_kernel, tiles_per_seq=tiles_per_seq, final_norm=final_norm),
        grid=(n // TOKEN_TILE,),
        in_specs=[pl.BlockSpec((TAB_WIDTH,), lambda i: (i,), memory_space=pltpu.SMEM),
                  pl.BlockSpec((TAB_WIDTH,), lambda i: (jnp.minimum(i + 1, n // TOKEN_TILE - 1),),
                               memory_space=pltpu.SMEM),
                  pl.BlockSpec((1, N_EXPERTS, LANES), lambda i: (i, 0, 0)),
                  pl.BlockSpec((TOKEN_TILE, N_EXPERTS), row),
                  pl.BlockSpec((TOKEN_TILE, N_EXPERTS), row),
                  pl.BlockSpec((TOKEN_TILE, N_EXPERTS), row),
                  pl.BlockSpec(memory_space=pl.ANY),
                  pl.BlockSpec((TOKEN_TILE, d), row),
                  pl.BlockSpec((TOKEN_TILE, d), row),
                  pl.BlockSpec(mod.shape, const),
                  pl.BlockSpec(ws_gu.shape, const),
                  pl.BlockSpec(ws_down.shape, const),
                  pl.BlockSpec((1, d), const)],
        out_specs=pl.BlockSpec((TOKEN_TILE, d), row),
        out_shape=jax.ShapeDtypeStruct((n, d), F32),
        scratch_shapes=[pltpu.VMEM((2, SORTED_ROWS, d), BF16), pltpu.SemaphoreType.DMA((2,))],
        compiler_params=_params(("arbitrary",)),
        name="moe_combine",
    )(tab, tab, tab_v, pos_hi, pos_lo, gate, ys, hx2, x1, mod, ws_gu, ws_down, fg)


def _moe(x1, hx2, pos_hi, pos_lo, pos_hi_t, pos_lo_t, gate_t, tab_f, tab_t, mod, w_gu, w_down, layer, ws_gu,
         ws_down, fg, tiles_per_seq, final_norm):
    n, d = x1.shape
    n_tiles = n // TOKEN_TILE
    run_len = tab_f[:, :, 0].astype(jnp.int32)
    counts = jnp.sum(run_len, axis=0)
    padded = (counts + EXPERT_BLOCK - 1) // EXPERT_BLOCK * EXPERT_BLOCK
    pad_end = jnp.cumsum(padded)
    pad_start = pad_end - padded
    max_rows = n * TOP_K + n_tiles * N_EXPERTS * ROW_ALIGN + N_EXPERTS * (EXPERT_BLOCK - ROW_ALIGN)
    n_blocks = (max_rows + EXPERT_BLOCK - 1) // EXPERT_BLOCK
    n_used = (pad_end[-1:] // EXPERT_BLOCK).astype(jnp.int32)
    block_start = jnp.arange(n_blocks, dtype=jnp.int32) * EXPERT_BLOCK
    block_e = jnp.minimum(jnp.sum((pad_end[None, :] <= block_start[:, None]).astype(jnp.int32), axis=1),
                          N_EXPERTS - 1)

    run_start = pad_start[None, :] + jnp.cumsum(run_len, axis=0) - run_len
    total = jnp.sum(run_len, axis=1, keepdims=True)
    prev_total = jnp.concatenate([jnp.zeros((DISPATCH_TILES, 1), jnp.int32), total[:-DISPATCH_TILES]], axis=0)
    run_local = tab_f[:, :, 2].astype(jnp.int32)
    fill = jnp.zeros((n_tiles, TAB_WIDTH - 3 * N_EXPERTS - 2), jnp.int32)
    tab = jnp.concatenate([run_len, run_start, total, prev_total, run_local, fill], axis=1).reshape(-1)
    ztab = jnp.concatenate([padded - counts, pad_start + counts, n_used])

    xs = _dispatch(tab, ztab, tab_t, pos_hi, pos_lo, hx2, n_blocks * EXPERT_BLOCK)
    ys = _experts(block_e, n_used, xs, w_gu, w_down, layer)
    return _combine(tab, tab_f, pos_hi_t, pos_lo_t, gate_t, ys, hx2, x1, mod, ws_gu, ws_down, fg,
                    tiles_per_seq, final_norm)


def _sincos_tables(rows, d):
    quarter = d // 4
    omega = 1.0 / (POS_BASE ** (jnp.arange(quarter, dtype=F32) / quarter))

    def emb(n):
        p = jnp.arange(n, dtype=F32)[:, None] * omega[None, :]
        return jnp.concatenate([jnp.sin(p), jnp.cos(p)], axis=-1)

    return emb(rows), emb(GRID_W)


def kernel(x, c, ctx, c_ctx, ada_w, ada_b, mix_norm_g, ffn_norm_g, a_w_in, a_conv_w, a_conv_b, a_gate_r_w, a_gate_r_b, a_gate_i_w, a_gate_i_b, a_lambda, a_w_out, b_w_in, b_ln_g, b_ln_b, b_w_s, b_b_s, b_w_out, router_w, router_b, moe_w_gu, moe_w_down, shared_w_gu, shared_w_down, final_norm_g):
    bsz, s, d = x.shape
    ctx_len = ctx.shape[1]
    depth = ada_w.shape[0]
    assert depth == 2 and bsz < MOD_ROWS and s % MIX_TILE == 0 and MIX_TILE % TOKEN_TILE == 0
    assert s % SCAN_TILE == 0 and MIX_TILE % GRID_W == 0
    assert ctx_len % TOKEN_TILE == 0
    n = bsz * s
    tps = s // TOKEN_TILE
    ctx_row = bsz

    cc = jnp.zeros((MOD_ROWS, d), F32).at[:bsz].set(c).at[ctx_row].set(c_ctx)
    mod = _modulation(cc, ada_w, ada_b)
    er, ec = _sincos_tables(s // GRID_W, d)
    rc = a_w_in.shape[2] // 2

    w_in0 = a_w_in[0].astype(BF16)
    g_mix0 = mix_norm_g[0].reshape(1, d)
    x2 = x.reshape(n, d)
    gate, ux = _rglru_in(x2, er, ec, mod[0], g_mix0, w_in0)
    w_ri = jnp.concatenate([a_gate_r_w[0], a_gate_i_w[0]], axis=-1).astype(BF16)
    conv = (a_conv_w[0], a_conv_b[0])
    gates = [(w_ri[k], a_gate_r_b[0, k], a_gate_i_b[0, k], a_lambda[0, k]) for k in range(2)]
    h_fwd, h_rev = _ctx_states(ctx, mod[0], g_mix0, w_in0[:, rc:], *conv, w_ri, a_gate_r_b[0], a_gate_i_b[0],
                               a_lambda[0], ctx_row)
    y_fwd, ux = _lru_scan(ux, conv, *gates[0], h_fwd, reverse=False, reset_first=False, emit_y=True)
    pre = _rglru_out(ux, *gates[1], h_rev, y_fwd, gate, x2, er, ec, mod[0], a_w_out[0].astype(BF16),
                     ffn_norm_g[0].reshape(1, d), router_w[0].T, router_b[0].reshape(N_EXPERTS, 1))
    x1 = _moe(*pre, mod[0], moe_w_gu, moe_w_down, 0, shared_w_gu[0].astype(BF16),
              shared_w_down[0].astype(BF16), final_norm_g.reshape(1, d), tps, False)

    pre = _sgu(x1, mod[1], mix_norm_g[1].reshape(1, d), b_w_in[0].astype(BF16),
               b_ln_g[0].reshape(1, -1), b_ln_b[0].reshape(1, -1), b_w_s[0].astype(BF16), b_b_s[0].T,
               b_w_out[0].astype(BF16), ffn_norm_g[1].reshape(1, d),
               router_w[1].T, router_b[1].reshape(N_EXPERTS, 1), s // MIX_TILE)
    out = _moe(*pre, mod[1], moe_w_gu, moe_w_down, 1, shared_w_gu[1].astype(BF16),
               shared_w_down[1].astype(BF16), final_norm_g.reshape(1, d), tps, True)
    return out.reshape(bsz, s, d)
```
